```python
import jax, jax.numpy as jnp
from jax import lax
import numpy as np

D_MODEL = 1024
BATCH = 4
SEQ = 4096
DEPTH = 2
DEC_BATCH = 128
DEC_SEQ = 8
PAST_LEN = 8192
PAGE_SIZE = 128

A_WIDTH = D_MODEL // 2
A_GROUPS = 4
A_GROUP_DIM = A_WIDTH // A_GROUPS
A_CHUNK = 128
B_HEADS = 8
B_KV_HEADS = 2
B_HEAD_DIM = 64
B_GROUP = B_HEADS // B_KV_HEADS
B_WIDTH = B_HEADS * B_HEAD_DIM
B_KV_WIDTH = B_KV_HEADS * B_HEAD_DIM
WINDOW = 128
C_HEADS = 4
C_HEAD_DIM = 128
C_WIDTH = C_HEADS * C_HEAD_DIM
C_CONV = 4
C_CHUNK = 64
N_BRANCH = 3
EPS = 1e-6

IN_SPLITS = (A_WIDTH, A_WIDTH, A_WIDTH,
             B_WIDTH, B_KV_WIDTH, B_KV_WIDTH, B_WIDTH,
             2 * C_WIDTH, C_WIDTH, C_HEADS, C_HEADS, C_WIDTH, C_WIDTH,
             N_BRANCH * D_MODEL)
IN_WIDTH = sum(IN_SPLITS)
IN_OFFSETS = tuple(int(s) for s in np.cumsum(IN_SPLITS)[:-1])

kernel_name = 'gated_parallel_gmlp_swa_mlstm_decoder_step'


def rmsnorm(x, g):
    xf = x.astype(jnp.float32)
    y = xf * lax.rsqrt(jnp.mean(xf * xf, axis=-1, keepdims=True) + EPS)
    return (y * g.astype(jnp.float32)).astype(x.dtype)


def causal_conv(x, buf, w, b):
    t = x.shape[1]
    xp = jnp.concatenate([buf.astype(x.dtype), x], axis=1)
    y = b
    for j in range(C_CONV):
        y = y + w[j] * xp[:, j:j + t]
    return y, xp[:, -(C_CONV - 1):]


def chunk_gmlp(u, v, vnorm_g, ws, bs):
    bsz, t = u.shape[0], u.shape[1]
    L = min(A_CHUNK, t)
    nc = t // L
    vn = rmsnorm(v, vnorm_g)
    vb = vn.reshape(bsz, nc, L, A_GROUPS, A_GROUP_DIM)
    w = ws[:, :L, :L] * jnp.tril(jnp.ones((L, L), ws.dtype))
    s = jnp.einsum('gts,bnsgc->bntgc', w, vb) + bs[:, :L].T[None, None, :, :, None]
    return u * s.reshape(bsz, t, A_WIDTH), vn


def sink_attention(qb, kb, vb, mask, sinks):
    logits = jnp.einsum('bnqkgd,bnskd->bnkgqs', qb, kb).astype(jnp.float32) * (B_HEAD_DIM ** -0.5)
    logits = jnp.where(mask[None, :, None, None], logits, -jnp.inf)
    snk = sinks.astype(jnp.float32).reshape(B_KV_HEADS, B_GROUP)[None, None, :, :, None]
    mx = jnp.maximum(logits.max(axis=-1), snk)
    p = jnp.exp(logits - mx[..., None])
    den = p.sum(axis=-1) + jnp.exp(snk - mx)
    out = jnp.einsum('bnkgqs,bnskd->bnkgqd', p, vb.astype(jnp.float32)) / den[..., None]
    return out.transpose(0, 1, 4, 2, 3, 5).astype(qb.dtype)


def swa_branch(q, k, v, qn_g, kn_g, sinks, buf_k, buf_v):
    bsz, t = q.shape[0], q.shape[1]
    q = rmsnorm(q.reshape(bsz, t, B_HEADS, B_HEAD_DIM), qn_g).reshape(bsz, t, B_KV_HEADS, B_GROUP, B_HEAD_DIM)
    k = rmsnorm(k.reshape(bsz, t, B_KV_HEADS, B_HEAD_DIM), kn_g)
    v = v.reshape(bsz, t, B_KV_HEADS, B_HEAD_DIM)
    if buf_k is None:
        nb = t // WINDOW
        kp = jnp.concatenate([jnp.zeros_like(k[:, :WINDOW]), k], axis=1)
        vp = jnp.concatenate([jnp.zeros_like(v[:, :WINDOW]), v], axis=1)
        kb = jnp.concatenate([kp[:, :t].reshape(bsz, nb, WINDOW, B_KV_HEADS, B_HEAD_DIM),
                              k.reshape(bsz, nb, WINDOW, B_KV_HEADS, B_HEAD_DIM)], axis=2)
        vb = jnp.concatenate([vp[:, :t].reshape(bsz, nb, WINDOW, B_KV_HEADS, B_HEAD_DIM),
                              v.reshape(bsz, nb, WINDOW, B_KV_HEADS, B_HEAD_DIM)], axis=2)
        qb = q.reshape(bsz, nb, WINDOW, B_KV_HEADS, B_GROUP, B_HEAD_DIM)
        qpos = jnp.arange(t).reshape(nb, WINDOW)
        kpos = (jnp.arange(nb) * WINDOW - WINDOW)[:, None] + jnp.arange(2 * WINDOW)[None]
        valid = (kpos >= 0)[:, None, :]
        new_k, new_v = k[:, t - WINDOW:], v[:, t - WINDOW:]
    else:
        wb = buf_k.shape[1]
        kall = jnp.concatenate([buf_k.astype(k.dtype), k], axis=1)
        vall = jnp.concatenate([buf_v.astype(v.dtype), v], axis=1)
        kb, vb, qb = kall[:, None], vall[:, None], q[:, None]
        qpos = jnp.arange(t)[None]
        kpos = (jnp.arange(wb + t) - wb)[None]
        valid = True
        new_k, new_v = kall[:, -wb:], vall[:, -wb:]
    diff = qpos[:, :, None] - kpos[:, None, :]
    mask = (diff >= 0) & (diff < WINDOW) & valid
    out = sink_attention(qb, kb, vb, mask, sinks).reshape(bsz, t, B_WIDTH)
    return out, new_k, new_v


def mlstm_scan(q, k, v, i_pre, logf, C0, n0, m0, chunk):
    bsz, t, nh, d = q.shape
    nc = t // chunk
    f32 = jnp.float32

    def to_chunks(a):
        a = a.astype(f32).reshape((bsz, nc, chunk) + a.shape[2:])
        return jnp.moveaxis(a, 1, 0)

    causal = jnp.tril(jnp.ones((chunk, chunk), bool))

    def step(carry, xs):
        C, n, m = carry
        qc, kc, vc, ic, fc = xs
        cum = jnp.cumsum(fc, axis=1)
        dmat = cum[:, :, None, :] - cum[:, None, :, :] + ic[:, None, :, :]
        dmat = jnp.where(causal[None, :, :, None], dmat, -jnp.inf)
        m_inter = cum + m[:, None, :]
        m_t = jnp.maximum(m_inter, dmat.max(axis=2))
        a = jnp.exp(dmat - m_t[:, :, None, :]) * jnp.einsum('bthd,bshd->btsh', qc, kc)
        w_inter = jnp.exp(m_inter - m_t)
        num = jnp.einsum('btsh,bshd->bthd', a, vc) + w_inter[..., None] * jnp.einsum('bthd,bhde->bthe', qc, C)
        den = a.sum(axis=2) + w_inter * jnp.einsum('bthd,bhd->bth', qc, n)
        h = num / jnp.maximum(jnp.abs(den), jnp.exp(-m_t))[..., None]
        total = cum[:, -1]
        g = total[:, None] - cum + ic
        m_new = jnp.maximum(total + m, g.max(axis=1))
        wsel = jnp.exp(g - m_new[:, None])
        decay = jnp.exp(total + m - m_new)
        C_new = decay[..., None, None] * C + jnp.einsum('bsh,bshd,bshe->bhde', wsel, kc, vc)
        n_new = decay[..., None] * n + jnp.einsum('bsh,bshd->bhd', wsel, kc)
        return (C_new, n_new, m_new), h

    xs = (to_chunks(q), to_chunks(k), to_chunks(v), to_chunks(i_pre), to_chunks(logf))
    (C1, n1, m1), h = lax.scan(step, (C0.astype(f32), n0.astype(f32), m0.astype(f32)), xs)
    h = jnp.moveaxis(h, 0, 1).reshape(bsz, t, nh, d)
    return h, C1, n1, m1


def mlstm_branch(qk, v, i_pre, f_pre, o_pre, conv_w, conv_b, f_bias, hnorm_g, conv_buf, C0, n0, m0):
    bsz, t = qk.shape[0], qk.shape[1]
    qk, new_buf = causal_conv(qk, conv_buf, conv_w, conv_b)
    qk = jax.nn.silu(qk)
    q, k = jnp.split(qk, 2, axis=-1)
    q = q.reshape(bsz, t, C_HEADS, C_HEAD_DIM)
    k = k.reshape(bsz, t, C_HEADS, C_HEAD_DIM) * (C_HEAD_DIM ** -0.5)
    v = v.reshape(bsz, t, C_HEADS, C_HEAD_DIM)
    logf = jax.nn.log_sigmoid((f_pre + f_bias).astype(jnp.float32))
    h, C1, n1, m1 = mlstm_scan(q, k, v, i_pre, logf, C0, n0, m0, min(C_CHUNK, t))
    h = rmsnorm(h.astype(qk.dtype), hnorm_g.reshape(C_HEADS, C_HEAD_DIM)).reshape(bsz, t, C_WIDTH)
    return h * jax.nn.sigmoid(o_pre), new_buf, C1, n1, m1


def trunk_layer(x, c, lp, past):
    bsz, t = x.shape[0], x.shape[1]
    mod = jax.nn.silu(c) @ lp['ada_w'] + lp['ada_b']
    shift, scale, gate = jnp.split(mod, 3, axis=-1)
    h = rmsnorm(x, lp['norm_g']) * (1.0 + scale[:, None]) + shift[:, None]
    z = h @ lp['w_in'] + lp['b_in']
    a_u, a_v, a_g, b_q, b_k, b_v, b_g, c_qk, c_v, c_i, c_f, c_o, c_g, m_g = jnp.split(z, IN_OFFSETS, axis=-1)
    ya, v_rows = chunk_gmlp(a_u, a_v, lp['gmlp_vnorm_g'], lp['gmlp_ws'], lp['gmlp_bs'])
    ya = ya * jax.nn.silu(a_g)
    if past is None:
        buf_k = None
        buf_v = None
        conv_buf = jnp.zeros((bsz, C_CONV - 1, 2 * C_WIDTH), x.dtype)
        C0 = jnp.zeros((bsz, C_HEADS, C_HEAD_DIM, C_HEAD_DIM), jnp.float32)
        n0 = jnp.zeros((bsz, C_HEADS, C_HEAD_DIM), jnp.float32)
        m0 = jnp.zeros((bsz, C_HEADS), jnp.float32)
    else:
        buf_k, buf_v, conv_buf, C0, n0, m0 = past
    yb, new_k, new_v = swa_branch(b_q, b_k, b_v, lp['swa_qnorm_g'], lp['swa_knorm_g'], lp['swa_sinks'], buf_k, buf_v)
    yb = yb * jax.nn.silu(b_g)
    yc, new_conv, C1, n1, m1 = mlstm_branch(c_qk, c_v, c_i, c_f, c_o, lp['mlstm_conv_w'], lp['mlstm_conv_b'],
                                            lp['mlstm_f_bias'], lp['mlstm_hnorm_g'], conv_buf, C0, n0, m0)
    yc = yc * jax.nn.silu(c_g)
    gates = jax.nn.sigmoid(m_g).reshape(bsz, t, N_BRANCH, D_MODEL)
    merged = (gates[:, :, 0] * (ya @ lp['w_branch_a'])
              + gates[:, :, 1] * (yb @ lp['w_branch_b'])
              + gates[:, :, 2] * (yc @ lp['w_branch_c']))
    x_out = x + gate[:, None] * (merged @ lp['w_out'])
    new_state = (new_k, new_v, new_conv, C1.astype(x.dtype), n1.astype(x.dtype), m1.astype(x.dtype))
    return x_out, new_state, v_rows


def setup_inputs(seed: int = 0) -> dict:
    key = jax.random.key(seed)
    ks = iter(jax.random.split(key, 40))

    def nrm(shape, s):
        return jax.random.normal(next(ks), shape, jnp.float32) * s

    wb = min(WINDOW, PAST_LEN)
    return {
        'x_prompt': nrm((BATCH, SEQ, D_MODEL), 1.0),
        'x_sample': nrm((DEC_BATCH, DEC_SEQ, D_MODEL), 1.0),
        'cache_swa_k': nrm((DEPTH, DEC_BATCH, wb, B_KV_HEADS, B_HEAD_DIM), 1.0),
        'cache_swa_v': nrm((DEPTH, DEC_BATCH, wb, B_KV_HEADS, B_HEAD_DIM), 1.0),
        'state_mlstm_conv': nrm((DEPTH, DEC_BATCH, C_CONV - 1, 2 * C_WIDTH), 1.0),
        'state_mlstm_C': nrm((DEPTH, DEC_BATCH, C_HEADS, C_HEAD_DIM, C_HEAD_DIM), 0.1),
        'state_mlstm_n': nrm((DEPTH, DEC_BATCH, C_HEADS, C_HEAD_DIM), 0.1),
        'state_mlstm_m': nrm((DEPTH, DEC_BATCH, C_HEADS), 1.0),
        'c_prompt': nrm((BATCH, D_MODEL), 1.0),
        'c_sample': nrm((DEC_BATCH, D_MODEL), 1.0),
        'ada_w': nrm((DEPTH, D_MODEL, 3 * D_MODEL), 0.3 * D_MODEL ** -0.5),
        'ada_b': nrm((DEPTH, 3 * D_MODEL), 0.02),
        'norm_g': 1.0 + nrm((DEPTH, D_MODEL), 0.05),
        'w_in': nrm((DEPTH, D_MODEL, IN_WIDTH), D_MODEL ** -0.5),
        'b_in': nrm((DEPTH, IN_WIDTH), 0.02),
        'gmlp_vnorm_g': 1.0 + nrm((DEPTH, A_WIDTH), 0.05),
        'gmlp_ws': nrm((DEPTH, A_GROUPS, A_CHUNK, A_CHUNK), A_CHUNK ** -0.5),
        'gmlp_bs': 1.0 + nrm((DEPTH, A_GROUPS, A_CHUNK), 0.1),
        'swa_qnorm_g': 1.0 + nrm((DEPTH, B_HEAD_DIM), 0.05),
        'swa_knorm_g': 1.0 + nrm((DEPTH, B_HEAD_DIM), 0.05),
        'swa_sinks': nrm((DEPTH, B_HEADS), 0.5),
        'mlstm_conv_w': nrm((DEPTH, C_CONV, 2 * C_WIDTH), C_CONV ** -0.5),
        'mlstm_conv_b': nrm((DEPTH, 2 * C_WIDTH), 0.02),
        'mlstm_f_bias': jnp.linspace(3.0, 6.0, C_HEADS, dtype=jnp.float32)[None] + nrm((DEPTH, C_HEADS), 0.1),
        'mlstm_hnorm_g': 1.0 + nrm((DEPTH, C_WIDTH), 0.05),
        'w_branch_a': nrm((DEPTH, A_WIDTH, D_MODEL), A_WIDTH ** -0.5),
        'w_branch_b': nrm((DEPTH, B_WIDTH, D_MODEL), B_WIDTH ** -0.5),
        'w_branch_c': nrm((DEPTH, C_WIDTH, D_MODEL), C_WIDTH ** -0.5),
        'w_out': nrm((DEPTH, D_MODEL, D_MODEL), D_MODEL ** -0.5),
    }


def reference(x_prompt, x_sample, cache_swa_k, cache_swa_v, state_mlstm_conv, state_mlstm_C, state_mlstm_n,
              state_mlstm_m, c_prompt, c_sample, ada_w, ada_b, norm_g, w_in, b_in, gmlp_vnorm_g, gmlp_ws, gmlp_bs,
              swa_qnorm_g, swa_knorm_g, swa_sinks, mlstm_conv_w, mlstm_conv_b, mlstm_f_bias, mlstm_hnorm_g,
              w_branch_a, w_branch_b, w_branch_c, w_out):
    x_p, x_s = x_prompt, x_sample
    states_p, states_s, vrows_s = [], [], []
    for l in range(DEPTH):
        lp = {
            'ada_w': ada_w[l], 'ada_b': ada_b[l], 'norm_g': norm_g[l], 'w_in': w_in[l], 'b_in': b_in[l],
            'gmlp_vnorm_g': gmlp_vnorm_g[l], 'gmlp_ws': gmlp_ws[l], 'gmlp_bs': gmlp_bs[l],
            'swa_qnorm_g': swa_qnorm_g[l], 'swa_knorm_g': swa_knorm_g[l], 'swa_sinks': swa_sinks[l],
            'mlstm_conv_w': mlstm_conv_w[l], 'mlstm_conv_b': mlstm_conv_b[l], 'mlstm_f_bias': mlstm_f_bias[l],
            'mlstm_hnorm_g': mlstm_hnorm_g[l], 'w_branch_a': w_branch_a[l], 'w_branch_b': w_branch_b[l],
            'w_branch_c': w_branch_c[l], 'w_out': w_out[l],
        }
        x_p, st_p, _ = trunk_layer(x_p, c_prompt, lp, None)
        past = (cache_swa_k[l], cache_swa_v[l], state_mlstm_conv[l], state_mlstm_C[l], state_mlstm_n[l],
                state_mlstm_m[l])
        x_s, st_s, vr = trunk_layer(x_s, c_sample, lp, past)
        states_p.append(st_p)
        states_s.append(st_s)
        vrows_s.append(vr)
    sp = [jnp.stack([st[j] for st in states_p]) for j in range(6)]
    ss = [jnp.stack([st[j] for st in states_s]) for j in range(6)]
    gmlp_v_sample = jnp.stack(vrows_s)
    return (x_p, x_s, sp[0], sp[1], sp[2], sp[3], sp[4], sp[5],
            ss[0], ss[1], ss[2], ss[3], ss[4], ss[5], gmlp_v_sample)
```

```python
import functools

import jax
import jax.numpy as jnp
from jax import lax
from jax.experimental import pallas as pl
from jax.experimental.pallas import tpu as pltpu

F32 = jnp.float32
BF16 = jnp.bfloat16

D_MODEL = 1024
DEPTH = 2
A_WIDTH = 512
A_GROUPS = 4
GROUP_DIM = 128
B_HEADS = 8
B_KV_HEADS = 2
B_HEAD_DIM = 64
B_WIDTH = 512
B_KV_WIDTH = 128
WINDOW = 128
C_HEADS = 4
C_HEAD_DIM = 128
C_WIDTH = 512
C_CONV = 4
EPS = 1e-6
NEG = -1e30

LANES = 128
SUBLANES = 8
VMEM_LIMIT = 56 * 1024 * 1024

ZA_W = 3 * A_WIDTH
ZB_W = 2 * B_WIDTH + 2 * B_KV_WIDTH
ZC_W = 2 * C_WIDTH + 3 * C_WIDTH + LANES
ZCAT_W = ZA_W + ZB_W + ZC_W
Y_W = A_WIDTH + B_WIDTH + C_WIDTH

PROMPT_TILE = 256
MLSTM_CHUNK = 128
SAMPLE_NB = 16
PROJ_TILE = 512


def _sigmoid(x):
    return 1.0 / (1.0 + jnp.exp(-x))


def _silu(x):
    return x * _sigmoid(x)


def _log_sigmoid(x):
    return jnp.minimum(x, 0.0) - jnp.log1p(jnp.exp(-jnp.abs(x)))


def _rms(x):
    return x * lax.rsqrt(jnp.mean(x * x, axis=-1, keepdims=True) + EPS)


def _dot(a, b):
    return jnp.dot(a, b, preferred_element_type=F32)


def _dot_nt(a, b):
    return lax.dot_general(a, b, (((1,), (1,)), ((), ())), preferred_element_type=F32)


def _dot_exact01(m01, x):
    hi = x.astype(BF16)
    r1 = x - hi.astype(F32)
    mid = r1.astype(BF16)
    lo = (r1 - mid.astype(F32)).astype(BF16)
    return _dot(m01, hi) + _dot(m01, mid) + _dot(m01, lo)


def _modulated_norm(x, mod_ref, ng_ref):
    xn = _rms(x) * ng_ref[...]
    shift = mod_ref[:, 0:D_MODEL]
    scale = mod_ref[:, D_MODEL:2 * D_MODEL]
    return (xn * (1.0 + scale) + shift).astype(BF16)


def _head_rms_scale(x2, lane_lo):
    s0 = jnp.sum(jnp.where(lane_lo, x2, 0.0), axis=-1, keepdims=True)
    s1 = jnp.sum(jnp.where(lane_lo, 0.0, x2), axis=-1, keepdims=True)
    r0 = lax.rsqrt(s0 * (1.0 / B_HEAD_DIM) + EPS)
    r1 = lax.rsqrt(s1 * (1.0 / B_HEAD_DIM) + EPS)
    return jnp.where(lane_lo, r0, r1)


def _qk_norm(x, g_row):
    rows, width = x.shape
    lane_lo = lax.broadcasted_iota(jnp.int32, (rows, LANES), 1) < B_HEAD_DIM
    outs = []
    for j in range(width // LANES):
        slab = x[:, j * LANES:(j + 1) * LANES]
        outs.append(slab * _head_rms_scale(slab * slab, lane_lo))
    y = outs[0] if len(outs) == 1 else jnp.concatenate(outs, axis=1)
    return y * g_row


def _ada_kernel(c_ref, w_ref, b_ref, o_ref):
    c = c_ref[...]
    o_ref[...] = _dot(_silu(c).astype(BF16), w_ref[...].astype(BF16)) + b_ref[...]


def _ada_call(c_all, ada_w, ada_b):
    rows = c_all.shape[0]
    return pl.pallas_call(
        _ada_kernel,
        grid=(DEPTH, 3),
        in_specs=[
            pl.BlockSpec((rows, D_MODEL), lambda l, j: (0, 0)),
            pl.BlockSpec((None, D_MODEL, D_MODEL), lambda l, j: (l, 0, j)),
            pl.BlockSpec((None, 1, D_MODEL), lambda l, j: (l, 0, j)),
        ],
        out_specs=pl.BlockSpec((None, rows, D_MODEL), lambda l, j: (l, 0, j)),
        out_shape=jax.ShapeDtypeStruct((DEPTH, rows, 3 * D_MODEL), F32),
        compiler_params=pltpu.CompilerParams(
            dimension_semantics=("arbitrary", "arbitrary"), vmem_limit_bytes=VMEM_LIMIT),
        name="adaln_mod",
    )(c_all, ada_w, ada_b.reshape(DEPTH, 1, 3 * D_MODEL))


def _col_chunks(width, step=512):
    return [(o, min(step, width - o)) for o in range(0, width, step)]


def _inproj_kernel(x_ref, mod_ref, ng_ref, w_ref, b_ref, za_ref, zb_ref, zc_ref):
    h = _modulated_norm(x_ref[...], mod_ref, ng_ref)
    woff = 0
    for o_ref, width in ((za_ref, ZA_W), (zb_ref, ZB_W), (zc_ref, ZC_W)):
        for off, w in _col_chunks(width):
            o_ref[:, off:off + w] = (_dot(h, w_ref[:, woff + off:woff + off + w])
                                     + b_ref[:, woff + off:woff + off + w])
        woff += width


def _mod_spec(tm, tokens_per_batch):
    if tokens_per_batch is None:
        return pl.BlockSpec((tm, 3 * D_MODEL), lambda i: (i, 0))
    tiles_per_batch = tokens_per_batch // tm
    return pl.BlockSpec((None, 1, 3 * D_MODEL), lambda i: (i // tiles_per_batch, 0, 0))


def _inproj_call(x2, mod, ng, wcat, bcat, tokens_per_batch):
    ntok = x2.shape[0]
    tm = PROJ_TILE
    const = lambda i: (0, 0)
    return pl.pallas_call(
        _inproj_kernel,
        grid=(ntok // tm,),
        in_specs=[
            pl.BlockSpec((tm, D_MODEL), lambda i: (i, 0)),
            _mod_spec(tm, tokens_per_batch),
            pl.BlockSpec((1, D_MODEL), const),
            pl.BlockSpec((D_MODEL, ZCAT_W), const, pipeline_mode=pl.Buffered(1)),
            pl.BlockSpec((1, ZCAT_W), const),
        ],
        out_specs=[
            pl.BlockSpec((tm, ZA_W), lambda i: (i, 0)),
            pl.BlockSpec((tm, ZB_W), lambda i: (i, 0)),
            pl.BlockSpec((tm, ZC_W), lambda i: (i, 0)),
        ],
        out_shape=[
            jax.ShapeDtypeStruct((ntok, ZA_W), F32),
            jax.ShapeDtypeStruct((ntok, ZB_W), F32),
            jax.ShapeDtypeStruct((ntok, ZC_W), F32),
        ],
        compiler_params=pltpu.CompilerParams(
            dimension_semantics=("arbitrary",), vmem_limit_bytes=VMEM_LIMIT),
        name="in_projection",
    )(x2, mod, ng, wcat, bcat)


def _outproj_kernel(x_ref, mod_ref, ng_ref, y_ref, wmg_ref, bmg_ref, wa_ref, wb_ref, wc_ref,
                    wo_ref, o_ref):
    x = x_ref[...]
    h = _modulated_norm(x, mod_ref, ng_ref)
    merged = None
    for i, wbr_ref in enumerate((wa_ref, wb_ref, wc_ref)):
        cols = slice(i * D_MODEL, (i + 1) * D_MODEL)
        gate = _sigmoid(_dot(h, wmg_ref[:, cols]) + bmg_ref[:, cols])
        term = gate * _dot(y_ref[:, i * A_WIDTH:(i + 1) * A_WIDTH], wbr_ref[...])
        merged = term if merged is None else merged + term
    ada_gate = mod_ref[:, 2 * D_MODEL:3 * D_MODEL]
    o_ref[...] = x + ada_gate * _dot(merged.astype(BF16), wo_ref[...])


def _outproj_call(x2, mod, ng, y, wmg, bmg, wa, wb, wc, wo, tokens_per_batch):
    ntok = x2.shape[0]
    tm = PROJ_TILE
    const = lambda i: (0, 0)
    once = pl.Buffered(1)
    return pl.pallas_call(
        _outproj_kernel,
        grid=(ntok // tm,),
        in_specs=[
            pl.BlockSpec((tm, D_MODEL), lambda i: (i, 0)),
            _mod_spec(tm, tokens_per_batch),
            pl.BlockSpec((1, D_MODEL), const),
            pl.BlockSpec((tm, Y_W), lambda i: (i, 0)),
            pl.BlockSpec((D_MODEL, 3 * D_MODEL), const, pipeline_mode=once),
            pl.BlockSpec((1, 3 * D_MODEL), const),
            pl.BlockSpec((A_WIDTH, D_MODEL), const, pipeline_mode=once),
            pl.BlockSpec((B_WIDTH, D_MODEL), const, pipeline_mode=once),
            pl.BlockSpec((C_WIDTH, D_MODEL), const, pipeline_mode=once),
            pl.BlockSpec((D_MODEL, D_MODEL), const, pipeline_mode=once),
        ],
        out_specs=pl.BlockSpec((tm, D_MODEL), lambda i: (i, 0)),
        out_shape=jax.ShapeDtypeStruct((ntok, D_MODEL), F32),
        compiler_params=pltpu.CompilerParams(
            dimension_semantics=("arbitrary",), vmem_limit_bytes=VMEM_LIMIT),
        name="out_projection",
    )(x2, mod, ng, y, wmg, bmg, wa, wb, wc, wo)


def _gmlp_gate(za_ref, vg_ref):
    u = za_ref[:, 0:A_WIDTH]
    vn = _rms(za_ref[:, A_WIDTH:2 * A_WIDTH]) * vg_ref[...]
    sg = _silu(za_ref[:, 2 * A_WIDTH:3 * A_WIDTH])
    return u, vn, sg


def _place_q_head(qn, h, rows):
    lane = lax.broadcasted_iota(jnp.int32, (rows, LANES), 1)
    slab = qn[:, (h // 2) * LANES:(h // 2 + 1) * LANES]
    src_hi = h % 2
    dst_hi = h // (B_HEADS // B_KV_HEADS)
    keep = (lane >= B_HEAD_DIM) if src_hi else (lane < B_HEAD_DIM)
    slab = jnp.where(keep, slab, 0.0)
    if src_hi != dst_hi:
        slab = pltpu.roll(slab, B_HEAD_DIM, 1)
    return slab


def _merge_head_pair(o_even, o_odd, h_even, rows):
    lane_lo = lax.broadcasted_iota(jnp.int32, (rows, LANES), 1) < B_HEAD_DIM
    kv_hi = h_even // (B_HEADS // B_KV_HEADS)
    if kv_hi:
        o_even = pltpu.roll(o_even, B_HEAD_DIM, 1)
    else:
        o_odd = pltpu.roll(o_odd, B_HEAD_DIM, 1)
    return jnp.where(lane_lo, o_even, o_odd)


def _conv_taps(xbuf_window, cw_ref, cb_ref):
    y = cb_ref[...]
    for j in range(C_CONV):
        y = y + cw_ref[j:j + 1, :] * xbuf_window(j)
    return y


def _prompt_mix_kernel(sink_ref, za_ref, zb_ref, zc_ref, vg_ref, gw_ref, gbs_ref, qg_ref, kg_ref,
                       cw_ref, cb_ref, fb_ref, hg_ref,
                       y_ref, ko_ref, vo_ref, convo_ref, c_ref, n_ref, m_ref,
                       kprev, vprev, xbuf):
    ts = PROMPT_TILE
    t_idx = pl.program_id(1)

    @pl.when(t_idx == 0)
    def _():
        kprev[...] = jnp.zeros_like(kprev)
        vprev[...] = jnp.zeros_like(vprev)
        xbuf[0:SUBLANES, :] = jnp.zeros((SUBLANES, 2 * C_WIDTH), F32)
        c_ref[...] = jnp.zeros_like(c_ref)
        n_ref[...] = jnp.zeros_like(n_ref)
        m_ref[...] = jnp.zeros_like(m_ref)

    tri = (lax.broadcasted_iota(jnp.int32, (LANES, LANES), 0)
           >= lax.broadcasted_iota(jnp.int32, (LANES, LANES), 1))

    u, vn, sg = _gmlp_gate(za_ref, vg_ref)
    vnb = vn.astype(BF16)
    s_rows = []
    for c in range(ts // WINDOW):
        s_cols = []
        for gi in range(A_GROUPS):
            wt = jnp.where(tri, gw_ref[gi], 0.0).astype(BF16)
            vblk = vnb[c * WINDOW:(c + 1) * WINDOW, gi * GROUP_DIM:(gi + 1) * GROUP_DIM]
            s_cols.append(_dot(wt, vblk) + gbs_ref[:, gi:gi + 1])
        s_rows.append(jnp.concatenate(s_cols, axis=1))
    s = jnp.concatenate(s_rows, axis=0)
    y_ref[:, 0:A_WIDTH] = (u * s * sg).astype(BF16)

    qn = _qk_norm(zb_ref[:, 0:B_WIDTH], qg_ref[...]) * (B_HEAD_DIM ** -0.5)
    kn = _qk_norm(zb_ref[:, B_WIDTH:B_WIDTH + B_KV_WIDTH], kg_ref[...])
    vv = zb_ref[:, B_WIDTH + B_KV_WIDTH:B_WIDTH + 2 * B_KV_WIDTH]
    sgb = _silu(zb_ref[:, B_WIDTH + 2 * B_KV_WIDTH:ZB_W])
    grp = B_HEADS // B_KV_HEADS
    row = lax.broadcasted_iota(jnp.int32, (WINDOW, 2 * WINDOW), 0)
    col = lax.broadcasted_iota(jnp.int32, (WINDOW, 2 * WINDOW), 1)
    band = (col > row) & (col <= row + WINDOW)
    lane_lo2 = lax.broadcasted_iota(jnp.int32, (2 * WINDOW, LANES), 1) < B_HEAD_DIM
    yb_rows = []
    for blk in range(ts // WINDOW):
        rs = slice(blk * WINDOW, (blk + 1) * WINDOW)
        if blk == 0:
            kcat = jnp.concatenate([kprev[...], kn[rs]], axis=0)
            vcat = jnp.concatenate([vprev[...], vv[rs]], axis=0)
            mask = band & ((col >= WINDOW) | (t_idx > 0))
        else:
            ps = slice((blk - 1) * WINDOW, blk * WINDOW)
            kcat = jnp.concatenate([kn[ps], kn[rs]], axis=0)
            vcat = jnp.concatenate([vv[ps], vv[rs]], axis=0)
            mask = band
        krol = pltpu.roll(kcat, B_HEAD_DIM, 1)
        vrol = pltpu.roll(vcat, B_HEAD_DIM, 1)
        head_out = [None] * B_HEADS
        for kh in range(B_KV_HEADS):
            own = lane_lo2 if kh == 0 else jnp.logical_not(lane_lo2)
            kdup = jnp.where(own, kcat, krol).astype(BF16)
            vdup = jnp.where(own, vcat, vrol).astype(BF16)
            qs = jnp.concatenate([_place_q_head(qn[rs], kh * grp + g, WINDOW) for g in range(grp)],
                                 axis=0).astype(BF16)
            logits = _dot_nt(qs, kdup)
            ps_list, den_list = [], []
            for g in range(grp):
                snk = sink_ref[kh * grp + g]
                lg = jnp.where(mask, logits[g * WINDOW:(g + 1) * WINDOW], NEG)
                mx = jnp.maximum(jnp.max(lg, axis=-1, keepdims=True), snk)
                p = jnp.exp(lg - mx)
                den_list.append(jnp.sum(p, axis=-1, keepdims=True) + jnp.exp(snk - mx))
                ps_list.append(p.astype(BF16))
            pv = _dot(jnp.concatenate(ps_list, axis=0), vdup)
            for g in range(grp):
                head_out[kh * grp + g] = pv[g * WINDOW:(g + 1) * WINDOW] / den_list[g]
        yb_rows.append(jnp.concatenate(
            [_merge_head_pair(head_out[2 * j], head_out[2 * j + 1], 2 * j, WINDOW)
             for j in range(B_HEADS // 2)], axis=1))
    yb = jnp.concatenate(yb_rows, axis=0)
    y_ref[:, A_WIDTH:A_WIDTH + B_WIDTH] = (yb * sgb).astype(BF16)
    last = slice(ts - WINDOW, ts)
    kprev[...] = kn[last]
    vprev[...] = vv[last]
    ko_ref[...] = kn[last]
    vo_ref[...] = vv[last]

    xbuf[SUBLANES:SUBLANES + ts, :] = zc_ref[:, 0:2 * C_WIDTH]
    qk = _silu(_conv_taps(
        lambda j: xbuf[SUBLANES - (C_CONV - 1) + j:SUBLANES - (C_CONV - 1) + j + ts, :],
        cw_ref, cb_ref))
    tail = xbuf[ts:ts + SUBLANES, :]
    xbuf[0:SUBLANES, :] = tail
    convo_ref[...] = tail
    qall = qk[:, 0:C_WIDTH].astype(BF16)
    kall = qk[:, C_WIDTH:2 * C_WIDTH] * (C_HEAD_DIM ** -0.5)
    vall = zc_ref[:, 2 * C_WIDTH:3 * C_WIDTH].astype(BF16)
    gate_o = _sigmoid(zc_ref[:, 3 * C_WIDTH:4 * C_WIDTH]) * _silu(zc_ref[:, 4 * C_WIDTH:5 * C_WIDTH])
    ifp = zc_ref[:, 5 * C_WIDTH:5 * C_WIDTH + LANES]
    lf = _log_sigmoid(ifp + fb_ref[...])
    tri_b = jnp.where(tri, 1.0, 0.0).astype(BF16)
    cl = MLSTM_CHUNK
    lane_c = lax.broadcasted_iota(jnp.int32, (cl, LANES), 1)
    lane_1 = lax.broadcasted_iota(jnp.int32, (1, LANES), 1)
    m_row = m_ref[...]
    h_rows = []
    for c in range(ts // cl):
        rs = slice(c * cl, (c + 1) * cl)
        cum_all = _dot_exact01(tri_b, lf[rs])
        st_col = jnp.where(lane_c < C_HEADS, ifp[rs], cum_all)
        st_row = st_col.T
        h_cols = []
        for hd in range(C_HEADS):
            hs = slice(hd * C_HEAD_DIM, (hd + 1) * C_HEAD_DIM)
            i_c = st_col[:, hd:hd + 1]
            cum_c = st_col[:, C_HEADS + hd:C_HEADS + hd + 1]
            i_r = st_row[hd:hd + 1, :]
            cum_r = st_row[C_HEADS + hd:C_HEADS + hd + 1, :]
            m_prev = m_row[:, hd:hd + 1]
            dmat = jnp.where(tri, cum_c - cum_r + i_r, NEG)
            m_inter = cum_c + m_prev
            m_t = jnp.maximum(m_inter, jnp.max(dmat, axis=-1, keepdims=True))
            q_h = qall[rs, hs]
            k_h = kall[rs, hs]
            v_h = vall[rs, hs]
            a = jnp.exp(dmat - m_t) * _dot_nt(q_h, k_h.astype(BF16))
            w_inter = jnp.exp(m_inter - m_t)
            c_prev = c_ref[hd]
            n_prev = n_ref[hd:hd + 1, :]
            num = _dot(a.astype(BF16), v_h) + w_inter * _dot(q_h, c_prev.astype(BF16))
            den = (jnp.sum(a, axis=-1, keepdims=True)
                   + w_inter * jnp.sum(q_h.astype(F32) * n_prev, axis=-1, keepdims=True))
            hh = num / jnp.maximum(jnp.abs(den), jnp.exp(-m_t))
            h_cols.append(_rms(hh))
            total = cum_r[:, cl - 1:cl]
            g_r = total - cum_r + i_r
            g_c = total - cum_c + i_c
            m_new = jnp.maximum(total + m_prev, jnp.max(g_r, axis=-1, keepdims=True))
            kw = jnp.exp(g_c - m_new) * k_h
            decay = jnp.exp(total + m_prev - m_new)
            c_ref[hd] = decay * c_prev + _dot(kw.T.astype(BF16), v_h)
            n_ref[hd:hd + 1, :] = decay * n_prev + jnp.sum(kw, axis=0, keepdims=True)
            m_row = jnp.where(lane_1 == hd, m_new, m_row)
        h_rows.append(jnp.concatenate(h_cols, axis=1))
    m_ref[...] = m_row
    hn = jnp.concatenate(h_rows, axis=0) * hg_ref[...]
    y_ref[:, A_WIDTH + B_WIDTH:Y_W] = (hn * gate_o).astype(BF16)


def _prompt_mix_call(za, zb, zc, lw, batch, seq):
    ts = PROMPT_TILE
    nt = seq // ts
    tok = lambda b, t: (b * nt + t, 0)
    const2 = lambda b, t: (0, 0)
    const3 = lambda b, t: (0, 0, 0)
    per_b3 = lambda b, t: (b, 0, 0)
    return pl.pallas_call(
        _prompt_mix_kernel,
        grid=(batch, nt),
        in_specs=[
            pl.BlockSpec(memory_space=pltpu.SMEM),
            pl.BlockSpec((ts, ZA_W), tok),
            pl.BlockSpec((ts, ZB_W), tok),
            pl.BlockSpec((ts, ZC_W), tok),
            pl.BlockSpec((1, A_WIDTH), const2),
            pl.BlockSpec((A_GROUPS, WINDOW, WINDOW), const3),
            pl.BlockSpec((WINDOW, LANES), const2),
            pl.BlockSpec((1, B_WIDTH), const2),
            pl.BlockSpec((1, B_KV_WIDTH), const2),
            pl.BlockSpec((C_CONV, 2 * C_WIDTH), const2),
            pl.BlockSpec((1, 2 * C_WIDTH), const2),
            pl.BlockSpec((1, LANES), const2),
            pl.BlockSpec((1, C_WIDTH), const2),
        ],
        out_specs=[
            pl.BlockSpec((ts, Y_W), tok),
            pl.BlockSpec((None, WINDOW, B_KV_WIDTH), per_b3),
            pl.BlockSpec((None, WINDOW, B_KV_WIDTH), per_b3),
            pl.BlockSpec((None, SUBLANES, 2 * C_WIDTH), per_b3),
            pl.BlockSpec((None, C_HEADS, C_HEAD_DIM, C_HEAD_DIM), lambda b, t: (b, 0, 0, 0)),
            pl.BlockSpec((None, C_HEADS, C_HEAD_DIM), per_b3),
            pl.BlockSpec((None, 1, LANES), per_b3),
        ],
        out_shape=[
            jax.ShapeDtypeStruct((batch * seq, Y_W), BF16),
            jax.ShapeDtypeStruct((batch, WINDOW, B_KV_WIDTH), F32),
            jax.ShapeDtypeStruct((batch, WINDOW, B_KV_WIDTH), F32),
            jax.ShapeDtypeStruct((batch, SUBLANES, 2 * C_WIDTH), F32),
            jax.ShapeDtypeStruct((batch, C_HEADS, C_HEAD_DIM, C_HEAD_DIM), F32),
            jax.ShapeDtypeStruct((batch, C_HEADS, C_HEAD_DIM), F32),
            jax.ShapeDtypeStruct((batch, 1, LANES), F32),
        ],
        scratch_shapes=[
            pltpu.VMEM((WINDOW, B_KV_WIDTH), F32),
            pltpu.VMEM((WINDOW, B_KV_WIDTH), F32),
            pltpu.VMEM((ts + SUBLANES, 2 * C_WIDTH), F32),
        ],
        compiler_params=pltpu.CompilerParams(
            dimension_semantics=("arbitrary", "arbitrary"), vmem_limit_bytes=VMEM_LIMIT),
        name="prompt_mixer",
    )(lw["sinks"], za, zb, zc, lw["vg"], lw["gws"], lw["gbs_col"], lw["qg"], lw["kg"],
      lw["cw"], lw["cb"], lw["fb"], lw["hg"])


def _sample_mix_kernel(sink_ref, za_ref, zb_ref, zc_ref, kc_ref, vc_ref, cs_ref, c0_ref, n0_ref,
                       m0_ref, vg_ref, gwb_ref, gbs_ref, qg_ref, kg_ref, cw_ref, cb_ref, fb_ref,
                       hg_ref,
                       y_ref, vrow_ref, ko_ref, vo_ref, convo_ref, c1_ref, n1_ref, m1_ref,
                       xbuf):
    nb = SAMPLE_NB
    t = SUBLANES
    rows = nb * t
    tok_r = lax.broadcasted_iota(jnp.int32, (rows, rows), 0)
    tok_c = lax.broadcasted_iota(jnp.int32, (rows, rows), 1)
    same_b = (tok_r // t) == (tok_c // t)
    causal_b = same_b & (tok_c <= tok_r)

    u, vn, sg = _gmlp_gate(za_ref, vg_ref)
    vrow_ref[...] = vn
    vnb = vn.astype(BF16)
    s_cols = []
    for gi in range(A_GROUPS):
        s_cols.append(_dot(gwb_ref[gi], vnb[:, gi * GROUP_DIM:(gi + 1) * GROUP_DIM])
                      + gbs_ref[:, gi:gi + 1])
    y_ref[:, 0:A_WIDTH] = (u * jnp.concatenate(s_cols, axis=1) * sg).astype(BF16)

    qn = _qk_norm(zb_ref[:, 0:B_WIDTH], qg_ref[...]) * (B_HEAD_DIM ** -0.5)
    kn = _qk_norm(zb_ref[:, B_WIDTH:B_WIDTH + B_KV_WIDTH], kg_ref[...])
    vv = zb_ref[:, B_WIDTH + B_KV_WIDTH:B_WIDTH + 2 * B_KV_WIDTH]
    sgb = _silu(zb_ref[:, B_WIDTH + 2 * B_KV_WIDTH:ZB_W])
    kn3 = kn.reshape(nb, t, B_KV_WIDTH)
    vv3 = vv.reshape(nb, t, B_KV_WIDTH)
    kcache = kc_ref[...]
    vcache = vc_ref[...]
    pad = jnp.zeros((nb, WINDOW - t, B_KV_WIDTH), F32)
    kall = jnp.concatenate([kcache, kn3, pad], axis=1).astype(BF16)
    vall = jnp.concatenate([vcache, vv3, pad], axis=1).astype(BF16)
    qp = jnp.concatenate([_place_q_head(qn, h, rows).reshape(nb, t, LANES) for h in range(B_HEADS)],
                         axis=1).astype(BF16)
    logits = lax.dot_general(qp, kall, (((2,), (2,)), ((0,), (0,))), preferred_element_type=F32)
    qrow = lax.broadcasted_iota(jnp.int32, (nb, B_HEADS * t, 2 * WINDOW), 1)
    kcol = lax.broadcasted_iota(jnp.int32, (nb, B_HEADS * t, 2 * WINDOW), 2)
    qt = qrow % t
    valid = ((kcol < WINDOW) & (kcol > qt)) | ((kcol >= WINDOW) & ((kcol - WINDOW) <= qt))
    hrow = lax.broadcasted_iota(jnp.int32, (B_HEADS * t, 1), 0) // t
    snk = jnp.zeros((B_HEADS * t, 1), F32)
    for h in range(B_HEADS):
        snk = jnp.where(hrow == h, sink_ref[h], snk)
    lg = jnp.where(valid, logits, NEG)
    mx = jnp.maximum(jnp.max(lg, axis=-1, keepdims=True), snk[None])
    p = jnp.exp(lg - mx)
    den = jnp.sum(p, axis=-1, keepdims=True) + jnp.exp(snk[None] - mx)
    pv = lax.dot_general(p.astype(BF16), vall, (((2,), (1,)), ((0,), (0,))),
                         preferred_element_type=F32) / den
    head_out = [pv[:, h * t:(h + 1) * t, :].reshape(rows, LANES) for h in range(B_HEADS)]
    yb = jnp.concatenate(
        [_merge_head_pair(head_out[2 * j], head_out[2 * j + 1], 2 * j, rows)
         for j in range(B_HEADS // 2)], axis=1)
    y_ref[:, A_WIDTH:A_WIDTH + B_WIDTH] = (yb * sgb).astype(BF16)
    ko_ref[...] = jnp.concatenate([kcache[:, t:, :], kn3], axis=1)
    vo_ref[...] = jnp.concatenate([vcache[:, t:, :], vv3], axis=1)

    xbuf[:, SUBLANES - (C_CONV - 1):SUBLANES, :] = cs_ref[...]
    xbuf[:, SUBLANES:2 * SUBLANES, :] = zc_ref[:, 0:2 * C_WIDTH].reshape(nb, t, 2 * C_WIDTH)
    y3 = cb_ref[...][None]
    for j in range(C_CONV):
        lo = SUBLANES - (C_CONV - 1) + j
        y3 = y3 + cw_ref[j:j + 1, :][None] * xbuf[:, lo:lo + t, :]
    convo_ref[...] = xbuf[:, 2 * SUBLANES - (C_CONV - 1):2 * SUBLANES, :]
    qk = _silu(y3.reshape(rows, 2 * C_WIDTH))
    qall = qk[:, 0:C_WIDTH].astype(BF16)
    kall_c = qk[:, C_WIDTH:2 * C_WIDTH] * (C_HEAD_DIM ** -0.5)
    vall_c = zc_ref[:, 2 * C_WIDTH:3 * C_WIDTH].astype(BF16)
    gate_o = _sigmoid(zc_ref[:, 3 * C_WIDTH:4 * C_WIDTH]) * _silu(zc_ref[:, 4 * C_WIDTH:5 * C_WIDTH])
    ifp = zc_ref[:, 5 * C_WIDTH:5 * C_WIDTH + LANES]
    lf = _log_sigmoid(ifp + fb_ref[...])
    lane_t = lax.broadcasted_iota(jnp.int32, (rows, LANES), 1)
    cum_all = _dot_exact01(jnp.where(causal_b, 1.0, 0.0).astype(BF16), lf)
    tot_all = _dot_exact01(jnp.where(same_b, 1.0, 0.0).astype(BF16), lf)
    st_col = jnp.where(lane_t < C_HEADS, ifp, cum_all)
    st_row = st_col.T
    tot_row = tot_all.T
    m0 = m0_ref[...]
    same_b_bf = jnp.where(same_b, 1.0, 0.0).astype(BF16)
    batch_of_lane = lax.broadcasted_iota(jnp.int32, (nb, 1, rows), 2) // t
    batch_id = lax.broadcasted_iota(jnp.int32, (nb, 1, rows), 0)
    own_tok = batch_of_lane == batch_id
    h_cols = []
    m_out = jnp.zeros((rows, LANES), F32)
    for hd in range(C_HEADS):
        hs = slice(hd * C_HEAD_DIM, (hd + 1) * C_HEAD_DIM)
        i_c = st_col[:, hd:hd + 1]
        cum_c = st_col[:, C_HEADS + hd:C_HEADS + hd + 1]
        tot_c = tot_all[:, C_HEADS + hd:C_HEADS + hd + 1]
        i_r = st_row[hd:hd + 1, :]
        cum_r = st_row[C_HEADS + hd:C_HEADS + hd + 1, :]
        tot_r = tot_row[C_HEADS + hd:C_HEADS + hd + 1, :]
        m_prev = m0[:, hd:hd + 1]
        dmat = jnp.where(causal_b, cum_c - cum_r + i_r, NEG)
        m_inter = cum_c + m_prev
        m_t = jnp.maximum(m_inter, jnp.max(dmat, axis=-1, keepdims=True))
        q_h = qall[:, hs]
        k_h = kall_c[:, hs]
        v_h = vall_c[:, hs]
        a = jnp.exp(dmat - m_t) * _dot_nt(q_h, k_h.astype(BF16))
        w_inter = jnp.exp(m_inter - m_t)
        c_prev = c0_ref[:, hd]
        n_tok = jnp.broadcast_to(n0_ref[hd][:, None, :], (nb, t, C_HEAD_DIM)).reshape(rows, C_HEAD_DIM)
        inter = lax.dot_general(q_h.reshape(nb, t, C_HEAD_DIM), c_prev.astype(BF16),
                                (((2,), (1,)), ((0,), (0,))), preferred_element_type=F32)
        num = _dot(a.astype(BF16), v_h) + w_inter * inter.reshape(rows, C_HEAD_DIM)
        den = (jnp.sum(a, axis=-1, keepdims=True)
               + w_inter * jnp.sum(q_h.astype(F32) * n_tok, axis=-1, keepdims=True))
        hh = num / jnp.maximum(jnp.abs(den), jnp.exp(-m_t))
        h_cols.append(_rms(hh))
        g_r = tot_r - cum_r + i_r
        g_c = tot_c - cum_c + i_c
        m_new = jnp.maximum(tot_c + m_prev,
                            jnp.max(jnp.where(same_b, g_r, NEG), axis=-1, keepdims=True))
        kw = jnp.exp(g_c - m_new) * k_h
        decay = jnp.exp(tot_c + m_prev - m_new)
        kwt = kw.T
        lhs = jnp.where(own_tok, kwt[None], 0.0).astype(BF16).reshape(nb * C_HEAD_DIM, rows)
        upd = _dot(lhs, v_h).reshape(nb, C_HEAD_DIM, C_HEAD_DIM)
        dec_b = jnp.broadcast_to(decay, (rows, C_HEAD_DIM)).reshape(nb, t, C_HEAD_DIM)[:, 0:1, :]
        c1_ref[:, hd] = dec_b * c_prev + upd
        n1_ref[hd] = decay * n_tok + _dot(same_b_bf, kw.astype(BF16))
        m_out = jnp.where(lane_t == hd, m_new, m_out)
    m1_ref[...] = m_out
    hn = jnp.concatenate(h_cols, axis=1) * hg_ref[...]
    y_ref[:, A_WIDTH + B_WIDTH:Y_W] = (hn * gate_o).astype(BF16)


def _sample_mix_call(za, zb, zc, kc, vc, cs, c0, n0t, m0tok, lw, nbatch):
    nb = SAMPLE_NB
    t = SUBLANES
    rows = nb * t
    tok = lambda i: (i, 0)
    const2 = lambda i: (0, 0)
    const3 = lambda i: (0, 0, 0)
    b3 = lambda i: (i, 0, 0)
    return pl.pallas_call(
        _sample_mix_kernel,
        grid=(nbatch // nb,),
        in_specs=[
            pl.BlockSpec(memory_space=pltpu.SMEM),
            pl.BlockSpec((rows, ZA_W), tok),
            pl.BlockSpec((rows, ZB_W), tok),
            pl.BlockSpec((rows, ZC_W), tok),
            pl.BlockSpec((nb, WINDOW, B_KV_WIDTH), b3),
            pl.BlockSpec((nb, WINDOW, B_KV_WIDTH), b3),
            pl.BlockSpec((nb, C_CONV - 1, 2 * C_WIDTH), b3),
            pl.BlockSpec((nb, C_HEADS, C_HEAD_DIM, C_HEAD_DIM), lambda i: (i, 0, 0, 0)),
            pl.BlockSpec((C_HEADS, nb, C_HEAD_DIM), lambda i: (0, i, 0)),
            pl.BlockSpec((rows, LANES), tok),
            pl.BlockSpec((1, A_WIDTH), const2),
            pl.BlockSpec((A_GROUPS, rows, rows), const3),
            pl.BlockSpec((rows, LANES), const2),
            pl.BlockSpec((1, B_WIDTH), const2),
            pl.BlockSpec((1, B_KV_WIDTH), const2),
            pl.BlockSpec((C_CONV, 2 * C_WIDTH), const2),
            pl.BlockSpec((1, 2 * C_WIDTH), const2),
            pl.BlockSpec((1, LANES), const2),
            pl.BlockSpec((1, C_WIDTH), const2),
        ],
        out_specs=[
            pl.BlockSpec((rows, Y_W), tok),
            pl.BlockSpec((rows, A_WIDTH), tok),
            pl.BlockSpec((nb, WINDOW, B_KV_WIDTH), b3),
            pl.BlockSpec((nb, WINDOW, B_KV_WIDTH), b3),
            pl.BlockSpec((nb, C_CONV - 1, 2 * C_WIDTH), b3),
            pl.BlockSpec((nb, C_HEADS, C_HEAD_DIM, C_HEAD_DIM), lambda i: (i, 0, 0, 0)),
            pl.BlockSpec((C_HEADS, rows, C_HEAD_DIM), lambda i: (0, i, 0)),
            pl.BlockSpec((rows, LANES), tok),
        ],
        out_shape=[
            jax.ShapeDtypeStruct((nbatch * t, Y_W), BF16),
            jax.ShapeDtypeStruct((nbatch * t, A_WIDTH), F32),
            jax.ShapeDtypeStruct((nbatch, WINDOW, B_KV_WIDTH), F32),
            jax.ShapeDtypeStruct((nbatch, WINDOW, B_KV_WIDTH), F32),
            jax.ShapeDtypeStruct((nbatch, C_CONV - 1, 2 * C_WIDTH), F32),
            jax.ShapeDtypeStruct((nbatch, C_HEADS, C_HEAD_DIM, C_HEAD_DIM), F32),
            jax.ShapeDtypeStruct((C_HEADS, nbatch * t, C_HEAD_DIM), F32),
            jax.ShapeDtypeStruct((nbatch * t, LANES), F32),
        ],
        scratch_shapes=[pltpu.VMEM((nb, 2 * SUBLANES, 2 * C_WIDTH), F32)],
        compiler_params=pltpu.CompilerParams(
            dimension_semantics=("arbitrary",), vmem_limit_bytes=VMEM_LIMIT),
        name="sample_mixer",
    )(lw["sinks"], za, zb, zc, kc, vc, cs, c0, n0t, m0tok, lw["vg"], lw["gwb"], lw["gbs_tok"],
      lw["qg"], lw["kg"], lw["cw"], lw["cb"], lw["fb"], lw["hg"])


def _layer_weights(l, w_in, b_in, gmlp_vnorm_g, gmlp_ws, gmlp_bs, swa_qnorm_g, swa_knorm_g,
                   swa_sinks, mlstm_conv_w, mlstm_conv_b, mlstm_f_bias, mlstm_hnorm_g,
                   w_branch_a, w_branch_b, w_branch_c, w_out, norm_g, dec_seq):
    wl, bl = w_in[l], b_in[l]
    o_ci = ZA_W + ZB_W + 3 * C_WIDTH
    o_co = o_ci + 2 * C_HEADS
    o_mg = o_co + 2 * C_WIDTH
    pad_w = LANES - 2 * C_HEADS

    def regroup(a):
        return jnp.concatenate(
            [a[..., :o_ci], a[..., o_co:o_mg], a[..., o_ci:o_co],
             jnp.zeros(a.shape[:-1] + (pad_w,), a.dtype)], axis=-1)

    t = dec_seq
    nb = SAMPLE_NB
    ws_t = gmlp_ws[l][:, :t, :t] * jnp.tril(jnp.ones((t, t), F32))
    eye = jnp.eye(nb, dtype=F32)
    gwb = jnp.einsum("bc,gts->gbtcs", eye, ws_t).reshape(A_GROUPS, nb * t, nb * t).astype(BF16)
    gbs_col = jnp.pad(gmlp_bs[l].T, ((0, 0), (0, LANES - A_GROUPS)))
    gbs_tok = jnp.pad(jnp.tile(gmlp_bs[l][:, :t].T, (nb, 1)), ((0, 0), (0, LANES - A_GROUPS)))
    fb = jnp.pad(mlstm_f_bias[l], (C_HEADS, LANES - 2 * C_HEADS)).reshape(1, LANES)
    return dict(
        ng=norm_g[l].reshape(1, D_MODEL),
        wcat=regroup(wl).astype(BF16), bcat=regroup(bl).reshape(1, ZCAT_W),
        wmg=wl[:, o_mg:].astype(BF16), bmg=bl[o_mg:].reshape(1, 3 * D_MODEL),
        wa=w_branch_a[l].astype(BF16), wb=w_branch_b[l].astype(BF16),
        wc=w_branch_c[l].astype(BF16), wo=w_out[l].astype(BF16),
        vg=gmlp_vnorm_g[l].reshape(1, A_WIDTH), gws=gmlp_ws[l], gwb=gwb,
        gbs_col=gbs_col, gbs_tok=gbs_tok,
        qg=jnp.tile(swa_qnorm_g[l], B_HEADS).reshape(1, B_WIDTH),
        kg=jnp.tile(swa_knorm_g[l], B_KV_HEADS).reshape(1, B_KV_WIDTH),
        sinks=swa_sinks[l],
        cw=mlstm_conv_w[l], cb=mlstm_conv_b[l].reshape(1, 2 * C_WIDTH), fb=fb,
        hg=mlstm_hnorm_g[l].reshape(1, C_WIDTH),
    )


def kernel(x_prompt, x_sample, cache_swa_k, cache_swa_v, state_mlstm_conv, state_mlstm_C, state_mlstm_n, state_mlstm_m, c_prompt, c_sample, ada_w, ada_b, norm_g, w_in, b_in, gmlp_vnorm_g, gmlp_ws, gmlp_bs, swa_qnorm_g, swa_knorm_g, swa_sinks, mlstm_conv_w, mlstm_conv_b, mlstm_f_bias, mlstm_hnorm_g, w_branch_a, w_branch_b, w_branch_c, w_out):
    batch, seq, _ = x_prompt.shape
    nbatch, dec_seq, _ = x_sample.shape
    assert dec_seq == SUBLANES and seq % PROMPT_TILE == 0 and nbatch % SAMPLE_NB == 0
    assert seq % PROJ_TILE == 0 and (nbatch * dec_seq) % PROJ_TILE == 0
    wb_len = cache_swa_k.shape[2]
    assert wb_len == WINDOW

    nc = batch + nbatch
    nc_pad = -(-nc // SUBLANES) * SUBLANES
    c_all = jnp.concatenate([c_prompt, c_sample, jnp.zeros((nc_pad - nc, D_MODEL), F32)], axis=0)
    mod_all = _ada_call(c_all, ada_w, ada_b)

    xp = x_prompt.reshape(batch * seq, D_MODEL)
    xs = x_sample.reshape(nbatch * dec_seq, D_MODEL)
    outs_p = [[] for _ in range(6)]
    outs_s = [[] for _ in range(6)]
    vrows = []
    for l in range(DEPTH):
        lw = _layer_weights(l, w_in, b_in, gmlp_vnorm_g, gmlp_ws, gmlp_bs, swa_qnorm_g,
                            swa_knorm_g, swa_sinks, mlstm_conv_w, mlstm_conv_b, mlstm_f_bias,
                            mlstm_hnorm_g, w_branch_a, w_branch_b, w_branch_c, w_out, norm_g,
                            dec_seq)
        mod_p = mod_all[l, :batch].reshape(batch, 1, 3 * D_MODEL)
        mod_s = jnp.repeat(mod_all[l, batch:nc], dec_seq, axis=0)

        za, zb, zc = _inproj_call(xp, mod_p, lw["ng"], lw["wcat"], lw["bcat"], seq)
        y, ko, vo, convo, c1, n1, m1 = _prompt_mix_call(za, zb, zc, lw, batch, seq)
        xp = _outproj_call(xp, mod_p, lw["ng"], y, lw["wmg"], lw["bmg"], lw["wa"], lw["wb"],
                           lw["wc"], lw["wo"], seq)
        outs_p[0].append(ko.reshape(batch, WINDOW, B_KV_HEADS, B_HEAD_DIM))
        outs_p[1].append(vo.reshape(batch, WINDOW, B_KV_HEADS, B_HEAD_DIM))
        outs_p[2].append(convo[:, SUBLANES - (C_CONV - 1):, :])
        outs_p[3].append(c1)
        outs_p[4].append(n1)
        outs_p[5].append(m1[:, 0, :C_HEADS])

        za, zb, zc = _inproj_call(xs, mod_s, lw["ng"], lw["wcat"], lw["bcat"], None)
        kc = cache_swa_k[l].reshape(nbatch, WINDOW, B_KV_WIDTH)
        vc = cache_swa_v[l].reshape(nbatch, WINDOW, B_KV_WIDTH)
        n0t = jnp.transpose(state_mlstm_n[l], (1, 0, 2))
        m0tok = jnp.pad(jnp.repeat(state_mlstm_m[l], dec_seq, axis=0),
                        ((0, 0), (0, LANES - C_HEADS)))
        y, vrow, ko, vo, convo, c1, n1tok, m1tok = _sample_mix_call(
            za, zb, zc, kc, vc, state_mlstm_conv[l], state_mlstm_C[l], n0t, m0tok, lw, nbatch)
        xs = _outproj_call(xs, mod_s, lw["ng"], y, lw["wmg"], lw["bmg"], lw["wa"], lw["wb"],
                           lw["wc"], lw["wo"], None)
        outs_s[0].append(ko.reshape(nbatch, WINDOW, B_KV_HEADS, B_HEAD_DIM))
        outs_s[1].append(vo.reshape(nbatch, WINDOW, B_KV_HEADS, B_HEAD_DIM))
        outs_s[2].append(convo)
        outs_s[3].append(c1)
        outs_s[4].append(jnp.transpose(n1tok[:, ::dec_seq, :], (1, 0, 2)))
        outs_s[5].append(m1tok[::dec_seq, :C_HEADS])
        vrows.append(vrow.reshape(nbatch, dec_seq, A_WIDTH))

    sp = [jnp.stack(o) for o in outs_p]
    ss = [jnp.stack(o) for o in outs_s]
    return (xp.reshape(batch, seq, D_MODEL), xs.reshape(nbatch, dec_seq, D_MODEL),
            sp[0], sp[1], sp[2], sp[3], sp[4], sp[5],
            ss[0], ss[1], ss[2], ss[3], ss[4], ss[5], jnp.stack(vrows))
```

```python
import numpy as np
import jax
import jax.numpy as jnp
from jax import lax
from jax.experimental import pallas as pl
from jax.experimental.pallas import tpu as pltpu

F32 = jnp.float32
BF16 = jnp.bfloat16

D_MODEL = 1024
DEPTH = 2
A_WIDTH = 512
A_GROUPS = 4
GROUP_DIM = 128
B_HEADS = 8
B_KV_HEADS = 2
B_HEAD_DIM = 64
B_WIDTH = 512
B_KV_WIDTH = 128
WINDOW = 128
C_HEADS = 4
C_HEAD_DIM = 128
C_WIDTH = 512
C_CONV = 4
EPS = 1e-6
NEG = -1e30

LANES = 128
SUBLANES = 8
VMEM_LIMIT = 56 * 1024 * 1024

ZA_W = 3 * A_WIDTH
ZB_W = 2 * B_WIDTH + 2 * B_KV_WIDTH
ZC_W = 2 * C_WIDTH + 3 * C_WIDTH + LANES
ZCAT_W = ZA_W + ZB_W + ZC_W
Y_W = A_WIDTH + B_WIDTH + C_WIDTH

PROMPT_TILE = 256
MLSTM_CHUNK = PROMPT_TILE
SAMPLE_NB = 16
PROJ_TILE = 512


def _sigmoid(x):
    return 0.5 * jnp.tanh(0.5 * x) + 0.5


def _silu(x):
    return x * _sigmoid(x)


def _log_sigmoid(x):
    return jnp.minimum(x, 0.0) - jnp.log1p(jnp.exp(-jnp.abs(x)))


def _rms(x):
    return x * lax.rsqrt(jnp.mean(x * x, axis=-1, keepdims=True) + EPS)


def _dot(a, b):
    return jnp.dot(a, b, preferred_element_type=F32)


def _dot_nt(a, b):
    return lax.dot_general(a, b, (((1,), (1,)), ((), ())), preferred_element_type=F32)


def _dot_exact01(m01, x):
    hi = x.astype(BF16)
    r1 = x - hi.astype(F32)
    mid = r1.astype(BF16)
    lo = (r1 - mid.astype(F32)).astype(BF16)
    return _dot(m01, hi) + _dot(m01, mid) + _dot(m01, lo)


def _modulated_norm(x, mod_ref, ng_ref):
    xn = _rms(x) * ng_ref[...]
    shift = mod_ref[:, 0:D_MODEL]
    scale = mod_ref[:, D_MODEL:2 * D_MODEL]
    return (xn * (1.0 + scale) + shift).astype(BF16)


def _head_rms_scale(x2, lane_lo):
    s0 = jnp.sum(jnp.where(lane_lo, x2, 0.0), axis=-1, keepdims=True)
    s1 = jnp.sum(jnp.where(lane_lo, 0.0, x2), axis=-1, keepdims=True)
    r0 = lax.rsqrt(s0 * (1.0 / B_HEAD_DIM) + EPS)
    r1 = lax.rsqrt(s1 * (1.0 / B_HEAD_DIM) + EPS)
    return jnp.where(lane_lo, r0, r1)


def _qk_norm(x, g_row):
    rows, width = x.shape
    lane_lo = lax.broadcasted_iota(jnp.int32, (rows, LANES), 1) < B_HEAD_DIM
    outs = []
    for j in range(width // LANES):
        slab = x[:, j * LANES:(j + 1) * LANES]
        outs.append(slab * _head_rms_scale(slab * slab, lane_lo))
    y = outs[0] if len(outs) == 1 else jnp.concatenate(outs, axis=1)
    return y * g_row


def _ada_kernel(c_ref, w_ref, b_ref, o_ref):
    c = c_ref[...]
    o_ref[...] = _dot(_silu(c).astype(BF16), w_ref[...].astype(BF16)) + b_ref[...]


def _ada_call(c_all, ada_w, ada_b):
    rows = c_all.shape[0]
    return pl.pallas_call(
        _ada_kernel,
        grid=(DEPTH, 3),
        in_specs=[
            pl.BlockSpec((rows, D_MODEL), lambda l, j: (0, 0)),
            pl.BlockSpec((None, D_MODEL, D_MODEL), lambda l, j: (l, 0, j)),
            pl.BlockSpec((None, 1, D_MODEL), lambda l, j: (l, 0, j)),
        ],
        out_specs=pl.BlockSpec((None, rows, D_MODEL), lambda l, j: (l, 0, j)),
        out_shape=jax.ShapeDtypeStruct((DEPTH, rows, 3 * D_MODEL), F32),
        compiler_params=pltpu.CompilerParams(
            dimension_semantics=("arbitrary", "arbitrary"), vmem_limit_bytes=VMEM_LIMIT),
        name="adaln_mod",
    )(c_all, ada_w, ada_b.reshape(DEPTH, 1, 3 * D_MODEL))


def _col_chunks(width, step=512):
    return [(o, min(step, width - o)) for o in range(0, width, step)]


def _inproj_kernel(x_ref, mod_ref, ng_ref, w_ref, b_ref, za_ref, zb_ref, zc_ref):
    h = _modulated_norm(x_ref[...], mod_ref, ng_ref)
    woff = 0
    for o_ref, width in ((za_ref, ZA_W), (zb_ref, ZB_W), (zc_ref, ZC_W)):
        for off, w in _col_chunks(width):
            o_ref[:, off:off + w] = (_dot(h, w_ref[:, woff + off:woff + off + w])
                                     + b_ref[:, woff + off:woff + off + w])
        woff += width


def _mod_spec(tm, tokens_per_batch):
    if tokens_per_batch is None:
        return pl.BlockSpec((tm, 3 * D_MODEL), lambda i: (i, 0))
    tiles_per_batch = tokens_per_batch // tm
    return pl.BlockSpec((None, 1, 3 * D_MODEL), lambda i: (i // tiles_per_batch, 0, 0))


def _inproj_call(x2, mod, ng, wcat, bcat, tokens_per_batch):
    ntok = x2.shape[0]
    tm = PROJ_TILE
    const = lambda i: (0, 0)
    return pl.pallas_call(
        _inproj_kernel,
        grid=(ntok // tm,),
        in_specs=[
            pl.BlockSpec((tm, D_MODEL), lambda i: (i, 0)),
            _mod_spec(tm, tokens_per_batch),
            pl.BlockSpec((1, D_MODEL), const),
            pl.BlockSpec((D_MODEL, ZCAT_W), const, pipeline_mode=pl.Buffered(1)),
            pl.BlockSpec((1, ZCAT_W), const),
        ],
        out_specs=[
            pl.BlockSpec((tm, ZA_W), lambda i: (i, 0)),
            pl.BlockSpec((tm, ZB_W), lambda i: (i, 0)),
            pl.BlockSpec((tm, ZC_W), lambda i: (i, 0)),
        ],
        out_shape=[
            jax.ShapeDtypeStruct((ntok, ZA_W), F32),
            jax.ShapeDtypeStruct((ntok, ZB_W), F32),
            jax.ShapeDtypeStruct((ntok, ZC_W), F32),
        ],
        compiler_params=pltpu.CompilerParams(
            dimension_semantics=("arbitrary",), vmem_limit_bytes=VMEM_LIMIT),
        name="in_projection",
    )(x2, mod, ng, wcat, bcat)


def _outproj_kernel(x_ref, mod_ref, ng_ref, y_ref, wmg_ref, bmg_ref, wa_ref, wb_ref, wc_ref,
                    wo_ref, o_ref):
    x = x_ref[...]
    h = _modulated_norm(x, mod_ref, ng_ref)
    merged = None
    for i, wbr_ref in enumerate((wa_ref, wb_ref, wc_ref)):
        cols = slice(i * D_MODEL, (i + 1) * D_MODEL)
        gate = _sigmoid(_dot(h, wmg_ref[:, cols]) + bmg_ref[:, cols])
        term = gate * _dot(y_ref[:, i * A_WIDTH:(i + 1) * A_WIDTH], wbr_ref[...])
        merged = term if merged is None else merged + term
    ada_gate = mod_ref[:, 2 * D_MODEL:3 * D_MODEL]
    o_ref[...] = x + ada_gate * _dot(merged.astype(BF16), wo_ref[...])


def _outproj_call(x2, mod, ng, y, wmg, bmg, wa, wb, wc, wo, tokens_per_batch):
    ntok = x2.shape[0]
    tm = PROJ_TILE
    const = lambda i: (0, 0)
    once = pl.Buffered(1)
    return pl.pallas_call(
        _outproj_kernel,
        grid=(ntok // tm,),
        in_specs=[
            pl.BlockSpec((tm, D_MODEL), lambda i: (i, 0)),
            _mod_spec(tm, tokens_per_batch),
            pl.BlockSpec((1, D_MODEL), const),
            pl.BlockSpec((tm, Y_W), lambda i: (i, 0)),
            pl.BlockSpec((D_MODEL, 3 * D_MODEL), const, pipeline_mode=once),
            pl.BlockSpec((1, 3 * D_MODEL), const),
            pl.BlockSpec((A_WIDTH, D_MODEL), const, pipeline_mode=once),
            pl.BlockSpec((B_WIDTH, D_MODEL), const, pipeline_mode=once),
            pl.BlockSpec((C_WIDTH, D_MODEL), const, pipeline_mode=once),
            pl.BlockSpec((D_MODEL, D_MODEL), const, pipeline_mode=once),
        ],
        out_specs=pl.BlockSpec((tm, D_MODEL), lambda i: (i, 0)),
        out_shape=jax.ShapeDtypeStruct((ntok, D_MODEL), F32),
        compiler_params=pltpu.CompilerParams(
            dimension_semantics=("arbitrary",), vmem_limit_bytes=VMEM_LIMIT),
        name="out_projection",
    )(x2, mod, ng, y, wmg, bmg, wa, wb, wc, wo)


def _gmlp_gate(za_ref, vg_ref):
    u = za_ref[:, 0:A_WIDTH]
    vn = _rms(za_ref[:, A_WIDTH:2 * A_WIDTH]) * vg_ref[...]
    sg = _silu(za_ref[:, 2 * A_WIDTH:3 * A_WIDTH])
    return u, vn, sg


def _place_q_head(qn, h, rows):
    lane = lax.broadcasted_iota(jnp.int32, (rows, LANES), 1)
    slab = qn[:, (h // 2) * LANES:(h // 2 + 1) * LANES]
    src_hi = h % 2
    dst_hi = h // (B_HEADS // B_KV_HEADS)
    keep = (lane >= B_HEAD_DIM) if src_hi else (lane < B_HEAD_DIM)
    slab = jnp.where(keep, slab, 0.0)
    if src_hi != dst_hi:
        slab = pltpu.roll(slab, B_HEAD_DIM, 1)
    return slab


def _merge_head_pair(o_even, o_odd, h_even, rows):
    lane_lo = lax.broadcasted_iota(jnp.int32, (rows, LANES), 1) < B_HEAD_DIM
    kv_hi = h_even // (B_HEADS // B_KV_HEADS)
    if kv_hi:
        o_even = pltpu.roll(o_even, B_HEAD_DIM, 1)
    else:
        o_odd = pltpu.roll(o_odd, B_HEAD_DIM, 1)
    return jnp.where(lane_lo, o_even, o_odd)


def _conv_taps(xbuf_window, cw_ref, cb_ref):
    y = cb_ref[...]
    for j in range(C_CONV):
        y = y + cw_ref[j:j + 1, :] * xbuf_window(j)
    return y


def _prompt_mix_kernel(sink_ref, za_ref, zb_ref, zc_ref, vg_ref, gw_ref, gbs_ref, qg_ref, kg_ref,
                       cw_ref, cb_ref, fb_ref, hg_ref, tril_ref, band_ref, tri01_ref, tribias_ref,
                       y_ref, ko_ref, vo_ref, convo_ref, c_ref, n_ref, m_ref,
                       kprev, vprev, xbuf):
    ts = PROMPT_TILE
    t_idx = pl.program_id(1)

    @pl.when(t_idx == 0)
    def _():
        kprev[...] = jnp.zeros_like(kprev)
        vprev[...] = jnp.zeros_like(vprev)
        xbuf[0:SUBLANES, :] = jnp.zeros((SUBLANES, 2 * C_WIDTH), F32)
        c_ref[...] = jnp.zeros_like(c_ref)
        n_ref[...] = jnp.zeros_like(n_ref)
        m_ref[...] = jnp.zeros_like(m_ref)

    u, vn, sg = _gmlp_gate(za_ref, vg_ref)
    vnb = vn.astype(BF16)
    wts = [(gw_ref[gi] * tril_ref[...]).astype(BF16) for gi in range(A_GROUPS)]
    s_rows = []
    for c in range(ts // WINDOW):
        s_cols = []
        for gi in range(A_GROUPS):
            vblk = vnb[c * WINDOW:(c + 1) * WINDOW, gi * GROUP_DIM:(gi + 1) * GROUP_DIM]
            s_cols.append(_dot(wts[gi], vblk) + gbs_ref[:, gi:gi + 1])
        s_rows.append(jnp.concatenate(s_cols, axis=1))
    s = jnp.concatenate(s_rows, axis=0)
    y_ref[:, 0:A_WIDTH] = (u * s * sg).astype(BF16)

    qn = _qk_norm(zb_ref[:, 0:B_WIDTH], qg_ref[...]) * (B_HEAD_DIM ** -0.5)
    kn = _qk_norm(zb_ref[:, B_WIDTH:B_WIDTH + B_KV_WIDTH], kg_ref[...])
    vv = zb_ref[:, B_WIDTH + B_KV_WIDTH:B_WIDTH + 2 * B_KV_WIDTH]
    sgb = _silu(zb_ref[:, B_WIDTH + 2 * B_KV_WIDTH:ZB_W])
    grp = B_HEADS // B_KV_HEADS
    nblk = ts // WINDOW
    lane_lo2 = lax.broadcasted_iota(jnp.int32, (2 * WINDOW, LANES), 1) < B_HEAD_DIM
    kblocks = [kprev[...]] + [kn[b * WINDOW:(b + 1) * WINDOW] for b in range(nblk)]
    vblocks = [vprev[...]] + [vv[b * WINDOW:(b + 1) * WINDOW] for b in range(nblk)]
    bias = [jnp.where(t_idx > 0, band_ref[0], band_ref[1])] + [band_ref[0]] * (nblk - 1)
    combos = [(blk, kh) for blk in range(nblk) for kh in range(B_KV_HEADS)]
    heads = [(blk, kh, g) for blk, kh in combos for g in range(grp)]
    kdup, vdup = {}, {}
    for blk in range(nblk):
        kcat = jnp.concatenate([kblocks[blk], kblocks[blk + 1]], axis=0)
        vcat = jnp.concatenate([vblocks[blk], vblocks[blk + 1]], axis=0)
        krol = pltpu.roll(kcat, B_HEAD_DIM, 1)
        vrol = pltpu.roll(vcat, B_HEAD_DIM, 1)
        for kh in range(B_KV_HEADS):
            own = lane_lo2 if kh == 0 else jnp.logical_not(lane_lo2)
            kdup[blk, kh] = jnp.where(own, kcat, krol).astype(BF16)
            vdup[blk, kh] = jnp.where(own, vcat, vrol).astype(BF16)
    qs = {(blk, kh): jnp.concatenate(
        [_place_q_head(qn[blk * WINDOW:(blk + 1) * WINDOW], kh * grp + g, WINDOW) for g in range(grp)],
        axis=0).astype(BF16) for blk, kh in combos}
    logits = {c: _dot_nt(qs[c], kdup[c]) for c in combos}
    snk = {k: sink_ref[k[1] * grp + k[2]] for k in heads}
    lg = {(blk, kh, g): logits[blk, kh][g * WINDOW:(g + 1) * WINDOW] + bias[blk]
          for blk, kh, g in heads}
    mx = {k: jnp.maximum(jnp.max(lg[k], axis=-1, keepdims=True), snk[k]) for k in heads}
    p = {k: jnp.exp(lg[k] - mx[k]) for k in heads}
    rden = {k: 1.0 / (jnp.sum(p[k], axis=-1, keepdims=True) + jnp.exp(snk[k] - mx[k])) for k in heads}
    pv = {c: _dot(jnp.concatenate([p[c + (g,)].astype(BF16) for g in range(grp)], axis=0), vdup[c])
          for c in combos}
    outs = {(blk, kh, g): pv[blk, kh][g * WINDOW:(g + 1) * WINDOW] * rden[blk, kh, g]
            for blk, kh, g in heads}
    yb = jnp.concatenate([jnp.concatenate(
        [_merge_head_pair(outs[blk, (2 * j) // grp, (2 * j) % grp],
                          outs[blk, (2 * j + 1) // grp, (2 * j + 1) % grp], 2 * j, WINDOW)
         for j in range(B_HEADS // 2)], axis=1) for blk in range(nblk)], axis=0)
    y_ref[:, A_WIDTH:A_WIDTH + B_WIDTH] = (yb * sgb).astype(BF16)
    kprev[...] = kblocks[nblk]
    vprev[...] = vblocks[nblk]
    ko_ref[...] = kblocks[nblk]
    vo_ref[...] = vblocks[nblk]

    xbuf[SUBLANES:SUBLANES + ts, :] = zc_ref[:, 0:2 * C_WIDTH]
    qk = _silu(_conv_taps(
        lambda j: xbuf[SUBLANES - (C_CONV - 1) + j:SUBLANES - (C_CONV - 1) + j + ts, :],
        cw_ref, cb_ref))
    tail = xbuf[ts:ts + SUBLANES, :]
    xbuf[0:SUBLANES, :] = tail
    convo_ref[...] = tail
    qall = qk[:, 0:C_WIDTH].astype(BF16)
    kall = qk[:, C_WIDTH:2 * C_WIDTH] * (C_HEAD_DIM ** -0.5)
    vall = zc_ref[:, 2 * C_WIDTH:3 * C_WIDTH].astype(BF16)
    gate_o = _sigmoid(zc_ref[:, 3 * C_WIDTH:4 * C_WIDTH]) * _silu(zc_ref[:, 4 * C_WIDTH:5 * C_WIDTH])
    ifp = zc_ref[:, 5 * C_WIDTH:5 * C_WIDTH + LANES]
    lf = _log_sigmoid(ifp + fb_ref[...])
    cl = MLSTM_CHUNK
    hds = range(C_HEADS)
    lane_c = lax.broadcasted_iota(jnp.int32, (cl, LANES), 1)
    lane_1 = lax.broadcasted_iota(jnp.int32, (1, LANES), 1)
    m_row = m_ref[...]
    cum_all = _dot_exact01(tri01_ref[...], lf)
    st_col = jnp.where(lane_c < C_HEADS, ifp, cum_all)
    st_row = st_col.T
    hs = [slice(hd * C_HEAD_DIM, (hd + 1) * C_HEAD_DIM) for hd in hds]
    i_c = [st_col[:, hd:hd + 1] for hd in hds]
    cum_c = [st_col[:, C_HEADS + hd:C_HEADS + hd + 1] for hd in hds]
    i_r = [st_row[hd:hd + 1, :] for hd in hds]
    cum_r = [st_row[C_HEADS + hd:C_HEADS + hd + 1, :] for hd in hds]
    m_prev = [m_row[:, hd:hd + 1] for hd in hds]
    tribias = tribias_ref[...]
    dmat = [cum_c[hd] - cum_r[hd] + i_r[hd] + tribias for hd in hds]
    m_inter = [cum_c[hd] + m_prev[hd] for hd in hds]
    m_t = [jnp.maximum(m_inter[hd], jnp.max(dmat[hd], axis=-1, keepdims=True)) for hd in hds]
    q_h = [qall[:, hs[hd]] for hd in hds]
    k_h = [kall[:, hs[hd]] for hd in hds]
    v_h = [vall[:, hs[hd]] for hd in hds]
    s_qk = [_dot_nt(q_h[hd], k_h[hd].astype(BF16)) for hd in hds]
    a = [jnp.exp(dmat[hd] - m_t[hd]) * s_qk[hd] for hd in hds]
    w_inter = [jnp.exp(m_inter[hd] - m_t[hd]) for hd in hds]
    c_prev = [c_ref[hd] for hd in hds]
    n_prev = [n_ref[hd:hd + 1, :] for hd in hds]
    inter = [_dot(q_h[hd], c_prev[hd].astype(BF16)) for hd in hds]
    intra = [_dot(a[hd].astype(BF16), v_h[hd]) for hd in hds]
    den = [jnp.sum(a[hd], axis=-1, keepdims=True)
           + w_inter[hd] * jnp.sum(q_h[hd].astype(F32) * n_prev[hd], axis=-1, keepdims=True)
           for hd in hds]
    rnorm = [1.0 / jnp.maximum(jnp.abs(den[hd]), jnp.exp(-m_t[hd])) for hd in hds]
    hh = [(intra[hd] + w_inter[hd] * inter[hd]) * rnorm[hd] for hd in hds]
    hn = jnp.concatenate([_rms(hh[hd]) for hd in hds], axis=1) * hg_ref[...]
    y_ref[:, A_WIDTH + B_WIDTH:Y_W] = (hn * gate_o).astype(BF16)
    total = [cum_r[hd][:, cl - 1:cl] for hd in hds]
    g_r = [total[hd] - cum_r[hd] + i_r[hd] for hd in hds]
    g_c = [total[hd] - cum_c[hd] + i_c[hd] for hd in hds]
    m_new = [jnp.maximum(total[hd] + m_prev[hd], jnp.max(g_r[hd], axis=-1, keepdims=True))
             for hd in hds]
    kw = [jnp.exp(g_c[hd] - m_new[hd]) * k_h[hd] for hd in hds]
    decay = [jnp.exp(total[hd] + m_prev[hd] - m_new[hd]) for hd in hds]
    upd = [_dot(kw[hd].T.astype(BF16), v_h[hd]) for hd in hds]
    for hd in hds:
        c_ref[hd] = decay[hd] * c_prev[hd] + upd[hd]
        n_ref[hd:hd + 1, :] = decay[hd] * n_prev[hd] + jnp.sum(kw[hd], axis=0, keepdims=True)
        m_row = jnp.where(lane_1 == hd, m_new[hd], m_row)
    m_ref[...] = m_row


def _prompt_mask_constants():
    r = np.arange(WINDOW)[:, None]
    c = np.arange(2 * WINDOW)[None, :]
    band = (c > r) & (c <= r + WINDOW)
    band_first = band & (c >= WINDOW)
    band_bias = np.where(np.stack([band, band_first]), 0.0, NEG).astype(np.float32)
    tril = (np.arange(WINDOW)[:, None] >= np.arange(WINDOW)[None, :]).astype(np.float32)
    tri = np.arange(MLSTM_CHUNK)[:, None] >= np.arange(MLSTM_CHUNK)[None, :]
    return (jnp.asarray(tril), jnp.asarray(band_bias), jnp.asarray(tri, dtype=BF16),
            jnp.asarray(np.where(tri, 0.0, NEG).astype(np.float32)))


def _prompt_mix_call(za, zb, zc, lw, batch, seq):
    ts = PROMPT_TILE
    nt = seq // ts
    tok = lambda b, t: (b * nt + t, 0)
    const2 = lambda b, t: (0, 0)
    const3 = lambda b, t: (0, 0, 0)
    per_b3 = lambda b, t: (b, 0, 0)
    return pl.pallas_call(
        _prompt_mix_kernel,
        grid=(batch, nt),
        in_specs=[
            pl.BlockSpec(memory_space=pltpu.SMEM),
            pl.BlockSpec((ts, ZA_W), tok),
            pl.BlockSpec((ts, ZB_W), tok),
            pl.BlockSpec((ts, ZC_W), tok),
            pl.BlockSpec((1, A_WIDTH), const2),
            pl.BlockSpec((A_GROUPS, WINDOW, WINDOW), const3),
            pl.BlockSpec((WINDOW, LANES), const2),
            pl.BlockSpec((1, B_WIDTH), const2),
            pl.BlockSpec((1, B_KV_WIDTH), const2),
            pl.BlockSpec((C_CONV, 2 * C_WIDTH), const2),
            pl.BlockSpec((1, 2 * C_WIDTH), const2),
            pl.BlockSpec((1, LANES), const2),
            pl.BlockSpec((1, C_WIDTH), const2),
            pl.BlockSpec((WINDOW, WINDOW), const2),
            pl.BlockSpec((2, WINDOW, 2 * WINDOW), const3),
            pl.BlockSpec((MLSTM_CHUNK, MLSTM_CHUNK), const2),
            pl.BlockSpec((MLSTM_CHUNK, MLSTM_CHUNK), const2),
        ],
        out_specs=[
            pl.BlockSpec((ts, Y_W), tok),
            pl.BlockSpec((None, WINDOW, B_KV_WIDTH), per_b3),
            pl.BlockSpec((None, WINDOW, B_KV_WIDTH), per_b3),
            pl.BlockSpec((None, SUBLANES, 2 * C_WIDTH), per_b3),
            pl.BlockSpec((None, C_HEADS, C_HEAD_DIM, C_HEAD_DIM), lambda b, t: (b, 0, 0, 0)),
            pl.BlockSpec((None, C_HEADS, C_HEAD_DIM), per_b3),
            pl.BlockSpec((None, 1, LANES), per_b3),
        ],
        out_shape=[
            jax.ShapeDtypeStruct((batch * seq, Y_W), BF16),
            jax.ShapeDtypeStruct((batch, WINDOW, B_KV_WIDTH), F32),
            jax.ShapeDtypeStruct((batch, WINDOW, B_KV_WIDTH), F32),
            jax.ShapeDtypeStruct((batch, SUBLANES, 2 * C_WIDTH), F32),
            jax.ShapeDtypeStruct((batch, C_HEADS, C_HEAD_DIM, C_HEAD_DIM), F32),
            jax.ShapeDtypeStruct((batch, C_HEADS, C_HEAD_DIM), F32),
            jax.ShapeDtypeStruct((batch, 1, LANES), F32),
        ],
        scratch_shapes=[
            pltpu.VMEM((WINDOW, B_KV_WIDTH), F32),
            pltpu.VMEM((WINDOW, B_KV_WIDTH), F32),
            pltpu.VMEM((ts + SUBLANES, 2 * C_WIDTH), F32),
        ],
        compiler_params=pltpu.CompilerParams(
            dimension_semantics=("arbitrary", "arbitrary"), vmem_limit_bytes=VMEM_LIMIT),
        name="prompt_mixer",
    )(lw["sinks"], za, zb, zc, lw["vg"], lw["gws"], lw["gbs_col"], lw["qg"], lw["kg"],
      lw["cw"], lw["cb"], lw["fb"], lw["hg"], *_prompt_mask_constants())


def _sample_mix_kernel(sink_ref, za_ref, zb_ref, zc_ref, kc_ref, vc_ref, cs_ref, c0_ref, n0_ref,
                       m0_ref, vg_ref, gwb_ref, gbs_ref, qg_ref, kg_ref, cw_ref, cb_ref, fb_ref,
                       hg_ref,
                       y_ref, vrow_ref, ko_ref, vo_ref, convo_ref, c1_ref, n1_ref, m1_ref,
                       xbuf):
    nb = SAMPLE_NB
    t = SUBLANES
    rows = nb * t
    tok_r = lax.broadcasted_iota(jnp.int32, (rows, rows), 0)
    tok_c = lax.broadcasted_iota(jnp.int32, (rows, rows), 1)
    same_b = (tok_r // t) == (tok_c // t)
    causal_b = same_b & (tok_c <= tok_r)

    u, vn, sg = _gmlp_gate(za_ref, vg_ref)
    vrow_ref[...] = vn
    vnb = vn.astype(BF16)
    s_cols = []
    for gi in range(A_GROUPS):
        s_cols.append(_dot(gwb_ref[gi], vnb[:, gi * GROUP_DIM:(gi + 1) * GROUP_DIM])
                      + gbs_ref[:, gi:gi + 1])
    y_ref[:, 0:A_WIDTH] = (u * jnp.concatenate(s_cols, axis=1) * sg).astype(BF16)

    qn = _qk_norm(zb_ref[:, 0:B_WIDTH], qg_ref[...]) * (B_HEAD_DIM ** -0.5)
    kn = _qk_norm(zb_ref[:, B_WIDTH:B_WIDTH + B_KV_WIDTH], kg_ref[...])
    vv = zb_ref[:, B_WIDTH + B_KV_WIDTH:B_WIDTH + 2 * B_KV_WIDTH]
    sgb = _silu(zb_ref[:, B_WIDTH + 2 * B_KV_WIDTH:ZB_W])
    kn3 = kn.reshape(nb, t, B_KV_WIDTH)
    vv3 = vv.reshape(nb, t, B_KV_WIDTH)
    kcache = kc_ref[...]
    vcache = vc_ref[...]
    pad = jnp.zeros((nb, WINDOW - t, B_KV_WIDTH), F32)
    kall = jnp.concatenate([kcache, kn3, pad], axis=1).astype(BF16)
    vall = jnp.concatenate([vcache, vv3, pad], axis=1).astype(BF16)
    qp = jnp.concatenate([_place_q_head(qn, h, rows).reshape(nb, t, LANES) for h in range(B_HEADS)],
                         axis=1).astype(BF16)
    logits = lax.dot_general(qp, kall, (((2,), (2,)), ((0,), (0,))), preferred_element_type=F32)
    qrow = lax.broadcasted_iota(jnp.int32, (nb, B_HEADS * t, 2 * WINDOW), 1)
    kcol = lax.broadcasted_iota(jnp.int32, (nb, B_HEADS * t, 2 * WINDOW), 2)
    qt = qrow % t
    valid = ((kcol < WINDOW) & (kcol > qt)) | ((kcol >= WINDOW) & ((kcol - WINDOW) <= qt))
    hrow = lax.broadcasted_iota(jnp.int32, (B_HEADS * t, 1), 0) // t
    snk = jnp.zeros((B_HEADS * t, 1), F32)
    for h in range(B_HEADS):
        snk = jnp.where(hrow == h, sink_ref[h], snk)
    lg = jnp.where(valid, logits, NEG)
    mx = jnp.maximum(jnp.max(lg, axis=-1, keepdims=True), snk[None])
    p = jnp.exp(lg - mx)
    den = jnp.sum(p, axis=-1, keepdims=True) + jnp.exp(snk[None] - mx)
    pv = lax.dot_general(p.astype(BF16), vall, (((2,), (1,)), ((0,), (0,))),
                         preferred_element_type=F32) / den
    head_out = [pv[:, h * t:(h + 1) * t, :].reshape(rows, LANES) for h in range(B_HEADS)]
    yb = jnp.concatenate(
        [_merge_head_pair(head_out[2 * j], head_out[2 * j + 1], 2 * j, rows)
         for j in range(B_HEADS // 2)], axis=1)
    y_ref[:, A_WIDTH:A_WIDTH + B_WIDTH] = (yb * sgb).astype(BF16)
    ko_ref[...] = jnp.concatenate([kcache[:, t:, :], kn3], axis=1)
    vo_ref[...] = jnp.concatenate([vcache[:, t:, :], vv3], axis=1)

    xbuf[:, SUBLANES - (C_CONV - 1):SUBLANES, :] = cs_ref[...]
    xbuf[:, SUBLANES:2 * SUBLANES, :] = zc_ref[:, 0:2 * C_WIDTH].reshape(nb, t, 2 * C_WIDTH)
    y3 = cb_ref[...][None]
    for j in range(C_CONV):
        lo = SUBLANES - (C_CONV - 1) + j
        y3 = y3 + cw_ref[j:j + 1, :][None] * xbuf[:, lo:lo + t, :]
    convo_ref[...] = xbuf[:, 2 * SUBLANES - (C_CONV - 1):2 * SUBLANES, :]
    qk = _silu(y3.reshape(rows, 2 * C_WIDTH))
    qall = qk[:, 0:C_WIDTH].astype(BF16)
    kall_c = qk[:, C_WIDTH:2 * C_WIDTH] * (C_HEAD_DIM ** -0.5)
    vall_c = zc_ref[:, 2 * C_WIDTH:3 * C_WIDTH].astype(BF16)
    gate_o = _sigmoid(zc_ref[:, 3 * C_WIDTH:4 * C_WIDTH]) * _silu(zc_ref[:, 4 * C_WIDTH:5 * C_WIDTH])
    ifp = zc_ref[:, 5 * C_WIDTH:5 * C_WIDTH + LANES]
    lf = _log_sigmoid(ifp + fb_ref[...])
    lane_t = lax.broadcasted_iota(jnp.int32, (rows, LANES), 1)
    cum_all = _dot_exact01(jnp.where(causal_b, 1.0, 0.0).astype(BF16), lf)
    tot_all = _dot_exact01(jnp.where(same_b, 1.0, 0.0).astype(BF16), lf)
    st_col = jnp.where(lane_t < C_HEADS, ifp, cum_all)
    st_row = st_col.T
    tot_row = tot_all.T
    m0 = m0_ref[...]
    same_b_bf = jnp.where(same_b, 1.0, 0.0).astype(BF16)
    batch_of_lane = lax.broadcasted_iota(jnp.int32, (nb, 1, rows), 2) // t
    batch_id = lax.broadcasted_iota(jnp.int32, (nb, 1, rows), 0)
    own_tok = batch_of_lane == batch_id
    h_cols = []
    m_out = jnp.zeros((rows, LANES), F32)
    for hd in range(C_HEADS):
        hs = slice(hd * C_HEAD_DIM, (hd + 1) * C_HEAD_DIM)
        i_c = st_col[:, hd:hd + 1]
        cum_c = st_col[:, C_HEADS + hd:C_HEADS + hd + 1]
        tot_c = tot_all[:, C_HEADS + hd:C_HEADS + hd + 1]
        i_r = st_row[hd:hd + 1, :]
        cum_r = st_row[C_HEADS + hd:C_HEADS + hd + 1, :]
        tot_r = tot_row[C_HEADS + hd:C_HEADS + hd + 1, :]
        m_prev = m0[:, hd:hd + 1]
        dmat = jnp.where(causal_b, cum_c - cum_r + i_r, NEG)
        m_inter = cum_c + m_prev
        m_t = jnp.maximum(m_inter, jnp.max(dmat, axis=-1, keepdims=True))
        q_h = qall[:, hs]
        k_h = kall_c[:, hs]
        v_h = vall_c[:, hs]
        a = jnp.exp(dmat - m_t) * _dot_nt(q_h, k_h.astype(BF16))
        w_inter = jnp.exp(m_inter - m_t)
        c_prev = c0_ref[:, hd]
        n_tok = jnp.broadcast_to(n0_ref[hd][:, None, :], (nb, t, C_HEAD_DIM)).reshape(rows, C_HEAD_DIM)
        inter = lax.dot_general(q_h.reshape(nb, t, C_HEAD_DIM), c_prev.astype(BF16),
                                (((2,), (1,)), ((0,), (0,))), preferred_element_type=F32)
        num = _dot(a.astype(BF16), v_h) + w_inter * inter.reshape(rows, C_HEAD_DIM)
        den = (jnp.sum(a, axis=-1, keepdims=True)
               + w_inter * jnp.sum(q_h.astype(F32) * n_tok, axis=-1, keepdims=True))
        hh = num / jnp.maximum(jnp.abs(den), jnp.exp(-m_t))
        h_cols.append(_rms(hh))
        g_r = tot_r - cum_r + i_r
        g_c = tot_c - cum_c + i_c
        m_new = jnp.maximum(tot_c + m_prev,
                            jnp.max(jnp.where(same_b, g_r, NEG), axis=-1, keepdims=True))
        kw = jnp.exp(g_c - m_new) * k_h
        decay = jnp.exp(tot_c + m_prev - m_new)
        kwt = kw.T
        lhs = jnp.where(own_tok, kwt[None], 0.0).astype(BF16).reshape(nb * C_HEAD_DIM, rows)
        upd = _dot(lhs, v_h).reshape(nb, C_HEAD_DIM, C_HEAD_DIM)
        dec_b = jnp.broadcast_to(decay, (rows, C_HEAD_DIM)).reshape(nb, t, C_HEAD_DIM)[:, 0:1, :]
        c1_ref[:, hd] = dec_b * c_prev + upd
        n1_ref[hd] = decay * n_tok + _dot(same_b_bf, kw.astype(BF16))
        m_out = jnp.where(lane_t == hd, m_new, m_out)
    m1_ref[...] = m_out
    hn = jnp.concatenate(h_cols, axis=1) * hg_ref[...]
    y_ref[:, A_WIDTH + B_WIDTH:Y_W] = (hn * gate_o).astype(BF16)


def _sample_mix_call(l, za, zb, zc, kc, vc, cs, c0, n0t, m0tok, lw, nbatch):
    nb = SAMPLE_NB
    t = SUBLANES
    rows = nb * t
    tok = lambda i: (i, 0)
    const2 = lambda i: (0, 0)
    const3 = lambda i: (0, 0, 0)
    b3 = lambda i: (i, 0, 0)
    lb4 = lambda i: (l, i, 0, 0)
    return pl.pallas_call(
        _sample_mix_kernel,
        grid=(nbatch // nb,),
        in_specs=[
            pl.BlockSpec(memory_space=pltpu.SMEM),
            pl.BlockSpec((rows, ZA_W), tok),
            pl.BlockSpec((rows, ZB_W), tok),
            pl.BlockSpec((rows, ZC_W), tok),
            pl.BlockSpec((None, nb, WINDOW, B_KV_WIDTH), lb4),
            pl.BlockSpec((None, nb, WINDOW, B_KV_WIDTH), lb4),
            pl.BlockSpec((None, nb, C_CONV - 1, 2 * C_WIDTH), lb4),
            pl.BlockSpec((None, nb, C_HEADS, C_HEAD_DIM, C_HEAD_DIM), lambda i: (l, i, 0, 0, 0)),
            pl.BlockSpec((None, C_HEADS, nb, C_HEAD_DIM), lambda i: (l, 0, i, 0)),
            pl.BlockSpec((None, rows, LANES), lambda i: (l, i, 0)),
            pl.BlockSpec((1, A_WIDTH), const2),
            pl.BlockSpec((A_GROUPS, rows, rows), const3),
            pl.BlockSpec((rows, LANES), const2),
            pl.BlockSpec((1, B_WIDTH), const2),
            pl.BlockSpec((1, B_KV_WIDTH), const2),
            pl.BlockSpec((C_CONV, 2 * C_WIDTH), const2),
            pl.BlockSpec((1, 2 * C_WIDTH), const2),
            pl.BlockSpec((1, LANES), const2),
            pl.BlockSpec((1, C_WIDTH), const2),
        ],
        out_specs=[
            pl.BlockSpec((rows, Y_W), tok),
            pl.BlockSpec((rows, A_WIDTH), tok),
            pl.BlockSpec((nb, WINDOW, B_KV_WIDTH), b3),
            pl.BlockSpec((nb, WINDOW, B_KV_WIDTH), b3),
            pl.BlockSpec((nb, C_CONV - 1, 2 * C_WIDTH), b3),
            pl.BlockSpec((nb, C_HEADS, C_HEAD_DIM, C_HEAD_DIM), lambda i: (i, 0, 0, 0)),
            pl.BlockSpec((C_HEADS, rows, C_HEAD_DIM), lambda i: (0, i, 0)),
            pl.BlockSpec((rows, LANES), tok),
        ],
        out_shape=[
            jax.ShapeDtypeStruct((nbatch * t, Y_W), BF16),
            jax.ShapeDtypeStruct((nbatch * t, A_WIDTH), F32),
            jax.ShapeDtypeStruct((nbatch, WINDOW, B_KV_WIDTH), F32),
            jax.ShapeDtypeStruct((nbatch, WINDOW, B_KV_WIDTH), F32),
            jax.ShapeDtypeStruct((nbatch, C_CONV - 1, 2 * C_WIDTH), F32),
            jax.ShapeDtypeStruct((nbatch, C_HEADS, C_HEAD_DIM, C_HEAD_DIM), F32),
            jax.ShapeDtypeStruct((C_HEADS, nbatch * t, C_HEAD_DIM), F32),
            jax.ShapeDtypeStruct((nbatch * t, LANES), F32),
        ],
        scratch_shapes=[pltpu.VMEM((nb, 2 * SUBLANES, 2 * C_WIDTH), F32)],
        compiler_params=pltpu.CompilerParams(
            dimension_semantics=("arbitrary",), vmem_limit_bytes=VMEM_LIMIT),
        name="sample_mixer",
    )(lw["sinks"], za, zb, zc, kc, vc, cs, c0, n0t, m0tok, lw["vg"], lw["gwb"], lw["gbs_tok"],
      lw["qg"], lw["kg"], lw["cw"], lw["cb"], lw["fb"], lw["hg"])


def _layer_weights(l, w_in, b_in, gmlp_vnorm_g, gmlp_ws, gmlp_bs, swa_qnorm_g, swa_knorm_g,
                   swa_sinks, mlstm_conv_w, mlstm_conv_b, mlstm_f_bias, mlstm_hnorm_g,
                   w_branch_a, w_branch_b, w_branch_c, w_out, norm_g, dec_seq):
    wl, bl = w_in[l], b_in[l]
    o_ci = ZA_W + ZB_W + 3 * C_WIDTH
    o_co = o_ci + 2 * C_HEADS
    o_mg = o_co + 2 * C_WIDTH
    pad_w = LANES - 2 * C_HEADS

    def regroup(a):
        return jnp.concatenate(
            [a[..., :o_ci], a[..., o_co:o_mg], a[..., o_ci:o_co],
             jnp.zeros(a.shape[:-1] + (pad_w,), a.dtype)], axis=-1)

    t = dec_seq
    nb = SAMPLE_NB
    ws_t = gmlp_ws[l][:, :t, :t] * jnp.tril(jnp.ones((t, t), F32))
    eye = jnp.eye(nb, dtype=F32)
    gwb = jnp.einsum("bc,gts->gbtcs", eye, ws_t).reshape(A_GROUPS, nb * t, nb * t).astype(BF16)
    gbs_col = jnp.pad(gmlp_bs[l].T, ((0, 0), (0, LANES - A_GROUPS)))
    gbs_tok = jnp.pad(jnp.tile(gmlp_bs[l][:, :t].T, (nb, 1)), ((0, 0), (0, LANES - A_GROUPS)))
    fb = jnp.pad(mlstm_f_bias[l], (C_HEADS, LANES - 2 * C_HEADS)).reshape(1, LANES)
    return dict(
        ng=norm_g[l].reshape(1, D_MODEL),
        wcat=regroup(wl).astype(BF16), bcat=regroup(bl).reshape(1, ZCAT_W),
        wmg=wl[:, o_mg:].astype(BF16), bmg=bl[o_mg:].reshape(1, 3 * D_MODEL),
        wa=w_branch_a[l].astype(BF16), wb=w_branch_b[l].astype(BF16),
        wc=w_branch_c[l].astype(BF16), wo=w_out[l].astype(BF16),
        vg=gmlp_vnorm_g[l].reshape(1, A_WIDTH), gws=gmlp_ws[l], gwb=gwb,
        gbs_col=gbs_col, gbs_tok=gbs_tok,
        qg=jnp.tile(swa_qnorm_g[l], B_HEADS).reshape(1, B_WIDTH),
        kg=jnp.tile(swa_knorm_g[l], B_KV_HEADS).reshape(1, B_KV_WIDTH),
        sinks=swa_sinks[l],
        cw=mlstm_conv_w[l], cb=mlstm_conv_b[l].reshape(1, 2 * C_WIDTH), fb=fb,
        hg=mlstm_hnorm_g[l].reshape(1, C_WIDTH),
    )


def kernel(x_prompt, x_sample, cache_swa_k, cache_swa_v, state_mlstm_conv, state_mlstm_C, state_mlstm_n, state_mlstm_m, c_prompt, c_sample, ada_w, ada_b, norm_g, w_in, b_in, gmlp_vnorm_g, gmlp_ws, gmlp_bs, swa_qnorm_g, swa_knorm_g, swa_sinks, mlstm_conv_w, mlstm_conv_b, mlstm_f_bias, mlstm_hnorm_g, w_branch_a, w_branch_b, w_branch_c, w_out):
    batch, seq, _ = x_prompt.shape
    nbatch, dec_seq, _ = x_sample.shape
    assert dec_seq == SUBLANES and seq % PROMPT_TILE == 0 and nbatch % SAMPLE_NB == 0
    assert seq % PROJ_TILE == 0 and (nbatch * dec_seq) % PROJ_TILE == 0
    wb_len = cache_swa_k.shape[2]
    assert wb_len == WINDOW

    nc = batch + nbatch
    nc_pad = -(-nc // SUBLANES) * SUBLANES
    c_all = jnp.concatenate([c_prompt, c_sample, jnp.zeros((nc_pad - nc, D_MODEL), F32)], axis=0)
    mod_all = _ada_call(c_all, ada_w, ada_b)

    xp = x_prompt.reshape(batch * seq, D_MODEL)
    xs = x_sample.reshape(nbatch * dec_seq, D_MODEL)
    kc_all = cache_swa_k.reshape(DEPTH, nbatch, WINDOW, B_KV_WIDTH)
    vc_all = cache_swa_v.reshape(DEPTH, nbatch, WINDOW, B_KV_WIDTH)
    n0t_all = jnp.transpose(state_mlstm_n, (0, 2, 1, 3))
    m0tok_all = jnp.pad(jnp.repeat(state_mlstm_m, dec_seq, axis=1),
                        ((0, 0), (0, 0), (0, LANES - C_HEADS)))
    outs_p = [[] for _ in range(6)]
    outs_s = [[] for _ in range(6)]
    vrows = []
    for l in range(DEPTH):
        lw = _layer_weights(l, w_in, b_in, gmlp_vnorm_g, gmlp_ws, gmlp_bs, swa_qnorm_g,
                            swa_knorm_g, swa_sinks, mlstm_conv_w, mlstm_conv_b, mlstm_f_bias,
                            mlstm_hnorm_g, w_branch_a, w_branch_b, w_branch_c, w_out, norm_g,
                            dec_seq)
        mod_p = mod_all[l, :batch].reshape(batch, 1, 3 * D_MODEL)
        mod_s = jnp.repeat(mod_all[l, batch:nc], dec_seq, axis=0)

        za, zb, zc = _inproj_call(xp, mod_p, lw["ng"], lw["wcat"], lw["bcat"], seq)
        y, ko, vo, convo, c1, n1, m1 = _prompt_mix_call(za, zb, zc, lw, batch, seq)
        xp = _outproj_call(xp, mod_p, lw["ng"], y, lw["wmg"], lw["bmg"], lw["wa"], lw["wb"],
                           lw["wc"], lw["wo"], seq)
        outs_p[0].append(ko.reshape(batch, WINDOW, B_KV_HEADS, B_HEAD_DIM))
        outs_p[1].append(vo.reshape(batch, WINDOW, B_KV_HEADS, B_HEAD_DIM))
        outs_p[2].append(convo[:, SUBLANES - (C_CONV - 1):, :])
        outs_p[3].append(c1)
        outs_p[4].append(n1)
        outs_p[5].append(m1[:, 0, :C_HEADS])

        za, zb, zc = _inproj_call(xs, mod_s, lw["ng"], lw["wcat"], lw["bcat"], None)
        y, vrow, ko, vo, convo, c1, n1tok, m1tok = _sample_mix_call(
            l, za, zb, zc, kc_all, vc_all, state_mlstm_conv, state_mlstm_C, n0t_all, m0tok_all,
            lw, nbatch)
        xs = _outproj_call(xs, mod_s, lw["ng"], y, lw["wmg"], lw["bmg"], lw["wa"], lw["wb"],
                           lw["wc"], lw["wo"], None)
        outs_s[0].append(ko.reshape(nbatch, WINDOW, B_KV_HEADS, B_HEAD_DIM))
        outs_s[1].append(vo.reshape(nbatch, WINDOW, B_KV_HEADS, B_HEAD_DIM))
        outs_s[2].append(convo)
        outs_s[3].append(c1)
        outs_s[4].append(jnp.transpose(n1tok[:, ::dec_seq, :], (1, 0, 2)))
        outs_s[5].append(m1tok[::dec_seq, :C_HEADS])
        vrows.append(vrow.reshape(nbatch, dec_seq, A_WIDTH))

    sp = [jnp.stack(o) for o in outs_p]
    ss = [jnp.stack(o) for o in outs_s]
    return (xp.reshape(batch, seq, D_MODEL), xs.reshape(nbatch, dec_seq, D_MODEL),
            sp[0], sp[1], sp[2], sp[3], sp[4], sp[5],
            ss[0], ss[1], ss[2], ss[3], ss[4], ss[5], jnp.stack(vrows))
```

```python
import functools

import numpy as np
import jax
import jax.numpy as jnp
from jax import lax
from jax.experimental import pallas as pl
from jax.experimental.pallas import tpu as pltpu

F32 = jnp.float32
BF16 = jnp.bfloat16

D_MODEL = 1024
DEPTH = 2
A_WIDTH = 512
A_GROUPS = 4
GROUP_DIM = 128
B_HEADS = 8
B_KV_HEADS = 2
B_HEAD_DIM = 64
B_WIDTH = 512
B_KV_WIDTH = 128
WINDOW = 128
C_HEADS = 4
C_HEAD_DIM = 128
C_WIDTH = 512
C_CONV = 4
EPS = 1e-6
NEG = -1e30

LANES = 128
SUBLANES = 8
VMEM_LIMIT = 56 * 1024 * 1024

ZA_W = 3 * A_WIDTH
ZB_W = 2 * B_WIDTH + 2 * B_KV_WIDTH
ZC_W = 2 * C_WIDTH + 3 * C_WIDTH + LANES
ZCAT_W = ZA_W + ZB_W + ZC_W
Y_W = A_WIDTH + B_WIDTH + C_WIDTH

PROMPT_TILE = 256
MLSTM_CHUNK = PROMPT_TILE
SAMPLE_NB = 16
PROJ_TILE = 512


def _sigmoid(x):
    return 0.5 * jnp.tanh(0.5 * x) + 0.5


def _silu(x):
    return x * _sigmoid(x)


def _log_sigmoid(x):
    return jnp.minimum(x, 0.0) - jnp.log1p(jnp.exp(-jnp.abs(x)))


def _rms(x):
    return x * lax.rsqrt(jnp.mean(x * x, axis=-1, keepdims=True) + EPS)


def _dot(a, b):
    return jnp.dot(a, b, preferred_element_type=F32)


def _dot_nt(a, b):
    return lax.dot_general(a, b, (((1,), (1,)), ((), ())), preferred_element_type=F32)


def _dot_exact01(m01, x):
    hi = x.astype(BF16)
    r1 = x - hi.astype(F32)
    mid = r1.astype(BF16)
    lo = (r1 - mid.astype(F32)).astype(BF16)
    return _dot(m01, hi) + _dot(m01, mid) + _dot(m01, lo)


def _modulated_norm(x, mod_ref, ng_ref):
    xn = _rms(x) * ng_ref[...]
    shift = mod_ref[:, 0:D_MODEL]
    scale = mod_ref[:, D_MODEL:2 * D_MODEL]
    return (xn * (1.0 + scale) + shift).astype(BF16)


def _head_rms_scale(x2, lane_lo):
    s0 = jnp.sum(jnp.where(lane_lo, x2, 0.0), axis=-1, keepdims=True)
    s1 = jnp.sum(jnp.where(lane_lo, 0.0, x2), axis=-1, keepdims=True)
    r0 = lax.rsqrt(s0 * (1.0 / B_HEAD_DIM) + EPS)
    r1 = lax.rsqrt(s1 * (1.0 / B_HEAD_DIM) + EPS)
    return jnp.where(lane_lo, r0, r1)


def _qk_norm(x, g_row):
    rows, width = x.shape
    lane_lo = lax.broadcasted_iota(jnp.int32, (rows, LANES), 1) < B_HEAD_DIM
    outs = []
    for j in range(width // LANES):
        slab = x[:, j * LANES:(j + 1) * LANES]
        outs.append(slab * _head_rms_scale(slab * slab, lane_lo))
    y = outs[0] if len(outs) == 1 else jnp.concatenate(outs, axis=1)
    return y * g_row


def _ada_kernel(c_ref, w_ref, b_ref, o_ref):
    c = c_ref[...]
    o_ref[...] = _dot(_silu(c).astype(BF16), w_ref[...].astype(BF16)) + b_ref[...]


def _ada_call(c_all, ada_w, ada_b):
    rows = c_all.shape[0]
    return pl.pallas_call(
        _ada_kernel,
        grid=(DEPTH, 3),
        in_specs=[
            pl.BlockSpec((rows, D_MODEL), lambda l, j: (0, 0)),
            pl.BlockSpec((None, D_MODEL, D_MODEL), lambda l, j: (l, 0, j)),
            pl.BlockSpec((None, 1, D_MODEL), lambda l, j: (l, 0, j)),
        ],
        out_specs=pl.BlockSpec((None, rows, D_MODEL), lambda l, j: (l, 0, j)),
        out_shape=jax.ShapeDtypeStruct((DEPTH, rows, 3 * D_MODEL), F32),
        compiler_params=pltpu.CompilerParams(
            dimension_semantics=("arbitrary", "arbitrary"), vmem_limit_bytes=VMEM_LIMIT),
        name="adaln_mod",
    )(c_all, ada_w, ada_b.reshape(DEPTH, 1, 3 * D_MODEL))


def _col_chunks(width, step):
    return [(o, min(step, width - o)) for o in range(0, width, step)]


def _inproj_pieces(get_h, w_ref, b_ref, za_ref, zb_ref, zc_ref, step):
    def piece(o_ref, off, woff, w):
        def run():
            o_ref[:, off:off + w] = _dot(get_h(), w_ref[:, woff:woff + w]) + b_ref[:, woff:woff + w]
        return run
    pieces = []
    base = 0
    for o_ref, width in ((za_ref, ZA_W), (zb_ref, ZB_W), (zc_ref, ZC_W)):
        pieces += [piece(o_ref, off, base + off, w) for off, w in _col_chunks(width, step)]
        base += width
    return pieces


def _inproj_kernel(x_ref, mod_ref, ng_ref, w_ref, b_ref, za_ref, zb_ref, zc_ref):
    h = _modulated_norm(x_ref[...], mod_ref, ng_ref)
    for piece in _inproj_pieces(lambda: h, w_ref, b_ref, za_ref, zb_ref, zc_ref, 512):
        piece()


def _mod_spec(tm, tokens_per_batch):
    if tokens_per_batch is None:
        return pl.BlockSpec((tm, 3 * D_MODEL), lambda i: (i, 0))
    tiles_per_batch = tokens_per_batch // tm
    return pl.BlockSpec((None, 1, 3 * D_MODEL), lambda i: (i // tiles_per_batch, 0, 0))


def _inproj_call(x2, mod, ng, wcat, bcat, tokens_per_batch):
    ntok = x2.shape[0]
    tm = PROJ_TILE
    const = lambda i: (0, 0)
    return pl.pallas_call(
        _inproj_kernel,
        grid=(ntok // tm,),
        in_specs=[
            pl.BlockSpec((tm, D_MODEL), lambda i: (i, 0)),
            _mod_spec(tm, tokens_per_batch),
            pl.BlockSpec((1, D_MODEL), const),
            pl.BlockSpec((D_MODEL, ZCAT_W), const, pipeline_mode=pl.Buffered(1)),
            pl.BlockSpec((1, ZCAT_W), const),
        ],
        out_specs=[
            pl.BlockSpec((tm, ZA_W), lambda i: (i, 0)),
            pl.BlockSpec((tm, ZB_W), lambda i: (i, 0)),
            pl.BlockSpec((tm, ZC_W), lambda i: (i, 0)),
        ],
        out_shape=[
            jax.ShapeDtypeStruct((ntok, ZA_W), F32),
            jax.ShapeDtypeStruct((ntok, ZB_W), F32),
            jax.ShapeDtypeStruct((ntok, ZC_W), F32),
        ],
        compiler_params=pltpu.CompilerParams(
            dimension_semantics=("arbitrary",), vmem_limit_bytes=VMEM_LIMIT),
        name="in_projection",
    )(x2, mod, ng, wcat, bcat)


def _outproj_kernel(x_ref, mod_ref, ng_ref, y_ref, wmg_ref, bmg_ref, wa_ref, wb_ref, wc_ref,
                    wo_ref, o_ref):
    x = x_ref[...]
    h = _modulated_norm(x, mod_ref, ng_ref)
    merged = None
    for i, wbr_ref in enumerate((wa_ref, wb_ref, wc_ref)):
        cols = slice(i * D_MODEL, (i + 1) * D_MODEL)
        gate = _sigmoid(_dot(h, wmg_ref[:, cols]) + bmg_ref[:, cols])
        term = gate * _dot(y_ref[:, i * A_WIDTH:(i + 1) * A_WIDTH], wbr_ref[...])
        merged = term if merged is None else merged + term
    ada_gate = mod_ref[:, 2 * D_MODEL:3 * D_MODEL]
    o_ref[...] = x + ada_gate * _dot(merged.astype(BF16), wo_ref[...])


def _outproj_call(x2, mod, ng, y, wmg, bmg, wa, wb, wc, wo, tokens_per_batch):
    ntok = x2.shape[0]
    tm = PROJ_TILE
    const = lambda i: (0, 0)
    once = pl.Buffered(1)
    return pl.pallas_call(
        _outproj_kernel,
        grid=(ntok // tm,),
        in_specs=[
            pl.BlockSpec((tm, D_MODEL), lambda i: (i, 0)),
            _mod_spec(tm, tokens_per_batch),
            pl.BlockSpec((1, D_MODEL), const),
            pl.BlockSpec((tm, Y_W), lambda i: (i, 0)),
            pl.BlockSpec((D_MODEL, 3 * D_MODEL), const, pipeline_mode=once),
            pl.BlockSpec((1, 3 * D_MODEL), const),
            pl.BlockSpec((A_WIDTH, D_MODEL), const, pipeline_mode=once),
            pl.BlockSpec((B_WIDTH, D_MODEL), const, pipeline_mode=once),
            pl.BlockSpec((C_WIDTH, D_MODEL), const, pipeline_mode=once),
            pl.BlockSpec((D_MODEL, D_MODEL), const, pipeline_mode=once),
        ],
        out_specs=pl.BlockSpec((tm, D_MODEL), lambda i: (i, 0)),
        out_shape=jax.ShapeDtypeStruct((ntok, D_MODEL), F32),
        compiler_params=pltpu.CompilerParams(
            dimension_semantics=("arbitrary",), vmem_limit_bytes=VMEM_LIMIT),
        name="out_projection",
    )(x2, mod, ng, y, wmg, bmg, wa, wb, wc, wo)


def _gmlp_gate(za_ref, vg_ref):
    u = za_ref[:, 0:A_WIDTH]
    vn = _rms(za_ref[:, A_WIDTH:2 * A_WIDTH]) * vg_ref[...]
    sg = _silu(za_ref[:, 2 * A_WIDTH:3 * A_WIDTH])
    return u, vn, sg


def _place_q_head(qn, h, rows):
    lane = lax.broadcasted_iota(jnp.int32, (rows, LANES), 1)
    slab = qn[:, (h // 2) * LANES:(h // 2 + 1) * LANES]
    src_hi = h % 2
    dst_hi = h // (B_HEADS // B_KV_HEADS)
    keep = (lane >= B_HEAD_DIM) if src_hi else (lane < B_HEAD_DIM)
    slab = jnp.where(keep, slab, 0.0)
    if src_hi != dst_hi:
        slab = pltpu.roll(slab, B_HEAD_DIM, 1)
    return slab


def _merge_head_pair(o_even, o_odd, h_even, rows):
    lane_lo = lax.broadcasted_iota(jnp.int32, (rows, LANES), 1) < B_HEAD_DIM
    kv_hi = h_even // (B_HEADS // B_KV_HEADS)
    if kv_hi:
        o_even = pltpu.roll(o_even, B_HEAD_DIM, 1)
    else:
        o_odd = pltpu.roll(o_odd, B_HEAD_DIM, 1)
    return jnp.where(lane_lo, o_even, o_odd)


def _conv_taps(xbuf_window, cw_ref, cb_ref):
    y = cb_ref[...]
    for j in range(C_CONV):
        y = y + cw_ref[j:j + 1, :] * xbuf_window(j)
    return y


def _prompt_mix_kernel(sink_ref, za_ref, zb_ref, zc_ref, vg_ref, gw_ref, gbs_ref, qg_ref, kg_ref,
                       cw_ref, cb_ref, fb_ref, hg_ref, tril_ref, band_ref, tri01_ref, tribias_ref,
                       y_ref, ko_ref, vo_ref, convo_ref, c_ref, n_ref, m_ref,
                       kprev, vprev, xbuf, first_tile, pump):
    ts = PROMPT_TILE

    u, vn, sg = _gmlp_gate(za_ref, vg_ref)
    vnb = vn.astype(BF16)
    wts = [(gw_ref[gi] * tril_ref[...]).astype(BF16) for gi in range(A_GROUPS)]
    s_rows = []
    for c in range(ts // WINDOW):
        s_cols = []
        for gi in range(A_GROUPS):
            vblk = vnb[c * WINDOW:(c + 1) * WINDOW, gi * GROUP_DIM:(gi + 1) * GROUP_DIM]
            s_cols.append(_dot(wts[gi], vblk) + gbs_ref[:, gi:gi + 1])
        s_rows.append(jnp.concatenate(s_cols, axis=1))
    s = jnp.concatenate(s_rows, axis=0)
    pump()
    y_ref[:, 0:A_WIDTH] = (u * s * sg).astype(BF16)
    pump()

    qn = _qk_norm(zb_ref[:, 0:B_WIDTH], qg_ref[...]) * (B_HEAD_DIM ** -0.5)
    pump()
    kn = _qk_norm(zb_ref[:, B_WIDTH:B_WIDTH + B_KV_WIDTH], kg_ref[...])
    vv = zb_ref[:, B_WIDTH + B_KV_WIDTH:B_WIDTH + 2 * B_KV_WIDTH]
    sgb = _silu(zb_ref[:, B_WIDTH + 2 * B_KV_WIDTH:ZB_W])
    pump()
    grp = B_HEADS // B_KV_HEADS
    nblk = ts // WINDOW
    lane_lo2 = lax.broadcasted_iota(jnp.int32, (2 * WINDOW, LANES), 1) < B_HEAD_DIM
    kblocks = [kprev[...]] + [kn[b * WINDOW:(b + 1) * WINDOW] for b in range(nblk)]
    vblocks = [vprev[...]] + [vv[b * WINDOW:(b + 1) * WINDOW] for b in range(nblk)]
    bias0 = band_ref[0] if first_tile is False else jnp.where(first_tile, band_ref[1], band_ref[0])
    bias = [bias0] + [band_ref[0]] * (nblk - 1)
    combos = [(blk, kh) for blk in range(nblk) for kh in range(B_KV_HEADS)]
    heads = [(blk, kh, g) for blk, kh in combos for g in range(grp)]
    kdup, vdup = {}, {}
    for blk in range(nblk):
        kcat = jnp.concatenate([kblocks[blk], kblocks[blk + 1]], axis=0)
        vcat = jnp.concatenate([vblocks[blk], vblocks[blk + 1]], axis=0)
        krol = pltpu.roll(kcat, B_HEAD_DIM, 1)
        vrol = pltpu.roll(vcat, B_HEAD_DIM, 1)
        for kh in range(B_KV_HEADS):
            own = lane_lo2 if kh == 0 else jnp.logical_not(lane_lo2)
            kdup[blk, kh] = jnp.where(own, kcat, krol).astype(BF16)
            vdup[blk, kh] = jnp.where(own, vcat, vrol).astype(BF16)
    pump()
    qs = {(blk, kh): jnp.concatenate(
        [_place_q_head(qn[blk * WINDOW:(blk + 1) * WINDOW], kh * grp + g, WINDOW) for g in range(grp)],
        axis=0).astype(BF16) for blk, kh in combos}
    pump()
    logits = {c: _dot_nt(qs[c], kdup[c]) for c in combos}
    pump()
    snk = {k: sink_ref[k[1] * grp + k[2]] for k in heads}
    lg = {(blk, kh, g): logits[blk, kh][g * WINDOW:(g + 1) * WINDOW] + bias[blk]
          for blk, kh, g in heads}
    pump()
    mx = {k: jnp.maximum(jnp.max(lg[k], axis=-1, keepdims=True), snk[k]) for k in heads}
    pump()
    p = {k: jnp.exp(lg[k] - mx[k]) for k in heads}
    pump()
    rden = {k: 1.0 / (jnp.sum(p[k], axis=-1, keepdims=True) + jnp.exp(snk[k] - mx[k])) for k in heads}
    pump()
    pv = {c: _dot(jnp.concatenate([p[c + (g,)].astype(BF16) for g in range(grp)], axis=0), vdup[c])
          for c in combos}
    pump()
    outs = {(blk, kh, g): pv[blk, kh][g * WINDOW:(g + 1) * WINDOW] * rden[blk, kh, g]
            for blk, kh, g in heads}
    pump()
    yb = jnp.concatenate([jnp.concatenate(
        [_merge_head_pair(outs[blk, (2 * j) // grp, (2 * j) % grp],
                          outs[blk, (2 * j + 1) // grp, (2 * j + 1) % grp], 2 * j, WINDOW)
         for j in range(B_HEADS // 2)], axis=1) for blk in range(nblk)], axis=0)
    y_ref[:, A_WIDTH:A_WIDTH + B_WIDTH] = (yb * sgb).astype(BF16)
    pump()
    kprev[...] = kblocks[nblk]
    vprev[...] = vblocks[nblk]
    ko_ref[...] = kblocks[nblk]
    vo_ref[...] = vblocks[nblk]
    pump()

    xbuf[SUBLANES:SUBLANES + ts, :] = zc_ref[:, 0:2 * C_WIDTH]
    qk = _silu(_conv_taps(
        lambda j: xbuf[SUBLANES - (C_CONV - 1) + j:SUBLANES - (C_CONV - 1) + j + ts, :],
        cw_ref, cb_ref))
    pump()
    tail = xbuf[ts:ts + SUBLANES, :]
    xbuf[0:SUBLANES, :] = tail
    convo_ref[...] = tail
    qall = qk[:, 0:C_WIDTH].astype(BF16)
    kall = qk[:, C_WIDTH:2 * C_WIDTH] * (C_HEAD_DIM ** -0.5)
    vall = zc_ref[:, 2 * C_WIDTH:3 * C_WIDTH].astype(BF16)
    gate_o = _sigmoid(zc_ref[:, 3 * C_WIDTH:4 * C_WIDTH]) * _silu(zc_ref[:, 4 * C_WIDTH:5 * C_WIDTH])
    pump()
    ifp = zc_ref[:, 5 * C_WIDTH:5 * C_WIDTH + LANES]
    lf = _log_sigmoid(ifp + fb_ref[...])
    pump()
    cl = MLSTM_CHUNK
    hds = range(C_HEADS)
    lane_c = lax.broadcasted_iota(jnp.int32, (cl, LANES), 1)
    lane_1 = lax.broadcasted_iota(jnp.int32, (1, LANES), 1)
    m_row = m_ref[...]
    cum_all = _dot_exact01(tri01_ref[...], lf)
    st_col = jnp.where(lane_c < C_HEADS, ifp, cum_all)
    st_row = st_col.T
    pump()
    hs = [slice(hd * C_HEAD_DIM, (hd + 1) * C_HEAD_DIM) for hd in hds]
    i_c = [st_col[:, hd:hd + 1] for hd in hds]
    cum_c = [st_col[:, C_HEADS + hd:C_HEADS + hd + 1] for hd in hds]
    i_r = [st_row[hd:hd + 1, :] for hd in hds]
    cum_r = [st_row[C_HEADS + hd:C_HEADS + hd + 1, :] for hd in hds]
    m_prev = [m_row[:, hd:hd + 1] for hd in hds]
    tribias = tribias_ref[...]
    dmat = [cum_c[hd] - cum_r[hd] + i_r[hd] + tribias for hd in hds]
    pump()
    m_inter = [cum_c[hd] + m_prev[hd] for hd in hds]
    m_t = [jnp.maximum(m_inter[hd], jnp.max(dmat[hd], axis=-1, keepdims=True)) for hd in hds]
    pump()
    q_h = [qall[:, hs[hd]] for hd in hds]
    k_h = [kall[:, hs[hd]] for hd in hds]
    v_h = [vall[:, hs[hd]] for hd in hds]
    s_qk = [_dot_nt(q_h[hd], k_h[hd].astype(BF16)) for hd in hds]
    pump()
    a = [jnp.exp(dmat[hd] - m_t[hd]) * s_qk[hd] for hd in hds]
    pump()
    w_inter = [jnp.exp(m_inter[hd] - m_t[hd]) for hd in hds]
    c_prev = [c_ref[hd] for hd in hds]
    n_prev = [n_ref[hd:hd + 1, :] for hd in hds]
    inter = [_dot(q_h[hd], c_prev[hd].astype(BF16)) for hd in hds]
    pump()
    intra = [_dot(a[hd].astype(BF16), v_h[hd]) for hd in hds]
    pump()
    den = [jnp.sum(a[hd], axis=-1, keepdims=True)
           + w_inter[hd] * jnp.sum(q_h[hd].astype(F32) * n_prev[hd], axis=-1, keepdims=True)
           for hd in hds]
    pump()
    rnorm = [1.0 / jnp.maximum(jnp.abs(den[hd]), jnp.exp(-m_t[hd])) for hd in hds]
    hh = [(intra[hd] + w_inter[hd] * inter[hd]) * rnorm[hd] for hd in hds]
    pump()
    hn = jnp.concatenate([_rms(hh[hd]) for hd in hds], axis=1) * hg_ref[...]
    y_ref[:, A_WIDTH + B_WIDTH:Y_W] = (hn * gate_o).astype(BF16)
    pump()
    total = [cum_r[hd][:, cl - 1:cl] for hd in hds]
    g_r = [total[hd] - cum_r[hd] + i_r[hd] for hd in hds]
    g_c = [total[hd] - cum_c[hd] + i_c[hd] for hd in hds]
    m_new = [jnp.maximum(total[hd] + m_prev[hd], jnp.max(g_r[hd], axis=-1, keepdims=True))
             for hd in hds]
    pump()
    kw = [jnp.exp(g_c[hd] - m_new[hd]) * k_h[hd] for hd in hds]
    decay = [jnp.exp(total[hd] + m_prev[hd] - m_new[hd]) for hd in hds]
    pump()
    upd = [_dot(kw[hd].T.astype(BF16), v_h[hd]) for hd in hds]
    pump()
    for hd in hds:
        c_ref[hd] = decay[hd] * c_prev[hd] + upd[hd]
        n_ref[hd:hd + 1, :] = decay[hd] * n_prev[hd] + jnp.sum(kw[hd], axis=0, keepdims=True)
        m_row = jnp.where(lane_1 == hd, m_new[hd], m_row)
    m_ref[...] = m_row


def _prompt_mask_constants():
    r = np.arange(WINDOW)[:, None]
    c = np.arange(2 * WINDOW)[None, :]
    band = (c > r) & (c <= r + WINDOW)
    band_first = band & (c >= WINDOW)
    band_bias = np.where(np.stack([band, band_first]), 0.0, NEG).astype(np.float32)
    tril = (np.arange(WINDOW)[:, None] >= np.arange(WINDOW)[None, :]).astype(np.float32)
    tri = np.arange(MLSTM_CHUNK)[:, None] >= np.arange(MLSTM_CHUNK)[None, :]
    return (jnp.asarray(tril), jnp.asarray(band_bias), jnp.asarray(tri, dtype=BF16),
            jnp.asarray(np.where(tri, 0.0, NEG).astype(np.float32)))


N_MIX_PARAMS = 13
MIX_PUMP_SITES = 31
MXU_PIECE_COLS = 256


class _Interleaver:
    def __init__(self, pieces, sites):
        self._pieces = list(pieces)
        self._sites_left = sites

    def __call__(self):
        assert self._sites_left > 0
        n = -(-len(self._pieces) // self._sites_left)
        self._sites_left -= 1
        for _ in range(n):
            self._pieces.pop(0)()

    def finish(self):
        assert self._sites_left == 0 and not self._pieces


def _gate_pieces(h_ref, wmg_ref, bmg_ref, g_ref):
    def piece(off):
        cols = slice(off, off + MXU_PIECE_COLS)
        def run():
            g_ref[:, cols] = _sigmoid(_dot(h_ref[...], wmg_ref[:, cols]) + bmg_ref[:, cols])
        return run
    return [piece(off) for off in range(0, 3 * D_MODEL, MXU_PIECE_COLS)]


def _merge_and_project(x, mod_ref, g_ref, y_ref, wa_ref, wb_ref, wc_ref, wo_ref):
    merged = None
    for i, wbr_ref in enumerate((wa_ref, wb_ref, wc_ref)):
        term = (g_ref[:, i * D_MODEL:(i + 1) * D_MODEL]
                * _dot(y_ref[:, i * A_WIDTH:(i + 1) * A_WIDTH], wbr_ref[...]))
        merged = term if merged is None else merged + term
    ada_gate = mod_ref[:, 2 * D_MODEL:3 * D_MODEL]
    return x + ada_gate * _dot(merged.astype(BF16), wo_ref[...])


def _prompt_layer_kernel(tiles_per_seq, sink_ref, x2_ref, xn_ref, mod_ref, modn_ref, ng_ref,
                         wcat_ref, bcat_ref, *rest):
    mix_params = rest[:N_MIX_PARAMS]
    wmg_ref, bmg_ref, wa_ref, wb_ref, wc_ref, wo_ref = rest[N_MIX_PARAMS:N_MIX_PARAMS + 6]
    o_ref, ko_ref, vo_ref, convo_ref, c_ref, n_ref, m_ref = rest[N_MIX_PARAMS + 6:N_MIX_PARAMS + 13]
    (za0, zb0, zc0, za1, zb1, zc1, h0, h1, y_scr, g_scr, kprev, vprev, xbuf) = rest[N_MIX_PARAMS + 13:]
    ts = PROMPT_TILE
    z = ((za0, zb0, zc0), (za1, zb1, zc1))
    h = (h0, h1)
    k = pl.program_id(0)
    seq_start = (k % (tiles_per_seq // 2)) == 0

    @pl.when(k == 0)
    def _():
        h0[...] = _modulated_norm(x2_ref[0:ts, :], mod_ref, ng_ref)
        for piece in _inproj_pieces(lambda: h0[...], wcat_ref, bcat_ref, *z[0], 512):
            piece()

    @pl.when(seq_start)
    def _():
        kprev[...] = jnp.zeros_like(kprev)
        vprev[...] = jnp.zeros_like(vprev)
        xbuf[0:SUBLANES, :] = jnp.zeros((SUBLANES, 2 * C_WIDTH), F32)
        c_ref[...] = jnp.zeros_like(c_ref)
        n_ref[...] = jnp.zeros_like(n_ref)
        m_ref[...] = jnp.zeros_like(m_ref)

    for half in range(2):
        cur, nxt = half, 1 - half
        rows = slice(half * ts, (half + 1) * ts)
        if half == 0:
            h[nxt][...] = _modulated_norm(x2_ref[ts:2 * ts, :], mod_ref, ng_ref)
        else:
            h[nxt][...] = _modulated_norm(xn_ref[...], modn_ref, ng_ref)
        get_h_next = functools.partial(lambda r: r[...], h[nxt])
        pump = _Interleaver(
            _inproj_pieces(get_h_next, wcat_ref, bcat_ref, *z[nxt], MXU_PIECE_COLS)
            + _gate_pieces(h[cur], wmg_ref, bmg_ref, g_scr), MIX_PUMP_SITES)
        _prompt_mix_kernel(sink_ref, *z[cur], *mix_params,
                           y_scr, ko_ref, vo_ref, convo_ref, c_ref, n_ref, m_ref, kprev, vprev, xbuf,
                           first_tile=seq_start if half == 0 else False, pump=pump)
        pump.finish()
        o_ref[rows, :] = _merge_and_project(x2_ref[rows, :], mod_ref, g_scr, y_scr,
                                            wa_ref, wb_ref, wc_ref, wo_ref)


def _prompt_layer_call(x2, mod, lw, batch, seq):
    ts = PROMPT_TILE
    nt = seq // ts
    assert nt % 2 == 0
    last_tile = batch * nt - 1
    const2 = lambda k: (0, 0)
    const3 = lambda k: (0, 0, 0)
    per_b3 = lambda k: ((2 * k) // nt, 0, 0)
    next_tile = lambda k: jnp.minimum(2 * k + 2, last_tile)
    once = pl.Buffered(1)
    return pl.pallas_call(
        functools.partial(_prompt_layer_kernel, nt),
        grid=(batch * nt // 2,),
        in_specs=[
            pl.BlockSpec(memory_space=pltpu.SMEM),
            pl.BlockSpec((2 * ts, D_MODEL), lambda k: (k, 0)),
            pl.BlockSpec((ts, D_MODEL), lambda k: (next_tile(k), 0)),
            pl.BlockSpec((None, 1, 3 * D_MODEL), per_b3),
            pl.BlockSpec((None, 1, 3 * D_MODEL), lambda k: (next_tile(k) // nt, 0, 0)),
            pl.BlockSpec((1, D_MODEL), const2),
            pl.BlockSpec((D_MODEL, ZCAT_W), const2, pipeline_mode=once),
            pl.BlockSpec((1, ZCAT_W), const2),
            pl.BlockSpec((1, A_WIDTH), const2),
            pl.BlockSpec((A_GROUPS, WINDOW, WINDOW), const3),
            pl.BlockSpec((WINDOW, LANES), const2),
            pl.BlockSpec((1, B_WIDTH), const2),
            pl.BlockSpec((1, B_KV_WIDTH), const2),
            pl.BlockSpec((C_CONV, 2 * C_WIDTH), const2),
            pl.BlockSpec((1, 2 * C_WIDTH), const2),
            pl.BlockSpec((1, LANES), const2),
            pl.BlockSpec((1, C_WIDTH), const2),
            pl.BlockSpec((WINDOW, WINDOW), const2),
            pl.BlockSpec((2, WINDOW, 2 * WINDOW), const3),
            pl.BlockSpec((MLSTM_CHUNK, MLSTM_CHUNK), const2),
            pl.BlockSpec((MLSTM_CHUNK, MLSTM_CHUNK), const2),
            pl.BlockSpec((D_MODEL, 3 * D_MODEL), const2, pipeline_mode=once),
            pl.BlockSpec((1, 3 * D_MODEL), const2),
            pl.BlockSpec((A_WIDTH, D_MODEL), const2, pipeline_mode=once),
            pl.BlockSpec((B_WIDTH, D_MODEL), const2, pipeline_mode=once),
            pl.BlockSpec((C_WIDTH, D_MODEL), const2, pipeline_mode=once),
            pl.BlockSpec((D_MODEL, D_MODEL), const2, pipeline_mode=once),
        ],
        out_specs=[
            pl.BlockSpec((2 * ts, D_MODEL), lambda k: (k, 0)),
            pl.BlockSpec((None, WINDOW, B_KV_WIDTH), per_b3),
            pl.BlockSpec((None, WINDOW, B_KV_WIDTH), per_b3),
            pl.BlockSpec((None, SUBLANES, 2 * C_WIDTH), per_b3),
            pl.BlockSpec((None, C_HEADS, C_HEAD_DIM, C_HEAD_DIM), lambda k: ((2 * k) // nt, 0, 0, 0)),
            pl.BlockSpec((None, C_HEADS, C_HEAD_DIM), per_b3),
            pl.BlockSpec((None, 1, LANES), per_b3),
        ],
        out_shape=[
            jax.ShapeDtypeStruct((batch * seq, D_MODEL), F32),
            jax.ShapeDtypeStruct((batch, WINDOW, B_KV_WIDTH), F32),
            jax.ShapeDtypeStruct((batch, WINDOW, B_KV_WIDTH), F32),
            jax.ShapeDtypeStruct((batch, SUBLANES, 2 * C_WIDTH), F32),
            jax.ShapeDtypeStruct((batch, C_HEADS, C_HEAD_DIM, C_HEAD_DIM), F32),
            jax.ShapeDtypeStruct((batch, C_HEADS, C_HEAD_DIM), F32),
            jax.ShapeDtypeStruct((batch, 1, LANES), F32),
        ],
        scratch_shapes=(
            [pltpu.VMEM((ts, w), F32) for w in (ZA_W, ZB_W, ZC_W)] * 2
            + [pltpu.VMEM((ts, D_MODEL), BF16)] * 2
            + [pltpu.VMEM((ts, Y_W), BF16),
               pltpu.VMEM((ts, 3 * D_MODEL), F32),
               pltpu.VMEM((WINDOW, B_KV_WIDTH), F32),
               pltpu.VMEM((WINDOW, B_KV_WIDTH), F32),
               pltpu.VMEM((ts + SUBLANES, 2 * C_WIDTH), F32)]),
        compiler_params=pltpu.CompilerParams(
            dimension_semantics=("arbitrary",), vmem_limit_bytes=VMEM_LIMIT),
        name="prompt_layer",
    )(lw["sinks"], x2, x2, mod, mod, lw["ng"], lw["wcat"], lw["bcat"],
      lw["vg"], lw["gws"], lw["gbs_col"], lw["qg"], lw["kg"], lw["cw"], lw["cb"], lw["fb"], lw["hg"],
      *_prompt_mask_constants(),
      lw["wmg"], lw["bmg"], lw["wa"], lw["wb"], lw["wc"], lw["wo"])


def _sample_mix_kernel(sink_ref, za_ref, zb_ref, zc_ref, kc_ref, vc_ref, cs_ref, c0_ref, n0_ref,
                       m0_ref, vg_ref, gwb_ref, gbs_ref, qg_ref, kg_ref, cw_ref, cb_ref, fb_ref,
                       hg_ref,
                       y_ref, vrow_ref, ko_ref, vo_ref, convo_ref, c1_ref, n1_ref, m1_ref,
                       xbuf):
    nb = SAMPLE_NB
    t = SUBLANES
    rows = nb * t
    tok_r = lax.broadcasted_iota(jnp.int32, (rows, rows), 0)
    tok_c = lax.broadcasted_iota(jnp.int32, (rows, rows), 1)
    same_b = (tok_r // t) == (tok_c // t)
    causal_b = same_b & (tok_c <= tok_r)

    u, vn, sg = _gmlp_gate(za_ref, vg_ref)
    vrow_ref[...] = vn
    vnb = vn.astype(BF16)
    s_cols = []
    for gi in range(A_GROUPS):
        s_cols.append(_dot(gwb_ref[gi], vnb[:, gi * GROUP_DIM:(gi + 1) * GROUP_DIM])
                      + gbs_ref[:, gi:gi + 1])
    y_ref[:, 0:A_WIDTH] = (u * jnp.concatenate(s_cols, axis=1) * sg).astype(BF16)

    qn = _qk_norm(zb_ref[:, 0:B_WIDTH], qg_ref[...]) * (B_HEAD_DIM ** -0.5)
    kn = _qk_norm(zb_ref[:, B_WIDTH:B_WIDTH + B_KV_WIDTH], kg_ref[...])
    vv = zb_ref[:, B_WIDTH + B_KV_WIDTH:B_WIDTH + 2 * B_KV_WIDTH]
    sgb = _silu(zb_ref[:, B_WIDTH + 2 * B_KV_WIDTH:ZB_W])
    kn3 = kn.reshape(nb, t, B_KV_WIDTH)
    vv3 = vv.reshape(nb, t, B_KV_WIDTH)
    kcache = kc_ref[...]
    vcache = vc_ref[...]
    pad = jnp.zeros((nb, WINDOW - t, B_KV_WIDTH), F32)
    kall = jnp.concatenate([kcache, kn3, pad], axis=1).astype(BF16)
    vall = jnp.concatenate([vcache, vv3, pad], axis=1).astype(BF16)
    qp = jnp.concatenate([_place_q_head(qn, h, rows).reshape(nb, t, LANES) for h in range(B_HEADS)],
                         axis=1).astype(BF16)
    logits = lax.dot_general(qp, kall, (((2,), (2,)), ((0,), (0,))), preferred_element_type=F32)
    qrow = lax.broadcasted_iota(jnp.int32, (nb, B_HEADS * t, 2 * WINDOW), 1)
    kcol = lax.broadcasted_iota(jnp.int32, (nb, B_HEADS * t, 2 * WINDOW), 2)
    qt = qrow % t
    valid = ((kcol < WINDOW) & (kcol > qt)) | ((kcol >= WINDOW) & ((kcol - WINDOW) <= qt))
    hrow = lax.broadcasted_iota(jnp.int32, (B_HEADS * t, 1), 0) // t
    snk = jnp.zeros((B_HEADS * t, 1), F32)
    for h in range(B_HEADS):
        snk = jnp.where(hrow == h, sink_ref[h], snk)
    lg = jnp.where(valid, logits, NEG)
    mx = jnp.maximum(jnp.max(lg, axis=-1, keepdims=True), snk[None])
    p = jnp.exp(lg - mx)
    den = jnp.sum(p, axis=-1, keepdims=True) + jnp.exp(snk[None] - mx)
    pv = lax.dot_general(p.astype(BF16), vall, (((2,), (1,)), ((0,), (0,))),
                         preferred_element_type=F32) / den
    head_out = [pv[:, h * t:(h + 1) * t, :].reshape(rows, LANES) for h in range(B_HEADS)]
    yb = jnp.concatenate(
        [_merge_head_pair(head_out[2 * j], head_out[2 * j + 1], 2 * j, rows)
         for j in range(B_HEADS // 2)], axis=1)
    y_ref[:, A_WIDTH:A_WIDTH + B_WIDTH] = (yb * sgb).astype(BF16)
    ko_ref[...] = jnp.concatenate([kcache[:, t:, :], kn3], axis=1)
    vo_ref[...] = jnp.concatenate([vcache[:, t:, :], vv3], axis=1)

    xbuf[:, SUBLANES - (C_CONV - 1):SUBLANES, :] = cs_ref[...]
    xbuf[:, SUBLANES:2 * SUBLANES, :] = zc_ref[:, 0:2 * C_WIDTH].reshape(nb, t, 2 * C_WIDTH)
    y3 = cb_ref[...][None]
    for j in range(C_CONV):
        lo = SUBLANES - (C_CONV - 1) + j
        y3 = y3 + cw_ref[j:j + 1, :][None] * xbuf[:, lo:lo + t, :]
    convo_ref[...] = xbuf[:, 2 * SUBLANES - (C_CONV - 1):2 * SUBLANES, :]
    qk = _silu(y3.reshape(rows, 2 * C_WIDTH))
    qall = qk[:, 0:C_WIDTH].astype(BF16)
    kall_c = qk[:, C_WIDTH:2 * C_WIDTH] * (C_HEAD_DIM ** -0.5)
    vall_c = zc_ref[:, 2 * C_WIDTH:3 * C_WIDTH].astype(BF16)
    gate_o = _sigmoid(zc_ref[:, 3 * C_WIDTH:4 * C_WIDTH]) * _silu(zc_ref[:, 4 * C_WIDTH:5 * C_WIDTH])
    ifp = zc_ref[:, 5 * C_WIDTH:5 * C_WIDTH + LANES]
    lf = _log_sigmoid(ifp + fb_ref[...])
    lane_t = lax.broadcasted_iota(jnp.int32, (rows, LANES), 1)
    cum_all = _dot_exact01(jnp.where(causal_b, 1.0, 0.0).astype(BF16), lf)
    tot_all = _dot_exact01(jnp.where(same_b, 1.0, 0.0).astype(BF16), lf)
    st_col = jnp.where(lane_t < C_HEADS, ifp, cum_all)
    st_row = st_col.T
    tot_row = tot_all.T
    m0 = m0_ref[...]
    same_b_bf = jnp.where(same_b, 1.0, 0.0).astype(BF16)
    batch_of_lane = lax.broadcasted_iota(jnp.int32, (nb, 1, rows), 2) // t
    batch_id = lax.broadcasted_iota(jnp.int32, (nb, 1, rows), 0)
    own_tok = batch_of_lane == batch_id
    h_cols = []
    m_out = jnp.zeros((rows, LANES), F32)
    for hd in range(C_HEADS):
        hs = slice(hd * C_HEAD_DIM, (hd + 1) * C_HEAD_DIM)
        i_c = st_col[:, hd:hd + 1]
        cum_c = st_col[:, C_HEADS + hd:C_HEADS + hd + 1]
        tot_c = tot_all[:, C_HEADS + hd:C_HEADS + hd + 1]
        i_r = st_row[hd:hd + 1, :]
        cum_r = st_row[C_HEADS + hd:C_HEADS + hd + 1, :]
        tot_r = tot_row[C_HEADS + hd:C_HEADS + hd + 1, :]
        m_prev = m0[:, hd:hd + 1]
        dmat = jnp.where(causal_b, cum_c - cum_r + i_r, NEG)
        m_inter = cum_c + m_prev
        m_t = jnp.maximum(m_inter, jnp.max(dmat, axis=-1, keepdims=True))
        q_h = qall[:, hs]
        k_h = kall_c[:, hs]
        v_h = vall_c[:, hs]
        a = jnp.exp(dmat - m_t) * _dot_nt(q_h, k_h.astype(BF16))
        w_inter = jnp.exp(m_inter - m_t)
        c_prev = c0_ref[:, hd]
        n_tok = jnp.broadcast_to(n0_ref[hd][:, None, :], (nb, t, C_HEAD_DIM)).reshape(rows, C_HEAD_DIM)
        inter = lax.dot_general(q_h.reshape(nb, t, C_HEAD_DIM), c_prev.astype(BF16),
                                (((2,), (1,)), ((0,), (0,))), preferred_element_type=F32)
        num = _dot(a.astype(BF16), v_h) + w_inter * inter.reshape(rows, C_HEAD_DIM)
        den = (jnp.sum(a, axis=-1, keepdims=True)
               + w_inter * jnp.sum(q_h.astype(F32) * n_tok, axis=-1, keepdims=True))
        hh = num / jnp.maximum(jnp.abs(den), jnp.exp(-m_t))
        h_cols.append(_rms(hh))
        g_r = tot_r - cum_r + i_r
        g_c = tot_c - cum_c + i_c
        m_new = jnp.maximum(tot_c + m_prev,
                            jnp.max(jnp.where(same_b, g_r, NEG), axis=-1, keepdims=True))
        kw = jnp.exp(g_c - m_new) * k_h
        decay = jnp.exp(tot_c + m_prev - m_new)
        kwt = kw.T
        lhs = jnp.where(own_tok, kwt[None], 0.0).astype(BF16).reshape(nb * C_HEAD_DIM, rows)
        upd = _dot(lhs, v_h).reshape(nb, C_HEAD_DIM, C_HEAD_DIM)
        dec_b = jnp.broadcast_to(decay, (rows, C_HEAD_DIM)).reshape(nb, t, C_HEAD_DIM)[:, 0:1, :]
        c1_ref[:, hd] = dec_b * c_prev + upd
        n1_ref[hd] = decay * n_tok + _dot(same_b_bf, kw.astype(BF16))
        m_out = jnp.where(lane_t == hd, m_new, m_out)
    m1_ref[...] = m_out
    hn = jnp.concatenate(h_cols, axis=1) * hg_ref[...]
    y_ref[:, A_WIDTH + B_WIDTH:Y_W] = (hn * gate_o).astype(BF16)


def _sample_mix_call(l, za, zb, zc, kc, vc, cs, c0, n0t, m0tok, lw, nbatch):
    nb = SAMPLE_NB
    t = SUBLANES
    rows = nb * t
    tok = lambda i: (i, 0)
    const2 = lambda i: (0, 0)
    const3 = lambda i: (0, 0, 0)
    b3 = lambda i: (i, 0, 0)
    lb4 = lambda i: (l, i, 0, 0)
    return pl.pallas_call(
        _sample_mix_kernel,
        grid=(nbatch // nb,),
        in_specs=[
            pl.BlockSpec(memory_space=pltpu.SMEM),
            pl.BlockSpec((rows, ZA_W), tok),
            pl.BlockSpec((rows, ZB_W), tok),
            pl.BlockSpec((rows, ZC_W), tok),
            pl.BlockSpec((None, nb, WINDOW, B_KV_WIDTH), lb4),
            pl.BlockSpec((None, nb, WINDOW, B_KV_WIDTH), lb4),
            pl.BlockSpec((None, nb, C_CONV - 1, 2 * C_WIDTH), lb4),
            pl.BlockSpec((None, nb, C_HEADS, C_HEAD_DIM, C_HEAD_DIM), lambda i: (l, i, 0, 0, 0)),
            pl.BlockSpec((None, C_HEADS, nb, C_HEAD_DIM), lambda i: (l, 0, i, 0)),
            pl.BlockSpec((None, rows, LANES), lambda i: (l, i, 0)),
            pl.BlockSpec((1, A_WIDTH), const2),
            pl.BlockSpec((A_GROUPS, rows, rows), const3),
            pl.BlockSpec((rows, LANES), const2),
            pl.BlockSpec((1, B_WIDTH), const2),
            pl.BlockSpec((1, B_KV_WIDTH), const2),
            pl.BlockSpec((C_CONV, 2 * C_WIDTH), const2),
            pl.BlockSpec((1, 2 * C_WIDTH), const2),
            pl.BlockSpec((1, LANES), const2),
            pl.BlockSpec((1, C_WIDTH), const2),
        ],
        out_specs=[
            pl.BlockSpec((rows, Y_W), tok),
            pl.BlockSpec((rows, A_WIDTH), tok),
            pl.BlockSpec((nb, WINDOW, B_KV_WIDTH), b3),
            pl.BlockSpec((nb, WINDOW, B_KV_WIDTH), b3),
            pl.BlockSpec((nb, C_CONV - 1, 2 * C_WIDTH), b3),
            pl.BlockSpec((nb, C_HEADS, C_HEAD_DIM, C_HEAD_DIM), lambda i: (i, 0, 0, 0)),
            pl.BlockSpec((C_HEADS, rows, C_HEAD_DIM), lambda i: (0, i, 0)),
            pl.BlockSpec((rows, LANES), tok),
        ],
        out_shape=[
            jax.ShapeDtypeStruct((nbatch * t, Y_W), BF16),
            jax.ShapeDtypeStruct((nbatch * t, A_WIDTH), F32),
            jax.ShapeDtypeStruct((nbatch, WINDOW, B_KV_WIDTH), F32),
            jax.ShapeDtypeStruct((nbatch, WINDOW, B_KV_WIDTH), F32),
            jax.ShapeDtypeStruct((nbatch, C_CONV - 1, 2 * C_WIDTH), F32),
            jax.ShapeDtypeStruct((nbatch, C_HEADS, C_HEAD_DIM, C_HEAD_DIM), F32),
            jax.ShapeDtypeStruct((C_HEADS, nbatch * t, C_HEAD_DIM), F32),
            jax.ShapeDtypeStruct((nbatch * t, LANES), F32),
        ],
        scratch_shapes=[pltpu.VMEM((nb, 2 * SUBLANES, 2 * C_WIDTH), F32)],
        compiler_params=pltpu.CompilerParams(
            dimension_semantics=("arbitrary",), vmem_limit_bytes=VMEM_LIMIT),
        name="sample_mixer",
    )(lw["sinks"], za, zb, zc, kc, vc, cs, c0, n0t, m0tok, lw["vg"], lw["gwb"], lw["gbs_tok"],
      lw["qg"], lw["kg"], lw["cw"], lw["cb"], lw["fb"], lw["hg"])


def _layer_weights(l, w_in, b_in, gmlp_vnorm_g, gmlp_ws, gmlp_bs, swa_qnorm_g, swa_knorm_g,
                   swa_sinks, mlstm_conv_w, mlstm_conv_b, mlstm_f_bias, mlstm_hnorm_g,
                   w_branch_a, w_branch_b, w_branch_c, w_out, norm_g, dec_seq):
    wl, bl = w_in[l], b_in[l]
    o_ci = ZA_W + ZB_W + 3 * C_WIDTH
    o_co = o_ci + 2 * C_HEADS
    o_mg = o_co + 2 * C_WIDTH
    pad_w = LANES - 2 * C_HEADS

    def regroup(a):
        return jnp.concatenate(
            [a[..., :o_ci], a[..., o_co:o_mg], a[..., o_ci:o_co],
             jnp.zeros(a.shape[:-1] + (pad_w,), a.dtype)], axis=-1)

    t = dec_seq
    nb = SAMPLE_NB
    ws_t = gmlp_ws[l][:, :t, :t] * jnp.tril(jnp.ones((t, t), F32))
    eye = jnp.eye(nb, dtype=F32)
    gwb = jnp.einsum("bc,gts->gbtcs", eye, ws_t).reshape(A_GROUPS, nb * t, nb * t).astype(BF16)
    gbs_col = jnp.pad(gmlp_bs[l].T, ((0, 0), (0, LANES - A_GROUPS)))
    gbs_tok = jnp.pad(jnp.tile(gmlp_bs[l][:, :t].T, (nb, 1)), ((0, 0), (0, LANES - A_GROUPS)))
    fb = jnp.pad(mlstm_f_bias[l], (C_HEADS, LANES - 2 * C_HEADS)).reshape(1, LANES)
    return dict(
        ng=norm_g[l].reshape(1, D_MODEL),
        wcat=regroup(wl).astype(BF16), bcat=regroup(bl).reshape(1, ZCAT_W),
        wmg=wl[:, o_mg:].astype(BF16), bmg=bl[o_mg:].reshape(1, 3 * D_MODEL),
        wa=w_branch_a[l].astype(BF16), wb=w_branch_b[l].astype(BF16),
        wc=w_branch_c[l].astype(BF16), wo=w_out[l].astype(BF16),
        vg=gmlp_vnorm_g[l].reshape(1, A_WIDTH), gws=gmlp_ws[l], gwb=gwb,
        gbs_col=gbs_col, gbs_tok=gbs_tok,
        qg=jnp.tile(swa_qnorm_g[l], B_HEADS).reshape(1, B_WIDTH),
        kg=jnp.tile(swa_knorm_g[l], B_KV_HEADS).reshape(1, B_KV_WIDTH),
        sinks=swa_sinks[l],
        cw=mlstm_conv_w[l], cb=mlstm_conv_b[l].reshape(1, 2 * C_WIDTH), fb=fb,
        hg=mlstm_hnorm_g[l].reshape(1, C_WIDTH),
    )


def kernel(x_prompt, x_sample, cache_swa_k, cache_swa_v, state_mlstm_conv, state_mlstm_C, state_mlstm_n, state_mlstm_m, c_prompt, c_sample, ada_w, ada_b, norm_g, w_in, b_in, gmlp_vnorm_g, gmlp_ws, gmlp_bs, swa_qnorm_g, swa_knorm_g, swa_sinks, mlstm_conv_w, mlstm_conv_b, mlstm_f_bias, mlstm_hnorm_g, w_branch_a, w_branch_b, w_branch_c, w_out):
    batch, seq, _ = x_prompt.shape
    nbatch, dec_seq, _ = x_sample.shape
    assert dec_seq == SUBLANES and seq % PROMPT_TILE == 0 and nbatch % SAMPLE_NB == 0
    assert seq % PROJ_TILE == 0 and (nbatch * dec_seq) % PROJ_TILE == 0
    wb_len = cache_swa_k.shape[2]
    assert wb_len == WINDOW

    nc = batch + nbatch
    nc_pad = -(-nc // SUBLANES) * SUBLANES
    c_all = jnp.concatenate([c_prompt, c_sample, jnp.zeros((nc_pad - nc, D_MODEL), F32)], axis=0)
    mod_all = _ada_call(c_all, ada_w, ada_b)

    xp = x_prompt.reshape(batch * seq, D_MODEL)
    xs = x_sample.reshape(nbatch * dec_seq, D_MODEL)
    kc_all = cache_swa_k.reshape(DEPTH, nbatch, WINDOW, B_KV_WIDTH)
    vc_all = cache_swa_v.reshape(DEPTH, nbatch, WINDOW, B_KV_WIDTH)
    n0t_all = jnp.transpose(state_mlstm_n, (0, 2, 1, 3))
    m0tok_all = jnp.pad(jnp.repeat(state_mlstm_m, dec_seq, axis=1),
                        ((0, 0), (0, 0), (0, LANES - C_HEADS)))
    outs_p = [[] for _ in range(6)]
    outs_s = [[] for _ in range(6)]
    vrows = []
    for l in range(DEPTH):
        lw = _layer_weights(l, w_in, b_in, gmlp_vnorm_g, gmlp_ws, gmlp_bs, swa_qnorm_g,
                            swa_knorm_g, swa_sinks, mlstm_conv_w, mlstm_conv_b, mlstm_f_bias,
                            mlstm_hnorm_g, w_branch_a, w_branch_b, w_branch_c, w_out, norm_g,
                            dec_seq)
        mod_p = mod_all[l, :batch].reshape(batch, 1, 3 * D_MODEL)
        mod_s = jnp.repeat(mod_all[l, batch:nc], dec_seq, axis=0)

        xp, ko, vo, convo, c1, n1, m1 = _prompt_layer_call(xp, mod_p, lw, batch, seq)
        outs_p[0].append(ko.reshape(batch, WINDOW, B_KV_HEADS, B_HEAD_DIM))
        outs_p[1].append(vo.reshape(batch, WINDOW, B_KV_HEADS, B_HEAD_DIM))
        outs_p[2].append(convo[:, SUBLANES - (C_CONV - 1):, :])
        outs_p[3].append(c1)
        outs_p[4].append(n1)
        outs_p[5].append(m1[:, 0, :C_HEADS])

        za, zb, zc = _inproj_call(xs, mod_s, lw["ng"], lw["wcat"], lw["bcat"], None)
        y, vrow, ko, vo, convo, c1, n1tok, m1tok = _sample_mix_call(
            l, za, zb, zc, kc_all, vc_all, state_mlstm_conv, state_mlstm_C, n0t_all, m0tok_all,
            lw, nbatch)
        xs = _outproj_call(xs, mod_s, lw["ng"], y, lw["wmg"], lw["bmg"], lw["wa"], lw["wb"],
                           lw["wc"], lw["wo"], None)
        outs_s[0].append(ko.reshape(nbatch, WINDOW, B_KV_HEADS, B_HEAD_DIM))
        outs_s[1].append(vo.reshape(nbatch, WINDOW, B_KV_HEADS, B_HEAD_DIM))
        outs_s[2].append(convo)
        outs_s[3].append(c1)
        outs_s[4].append(jnp.transpose(n1tok[:, ::dec_seq, :], (1, 0, 2)))
        outs_s[5].append(m1tok[::dec_seq, :C_HEADS])
        vrows.append(vrow.reshape(nbatch, dec_seq, A_WIDTH))

    sp = [jnp.stack(o) for o in outs_p]
    ss = [jnp.stack(o) for o in outs_s]
    return (xp.reshape(batch, seq, D_MODEL), xs.reshape(nbatch, dec_seq, D_MODEL),
            sp[0], sp[1], sp[2], sp[3], sp[4], sp[5],
            ss[0], ss[1], ss[2], ss[3], ss[4], ss[5], jnp.stack(vrows))
```

```python
import functools

import numpy as np
import jax
import jax.numpy as jnp
from jax import lax
from jax.experimental import pallas as pl
from jax.experimental.pallas import tpu as pltpu

F32 = jnp.float32
BF16 = jnp.bfloat16

D_MODEL = 1024
DEPTH = 2
A_WIDTH = 512
A_GROUPS = 4
GROUP_DIM = 128
B_HEADS = 8
B_KV_HEADS = 2
B_HEAD_DIM = 64
B_WIDTH = 512
B_KV_WIDTH = 128
WINDOW = 128
C_HEADS = 4
C_HEAD_DIM = 128
C_WIDTH = 512
C_CONV = 4
EPS = 1e-6
NEG = -1e30

LANES = 128
SUBLANES = 8
VMEM_LIMIT = 56 * 1024 * 1024

ZA_W = 3 * A_WIDTH
ZB_W = 2 * B_WIDTH + 2 * B_KV_WIDTH
ZC_W = 2 * C_WIDTH + 3 * C_WIDTH + LANES
ZCAT_W = ZA_W + ZB_W + ZC_W
Y_W = A_WIDTH + B_WIDTH + C_WIDTH

PROMPT_TILE = 256
MLSTM_CHUNK = PROMPT_TILE
SAMPLE_NB = 16
PROJ_TILE = 512


def _sigmoid(x):
    return 0.5 * jnp.tanh(0.5 * x) + 0.5


def _silu(x):
    return x * _sigmoid(x)


def _log_sigmoid(x):
    return jnp.minimum(x, 0.0) - jnp.log1p(jnp.exp(-jnp.abs(x)))


def _rms(x):
    return x * lax.rsqrt(jnp.mean(x * x, axis=-1, keepdims=True) + EPS)


def _dot(a, b):
    return jnp.dot(a, b, preferred_element_type=F32)


def _dot_nt(a, b):
    return lax.dot_general(a, b, (((1,), (1,)), ((), ())), preferred_element_type=F32)


def _dot_exact01(m01, x):
    hi = x.astype(BF16)
    r1 = x - hi.astype(F32)
    mid = r1.astype(BF16)
    lo = (r1 - mid.astype(F32)).astype(BF16)
    return _dot(m01, hi) + _dot(m01, mid) + _dot(m01, lo)


def _modulated_norm(x, mod_ref, ng_ref):
    xn = _rms(x) * ng_ref[...]
    shift = mod_ref[:, 0:D_MODEL]
    scale = mod_ref[:, D_MODEL:2 * D_MODEL]
    return (xn * (1.0 + scale) + shift).astype(BF16)


def _head_rms_scale(x2, lane_lo):
    s0 = jnp.sum(jnp.where(lane_lo, x2, 0.0), axis=-1, keepdims=True)
    s1 = jnp.sum(jnp.where(lane_lo, 0.0, x2), axis=-1, keepdims=True)
    r0 = lax.rsqrt(s0 * (1.0 / B_HEAD_DIM) + EPS)
    r1 = lax.rsqrt(s1 * (1.0 / B_HEAD_DIM) + EPS)
    return jnp.where(lane_lo, r0, r1)


def _qk_norm(x, g_row):
    rows, width = x.shape
    lane_lo = lax.broadcasted_iota(jnp.int32, (rows, LANES), 1) < B_HEAD_DIM
    outs = []
    for j in range(width // LANES):
        slab = x[:, j * LANES:(j + 1) * LANES]
        outs.append(slab * _head_rms_scale(slab * slab, lane_lo))
    y = outs[0] if len(outs) == 1 else jnp.concatenate(outs, axis=1)
    return y * g_row


def _ada_kernel(c_ref, w_ref, b_ref, o_ref):
    c = c_ref[...]
    o_ref[...] = _dot(_silu(c).astype(BF16), w_ref[...].astype(BF16)) + b_ref[...]


def _ada_call(c_all, ada_w, ada_b):
    rows = c_all.shape[0]
    return pl.pallas_call(
        _ada_kernel,
        grid=(DEPTH, 3),
        in_specs=[
            pl.BlockSpec((rows, D_MODEL), lambda l, j: (0, 0)),
            pl.BlockSpec((None, D_MODEL, D_MODEL), lambda l, j: (l, 0, j)),
            pl.BlockSpec((None, 1, D_MODEL), lambda l, j: (l, 0, j)),
        ],
        out_specs=pl.BlockSpec((None, rows, D_MODEL), lambda l, j: (l, 0, j)),
        out_shape=jax.ShapeDtypeStruct((DEPTH, rows, 3 * D_MODEL), F32),
        compiler_params=pltpu.CompilerParams(
            dimension_semantics=("arbitrary", "arbitrary"), vmem_limit_bytes=VMEM_LIMIT),
        name="adaln_mod",
    )(c_all, ada_w, ada_b.reshape(DEPTH, 1, 3 * D_MODEL))


def _col_chunks(width, step):
    return [(o, min(step, width - o)) for o in range(0, width, step)]


def _inproj_pieces(get_h, w_ref, b_ref, za_ref, zb_ref, zc_ref, step):
    def piece(o_ref, off, woff, w):
        def run():
            o_ref[:, off:off + w] = _dot(get_h(), w_ref[:, woff:woff + w]) + b_ref[:, woff:woff + w]
        return run
    pieces = []
    base = 0
    for o_ref, width in ((za_ref, ZA_W), (zb_ref, ZB_W), (zc_ref, ZC_W)):
        pieces += [piece(o_ref, off, base + off, w) for off, w in _col_chunks(width, step)]
        base += width
    return pieces


def _inproj_kernel(x_ref, mod_ref, ng_ref, w_ref, b_ref, za_ref, zb_ref, zc_ref):
    h = _modulated_norm(x_ref[...], mod_ref, ng_ref)
    for piece in _inproj_pieces(lambda: h, w_ref, b_ref, za_ref, zb_ref, zc_ref, 512):
        piece()


def _mod_spec(tm, tokens_per_batch):
    if tokens_per_batch is None:
        return pl.BlockSpec((tm, 3 * D_MODEL), lambda i: (i, 0))
    tiles_per_batch = tokens_per_batch // tm
    return pl.BlockSpec((None, 1, 3 * D_MODEL), lambda i: (i // tiles_per_batch, 0, 0))


def _inproj_call(x2, mod, ng, wcat, bcat, tokens_per_batch):
    ntok = x2.shape[0]
    tm = PROJ_TILE
    const = lambda i: (0, 0)
    return pl.pallas_call(
        _inproj_kernel,
        grid=(ntok // tm,),
        in_specs=[
            pl.BlockSpec((tm, D_MODEL), lambda i: (i, 0)),
            _mod_spec(tm, tokens_per_batch),
            pl.BlockSpec((1, D_MODEL), const),
            pl.BlockSpec((D_MODEL, ZCAT_W), const, pipeline_mode=pl.Buffered(1)),
            pl.BlockSpec((1, ZCAT_W), const),
        ],
        out_specs=[
            pl.BlockSpec((tm, ZA_W), lambda i: (i, 0)),
            pl.BlockSpec((tm, ZB_W), lambda i: (i, 0)),
            pl.BlockSpec((tm, ZC_W), lambda i: (i, 0)),
        ],
        out_shape=[
            jax.ShapeDtypeStruct((ntok, ZA_W), F32),
            jax.ShapeDtypeStruct((ntok, ZB_W), F32),
            jax.ShapeDtypeStruct((ntok, ZC_W), F32),
        ],
        compiler_params=pltpu.CompilerParams(
            dimension_semantics=("arbitrary",), vmem_limit_bytes=VMEM_LIMIT),
        name="in_projection",
    )(x2, mod, ng, wcat, bcat)


def _outproj_kernel(x_ref, mod_ref, ng_ref, y_ref, wmg_ref, bmg_ref, wa_ref, wb_ref, wc_ref,
                    wo_ref, o_ref):
    x = x_ref[...]
    h = _modulated_norm(x, mod_ref, ng_ref)
    merged = None
    for i, wbr_ref in enumerate((wa_ref, wb_ref, wc_ref)):
        cols = slice(i * D_MODEL, (i + 1) * D_MODEL)
        gate = _sigmoid(_dot(h, wmg_ref[:, cols]) + bmg_ref[:, cols])
        term = gate * _dot(y_ref[:, i * A_WIDTH:(i + 1) * A_WIDTH], wbr_ref[...])
        merged = term if merged is None else merged + term
    ada_gate = mod_ref[:, 2 * D_MODEL:3 * D_MODEL]
    o_ref[...] = x + ada_gate * _dot(merged.astype(BF16), wo_ref[...])


def _outproj_call(x2, mod, ng, y, wmg, bmg, wa, wb, wc, wo, tokens_per_batch):
    ntok = x2.shape[0]
    tm = PROJ_TILE
    const = lambda i: (0, 0)
    once = pl.Buffered(1)
    return pl.pallas_call(
        _outproj_kernel,
        grid=(ntok // tm,),
        in_specs=[
            pl.BlockSpec((tm, D_MODEL), lambda i: (i, 0)),
            _mod_spec(tm, tokens_per_batch),
            pl.BlockSpec((1, D_MODEL), const),
            pl.BlockSpec((tm, Y_W), lambda i: (i, 0)),
            pl.BlockSpec((D_MODEL, 3 * D_MODEL), const, pipeline_mode=once),
            pl.BlockSpec((1, 3 * D_MODEL), const),
            pl.BlockSpec((A_WIDTH, D_MODEL), const, pipeline_mode=once),
            pl.BlockSpec((B_WIDTH, D_MODEL), const, pipeline_mode=once),
            pl.BlockSpec((C_WIDTH, D_MODEL), const, pipeline_mode=once),
            pl.BlockSpec((D_MODEL, D_MODEL), const, pipeline_mode=once),
        ],
        out_specs=pl.BlockSpec((tm, D_MODEL), lambda i: (i, 0)),
        out_shape=jax.ShapeDtypeStruct((ntok, D_MODEL), F32),
        compiler_params=pltpu.CompilerParams(
            dimension_semantics=("arbitrary",), vmem_limit_bytes=VMEM_LIMIT),
        name="out_projection",
    )(x2, mod, ng, y, wmg, bmg, wa, wb, wc, wo)


def _gmlp_gate(za_ref, vg_ref):
    u = za_ref[:, 0:A_WIDTH]
    vn = _rms(za_ref[:, A_WIDTH:2 * A_WIDTH]) * vg_ref[...]
    sg = _silu(za_ref[:, 2 * A_WIDTH:3 * A_WIDTH])
    return u, vn, sg


def _place_q_head(qn, h, rows):
    lane = lax.broadcasted_iota(jnp.int32, (rows, LANES), 1)
    slab = qn[:, (h // 2) * LANES:(h // 2 + 1) * LANES]
    src_hi = h % 2
    dst_hi = h // (B_HEADS // B_KV_HEADS)
    keep = (lane >= B_HEAD_DIM) if src_hi else (lane < B_HEAD_DIM)
    slab = jnp.where(keep, slab, 0.0)
    if src_hi != dst_hi:
        slab = pltpu.roll(slab, B_HEAD_DIM, 1)
    return slab


def _merge_head_pair(o_even, o_odd, h_even, rows):
    lane_lo = lax.broadcasted_iota(jnp.int32, (rows, LANES), 1) < B_HEAD_DIM
    kv_hi = h_even // (B_HEADS // B_KV_HEADS)
    if kv_hi:
        o_even = pltpu.roll(o_even, B_HEAD_DIM, 1)
    else:
        o_odd = pltpu.roll(o_odd, B_HEAD_DIM, 1)
    return jnp.where(lane_lo, o_even, o_odd)


def _conv_taps(xbuf_window, cw_ref, cb_ref):
    y = cb_ref[...]
    for j in range(C_CONV):
        y = y + cw_ref[j:j + 1, :] * xbuf_window(j)
    return y


def _prompt_mix_kernel(sink_ref, za_ref, zb_ref, zc_ref, vg_ref, gw_ref, gbs_ref, qg_ref, kg_ref,
                       cw_ref, cb_ref, fb_ref, hg_ref, tril_ref, band_ref, tri01_ref, tribias_ref,
                       y_ref, ko_ref, vo_ref, convo_ref, c_ref, n_ref, m_ref,
                       kprev, vprev, xbuf, first_tile, pump):
    ts = PROMPT_TILE
    pump()

    u, vn, sg = _gmlp_gate(za_ref, vg_ref)
    vnb = vn.astype(BF16)
    wts = [(gw_ref[gi] * tril_ref[...]).astype(BF16) for gi in range(A_GROUPS)]
    s_rows = []
    for c in range(ts // WINDOW):
        s_cols = []
        for gi in range(A_GROUPS):
            vblk = vnb[c * WINDOW:(c + 1) * WINDOW, gi * GROUP_DIM:(gi + 1) * GROUP_DIM]
            s_cols.append(_dot(wts[gi], vblk) + gbs_ref[:, gi:gi + 1])
        s_rows.append(jnp.concatenate(s_cols, axis=1))
    s = jnp.concatenate(s_rows, axis=0)
    pump()
    y_ref[:, 0:A_WIDTH] = (u * s * sg).astype(BF16)
    pump()

    qn = _qk_norm(zb_ref[:, 0:B_WIDTH], qg_ref[...]) * (B_HEAD_DIM ** -0.5)
    pump()
    kn = _qk_norm(zb_ref[:, B_WIDTH:B_WIDTH + B_KV_WIDTH], kg_ref[...])
    vv = zb_ref[:, B_WIDTH + B_KV_WIDTH:B_WIDTH + 2 * B_KV_WIDTH]
    sgb = _silu(zb_ref[:, B_WIDTH + 2 * B_KV_WIDTH:ZB_W])
    pump()
    grp = B_HEADS // B_KV_HEADS
    nblk = ts // WINDOW
    lane_lo2 = lax.broadcasted_iota(jnp.int32, (2 * WINDOW, LANES), 1) < B_HEAD_DIM
    kblocks = [kprev[...]] + [kn[b * WINDOW:(b + 1) * WINDOW] for b in range(nblk)]
    vblocks = [vprev[...]] + [vv[b * WINDOW:(b + 1) * WINDOW] for b in range(nblk)]
    bias0 = band_ref[0] if first_tile is False else jnp.where(first_tile, band_ref[1], band_ref[0])
    bias = [bias0] + [band_ref[0]] * (nblk - 1)
    combos = [(blk, kh) for blk in range(nblk) for kh in range(B_KV_HEADS)]
    heads = [(blk, kh, g) for blk, kh in combos for g in range(grp)]
    kdup, vdup = {}, {}
    for blk in range(nblk):
        kcat = jnp.concatenate([kblocks[blk], kblocks[blk + 1]], axis=0)
        vcat = jnp.concatenate([vblocks[blk], vblocks[blk + 1]], axis=0)
        krol = pltpu.roll(kcat, B_HEAD_DIM, 1)
        vrol = pltpu.roll(vcat, B_HEAD_DIM, 1)
        for kh in range(B_KV_HEADS):
            own = lane_lo2 if kh == 0 else jnp.logical_not(lane_lo2)
            kdup[blk, kh] = jnp.where(own, kcat, krol).astype(BF16)
            vdup[blk, kh] = jnp.where(own, vcat, vrol).astype(BF16)
    pump()
    qs = {(blk, kh): jnp.concatenate(
        [_place_q_head(qn[blk * WINDOW:(blk + 1) * WINDOW], kh * grp + g, WINDOW) for g in range(grp)],
        axis=0).astype(BF16) for blk, kh in combos}
    pump()
    logits = {c: _dot_nt(qs[c], kdup[c]) for c in combos}
    pump()
    snk = {k: sink_ref[k[1] * grp + k[2]] for k in heads}
    lg = {(blk, kh, g): logits[blk, kh][g * WINDOW:(g + 1) * WINDOW] + bias[blk]
          for blk, kh, g in heads}
    pump()
    mx = {k: jnp.maximum(jnp.max(lg[k], axis=-1, keepdims=True), snk[k]) for k in heads}
    pump()
    p = {k: jnp.exp(lg[k] - mx[k]) for k in heads}
    pump()
    rden = {k: 1.0 / (jnp.sum(p[k], axis=-1, keepdims=True) + jnp.exp(snk[k] - mx[k])) for k in heads}
    pump()
    pv = {c: _dot(jnp.concatenate([p[c + (g,)].astype(BF16) for g in range(grp)], axis=0), vdup[c])
          for c in combos}
    pump()
    outs = {(blk, kh, g): pv[blk, kh][g * WINDOW:(g + 1) * WINDOW] * rden[blk, kh, g]
            for blk, kh, g in heads}
    pump()
    yb = jnp.concatenate([jnp.concatenate(
        [_merge_head_pair(outs[blk, (2 * j) // grp, (2 * j) % grp],
                          outs[blk, (2 * j + 1) // grp, (2 * j + 1) % grp], 2 * j, WINDOW)
         for j in range(B_HEADS // 2)], axis=1) for blk in range(nblk)], axis=0)
    y_ref[:, A_WIDTH:A_WIDTH + B_WIDTH] = (yb * sgb).astype(BF16)
    pump()
    kprev[...] = kblocks[nblk]
    vprev[...] = vblocks[nblk]
    ko_ref[...] = kblocks[nblk]
    vo_ref[...] = vblocks[nblk]
    pump()

    xbuf[SUBLANES:SUBLANES + ts, :] = zc_ref[:, 0:2 * C_WIDTH]
    qk = _silu(_conv_taps(
        lambda j: xbuf[SUBLANES - (C_CONV - 1) + j:SUBLANES - (C_CONV - 1) + j + ts, :],
        cw_ref, cb_ref))
    pump()
    tail = xbuf[ts:ts + SUBLANES, :]
    xbuf[0:SUBLANES, :] = tail
    convo_ref[...] = tail
    qall = qk[:, 0:C_WIDTH].astype(BF16)
    kall = qk[:, C_WIDTH:2 * C_WIDTH] * (C_HEAD_DIM ** -0.5)
    vall = zc_ref[:, 2 * C_WIDTH:3 * C_WIDTH].astype(BF16)
    gate_o = _sigmoid(zc_ref[:, 3 * C_WIDTH:4 * C_WIDTH]) * _silu(zc_ref[:, 4 * C_WIDTH:5 * C_WIDTH])
    pump()
    ifp = zc_ref[:, 5 * C_WIDTH:5 * C_WIDTH + LANES]
    lf = _log_sigmoid(ifp + fb_ref[...])
    pump()
    cl = MLSTM_CHUNK
    hds = range(C_HEADS)
    lane_c = lax.broadcasted_iota(jnp.int32, (cl, LANES), 1)
    lane_1 = lax.broadcasted_iota(jnp.int32, (1, LANES), 1)
    m_row = m_ref[...]
    cum_all = _dot_exact01(tri01_ref[...], lf)
    st_col = jnp.where(lane_c < C_HEADS, ifp, cum_all)
    st_row = st_col.T
    pump()
    hs = [slice(hd * C_HEAD_DIM, (hd + 1) * C_HEAD_DIM) for hd in hds]
    i_c = [st_col[:, hd:hd + 1] for hd in hds]
    cum_c = [st_col[:, C_HEADS + hd:C_HEADS + hd + 1] for hd in hds]
    i_r = [st_row[hd:hd + 1, :] for hd in hds]
    cum_r = [st_row[C_HEADS + hd:C_HEADS + hd + 1, :] for hd in hds]
    m_prev = [m_row[:, hd:hd + 1] for hd in hds]
    tribias = tribias_ref[...]
    dmat = [cum_c[hd] - cum_r[hd] + i_r[hd] + tribias for hd in hds]
    pump()
    m_inter = [cum_c[hd] + m_prev[hd] for hd in hds]
    m_t = [jnp.maximum(m_inter[hd], jnp.max(dmat[hd], axis=-1, keepdims=True)) for hd in hds]
    pump()
    q_h = [qall[:, hs[hd]] for hd in hds]
    k_h = [kall[:, hs[hd]] for hd in hds]
    v_h = [vall[:, hs[hd]] for hd in hds]
    s_qk = [_dot_nt(q_h[hd], k_h[hd].astype(BF16)) for hd in hds]
    pump()
    a = [jnp.exp(dmat[hd] - m_t[hd]) * s_qk[hd] for hd in hds]
    pump()
    w_inter = [jnp.exp(m_inter[hd] - m_t[hd]) for hd in hds]
    c_prev = [c_ref[hd] for hd in hds]
    n_prev = [n_ref[hd:hd + 1, :] for hd in hds]
    inter = [_dot(q_h[hd], c_prev[hd].astype(BF16)) for hd in hds]
    pump()
    intra = [_dot(a[hd].astype(BF16), v_h[hd]) for hd in hds]
    pump()
    den = [jnp.sum(a[hd], axis=-1, keepdims=True)
           + w_inter[hd] * jnp.sum(q_h[hd].astype(F32) * n_prev[hd], axis=-1, keepdims=True)
           for hd in hds]
    pump()
    rnorm = [1.0 / jnp.maximum(jnp.abs(den[hd]), jnp.exp(-m_t[hd])) for hd in hds]
    hh = [(intra[hd] + w_inter[hd] * inter[hd]) * rnorm[hd] for hd in hds]
    pump()
    hn = jnp.concatenate([_rms(hh[hd]) for hd in hds], axis=1) * hg_ref[...]
    y_ref[:, A_WIDTH + B_WIDTH:Y_W] = (hn * gate_o).astype(BF16)
    pump()
    total = [cum_r[hd][:, cl - 1:cl] for hd in hds]
    g_r = [total[hd] - cum_r[hd] + i_r[hd] for hd in hds]
    g_c = [total[hd] - cum_c[hd] + i_c[hd] for hd in hds]
    m_new = [jnp.maximum(total[hd] + m_prev[hd], jnp.max(g_r[hd], axis=-1, keepdims=True))
             for hd in hds]
    pump()
    kw = [jnp.exp(g_c[hd] - m_new[hd]) * k_h[hd] for hd in hds]
    decay = [jnp.exp(total[hd] + m_prev[hd] - m_new[hd]) for hd in hds]
    pump()
    upd = [_dot(kw[hd].T.astype(BF16), v_h[hd]) for hd in hds]
    pump()
    for hd in hds:
        c_ref[hd] = decay[hd] * c_prev[hd] + upd[hd]
        n_ref[hd:hd + 1, :] = decay[hd] * n_prev[hd] + jnp.sum(kw[hd], axis=0, keepdims=True)
        m_row = jnp.where(lane_1 == hd, m_new[hd], m_row)
    m_ref[...] = m_row


def _prompt_mask_constants():
    r = np.arange(WINDOW)[:, None]
    c = np.arange(2 * WINDOW)[None, :]
    band = (c > r) & (c <= r + WINDOW)
    band_first = band & (c >= WINDOW)
    band_bias = np.where(np.stack([band, band_first]), 0.0, NEG).astype(np.float32)
    tril = (np.arange(WINDOW)[:, None] >= np.arange(WINDOW)[None, :]).astype(np.float32)
    tri = np.arange(MLSTM_CHUNK)[:, None] >= np.arange(MLSTM_CHUNK)[None, :]
    return (jnp.asarray(tril), jnp.asarray(band_bias), jnp.asarray(tri, dtype=BF16),
            jnp.asarray(np.where(tri, 0.0, NEG).astype(np.float32)))


N_MIX_PARAMS = 13
MIX_PHASE_WEIGHTS = (3300, 512, 1100, 800, 700, 600, 100, 512, 1030, 1024, 2050, 320, 512, 600,
                     100, 5100, 1700, 300, 300, 1040, 640, 100, 768, 516, 128, 772, 768, 1000,
                     160, 150, 100, 290)
MIX_SITE_AFTER_B = 14
MXU_PIECE_COLS = 256


class _Interleaver:
    def __init__(self, pieces, weights, late=None):
        self._pieces = list(pieces)
        self._total = len(self._pieces)
        self._cum = np.cumsum(np.asarray(weights, np.float64)) / float(np.sum(weights))
        self._late = dict(late or {})
        self._call = 0

    def __call__(self):
        target = int(round(self._cum[self._call] * self._total))
        while self._total - len(self._pieces) < target:
            self._pieces.pop(0)()
        for piece in self._late.pop(self._call, ()):
            piece()
        self._call += 1

    def emitted_by(self, call):
        return int(round(self._cum[call] * self._total))

    def finish(self):
        assert self._call == len(self._cum) and not self._pieces and not self._late


def _gate_pieces(h_ref, wmg_ref, bmg_ref, g_ref):
    def piece(off):
        cols = slice(off, off + MXU_PIECE_COLS)
        def run():
            g_ref[:, cols] = _sigmoid(_dot(h_ref[...], wmg_ref[:, cols]) + bmg_ref[:, cols])
        return run
    return [piece(off) for off in range(0, 3 * D_MODEL, MXU_PIECE_COLS)]


def _gated_branch(i, g_ref, y_ref, wbr_ref):
    return (g_ref[:, i * D_MODEL:(i + 1) * D_MODEL]
            * _dot(y_ref[:, i * A_WIDTH:(i + 1) * A_WIDTH], wbr_ref[...]))


def _merge_ab_piece(g_ref, y_ref, wa_ref, wb_ref, mab_ref):
    def run():
        mab_ref[...] = _gated_branch(0, g_ref, y_ref, wa_ref) + _gated_branch(1, g_ref, y_ref, wb_ref)
    return run


def _merge_c_and_project(x, mod_ref, g_ref, y_ref, mab_ref, wc_ref, wo_ref):
    merged = mab_ref[...] + _gated_branch(2, g_ref, y_ref, wc_ref)
    ada_gate = mod_ref[:, 2 * D_MODEL:3 * D_MODEL]
    return x + ada_gate * _dot(merged.astype(BF16), wo_ref[...])


def _prompt_layer_kernel(tiles_per_seq, sink_ref, x2_ref, xn_ref, mod_ref, modn_ref, ng_ref,
                         wcat_ref, bcat_ref, *rest):
    mix_params = rest[:N_MIX_PARAMS]
    wmg_ref, bmg_ref, wa_ref, wb_ref, wc_ref, wo_ref = rest[N_MIX_PARAMS:N_MIX_PARAMS + 6]
    o_ref, ko_ref, vo_ref, convo_ref, c_ref, n_ref, m_ref = rest[N_MIX_PARAMS + 6:N_MIX_PARAMS + 13]
    (za0, zb0, zc0, za1, zb1, zc1, h0, h1, y_scr, g_scr, mab_scr,
     kprev, vprev, xbuf) = rest[N_MIX_PARAMS + 13:]
    ts = PROMPT_TILE
    z = ((za0, zb0, zc0), (za1, zb1, zc1))
    h = (h0, h1)
    k = pl.program_id(0)
    seq_start = (k % (tiles_per_seq // 2)) == 0

    @pl.when(k == 0)
    def _():
        h0[...] = _modulated_norm(x2_ref[0:ts, :], mod_ref, ng_ref)
        for piece in _inproj_pieces(lambda: h0[...], wcat_ref, bcat_ref, *z[0], 512):
            piece()

    @pl.when(seq_start)
    def _():
        kprev[...] = jnp.zeros_like(kprev)
        vprev[...] = jnp.zeros_like(vprev)
        xbuf[0:SUBLANES, :] = jnp.zeros((SUBLANES, 2 * C_WIDTH), F32)
        c_ref[...] = jnp.zeros_like(c_ref)
        n_ref[...] = jnp.zeros_like(n_ref)
        m_ref[...] = jnp.zeros_like(m_ref)

    for half in range(2):
        cur, nxt = half, 1 - half
        rows = slice(half * ts, (half + 1) * ts)
        if half == 0:
            h[nxt][...] = _modulated_norm(x2_ref[ts:2 * ts, :], mod_ref, ng_ref)
        else:
            h[nxt][...] = _modulated_norm(xn_ref[...], modn_ref, ng_ref)
        get_h_next = functools.partial(lambda r: r[...], h[nxt])
        gate_pieces = _gate_pieces(h[cur], wmg_ref, bmg_ref, g_scr)
        pump = _Interleaver(
            gate_pieces + _inproj_pieces(get_h_next, wcat_ref, bcat_ref, *z[nxt], MXU_PIECE_COLS),
            MIX_PHASE_WEIGHTS,
            late={MIX_SITE_AFTER_B: [_merge_ab_piece(g_scr, y_scr, wa_ref, wb_ref, mab_scr)]})
        assert pump.emitted_by(MIX_SITE_AFTER_B) >= len(gate_pieces)
        _prompt_mix_kernel(sink_ref, *z[cur], *mix_params,
                           y_scr, ko_ref, vo_ref, convo_ref, c_ref, n_ref, m_ref, kprev, vprev, xbuf,
                           first_tile=seq_start if half == 0 else False, pump=pump)
        pump.finish()
        o_ref[rows, :] = _merge_c_and_project(x2_ref[rows, :], mod_ref, g_scr, y_scr, mab_scr,
                                              wc_ref, wo_ref)


def _prompt_layer_call(x2, mod, lw, batch, seq):
    ts = PROMPT_TILE
    nt = seq // ts
    assert nt % 2 == 0
    last_tile = batch * nt - 1
    const2 = lambda k: (0, 0)
    const3 = lambda k: (0, 0, 0)
    per_b3 = lambda k: ((2 * k) // nt, 0, 0)
    next_tile = lambda k: jnp.minimum(2 * k + 2, last_tile)
    once = pl.Buffered(1)
    return pl.pallas_call(
        functools.partial(_prompt_layer_kernel, nt),
        grid=(batch * nt // 2,),
        in_specs=[
            pl.BlockSpec(memory_space=pltpu.SMEM),
            pl.BlockSpec((2 * ts, D_MODEL), lambda k: (k, 0)),
            pl.BlockSpec((ts, D_MODEL), lambda k: (next_tile(k), 0)),
            pl.BlockSpec((None, 1, 3 * D_MODEL), per_b3),
            pl.BlockSpec((None, 1, 3 * D_MODEL), lambda k: (next_tile(k) // nt, 0, 0)),
            pl.BlockSpec((1, D_MODEL), const2),
            pl.BlockSpec((D_MODEL, ZCAT_W), const2, pipeline_mode=once),
            pl.BlockSpec((1, ZCAT_W), const2),
            pl.BlockSpec((1, A_WIDTH), const2),
            pl.BlockSpec((A_GROUPS, WINDOW, WINDOW), const3),
            pl.BlockSpec((WINDOW, LANES), const2),
            pl.BlockSpec((1, B_WIDTH), const2),
            pl.BlockSpec((1, B_KV_WIDTH), const2),
            pl.BlockSpec((C_CONV, 2 * C_WIDTH), const2),
            pl.BlockSpec((1, 2 * C_WIDTH), const2),
            pl.BlockSpec((1, LANES), const2),
            pl.BlockSpec((1, C_WIDTH), const2),
            pl.BlockSpec((WINDOW, WINDOW), const2),
            pl.BlockSpec((2, WINDOW, 2 * WINDOW), const3),
            pl.BlockSpec((MLSTM_CHUNK, MLSTM_CHUNK), const2),
            pl.BlockSpec((MLSTM_CHUNK, MLSTM_CHUNK), const2),
            pl.BlockSpec((D_MODEL, 3 * D_MODEL), const2, pipeline_mode=once),
            pl.BlockSpec((1, 3 * D_MODEL), const2),
            pl.BlockSpec((A_WIDTH, D_MODEL), const2, pipeline_mode=once),
            pl.BlockSpec((B_WIDTH, D_MODEL), const2, pipeline_mode=once),
            pl.BlockSpec((C_WIDTH, D_MODEL), const2, pipeline_mode=once),
            pl.BlockSpec((D_MODEL, D_MODEL), const2, pipeline_mode=once),
        ],
        out_specs=[
            pl.BlockSpec((2 * ts, D_MODEL), lambda k: (k, 0)),
            pl.BlockSpec((None, WINDOW, B_KV_WIDTH), per_b3),
            pl.BlockSpec((None, WINDOW, B_KV_WIDTH), per_b3),
            pl.BlockSpec((None, SUBLANES, 2 * C_WIDTH), per_b3),
            pl.BlockSpec((None, C_HEADS, C_HEAD_DIM, C_HEAD_DIM), lambda k: ((2 * k) // nt, 0, 0, 0)),
            pl.BlockSpec((None, C_HEADS, C_HEAD_DIM), per_b3),
            pl.BlockSpec((None, 1, LANES), per_b3),
        ],
        out_shape=[
            jax.ShapeDtypeStruct((batch * seq, D_MODEL), F32),
            jax.ShapeDtypeStruct((batch, WINDOW, B_KV_WIDTH), F32),
            jax.ShapeDtypeStruct((batch, WINDOW, B_KV_WIDTH), F32),
            jax.ShapeDtypeStruct((batch, SUBLANES, 2 * C_WIDTH), F32),
            jax.ShapeDtypeStruct((batch, C_HEADS, C_HEAD_DIM, C_HEAD_DIM), F32),
            jax.ShapeDtypeStruct((batch, C_HEADS, C_HEAD_DIM), F32),
            jax.ShapeDtypeStruct((batch, 1, LANES), F32),
        ],
        scratch_shapes=(
            [pltpu.VMEM((ts, w), F32) for w in (ZA_W, ZB_W, ZC_W)] * 2
            + [pltpu.VMEM((ts, D_MODEL), BF16)] * 2
            + [pltpu.VMEM((ts, Y_W), BF16),
               pltpu.VMEM((ts, 3 * D_MODEL), F32),
               pltpu.VMEM((ts, D_MODEL), F32),
               pltpu.VMEM((WINDOW, B_KV_WIDTH), F32),
               pltpu.VMEM((WINDOW, B_KV_WIDTH), F32),
               pltpu.VMEM((ts + SUBLANES, 2 * C_WIDTH), F32)]),
        compiler_params=pltpu.CompilerParams(
            dimension_semantics=("arbitrary",), vmem_limit_bytes=VMEM_LIMIT),
        name="prompt_layer",
    )(lw["sinks"], x2, x2, mod, mod, lw["ng"], lw["wcat"], lw["bcat"],
      lw["vg"], lw["gws"], lw["gbs_col"], lw["qg"], lw["kg"], lw["cw"], lw["cb"], lw["fb"], lw["hg"],
      *_prompt_mask_constants(),
      lw["wmg"], lw["bmg"], lw["wa"], lw["wb"], lw["wc"], lw["wo"])


def _sample_mix_kernel(sink_ref, za_ref, zb_ref, zc_ref, kc_ref, vc_ref, cs_ref, c0_ref, n0_ref,
                       m0_ref, vg_ref, gwb_ref, gbs_ref, qg_ref, kg_ref, cw_ref, cb_ref, fb_ref,
                       hg_ref,
                       y_ref, vrow_ref, ko_ref, vo_ref, convo_ref, c1_ref, n1_ref, m1_ref,
                       xbuf):
    nb = SAMPLE_NB
    t = SUBLANES
    rows = nb * t
    tok_r = lax.broadcasted_iota(jnp.int32, (rows, rows), 0)
    tok_c = lax.broadcasted_iota(jnp.int32, (rows, rows), 1)
    same_b = (tok_r // t) == (tok_c // t)
    causal_b = same_b & (tok_c <= tok_r)

    u, vn, sg = _gmlp_gate(za_ref, vg_ref)
    vrow_ref[...] = vn
    vnb = vn.astype(BF16)
    s_cols = []
    for gi in range(A_GROUPS):
        s_cols.append(_dot(gwb_ref[gi], vnb[:, gi * GROUP_DIM:(gi + 1) * GROUP_DIM])
                      + gbs_ref[:, gi:gi + 1])
    y_ref[:, 0:A_WIDTH] = (u * jnp.concatenate(s_cols, axis=1) * sg).astype(BF16)

    qn = _qk_norm(zb_ref[:, 0:B_WIDTH], qg_ref[...]) * (B_HEAD_DIM ** -0.5)
    kn = _qk_norm(zb_ref[:, B_WIDTH:B_WIDTH + B_KV_WIDTH], kg_ref[...])
    vv = zb_ref[:, B_WIDTH + B_KV_WIDTH:B_WIDTH + 2 * B_KV_WIDTH]
    sgb = _silu(zb_ref[:, B_WIDTH + 2 * B_KV_WIDTH:ZB_W])
    kn3 = kn.reshape(nb, t, B_KV_WIDTH)
    vv3 = vv.reshape(nb, t, B_KV_WIDTH)
    kcache = kc_ref[...]
    vcache = vc_ref[...]
    pad = jnp.zeros((nb, WINDOW - t, B_KV_WIDTH), F32)
    kall = jnp.concatenate([kcache, kn3, pad], axis=1).astype(BF16)
    vall = jnp.concatenate([vcache, vv3, pad], axis=1).astype(BF16)
    qp = jnp.concatenate([_place_q_head(qn, h, rows).reshape(nb, t, LANES) for h in range(B_HEADS)],
                         axis=1).astype(BF16)
    logits = lax.dot_general(qp, kall, (((2,), (2,)), ((0,), (0,))), preferred_element_type=F32)
    qrow = lax.broadcasted_iota(jnp.int32, (nb, B_HEADS * t, 2 * WINDOW), 1)
    kcol = lax.broadcasted_iota(jnp.int32, (nb, B_HEADS * t, 2 * WINDOW), 2)
    qt = qrow % t
    valid = ((kcol < WINDOW) & (kcol > qt)) | ((kcol >= WINDOW) & ((kcol - WINDOW) <= qt))
    hrow = lax.broadcasted_iota(jnp.int32, (B_HEADS * t, 1), 0) // t
    snk = jnp.zeros((B_HEADS * t, 1), F32)
    for h in range(B_HEADS):
        snk = jnp.where(hrow == h, sink_ref[h], snk)
    lg = jnp.where(valid, logits, NEG)
    mx = jnp.maximum(jnp.max(lg, axis=-1, keepdims=True), snk[None])
    p = jnp.exp(lg - mx)
    den = jnp.sum(p, axis=-1, keepdims=True) + jnp.exp(snk[None] - mx)
    pv = lax.dot_general(p.astype(BF16), vall, (((2,), (1,)), ((0,), (0,))),
                         preferred_element_type=F32) / den
    head_out = [pv[:, h * t:(h + 1) * t, :].reshape(rows, LANES) for h in range(B_HEADS)]
    yb = jnp.concatenate(
        [_merge_head_pair(head_out[2 * j], head_out[2 * j + 1], 2 * j, rows)
         for j in range(B_HEADS // 2)], axis=1)
    y_ref[:, A_WIDTH:A_WIDTH + B_WIDTH] = (yb * sgb).astype(BF16)
    ko_ref[...] = jnp.concatenate([kcache[:, t:, :], kn3], axis=1)
    vo_ref[...] = jnp.concatenate([vcache[:, t:, :], vv3], axis=1)

    xbuf[:, SUBLANES - (C_CONV - 1):SUBLANES, :] = cs_ref[...]
    xbuf[:, SUBLANES:2 * SUBLANES, :] = zc_ref[:, 0:2 * C_WIDTH].reshape(nb, t, 2 * C_WIDTH)
    y3 = cb_ref[...][None]
    for j in range(C_CONV):
        lo = SUBLANES - (C_CONV - 1) + j
        y3 = y3 + cw_ref[j:j + 1, :][None] * xbuf[:, lo:lo + t, :]
    convo_ref[...] = xbuf[:, 2 * SUBLANES - (C_CONV - 1):2 * SUBLANES, :]
    qk = _silu(y3.reshape(rows, 2 * C_WIDTH))
    qall = qk[:, 0:C_WIDTH].astype(BF16)
    kall_c = qk[:, C_WIDTH:2 * C_WIDTH] * (C_HEAD_DIM ** -0.5)
    vall_c = zc_ref[:, 2 * C_WIDTH:3 * C_WIDTH].astype(BF16)
    gate_o = _sigmoid(zc_ref[:, 3 * C_WIDTH:4 * C_WIDTH]) * _silu(zc_ref[:, 4 * C_WIDTH:5 * C_WIDTH])
    ifp = zc_ref[:, 5 * C_WIDTH:5 * C_WIDTH + LANES]
    lf = _log_sigmoid(ifp + fb_ref[...])
    lane_t = lax.broadcasted_iota(jnp.int32, (rows, LANES), 1)
    cum_all = _dot_exact01(jnp.where(causal_b, 1.0, 0.0).astype(BF16), lf)
    tot_all = _dot_exact01(jnp.where(same_b, 1.0, 0.0).astype(BF16), lf)
    st_col = jnp.where(lane_t < C_HEADS, ifp, cum_all)
    st_row = st_col.T
    tot_row = tot_all.T
    m0 = m0_ref[...]
    same_b_bf = jnp.where(same_b, 1.0, 0.0).astype(BF16)
    batch_of_lane = lax.broadcasted_iota(jnp.int32, (nb, 1, rows), 2) // t
    batch_id = lax.broadcasted_iota(jnp.int32, (nb, 1, rows), 0)
    own_tok = batch_of_lane == batch_id
    h_cols = []
    m_out = jnp.zeros((rows, LANES), F32)
    for hd in range(C_HEADS):
        hs = slice(hd * C_HEAD_DIM, (hd + 1) * C_HEAD_DIM)
        i_c = st_col[:, hd:hd + 1]
        cum_c = st_col[:, C_HEADS + hd:C_HEADS + hd + 1]
        tot_c = tot_all[:, C_HEADS + hd:C_HEADS + hd + 1]
        i_r = st_row[hd:hd + 1, :]
        cum_r = st_row[C_HEADS + hd:C_HEADS + hd + 1, :]
        tot_r = tot_row[C_HEADS + hd:C_HEADS + hd + 1, :]
        m_prev = m0[:, hd:hd + 1]
        dmat = jnp.where(causal_b, cum_c - cum_r + i_r, NEG)
        m_inter = cum_c + m_prev
        m_t = jnp.maximum(m_inter, jnp.max(dmat, axis=-1, keepdims=True))
        q_h = qall[:, hs]
        k_h = kall_c[:, hs]
        v_h = vall_c[:, hs]
        a = jnp.exp(dmat - m_t) * _dot_nt(q_h, k_h.astype(BF16))
        w_inter = jnp.exp(m_inter - m_t)
        c_prev = c0_ref[:, hd]
        n_tok = jnp.broadcast_to(n0_ref[hd][:, None, :], (nb, t, C_HEAD_DIM)).reshape(rows, C_HEAD_DIM)
        inter = lax.dot_general(q_h.reshape(nb, t, C_HEAD_DIM), c_prev.astype(BF16),
                                (((2,), (1,)), ((0,), (0,))), preferred_element_type=F32)
        num = _dot(a.astype(BF16), v_h) + w_inter * inter.reshape(rows, C_HEAD_DIM)
        den = (jnp.sum(a, axis=-1, keepdims=True)
               + w_inter * jnp.sum(q_h.astype(F32) * n_tok, axis=-1, keepdims=True))
        hh = num / jnp.maximum(jnp.abs(den), jnp.exp(-m_t))
        h_cols.append(_rms(hh))
        g_r = tot_r - cum_r + i_r
        g_c = tot_c - cum_c + i_c
        m_new = jnp.maximum(tot_c + m_prev,
                            jnp.max(jnp.where(same_b, g_r, NEG), axis=-1, keepdims=True))
        kw = jnp.exp(g_c - m_new) * k_h
        decay = jnp.exp(tot_c + m_prev - m_new)
        kwt = kw.T
        lhs = jnp.where(own_tok, kwt[None], 0.0).astype(BF16).reshape(nb * C_HEAD_DIM, rows)
        upd = _dot(lhs, v_h).reshape(nb, C_HEAD_DIM, C_HEAD_DIM)
        dec_b = jnp.broadcast_to(decay, (rows, C_HEAD_DIM)).reshape(nb, t, C_HEAD_DIM)[:, 0:1, :]
        c1_ref[:, hd] = dec_b * c_prev + upd
        n1_ref[hd] = decay * n_tok + _dot(same_b_bf, kw.astype(BF16))
        m_out = jnp.where(lane_t == hd, m_new, m_out)
    m1_ref[...] = m_out
    hn = jnp.concatenate(h_cols, axis=1) * hg_ref[...]
    y_ref[:, A_WIDTH + B_WIDTH:Y_W] = (hn * gate_o).astype(BF16)


def _sample_mix_call(l, za, zb, zc, kc, vc, cs, c0, n0t, m0tok, lw, nbatch):
    nb = SAMPLE_NB
    t = SUBLANES
    rows = nb * t
    tok = lambda i: (i, 0)
    const2 = lambda i: (0, 0)
    const3 = lambda i: (0, 0, 0)
    b3 = lambda i: (i, 0, 0)
    lb4 = lambda i: (l, i, 0, 0)
    return pl.pallas_call(
        _sample_mix_kernel,
        grid=(nbatch // nb,),
        in_specs=[
            pl.BlockSpec(memory_space=pltpu.SMEM),
            pl.BlockSpec((rows, ZA_W), tok),
            pl.BlockSpec((rows, ZB_W), tok),
            pl.BlockSpec((rows, ZC_W), tok),
            pl.BlockSpec((None, nb, WINDOW, B_KV_WIDTH), lb4),
            pl.BlockSpec((None, nb, WINDOW, B_KV_WIDTH), lb4),
            pl.BlockSpec((None, nb, C_CONV - 1, 2 * C_WIDTH), lb4),
            pl.BlockSpec((None, nb, C_HEADS, C_HEAD_DIM, C_HEAD_DIM), lambda i: (l, i, 0, 0, 0)),
            pl.BlockSpec((None, C_HEADS, nb, C_HEAD_DIM), lambda i: (l, 0, i, 0)),
            pl.BlockSpec((None, rows, LANES), lambda i: (l, i, 0)),
            pl.BlockSpec((1, A_WIDTH), const2),
            pl.BlockSpec((A_GROUPS, rows, rows), const3),
            pl.BlockSpec((rows, LANES), const2),
            pl.BlockSpec((1, B_WIDTH), const2),
            pl.BlockSpec((1, B_KV_WIDTH), const2),
            pl.BlockSpec((C_CONV, 2 * C_WIDTH), const2),
            pl.BlockSpec((1, 2 * C_WIDTH), const2),
            pl.BlockSpec((1, LANES), const2),
            pl.BlockSpec((1, C_WIDTH), const2),
        ],
        out_specs=[
            pl.BlockSpec((rows, Y_W), tok),
            pl.BlockSpec((rows, A_WIDTH), tok),
            pl.BlockSpec((nb, WINDOW, B_KV_WIDTH), b3),
            pl.BlockSpec((nb, WINDOW, B_KV_WIDTH), b3),
            pl.BlockSpec((nb, C_CONV - 1, 2 * C_WIDTH), b3),
            pl.BlockSpec((nb, C_HEADS, C_HEAD_DIM, C_HEAD_DIM), lambda i: (i, 0, 0, 0)),
            pl.BlockSpec((C_HEADS, rows, C_HEAD_DIM), lambda i: (0, i, 0)),
            pl.BlockSpec((rows, LANES), tok),
        ],
        out_shape=[
            jax.ShapeDtypeStruct((nbatch * t, Y_W), BF16),
            jax.ShapeDtypeStruct((nbatch * t, A_WIDTH), F32),
            jax.ShapeDtypeStruct((nbatch, WINDOW, B_KV_WIDTH), F32),
            jax.ShapeDtypeStruct((nbatch, WINDOW, B_KV_WIDTH), F32),
            jax.ShapeDtypeStruct((nbatch, C_CONV - 1, 2 * C_WIDTH), F32),
            jax.ShapeDtypeStruct((nbatch, C_HEADS, C_HEAD_DIM, C_HEAD_DIM), F32),
            jax.ShapeDtypeStruct((C_HEADS, nbatch * t, C_HEAD_DIM), F32),
            jax.ShapeDtypeStruct((nbatch * t, LANES), F32),
        ],
        scratch_shapes=[pltpu.VMEM((nb, 2 * SUBLANES, 2 * C_WIDTH), F32)],
        compiler_params=pltpu.CompilerParams(
            dimension_semantics=("arbitrary",), vmem_limit_bytes=VMEM_LIMIT),
        name="sample_mixer",
    )(lw["sinks"], za, zb, zc, kc, vc, cs, c0, n0t, m0tok, lw["vg"], lw["gwb"], lw["gbs_tok"],
      lw["qg"], lw["kg"], lw["cw"], lw["cb"], lw["fb"], lw["hg"])


def _layer_weights(l, w_in, b_in, gmlp_vnorm_g, gmlp_ws, gmlp_bs, swa_qnorm_g, swa_knorm_g,
                   swa_sinks, mlstm_conv_w, mlstm_conv_b, mlstm_f_bias, mlstm_hnorm_g,
                   w_branch_a, w_branch_b, w_branch_c, w_out, norm_g, dec_seq):
    wl, bl = w_in[l], b_in[l]
    o_ci = ZA_W + ZB_W + 3 * C_WIDTH
    o_co = o_ci + 2 * C_HEADS
    o_mg = o_co + 2 * C_WIDTH
    pad_w = LANES - 2 * C_HEADS

    def regroup(a):
        return jnp.concatenate(
            [a[..., :o_ci], a[..., o_co:o_mg], a[..., o_ci:o_co],
             jnp.zeros(a.shape[:-1] + (pad_w,), a.dtype)], axis=-1)

    t = dec_seq
    nb = SAMPLE_NB
    ws_t = gmlp_ws[l][:, :t, :t] * jnp.tril(jnp.ones((t, t), F32))
    eye = jnp.eye(nb, dtype=F32)
    gwb = jnp.einsum("bc,gts->gbtcs", eye, ws_t).reshape(A_GROUPS, nb * t, nb * t).astype(BF16)
    gbs_col = jnp.pad(gmlp_bs[l].T, ((0, 0), (0, LANES - A_GROUPS)))
    gbs_tok = jnp.pad(jnp.tile(gmlp_bs[l][:, :t].T, (nb, 1)), ((0, 0), (0, LANES - A_GROUPS)))
    fb = jnp.pad(mlstm_f_bias[l], (C_HEADS, LANES - 2 * C_HEADS)).reshape(1, LANES)
    return dict(
        ng=norm_g[l].reshape(1, D_MODEL),
        wcat=regroup(wl).astype(BF16), bcat=regroup(bl).reshape(1, ZCAT_W),
        wmg=wl[:, o_mg:].astype(BF16), bmg=bl[o_mg:].reshape(1, 3 * D_MODEL),
        wa=w_branch_a[l].astype(BF16), wb=w_branch_b[l].astype(BF16),
        wc=w_branch_c[l].astype(BF16), wo=w_out[l].astype(BF16),
        vg=gmlp_vnorm_g[l].reshape(1, A_WIDTH), gws=gmlp_ws[l], gwb=gwb,
        gbs_col=gbs_col, gbs_tok=gbs_tok,
        qg=jnp.tile(swa_qnorm_g[l], B_HEADS).reshape(1, B_WIDTH),
        kg=jnp.tile(swa_knorm_g[l], B_KV_HEADS).reshape(1, B_KV_WIDTH),
        sinks=swa_sinks[l],
        cw=mlstm_conv_w[l], cb=mlstm_conv_b[l].reshape(1, 2 * C_WIDTH), fb=fb,
        hg=mlstm_hnorm_g[l].reshape(1, C_WIDTH),
    )


def kernel(x_prompt, x_sample, cache_swa_k, cache_swa_v, state_mlstm_conv, state_mlstm_C, state_mlstm_n, state_mlstm_m, c_prompt, c_sample, ada_w, ada_b, norm_g, w_in, b_in, gmlp_vnorm_g, gmlp_ws, gmlp_bs, swa_qnorm_g, swa_knorm_g, swa_sinks, mlstm_conv_w, mlstm_conv_b, mlstm_f_bias, mlstm_hnorm_g, w_branch_a, w_branch_b, w_branch_c, w_out):
    batch, seq, _ = x_prompt.shape
    nbatch, dec_seq, _ = x_sample.shape
    assert dec_seq == SUBLANES and seq % PROMPT_TILE == 0 and nbatch % SAMPLE_NB == 0
    assert seq % PROJ_TILE == 0 and (nbatch * dec_seq) % PROJ_TILE == 0
    wb_len = cache_swa_k.shape[2]
    assert wb_len == WINDOW

    nc = batch + nbatch
    nc_pad = -(-nc // SUBLANES) * SUBLANES
    c_all = jnp.concatenate([c_prompt, c_sample, jnp.zeros((nc_pad - nc, D_MODEL), F32)], axis=0)
    mod_all = _ada_call(c_all, ada_w, ada_b)

    xp = x_prompt.reshape(batch * seq, D_MODEL)
    xs = x_sample.reshape(nbatch * dec_seq, D_MODEL)
    kc_all = cache_swa_k.reshape(DEPTH, nbatch, WINDOW, B_KV_WIDTH)
    vc_all = cache_swa_v.reshape(DEPTH, nbatch, WINDOW, B_KV_WIDTH)
    n0t_all = jnp.transpose(state_mlstm_n, (0, 2, 1, 3))
    m0tok_all = jnp.pad(jnp.repeat(state_mlstm_m, dec_seq, axis=1),
                        ((0, 0), (0, 0), (0, LANES - C_HEADS)))
    outs_p = [[] for _ in range(6)]
    outs_s = [[] for _ in range(6)]
    vrows = []
    for l in range(DEPTH):
        lw = _layer_weights(l, w_in, b_in, gmlp_vnorm_g, gmlp_ws, gmlp_bs, swa_qnorm_g,
                            swa_knorm_g, swa_sinks, mlstm_conv_w, mlstm_conv_b, mlstm_f_bias,
                            mlstm_hnorm_g, w_branch_a, w_branch_b, w_branch_c, w_out, norm_g,
                            dec_seq)
        mod_p = mod_all[l, :batch].reshape(batch, 1, 3 * D_MODEL)
        mod_s = jnp.repeat(mod_all[l, batch:nc], dec_seq, axis=0)

        xp, ko, vo, convo, c1, n1, m1 = _prompt_layer_call(xp, mod_p, lw, batch, seq)
        outs_p[0].append(ko.reshape(batch, WINDOW, B_KV_HEADS, B_HEAD_DIM))
        outs_p[1].append(vo.reshape(batch, WINDOW, B_KV_HEADS, B_HEAD_DIM))
        outs_p[2].append(convo[:, SUBLANES - (C_CONV - 1):, :])
        outs_p[3].append(c1)
        outs_p[4].append(n1)
        outs_p[5].append(m1[:, 0, :C_HEADS])

        za, zb, zc = _inproj_call(xs, mod_s, lw["ng"], lw["wcat"], lw["bcat"], None)
        y, vrow, ko, vo, convo, c1, n1tok, m1tok = _sample_mix_call(
            l, za, zb, zc, kc_all, vc_all, state_mlstm_conv, state_mlstm_C, n0t_all, m0tok_all,
            lw, nbatch)
        xs = _outproj_call(xs, mod_s, lw["ng"], y, lw["wmg"], lw["bmg"], lw["wa"], lw["wb"],
                           lw["wc"], lw["wo"], None)
        outs_s[0].append(ko.reshape(nbatch, WINDOW, B_KV_HEADS, B_HEAD_DIM))
        outs_s[1].append(vo.reshape(nbatch, WINDOW, B_KV_HEADS, B_HEAD_DIM))
        outs_s[2].append(convo)
        outs_s[3].append(c1)
        outs_s[4].append(jnp.transpose(n1tok[:, ::dec_seq, :], (1, 0, 2)))
        outs_s[5].append(m1tok[::dec_seq, :C_HEADS])
        vrows.append(vrow.reshape(nbatch, dec_seq, A_WIDTH))

    sp = [jnp.stack(o) for o in outs_p]
    ss = [jnp.stack(o) for o in outs_s]
    return (xp.reshape(batch, seq, D_MODEL), xs.reshape(nbatch, dec_seq, D_MODEL),
            sp[0], sp[1], sp[2], sp[3], sp[4], sp[5],
            ss[0], ss[1], ss[2], ss[3], ss[4], ss[5], jnp.stack(vrows))
```

```python
import functools

import numpy as np
import jax
import jax.numpy as jnp
from jax import lax
from jax.experimental import pallas as pl
from jax.experimental.pallas import tpu as pltpu

F32 = jnp.float32
BF16 = jnp.bfloat16

D_MODEL = 1024
DEPTH = 2
A_WIDTH = 512
A_GROUPS = 4
GROUP_DIM = 128
B_HEADS = 8
B_KV_HEADS = 2
B_HEAD_DIM = 64
B_WIDTH = 512
B_KV_WIDTH = 128
WINDOW = 128
C_HEADS = 4
C_HEAD_DIM = 128
C_WIDTH = 512
C_CONV = 4
EPS = 1e-6
NEG = -1e30

LANES = 128
SUBLANES = 8
VMEM_LIMIT = 56 * 1024 * 1024

ZA_W = 3 * A_WIDTH
ZB_W = 2 * B_WIDTH + 2 * B_KV_WIDTH
ZC_W = 2 * C_WIDTH + 3 * C_WIDTH + LANES
ZCAT_W = ZA_W + ZB_W + ZC_W
Y_W = A_WIDTH + B_WIDTH + C_WIDTH

PROMPT_TILE = 256
MLSTM_CHUNK = PROMPT_TILE
SAMPLE_NB = 16
PROJ_TILE = 512


def _sigmoid(x):
    return 0.5 * jnp.tanh(0.5 * x) + 0.5


def _silu(x):
    return x * _sigmoid(x)


def _log_sigmoid(x):
    return jnp.minimum(x, 0.0) - jnp.log1p(jnp.exp(-jnp.abs(x)))


def _rms(x):
    return x * lax.rsqrt(jnp.mean(x * x, axis=-1, keepdims=True) + EPS)


def _dot(a, b):
    return jnp.dot(a, b, preferred_element_type=F32)


def _dot_nt(a, b):
    return lax.dot_general(a, b, (((1,), (1,)), ((), ())), preferred_element_type=F32)


def _dot_exact01(m01, x):
    hi = x.astype(BF16)
    r1 = x - hi.astype(F32)
    mid = r1.astype(BF16)
    lo = (r1 - mid.astype(F32)).astype(BF16)
    return _dot(m01, hi) + _dot(m01, mid) + _dot(m01, lo)


def _modulated_norm(x, mod_ref, ng_ref):
    xn = _rms(x) * ng_ref[...]
    shift = mod_ref[:, 0:D_MODEL]
    scale = mod_ref[:, D_MODEL:2 * D_MODEL]
    return (xn * (1.0 + scale) + shift).astype(BF16)


def _head_rms_scale(x2, lane_lo):
    s0 = jnp.sum(jnp.where(lane_lo, x2, 0.0), axis=-1, keepdims=True)
    s1 = jnp.sum(jnp.where(lane_lo, 0.0, x2), axis=-1, keepdims=True)
    r0 = lax.rsqrt(s0 * (1.0 / B_HEAD_DIM) + EPS)
    r1 = lax.rsqrt(s1 * (1.0 / B_HEAD_DIM) + EPS)
    return jnp.where(lane_lo, r0, r1)


def _qk_norm(x, g_row):
    rows, width = x.shape
    lane_lo = lax.broadcasted_iota(jnp.int32, (rows, LANES), 1) < B_HEAD_DIM
    outs = []
    for j in range(width // LANES):
        slab = x[:, j * LANES:(j + 1) * LANES]
        outs.append(slab * _head_rms_scale(slab * slab, lane_lo))
    y = outs[0] if len(outs) == 1 else jnp.concatenate(outs, axis=1)
    return y * g_row


def _ada_kernel(c_ref, w_ref, b_ref, o_ref):
    c = c_ref[...]
    o_ref[...] = _dot(_silu(c).astype(BF16), w_ref[...].astype(BF16)) + b_ref[...]


def _ada_call(c_all, ada_w, ada_b):
    rows = c_all.shape[0]
    return pl.pallas_call(
        _ada_kernel,
        grid=(DEPTH, 3),
        in_specs=[
            pl.BlockSpec((rows, D_MODEL), lambda l, j: (0, 0)),
            pl.BlockSpec((None, D_MODEL, D_MODEL), lambda l, j: (l, 0, j)),
            pl.BlockSpec((None, 1, D_MODEL), lambda l, j: (l, 0, j)),
        ],
        out_specs=pl.BlockSpec((None, rows, D_MODEL), lambda l, j: (l, 0, j)),
        out_shape=jax.ShapeDtypeStruct((DEPTH, rows, 3 * D_MODEL), F32),
        compiler_params=pltpu.CompilerParams(
            dimension_semantics=("arbitrary", "arbitrary"), vmem_limit_bytes=VMEM_LIMIT),
        name="adaln_mod",
    )(c_all, ada_w, ada_b.reshape(DEPTH, 1, 3 * D_MODEL))


def _col_chunks(width, step):
    return [(o, min(step, width - o)) for o in range(0, width, step)]


def _inproj_pieces(get_h, w_ref, b_ref, za_ref, zb_ref, zc_ref, step):
    def piece(o_ref, off, woff, w):
        def run():
            o_ref[:, off:off + w] = _dot(get_h(), w_ref[:, woff:woff + w]) + b_ref[:, woff:woff + w]
        return run
    pieces = []
    base = 0
    for o_ref, width in ((za_ref, ZA_W), (zb_ref, ZB_W), (zc_ref, ZC_W)):
        pieces += [piece(o_ref, off, base + off, w) for off, w in _col_chunks(width, step)]
        base += width
    return pieces


def _inproj_kernel(x_ref, mod_ref, ng_ref, w_ref, b_ref, za_ref, zb_ref, zc_ref):
    h = _modulated_norm(x_ref[...], mod_ref, ng_ref)
    for piece in _inproj_pieces(lambda: h, w_ref, b_ref, za_ref, zb_ref, zc_ref, 512):
        piece()


def _mod_spec(tm, tokens_per_batch):
    if tokens_per_batch is None:
        return pl.BlockSpec((tm, 3 * D_MODEL), lambda i: (i, 0))
    tiles_per_batch = tokens_per_batch // tm
    return pl.BlockSpec((None, 1, 3 * D_MODEL), lambda i: (i // tiles_per_batch, 0, 0))


def _inproj_call(x2, mod, ng, wcat, bcat, tokens_per_batch):
    ntok = x2.shape[0]
    tm = PROJ_TILE
    const = lambda i: (0, 0)
    return pl.pallas_call(
        _inproj_kernel,
        grid=(ntok // tm,),
        in_specs=[
            pl.BlockSpec((tm, D_MODEL), lambda i: (i, 0)),
            _mod_spec(tm, tokens_per_batch),
            pl.BlockSpec((1, D_MODEL), const),
            pl.BlockSpec((D_MODEL, ZCAT_W), const, pipeline_mode=pl.Buffered(1)),
            pl.BlockSpec((1, ZCAT_W), const),
        ],
        out_specs=[
            pl.BlockSpec((tm, ZA_W), lambda i: (i, 0)),
            pl.BlockSpec((tm, ZB_W), lambda i: (i, 0)),
            pl.BlockSpec((tm, ZC_W), lambda i: (i, 0)),
        ],
        out_shape=[
            jax.ShapeDtypeStruct((ntok, ZA_W), F32),
            jax.ShapeDtypeStruct((ntok, ZB_W), F32),
            jax.ShapeDtypeStruct((ntok, ZC_W), F32),
        ],
        compiler_params=pltpu.CompilerParams(
            dimension_semantics=("arbitrary",), vmem_limit_bytes=VMEM_LIMIT),
        name="in_projection",
    )(x2, mod, ng, wcat, bcat)


def _outproj_kernel(x_ref, mod_ref, ng_ref, y_ref, wmg_ref, bmg_ref, wa_ref, wb_ref, wc_ref,
                    wo_ref, o_ref):
    x = x_ref[...]
    h = _modulated_norm(x, mod_ref, ng_ref)
    merged = None
    for i, wbr_ref in enumerate((wa_ref, wb_ref, wc_ref)):
        cols = slice(i * D_MODEL, (i + 1) * D_MODEL)
        gate = _sigmoid(_dot(h, wmg_ref[:, cols]) + bmg_ref[:, cols])
        term = gate * _dot(y_ref[:, i * A_WIDTH:(i + 1) * A_WIDTH], wbr_ref[...])
        merged = term if merged is None else merged + term
    ada_gate = mod_ref[:, 2 * D_MODEL:3 * D_MODEL]
    o_ref[...] = x + ada_gate * _dot(merged.astype(BF16), wo_ref[...])


def _outproj_call(x2, mod, ng, y, wmg, bmg, wa, wb, wc, wo, tokens_per_batch):
    ntok = x2.shape[0]
    tm = PROJ_TILE
    const = lambda i: (0, 0)
    once = pl.Buffered(1)
    return pl.pallas_call(
        _outproj_kernel,
        grid=(ntok // tm,),
        in_specs=[
            pl.BlockSpec((tm, D_MODEL), lambda i: (i, 0)),
            _mod_spec(tm, tokens_per_batch),
            pl.BlockSpec((1, D_MODEL), const),
            pl.BlockSpec((tm, Y_W), lambda i: (i, 0)),
            pl.BlockSpec((D_MODEL, 3 * D_MODEL), const, pipeline_mode=once),
            pl.BlockSpec((1, 3 * D_MODEL), const),
            pl.BlockSpec((A_WIDTH, D_MODEL), const, pipeline_mode=once),
            pl.BlockSpec((B_WIDTH, D_MODEL), const, pipeline_mode=once),
            pl.BlockSpec((C_WIDTH, D_MODEL), const, pipeline_mode=once),
            pl.BlockSpec((D_MODEL, D_MODEL), const, pipeline_mode=once),
        ],
        out_specs=pl.BlockSpec((tm, D_MODEL), lambda i: (i, 0)),
        out_shape=jax.ShapeDtypeStruct((ntok, D_MODEL), F32),
        compiler_params=pltpu.CompilerParams(
            dimension_semantics=("arbitrary",), vmem_limit_bytes=VMEM_LIMIT),
        name="out_projection",
    )(x2, mod, ng, y, wmg, bmg, wa, wb, wc, wo)


def _place_q_head(qn, h, rows):
    lane = lax.broadcasted_iota(jnp.int32, (rows, LANES), 1)
    slab = qn[:, (h // 2) * LANES:(h // 2 + 1) * LANES]
    src_hi = h % 2
    dst_hi = h // (B_HEADS // B_KV_HEADS)
    keep = (lane >= B_HEAD_DIM) if src_hi else (lane < B_HEAD_DIM)
    slab = jnp.where(keep, slab, 0.0)
    if src_hi != dst_hi:
        slab = pltpu.roll(slab, B_HEAD_DIM, 1)
    return slab


def _merge_head_pair(o_even, o_odd, h_even, rows):
    lane_lo = lax.broadcasted_iota(jnp.int32, (rows, LANES), 1) < B_HEAD_DIM
    kv_hi = h_even // (B_HEADS // B_KV_HEADS)
    if kv_hi:
        o_even = pltpu.roll(o_even, B_HEAD_DIM, 1)
    else:
        o_odd = pltpu.roll(o_odd, B_HEAD_DIM, 1)
    return jnp.where(lane_lo, o_even, o_odd)


def _conv_taps(xbuf, cw_ref, cb_ref, cols, ts):
    y = cb_ref[:, cols]
    for j in range(C_CONV):
        lo = SUBLANES - (C_CONV - 1) + j
        y = y + cw_ref[j:j + 1, cols] * xbuf[lo:lo + ts, cols]
    return y


def _prompt_mix_kernel(sink_ref, za_ref, zb_ref, zc_ref, vg_ref, gw_ref, gbs_ref, qg_ref, kg_ref,
                       cw_ref, cb_ref, fb_ref, hg_ref, tril_ref, band_ref, tri01_ref, tribias_ref,
                       y_ref, ko_ref, vo_ref, convo_ref, c_ref, n_ref, m_ref,
                       kprev, vprev, xbuf, first_tile, pump):
    ts = PROMPT_TILE

    vn = _rms(za_ref[:, A_WIDTH:2 * A_WIDTH]) * vg_ref[...]
    pump()
    vnb = vn.astype(BF16)
    wts = [(gw_ref[gi] * tril_ref[...]).astype(BF16) for gi in range(A_GROUPS)]
    pump()
    s_rows = []
    for c in range(ts // WINDOW):
        s_cols = []
        for gi in range(A_GROUPS):
            vblk = vnb[c * WINDOW:(c + 1) * WINDOW, gi * GROUP_DIM:(gi + 1) * GROUP_DIM]
            s_cols.append(_dot(wts[gi], vblk) + gbs_ref[:, gi:gi + 1])
        s_rows.append(jnp.concatenate(s_cols, axis=1))
    s = jnp.concatenate(s_rows, axis=0)
    pump()
    sg = _silu(za_ref[:, 2 * A_WIDTH:3 * A_WIDTH])
    pump()
    y_ref[:, 0:A_WIDTH] = (za_ref[:, 0:A_WIDTH] * s * sg).astype(BF16)
    pump()

    qn = _qk_norm(zb_ref[:, 0:B_WIDTH], qg_ref[...]) * (B_HEAD_DIM ** -0.5)
    pump()
    kn = _qk_norm(zb_ref[:, B_WIDTH:B_WIDTH + B_KV_WIDTH], kg_ref[...])
    vv = zb_ref[:, B_WIDTH + B_KV_WIDTH:B_WIDTH + 2 * B_KV_WIDTH]
    sgb = _silu(zb_ref[:, B_WIDTH + 2 * B_KV_WIDTH:ZB_W])
    pump()
    grp = B_HEADS // B_KV_HEADS
    nblk = ts // WINDOW
    lane_lo2 = lax.broadcasted_iota(jnp.int32, (2 * WINDOW, LANES), 1) < B_HEAD_DIM
    kblocks = [kprev[...]] + [kn[b * WINDOW:(b + 1) * WINDOW] for b in range(nblk)]
    vblocks = [vprev[...]] + [vv[b * WINDOW:(b + 1) * WINDOW] for b in range(nblk)]
    bias0 = band_ref[0] if first_tile is False else jnp.where(first_tile, band_ref[1], band_ref[0])
    bias = [bias0] + [band_ref[0]] * (nblk - 1)
    combos = [(blk, kh) for blk in range(nblk) for kh in range(B_KV_HEADS)]
    heads = [(blk, kh, g) for blk, kh in combos for g in range(grp)]
    kdup, vdup = {}, {}
    for blk in range(nblk):
        kcat = jnp.concatenate([kblocks[blk], kblocks[blk + 1]], axis=0)
        vcat = jnp.concatenate([vblocks[blk], vblocks[blk + 1]], axis=0)
        krol = pltpu.roll(kcat, B_HEAD_DIM, 1)
        vrol = pltpu.roll(vcat, B_HEAD_DIM, 1)
        for kh in range(B_KV_HEADS):
            own = lane_lo2 if kh == 0 else jnp.logical_not(lane_lo2)
            kdup[blk, kh] = jnp.where(own, kcat, krol).astype(BF16)
            vdup[blk, kh] = jnp.where(own, vcat, vrol).astype(BF16)
    pump()
    qs = {(blk, kh): jnp.concatenate(
        [_place_q_head(qn[blk * WINDOW:(blk + 1) * WINDOW], kh * grp + g, WINDOW) for g in range(grp)],
        axis=0).astype(BF16) for blk, kh in combos}
    pump()
    logits = {c: _dot_nt(qs[c], kdup[c]) for c in combos}
    pump()
    snk = {k: sink_ref[k[1] * grp + k[2]] for k in heads}
    lg = {(blk, kh, g): logits[blk, kh][g * WINDOW:(g + 1) * WINDOW] + bias[blk]
          for blk, kh, g in heads}
    pump()
    mx, p, rden = {}, {}, {}
    for blk in range(nblk):
        mx.update({k: jnp.maximum(jnp.max(lg[k], axis=-1, keepdims=True), snk[k])
                   for k in heads if k[0] == blk})
        pump()
    for blk in range(nblk):
        p.update({k: jnp.exp(lg[k] - mx[k]) for k in heads if k[0] == blk})
        pump()
    for blk in range(nblk):
        rden.update({k: 1.0 / (jnp.sum(p[k], axis=-1, keepdims=True) + jnp.exp(snk[k] - mx[k]))
                     for k in heads if k[0] == blk})
        pump()
    pv = {c: _dot(jnp.concatenate([p[c + (g,)].astype(BF16) for g in range(grp)], axis=0), vdup[c])
          for c in combos}
    pump()
    outs = {(blk, kh, g): pv[blk, kh][g * WINDOW:(g + 1) * WINDOW] * rden[blk, kh, g]
            for blk, kh, g in heads}
    pump()
    yb = jnp.concatenate([jnp.concatenate(
        [_merge_head_pair(outs[blk, (2 * j) // grp, (2 * j) % grp],
                          outs[blk, (2 * j + 1) // grp, (2 * j + 1) % grp], 2 * j, WINDOW)
         for j in range(B_HEADS // 2)], axis=1) for blk in range(nblk)], axis=0)
    y_ref[:, A_WIDTH:A_WIDTH + B_WIDTH] = (yb * sgb).astype(BF16)
    pump()
    kprev[...] = kblocks[nblk]
    vprev[...] = vblocks[nblk]
    ko_ref[...] = kblocks[nblk]
    vo_ref[...] = vblocks[nblk]
    pump()

    xbuf[SUBLANES:SUBLANES + ts, :] = zc_ref[:, 0:2 * C_WIDTH]
    qk_cols = []
    for c0 in range(0, 2 * C_WIDTH, 2 * LANES):
        qk_cols.append(_silu(_conv_taps(xbuf, cw_ref, cb_ref, slice(c0, c0 + 2 * LANES), ts)))
        pump()
    tail = xbuf[ts:ts + SUBLANES, :]
    xbuf[0:SUBLANES, :] = tail
    convo_ref[...] = tail
    half_n = len(qk_cols) // 2
    qall = jnp.concatenate(qk_cols[:half_n], axis=1).astype(BF16)
    kall = jnp.concatenate(qk_cols[half_n:], axis=1) * (C_HEAD_DIM ** -0.5)
    vall = zc_ref[:, 2 * C_WIDTH:3 * C_WIDTH].astype(BF16)
    pump()
    gate_cols = []
    for c0 in range(0, C_WIDTH, 2 * LANES):
        gate_cols.append(_sigmoid(zc_ref[:, 3 * C_WIDTH + c0:3 * C_WIDTH + c0 + 2 * LANES])
                         * _silu(zc_ref[:, 4 * C_WIDTH + c0:4 * C_WIDTH + c0 + 2 * LANES]))
        pump()
    gate_o = jnp.concatenate(gate_cols, axis=1)
    ifp = zc_ref[:, 5 * C_WIDTH:5 * C_WIDTH + LANES]
    lf = _log_sigmoid(ifp + fb_ref[...])
    pump()
    cl = MLSTM_CHUNK
    hds = range(C_HEADS)
    lane_c = lax.broadcasted_iota(jnp.int32, (cl, LANES), 1)
    lane_1 = lax.broadcasted_iota(jnp.int32, (1, LANES), 1)
    m_row = m_ref[...]
    cum_all = _dot_exact01(tri01_ref[...], lf)
    st_col = jnp.where(lane_c < C_HEADS, ifp, cum_all)
    st_row = st_col.T
    pump()
    hs = [slice(hd * C_HEAD_DIM, (hd + 1) * C_HEAD_DIM) for hd in hds]
    i_c = [st_col[:, hd:hd + 1] for hd in hds]
    cum_c = [st_col[:, C_HEADS + hd:C_HEADS + hd + 1] for hd in hds]
    i_r = [st_row[hd:hd + 1, :] for hd in hds]
    cum_r = [st_row[C_HEADS + hd:C_HEADS + hd + 1, :] for hd in hds]
    m_prev = [m_row[:, hd:hd + 1] for hd in hds]
    tribias = tribias_ref[...]
    dmat = [cum_c[hd] - cum_r[hd] + i_r[hd] + tribias for hd in hds]
    pump()
    m_inter = [cum_c[hd] + m_prev[hd] for hd in hds]
    m_t = [jnp.maximum(m_inter[hd], jnp.max(dmat[hd], axis=-1, keepdims=True)) for hd in hds]
    pump()
    q_h = [qall[:, hs[hd]] for hd in hds]
    k_h = [kall[:, hs[hd]] for hd in hds]
    v_h = [vall[:, hs[hd]] for hd in hds]
    s_qk = [_dot_nt(q_h[hd], k_h[hd].astype(BF16)) for hd in hds]
    pump()
    a = [jnp.exp(dmat[hd] - m_t[hd]) * s_qk[hd] for hd in hds]
    pump()
    w_inter = [jnp.exp(m_inter[hd] - m_t[hd]) for hd in hds]
    c_prev = [c_ref[hd] for hd in hds]
    n_prev = [n_ref[hd:hd + 1, :] for hd in hds]
    inter = [_dot(q_h[hd], c_prev[hd].astype(BF16)) for hd in hds]
    pump()
    intra = [_dot(a[hd].astype(BF16), v_h[hd]) for hd in hds]
    pump()
    den = [jnp.sum(a[hd], axis=-1, keepdims=True)
           + w_inter[hd] * jnp.sum(q_h[hd].astype(F32) * n_prev[hd], axis=-1, keepdims=True)
           for hd in hds]
    pump()
    rnorm = [1.0 / jnp.maximum(jnp.abs(den[hd]), jnp.exp(-m_t[hd])) for hd in hds]
    hh = [(intra[hd] + w_inter[hd] * inter[hd]) * rnorm[hd] for hd in hds]
    pump()
    hn = jnp.concatenate([_rms(hh[hd]) for hd in hds], axis=1) * hg_ref[...]
    y_ref[:, A_WIDTH + B_WIDTH:Y_W] = (hn * gate_o).astype(BF16)
    pump()
    total = [cum_r[hd][:, cl - 1:cl] for hd in hds]
    g_r = [total[hd] - cum_r[hd] + i_r[hd] for hd in hds]
    g_c = [total[hd] - cum_c[hd] + i_c[hd] for hd in hds]
    m_new = [jnp.maximum(total[hd] + m_prev[hd], jnp.max(g_r[hd], axis=-1, keepdims=True))
             for hd in hds]
    pump()
    kw = [jnp.exp(g_c[hd] - m_new[hd]) * k_h[hd] for hd in hds]
    decay = [jnp.exp(total[hd] + m_prev[hd] - m_new[hd]) for hd in hds]
    pump()
    upd = [_dot(kw[hd].T.astype(BF16), v_h[hd]) for hd in hds]
    pump()
    for hd in hds:
        c_ref[hd] = decay[hd] * c_prev[hd] + upd[hd]
        n_ref[hd:hd + 1, :] = decay[hd] * n_prev[hd] + jnp.sum(kw[hd], axis=0, keepdims=True)
        m_row = jnp.where(lane_1 == hd, m_new[hd], m_row)
    m_ref[...] = m_row


def _prompt_mask_constants():
    r = np.arange(WINDOW)[:, None]
    c = np.arange(2 * WINDOW)[None, :]
    band = (c > r) & (c <= r + WINDOW)
    band_first = band & (c >= WINDOW)
    band_bias = np.where(np.stack([band, band_first]), 0.0, NEG).astype(np.float32)
    tril = (np.arange(WINDOW)[:, None] >= np.arange(WINDOW)[None, :]).astype(np.float32)
    tri = np.arange(MLSTM_CHUNK)[:, None] >= np.arange(MLSTM_CHUNK)[None, :]
    return (jnp.asarray(tril), jnp.asarray(band_bias), jnp.asarray(tri, dtype=BF16),
            jnp.asarray(np.where(tri, 0.0, NEG).astype(np.float32)))


N_MIX_PARAMS = 13
MIX_PUMP_CALLS = 42
MXU_PIECE_COLS = 256


class _Interleaver:
    def __init__(self, pieces, calls):
        self._pieces = list(pieces)
        self._total = len(self._pieces)
        self._calls = calls
        self._call = 0

    def __call__(self):
        self._call += 1
        target = (self._call * self._total) // self._calls
        while self._total - len(self._pieces) < target:
            self._pieces.pop(0)()

    def finish(self):
        assert self._call == self._calls and not self._pieces, (self._call, len(self._pieces))


def _gate_pieces(h_ref, wmg_ref, bmg_ref, g_ref):
    def piece(off):
        cols = slice(off, off + MXU_PIECE_COLS)
        def run():
            g_ref[:, cols] = _sigmoid(_dot(h_ref[...], wmg_ref[:, cols]) + bmg_ref[:, cols])
        return run
    return [piece(off) for off in range(0, 3 * D_MODEL, MXU_PIECE_COLS)]


def _merge_and_project(x, mod_ref, g_ref, y_ref, wa_ref, wb_ref, wc_ref, wo_ref):
    merged = None
    for i, wbr_ref in enumerate((wa_ref, wb_ref, wc_ref)):
        term = (g_ref[:, i * D_MODEL:(i + 1) * D_MODEL]
                * _dot(y_ref[:, i * A_WIDTH:(i + 1) * A_WIDTH], wbr_ref[...]))
        merged = term if merged is None else merged + term
    ada_gate = mod_ref[:, 2 * D_MODEL:3 * D_MODEL]
    return x + ada_gate * _dot(merged.astype(BF16), wo_ref[...])


def _prompt_layer_kernel(tiles_per_seq, sink_ref, x2_ref, xn_ref, mod_ref, modn_ref, ng_ref,
                         wcat_ref, bcat_ref, *rest):
    mix_params = rest[:N_MIX_PARAMS]
    wmg_ref, bmg_ref, wa_ref, wb_ref, wc_ref, wo_ref = rest[N_MIX_PARAMS:N_MIX_PARAMS + 6]
    o_ref, ko_ref, vo_ref, convo_ref, c_ref, n_ref, m_ref = rest[N_MIX_PARAMS + 6:N_MIX_PARAMS + 13]
    (za0, zb0, zc0, za1, zb1, zc1, h0, h1, y_scr, g_scr, kprev, vprev, xbuf) = rest[N_MIX_PARAMS + 13:]
    ts = PROMPT_TILE
    z = ((za0, zb0, zc0), (za1, zb1, zc1))
    h = (h0, h1)
    k = pl.program_id(0)
    seq_start = (k % (tiles_per_seq // 2)) == 0

    @pl.when(k == 0)
    def _():
        h0[...] = _modulated_norm(x2_ref[0:ts, :], mod_ref, ng_ref)
        for piece in _inproj_pieces(lambda: h0[...], wcat_ref, bcat_ref, *z[0], 512):
            piece()

    @pl.when(seq_start)
    def _():
        kprev[...] = jnp.zeros_like(kprev)
        vprev[...] = jnp.zeros_like(vprev)
        xbuf[0:SUBLANES, :] = jnp.zeros((SUBLANES, 2 * C_WIDTH), F32)
        c_ref[...] = jnp.zeros_like(c_ref)
        n_ref[...] = jnp.zeros_like(n_ref)
        m_ref[...] = jnp.zeros_like(m_ref)

    for half in range(2):
        cur, nxt = half, 1 - half
        rows = slice(half * ts, (half + 1) * ts)
        if half == 0:
            h[nxt][...] = _modulated_norm(x2_ref[ts:2 * ts, :], mod_ref, ng_ref)
        else:
            h[nxt][...] = _modulated_norm(xn_ref[...], modn_ref, ng_ref)
        get_h_next = functools.partial(lambda r: r[...], h[nxt])
        pump = _Interleaver(
            _inproj_pieces(get_h_next, wcat_ref, bcat_ref, *z[nxt], MXU_PIECE_COLS)
            + _gate_pieces(h[cur], wmg_ref, bmg_ref, g_scr), MIX_PUMP_CALLS)
        _prompt_mix_kernel(sink_ref, *z[cur], *mix_params,
                           y_scr, ko_ref, vo_ref, convo_ref, c_ref, n_ref, m_ref, kprev, vprev, xbuf,
                           first_tile=seq_start if half == 0 else False, pump=pump)
        pump.finish()
        o_ref[rows, :] = _merge_and_project(x2_ref[rows, :], mod_ref, g_scr, y_scr,
                                            wa_ref, wb_ref, wc_ref, wo_ref)


def _prompt_layer_call(x2, mod, lw, batch, seq):
    ts = PROMPT_TILE
    nt = seq // ts
    assert nt % 2 == 0
    last_tile = batch * nt - 1
    const2 = lambda k: (0, 0)
    const3 = lambda k: (0, 0, 0)
    per_b3 = lambda k: ((2 * k) // nt, 0, 0)
    next_tile = lambda k: jnp.minimum(2 * k + 2, last_tile)
    once = pl.Buffered(1)
    return pl.pallas_call(
        functools.partial(_prompt_layer_kernel, nt),
        grid=(batch * nt // 2,),
        in_specs=[
            pl.BlockSpec(memory_space=pltpu.SMEM),
            pl.BlockSpec((2 * ts, D_MODEL), lambda k: (k, 0)),
            pl.BlockSpec((ts, D_MODEL), lambda k: (next_tile(k), 0)),
            pl.BlockSpec((None, 1, 3 * D_MODEL), per_b3),
            pl.BlockSpec((None, 1, 3 * D_MODEL), lambda k: (next_tile(k) // nt, 0, 0)),
            pl.BlockSpec((1, D_MODEL), const2),
            pl.BlockSpec((D_MODEL, ZCAT_W), const2, pipeline_mode=once),
            pl.BlockSpec((1, ZCAT_W), const2),
            pl.BlockSpec((1, A_WIDTH), const2),
            pl.BlockSpec((A_GROUPS, WINDOW, WINDOW), const3),
            pl.BlockSpec((WINDOW, LANES), const2),
            pl.BlockSpec((1, B_WIDTH), const2),
            pl.BlockSpec((1, B_KV_WIDTH), const2),
            pl.BlockSpec((C_CONV, 2 * C_WIDTH), const2),
            pl.BlockSpec((1, 2 * C_WIDTH), const2),
            pl.BlockSpec((1, LANES), const2),
            pl.BlockSpec((1, C_WIDTH), const2),
            pl.BlockSpec((WINDOW, WINDOW), const2),
            pl.BlockSpec((2, WINDOW, 2 * WINDOW), const3),
            pl.BlockSpec((MLSTM_CHUNK, MLSTM_CHUNK), const2),
            pl.BlockSpec((MLSTM_CHUNK, MLSTM_CHUNK), const2),
            pl.BlockSpec((D_MODEL, 3 * D_MODEL), const2, pipeline_mode=once),
            pl.BlockSpec((1, 3 * D_MODEL), const2),
            pl.BlockSpec((A_WIDTH, D_MODEL), const2, pipeline_mode=once),
            pl.BlockSpec((B_WIDTH, D_MODEL), const2, pipeline_mode=once),
            pl.BlockSpec((C_WIDTH, D_MODEL), const2, pipeline_mode=once),
            pl.BlockSpec((D_MODEL, D_MODEL), const2, pipeline_mode=once),
        ],
        out_specs=[
            pl.BlockSpec((2 * ts, D_MODEL), lambda k: (k, 0)),
            pl.BlockSpec((None, WINDOW, B_KV_WIDTH), per_b3),
            pl.BlockSpec((None, WINDOW, B_KV_WIDTH), per_b3),
            pl.BlockSpec((None, SUBLANES, 2 * C_WIDTH), per_b3),
            pl.BlockSpec((None, C_HEADS, C_HEAD_DIM, C_HEAD_DIM), lambda k: ((2 * k) // nt, 0, 0, 0)),
            pl.BlockSpec((None, C_HEADS, C_HEAD_DIM), per_b3),
            pl.BlockSpec((None, 1, LANES), per_b3),
        ],
        out_shape=[
            jax.ShapeDtypeStruct((batch * seq, D_MODEL), F32),
            jax.ShapeDtypeStruct((batch, WINDOW, B_KV_WIDTH), F32),
            jax.ShapeDtypeStruct((batch, WINDOW, B_KV_WIDTH), F32),
            jax.ShapeDtypeStruct((batch, SUBLANES, 2 * C_WIDTH), F32),
            jax.ShapeDtypeStruct((batch, C_HEADS, C_HEAD_DIM, C_HEAD_DIM), F32),
            jax.ShapeDtypeStruct((batch, C_HEADS, C_HEAD_DIM), F32),
            jax.ShapeDtypeStruct((batch, 1, LANES), F32),
        ],
        scratch_shapes=(
            [pltpu.VMEM((ts, w), F32) for w in (ZA_W, ZB_W, ZC_W)] * 2
            + [pltpu.VMEM((ts, D_MODEL), BF16)] * 2
            + [pltpu.VMEM((ts, Y_W), BF16),
               pltpu.VMEM((ts, 3 * D_MODEL), F32),
               pltpu.VMEM((WINDOW, B_KV_WIDTH), F32),
               pltpu.VMEM((WINDOW, B_KV_WIDTH), F32),
               pltpu.VMEM((ts + SUBLANES, 2 * C_WIDTH), F32)]),
        compiler_params=pltpu.CompilerParams(
            dimension_semantics=("arbitrary",), vmem_limit_bytes=VMEM_LIMIT),
        name="prompt_layer",
    )(lw["sinks"], x2, x2, mod, mod, lw["ng"], lw["wcat"], lw["bcat"],
      lw["vg"], lw["gws"], lw["gbs_col"], lw["qg"], lw["kg"], lw["cw"], lw["cb"], lw["fb"], lw["hg"],
      *_prompt_mask_constants(),
      lw["wmg"], lw["bmg"], lw["wa"], lw["wb"], lw["wc"], lw["wo"])


def _sample_mix_kernel(sink_ref, za_ref, zb_ref, zc_ref, kc_ref, vc_ref, cs_ref, c0_ref, n0_ref,
                       m0_ref, vg_ref, gwb_ref, gbs_ref, qg_ref, kg_ref, cw_ref, cb_ref, fb_ref,
                       hg_ref,
                       y_ref, vrow_ref, ko_ref, vo_ref, convo_ref, c1_ref, n1_ref, m1_ref,
                       xbuf):
    nb = SAMPLE_NB
    t = SUBLANES
    rows = nb * t
    tok_r = lax.broadcasted_iota(jnp.int32, (rows, rows), 0)
    tok_c = lax.broadcasted_iota(jnp.int32, (rows, rows), 1)
    same_b = (tok_r // t) == (tok_c // t)
    causal_b = same_b & (tok_c <= tok_r)

    u = za_ref[:, 0:A_WIDTH]
    vn = _rms(za_ref[:, A_WIDTH:2 * A_WIDTH]) * vg_ref[...]
    sg = _silu(za_ref[:, 2 * A_WIDTH:3 * A_WIDTH])
    vrow_ref[...] = vn
    vnb = vn.astype(BF16)
    s_cols = []
    for gi in range(A_GROUPS):
        s_cols.append(_dot(gwb_ref[gi], vnb[:, gi * GROUP_DIM:(gi + 1) * GROUP_DIM])
                      + gbs_ref[:, gi:gi + 1])
    y_ref[:, 0:A_WIDTH] = (u * jnp.concatenate(s_cols, axis=1) * sg).astype(BF16)

    qn = _qk_norm(zb_ref[:, 0:B_WIDTH], qg_ref[...]) * (B_HEAD_DIM ** -0.5)
    kn = _qk_norm(zb_ref[:, B_WIDTH:B_WIDTH + B_KV_WIDTH], kg_ref[...])
    vv = zb_ref[:, B_WIDTH + B_KV_WIDTH:B_WIDTH + 2 * B_KV_WIDTH]
    sgb = _silu(zb_ref[:, B_WIDTH + 2 * B_KV_WIDTH:ZB_W])
    kn3 = kn.reshape(nb, t, B_KV_WIDTH)
    vv3 = vv.reshape(nb, t, B_KV_WIDTH)
    kcache = kc_ref[...]
    vcache = vc_ref[...]
    pad = jnp.zeros((nb, WINDOW - t, B_KV_WIDTH), F32)
    kall = jnp.concatenate([kcache, kn3, pad], axis=1).astype(BF16)
    vall = jnp.concatenate([vcache, vv3, pad], axis=1).astype(BF16)
    qp = jnp.concatenate([_place_q_head(qn, h, rows).reshape(nb, t, LANES) for h in range(B_HEADS)],
                         axis=1).astype(BF16)
    logits = lax.dot_general(qp, kall, (((2,), (2,)), ((0,), (0,))), preferred_element_type=F32)
    qrow = lax.broadcasted_iota(jnp.int32, (nb, B_HEADS * t, 2 * WINDOW), 1)
    kcol = lax.broadcasted_iota(jnp.int32, (nb, B_HEADS * t, 2 * WINDOW), 2)
    qt = qrow % t
    valid = ((kcol < WINDOW) & (kcol > qt)) | ((kcol >= WINDOW) & ((kcol - WINDOW) <= qt))
    hrow = lax.broadcasted_iota(jnp.int32, (B_HEADS * t, 1), 0) // t
    snk = jnp.zeros((B_HEADS * t, 1), F32)
    for h in range(B_HEADS):
        snk = jnp.where(hrow == h, sink_ref[h], snk)
    lg = jnp.where(valid, logits, NEG)
    mx = jnp.maximum(jnp.max(lg, axis=-1, keepdims=True), snk[None])
    p = jnp.exp(lg - mx)
    den = jnp.sum(p, axis=-1, keepdims=True) + jnp.exp(snk[None] - mx)
    pv = lax.dot_general(p.astype(BF16), vall, (((2,), (1,)), ((0,), (0,))),
                         preferred_element_type=F32) / den
    head_out = [pv[:, h * t:(h + 1) * t, :].reshape(rows, LANES) for h in range(B_HEADS)]
    yb = jnp.concatenate(
        [_merge_head_pair(head_out[2 * j], head_out[2 * j + 1], 2 * j, rows)
         for j in range(B_HEADS // 2)], axis=1)
    y_ref[:, A_WIDTH:A_WIDTH + B_WIDTH] = (yb * sgb).astype(BF16)
    ko_ref[...] = jnp.concatenate([kcache[:, t:, :], kn3], axis=1)
    vo_ref[...] = jnp.concatenate([vcache[:, t:, :], vv3], axis=1)

    xbuf[:, SUBLANES - (C_CONV - 1):SUBLANES, :] = cs_ref[...]
    xbuf[:, SUBLANES:2 * SUBLANES, :] = zc_ref[:, 0:2 * C_WIDTH].reshape(nb, t, 2 * C_WIDTH)
    y3 = cb_ref[...][None]
    for j in range(C_CONV):
        lo = SUBLANES - (C_CONV - 1) + j
        y3 = y3 + cw_ref[j:j + 1, :][None] * xbuf[:, lo:lo + t, :]
    convo_ref[...] = xbuf[:, 2 * SUBLANES - (C_CONV - 1):2 * SUBLANES, :]
    qk = _silu(y3.reshape(rows, 2 * C_WIDTH))
    qall = qk[:, 0:C_WIDTH].astype(BF16)
    kall_c = qk[:, C_WIDTH:2 * C_WIDTH] * (C_HEAD_DIM ** -0.5)
    vall_c = zc_ref[:, 2 * C_WIDTH:3 * C_WIDTH].astype(BF16)
    gate_o = _sigmoid(zc_ref[:, 3 * C_WIDTH:4 * C_WIDTH]) * _silu(zc_ref[:, 4 * C_WIDTH:5 * C_WIDTH])
    ifp = zc_ref[:, 5 * C_WIDTH:5 * C_WIDTH + LANES]
    lf = _log_sigmoid(ifp + fb_ref[...])
    lane_t = lax.broadcasted_iota(jnp.int32, (rows, LANES), 1)
    cum_all = _dot_exact01(jnp.where(causal_b, 1.0, 0.0).astype(BF16), lf)
    tot_all = _dot_exact01(jnp.where(same_b, 1.0, 0.0).astype(BF16), lf)
    st_col = jnp.where(lane_t < C_HEADS, ifp, cum_all)
    st_row = st_col.T
    tot_row = tot_all.T
    m0 = m0_ref[...]
    same_b_bf = jnp.where(same_b, 1.0, 0.0).astype(BF16)
    batch_of_lane = lax.broadcasted_iota(jnp.int32, (nb, 1, rows), 2) // t
    batch_id = lax.broadcasted_iota(jnp.int32, (nb, 1, rows), 0)
    own_tok = batch_of_lane == batch_id
    h_cols = []
    m_out = jnp.zeros((rows, LANES), F32)
    for hd in range(C_HEADS):
        hs = slice(hd * C_HEAD_DIM, (hd + 1) * C_HEAD_DIM)
        i_c = st_col[:, hd:hd + 1]
        cum_c = st_col[:, C_HEADS + hd:C_HEADS + hd + 1]
        tot_c = tot_all[:, C_HEADS + hd:C_HEADS + hd + 1]
        i_r = st_row[hd:hd + 1, :]
        cum_r = st_row[C_HEADS + hd:C_HEADS + hd + 1, :]
        tot_r = tot_row[C_HEADS + hd:C_HEADS + hd + 1, :]
        m_prev = m0[:, hd:hd + 1]
        dmat = jnp.where(causal_b, cum_c - cum_r + i_r, NEG)
        m_inter = cum_c + m_prev
        m_t = jnp.maximum(m_inter, jnp.max(dmat, axis=-1, keepdims=True))
        q_h = qall[:, hs]
        k_h = kall_c[:, hs]
        v_h = vall_c[:, hs]
        a = jnp.exp(dmat - m_t) * _dot_nt(q_h, k_h.astype(BF16))
        w_inter = jnp.exp(m_inter - m_t)
        c_prev = c0_ref[:, hd]
        n_tok = jnp.broadcast_to(n0_ref[hd][:, None, :], (nb, t, C_HEAD_DIM)).reshape(rows, C_HEAD_DIM)
        inter = lax.dot_general(q_h.reshape(nb, t, C_HEAD_DIM), c_prev.astype(BF16),
                                (((2,), (1,)), ((0,), (0,))), preferred_element_type=F32)
        num = _dot(a.astype(BF16), v_h) + w_inter * inter.reshape(rows, C_HEAD_DIM)
        den = (jnp.sum(a, axis=-1, keepdims=True)
               + w_inter * jnp.sum(q_h.astype(F32) * n_tok, axis=-1, keepdims=True))
        hh = num / jnp.maximum(jnp.abs(den), jnp.exp(-m_t))
        h_cols.append(_rms(hh))
        g_r = tot_r - cum_r + i_r
        g_c = tot_c - cum_c + i_c
        m_new = jnp.maximum(tot_c + m_prev,
                            jnp.max(jnp.where(same_b, g_r, NEG), axis=-1, keepdims=True))
        kw = jnp.exp(g_c - m_new) * k_h
        decay = jnp.exp(tot_c + m_prev - m_new)
        kwt = kw.T
        lhs = jnp.where(own_tok, kwt[None], 0.0).astype(BF16).reshape(nb * C_HEAD_DIM, rows)
        upd = _dot(lhs, v_h).reshape(nb, C_HEAD_DIM, C_HEAD_DIM)
        dec_b = jnp.broadcast_to(decay, (rows, C_HEAD_DIM)).reshape(nb, t, C_HEAD_DIM)[:, 0:1, :]
        c1_ref[:, hd] = dec_b * c_prev + upd
        n1_ref[hd] = decay * n_tok + _dot(same_b_bf, kw.astype(BF16))
        m_out = jnp.where(lane_t == hd, m_new, m_out)
    m1_ref[...] = m_out
    hn = jnp.concatenate(h_cols, axis=1) * hg_ref[...]
    y_ref[:, A_WIDTH + B_WIDTH:Y_W] = (hn * gate_o).astype(BF16)


def _sample_mix_call(l, za, zb, zc, kc, vc, cs, c0, n0t, m0tok, lw, nbatch):
    nb = SAMPLE_NB
    t = SUBLANES
    rows = nb * t
    tok = lambda i: (i, 0)
    const2 = lambda i: (0, 0)
    const3 = lambda i: (0, 0, 0)
    b3 = lambda i: (i, 0, 0)
    lb4 = lambda i: (l, i, 0, 0)
    return pl.pallas_call(
        _sample_mix_kernel,
        grid=(nbatch // nb,),
        in_specs=[
            pl.BlockSpec(memory_space=pltpu.SMEM),
            pl.BlockSpec((rows, ZA_W), tok),
            pl.BlockSpec((rows, ZB_W), tok),
            pl.BlockSpec((rows, ZC_W), tok),
            pl.BlockSpec((None, nb, WINDOW, B_KV_WIDTH), lb4),
            pl.BlockSpec((None, nb, WINDOW, B_KV_WIDTH), lb4),
            pl.BlockSpec((None, nb, C_CONV - 1, 2 * C_WIDTH), lb4),
            pl.BlockSpec((None, nb, C_HEADS, C_HEAD_DIM, C_HEAD_DIM), lambda i: (l, i, 0, 0, 0)),
            pl.BlockSpec((None, C_HEADS, nb, C_HEAD_DIM), lambda i: (l, 0, i, 0)),
            pl.BlockSpec((None, rows, LANES), lambda i: (l, i, 0)),
            pl.BlockSpec((1, A_WIDTH), const2),
            pl.BlockSpec((A_GROUPS, rows, rows), const3),
            pl.BlockSpec((rows, LANES), const2),
            pl.BlockSpec((1, B_WIDTH), const2),
            pl.BlockSpec((1, B_KV_WIDTH), const2),
            pl.BlockSpec((C_CONV, 2 * C_WIDTH), const2),
            pl.BlockSpec((1, 2 * C_WIDTH), const2),
            pl.BlockSpec((1, LANES), const2),
            pl.BlockSpec((1, C_WIDTH), const2),
        ],
        out_specs=[
            pl.BlockSpec((rows, Y_W), tok),
            pl.BlockSpec((rows, A_WIDTH), tok),
            pl.BlockSpec((nb, WINDOW, B_KV_WIDTH), b3),
            pl.BlockSpec((nb, WINDOW, B_KV_WIDTH), b3),
            pl.BlockSpec((nb, C_CONV - 1, 2 * C_WIDTH), b3),
            pl.BlockSpec((nb, C_HEADS, C_HEAD_DIM, C_HEAD_DIM), lambda i: (i, 0, 0, 0)),
            pl.BlockSpec((C_HEADS, rows, C_HEAD_DIM), lambda i: (0, i, 0)),
            pl.BlockSpec((rows, LANES), tok),
        ],
        out_shape=[
            jax.ShapeDtypeStruct((nbatch * t, Y_W), BF16),
            jax.ShapeDtypeStruct((nbatch * t, A_WIDTH), F32),
            jax.ShapeDtypeStruct((nbatch, WINDOW, B_KV_WIDTH), F32),
            jax.ShapeDtypeStruct((nbatch, WINDOW, B_KV_WIDTH), F32),
            jax.ShapeDtypeStruct((nbatch, C_CONV - 1, 2 * C_WIDTH), F32),
            jax.ShapeDtypeStruct((nbatch, C_HEADS, C_HEAD_DIM, C_HEAD_DIM), F32),
            jax.ShapeDtypeStruct((C_HEADS, nbatch * t, C_HEAD_DIM), F32),
            jax.ShapeDtypeStruct((nbatch * t, LANES), F32),
        ],
        scratch_shapes=[pltpu.VMEM((nb, 2 * SUBLANES, 2 * C_WIDTH), F32)],
        compiler_params=pltpu.CompilerParams(
            dimension_semantics=("arbitrary",), vmem_limit_bytes=VMEM_LIMIT),
        name="sample_mixer",
    )(lw["sinks"], za, zb, zc, kc, vc, cs, c0, n0t, m0tok, lw["vg"], lw["gwb"], lw["gbs_tok"],
      lw["qg"], lw["kg"], lw["cw"], lw["cb"], lw["fb"], lw["hg"])


def _layer_weights(l, w_in, b_in, gmlp_vnorm_g, gmlp_ws, gmlp_bs, swa_qnorm_g, swa_knorm_g,
                   swa_sinks, mlstm_conv_w, mlstm_conv_b, mlstm_f_bias, mlstm_hnorm_g,
                   w_branch_a, w_branch_b, w_branch_c, w_out, norm_g, dec_seq):
    wl, bl = w_in[l], b_in[l]
    o_ci = ZA_W + ZB_W + 3 * C_WIDTH
    o_co = o_ci + 2 * C_HEADS
    o_mg = o_co + 2 * C_WIDTH
    pad_w = LANES - 2 * C_HEADS

    def regroup(a):
        return jnp.concatenate(
            [a[..., :o_ci], a[..., o_co:o_mg], a[..., o_ci:o_co],
             jnp.zeros(a.shape[:-1] + (pad_w,), a.dtype)], axis=-1)

    t = dec_seq
    nb = SAMPLE_NB
    ws_t = gmlp_ws[l][:, :t, :t] * jnp.tril(jnp.ones((t, t), F32))
    eye = jnp.eye(nb, dtype=F32)
    gwb = jnp.einsum("bc,gts->gbtcs", eye, ws_t).reshape(A_GROUPS, nb * t, nb * t).astype(BF16)
    gbs_col = jnp.pad(gmlp_bs[l].T, ((0, 0), (0, LANES - A_GROUPS)))
    gbs_tok = jnp.pad(jnp.tile(gmlp_bs[l][:, :t].T, (nb, 1)), ((0, 0), (0, LANES - A_GROUPS)))
    fb = jnp.pad(mlstm_f_bias[l], (C_HEADS, LANES - 2 * C_HEADS)).reshape(1, LANES)
    return dict(
        ng=norm_g[l].reshape(1, D_MODEL),
        wcat=regroup(wl).astype(BF16), bcat=regroup(bl).reshape(1, ZCAT_W),
        wmg=wl[:, o_mg:].astype(BF16), bmg=bl[o_mg:].reshape(1, 3 * D_MODEL),
        wa=w_branch_a[l].astype(BF16), wb=w_branch_b[l].astype(BF16),
        wc=w_branch_c[l].astype(BF16), wo=w_out[l].astype(BF16),
        vg=gmlp_vnorm_g[l].reshape(1, A_WIDTH), gws=gmlp_ws[l], gwb=gwb,
        gbs_col=gbs_col, gbs_tok=gbs_tok,
        qg=jnp.tile(swa_qnorm_g[l], B_HEADS).reshape(1, B_WIDTH),
        kg=jnp.tile(swa_knorm_g[l], B_KV_HEADS).reshape(1, B_KV_WIDTH),
        sinks=swa_sinks[l],
        cw=mlstm_conv_w[l], cb=mlstm_conv_b[l].reshape(1, 2 * C_WIDTH), fb=fb,
        hg=mlstm_hnorm_g[l].reshape(1, C_WIDTH),
    )


def kernel(x_prompt, x_sample, cache_swa_k, cache_swa_v, state_mlstm_conv, state_mlstm_C, state_mlstm_n, state_mlstm_m, c_prompt, c_sample, ada_w, ada_b, norm_g, w_in, b_in, gmlp_vnorm_g, gmlp_ws, gmlp_bs, swa_qnorm_g, swa_knorm_g, swa_sinks, mlstm_conv_w, mlstm_conv_b, mlstm_f_bias, mlstm_hnorm_g, w_branch_a, w_branch_b, w_branch_c, w_out):
    batch, seq, _ = x_prompt.shape
    nbatch, dec_seq, _ = x_sample.shape
    assert dec_seq == SUBLANES and seq % PROMPT_TILE == 0 and nbatch % SAMPLE_NB == 0
    assert seq % PROJ_TILE == 0 and (nbatch * dec_seq) % PROJ_TILE == 0
    wb_len = cache_swa_k.shape[2]
    assert wb_len == WINDOW

    nc = batch + nbatch
    nc_pad = -(-nc // SUBLANES) * SUBLANES
    c_all = jnp.concatenate([c_prompt, c_sample, jnp.zeros((nc_pad - nc, D_MODEL), F32)], axis=0)
    mod_all = _ada_call(c_all, ada_w, ada_b)

    xp = x_prompt.reshape(batch * seq, D_MODEL)
    xs = x_sample.reshape(nbatch * dec_seq, D_MODEL)
    kc_all = cache_swa_k.reshape(DEPTH, nbatch, WINDOW, B_KV_WIDTH)
    vc_all = cache_swa_v.reshape(DEPTH, nbatch, WINDOW, B_KV_WIDTH)
    n0t_all = jnp.transpose(state_mlstm_n, (0, 2, 1, 3))
    m0tok_all = jnp.pad(jnp.repeat(state_mlstm_m, dec_seq, axis=1),
                        ((0, 0), (0, 0), (0, LANES - C_HEADS)))
    outs_p = [[] for _ in range(6)]
    outs_s = [[] for _ in range(6)]
    vrows = []
    for l in range(DEPTH):
        lw = _layer_weights(l, w_in, b_in, gmlp_vnorm_g, gmlp_ws, gmlp_bs, swa_qnorm_g,
                            swa_knorm_g, swa_sinks, mlstm_conv_w, mlstm_conv_b, mlstm_f_bias,
                            mlstm_hnorm_g, w_branch_a, w_branch_b, w_branch_c, w_out, norm_g,
                            dec_seq)
        mod_p = mod_all[l, :batch].reshape(batch, 1, 3 * D_MODEL)
        mod_s = jnp.repeat(mod_all[l, batch:nc], dec_seq, axis=0)

        xp, ko, vo, convo, c1, n1, m1 = _prompt_layer_call(xp, mod_p, lw, batch, seq)
        outs_p[0].append(ko.reshape(batch, WINDOW, B_KV_HEADS, B_HEAD_DIM))
        outs_p[1].append(vo.reshape(batch, WINDOW, B_KV_HEADS, B_HEAD_DIM))
        outs_p[2].append(convo[:, SUBLANES - (C_CONV - 1):, :])
        outs_p[3].append(c1)
        outs_p[4].append(n1)
        outs_p[5].append(m1[:, 0, :C_HEADS])

        za, zb, zc = _inproj_call(xs, mod_s, lw["ng"], lw["wcat"], lw["bcat"], None)
        y, vrow, ko, vo, convo, c1, n1tok, m1tok = _sample_mix_call(
            l, za, zb, zc, kc_all, vc_all, state_mlstm_conv, state_mlstm_C, n0t_all, m0tok_all,
            lw, nbatch)
        xs = _outproj_call(xs, mod_s, lw["ng"], y, lw["wmg"], lw["bmg"], lw["wa"], lw["wb"],
                           lw["wc"], lw["wo"], None)
        outs_s[0].append(ko.reshape(nbatch, WINDOW, B_KV_HEADS, B_HEAD_DIM))
        outs_s[1].append(vo.reshape(nbatch, WINDOW, B_KV_HEADS, B_HEAD_DIM))
        outs_s[2].append(convo)
        outs_s[3].append(c1)
        outs_s[4].append(jnp.transpose(n1tok[:, ::dec_seq, :], (1, 0, 2)))
        outs_s[5].append(m1tok[::dec_seq, :C_HEADS])
        vrows.append(vrow.reshape(nbatch, dec_seq, A_WIDTH))

    sp = [jnp.stack(o) for o in outs_p]
    ss = [jnp.stack(o) for o in outs_s]
    return (xp.reshape(batch, seq, D_MODEL), xs.reshape(nbatch, dec_seq, D_MODEL),
            sp[0], sp[1], sp[2], sp[3], sp[4], sp[5],
            ss[0], ss[1], ss[2], ss[3], ss[4], ss[5], jnp.stack(vrows))
```

```python
import functools

import numpy as np
import jax
import jax.numpy as jnp
from jax import lax
from jax.experimental import pallas as pl
from jax.experimental.pallas import tpu as pltpu

F32 = jnp.float32
BF16 = jnp.bfloat16

D_MODEL = 1024
DEPTH = 2
A_WIDTH = 512
A_GROUPS = 4
GROUP_DIM = 128
B_HEADS = 8
B_KV_HEADS = 2
B_HEAD_DIM = 64
B_WIDTH = 512
B_KV_WIDTH = 128
WINDOW = 128
C_HEADS = 4
C_HEAD_DIM = 128
C_WIDTH = 512
C_CONV = 4
EPS = 1e-6
NEG = -1e30

LANES = 128
SUBLANES = 8
VMEM_LIMIT = 56 * 1024 * 1024

ZA_W = 3 * A_WIDTH
ZB_W = 2 * B_WIDTH + 2 * B_KV_WIDTH
ZC_W = 2 * C_WIDTH + 3 * C_WIDTH + LANES
ZCAT_W = ZA_W + ZB_W + ZC_W
Y_W = A_WIDTH + B_WIDTH + C_WIDTH

PROMPT_TILE = 256
MLSTM_CHUNK = PROMPT_TILE
SAMPLE_NB = 16
PROJ_TILE = 512


def _sigmoid(x):
    return 0.5 * jnp.tanh(0.5 * x) + 0.5


def _silu(x):
    return x * _sigmoid(x)


def _log_sigmoid(x):
    return jnp.minimum(x, 0.0) - jnp.log1p(jnp.exp(-jnp.abs(x)))


def _rms(x):
    return x * lax.rsqrt(jnp.mean(x * x, axis=-1, keepdims=True) + EPS)


def _dot(a, b):
    return jnp.dot(a, b, preferred_element_type=F32)


def _dot_nt(a, b):
    return lax.dot_general(a, b, (((1,), (1,)), ((), ())), preferred_element_type=F32)


def _dot_exact01(m01, x):
    hi = x.astype(BF16)
    r1 = x - hi.astype(F32)
    mid = r1.astype(BF16)
    lo = (r1 - mid.astype(F32)).astype(BF16)
    return _dot(m01, hi) + _dot(m01, mid) + _dot(m01, lo)


def _modulated_norm(x, mod_ref, ng_ref):
    xn = _rms(x) * ng_ref[...]
    shift = mod_ref[:, 0:D_MODEL]
    scale = mod_ref[:, D_MODEL:2 * D_MODEL]
    return (xn * (1.0 + scale) + shift).astype(BF16)


def _head_rms_scale(x2, lane_lo):
    s0 = jnp.sum(jnp.where(lane_lo, x2, 0.0), axis=-1, keepdims=True)
    s1 = jnp.sum(jnp.where(lane_lo, 0.0, x2), axis=-1, keepdims=True)
    r0 = lax.rsqrt(s0 * (1.0 / B_HEAD_DIM) + EPS)
    r1 = lax.rsqrt(s1 * (1.0 / B_HEAD_DIM) + EPS)
    return jnp.where(lane_lo, r0, r1)


def _qk_norm(x, g_row):
    rows, width = x.shape
    lane_lo = lax.broadcasted_iota(jnp.int32, (rows, LANES), 1) < B_HEAD_DIM
    outs = []
    for j in range(width // LANES):
        slab = x[:, j * LANES:(j + 1) * LANES]
        outs.append(slab * _head_rms_scale(slab * slab, lane_lo))
    y = outs[0] if len(outs) == 1 else jnp.concatenate(outs, axis=1)
    return y * g_row


def _ada_kernel(c_ref, w_ref, b_ref, o_ref):
    c = c_ref[...]
    o_ref[...] = _dot(_silu(c).astype(BF16), w_ref[...].astype(BF16)) + b_ref[...]


def _ada_call(c_all, ada_w, ada_b):
    rows = c_all.shape[0]
    return pl.pallas_call(
        _ada_kernel,
        grid=(DEPTH, 3),
        in_specs=[
            pl.BlockSpec((rows, D_MODEL), lambda l, j: (0, 0)),
            pl.BlockSpec((None, D_MODEL, D_MODEL), lambda l, j: (l, 0, j)),
            pl.BlockSpec((None, 1, D_MODEL), lambda l, j: (l, 0, j)),
        ],
        out_specs=pl.BlockSpec((None, rows, D_MODEL), lambda l, j: (l, 0, j)),
        out_shape=jax.ShapeDtypeStruct((DEPTH, rows, 3 * D_MODEL), F32),
        compiler_params=pltpu.CompilerParams(
            dimension_semantics=("arbitrary", "arbitrary"), vmem_limit_bytes=VMEM_LIMIT),
        name="adaln_mod",
    )(c_all, ada_w, ada_b.reshape(DEPTH, 1, 3 * D_MODEL))


def _col_chunks(width, step):
    return [(o, min(step, width - o)) for o in range(0, width, step)]


def _inproj_pieces(get_h, w_ref, b_ref, za_ref, zb_ref, zc_ref, step):
    def piece(o_ref, off, woff, w):
        def run():
            o_ref[:, off:off + w] = _dot(get_h(), w_ref[:, woff:woff + w]) + b_ref[:, woff:woff + w]
        return run
    pieces = []
    base = 0
    for o_ref, width in ((za_ref, ZA_W), (zb_ref, ZB_W), (zc_ref, ZC_W)):
        pieces += [piece(o_ref, off, base + off, w) for off, w in _col_chunks(width, step)]
        base += width
    return pieces


def _inproj_kernel(x_ref, mod_ref, ng_ref, w_ref, b_ref, za_ref, zb_ref, zc_ref):
    h = _modulated_norm(x_ref[...], mod_ref, ng_ref)
    for piece in _inproj_pieces(lambda: h, w_ref, b_ref, za_ref, zb_ref, zc_ref, 512):
        piece()


def _mod_spec(tm, tokens_per_batch):
    if tokens_per_batch is None:
        return pl.BlockSpec((tm, 3 * D_MODEL), lambda i: (i, 0))
    tiles_per_batch = tokens_per_batch // tm
    return pl.BlockSpec((None, 1, 3 * D_MODEL), lambda i: (i // tiles_per_batch, 0, 0))


def _inproj_call(x2, mod, ng, wcat, bcat, tokens_per_batch):
    ntok = x2.shape[0]
    tm = PROJ_TILE
    const = lambda i: (0, 0)
    return pl.pallas_call(
        _inproj_kernel,
        grid=(ntok // tm,),
        in_specs=[
            pl.BlockSpec((tm, D_MODEL), lambda i: (i, 0)),
            _mod_spec(tm, tokens_per_batch),
            pl.BlockSpec((1, D_MODEL), const),
            pl.BlockSpec((D_MODEL, ZCAT_W), const, pipeline_mode=pl.Buffered(1)),
            pl.BlockSpec((1, ZCAT_W), const),
        ],
        out_specs=[
            pl.BlockSpec((tm, ZA_W), lambda i: (i, 0)),
            pl.BlockSpec((tm, ZB_W), lambda i: (i, 0)),
            pl.BlockSpec((tm, ZC_W), lambda i: (i, 0)),
        ],
        out_shape=[
            jax.ShapeDtypeStruct((ntok, ZA_W), F32),
            jax.ShapeDtypeStruct((ntok, ZB_W), F32),
            jax.ShapeDtypeStruct((ntok, ZC_W), F32),
        ],
        compiler_params=pltpu.CompilerParams(
            dimension_semantics=("arbitrary",), vmem_limit_bytes=VMEM_LIMIT),
        name="in_projection",
    )(x2, mod, ng, wcat, bcat)


def _outproj_kernel(x_ref, mod_ref, ng_ref, y_ref, wmg_ref, bmg_ref, wa_ref, wb_ref, wc_ref,
                    wo_ref, o_ref):
    x = x_ref[...]
    h = _modulated_norm(x, mod_ref, ng_ref)
    merged = None
    for i, wbr_ref in enumerate((wa_ref, wb_ref, wc_ref)):
        cols = slice(i * D_MODEL, (i + 1) * D_MODEL)
        gate = _sigmoid(_dot(h, wmg_ref[:, cols]) + bmg_ref[:, cols])
        term = gate * _dot(y_ref[:, i * A_WIDTH:(i + 1) * A_WIDTH], wbr_ref[...])
        merged = term if merged is None else merged + term
    ada_gate = mod_ref[:, 2 * D_MODEL:3 * D_MODEL]
    o_ref[...] = x + ada_gate * _dot(merged.astype(BF16), wo_ref[...])


def _outproj_call(x2, mod, ng, y, wmg, bmg, wa, wb, wc, wo, tokens_per_batch):
    ntok = x2.shape[0]
    tm = PROJ_TILE
    const = lambda i: (0, 0)
    once = pl.Buffered(1)
    return pl.pallas_call(
        _outproj_kernel,
        grid=(ntok // tm,),
        in_specs=[
            pl.BlockSpec((tm, D_MODEL), lambda i: (i, 0)),
            _mod_spec(tm, tokens_per_batch),
            pl.BlockSpec((1, D_MODEL), const),
            pl.BlockSpec((tm, Y_W), lambda i: (i, 0)),
            pl.BlockSpec((D_MODEL, 3 * D_MODEL), const, pipeline_mode=once),
            pl.BlockSpec((1, 3 * D_MODEL), const),
            pl.BlockSpec((A_WIDTH, D_MODEL), const, pipeline_mode=once),
            pl.BlockSpec((B_WIDTH, D_MODEL), const, pipeline_mode=once),
            pl.BlockSpec((C_WIDTH, D_MODEL), const, pipeline_mode=once),
            pl.BlockSpec((D_MODEL, D_MODEL), const, pipeline_mode=once),
        ],
        out_specs=pl.BlockSpec((tm, D_MODEL), lambda i: (i, 0)),
        out_shape=jax.ShapeDtypeStruct((ntok, D_MODEL), F32),
        compiler_params=pltpu.CompilerParams(
            dimension_semantics=("arbitrary",), vmem_limit_bytes=VMEM_LIMIT),
        name="out_projection",
    )(x2, mod, ng, y, wmg, bmg, wa, wb, wc, wo)


def _place_q_head(qn, h, rows):
    lane = lax.broadcasted_iota(jnp.int32, (rows, LANES), 1)
    slab = qn[:, (h // 2) * LANES:(h // 2 + 1) * LANES]
    src_hi = h % 2
    dst_hi = h // (B_HEADS // B_KV_HEADS)
    keep = (lane >= B_HEAD_DIM) if src_hi else (lane < B_HEAD_DIM)
    slab = jnp.where(keep, slab, 0.0)
    if src_hi != dst_hi:
        slab = pltpu.roll(slab, B_HEAD_DIM, 1)
    return slab


def _merge_head_pair(o_even, o_odd, h_even, rows):
    lane_lo = lax.broadcasted_iota(jnp.int32, (rows, LANES), 1) < B_HEAD_DIM
    kv_hi = h_even // (B_HEADS // B_KV_HEADS)
    if kv_hi:
        o_even = pltpu.roll(o_even, B_HEAD_DIM, 1)
    else:
        o_odd = pltpu.roll(o_odd, B_HEAD_DIM, 1)
    return jnp.where(lane_lo, o_even, o_odd)


def _conv_taps(xbuf, cw_ref, cb_ref, cols, ts):
    y = cb_ref[:, cols]
    for j in range(C_CONV):
        lo = SUBLANES - (C_CONV - 1) + j
        y = y + cw_ref[j:j + 1, cols] * xbuf[lo:lo + ts, cols]
    return y


def _prompt_mix_kernel(sink_ref, za_ref, zb_ref, zc_ref, vg_ref, gw_ref, gbs_ref, qg_ref, kg_ref,
                       cw_ref, cb_ref, fb_ref, hg_ref, tril_ref, band_ref, tri01_ref, tribias_ref,
                       y_ref, ko_ref, vo_ref, convo_ref, c_ref, n_ref, m_ref,
                       kprev, vprev, xbuf, first_tile, pump):
    ts = PROMPT_TILE

    u = za_ref[:, 0:A_WIDTH]
    vn = _rms(za_ref[:, A_WIDTH:2 * A_WIDTH]) * vg_ref[...]
    sg = _silu(za_ref[:, 2 * A_WIDTH:3 * A_WIDTH])
    vnb = vn.astype(BF16)
    wts = [(gw_ref[gi] * tril_ref[...]).astype(BF16) for gi in range(A_GROUPS)]
    s_rows = []
    for c in range(ts // WINDOW):
        s_cols = []
        for gi in range(A_GROUPS):
            vblk = vnb[c * WINDOW:(c + 1) * WINDOW, gi * GROUP_DIM:(gi + 1) * GROUP_DIM]
            s_cols.append(_dot(wts[gi], vblk) + gbs_ref[:, gi:gi + 1])
        s_rows.append(jnp.concatenate(s_cols, axis=1))
    s = jnp.concatenate(s_rows, axis=0)
    pump()
    y_ref[:, 0:A_WIDTH] = (u * s * sg).astype(BF16)
    pump()

    qn = _qk_norm(zb_ref[:, 0:B_WIDTH], qg_ref[...]) * (B_HEAD_DIM ** -0.5)
    pump()
    kn = _qk_norm(zb_ref[:, B_WIDTH:B_WIDTH + B_KV_WIDTH], kg_ref[...])
    vv = zb_ref[:, B_WIDTH + B_KV_WIDTH:B_WIDTH + 2 * B_KV_WIDTH]
    sgb = _silu(zb_ref[:, B_WIDTH + 2 * B_KV_WIDTH:ZB_W])
    pump()
    grp = B_HEADS // B_KV_HEADS
    nblk = ts // WINDOW
    lane_lo2 = lax.broadcasted_iota(jnp.int32, (2 * WINDOW, LANES), 1) < B_HEAD_DIM
    kblocks = [kprev[...]] + [kn[b * WINDOW:(b + 1) * WINDOW] for b in range(nblk)]
    vblocks = [vprev[...]] + [vv[b * WINDOW:(b + 1) * WINDOW] for b in range(nblk)]
    bias0 = band_ref[0] if first_tile is False else jnp.where(first_tile, band_ref[1], band_ref[0])
    bias = [bias0] + [band_ref[0]] * (nblk - 1)
    combos = [(blk, kh) for blk in range(nblk) for kh in range(B_KV_HEADS)]
    heads = [(blk, kh, g) for blk, kh in combos for g in range(grp)]
    kdup, vdup = {}, {}
    for blk in range(nblk):
        kcat = jnp.concatenate([kblocks[blk], kblocks[blk + 1]], axis=0)
        vcat = jnp.concatenate([vblocks[blk], vblocks[blk + 1]], axis=0)
        krol = pltpu.roll(kcat, B_HEAD_DIM, 1)
        vrol = pltpu.roll(vcat, B_HEAD_DIM, 1)
        for kh in range(B_KV_HEADS):
            own = lane_lo2 if kh == 0 else jnp.logical_not(lane_lo2)
            kdup[blk, kh] = jnp.where(own, kcat, krol).astype(BF16)
            vdup[blk, kh] = jnp.where(own, vcat, vrol).astype(BF16)
    pump()
    qs = {(blk, kh): jnp.concatenate(
        [_place_q_head(qn[blk * WINDOW:(blk + 1) * WINDOW], kh * grp + g, WINDOW) for g in range(grp)],
        axis=0).astype(BF16) for blk, kh in combos}
    pump()
    logits = {c: _dot_nt(qs[c], kdup[c]) for c in combos}
    pump()
    snk = {k: sink_ref[k[1] * grp + k[2]] for k in heads}
    lg = {(blk, kh, g): logits[blk, kh][g * WINDOW:(g + 1) * WINDOW] + bias[blk]
          for blk, kh, g in heads}
    pump()
    mx = {k: jnp.maximum(jnp.max(lg[k], axis=-1, keepdims=True), snk[k]) for k in heads}
    pump()
    p = {k: jnp.exp(lg[k] - mx[k]) for k in heads}
    pump()
    rden = {k: 1.0 / (jnp.sum(p[k], axis=-1, keepdims=True) + jnp.exp(snk[k] - mx[k])) for k in heads}
    pump()
    pv = {c: _dot(jnp.concatenate([p[c + (g,)].astype(BF16) for g in range(grp)], axis=0), vdup[c])
          for c in combos}
    pump()
    outs = {(blk, kh, g): pv[blk, kh][g * WINDOW:(g + 1) * WINDOW] * rden[blk, kh, g]
            for blk, kh, g in heads}
    pump()
    yb = jnp.concatenate([jnp.concatenate(
        [_merge_head_pair(outs[blk, (2 * j) // grp, (2 * j) % grp],
                          outs[blk, (2 * j + 1) // grp, (2 * j + 1) % grp], 2 * j, WINDOW)
         for j in range(B_HEADS // 2)], axis=1) for blk in range(nblk)], axis=0)
    y_ref[:, A_WIDTH:A_WIDTH + B_WIDTH] = (yb * sgb).astype(BF16)
    pump()
    kprev[...] = kblocks[nblk]
    vprev[...] = vblocks[nblk]
    ko_ref[...] = kblocks[nblk]
    vo_ref[...] = vblocks[nblk]
    pump()

    xbuf[SUBLANES:SUBLANES + ts, :] = zc_ref[:, 0:2 * C_WIDTH]
    qk = _silu(_conv_taps(xbuf, cw_ref, cb_ref, slice(0, 2 * C_WIDTH), ts))
    pump()
    tail = xbuf[ts:ts + SUBLANES, :]
    xbuf[0:SUBLANES, :] = tail
    convo_ref[...] = tail
    qall = qk[:, 0:C_WIDTH].astype(BF16)
    kall = qk[:, C_WIDTH:2 * C_WIDTH] * (C_HEAD_DIM ** -0.5)
    vall = zc_ref[:, 2 * C_WIDTH:3 * C_WIDTH].astype(BF16)
    gate_o = _sigmoid(zc_ref[:, 3 * C_WIDTH:4 * C_WIDTH]) * _silu(zc_ref[:, 4 * C_WIDTH:5 * C_WIDTH])
    pump()
    ifp = zc_ref[:, 5 * C_WIDTH:5 * C_WIDTH + LANES]
    lf = _log_sigmoid(ifp + fb_ref[...])
    pump()
    cl = MLSTM_CHUNK
    hds = range(C_HEADS)
    lane_c = lax.broadcasted_iota(jnp.int32, (cl, LANES), 1)
    lane_1 = lax.broadcasted_iota(jnp.int32, (1, LANES), 1)
    m_row = m_ref[...]
    cum_all = _dot_exact01(tri01_ref[...], lf)
    st_col = jnp.where(lane_c < C_HEADS, ifp, cum_all)
    st_row = st_col.T
    pump()
    hs = [slice(hd * C_HEAD_DIM, (hd + 1) * C_HEAD_DIM) for hd in hds]
    i_c = [st_col[:, hd:hd + 1] for hd in hds]
    cum_c = [st_col[:, C_HEADS + hd:C_HEADS + hd + 1] for hd in hds]
    i_r = [st_row[hd:hd + 1, :] for hd in hds]
    cum_r = [st_row[C_HEADS + hd:C_HEADS + hd + 1, :] for hd in hds]
    m_prev = [m_row[:, hd:hd + 1] for hd in hds]
    tribias = tribias_ref[...]
    dmat = [cum_c[hd] - cum_r[hd] + i_r[hd] + tribias for hd in hds]
    pump()
    m_inter = [cum_c[hd] + m_prev[hd] for hd in hds]
    m_t = [jnp.maximum(m_inter[hd], jnp.max(dmat[hd], axis=-1, keepdims=True)) for hd in hds]
    pump()
    q_h = [qall[:, hs[hd]] for hd in hds]
    k_h = [kall[:, hs[hd]] for hd in hds]
    v_h = [vall[:, hs[hd]] for hd in hds]
    s_qk = [_dot_nt(q_h[hd], k_h[hd].astype(BF16)) for hd in hds]
    pump()
    a = [jnp.exp(dmat[hd] - m_t[hd]) * s_qk[hd] for hd in hds]
    pump()
    w_inter = [jnp.exp(m_inter[hd] - m_t[hd]) for hd in hds]
    c_prev = [c_ref[hd] for hd in hds]
    n_prev = [n_ref[hd:hd + 1, :] for hd in hds]
    inter = [_dot(q_h[hd], c_prev[hd].astype(BF16)) for hd in hds]
    pump()
    intra = [_dot(a[hd].astype(BF16), v_h[hd]) for hd in hds]
    pump()
    den = [jnp.sum(a[hd], axis=-1, keepdims=True)
           + w_inter[hd] * jnp.sum(q_h[hd].astype(F32) * n_prev[hd], axis=-1, keepdims=True)
           for hd in hds]
    pump()
    rnorm = [1.0 / jnp.maximum(jnp.abs(den[hd]), jnp.exp(-m_t[hd])) for hd in hds]
    hh = [(intra[hd] + w_inter[hd] * inter[hd]) * rnorm[hd] for hd in hds]
    pump()
    hn = jnp.concatenate([_rms(hh[hd]) for hd in hds], axis=1) * hg_ref[...]
    y_ref[:, A_WIDTH + B_WIDTH:Y_W] = (hn * gate_o).astype(BF16)
    pump()
    total = [cum_r[hd][:, cl - 1:cl] for hd in hds]
    g_r = [total[hd] - cum_r[hd] + i_r[hd] for hd in hds]
    g_c = [total[hd] - cum_c[hd] + i_c[hd] for hd in hds]
    m_new = [jnp.maximum(total[hd] + m_prev[hd], jnp.max(g_r[hd], axis=-1, keepdims=True))
             for hd in hds]
    pump()
    kw = [jnp.exp(g_c[hd] - m_new[hd]) * k_h[hd] for hd in hds]
    decay = [jnp.exp(total[hd] + m_prev[hd] - m_new[hd]) for hd in hds]
    pump()
    upd = [_dot(kw[hd].T.astype(BF16), v_h[hd]) for hd in hds]
    pump()
    for hd in hds:
        c_ref[hd] = decay[hd] * c_prev[hd] + upd[hd]
        n_ref[hd:hd + 1, :] = decay[hd] * n_prev[hd] + jnp.sum(kw[hd], axis=0, keepdims=True)
        m_row = jnp.where(lane_1 == hd, m_new[hd], m_row)
    m_ref[...] = m_row


def _prompt_mask_constants():
    r = np.arange(WINDOW)[:, None]
    c = np.arange(2 * WINDOW)[None, :]
    band = (c > r) & (c <= r + WINDOW)
    band_first = band & (c >= WINDOW)
    band_bias = np.where(np.stack([band, band_first]), 0.0, NEG).astype(np.float32)
    tril = (np.arange(WINDOW)[:, None] >= np.arange(WINDOW)[None, :]).astype(np.float32)
    tri = np.arange(MLSTM_CHUNK)[:, None] >= np.arange(MLSTM_CHUNK)[None, :]
    return (jnp.asarray(tril), jnp.asarray(band_bias), jnp.asarray(tri, dtype=BF16),
            jnp.asarray(np.where(tri, 0.0, NEG).astype(np.float32)))


N_MIX_PARAMS = 13
MIX_PUMP_CALLS = 31
TAIL_FILL_PIECES = 8
MXU_PIECE_COLS = 256


class _Interleaver:
    def __init__(self, pieces, calls, hold_back=0):
        self._pieces = list(pieces)
        self._hold_back = hold_back
        self._spread = len(self._pieces) - hold_back
        self._emitted = 0
        self._calls = calls
        self._call = 0

    def __call__(self):
        self._call += 1
        target = (self._call * self._spread) // self._calls
        while self._emitted < target:
            self._pieces.pop(0)()
            self._emitted += 1

    def finish(self):
        assert self._call == self._calls and len(self._pieces) == self._hold_back, self._call
        return self._pieces


def _gate_pieces(h_ref, wmg_ref, bmg_ref, g_ref):
    def piece(off):
        cols = slice(off, off + MXU_PIECE_COLS)
        def run():
            g_ref[:, cols] = _sigmoid(_dot(h_ref[...], wmg_ref[:, cols]) + bmg_ref[:, cols])
        return run
    return [piece(off) for off in range(0, 3 * D_MODEL, MXU_PIECE_COLS)]


def _merge_and_project(x, mod_ref, g_ref, y_ref, wa_ref, wb_ref, wc_ref, wo_ref, fillers=()):
    fillers = list(fillers)
    per_stage = -(-len(fillers) // 4)
    merged = None
    for i, wbr_ref in enumerate((wa_ref, wb_ref, wc_ref)):
        for piece in fillers[i * per_stage:(i + 1) * per_stage]:
            piece()
        term = (g_ref[:, i * D_MODEL:(i + 1) * D_MODEL]
                * _dot(y_ref[:, i * A_WIDTH:(i + 1) * A_WIDTH], wbr_ref[...]))
        merged = term if merged is None else merged + term
    for piece in fillers[3 * per_stage:]:
        piece()
    ada_gate = mod_ref[:, 2 * D_MODEL:3 * D_MODEL]
    return x + ada_gate * _dot(merged.astype(BF16), wo_ref[...])


def _prompt_layer_kernel(tiles_per_seq, sink_ref, x2_ref, xn_ref, mod_ref, modn_ref, ng_ref,
                         wcat_ref, bcat_ref, *rest):
    mix_params = rest[:N_MIX_PARAMS]
    wmg_ref, bmg_ref, wa_ref, wb_ref, wc_ref, wo_ref = rest[N_MIX_PARAMS:N_MIX_PARAMS + 6]
    o_ref, ko_ref, vo_ref, convo_ref, c_ref, n_ref, m_ref = rest[N_MIX_PARAMS + 6:N_MIX_PARAMS + 13]
    (za0, zb0, zc0, za1, zb1, zc1, h0, h1, y_scr, g_scr, kprev, vprev, xbuf) = rest[N_MIX_PARAMS + 13:]
    ts = PROMPT_TILE
    z = ((za0, zb0, zc0), (za1, zb1, zc1))
    h = (h0, h1)
    k = pl.program_id(0)
    seq_start = (k % (tiles_per_seq // 2)) == 0

    @pl.when(k == 0)
    def _():
        h0[...] = _modulated_norm(x2_ref[0:ts, :], mod_ref, ng_ref)
        for piece in _inproj_pieces(lambda: h0[...], wcat_ref, bcat_ref, *z[0], 512):
            piece()

    @pl.when(seq_start)
    def _():
        kprev[...] = jnp.zeros_like(kprev)
        vprev[...] = jnp.zeros_like(vprev)
        xbuf[0:SUBLANES, :] = jnp.zeros((SUBLANES, 2 * C_WIDTH), F32)
        c_ref[...] = jnp.zeros_like(c_ref)
        n_ref[...] = jnp.zeros_like(n_ref)
        m_ref[...] = jnp.zeros_like(m_ref)

    for half in range(2):
        cur, nxt = half, 1 - half
        rows = slice(half * ts, (half + 1) * ts)
        if half == 0:
            h[nxt][...] = _modulated_norm(x2_ref[ts:2 * ts, :], mod_ref, ng_ref)
        else:
            h[nxt][...] = _modulated_norm(xn_ref[...], modn_ref, ng_ref)
        get_h_next = functools.partial(lambda r: r[...], h[nxt])
        hold = TAIL_FILL_PIECES if half == 1 else 0
        proj = _inproj_pieces(get_h_next, wcat_ref, bcat_ref, *z[nxt], MXU_PIECE_COLS)
        pump = _Interleaver(
            proj[:len(proj) - hold] + _gate_pieces(h[cur], wmg_ref, bmg_ref, g_scr)
            + proj[len(proj) - hold:], MIX_PUMP_CALLS, hold_back=hold)
        _prompt_mix_kernel(sink_ref, *z[cur], *mix_params,
                           y_scr, ko_ref, vo_ref, convo_ref, c_ref, n_ref, m_ref, kprev, vprev, xbuf,
                           first_tile=seq_start if half == 0 else False, pump=pump)
        o_ref[rows, :] = _merge_and_project(x2_ref[rows, :], mod_ref, g_scr, y_scr,
                                            wa_ref, wb_ref, wc_ref, wo_ref, fillers=pump.finish())


def _prompt_layer_call(x2, mod, lw, batch, seq):
    ts = PROMPT_TILE
    nt = seq // ts
    assert nt % 2 == 0
    last_tile = batch * nt - 1
    const2 = lambda k: (0, 0)
    const3 = lambda k: (0, 0, 0)
    per_b3 = lambda k: ((2 * k) // nt, 0, 0)
    next_tile = lambda k: jnp.minimum(2 * k + 2, last_tile)
    once = pl.Buffered(1)
    return pl.pallas_call(
        functools.partial(_prompt_layer_kernel, nt),
        grid=(batch * nt // 2,),
        in_specs=[
            pl.BlockSpec(memory_space=pltpu.SMEM),
            pl.BlockSpec((2 * ts, D_MODEL), lambda k: (k, 0)),
            pl.BlockSpec((ts, D_MODEL), lambda k: (next_tile(k), 0)),
            pl.BlockSpec((None, 1, 3 * D_MODEL), per_b3),
            pl.BlockSpec((None, 1, 3 * D_MODEL), lambda k: (next_tile(k) // nt, 0, 0)),
            pl.BlockSpec((1, D_MODEL), const2),
            pl.BlockSpec((D_MODEL, ZCAT_W), const2, pipeline_mode=once),
            pl.BlockSpec((1, ZCAT_W), const2),
            pl.BlockSpec((1, A_WIDTH), const2),
            pl.BlockSpec((A_GROUPS, WINDOW, WINDOW), const3),
            pl.BlockSpec((WINDOW, LANES), const2),
            pl.BlockSpec((1, B_WIDTH), const2),
            pl.BlockSpec((1, B_KV_WIDTH), const2),
            pl.BlockSpec((C_CONV, 2 * C_WIDTH), const2),
            pl.BlockSpec((1, 2 * C_WIDTH), const2),
            pl.BlockSpec((1, LANES), const2),
            pl.BlockSpec((1, C_WIDTH), const2),
            pl.BlockSpec((WINDOW, WINDOW), const2),
            pl.BlockSpec((2, WINDOW, 2 * WINDOW), const3),
            pl.BlockSpec((MLSTM_CHUNK, MLSTM_CHUNK), const2),
            pl.BlockSpec((MLSTM_CHUNK, MLSTM_CHUNK), const2),
            pl.BlockSpec((D_MODEL, 3 * D_MODEL), const2, pipeline_mode=once),
            pl.BlockSpec((1, 3 * D_MODEL), const2),
            pl.BlockSpec((A_WIDTH, D_MODEL), const2, pipeline_mode=once),
            pl.BlockSpec((B_WIDTH, D_MODEL), const2, pipeline_mode=once),
            pl.BlockSpec((C_WIDTH, D_MODEL), const2, pipeline_mode=once),
            pl.BlockSpec((D_MODEL, D_MODEL), const2, pipeline_mode=once),
        ],
        out_specs=[
            pl.BlockSpec((2 * ts, D_MODEL), lambda k: (k, 0)),
            pl.BlockSpec((None, WINDOW, B_KV_WIDTH), per_b3),
            pl.BlockSpec((None, WINDOW, B_KV_WIDTH), per_b3),
            pl.BlockSpec((None, SUBLANES, 2 * C_WIDTH), per_b3),
            pl.BlockSpec((None, C_HEADS, C_HEAD_DIM, C_HEAD_DIM), lambda k: ((2 * k) // nt, 0, 0, 0)),
            pl.BlockSpec((None, C_HEADS, C_HEAD_DIM), per_b3),
            pl.BlockSpec((None, 1, LANES), per_b3),
        ],
        out_shape=[
            jax.ShapeDtypeStruct((batch * seq, D_MODEL), F32),
            jax.ShapeDtypeStruct((batch, WINDOW, B_KV_WIDTH), F32),
            jax.ShapeDtypeStruct((batch, WINDOW, B_KV_WIDTH), F32),
            jax.ShapeDtypeStruct((batch, SUBLANES, 2 * C_WIDTH), F32),
            jax.ShapeDtypeStruct((batch, C_HEADS, C_HEAD_DIM, C_HEAD_DIM), F32),
            jax.ShapeDtypeStruct((batch, C_HEADS, C_HEAD_DIM), F32),
            jax.ShapeDtypeStruct((batch, 1, LANES), F32),
        ],
        scratch_shapes=(
            [pltpu.VMEM((ts, w), F32) for w in (ZA_W, ZB_W, ZC_W)] * 2
            + [pltpu.VMEM((ts, D_MODEL), BF16)] * 2
            + [pltpu.VMEM((ts, Y_W), BF16),
               pltpu.VMEM((ts, 3 * D_MODEL), F32),
               pltpu.VMEM((WINDOW, B_KV_WIDTH), F32),
               pltpu.VMEM((WINDOW, B_KV_WIDTH), F32),
               pltpu.VMEM((ts + SUBLANES, 2 * C_WIDTH), F32)]),
        compiler_params=pltpu.CompilerParams(
            dimension_semantics=("arbitrary",), vmem_limit_bytes=VMEM_LIMIT),
        name="prompt_layer",
    )(lw["sinks"], x2, x2, mod, mod, lw["ng"], lw["wcat"], lw["bcat"],
      lw["vg"], lw["gws"], lw["gbs_col"], lw["qg"], lw["kg"], lw["cw"], lw["cb"], lw["fb"], lw["hg"],
      *_prompt_mask_constants(),
      lw["wmg"], lw["bmg"], lw["wa"], lw["wb"], lw["wc"], lw["wo"])


def _sample_mix_kernel(sink_ref, za_ref, zb_ref, zc_ref, kc_ref, vc_ref, cs_ref, c0_ref, n0_ref,
                       m0_ref, vg_ref, gwb_ref, gbs_ref, qg_ref, kg_ref, cw_ref, cb_ref, fb_ref,
                       hg_ref,
                       y_ref, vrow_ref, ko_ref, vo_ref, convo_ref, c1_ref, n1_ref, m1_ref,
                       xbuf):
    nb = SAMPLE_NB
    t = SUBLANES
    rows = nb * t
    tok_r = lax.broadcasted_iota(jnp.int32, (rows, rows), 0)
    tok_c = lax.broadcasted_iota(jnp.int32, (rows, rows), 1)
    same_b = (tok_r // t) == (tok_c // t)
    causal_b = same_b & (tok_c <= tok_r)

    u = za_ref[:, 0:A_WIDTH]
    vn = _rms(za_ref[:, A_WIDTH:2 * A_WIDTH]) * vg_ref[...]
    sg = _silu(za_ref[:, 2 * A_WIDTH:3 * A_WIDTH])
    vrow_ref[...] = vn
    vnb = vn.astype(BF16)
    s_cols = []
    for gi in range(A_GROUPS):
        s_cols.append(_dot(gwb_ref[gi], vnb[:, gi * GROUP_DIM:(gi + 1) * GROUP_DIM])
                      + gbs_ref[:, gi:gi + 1])
    y_ref[:, 0:A_WIDTH] = (u * jnp.concatenate(s_cols, axis=1) * sg).astype(BF16)

    qn = _qk_norm(zb_ref[:, 0:B_WIDTH], qg_ref[...]) * (B_HEAD_DIM ** -0.5)
    kn = _qk_norm(zb_ref[:, B_WIDTH:B_WIDTH + B_KV_WIDTH], kg_ref[...])
    vv = zb_ref[:, B_WIDTH + B_KV_WIDTH:B_WIDTH + 2 * B_KV_WIDTH]
    sgb = _silu(zb_ref[:, B_WIDTH + 2 * B_KV_WIDTH:ZB_W])
    kn3 = kn.reshape(nb, t, B_KV_WIDTH)
    vv3 = vv.reshape(nb, t, B_KV_WIDTH)
    kcache = kc_ref[...]
    vcache = vc_ref[...]
    pad = jnp.zeros((nb, WINDOW - t, B_KV_WIDTH), F32)
    kall = jnp.concatenate([kcache, kn3, pad], axis=1).astype(BF16)
    vall = jnp.concatenate([vcache, vv3, pad], axis=1).astype(BF16)
    qp = jnp.concatenate([_place_q_head(qn, h, rows).reshape(nb, t, LANES) for h in range(B_HEADS)],
                         axis=1).astype(BF16)
    logits = lax.dot_general(qp, kall, (((2,), (2,)), ((0,), (0,))), preferred_element_type=F32)
    qrow = lax.broadcasted_iota(jnp.int32, (nb, B_HEADS * t, 2 * WINDOW), 1)
    kcol = lax.broadcasted_iota(jnp.int32, (nb, B_HEADS * t, 2 * WINDOW), 2)
    qt = qrow % t
    valid = ((kcol < WINDOW) & (kcol > qt)) | ((kcol >= WINDOW) & ((kcol - WINDOW) <= qt))
    hrow = lax.broadcasted_iota(jnp.int32, (B_HEADS * t, 1), 0) // t
    snk = jnp.zeros((B_HEADS * t, 1), F32)
    for h in range(B_HEADS):
        snk = jnp.where(hrow == h, sink_ref[h], snk)
    lg = jnp.where(valid, logits, NEG)
    mx = jnp.maximum(jnp.max(lg, axis=-1, keepdims=True), snk[None])
    p = jnp.exp(lg - mx)
    den = jnp.sum(p, axis=-1, keepdims=True) + jnp.exp(snk[None] - mx)
    pv = lax.dot_general(p.astype(BF16), vall, (((2,), (1,)), ((0,), (0,))),
                         preferred_element_type=F32) / den
    head_out = [pv[:, h * t:(h + 1) * t, :].reshape(rows, LANES) for h in range(B_HEADS)]
    yb = jnp.concatenate(
        [_merge_head_pair(head_out[2 * j], head_out[2 * j + 1], 2 * j, rows)
         for j in range(B_HEADS // 2)], axis=1)
    y_ref[:, A_WIDTH:A_WIDTH + B_WIDTH] = (yb * sgb).astype(BF16)
    ko_ref[...] = jnp.concatenate([kcache[:, t:, :], kn3], axis=1)
    vo_ref[...] = jnp.concatenate([vcache[:, t:, :], vv3], axis=1)

    xbuf[:, SUBLANES - (C_CONV - 1):SUBLANES, :] = cs_ref[...]
    xbuf[:, SUBLANES:2 * SUBLANES, :] = zc_ref[:, 0:2 * C_WIDTH].reshape(nb, t, 2 * C_WIDTH)
    y3 = cb_ref[...][None]
    for j in range(C_CONV):
        lo = SUBLANES - (C_CONV - 1) + j
        y3 = y3 + cw_ref[j:j + 1, :][None] * xbuf[:, lo:lo + t, :]
    convo_ref[...] = xbuf[:, 2 * SUBLANES - (C_CONV - 1):2 * SUBLANES, :]
    qk = _silu(y3.reshape(rows, 2 * C_WIDTH))
    qall = qk[:, 0:C_WIDTH].astype(BF16)
    kall_c = qk[:, C_WIDTH:2 * C_WIDTH] * (C_HEAD_DIM ** -0.5)
    vall_c = zc_ref[:, 2 * C_WIDTH:3 * C_WIDTH].astype(BF16)
    gate_o = _sigmoid(zc_ref[:, 3 * C_WIDTH:4 * C_WIDTH]) * _silu(zc_ref[:, 4 * C_WIDTH:5 * C_WIDTH])
    ifp = zc_ref[:, 5 * C_WIDTH:5 * C_WIDTH + LANES]
    lf = _log_sigmoid(ifp + fb_ref[...])
    lane_t = lax.broadcasted_iota(jnp.int32, (rows, LANES), 1)
    cum_all = _dot_exact01(jnp.where(causal_b, 1.0, 0.0).astype(BF16), lf)
    tot_all = _dot_exact01(jnp.where(same_b, 1.0, 0.0).astype(BF16), lf)
    st_col = jnp.where(lane_t < C_HEADS, ifp, cum_all)
    st_row = st_col.T
    tot_row = tot_all.T
    m0 = m0_ref[...]
    same_b_bf = jnp.where(same_b, 1.0, 0.0).astype(BF16)
    batch_of_lane = lax.broadcasted_iota(jnp.int32, (nb, 1, rows), 2) // t
    batch_id = lax.broadcasted_iota(jnp.int32, (nb, 1, rows), 0)
    own_tok = batch_of_lane == batch_id
    h_cols = []
    m_out = jnp.zeros((rows, LANES), F32)
    for hd in range(C_HEADS):
        hs = slice(hd * C_HEAD_DIM, (hd + 1) * C_HEAD_DIM)
        i_c = st_col[:, hd:hd + 1]
        cum_c = st_col[:, C_HEADS + hd:C_HEADS + hd + 1]
        tot_c = tot_all[:, C_HEADS + hd:C_HEADS + hd + 1]
        i_r = st_row[hd:hd + 1, :]
        cum_r = st_row[C_HEADS + hd:C_HEADS + hd + 1, :]
        tot_r = tot_row[C_HEADS + hd:C_HEADS + hd + 1, :]
        m_prev = m0[:, hd:hd + 1]
        dmat = jnp.where(causal_b, cum_c - cum_r + i_r, NEG)
        m_inter = cum_c + m_prev
        m_t = jnp.maximum(m_inter, jnp.max(dmat, axis=-1, keepdims=True))
        q_h = qall[:, hs]
        k_h = kall_c[:, hs]
        v_h = vall_c[:, hs]
        a = jnp.exp(dmat - m_t) * _dot_nt(q_h, k_h.astype(BF16))
        w_inter = jnp.exp(m_inter - m_t)
        c_prev = c0_ref[:, hd]
        n_tok = jnp.broadcast_to(n0_ref[hd][:, None, :], (nb, t, C_HEAD_DIM)).reshape(rows, C_HEAD_DIM)
        inter = lax.dot_general(q_h.reshape(nb, t, C_HEAD_DIM), c_prev.astype(BF16),
                                (((2,), (1,)), ((0,), (0,))), preferred_element_type=F32)
        num = _dot(a.astype(BF16), v_h) + w_inter * inter.reshape(rows, C_HEAD_DIM)
        den = (jnp.sum(a, axis=-1, keepdims=True)
               + w_inter * jnp.sum(q_h.astype(F32) * n_tok, axis=-1, keepdims=True))
        hh = num / jnp.maximum(jnp.abs(den), jnp.exp(-m_t))
        h_cols.append(_rms(hh))
        g_r = tot_r - cum_r + i_r
        g_c = tot_c - cum_c + i_c
        m_new = jnp.maximum(tot_c + m_prev,
                            jnp.max(jnp.where(same_b, g_r, NEG), axis=-1, keepdims=True))
        kw = jnp.exp(g_c - m_new) * k_h
        decay = jnp.exp(tot_c + m_prev - m_new)
        kwt = kw.T
        lhs = jnp.where(own_tok, kwt[None], 0.0).astype(BF16).reshape(nb * C_HEAD_DIM, rows)
        upd = _dot(lhs, v_h).reshape(nb, C_HEAD_DIM, C_HEAD_DIM)
        dec_b = jnp.broadcast_to(decay, (rows, C_HEAD_DIM)).reshape(nb, t, C_HEAD_DIM)[:, 0:1, :]
        c1_ref[:, hd] = dec_b * c_prev + upd
        n1_ref[hd] = decay * n_tok + _dot(same_b_bf, kw.astype(BF16))
        m_out = jnp.where(lane_t == hd, m_new, m_out)
    m1_ref[...] = m_out
    hn = jnp.concatenate(h_cols, axis=1) * hg_ref[...]
    y_ref[:, A_WIDTH + B_WIDTH:Y_W] = (hn * gate_o).astype(BF16)


def _sample_mix_call(l, za, zb, zc, kc, vc, cs, c0, n0t, m0tok, lw, nbatch):
    nb = SAMPLE_NB
    t = SUBLANES
    rows = nb * t
    tok = lambda i: (i, 0)
    const2 = lambda i: (0, 0)
    const3 = lambda i: (0, 0, 0)
    b3 = lambda i: (i, 0, 0)
    lb4 = lambda i: (l, i, 0, 0)
    return pl.pallas_call(
        _sample_mix_kernel,
        grid=(nbatch // nb,),
        in_specs=[
            pl.BlockSpec(memory_space=pltpu.SMEM),
            pl.BlockSpec((rows, ZA_W), tok),
            pl.BlockSpec((rows, ZB_W), tok),
            pl.BlockSpec((rows, ZC_W), tok),
            pl.BlockSpec((None, nb, WINDOW, B_KV_WIDTH), lb4),
            pl.BlockSpec((None, nb, WINDOW, B_KV_WIDTH), lb4),
            pl.BlockSpec((None, nb, C_CONV - 1, 2 * C_WIDTH), lb4),
            pl.BlockSpec((None, nb, C_HEADS, C_HEAD_DIM, C_HEAD_DIM), lambda i: (l, i, 0, 0, 0)),
            pl.BlockSpec((None, C_HEADS, nb, C_HEAD_DIM), lambda i: (l, 0, i, 0)),
            pl.BlockSpec((None, rows, LANES), lambda i: (l, i, 0)),
            pl.BlockSpec((1, A_WIDTH), const2),
            pl.BlockSpec((A_GROUPS, rows, rows), const3),
            pl.BlockSpec((rows, LANES), const2),
            pl.BlockSpec((1, B_WIDTH), const2),
            pl.BlockSpec((1, B_KV_WIDTH), const2),
            pl.BlockSpec((C_CONV, 2 * C_WIDTH), const2),
            pl.BlockSpec((1, 2 * C_WIDTH), const2),
            pl.BlockSpec((1, LANES), const2),
            pl.BlockSpec((1, C_WIDTH), const2),
        ],
        out_specs=[
            pl.BlockSpec((rows, Y_W), tok),
            pl.BlockSpec((rows, A_WIDTH), tok),
            pl.BlockSpec((nb, WINDOW, B_KV_WIDTH), b3),
            pl.BlockSpec((nb, WINDOW, B_KV_WIDTH), b3),
            pl.BlockSpec((nb, C_CONV - 1, 2 * C_WIDTH), b3),
            pl.BlockSpec((nb, C_HEADS, C_HEAD_DIM, C_HEAD_DIM), lambda i: (i, 0, 0, 0)),
            pl.BlockSpec((C_HEADS, rows, C_HEAD_DIM), lambda i: (0, i, 0)),
            pl.BlockSpec((rows, LANES), tok),
        ],
        out_shape=[
            jax.ShapeDtypeStruct((nbatch * t, Y_W), BF16),
            jax.ShapeDtypeStruct((nbatch * t, A_WIDTH), F32),
            jax.ShapeDtypeStruct((nbatch, WINDOW, B_KV_WIDTH), F32),
            jax.ShapeDtypeStruct((nbatch, WINDOW, B_KV_WIDTH), F32),
            jax.ShapeDtypeStruct((nbatch, C_CONV - 1, 2 * C_WIDTH), F32),
            jax.ShapeDtypeStruct((nbatch, C_HEADS, C_HEAD_DIM, C_HEAD_DIM), F32),
            jax.ShapeDtypeStruct((C_HEADS, nbatch * t, C_HEAD_DIM), F32),
            jax.ShapeDtypeStruct((nbatch * t, LANES), F32),
        ],
        scratch_shapes=[pltpu.VMEM((nb, 2 * SUBLANES, 2 * C_WIDTH), F32)],
        compiler_params=pltpu.CompilerParams(
            dimension_semantics=("arbitrary",), vmem_limit_bytes=VMEM_LIMIT),
        name="sample_mixer",
    )(lw["sinks"], za, zb, zc, kc, vc, cs, c0, n0t, m0tok, lw["vg"], lw["gwb"], lw["gbs_tok"],
      lw["qg"], lw["kg"], lw["cw"], lw["cb"], lw["fb"], lw["hg"])


def _layer_weights(l, w_in, b_in, gmlp_vnorm_g, gmlp_ws, gmlp_bs, swa_qnorm_g, swa_knorm_g,
                   swa_sinks, mlstm_conv_w, mlstm_conv_b, mlstm_f_bias, mlstm_hnorm_g,
                   w_branch_a, w_branch_b, w_branch_c, w_out, norm_g, dec_seq):
    wl, bl = w_in[l], b_in[l]
    o_ci = ZA_W + ZB_W + 3 * C_WIDTH
    o_co = o_ci + 2 * C_HEADS
    o_mg = o_co + 2 * C_WIDTH
    pad_w = LANES - 2 * C_HEADS

    def regroup(a):
        return jnp.concatenate(
            [a[..., :o_ci], a[..., o_co:o_mg], a[..., o_ci:o_co],
             jnp.zeros(a.shape[:-1] + (pad_w,), a.dtype)], axis=-1)

    t = dec_seq
    nb = SAMPLE_NB
    ws_t = gmlp_ws[l][:, :t, :t] * jnp.tril(jnp.ones((t, t), F32))
    eye = jnp.eye(nb, dtype=F32)
    gwb = jnp.einsum("bc,gts->gbtcs", eye, ws_t).reshape(A_GROUPS, nb * t, nb * t).astype(BF16)
    gbs_col = jnp.pad(gmlp_bs[l].T, ((0, 0), (0, LANES - A_GROUPS)))
    gbs_tok = jnp.pad(jnp.tile(gmlp_bs[l][:, :t].T, (nb, 1)), ((0, 0), (0, LANES - A_GROUPS)))
    fb = jnp.pad(mlstm_f_bias[l], (C_HEADS, LANES - 2 * C_HEADS)).reshape(1, LANES)
    return dict(
        ng=norm_g[l].reshape(1, D_MODEL),
        wcat=regroup(wl).astype(BF16), bcat=regroup(bl).reshape(1, ZCAT_W),
        wmg=wl[:, o_mg:].astype(BF16), bmg=bl[o_mg:].reshape(1, 3 * D_MODEL),
        wa=w_branch_a[l].astype(BF16), wb=w_branch_b[l].astype(BF16),
        wc=w_branch_c[l].astype(BF16), wo=w_out[l].astype(BF16),
        vg=gmlp_vnorm_g[l].reshape(1, A_WIDTH), gws=gmlp_ws[l], gwb=gwb,
        gbs_col=gbs_col, gbs_tok=gbs_tok,
        qg=jnp.tile(swa_qnorm_g[l], B_HEADS).reshape(1, B_WIDTH),
        kg=jnp.tile(swa_knorm_g[l], B_KV_HEADS).reshape(1, B_KV_WIDTH),
        sinks=swa_sinks[l],
        cw=mlstm_conv_w[l], cb=mlstm_conv_b[l].reshape(1, 2 * C_WIDTH), fb=fb,
        hg=mlstm_hnorm_g[l].reshape(1, C_WIDTH),
    )


def kernel(x_prompt, x_sample, cache_swa_k, cache_swa_v, state_mlstm_conv, state_mlstm_C, state_mlstm_n, state_mlstm_m, c_prompt, c_sample, ada_w, ada_b, norm_g, w_in, b_in, gmlp_vnorm_g, gmlp_ws, gmlp_bs, swa_qnorm_g, swa_knorm_g, swa_sinks, mlstm_conv_w, mlstm_conv_b, mlstm_f_bias, mlstm_hnorm_g, w_branch_a, w_branch_b, w_branch_c, w_out):
    batch, seq, _ = x_prompt.shape
    nbatch, dec_seq, _ = x_sample.shape
    assert dec_seq == SUBLANES and seq % PROMPT_TILE == 0 and nbatch % SAMPLE_NB == 0
    assert seq % PROJ_TILE == 0 and (nbatch * dec_seq) % PROJ_TILE == 0
    wb_len = cache_swa_k.shape[2]
    assert wb_len == WINDOW

    nc = batch + nbatch
    nc_pad = -(-nc // SUBLANES) * SUBLANES
    c_all = jnp.concatenate([c_prompt, c_sample, jnp.zeros((nc_pad - nc, D_MODEL), F32)], axis=0)
    mod_all = _ada_call(c_all, ada_w, ada_b)

    xp = x_prompt.reshape(batch * seq, D_MODEL)
    xs = x_sample.reshape(nbatch * dec_seq, D_MODEL)
    kc_all = cache_swa_k.reshape(DEPTH, nbatch, WINDOW, B_KV_WIDTH)
    vc_all = cache_swa_v.reshape(DEPTH, nbatch, WINDOW, B_KV_WIDTH)
    n0t_all = jnp.transpose(state_mlstm_n, (0, 2, 1, 3))
    m0tok_all = jnp.pad(jnp.repeat(state_mlstm_m, dec_seq, axis=1),
                        ((0, 0), (0, 0), (0, LANES - C_HEADS)))
    outs_p = [[] for _ in range(6)]
    outs_s = [[] for _ in range(6)]
    vrows = []
    for l in range(DEPTH):
        lw = _layer_weights(l, w_in, b_in, gmlp_vnorm_g, gmlp_ws, gmlp_bs, swa_qnorm_g,
                            swa_knorm_g, swa_sinks, mlstm_conv_w, mlstm_conv_b, mlstm_f_bias,
                            mlstm_hnorm_g, w_branch_a, w_branch_b, w_branch_c, w_out, norm_g,
                            dec_seq)
        mod_p = mod_all[l, :batch].reshape(batch, 1, 3 * D_MODEL)
        mod_s = jnp.repeat(mod_all[l, batch:nc], dec_seq, axis=0)

        xp, ko, vo, convo, c1, n1, m1 = _prompt_layer_call(xp, mod_p, lw, batch, seq)
        outs_p[0].append(ko.reshape(batch, WINDOW, B_KV_HEADS, B_HEAD_DIM))
        outs_p[1].append(vo.reshape(batch, WINDOW, B_KV_HEADS, B_HEAD_DIM))
        outs_p[2].append(convo[:, SUBLANES - (C_CONV - 1):, :])
        outs_p[3].append(c1)
        outs_p[4].append(n1)
        outs_p[5].append(m1[:, 0, :C_HEADS])

        za, zb, zc = _inproj_call(xs, mod_s, lw["ng"], lw["wcat"], lw["bcat"], None)
        y, vrow, ko, vo, convo, c1, n1tok, m1tok = _sample_mix_call(
            l, za, zb, zc, kc_all, vc_all, state_mlstm_conv, state_mlstm_C, n0t_all, m0tok_all,
            lw, nbatch)
        xs = _outproj_call(xs, mod_s, lw["ng"], y, lw["wmg"], lw["bmg"], lw["wa"], lw["wb"],
                           lw["wc"], lw["wo"], None)
        outs_s[0].append(ko.reshape(nbatch, WINDOW, B_KV_HEADS, B_HEAD_DIM))
        outs_s[1].append(vo.reshape(nbatch, WINDOW, B_KV_HEADS, B_HEAD_DIM))
        outs_s[2].append(convo)
        outs_s[3].append(c1)
        outs_s[4].append(jnp.transpose(n1tok[:, ::dec_seq, :], (1, 0, 2)))
        outs_s[5].append(m1tok[::dec_seq, :C_HEADS])
        vrows.append(vrow.reshape(nbatch, dec_seq, A_WIDTH))

    sp = [jnp.stack(o) for o in outs_p]
    ss = [jnp.stack(o) for o in outs_s]
    return (xp.reshape(batch, seq, D_MODEL), xs.reshape(nbatch, dec_seq, D_MODEL),
            sp[0], sp[1], sp[2], sp[3], sp[4], sp[5],
            ss[0], ss[1], ss[2], ss[3], ss[4], ss[5], jnp.stack(vrows))
```

```python
import functools

import numpy as np
import jax
import jax.numpy as jnp
from jax import lax
from jax.experimental import pallas as pl
from jax.experimental.pallas import tpu as pltpu

F32 = jnp.float32
BF16 = jnp.bfloat16

D_MODEL = 1024
DEPTH = 2
A_WIDTH = 512
A_GROUPS = 4
GROUP_DIM = 128
B_HEADS = 8
B_KV_HEADS = 2
B_HEAD_DIM = 64
B_WIDTH = 512
B_KV_WIDTH = 128
WINDOW = 128
C_HEADS = 4
C_HEAD_DIM = 128
C_WIDTH = 512
C_CONV = 4
EPS = 1e-6
NEG = -1e30

LANES = 128
SUBLANES = 8
VMEM_LIMIT = 56 * 1024 * 1024

ZA_W = 3 * A_WIDTH
ZB_W = 2 * B_WIDTH + 2 * B_KV_WIDTH
ZC_W = 2 * C_WIDTH + 3 * C_WIDTH + LANES
ZCAT_W = ZA_W + ZB_W + ZC_W
Y_W = A_WIDTH + B_WIDTH + C_WIDTH

PROMPT_TILE = 256
MLSTM_CHUNK = PROMPT_TILE
SAMPLE_NB = 16
PROJ_TILE = 512


def _sigmoid(x):
    return 0.5 * jnp.tanh(0.5 * x) + 0.5


def _silu(x):
    t = 0.5 * x
    return t * (jnp.tanh(t) + 1.0)


def _log_sigmoid(x):
    return jnp.minimum(x, 0.0) - jnp.log1p(jnp.exp(-jnp.abs(x)))


def _rms(x):
    return x * lax.rsqrt(jnp.mean(x * x, axis=-1, keepdims=True) + EPS)


def _dot(a, b):
    return jnp.dot(a, b, preferred_element_type=F32)


def _dot_nt(a, b):
    return lax.dot_general(a, b, (((1,), (1,)), ((), ())), preferred_element_type=F32)


def _dot_exact01(m01, x):
    hi = x.astype(BF16)
    r1 = x - hi.astype(F32)
    mid = r1.astype(BF16)
    lo = (r1 - mid.astype(F32)).astype(BF16)
    return _dot(m01, hi) + _dot(m01, mid) + _dot(m01, lo)


def _modulated_norm(x, mod_ref, ng_ref):
    xn = _rms(x) * ng_ref[...]
    shift = mod_ref[:, 0:D_MODEL]
    scale = mod_ref[:, D_MODEL:2 * D_MODEL]
    return (xn * (1.0 + scale) + shift).astype(BF16)


def _head_rms_scale(x2, lane_lo):
    s0 = jnp.sum(jnp.where(lane_lo, x2, 0.0), axis=-1, keepdims=True)
    s1 = jnp.sum(jnp.where(lane_lo, 0.0, x2), axis=-1, keepdims=True)
    r0 = lax.rsqrt(s0 * (1.0 / B_HEAD_DIM) + EPS)
    r1 = lax.rsqrt(s1 * (1.0 / B_HEAD_DIM) + EPS)
    return jnp.where(lane_lo, r0, r1)


def _qk_norm(x, g_row):
    rows, width = x.shape
    lane_lo = lax.broadcasted_iota(jnp.int32, (rows, LANES), 1) < B_HEAD_DIM
    outs = []
    for j in range(width // LANES):
        slab = x[:, j * LANES:(j + 1) * LANES]
        outs.append(slab * _head_rms_scale(slab * slab, lane_lo))
    y = outs[0] if len(outs) == 1 else jnp.concatenate(outs, axis=1)
    return y * g_row


def _ada_kernel(c_ref, w_ref, b_ref, o_ref):
    c = c_ref[...]
    o_ref[...] = _dot(_silu(c).astype(BF16), w_ref[...].astype(BF16)) + b_ref[...]


def _ada_call(c_all, ada_w, ada_b):
    rows = c_all.shape[0]
    return pl.pallas_call(
        _ada_kernel,
        grid=(DEPTH, 3),
        in_specs=[
            pl.BlockSpec((rows, D_MODEL), lambda l, j: (0, 0)),
            pl.BlockSpec((None, D_MODEL, D_MODEL), lambda l, j: (l, 0, j)),
            pl.BlockSpec((None, 1, D_MODEL), lambda l, j: (l, 0, j)),
        ],
        out_specs=pl.BlockSpec((None, rows, D_MODEL), lambda l, j: (l, 0, j)),
        out_shape=jax.ShapeDtypeStruct((DEPTH, rows, 3 * D_MODEL), F32),
        compiler_params=pltpu.CompilerParams(
            dimension_semantics=("arbitrary", "arbitrary"), vmem_limit_bytes=VMEM_LIMIT),
        name="adaln_mod",
    )(c_all, ada_w, ada_b.reshape(DEPTH, 1, 3 * D_MODEL))


COL_CI = ZA_W + ZB_W + 3 * C_WIDTH
COL_CO = COL_CI + 2 * C_HEADS
COL_MG = COL_CO + 2 * C_WIDTH
PREP_ROWS = 128


def _weight_prep_kernel(w_ref, wcat_ref, wmg_ref):
    wcat_ref[:, 0:COL_CI] = w_ref[:, 0:COL_CI].astype(BF16)
    tail = w_ref[:, COL_CI:COL_CI + 2 * C_WIDTH + LANES]
    wcat_ref[:, COL_CI:COL_CI + 2 * C_WIDTH] = (
        tail[:, 2 * C_HEADS:2 * C_HEADS + 2 * C_WIDTH].astype(BF16))
    lane = lax.broadcasted_iota(jnp.int32, (PREP_ROWS, LANES), 1)
    wcat_ref[:, ZCAT_W - LANES:ZCAT_W] = jnp.where(lane < 2 * C_HEADS, tail[:, 0:LANES], 0.0).astype(BF16)
    gates = w_ref[:, COL_MG - 2 * C_HEADS:COL_MG + 3 * D_MODEL]
    wmg_ref[...] = gates[:, 2 * C_HEADS:].astype(BF16)


def _weight_prep_call(w_in):
    in_width = w_in.shape[-1]
    assert in_width == COL_MG + 3 * D_MODEL and (COL_MG - 2 * C_HEADS) % LANES == 0
    return pl.pallas_call(
        _weight_prep_kernel,
        grid=(DEPTH, D_MODEL // PREP_ROWS),
        in_specs=[pl.BlockSpec((None, PREP_ROWS, in_width), lambda l, i: (l, i, 0))],
        out_specs=[
            pl.BlockSpec((None, PREP_ROWS, ZCAT_W), lambda l, i: (l, i, 0)),
            pl.BlockSpec((None, PREP_ROWS, 3 * D_MODEL), lambda l, i: (l, i, 0)),
        ],
        out_shape=[
            jax.ShapeDtypeStruct((DEPTH, D_MODEL, ZCAT_W), BF16),
            jax.ShapeDtypeStruct((DEPTH, D_MODEL, 3 * D_MODEL), BF16),
        ],
        compiler_params=pltpu.CompilerParams(
            dimension_semantics=("arbitrary", "arbitrary"), vmem_limit_bytes=VMEM_LIMIT),
        name="weight_prep",
    )(w_in)


def _col_chunks(width, step):
    return [(o, min(step, width - o)) for o in range(0, width, step)]


def _inproj_pieces(get_h, w_ref, b_ref, za_ref, zb_ref, zc_ref, step):
    def piece(o_ref, off, woff, w):
        def run():
            o_ref[:, off:off + w] = _dot(get_h(), w_ref[:, woff:woff + w]) + b_ref[:, woff:woff + w]
        return run
    pieces = []
    base = 0
    for o_ref, width in ((za_ref, ZA_W), (zb_ref, ZB_W), (zc_ref, ZC_W)):
        pieces += [piece(o_ref, off, base + off, w) for off, w in _col_chunks(width, step)]
        base += width
    return pieces


def _inproj_kernel(x_ref, mod_ref, ng_ref, w_ref, b_ref, za_ref, zb_ref, zc_ref):
    h = _modulated_norm(x_ref[...], mod_ref, ng_ref)
    for piece in _inproj_pieces(lambda: h, w_ref, b_ref, za_ref, zb_ref, zc_ref, 512):
        piece()


def _mod_spec(tm, tokens_per_batch):
    if tokens_per_batch is None:
        return pl.BlockSpec((tm, 3 * D_MODEL), lambda i: (i, 0))
    tiles_per_batch = tokens_per_batch // tm
    return pl.BlockSpec((None, 1, 3 * D_MODEL), lambda i: (i // tiles_per_batch, 0, 0))


def _layer_weight_spec(layer, rows, cols):
    return pl.BlockSpec((None, rows, cols), lambda i: (layer, 0, 0), pipeline_mode=pl.Buffered(1))


def _inproj_call(layer, x2, mod, ng, wcat, bcat, tokens_per_batch):
    ntok = x2.shape[0]
    tm = PROJ_TILE
    const = lambda i: (0, 0)
    return pl.pallas_call(
        _inproj_kernel,
        grid=(ntok // tm,),
        in_specs=[
            pl.BlockSpec((tm, D_MODEL), lambda i: (i, 0)),
            _mod_spec(tm, tokens_per_batch),
            pl.BlockSpec((1, D_MODEL), const),
            _layer_weight_spec(layer, D_MODEL, ZCAT_W),
            pl.BlockSpec((1, ZCAT_W), const),
        ],
        out_specs=[
            pl.BlockSpec((tm, ZA_W), lambda i: (i, 0)),
            pl.BlockSpec((tm, ZB_W), lambda i: (i, 0)),
            pl.BlockSpec((tm, ZC_W), lambda i: (i, 0)),
        ],
        out_shape=[
            jax.ShapeDtypeStruct((ntok, ZA_W), F32),
            jax.ShapeDtypeStruct((ntok, ZB_W), F32),
            jax.ShapeDtypeStruct((ntok, ZC_W), F32),
        ],
        compiler_params=pltpu.CompilerParams(
            dimension_semantics=("arbitrary",), vmem_limit_bytes=VMEM_LIMIT),
        name="in_projection",
    )(x2, mod, ng, wcat, bcat)


def _outproj_kernel(x_ref, mod_ref, ng_ref, y_ref, wmg_ref, bmg_ref, wa_ref, wb_ref, wc_ref,
                    wo_ref, o_ref):
    x = x_ref[...]
    h = _modulated_norm(x, mod_ref, ng_ref)
    merged = None
    for i, wbr_ref in enumerate((wa_ref, wb_ref, wc_ref)):
        cols = slice(i * D_MODEL, (i + 1) * D_MODEL)
        gate = _sigmoid(_dot(h, wmg_ref[:, cols]) + bmg_ref[:, cols])
        term = gate * _dot(y_ref[:, i * A_WIDTH:(i + 1) * A_WIDTH], wbr_ref[...])
        merged = term if merged is None else merged + term
    ada_gate = mod_ref[:, 2 * D_MODEL:3 * D_MODEL]
    o_ref[...] = x + ada_gate * _dot(merged.astype(BF16), wo_ref[...])


def _outproj_call(layer, x2, mod, ng, y, wmg, bmg, wa, wb, wc, wo, tokens_per_batch):
    ntok = x2.shape[0]
    tm = PROJ_TILE
    const = lambda i: (0, 0)
    once = pl.Buffered(1)
    return pl.pallas_call(
        _outproj_kernel,
        grid=(ntok // tm,),
        in_specs=[
            pl.BlockSpec((tm, D_MODEL), lambda i: (i, 0)),
            _mod_spec(tm, tokens_per_batch),
            pl.BlockSpec((1, D_MODEL), const),
            pl.BlockSpec((tm, Y_W), lambda i: (i, 0)),
            _layer_weight_spec(layer, D_MODEL, 3 * D_MODEL),
            pl.BlockSpec((1, 3 * D_MODEL), const),
            pl.BlockSpec((A_WIDTH, D_MODEL), const, pipeline_mode=once),
            pl.BlockSpec((B_WIDTH, D_MODEL), const, pipeline_mode=once),
            pl.BlockSpec((C_WIDTH, D_MODEL), const, pipeline_mode=once),
            pl.BlockSpec((D_MODEL, D_MODEL), const, pipeline_mode=once),
        ],
        out_specs=pl.BlockSpec((tm, D_MODEL), lambda i: (i, 0)),
        out_shape=jax.ShapeDtypeStruct((ntok, D_MODEL), F32),
        compiler_params=pltpu.CompilerParams(
            dimension_semantics=("arbitrary",), vmem_limit_bytes=VMEM_LIMIT),
        name="out_projection",
    )(x2, mod, ng, y, wmg, bmg, wa, wb, wc, wo)


def _place_q_head(qn, h, rows):
    lane = lax.broadcasted_iota(jnp.int32, (rows, LANES), 1)
    slab = qn[:, (h // 2) * LANES:(h // 2 + 1) * LANES]
    src_hi = h % 2
    dst_hi = h // (B_HEADS // B_KV_HEADS)
    keep = (lane >= B_HEAD_DIM) if src_hi else (lane < B_HEAD_DIM)
    slab = jnp.where(keep, slab, 0.0)
    if src_hi != dst_hi:
        slab = pltpu.roll(slab, B_HEAD_DIM, 1)
    return slab


def _merge_head_pair(o_even, o_odd, h_even, rows):
    lane_lo = lax.broadcasted_iota(jnp.int32, (rows, LANES), 1) < B_HEAD_DIM
    kv_hi = h_even // (B_HEADS // B_KV_HEADS)
    if kv_hi:
        o_even = pltpu.roll(o_even, B_HEAD_DIM, 1)
    else:
        o_odd = pltpu.roll(o_odd, B_HEAD_DIM, 1)
    return jnp.where(lane_lo, o_even, o_odd)


def _conv_taps(xbuf, cw_ref, cb_ref, cols, ts):
    y = cb_ref[:, cols]
    for j in range(C_CONV):
        lo = SUBLANES - (C_CONV - 1) + j
        y = y + cw_ref[j:j + 1, cols] * xbuf[lo:lo + ts, cols]
    return y


def _prompt_mix_kernel(sink_ref, za_ref, zb_ref, zc_ref, vg_ref, gw_ref, gbs_ref, qg_ref, kg_ref,
                       cw_ref, cb_ref, fb_ref, hg_ref, tril_ref, band_ref, tri01_ref, tribias_ref,
                       y_ref, ko_ref, vo_ref, convo_ref, c_ref, n_ref, m_ref,
                       kprev, vprev, xbuf, first_tile, pump):
    ts = PROMPT_TILE

    u = za_ref[:, 0:A_WIDTH]
    vn = _rms(za_ref[:, A_WIDTH:2 * A_WIDTH]) * vg_ref[...]
    sg = _silu(za_ref[:, 2 * A_WIDTH:3 * A_WIDTH])
    vnb = vn.astype(BF16)
    wts = [(gw_ref[gi] * tril_ref[...]).astype(BF16) for gi in range(A_GROUPS)]
    s_rows = []
    for c in range(ts // WINDOW):
        s_cols = []
        for gi in range(A_GROUPS):
            vblk = vnb[c * WINDOW:(c + 1) * WINDOW, gi * GROUP_DIM:(gi + 1) * GROUP_DIM]
            s_cols.append(_dot(wts[gi], vblk) + gbs_ref[:, gi:gi + 1])
        s_rows.append(jnp.concatenate(s_cols, axis=1))
    s = jnp.concatenate(s_rows, axis=0)
    pump()
    y_ref[:, 0:A_WIDTH] = (u * s * sg).astype(BF16)
    pump()

    qn = _qk_norm(zb_ref[:, 0:B_WIDTH], qg_ref[...]) * (B_HEAD_DIM ** -0.5)
    pump()
    kn = _qk_norm(zb_ref[:, B_WIDTH:B_WIDTH + B_KV_WIDTH], kg_ref[...])
    vv = zb_ref[:, B_WIDTH + B_KV_WIDTH:B_WIDTH + 2 * B_KV_WIDTH]
    sgb = _silu(zb_ref[:, B_WIDTH + 2 * B_KV_WIDTH:ZB_W])
    pump()
    grp = B_HEADS // B_KV_HEADS
    nblk = ts // WINDOW
    lane_lo2 = lax.broadcasted_iota(jnp.int32, (2 * WINDOW, LANES), 1) < B_HEAD_DIM
    kblocks = [kprev[...]] + [kn[b * WINDOW:(b + 1) * WINDOW] for b in range(nblk)]
    vblocks = [vprev[...]] + [vv[b * WINDOW:(b + 1) * WINDOW] for b in range(nblk)]
    bias0 = band_ref[0] if first_tile is False else jnp.where(first_tile, band_ref[1], band_ref[0])
    bias = [bias0] + [band_ref[0]] * (nblk - 1)
    combos = [(blk, kh) for blk in range(nblk) for kh in range(B_KV_HEADS)]
    heads = [(blk, kh, g) for blk, kh in combos for g in range(grp)]
    kdup, vdup = {}, {}
    for blk in range(nblk):
        kcat = jnp.concatenate([kblocks[blk], kblocks[blk + 1]], axis=0)
        vcat = jnp.concatenate([vblocks[blk], vblocks[blk + 1]], axis=0)
        krol = pltpu.roll(kcat, B_HEAD_DIM, 1)
        vrol = pltpu.roll(vcat, B_HEAD_DIM, 1)
        for kh in range(B_KV_HEADS):
            own = lane_lo2 if kh == 0 else jnp.logical_not(lane_lo2)
            kdup[blk, kh] = jnp.where(own, kcat, krol).astype(BF16)
            vdup[blk, kh] = jnp.where(own, vcat, vrol).astype(BF16)
    pump()
    qs = {(blk, kh): jnp.concatenate(
        [_place_q_head(qn[blk * WINDOW:(blk + 1) * WINDOW], kh * grp + g, WINDOW) for g in range(grp)],
        axis=0).astype(BF16) for blk, kh in combos}
    pump()
    logits = {c: _dot_nt(qs[c], kdup[c]) for c in combos}
    pump()
    snk = {k: sink_ref[k[1] * grp + k[2]] for k in heads}
    lg = {(blk, kh, g): logits[blk, kh][g * WINDOW:(g + 1) * WINDOW] + bias[blk]
          for blk, kh, g in heads}
    pump()
    mx = {k: jnp.maximum(jnp.max(lg[k], axis=-1, keepdims=True), snk[k]) for k in heads}
    pump()
    p = {k: jnp.exp(lg[k] - mx[k]) for k in heads}
    pump()
    rden = {k: 1.0 / (jnp.sum(p[k], axis=-1, keepdims=True) + jnp.exp(snk[k] - mx[k])) for k in heads}
    pump()
    pv = {c: _dot(jnp.concatenate([p[c + (g,)].astype(BF16) for g in range(grp)], axis=0), vdup[c])
          for c in combos}
    pump()
    outs = {(blk, kh, g): pv[blk, kh][g * WINDOW:(g + 1) * WINDOW] * rden[blk, kh, g]
            for blk, kh, g in heads}
    pump()
    yb = jnp.concatenate([jnp.concatenate(
        [_merge_head_pair(outs[blk, (2 * j) // grp, (2 * j) % grp],
                          outs[blk, (2 * j + 1) // grp, (2 * j + 1) % grp], 2 * j, WINDOW)
         for j in range(B_HEADS // 2)], axis=1) for blk in range(nblk)], axis=0)
    y_ref[:, A_WIDTH:A_WIDTH + B_WIDTH] = (yb * sgb).astype(BF16)
    pump()
    kprev[...] = kblocks[nblk]
    vprev[...] = vblocks[nblk]
    ko_ref[...] = kblocks[nblk]
    vo_ref[...] = vblocks[nblk]
    pump()

    xbuf[SUBLANES:SUBLANES + ts, :] = zc_ref[:, 0:2 * C_WIDTH]
    qk = _silu(_conv_taps(xbuf, cw_ref, cb_ref, slice(0, 2 * C_WIDTH), ts))
    pump()
    tail = xbuf[ts:ts + SUBLANES, :]
    xbuf[0:SUBLANES, :] = tail
    convo_ref[...] = tail
    qall = qk[:, 0:C_WIDTH].astype(BF16)
    kall = qk[:, C_WIDTH:2 * C_WIDTH] * (C_HEAD_DIM ** -0.5)
    vall = zc_ref[:, 2 * C_WIDTH:3 * C_WIDTH].astype(BF16)
    gate_o = _sigmoid(zc_ref[:, 3 * C_WIDTH:4 * C_WIDTH]) * _silu(zc_ref[:, 4 * C_WIDTH:5 * C_WIDTH])
    pump()
    ifp = zc_ref[:, 5 * C_WIDTH:5 * C_WIDTH + LANES]
    lf = _log_sigmoid(ifp + fb_ref[...])
    pump()
    cl = MLSTM_CHUNK
    hds = range(C_HEADS)
    lane_c = lax.broadcasted_iota(jnp.int32, (cl, LANES), 1)
    lane_1 = lax.broadcasted_iota(jnp.int32, (1, LANES), 1)
    m_row = m_ref[...]
    cum_all = _dot_exact01(tri01_ref[...], lf)
    st_col = jnp.where(lane_c < C_HEADS, ifp, cum_all)
    st_row = st_col.T
    pump()
    hs = [slice(hd * C_HEAD_DIM, (hd + 1) * C_HEAD_DIM) for hd in hds]
    i_c = [st_col[:, hd:hd + 1] for hd in hds]
    cum_c = [st_col[:, C_HEADS + hd:C_HEADS + hd + 1] for hd in hds]
    i_r = [st_row[hd:hd + 1, :] for hd in hds]
    cum_r = [st_row[C_HEADS + hd:C_HEADS + hd + 1, :] for hd in hds]
    m_prev = [m_row[:, hd:hd + 1] for hd in hds]
    tribias = tribias_ref[...]
    dmat = [cum_c[hd] - cum_r[hd] + i_r[hd] + tribias for hd in hds]
    pump()
    m_inter = [cum_c[hd] + m_prev[hd] for hd in hds]
    m_t = [jnp.maximum(m_inter[hd], jnp.max(dmat[hd], axis=-1, keepdims=True)) for hd in hds]
    pump()
    q_h = [qall[:, hs[hd]] for hd in hds]
    k_h = [kall[:, hs[hd]] for hd in hds]
    v_h = [vall[:, hs[hd]] for hd in hds]
    s_qk = [_dot_nt(q_h[hd], k_h[hd].astype(BF16)) for hd in hds]
    pump()
    a = [jnp.exp(dmat[hd] - m_t[hd]) * s_qk[hd] for hd in hds]
    pump()
    w_inter = [jnp.exp(m_inter[hd] - m_t[hd]) for hd in hds]
    c_prev = [c_ref[hd] for hd in hds]
    n_prev = [n_ref[hd:hd + 1, :] for hd in hds]
    inter = [_dot(q_h[hd], c_prev[hd].astype(BF16)) for hd in hds]
    pump()
    intra = [_dot(a[hd].astype(BF16), v_h[hd]) for hd in hds]
    pump()
    den = [jnp.sum(a[hd], axis=-1, keepdims=True)
           + w_inter[hd] * jnp.sum(q_h[hd].astype(F32) * n_prev[hd], axis=-1, keepdims=True)
           for hd in hds]
    pump()
    rnorm = [1.0 / jnp.maximum(jnp.abs(den[hd]), jnp.exp(-m_t[hd])) for hd in hds]
    hh = [(intra[hd] + w_inter[hd] * inter[hd]) * rnorm[hd] for hd in hds]
    pump()
    hn = jnp.concatenate([_rms(hh[hd]) for hd in hds], axis=1) * hg_ref[...]
    y_ref[:, A_WIDTH + B_WIDTH:Y_W] = (hn * gate_o).astype(BF16)
    pump()
    total = [cum_r[hd][:, cl - 1:cl] for hd in hds]
    g_r = [total[hd] - cum_r[hd] + i_r[hd] for hd in hds]
    g_c = [total[hd] - cum_c[hd] + i_c[hd] for hd in hds]
    m_new = [jnp.maximum(total[hd] + m_prev[hd], jnp.max(g_r[hd], axis=-1, keepdims=True))
             for hd in hds]
    pump()
    kw = [jnp.exp(g_c[hd] - m_new[hd]) * k_h[hd] for hd in hds]
    decay = [jnp.exp(total[hd] + m_prev[hd] - m_new[hd]) for hd in hds]
    pump()
    upd = [_dot(kw[hd].T.astype(BF16), v_h[hd]) for hd in hds]
    pump()
    for hd in hds:
        c_ref[hd] = decay[hd] * c_prev[hd] + upd[hd]
        n_ref[hd:hd + 1, :] = decay[hd] * n_prev[hd] + jnp.sum(kw[hd], axis=0, keepdims=True)
        m_row = jnp.where(lane_1 == hd, m_new[hd], m_row)
    m_ref[...] = m_row


def _prompt_mask_constants():
    r = np.arange(WINDOW)[:, None]
    c = np.arange(2 * WINDOW)[None, :]
    band = (c > r) & (c <= r + WINDOW)
    band_first = band & (c >= WINDOW)
    band_bias = np.where(np.stack([band, band_first]), 0.0, NEG).astype(np.float32)
    tril = (np.arange(WINDOW)[:, None] >= np.arange(WINDOW)[None, :]).astype(np.float32)
    tri = np.arange(MLSTM_CHUNK)[:, None] >= np.arange(MLSTM_CHUNK)[None, :]
    return (jnp.asarray(tril), jnp.asarray(band_bias), jnp.asarray(tri, dtype=BF16),
            jnp.asarray(np.where(tri, 0.0, NEG).astype(np.float32)))


N_MIX_PARAMS = 13
MIX_PUMP_CALLS = 31
TAIL_FILL_PIECES = 8
MXU_PIECE_COLS = 256


class _Interleaver:
    def __init__(self, pieces, calls, hold_back=0):
        self._pieces = list(pieces)
        self._hold_back = hold_back
        self._spread = len(self._pieces) - hold_back
        self._emitted = 0
        self._calls = calls
        self._call = 0

    def __call__(self):
        self._call += 1
        target = (self._call * self._spread) // self._calls
        while self._emitted < target:
            self._pieces.pop(0)()
            self._emitted += 1

    def finish(self):
        assert self._call == self._calls and len(self._pieces) == self._hold_back, self._call
        return self._pieces


def _gate_pieces(h_ref, wmg_ref, bmg_ref, g_ref):
    def piece(off):
        cols = slice(off, off + MXU_PIECE_COLS)
        def run():
            g_ref[:, cols] = _sigmoid(_dot(h_ref[...], wmg_ref[:, cols]) + bmg_ref[:, cols])
        return run
    return [piece(off) for off in range(0, 3 * D_MODEL, MXU_PIECE_COLS)]


def _merge_and_project(x, mod_ref, g_ref, y_ref, wa_ref, wb_ref, wc_ref, wo_ref, fillers=()):
    fillers = list(fillers)
    per_stage = -(-len(fillers) // 4)
    merged = None
    for i, wbr_ref in enumerate((wa_ref, wb_ref, wc_ref)):
        for piece in fillers[i * per_stage:(i + 1) * per_stage]:
            piece()
        term = (g_ref[:, i * D_MODEL:(i + 1) * D_MODEL]
                * _dot(y_ref[:, i * A_WIDTH:(i + 1) * A_WIDTH], wbr_ref[...]))
        merged = term if merged is None else merged + term
    for piece in fillers[3 * per_stage:]:
        piece()
    ada_gate = mod_ref[:, 2 * D_MODEL:3 * D_MODEL]
    return x + ada_gate * _dot(merged.astype(BF16), wo_ref[...])


def _prompt_layer_kernel(tiles_per_seq, sink_ref, x2_ref, xn_ref, mod_ref, modn_ref, ng_ref,
                         wcat_ref, bcat_ref, *rest):
    mix_params = rest[:N_MIX_PARAMS]
    wmg_ref, bmg_ref, wa_ref, wb_ref, wc_ref, wo_ref = rest[N_MIX_PARAMS:N_MIX_PARAMS + 6]
    o_ref, ko_ref, vo_ref, convo_ref, c_ref, n_ref, m_ref = rest[N_MIX_PARAMS + 6:N_MIX_PARAMS + 13]
    (za0, zb0, zc0, za1, zb1, zc1, h0, h1, y_scr, g_scr, kprev, vprev, xbuf) = rest[N_MIX_PARAMS + 13:]
    ts = PROMPT_TILE
    z = ((za0, zb0, zc0), (za1, zb1, zc1))
    h = (h0, h1)
    k = pl.program_id(0)
    seq_start = (k % (tiles_per_seq // 2)) == 0

    @pl.when(k == 0)
    def _():
        h0[...] = _modulated_norm(x2_ref[0:ts, :], mod_ref, ng_ref)
        for piece in _inproj_pieces(lambda: h0[...], wcat_ref, bcat_ref, *z[0], 512):
            piece()

    @pl.when(seq_start)
    def _():
        kprev[...] = jnp.zeros_like(kprev)
        vprev[...] = jnp.zeros_like(vprev)
        xbuf[0:SUBLANES, :] = jnp.zeros((SUBLANES, 2 * C_WIDTH), F32)
        c_ref[...] = jnp.zeros_like(c_ref)
        n_ref[...] = jnp.zeros_like(n_ref)
        m_ref[...] = jnp.zeros_like(m_ref)

    for half in range(2):
        cur, nxt = half, 1 - half
        rows = slice(half * ts, (half + 1) * ts)
        if half == 0:
            h[nxt][...] = _modulated_norm(x2_ref[ts:2 * ts, :], mod_ref, ng_ref)
        else:
            h[nxt][...] = _modulated_norm(xn_ref[...], modn_ref, ng_ref)
        get_h_next = functools.partial(lambda r: r[...], h[nxt])
        hold = TAIL_FILL_PIECES if half == 1 else 0
        proj = _inproj_pieces(get_h_next, wcat_ref, bcat_ref, *z[nxt], MXU_PIECE_COLS)
        pump = _Interleaver(
            proj[:len(proj) - hold] + _gate_pieces(h[cur], wmg_ref, bmg_ref, g_scr)
            + proj[len(proj) - hold:], MIX_PUMP_CALLS, hold_back=hold)
        _prompt_mix_kernel(sink_ref, *z[cur], *mix_params,
                           y_scr, ko_ref, vo_ref, convo_ref, c_ref, n_ref, m_ref, kprev, vprev, xbuf,
                           first_tile=seq_start if half == 0 else False, pump=pump)
        o_ref[rows, :] = _merge_and_project(x2_ref[rows, :], mod_ref, g_scr, y_scr,
                                            wa_ref, wb_ref, wc_ref, wo_ref, fillers=pump.finish())


def _prompt_layer_call(layer, x2, mod, lw, batch, seq):
    ts = PROMPT_TILE
    nt = seq // ts
    assert nt % 2 == 0
    last_tile = batch * nt - 1
    const2 = lambda k: (0, 0)
    const3 = lambda k: (0, 0, 0)
    per_b3 = lambda k: ((2 * k) // nt, 0, 0)
    next_tile = lambda k: jnp.minimum(2 * k + 2, last_tile)
    once = pl.Buffered(1)
    return pl.pallas_call(
        functools.partial(_prompt_layer_kernel, nt),
        grid=(batch * nt // 2,),
        in_specs=[
            pl.BlockSpec(memory_space=pltpu.SMEM),
            pl.BlockSpec((2 * ts, D_MODEL), lambda k: (k, 0)),
            pl.BlockSpec((ts, D_MODEL), lambda k: (next_tile(k), 0)),
            pl.BlockSpec((None, 1, 3 * D_MODEL), per_b3),
            pl.BlockSpec((None, 1, 3 * D_MODEL), lambda k: (next_tile(k) // nt, 0, 0)),
            pl.BlockSpec((1, D_MODEL), const2),
            _layer_weight_spec(layer, D_MODEL, ZCAT_W),
            pl.BlockSpec((1, ZCAT_W), const2),
            pl.BlockSpec((1, A_WIDTH), const2),
            pl.BlockSpec((A_GROUPS, WINDOW, WINDOW), const3),
            pl.BlockSpec((WINDOW, LANES), const2),
            pl.BlockSpec((1, B_WIDTH), const2),
            pl.BlockSpec((1, B_KV_WIDTH), const2),
            pl.BlockSpec((C_CONV, 2 * C_WIDTH), const2),
            pl.BlockSpec((1, 2 * C_WIDTH), const2),
            pl.BlockSpec((1, LANES), const2),
            pl.BlockSpec((1, C_WIDTH), const2),
            pl.BlockSpec((WINDOW, WINDOW), const2),
            pl.BlockSpec((2, WINDOW, 2 * WINDOW), const3),
            pl.BlockSpec((MLSTM_CHUNK, MLSTM_CHUNK), const2),
            pl.BlockSpec((MLSTM_CHUNK, MLSTM_CHUNK), const2),
            _layer_weight_spec(layer, D_MODEL, 3 * D_MODEL),
            pl.BlockSpec((1, 3 * D_MODEL), const2),
            pl.BlockSpec((A_WIDTH, D_MODEL), const2, pipeline_mode=once),
            pl.BlockSpec((B_WIDTH, D_MODEL), const2, pipeline_mode=once),
            pl.BlockSpec((C_WIDTH, D_MODEL), const2, pipeline_mode=once),
            pl.BlockSpec((D_MODEL, D_MODEL), const2, pipeline_mode=once),
        ],
        out_specs=[
            pl.BlockSpec((2 * ts, D_MODEL), lambda k: (k, 0)),
            pl.BlockSpec((None, WINDOW, B_KV_WIDTH), per_b3),
            pl.BlockSpec((None, WINDOW, B_KV_WIDTH), per_b3),
            pl.BlockSpec((None, SUBLANES, 2 * C_WIDTH), per_b3),
            pl.BlockSpec((None, C_HEADS, C_HEAD_DIM, C_HEAD_DIM), lambda k: ((2 * k) // nt, 0, 0, 0)),
            pl.BlockSpec((None, C_HEADS, C_HEAD_DIM), per_b3),
            pl.BlockSpec((None, 1, LANES), per_b3),
        ],
        out_shape=[
            jax.ShapeDtypeStruct((batch * seq, D_MODEL), F32),
            jax.ShapeDtypeStruct((batch, WINDOW, B_KV_WIDTH), F32),
            jax.ShapeDtypeStruct((batch, WINDOW, B_KV_WIDTH), F32),
            jax.ShapeDtypeStruct((batch, SUBLANES, 2 * C_WIDTH), F32),
            jax.ShapeDtypeStruct((batch, C_HEADS, C_HEAD_DIM, C_HEAD_DIM), F32),
            jax.ShapeDtypeStruct((batch, C_HEADS, C_HEAD_DIM), F32),
            jax.ShapeDtypeStruct((batch, 1, LANES), F32),
        ],
        scratch_shapes=(
            [pltpu.VMEM((ts, w), F32) for w in (ZA_W, ZB_W, ZC_W)] * 2
            + [pltpu.VMEM((ts, D_MODEL), BF16)] * 2
            + [pltpu.VMEM((ts, Y_W), BF16),
               pltpu.VMEM((ts, 3 * D_MODEL), F32),
               pltpu.VMEM((WINDOW, B_KV_WIDTH), F32),
               pltpu.VMEM((WINDOW, B_KV_WIDTH), F32),
               pltpu.VMEM((ts + SUBLANES, 2 * C_WIDTH), F32)]),
        compiler_params=pltpu.CompilerParams(
            dimension_semantics=("arbitrary",), vmem_limit_bytes=VMEM_LIMIT),
        name="prompt_layer",
    )(lw["sinks"], x2, x2, mod, mod, lw["ng"], lw["wcat"], lw["bcat"],
      lw["vg"], lw["gws"], lw["gbs_col"], lw["qg"], lw["kg"], lw["cw"], lw["cb"], lw["fb"], lw["hg"],
      *_prompt_mask_constants(),
      lw["wmg"], lw["bmg"], lw["wa"], lw["wb"], lw["wc"], lw["wo"])


def _sample_mix_kernel(sink_ref, za_ref, zb_ref, zc_ref, kc_ref, vc_ref, cs_ref, c0_ref, n0_ref,
                       m0_ref, vg_ref, gwb_ref, gbs_ref, qg_ref, kg_ref, cw_ref, cb_ref, fb_ref,
                       hg_ref,
                       y_ref, vrow_ref, ko_ref, vo_ref, convo_ref, c1_ref, n1_ref, m1_ref,
                       xbuf):
    nb = SAMPLE_NB
    t = SUBLANES
    rows = nb * t
    tok_r = lax.broadcasted_iota(jnp.int32, (rows, rows), 0)
    tok_c = lax.broadcasted_iota(jnp.int32, (rows, rows), 1)
    same_b = (tok_r // t) == (tok_c // t)
    causal_b = same_b & (tok_c <= tok_r)

    u = za_ref[:, 0:A_WIDTH]
    vn = _rms(za_ref[:, A_WIDTH:2 * A_WIDTH]) * vg_ref[...]
    sg = _silu(za_ref[:, 2 * A_WIDTH:3 * A_WIDTH])
    vrow_ref[...] = vn
    vnb = vn.astype(BF16)
    s_cols = []
    for gi in range(A_GROUPS):
        s_cols.append(_dot(gwb_ref[gi], vnb[:, gi * GROUP_DIM:(gi + 1) * GROUP_DIM])
                      + gbs_ref[:, gi:gi + 1])
    y_ref[:, 0:A_WIDTH] = (u * jnp.concatenate(s_cols, axis=1) * sg).astype(BF16)

    qn = _qk_norm(zb_ref[:, 0:B_WIDTH], qg_ref[...]) * (B_HEAD_DIM ** -0.5)
    kn = _qk_norm(zb_ref[:, B_WIDTH:B_WIDTH + B_KV_WIDTH], kg_ref[...])
    vv = zb_ref[:, B_WIDTH + B_KV_WIDTH:B_WIDTH + 2 * B_KV_WIDTH]
    sgb = _silu(zb_ref[:, B_WIDTH + 2 * B_KV_WIDTH:ZB_W])
    kn3 = kn.reshape(nb, t, B_KV_WIDTH)
    vv3 = vv.reshape(nb, t, B_KV_WIDTH)
    kcache = kc_ref[...]
    vcache = vc_ref[...]
    pad = jnp.zeros((nb, WINDOW - t, B_KV_WIDTH), F32)
    kall = jnp.concatenate([kcache, kn3, pad], axis=1).astype(BF16)
    vall = jnp.concatenate([vcache, vv3, pad], axis=1).astype(BF16)
    qp = jnp.concatenate([_place_q_head(qn, h, rows).reshape(nb, t, LANES) for h in range(B_HEADS)],
                         axis=1).astype(BF16)
    logits = lax.dot_general(qp, kall, (((2,), (2,)), ((0,), (0,))), preferred_element_type=F32)
    qrow = lax.broadcasted_iota(jnp.int32, (nb, B_HEADS * t, 2 * WINDOW), 1)
    kcol = lax.broadcasted_iota(jnp.int32, (nb, B_HEADS * t, 2 * WINDOW), 2)
    qt = qrow % t
    valid = ((kcol < WINDOW) & (kcol > qt)) | ((kcol >= WINDOW) & ((kcol - WINDOW) <= qt))
    hrow = lax.broadcasted_iota(jnp.int32, (B_HEADS * t, 1), 0) // t
    snk = jnp.zeros((B_HEADS * t, 1), F32)
    for h in range(B_HEADS):
        snk = jnp.where(hrow == h, sink_ref[h], snk)
    lg = jnp.where(valid, logits, NEG)
    mx = jnp.maximum(jnp.max(lg, axis=-1, keepdims=True), snk[None])
    p = jnp.exp(lg - mx)
    den = jnp.sum(p, axis=-1, keepdims=True) + jnp.exp(snk[None] - mx)
    pv = lax.dot_general(p.astype(BF16), vall, (((2,), (1,)), ((0,), (0,))),
                         preferred_element_type=F32) / den
    head_out = [pv[:, h * t:(h + 1) * t, :].reshape(rows, LANES) for h in range(B_HEADS)]
    yb = jnp.concatenate(
        [_merge_head_pair(head_out[2 * j], head_out[2 * j + 1], 2 * j, rows)
         for j in range(B_HEADS // 2)], axis=1)
    y_ref[:, A_WIDTH:A_WIDTH + B_WIDTH] = (yb * sgb).astype(BF16)
    ko_ref[...] = jnp.concatenate([kcache[:, t:, :], kn3], axis=1)
    vo_ref[...] = jnp.concatenate([vcache[:, t:, :], vv3], axis=1)

    xbuf[:, SUBLANES - (C_CONV - 1):SUBLANES, :] = cs_ref[...]
    xbuf[:, SUBLANES:2 * SUBLANES, :] = zc_ref[:, 0:2 * C_WIDTH].reshape(nb, t, 2 * C_WIDTH)
    y3 = cb_ref[...][None]
    for j in range(C_CONV):
        lo = SUBLANES - (C_CONV - 1) + j
        y3 = y3 + cw_ref[j:j + 1, :][None] * xbuf[:, lo:lo + t, :]
    convo_ref[...] = xbuf[:, 2 * SUBLANES - (C_CONV - 1):2 * SUBLANES, :]
    qk = _silu(y3.reshape(rows, 2 * C_WIDTH))
    qall = qk[:, 0:C_WIDTH].astype(BF16)
    kall_c = qk[:, C_WIDTH:2 * C_WIDTH] * (C_HEAD_DIM ** -0.5)
    vall_c = zc_ref[:, 2 * C_WIDTH:3 * C_WIDTH].astype(BF16)
    gate_o = _sigmoid(zc_ref[:, 3 * C_WIDTH:4 * C_WIDTH]) * _silu(zc_ref[:, 4 * C_WIDTH:5 * C_WIDTH])
    ifp = zc_ref[:, 5 * C_WIDTH:5 * C_WIDTH + LANES]
    lf = _log_sigmoid(ifp + fb_ref[...])
    lane_t = lax.broadcasted_iota(jnp.int32, (rows, LANES), 1)
    cum_all = _dot_exact01(jnp.where(causal_b, 1.0, 0.0).astype(BF16), lf)
    tot_all = _dot_exact01(jnp.where(same_b, 1.0, 0.0).astype(BF16), lf)
    st_col = jnp.where(lane_t < C_HEADS, ifp, cum_all)
    st_row = st_col.T
    tot_row = tot_all.T
    m0 = m0_ref[...]
    same_b_bf = jnp.where(same_b, 1.0, 0.0).astype(BF16)
    batch_of_lane = lax.broadcasted_iota(jnp.int32, (nb, 1, rows), 2) // t
    batch_id = lax.broadcasted_iota(jnp.int32, (nb, 1, rows), 0)
    own_tok = batch_of_lane == batch_id
    h_cols = []
    m_out = jnp.zeros((rows, LANES), F32)
    for hd in range(C_HEADS):
        hs = slice(hd * C_HEAD_DIM, (hd + 1) * C_HEAD_DIM)
        i_c = st_col[:, hd:hd + 1]
        cum_c = st_col[:, C_HEADS + hd:C_HEADS + hd + 1]
        tot_c = tot_all[:, C_HEADS + hd:C_HEADS + hd + 1]
        i_r = st_row[hd:hd + 1, :]
        cum_r = st_row[C_HEADS + hd:C_HEADS + hd + 1, :]
        tot_r = tot_row[C_HEADS + hd:C_HEADS + hd + 1, :]
        m_prev = m0[:, hd:hd + 1]
        dmat = jnp.where(causal_b, cum_c - cum_r + i_r, NEG)
        m_inter = cum_c + m_prev
        m_t = jnp.maximum(m_inter, jnp.max(dmat, axis=-1, keepdims=True))
        q_h = qall[:, hs]
        k_h = kall_c[:, hs]
        v_h = vall_c[:, hs]
        a = jnp.exp(dmat - m_t) * _dot_nt(q_h, k_h.astype(BF16))
        w_inter = jnp.exp(m_inter - m_t)
        c_prev = c0_ref[:, hd]
        n_tok = jnp.broadcast_to(n0_ref[hd][:, None, :], (nb, t, C_HEAD_DIM)).reshape(rows, C_HEAD_DIM)
        inter = lax.dot_general(q_h.reshape(nb, t, C_HEAD_DIM), c_prev.astype(BF16),
                                (((2,), (1,)), ((0,), (0,))), preferred_element_type=F32)
        num = _dot(a.astype(BF16), v_h) + w_inter * inter.reshape(rows, C_HEAD_DIM)
        den = (jnp.sum(a, axis=-1, keepdims=True)
               + w_inter * jnp.sum(q_h.astype(F32) * n_tok, axis=-1, keepdims=True))
        hh = num / jnp.maximum(jnp.abs(den), jnp.exp(-m_t))
        h_cols.append(_rms(hh))
        g_r = tot_r - cum_r + i_r
        g_c = tot_c - cum_c + i_c
        m_new = jnp.maximum(tot_c + m_prev,
                            jnp.max(jnp.where(same_b, g_r, NEG), axis=-1, keepdims=True))
        kw = jnp.exp(g_c - m_new) * k_h
        decay = jnp.exp(tot_c + m_prev - m_new)
        kwt = kw.T
        lhs = jnp.where(own_tok, kwt[None], 0.0).astype(BF16).reshape(nb * C_HEAD_DIM, rows)
        upd = _dot(lhs, v_h).reshape(nb, C_HEAD_DIM, C_HEAD_DIM)
        dec_b = jnp.broadcast_to(decay, (rows, C_HEAD_DIM)).reshape(nb, t, C_HEAD_DIM)[:, 0:1, :]
        c1_ref[:, hd] = dec_b * c_prev + upd
        n1_ref[hd] = decay * n_tok + _dot(same_b_bf, kw.astype(BF16))
        m_out = jnp.where(lane_t == hd, m_new, m_out)
    m1_ref[...] = m_out
    hn = jnp.concatenate(h_cols, axis=1) * hg_ref[...]
    y_ref[:, A_WIDTH + B_WIDTH:Y_W] = (hn * gate_o).astype(BF16)


def _sample_mix_call(l, za, zb, zc, kc, vc, cs, c0, n0t, m0tok, lw, nbatch):
    nb = SAMPLE_NB
    t = SUBLANES
    rows = nb * t
    tok = lambda i: (i, 0)
    const2 = lambda i: (0, 0)
    const3 = lambda i: (0, 0, 0)
    b3 = lambda i: (i, 0, 0)
    lb4 = lambda i: (l, i, 0, 0)
    return pl.pallas_call(
        _sample_mix_kernel,
        grid=(nbatch // nb,),
        in_specs=[
            pl.BlockSpec(memory_space=pltpu.SMEM),
            pl.BlockSpec((rows, ZA_W), tok),
            pl.BlockSpec((rows, ZB_W), tok),
            pl.BlockSpec((rows, ZC_W), tok),
            pl.BlockSpec((None, nb, WINDOW, B_KV_WIDTH), lb4),
            pl.BlockSpec((None, nb, WINDOW, B_KV_WIDTH), lb4),
            pl.BlockSpec((None, nb, C_CONV - 1, 2 * C_WIDTH), lb4),
            pl.BlockSpec((None, nb, C_HEADS, C_HEAD_DIM, C_HEAD_DIM), lambda i: (l, i, 0, 0, 0)),
            pl.BlockSpec((None, C_HEADS, nb, C_HEAD_DIM), lambda i: (l, 0, i, 0)),
            pl.BlockSpec((None, rows, LANES), lambda i: (l, i, 0)),
            pl.BlockSpec((1, A_WIDTH), const2),
            pl.BlockSpec((A_GROUPS, rows, rows), const3),
            pl.BlockSpec((rows, LANES), const2),
            pl.BlockSpec((1, B_WIDTH), const2),
            pl.BlockSpec((1, B_KV_WIDTH), const2),
            pl.BlockSpec((C_CONV, 2 * C_WIDTH), const2),
            pl.BlockSpec((1, 2 * C_WIDTH), const2),
            pl.BlockSpec((1, LANES), const2),
            pl.BlockSpec((1, C_WIDTH), const2),
        ],
        out_specs=[
            pl.BlockSpec((rows, Y_W), tok),
            pl.BlockSpec((rows, A_WIDTH), tok),
            pl.BlockSpec((nb, WINDOW, B_KV_WIDTH), b3),
            pl.BlockSpec((nb, WINDOW, B_KV_WIDTH), b3),
            pl.BlockSpec((nb, C_CONV - 1, 2 * C_WIDTH), b3),
            pl.BlockSpec((nb, C_HEADS, C_HEAD_DIM, C_HEAD_DIM), lambda i: (i, 0, 0, 0)),
            pl.BlockSpec((C_HEADS, rows, C_HEAD_DIM), lambda i: (0, i, 0)),
            pl.BlockSpec((rows, LANES), tok),
        ],
        out_shape=[
            jax.ShapeDtypeStruct((nbatch * t, Y_W), BF16),
            jax.ShapeDtypeStruct((nbatch * t, A_WIDTH), F32),
            jax.ShapeDtypeStruct((nbatch, WINDOW, B_KV_WIDTH), F32),
            jax.ShapeDtypeStruct((nbatch, WINDOW, B_KV_WIDTH), F32),
            jax.ShapeDtypeStruct((nbatch, C_CONV - 1, 2 * C_WIDTH), F32),
            jax.ShapeDtypeStruct((nbatch, C_HEADS, C_HEAD_DIM, C_HEAD_DIM), F32),
            jax.ShapeDtypeStruct((C_HEADS, nbatch * t, C_HEAD_DIM), F32),
            jax.ShapeDtypeStruct((nbatch * t, LANES), F32),
        ],
        scratch_shapes=[pltpu.VMEM((nb, 2 * SUBLANES, 2 * C_WIDTH), F32)],
        compiler_params=pltpu.CompilerParams(
            dimension_semantics=("arbitrary",), vmem_limit_bytes=VMEM_LIMIT),
        name="sample_mixer",
    )(lw["sinks"], za, zb, zc, kc, vc, cs, c0, n0t, m0tok, lw["vg"], lw["gwb"], lw["gbs_tok"],
      lw["qg"], lw["kg"], lw["cw"], lw["cb"], lw["fb"], lw["hg"])


def _layer_weights(l, wcat_all, wmg_all, b_in, gmlp_vnorm_g, gmlp_ws, gmlp_bs, swa_qnorm_g,
                   swa_knorm_g, swa_sinks, mlstm_conv_w, mlstm_conv_b, mlstm_f_bias, mlstm_hnorm_g,
                   w_branch_a, w_branch_b, w_branch_c, w_out, norm_g, dec_seq):
    bl = b_in[l]
    bcat = jnp.concatenate([bl[:COL_CI], bl[COL_CO:COL_MG], bl[COL_CI:COL_CO],
                            jnp.zeros((LANES - 2 * C_HEADS,), F32)])
    t = dec_seq
    nb = SAMPLE_NB
    ws_t = gmlp_ws[l][:, :t, :t] * jnp.tril(jnp.ones((t, t), F32))
    eye = jnp.eye(nb, dtype=F32)
    gwb = jnp.einsum("bc,gts->gbtcs", eye, ws_t).reshape(A_GROUPS, nb * t, nb * t).astype(BF16)
    gbs_col = jnp.pad(gmlp_bs[l].T, ((0, 0), (0, LANES - A_GROUPS)))
    gbs_tok = jnp.pad(jnp.tile(gmlp_bs[l][:, :t].T, (nb, 1)), ((0, 0), (0, LANES - A_GROUPS)))
    fb = jnp.pad(mlstm_f_bias[l], (C_HEADS, LANES - 2 * C_HEADS)).reshape(1, LANES)
    return dict(
        ng=norm_g[l].reshape(1, D_MODEL),
        wcat=wcat_all, bcat=bcat.reshape(1, ZCAT_W),
        wmg=wmg_all, bmg=bl[COL_MG:].reshape(1, 3 * D_MODEL),
        wa=w_branch_a[l].astype(BF16), wb=w_branch_b[l].astype(BF16),
        wc=w_branch_c[l].astype(BF16), wo=w_out[l].astype(BF16),
        vg=gmlp_vnorm_g[l].reshape(1, A_WIDTH), gws=gmlp_ws[l], gwb=gwb,
        gbs_col=gbs_col, gbs_tok=gbs_tok,
        qg=jnp.tile(swa_qnorm_g[l], B_HEADS).reshape(1, B_WIDTH),
        kg=jnp.tile(swa_knorm_g[l], B_KV_HEADS).reshape(1, B_KV_WIDTH),
        sinks=swa_sinks[l],
        cw=mlstm_conv_w[l], cb=mlstm_conv_b[l].reshape(1, 2 * C_WIDTH), fb=fb,
        hg=mlstm_hnorm_g[l].reshape(1, C_WIDTH),
    )


def kernel(x_prompt, x_sample, cache_swa_k, cache_swa_v, state_mlstm_conv, state_mlstm_C, state_mlstm_n, state_mlstm_m, c_prompt, c_sample, ada_w, ada_b, norm_g, w_in, b_in, gmlp_vnorm_g, gmlp_ws, gmlp_bs, swa_qnorm_g, swa_knorm_g, swa_sinks, mlstm_conv_w, mlstm_conv_b, mlstm_f_bias, mlstm_hnorm_g, w_branch_a, w_branch_b, w_branch_c, w_out):
    batch, seq, _ = x_prompt.shape
    nbatch, dec_seq, _ = x_sample.shape
    assert dec_seq == SUBLANES and seq % PROMPT_TILE == 0 and nbatch % SAMPLE_NB == 0
    assert seq % PROJ_TILE == 0 and (nbatch * dec_seq) % PROJ_TILE == 0
    wb_len = cache_swa_k.shape[2]
    assert wb_len == WINDOW

    nc = batch + nbatch
    nc_pad = -(-nc // SUBLANES) * SUBLANES
    c_all = jnp.concatenate([c_prompt, c_sample, jnp.zeros((nc_pad - nc, D_MODEL), F32)], axis=0)
    mod_all = _ada_call(c_all, ada_w, ada_b)

    xp = x_prompt.reshape(batch * seq, D_MODEL)
    xs = x_sample.reshape(nbatch * dec_seq, D_MODEL)
    kc_all = cache_swa_k.reshape(DEPTH, nbatch, WINDOW, B_KV_WIDTH)
    vc_all = cache_swa_v.reshape(DEPTH, nbatch, WINDOW, B_KV_WIDTH)
    n0t_all = jnp.transpose(state_mlstm_n, (0, 2, 1, 3))
    m0tok_all = jnp.pad(jnp.repeat(state_mlstm_m, dec_seq, axis=1),
                        ((0, 0), (0, 0), (0, LANES - C_HEADS)))
    wcat_all, wmg_all = _weight_prep_call(w_in)
    outs_p = [[] for _ in range(6)]
    outs_s = [[] for _ in range(6)]
    vrows = []
    for l in range(DEPTH):
        lw = _layer_weights(l, wcat_all, wmg_all, b_in, gmlp_vnorm_g, gmlp_ws, gmlp_bs, swa_qnorm_g,
                            swa_knorm_g, swa_sinks, mlstm_conv_w, mlstm_conv_b, mlstm_f_bias,
                            mlstm_hnorm_g, w_branch_a, w_branch_b, w_branch_c, w_out, norm_g,
                            dec_seq)
        mod_p = mod_all[l, :batch].reshape(batch, 1, 3 * D_MODEL)
        mod_s = jnp.repeat(mod_all[l, batch:nc], dec_seq, axis=0)

        xp, ko, vo, convo, c1, n1, m1 = _prompt_layer_call(l, xp, mod_p, lw, batch, seq)
        outs_p[0].append(ko.reshape(batch, WINDOW, B_KV_HEADS, B_HEAD_DIM))
        outs_p[1].append(vo.reshape(batch, WINDOW, B_KV_HEADS, B_HEAD_DIM))
        outs_p[2].append(convo[:, SUBLANES - (C_CONV - 1):, :])
        outs_p[3].append(c1)
        outs_p[4].append(n1)
        outs_p[5].append(m1[:, 0, :C_HEADS])

        za, zb, zc = _inproj_call(l, xs, mod_s, lw["ng"], lw["wcat"], lw["bcat"], None)
        y, vrow, ko, vo, convo, c1, n1tok, m1tok = _sample_mix_call(
            l, za, zb, zc, kc_all, vc_all, state_mlstm_conv, state_mlstm_C, n0t_all, m0tok_all,
            lw, nbatch)
        xs = _outproj_call(l, xs, mod_s, lw["ng"], y, lw["wmg"], lw["bmg"], lw["wa"], lw["wb"],
                           lw["wc"], lw["wo"], None)
        outs_s[0].append(ko.reshape(nbatch, WINDOW, B_KV_HEADS, B_HEAD_DIM))
        outs_s[1].append(vo.reshape(nbatch, WINDOW, B_KV_HEADS, B_HEAD_DIM))
        outs_s[2].append(convo)
        outs_s[3].append(c1)
        outs_s[4].append(jnp.transpose(n1tok[:, ::dec_seq, :], (1, 0, 2)))
        outs_s[5].append(m1tok[::dec_seq, :C_HEADS])
        vrows.append(vrow.reshape(nbatch, dec_seq, A_WIDTH))

    sp = [jnp.stack(o) for o in outs_p]
    ss = [jnp.stack(o) for o in outs_s]
    return (xp.reshape(batch, seq, D_MODEL), xs.reshape(nbatch, dec_seq, D_MODEL),
            sp[0], sp[1], sp[2], sp[3], sp[4], sp[5],
            ss[0], ss[1], ss[2], ss[3], ss[4], ss[5], jnp.stack(vrows))
```

```python
import functools

import numpy as np
import jax
import jax.numpy as jnp
from jax import lax
from jax.experimental import pallas as pl
from jax.experimental.pallas import tpu as pltpu

F32 = jnp.float32
BF16 = jnp.bfloat16

D_MODEL = 1024
DEPTH = 2
A_WIDTH = 512
A_GROUPS = 4
GROUP_DIM = 128
B_HEADS = 8
B_KV_HEADS = 2
B_HEAD_DIM = 64
B_WIDTH = 512
B_KV_WIDTH = 128
WINDOW = 128
C_HEADS = 4
C_HEAD_DIM = 128
C_WIDTH = 512
C_CONV = 4
EPS = 1e-6
NEG = -1e30

LANES = 128
SUBLANES = 8
VMEM_LIMIT = 56 * 1024 * 1024

ZA_W = 3 * A_WIDTH
ZB_W = 2 * B_WIDTH + 2 * B_KV_WIDTH
ZC_W = 2 * C_WIDTH + 3 * C_WIDTH + LANES
ZCAT_W = ZA_W + ZB_W + ZC_W
Y_W = A_WIDTH + B_WIDTH + C_WIDTH

PROMPT_TILE = 256
MLSTM_CHUNK = PROMPT_TILE
SAMPLE_NB = 16
PROJ_TILE = 512


def _sigmoid(x):
    return 0.5 * jnp.tanh(0.5 * x) + 0.5


def _silu(x):
    t = 0.5 * x
    return t * (jnp.tanh(t) + 1.0)


def _log_sigmoid(x):
    return jnp.minimum(x, 0.0) - jnp.log1p(jnp.exp(-jnp.abs(x)))


def _rms(x):
    return x * lax.rsqrt(jnp.mean(x * x, axis=-1, keepdims=True) + EPS)


def _dot(a, b):
    return jnp.dot(a, b, preferred_element_type=F32)


def _dot_nt(a, b):
    return lax.dot_general(a, b, (((1,), (1,)), ((), ())), preferred_element_type=F32)


def _dot_exact01(m01, x):
    hi = x.astype(BF16)
    r1 = x - hi.astype(F32)
    mid = r1.astype(BF16)
    lo = (r1 - mid.astype(F32)).astype(BF16)
    return _dot(m01, hi) + _dot(m01, mid) + _dot(m01, lo)


def _modulated_norm(x, mod_ref, ng_ref):
    xn = _rms(x) * ng_ref[...]
    shift = mod_ref[:, 0:D_MODEL]
    scale = mod_ref[:, D_MODEL:2 * D_MODEL]
    return (xn * (1.0 + scale) + shift).astype(BF16)


def _head_rms_scale(x2, lane_lo):
    s0 = jnp.sum(jnp.where(lane_lo, x2, 0.0), axis=-1, keepdims=True)
    s1 = jnp.sum(jnp.where(lane_lo, 0.0, x2), axis=-1, keepdims=True)
    r0 = lax.rsqrt(s0 * (1.0 / B_HEAD_DIM) + EPS)
    r1 = lax.rsqrt(s1 * (1.0 / B_HEAD_DIM) + EPS)
    return jnp.where(lane_lo, r0, r1)


def _qk_norm(x, g_row):
    rows, width = x.shape
    lane_lo = lax.broadcasted_iota(jnp.int32, (rows, LANES), 1) < B_HEAD_DIM
    outs = []
    for j in range(width // LANES):
        slab = x[:, j * LANES:(j + 1) * LANES]
        outs.append(slab * _head_rms_scale(slab * slab, lane_lo))
    y = outs[0] if len(outs) == 1 else jnp.concatenate(outs, axis=1)
    return y * g_row


def _ada_kernel(c_ref, w_ref, b_ref, o_ref):
    c = c_ref[...]
    o_ref[...] = _dot(_silu(c).astype(BF16), w_ref[...].astype(BF16)) + b_ref[...]


def _ada_call(c_all, ada_w, ada_b):
    rows = c_all.shape[0]
    return pl.pallas_call(
        _ada_kernel,
        grid=(DEPTH, 3),
        in_specs=[
            pl.BlockSpec((rows, D_MODEL), lambda l, j: (0, 0)),
            pl.BlockSpec((None, D_MODEL, D_MODEL), lambda l, j: (l, 0, j)),
            pl.BlockSpec((None, 1, D_MODEL), lambda l, j: (l, 0, j)),
        ],
        out_specs=pl.BlockSpec((None, rows, D_MODEL), lambda l, j: (l, 0, j)),
        out_shape=jax.ShapeDtypeStruct((DEPTH, rows, 3 * D_MODEL), F32),
        compiler_params=pltpu.CompilerParams(
            dimension_semantics=("arbitrary", "arbitrary"), vmem_limit_bytes=VMEM_LIMIT),
        name="adaln_mod",
    )(c_all, ada_w, ada_b.reshape(DEPTH, 1, 3 * D_MODEL))


COL_CI = ZA_W + ZB_W + 3 * C_WIDTH
COL_CO = COL_CI + 2 * C_HEADS
COL_MG = COL_CO + 2 * C_WIDTH
PREP_CHUNK = 256
PREP_SHIFT = 2 * C_HEADS
N_MAIN = COL_CI // PREP_CHUNK
N_CO = (2 * C_WIDTH) // PREP_CHUNK
N_MG = (3 * D_MODEL) // PREP_CHUNK
J_CIF = N_MAIN + N_CO
J_MG = J_CIF + 1


def _weight_prep_kernel(wa_ref, wb_ref, wcat_ref, wmg_ref):
    j = pl.program_id(1)
    shifted = jnp.concatenate([wa_ref[PREP_SHIFT:PREP_CHUNK, :], wb_ref[0:PREP_SHIFT, :]], axis=0)

    @pl.when(j < N_MAIN)
    def _():
        wcat_ref[...] = wa_ref[...].T.astype(BF16)

    @pl.when((j >= N_MAIN) & (j < J_CIF))
    def _():
        wcat_ref[...] = shifted.T.astype(BF16)

    @pl.when(j == J_CIF)
    def _():
        row = lax.broadcasted_iota(jnp.int32, (PREP_CHUNK, D_MODEL), 0)
        wcat_ref[...] = jnp.where(row < PREP_SHIFT, wa_ref[...], 0.0).T.astype(BF16)

    @pl.when(j >= J_MG)
    def _():
        wmg_ref[...] = shifted.T.astype(BF16)


def _weight_prep_call(w_in):
    in_width = w_in.shape[-1]
    assert in_width == COL_MG + 3 * D_MODEL
    assert COL_CI % PREP_CHUNK == 0 and COL_CO % PREP_CHUNK == PREP_SHIFT == COL_MG % PREP_CHUNK
    w_t = jnp.swapaxes(w_in, 1, 2)
    last_block = (in_width - 1) // PREP_CHUNK

    def src_block(j):
        return jnp.where(j < J_CIF, j, jnp.where(j == J_CIF, N_MAIN, j - 1))

    return pl.pallas_call(
        _weight_prep_kernel,
        grid=(DEPTH, J_MG + N_MG),
        in_specs=[
            pl.BlockSpec((None, PREP_CHUNK, D_MODEL), lambda l, j: (l, src_block(j), 0)),
            pl.BlockSpec((None, PREP_CHUNK, D_MODEL),
                         lambda l, j: (l, jnp.minimum(src_block(j) + 1, last_block), 0)),
        ],
        out_specs=[
            pl.BlockSpec((None, D_MODEL, PREP_CHUNK), lambda l, j: (l, 0, jnp.minimum(j, J_CIF))),
            pl.BlockSpec((None, D_MODEL, PREP_CHUNK), lambda l, j: (l, 0, jnp.maximum(j - J_MG, 0))),
        ],
        out_shape=[
            jax.ShapeDtypeStruct((DEPTH, D_MODEL, ZCAT_W), BF16),
            jax.ShapeDtypeStruct((DEPTH, D_MODEL, 3 * D_MODEL), BF16),
        ],
        compiler_params=pltpu.CompilerParams(
            dimension_semantics=("arbitrary", "arbitrary"), vmem_limit_bytes=VMEM_LIMIT),
        name="weight_prep",
    )(w_t, w_t)


def _col_chunks(width, step):
    return [(o, min(step, width - o)) for o in range(0, width, step)]


def _inproj_pieces(get_h, w_ref, b_ref, za_ref, zb_ref, zc_ref, step):
    def piece(o_ref, off, woff, w):
        def run():
            o_ref[:, off:off + w] = _dot(get_h(), w_ref[:, woff:woff + w]) + b_ref[:, woff:woff + w]
        return run
    pieces = []
    base = 0
    for o_ref, width in ((za_ref, ZA_W), (zb_ref, ZB_W), (zc_ref, ZC_W)):
        pieces += [piece(o_ref, off, base + off, w) for off, w in _col_chunks(width, step)]
        base += width
    return pieces


def _inproj_kernel(x_ref, mod_ref, ng_ref, w_ref, b_ref, za_ref, zb_ref, zc_ref):
    h = _modulated_norm(x_ref[...], mod_ref, ng_ref)
    for piece in _inproj_pieces(lambda: h, w_ref, b_ref, za_ref, zb_ref, zc_ref, 512):
        piece()


def _mod_spec(tm, tokens_per_batch):
    if tokens_per_batch is None:
        return pl.BlockSpec((tm, 3 * D_MODEL), lambda i: (i, 0))
    tiles_per_batch = tokens_per_batch // tm
    return pl.BlockSpec((None, 1, 3 * D_MODEL), lambda i: (i // tiles_per_batch, 0, 0))


def _layer_weight_spec(layer, rows, cols):
    return pl.BlockSpec((None, rows, cols), lambda i: (layer, 0, 0), pipeline_mode=pl.Buffered(1))


def _inproj_call(layer, x2, mod, ng, wcat, bcat, tokens_per_batch):
    ntok = x2.shape[0]
    tm = PROJ_TILE
    const = lambda i: (0, 0)
    return pl.pallas_call(
        _inproj_kernel,
        grid=(ntok // tm,),
        in_specs=[
            pl.BlockSpec((tm, D_MODEL), lambda i: (i, 0)),
            _mod_spec(tm, tokens_per_batch),
            pl.BlockSpec((1, D_MODEL), const),
            _layer_weight_spec(layer, D_MODEL, ZCAT_W),
            pl.BlockSpec((1, ZCAT_W), const),
        ],
        out_specs=[
            pl.BlockSpec((tm, ZA_W), lambda i: (i, 0)),
            pl.BlockSpec((tm, ZB_W), lambda i: (i, 0)),
            pl.BlockSpec((tm, ZC_W), lambda i: (i, 0)),
        ],
        out_shape=[
            jax.ShapeDtypeStruct((ntok, ZA_W), F32),
            jax.ShapeDtypeStruct((ntok, ZB_W), F32),
            jax.ShapeDtypeStruct((ntok, ZC_W), F32),
        ],
        compiler_params=pltpu.CompilerParams(
            dimension_semantics=("arbitrary",), vmem_limit_bytes=VMEM_LIMIT),
        name="in_projection",
    )(x2, mod, ng, wcat, bcat)


def _outproj_kernel(x_ref, mod_ref, ng_ref, y_ref, wmg_ref, bmg_ref, wa_ref, wb_ref, wc_ref,
                    wo_ref, o_ref):
    x = x_ref[...]
    h = _modulated_norm(x, mod_ref, ng_ref)
    merged = None
    for i, wbr_ref in enumerate((wa_ref, wb_ref, wc_ref)):
        cols = slice(i * D_MODEL, (i + 1) * D_MODEL)
        gate = _sigmoid(_dot(h, wmg_ref[:, cols]) + bmg_ref[:, cols])
        term = gate * _dot(y_ref[:, i * A_WIDTH:(i + 1) * A_WIDTH], wbr_ref[...])
        merged = term if merged is None else merged + term
    ada_gate = mod_ref[:, 2 * D_MODEL:3 * D_MODEL]
    o_ref[...] = x + ada_gate * _dot(merged.astype(BF16), wo_ref[...])


def _outproj_call(layer, x2, mod, ng, y, wmg, bmg, wa, wb, wc, wo, tokens_per_batch):
    ntok = x2.shape[0]
    tm = PROJ_TILE
    const = lambda i: (0, 0)
    once = pl.Buffered(1)
    return pl.pallas_call(
        _outproj_kernel,
        grid=(ntok // tm,),
        in_specs=[
            pl.BlockSpec((tm, D_MODEL), lambda i: (i, 0)),
            _mod_spec(tm, tokens_per_batch),
            pl.BlockSpec((1, D_MODEL), const),
            pl.BlockSpec((tm, Y_W), lambda i: (i, 0)),
            _layer_weight_spec(layer, D_MODEL, 3 * D_MODEL),
            pl.BlockSpec((1, 3 * D_MODEL), const),
            pl.BlockSpec((A_WIDTH, D_MODEL), const, pipeline_mode=once),
            pl.BlockSpec((B_WIDTH, D_MODEL), const, pipeline_mode=once),
            pl.BlockSpec((C_WIDTH, D_MODEL), const, pipeline_mode=once),
            pl.BlockSpec((D_MODEL, D_MODEL), const, pipeline_mode=once),
        ],
        out_specs=pl.BlockSpec((tm, D_MODEL), lambda i: (i, 0)),
        out_shape=jax.ShapeDtypeStruct((ntok, D_MODEL), F32),
        compiler_params=pltpu.CompilerParams(
            dimension_semantics=("arbitrary",), vmem_limit_bytes=VMEM_LIMIT),
        name="out_projection",
    )(x2, mod, ng, y, wmg, bmg, wa, wb, wc, wo)


def _place_q_head(qn, h, rows):
    lane = lax.broadcasted_iota(jnp.int32, (rows, LANES), 1)
    slab = qn[:, (h // 2) * LANES:(h // 2 + 1) * LANES]
    src_hi = h % 2
    dst_hi = h // (B_HEADS // B_KV_HEADS)
    keep = (lane >= B_HEAD_DIM) if src_hi else (lane < B_HEAD_DIM)
    slab = jnp.where(keep, slab, 0.0)
    if src_hi != dst_hi:
        slab = pltpu.roll(slab, B_HEAD_DIM, 1)
    return slab


def _merge_head_pair(o_even, o_odd, h_even, rows):
    lane_lo = lax.broadcasted_iota(jnp.int32, (rows, LANES), 1) < B_HEAD_DIM
    kv_hi = h_even // (B_HEADS // B_KV_HEADS)
    if kv_hi:
        o_even = pltpu.roll(o_even, B_HEAD_DIM, 1)
    else:
        o_odd = pltpu.roll(o_odd, B_HEAD_DIM, 1)
    return jnp.where(lane_lo, o_even, o_odd)


def _conv_taps(xbuf, cw_ref, cb_ref, cols, ts):
    y = cb_ref[:, cols]
    for j in range(C_CONV):
        lo = SUBLANES - (C_CONV - 1) + j
        y = y + cw_ref[j:j + 1, cols] * xbuf[lo:lo + ts, cols]
    return y


def _prompt_mix_kernel(sink_ref, za_ref, zb_ref, zc_ref, vg_ref, gw_ref, gbs_ref, qg_ref, kg_ref,
                       cw_ref, cb_ref, fb_ref, hg_ref, tril_ref, band_ref, tri01_ref, tribias_ref,
                       y_ref, ko_ref, vo_ref, convo_ref, c_ref, n_ref, m_ref,
                       kprev, vprev, xbuf, first_tile, pump):
    ts = PROMPT_TILE

    u = za_ref[:, 0:A_WIDTH]
    vn = _rms(za_ref[:, A_WIDTH:2 * A_WIDTH]) * vg_ref[...]
    sg = _silu(za_ref[:, 2 * A_WIDTH:3 * A_WIDTH])
    vnb = vn.astype(BF16)
    wts = [(gw_ref[gi] * tril_ref[...]).astype(BF16) for gi in range(A_GROUPS)]
    s_rows = []
    for c in range(ts // WINDOW):
        s_cols = []
        for gi in range(A_GROUPS):
            vblk = vnb[c * WINDOW:(c + 1) * WINDOW, gi * GROUP_DIM:(gi + 1) * GROUP_DIM]
            s_cols.append(_dot(wts[gi], vblk) + gbs_ref[:, gi:gi + 1])
        s_rows.append(jnp.concatenate(s_cols, axis=1))
    s = jnp.concatenate(s_rows, axis=0)
    pump()
    y_ref[:, 0:A_WIDTH] = (u * s * sg).astype(BF16)
    pump()

    qn = _qk_norm(zb_ref[:, 0:B_WIDTH], qg_ref[...]) * (B_HEAD_DIM ** -0.5)
    pump()
    kn = _qk_norm(zb_ref[:, B_WIDTH:B_WIDTH + B_KV_WIDTH], kg_ref[...])
    vv = zb_ref[:, B_WIDTH + B_KV_WIDTH:B_WIDTH + 2 * B_KV_WIDTH]
    sgb = _silu(zb_ref[:, B_WIDTH + 2 * B_KV_WIDTH:ZB_W])
    pump()
    grp = B_HEADS // B_KV_HEADS
    nblk = ts // WINDOW
    lane_lo2 = lax.broadcasted_iota(jnp.int32, (2 * WINDOW, LANES), 1) < B_HEAD_DIM
    kblocks = [kprev[...]] + [kn[b * WINDOW:(b + 1) * WINDOW] for b in range(nblk)]
    vblocks = [vprev[...]] + [vv[b * WINDOW:(b + 1) * WINDOW] for b in range(nblk)]
    bias0 = band_ref[0] if first_tile is False else jnp.where(first_tile, band_ref[1], band_ref[0])
    bias = [bias0] + [band_ref[0]] * (nblk - 1)
    combos = [(blk, kh) for blk in range(nblk) for kh in range(B_KV_HEADS)]
    heads = [(blk, kh, g) for blk, kh in combos for g in range(grp)]
    kdup, vdup = {}, {}
    for blk in range(nblk):
        kcat = jnp.concatenate([kblocks[blk], kblocks[blk + 1]], axis=0)
        vcat = jnp.concatenate([vblocks[blk], vblocks[blk + 1]], axis=0)
        krol = pltpu.roll(kcat, B_HEAD_DIM, 1)
        vrol = pltpu.roll(vcat, B_HEAD_DIM, 1)
        for kh in range(B_KV_HEADS):
            own = lane_lo2 if kh == 0 else jnp.logical_not(lane_lo2)
            kdup[blk, kh] = jnp.where(own, kcat, krol).astype(BF16)
            vdup[blk, kh] = jnp.where(own, vcat, vrol).astype(BF16)
    pump()
    qs = {(blk, kh): jnp.concatenate(
        [_place_q_head(qn[blk * WINDOW:(blk + 1) * WINDOW], kh * grp + g, WINDOW) for g in range(grp)],
        axis=0).astype(BF16) for blk, kh in combos}
    pump()
    logits = {c: _dot_nt(qs[c], kdup[c]) for c in combos}
    pump()
    snk = {k: sink_ref[k[1] * grp + k[2]] for k in heads}
    lg = {(blk, kh, g): logits[blk, kh][g * WINDOW:(g + 1) * WINDOW] + bias[blk]
          for blk, kh, g in heads}
    pump()
    mx = {k: jnp.maximum(jnp.max(lg[k], axis=-1, keepdims=True), snk[k]) for k in heads}
    pump()
    p = {k: jnp.exp(lg[k] - mx[k]) for k in heads}
    pump()
    rden = {k: 1.0 / (jnp.sum(p[k], axis=-1, keepdims=True) + jnp.exp(snk[k] - mx[k])) for k in heads}
    pump()
    pv = {c: _dot(jnp.concatenate([p[c + (g,)].astype(BF16) for g in range(grp)], axis=0), vdup[c])
          for c in combos}
    pump()
    outs = {(blk, kh, g): pv[blk, kh][g * WINDOW:(g + 1) * WINDOW] * rden[blk, kh, g]
            for blk, kh, g in heads}
    pump()
    yb = jnp.concatenate([jnp.concatenate(
        [_merge_head_pair(outs[blk, (2 * j) // grp, (2 * j) % grp],
                          outs[blk, (2 * j + 1) // grp, (2 * j + 1) % grp], 2 * j, WINDOW)
         for j in range(B_HEADS // 2)], axis=1) for blk in range(nblk)], axis=0)
    y_ref[:, A_WIDTH:A_WIDTH + B_WIDTH] = (yb * sgb).astype(BF16)
    pump()
    kprev[...] = kblocks[nblk]
    vprev[...] = vblocks[nblk]
    ko_ref[...] = kblocks[nblk]
    vo_ref[...] = vblocks[nblk]
    pump()

    xbuf[SUBLANES:SUBLANES + ts, :] = zc_ref[:, 0:2 * C_WIDTH]
    qk = _silu(_conv_taps(xbuf, cw_ref, cb_ref, slice(0, 2 * C_WIDTH), ts))
    pump()
    tail = xbuf[ts:ts + SUBLANES, :]
    xbuf[0:SUBLANES, :] = tail
    convo_ref[...] = tail
    qall = qk[:, 0:C_WIDTH].astype(BF16)
    kall = qk[:, C_WIDTH:2 * C_WIDTH] * (C_HEAD_DIM ** -0.5)
    vall = zc_ref[:, 2 * C_WIDTH:3 * C_WIDTH].astype(BF16)
    gate_o = _sigmoid(zc_ref[:, 3 * C_WIDTH:4 * C_WIDTH]) * _silu(zc_ref[:, 4 * C_WIDTH:5 * C_WIDTH])
    pump()
    ifp = zc_ref[:, 5 * C_WIDTH:5 * C_WIDTH + LANES]
    lf = _log_sigmoid(ifp + fb_ref[...])
    pump()
    cl = MLSTM_CHUNK
    hds = range(C_HEADS)
    lane_c = lax.broadcasted_iota(jnp.int32, (cl, LANES), 1)
    lane_1 = lax.broadcasted_iota(jnp.int32, (1, LANES), 1)
    m_row = m_ref[...]
    cum_all = _dot_exact01(tri01_ref[...], lf)
    st_col = jnp.where(lane_c < C_HEADS, ifp, cum_all)
    st_row = st_col.T
    pump()
    hs = [slice(hd * C_HEAD_DIM, (hd + 1) * C_HEAD_DIM) for hd in hds]
    i_c = [st_col[:, hd:hd + 1] for hd in hds]
    cum_c = [st_col[:, C_HEADS + hd:C_HEADS + hd + 1] for hd in hds]
    i_r = [st_row[hd:hd + 1, :] for hd in hds]
    cum_r = [st_row[C_HEADS + hd:C_HEADS + hd + 1, :] for hd in hds]
    m_prev = [m_row[:, hd:hd + 1] for hd in hds]
    tribias = tribias_ref[...]
    dmat = [cum_c[hd] - cum_r[hd] + i_r[hd] + tribias for hd in hds]
    pump()
    m_inter = [cum_c[hd] + m_prev[hd] for hd in hds]
    m_t = [jnp.maximum(m_inter[hd], jnp.max(dmat[hd], axis=-1, keepdims=True)) for hd in hds]
    pump()
    q_h = [qall[:, hs[hd]] for hd in hds]
    k_h = [kall[:, hs[hd]] for hd in hds]
    v_h = [vall[:, hs[hd]] for hd in hds]
    s_qk = [_dot_nt(q_h[hd], k_h[hd].astype(BF16)) for hd in hds]
    pump()
    a = [jnp.exp(dmat[hd] - m_t[hd]) * s_qk[hd] for hd in hds]
    pump()
    w_inter = [jnp.exp(m_inter[hd] - m_t[hd]) for hd in hds]
    c_prev = [c_ref[hd] for hd in hds]
    n_prev = [n_ref[hd:hd + 1, :] for hd in hds]
    inter = [_dot(q_h[hd], c_prev[hd].astype(BF16)) for hd in hds]
    pump()
    intra = [_dot(a[hd].astype(BF16), v_h[hd]) for hd in hds]
    pump()
    den = [jnp.sum(a[hd], axis=-1, keepdims=True)
           + w_inter[hd] * jnp.sum(q_h[hd].astype(F32) * n_prev[hd], axis=-1, keepdims=True)
           for hd in hds]
    pump()
    rnorm = [1.0 / jnp.maximum(jnp.abs(den[hd]), jnp.exp(-m_t[hd])) for hd in hds]
    hh = [(intra[hd] + w_inter[hd] * inter[hd]) * rnorm[hd] for hd in hds]
    pump()
    hn = jnp.concatenate([_rms(hh[hd]) for hd in hds], axis=1) * hg_ref[...]
    y_ref[:, A_WIDTH + B_WIDTH:Y_W] = (hn * gate_o).astype(BF16)
    pump()
    total = [cum_r[hd][:, cl - 1:cl] for hd in hds]
    g_r = [total[hd] - cum_r[hd] + i_r[hd] for hd in hds]
    g_c = [total[hd] - cum_c[hd] + i_c[hd] for hd in hds]
    m_new = [jnp.maximum(total[hd] + m_prev[hd], jnp.max(g_r[hd], axis=-1, keepdims=True))
             for hd in hds]
    pump()
    kw = [jnp.exp(g_c[hd] - m_new[hd]) * k_h[hd] for hd in hds]
    decay = [jnp.exp(total[hd] + m_prev[hd] - m_new[hd]) for hd in hds]
    pump()
    upd = [_dot(kw[hd].T.astype(BF16), v_h[hd]) for hd in hds]
    pump()
    for hd in hds:
        c_ref[hd] = decay[hd] * c_prev[hd] + upd[hd]
        n_ref[hd:hd + 1, :] = decay[hd] * n_prev[hd] + jnp.sum(kw[hd], axis=0, keepdims=True)
        m_row = jnp.where(lane_1 == hd, m_new[hd], m_row)
    m_ref[...] = m_row


def _prompt_mask_constants():
    r = np.arange(WINDOW)[:, None]
    c = np.arange(2 * WINDOW)[None, :]
    band = (c > r) & (c <= r + WINDOW)
    band_first = band & (c >= WINDOW)
    band_bias = np.where(np.stack([band, band_first]), 0.0, NEG).astype(np.float32)
    tril = (np.arange(WINDOW)[:, None] >= np.arange(WINDOW)[None, :]).astype(np.float32)
    tri = np.arange(MLSTM_CHUNK)[:, None] >= np.arange(MLSTM_CHUNK)[None, :]
    return (jnp.asarray(tril), jnp.asarray(band_bias), jnp.asarray(tri, dtype=BF16),
            jnp.asarray(np.where(tri, 0.0, NEG).astype(np.float32)))


N_MIX_PARAMS = 13
MIX_PUMP_CALLS = 31
TAIL_FILL_PIECES = 8
MXU_PIECE_COLS = 256


class _Interleaver:
    def __init__(self, pieces, calls, hold_back=0):
        self._pieces = list(pieces)
        self._hold_back = hold_back
        self._spread = len(self._pieces) - hold_back
        self._emitted = 0
        self._calls = calls
        self._call = 0

    def __call__(self):
        self._call += 1
        target = (self._call * self._spread) // self._calls
        while self._emitted < target:
            self._pieces.pop(0)()
            self._emitted += 1

    def finish(self):
        assert self._call == self._calls and len(self._pieces) == self._hold_back, self._call
        return self._pieces


def _gate_pieces(h_ref, wmg_ref, bmg_ref, g_ref):
    def piece(off):
        cols = slice(off, off + MXU_PIECE_COLS)
        def run():
            g_ref[:, cols] = _sigmoid(_dot(h_ref[...], wmg_ref[:, cols]) + bmg_ref[:, cols])
        return run
    return [piece(off) for off in range(0, 3 * D_MODEL, MXU_PIECE_COLS)]


def _merge_and_project(x, mod_ref, g_ref, y_ref, wa_ref, wb_ref, wc_ref, wo_ref, fillers=()):
    fillers = list(fillers)
    per_stage = -(-len(fillers) // 4)
    merged = None
    for i, wbr_ref in enumerate((wa_ref, wb_ref, wc_ref)):
        for piece in fillers[i * per_stage:(i + 1) * per_stage]:
            piece()
        term = (g_ref[:, i * D_MODEL:(i + 1) * D_MODEL]
                * _dot(y_ref[:, i * A_WIDTH:(i + 1) * A_WIDTH], wbr_ref[...]))
        merged = term if merged is None else merged + term
    for piece in fillers[3 * per_stage:]:
        piece()
    ada_gate = mod_ref[:, 2 * D_MODEL:3 * D_MODEL]
    return x + ada_gate * _dot(merged.astype(BF16), wo_ref[...])


def _prompt_layer_kernel(tiles_per_seq, sink_ref, x2_ref, xn_ref, mod_ref, modn_ref, ng_ref,
                         wcat_ref, bcat_ref, *rest):
    mix_params = rest[:N_MIX_PARAMS]
    wmg_ref, bmg_ref, wa_ref, wb_ref, wc_ref, wo_ref = rest[N_MIX_PARAMS:N_MIX_PARAMS + 6]
    o_ref, ko_ref, vo_ref, convo_ref, c_ref, n_ref, m_ref = rest[N_MIX_PARAMS + 6:N_MIX_PARAMS + 13]
    (za0, zb0, zc0, za1, zb1, zc1, h0, h1, y_scr, g_scr, kprev, vprev, xbuf) = rest[N_MIX_PARAMS + 13:]
    ts = PROMPT_TILE
    z = ((za0, zb0, zc0), (za1, zb1, zc1))
    h = (h0, h1)
    k = pl.program_id(0)
    seq_start = (k % (tiles_per_seq // 2)) == 0

    @pl.when(k == 0)
    def _():
        h0[...] = _modulated_norm(x2_ref[0:ts, :], mod_ref, ng_ref)
        for piece in _inproj_pieces(lambda: h0[...], wcat_ref, bcat_ref, *z[0], 512):
            piece()

    @pl.when(seq_start)
    def _():
        kprev[...] = jnp.zeros_like(kprev)
        vprev[...] = jnp.zeros_like(vprev)
        xbuf[0:SUBLANES, :] = jnp.zeros((SUBLANES, 2 * C_WIDTH), F32)
        c_ref[...] = jnp.zeros_like(c_ref)
        n_ref[...] = jnp.zeros_like(n_ref)
        m_ref[...] = jnp.zeros_like(m_ref)

    for half in range(2):
        cur, nxt = half, 1 - half
        rows = slice(half * ts, (half + 1) * ts)
        if half == 0:
            h[nxt][...] = _modulated_norm(x2_ref[ts:2 * ts, :], mod_ref, ng_ref)
        else:
            h[nxt][...] = _modulated_norm(xn_ref[...], modn_ref, ng_ref)
        get_h_next = functools.partial(lambda r: r[...], h[nxt])
        hold = TAIL_FILL_PIECES if half == 1 else 0
        proj = _inproj_pieces(get_h_next, wcat_ref, bcat_ref, *z[nxt], MXU_PIECE_COLS)
        pump = _Interleaver(
            proj[:len(proj) - hold] + _gate_pieces(h[cur], wmg_ref, bmg_ref, g_scr)
            + proj[len(proj) - hold:], MIX_PUMP_CALLS, hold_back=hold)
        _prompt_mix_kernel(sink_ref, *z[cur], *mix_params,
                           y_scr, ko_ref, vo_ref, convo_ref, c_ref, n_ref, m_ref, kprev, vprev, xbuf,
                           first_tile=seq_start if half == 0 else False, pump=pump)
        o_ref[rows, :] = _merge_and_project(x2_ref[rows, :], mod_ref, g_scr, y_scr,
                                            wa_ref, wb_ref, wc_ref, wo_ref, fillers=pump.finish())


def _prompt_layer_call(layer, x2, mod, lw, batch, seq):
    ts = PROMPT_TILE
    nt = seq // ts
    assert nt % 2 == 0
    last_tile = batch * nt - 1
    const2 = lambda k: (0, 0)
    const3 = lambda k: (0, 0, 0)
    per_b3 = lambda k: ((2 * k) // nt, 0, 0)
    next_tile = lambda k: jnp.minimum(2 * k + 2, last_tile)
    once = pl.Buffered(1)
    return pl.pallas_call(
        functools.partial(_prompt_layer_kernel, nt),
        grid=(batch * nt // 2,),
        in_specs=[
            pl.BlockSpec(memory_space=pltpu.SMEM),
            pl.BlockSpec((2 * ts, D_MODEL), lambda k: (k, 0)),
            pl.BlockSpec((ts, D_MODEL), lambda k: (next_tile(k), 0)),
            pl.BlockSpec((None, 1, 3 * D_MODEL), per_b3),
            pl.BlockSpec((None, 1, 3 * D_MODEL), lambda k: (next_tile(k) // nt, 0, 0)),
            pl.BlockSpec((1, D_MODEL), const2),
            _layer_weight_spec(layer, D_MODEL, ZCAT_W),
            pl.BlockSpec((1, ZCAT_W), const2),
            pl.BlockSpec((1, A_WIDTH), const2),
            pl.BlockSpec((A_GROUPS, WINDOW, WINDOW), const3),
            pl.BlockSpec((WINDOW, LANES), const2),
            pl.BlockSpec((1, B_WIDTH), const2),
            pl.BlockSpec((1, B_KV_WIDTH), const2),
            pl.BlockSpec((C_CONV, 2 * C_WIDTH), const2),
            pl.BlockSpec((1, 2 * C_WIDTH), const2),
            pl.BlockSpec((1, LANES), const2),
            pl.BlockSpec((1, C_WIDTH), const2),
            pl.BlockSpec((WINDOW, WINDOW), const2),
            pl.BlockSpec((2, WINDOW, 2 * WINDOW), const3),
            pl.BlockSpec((MLSTM_CHUNK, MLSTM_CHUNK), const2),
            pl.BlockSpec((MLSTM_CHUNK, MLSTM_CHUNK), const2),
            _layer_weight_spec(layer, D_MODEL, 3 * D_MODEL),
            pl.BlockSpec((1, 3 * D_MODEL), const2),
            pl.BlockSpec((A_WIDTH, D_MODEL), const2, pipeline_mode=once),
            pl.BlockSpec((B_WIDTH, D_MODEL), const2, pipeline_mode=once),
            pl.BlockSpec((C_WIDTH, D_MODEL), const2, pipeline_mode=once),
            pl.BlockSpec((D_MODEL, D_MODEL), const2, pipeline_mode=once),
        ],
        out_specs=[
            pl.BlockSpec((2 * ts, D_MODEL), lambda k: (k, 0)),
            pl.BlockSpec((None, WINDOW, B_KV_WIDTH), per_b3),
            pl.BlockSpec((None, WINDOW, B_KV_WIDTH), per_b3),
            pl.BlockSpec((None, SUBLANES, 2 * C_WIDTH), per_b3),
            pl.BlockSpec((None, C_HEADS, C_HEAD_DIM, C_HEAD_DIM), lambda k: ((2 * k) // nt, 0, 0, 0)),
            pl.BlockSpec((None, C_HEADS, C_HEAD_DIM), per_b3),
            pl.BlockSpec((None, 1, LANES), per_b3),
        ],
        out_shape=[
            jax.ShapeDtypeStruct((batch * seq, D_MODEL), F32),
            jax.ShapeDtypeStruct((batch, WINDOW, B_KV_WIDTH), F32),
            jax.ShapeDtypeStruct((batch, WINDOW, B_KV_WIDTH), F32),
            jax.ShapeDtypeStruct((batch, SUBLANES, 2 * C_WIDTH), F32),
            jax.ShapeDtypeStruct((batch, C_HEADS, C_HEAD_DIM, C_HEAD_DIM), F32),
            jax.ShapeDtypeStruct((batch, C_HEADS, C_HEAD_DIM), F32),
            jax.ShapeDtypeStruct((batch, 1, LANES), F32),
        ],
        scratch_shapes=(
            [pltpu.VMEM((ts, w), F32) for w in (ZA_W, ZB_W, ZC_W)] * 2
            + [pltpu.VMEM((ts, D_MODEL), BF16)] * 2
            + [pltpu.VMEM((ts, Y_W), BF16),
               pltpu.VMEM((ts, 3 * D_MODEL), F32),
               pltpu.VMEM((WINDOW, B_KV_WIDTH), F32),
               pltpu.VMEM((WINDOW, B_KV_WIDTH), F32),
               pltpu.VMEM((ts + SUBLANES, 2 * C_WIDTH), F32)]),
        compiler_params=pltpu.CompilerParams(
            dimension_semantics=("arbitrary",), vmem_limit_bytes=VMEM_LIMIT),
        name="prompt_layer",
    )(lw["sinks"], x2, x2, mod, mod, lw["ng"], lw["wcat"], lw["bcat"],
      lw["vg"], lw["gws"], lw["gbs_col"], lw["qg"], lw["kg"], lw["cw"], lw["cb"], lw["fb"], lw["hg"],
      *_prompt_mask_constants(),
      lw["wmg"], lw["bmg"], lw["wa"], lw["wb"], lw["wc"], lw["wo"])


def _sample_mix_kernel(sink_ref, za_ref, zb_ref, zc_ref, kc_ref, vc_ref, cs_ref, c0_ref, n0_ref,
                       m0_ref, vg_ref, gwb_ref, gbs_ref, qg_ref, kg_ref, cw_ref, cb_ref, fb_ref,
                       hg_ref,
                       y_ref, vrow_ref, ko_ref, vo_ref, convo_ref, c1_ref, n1_ref, m1_ref,
                       xbuf):
    nb = SAMPLE_NB
    t = SUBLANES
    rows = nb * t
    tok_r = lax.broadcasted_iota(jnp.int32, (rows, rows), 0)
    tok_c = lax.broadcasted_iota(jnp.int32, (rows, rows), 1)
    same_b = (tok_r // t) == (tok_c // t)
    causal_b = same_b & (tok_c <= tok_r)

    u = za_ref[:, 0:A_WIDTH]
    vn = _rms(za_ref[:, A_WIDTH:2 * A_WIDTH]) * vg_ref[...]
    sg = _silu(za_ref[:, 2 * A_WIDTH:3 * A_WIDTH])
    vrow_ref[...] = vn
    vnb = vn.astype(BF16)
    s_cols = []
    for gi in range(A_GROUPS):
        s_cols.append(_dot(gwb_ref[gi], vnb[:, gi * GROUP_DIM:(gi + 1) * GROUP_DIM])
                      + gbs_ref[:, gi:gi + 1])
    y_ref[:, 0:A_WIDTH] = (u * jnp.concatenate(s_cols, axis=1) * sg).astype(BF16)

    qn = _qk_norm(zb_ref[:, 0:B_WIDTH], qg_ref[...]) * (B_HEAD_DIM ** -0.5)
    kn = _qk_norm(zb_ref[:, B_WIDTH:B_WIDTH + B_KV_WIDTH], kg_ref[...])
    vv = zb_ref[:, B_WIDTH + B_KV_WIDTH:B_WIDTH + 2 * B_KV_WIDTH]
    sgb = _silu(zb_ref[:, B_WIDTH + 2 * B_KV_WIDTH:ZB_W])
    kn3 = kn.reshape(nb, t, B_KV_WIDTH)
    vv3 = vv.reshape(nb, t, B_KV_WIDTH)
    kcache = kc_ref[...]
    vcache = vc_ref[...]
    pad = jnp.zeros((nb, WINDOW - t, B_KV_WIDTH), F32)
    kall = jnp.concatenate([kcache, kn3, pad], axis=1).astype(BF16)
    vall = jnp.concatenate([vcache, vv3, pad], axis=1).astype(BF16)
    qp = jnp.concatenate([_place_q_head(qn, h, rows).reshape(nb, t, LANES) for h in range(B_HEADS)],
                         axis=1).astype(BF16)
    logits = lax.dot_general(qp, kall, (((2,), (2,)), ((0,), (0,))), preferred_element_type=F32)
    qrow = lax.broadcasted_iota(jnp.int32, (nb, B_HEADS * t, 2 * WINDOW), 1)
    kcol = lax.broadcasted_iota(jnp.int32, (nb, B_HEADS * t, 2 * WINDOW), 2)
    qt = qrow % t
    valid = ((kcol < WINDOW) & (kcol > qt)) | ((kcol >= WINDOW) & ((kcol - WINDOW) <= qt))
    hrow = lax.broadcasted_iota(jnp.int32, (B_HEADS * t, 1), 0) // t
    snk = jnp.zeros((B_HEADS * t, 1), F32)
    for h in range(B_HEADS):
        snk = jnp.where(hrow == h, sink_ref[h], snk)
    lg = jnp.where(valid, logits, NEG)
    mx = jnp.maximum(jnp.max(lg, axis=-1, keepdims=True), snk[None])
    p = jnp.exp(lg - mx)
    den = jnp.sum(p, axis=-1, keepdims=True) + jnp.exp(snk[None] - mx)
    pv = lax.dot_general(p.astype(BF16), vall, (((2,), (1,)), ((0,), (0,))),
                         preferred_element_type=F32) / den
    head_out = [pv[:, h * t:(h + 1) * t, :].reshape(rows, LANES) for h in range(B_HEADS)]
    yb = jnp.concatenate(
        [_merge_head_pair(head_out[2 * j], head_out[2 * j + 1], 2 * j, rows)
         for j in range(B_HEADS // 2)], axis=1)
    y_ref[:, A_WIDTH:A_WIDTH + B_WIDTH] = (yb * sgb).astype(BF16)
    ko_ref[...] = jnp.concatenate([kcache[:, t:, :], kn3], axis=1)
    vo_ref[...] = jnp.concatenate([vcache[:, t:, :], vv3], axis=1)

    xbuf[:, SUBLANES - (C_CONV - 1):SUBLANES, :] = cs_ref[...]
    xbuf[:, SUBLANES:2 * SUBLANES, :] = zc_ref[:, 0:2 * C_WIDTH].reshape(nb, t, 2 * C_WIDTH)
    y3 = cb_ref[...][None]
    for j in range(C_CONV):
        lo = SUBLANES - (C_CONV - 1) + j
        y3 = y3 + cw_ref[j:j + 1, :][None] * xbuf[:, lo:lo + t, :]
    convo_ref[...] = xbuf[:, 2 * SUBLANES - (C_CONV - 1):2 * SUBLANES, :]
    qk = _silu(y3.reshape(rows, 2 * C_WIDTH))
    qall = qk[:, 0:C_WIDTH].astype(BF16)
    kall_c = qk[:, C_WIDTH:2 * C_WIDTH] * (C_HEAD_DIM ** -0.5)
    vall_c = zc_ref[:, 2 * C_WIDTH:3 * C_WIDTH].astype(BF16)
    gate_o = _sigmoid(zc_ref[:, 3 * C_WIDTH:4 * C_WIDTH]) * _silu(zc_ref[:, 4 * C_WIDTH:5 * C_WIDTH])
    ifp = zc_ref[:, 5 * C_WIDTH:5 * C_WIDTH + LANES]
    lf = _log_sigmoid(ifp + fb_ref[...])
    lane_t = lax.broadcasted_iota(jnp.int32, (rows, LANES), 1)
    cum_all = _dot_exact01(jnp.where(causal_b, 1.0, 0.0).astype(BF16), lf)
    tot_all = _dot_exact01(jnp.where(same_b, 1.0, 0.0).astype(BF16), lf)
    st_col = jnp.where(lane_t < C_HEADS, ifp, cum_all)
    st_row = st_col.T
    tot_row = tot_all.T
    m0 = m0_ref[...]
    same_b_bf = jnp.where(same_b, 1.0, 0.0).astype(BF16)
    batch_of_lane = lax.broadcasted_iota(jnp.int32, (nb, 1, rows), 2) // t
    batch_id = lax.broadcasted_iota(jnp.int32, (nb, 1, rows), 0)
    own_tok = batch_of_lane == batch_id
    h_cols = []
    m_out = jnp.zeros((rows, LANES), F32)
    for hd in range(C_HEADS):
        hs = slice(hd * C_HEAD_DIM, (hd + 1) * C_HEAD_DIM)
        i_c = st_col[:, hd:hd + 1]
        cum_c = st_col[:, C_HEADS + hd:C_HEADS + hd + 1]
        tot_c = tot_all[:, C_HEADS + hd:C_HEADS + hd + 1]
        i_r = st_row[hd:hd + 1, :]
        cum_r = st_row[C_HEADS + hd:C_HEADS + hd + 1, :]
        tot_r = tot_row[C_HEADS + hd:C_HEADS + hd + 1, :]
        m_prev = m0[:, hd:hd + 1]
        dmat = jnp.where(causal_b, cum_c - cum_r + i_r, NEG)
        m_inter = cum_c + m_prev
        m_t = jnp.maximum(m_inter, jnp.max(dmat, axis=-1, keepdims=True))
        q_h = qall[:, hs]
        k_h = kall_c[:, hs]
        v_h = vall_c[:, hs]
        a = jnp.exp(dmat - m_t) * _dot_nt(q_h, k_h.astype(BF16))
        w_inter = jnp.exp(m_inter - m_t)
        c_prev = c0_ref[:, hd]
        n_tok = jnp.broadcast_to(n0_ref[hd][:, None, :], (nb, t, C_HEAD_DIM)).reshape(rows, C_HEAD_DIM)
        inter = lax.dot_general(q_h.reshape(nb, t, C_HEAD_DIM), c_prev.astype(BF16),
                                (((2,), (1,)), ((0,), (0,))), preferred_element_type=F32)
        num = _dot(a.astype(BF16), v_h) + w_inter * inter.reshape(rows, C_HEAD_DIM)
        den = (jnp.sum(a, axis=-1, keepdims=True)
               + w_inter * jnp.sum(q_h.astype(F32) * n_tok, axis=-1, keepdims=True))
        hh = num / jnp.maximum(jnp.abs(den), jnp.exp(-m_t))
        h_cols.append(_rms(hh))
        g_r = tot_r - cum_r + i_r
        g_c = tot_c - cum_c + i_c
        m_new = jnp.maximum(tot_c + m_prev,
                            jnp.max(jnp.where(same_b, g_r, NEG), axis=-1, keepdims=True))
        kw = jnp.exp(g_c - m_new) * k_h
        decay = jnp.exp(tot_c + m_prev - m_new)
        kwt = kw.T
        lhs = jnp.where(own_tok, kwt[None], 0.0).astype(BF16).reshape(nb * C_HEAD_DIM, rows)
        upd = _dot(lhs, v_h).reshape(nb, C_HEAD_DIM, C_HEAD_DIM)
        dec_b = jnp.broadcast_to(decay, (rows, C_HEAD_DIM)).reshape(nb, t, C_HEAD_DIM)[:, 0:1, :]
        c1_ref[:, hd] = dec_b * c_prev + upd
        n1_ref[hd] = decay * n_tok + _dot(same_b_bf, kw.astype(BF16))
        m_out = jnp.where(lane_t == hd, m_new, m_out)
    m1_ref[...] = m_out
    hn = jnp.concatenate(h_cols, axis=1) * hg_ref[...]
    y_ref[:, A_WIDTH + B_WIDTH:Y_W] = (hn * gate_o).astype(BF16)


def _sample_mix_call(l, za, zb, zc, kc, vc, cs, c0, n0t, m0tok, lw, nbatch):
    nb = SAMPLE_NB
    t = SUBLANES
    rows = nb * t
    tok = lambda i: (i, 0)
    const2 = lambda i: (0, 0)
    const3 = lambda i: (0, 0, 0)
    b3 = lambda i: (i, 0, 0)
    lb4 = lambda i: (l, i, 0, 0)
    return pl.pallas_call(
        _sample_mix_kernel,
        grid=(nbatch // nb,),
        in_specs=[
            pl.BlockSpec(memory_space=pltpu.SMEM),
            pl.BlockSpec((rows, ZA_W), tok),
            pl.BlockSpec((rows, ZB_W), tok),
            pl.BlockSpec((rows, ZC_W), tok),
            pl.BlockSpec((None, nb, WINDOW, B_KV_WIDTH), lb4),
            pl.BlockSpec((None, nb, WINDOW, B_KV_WIDTH), lb4),
            pl.BlockSpec((None, nb, C_CONV - 1, 2 * C_WIDTH), lb4),
            pl.BlockSpec((None, nb, C_HEADS, C_HEAD_DIM, C_HEAD_DIM), lambda i: (l, i, 0, 0, 0)),
            pl.BlockSpec((None, C_HEADS, nb, C_HEAD_DIM), lambda i: (l, 0, i, 0)),
            pl.BlockSpec((None, rows, LANES), lambda i: (l, i, 0)),
            pl.BlockSpec((1, A_WIDTH), const2),
            pl.BlockSpec((A_GROUPS, rows, rows), const3),
            pl.BlockSpec((rows, LANES), const2),
            pl.BlockSpec((1, B_WIDTH), const2),
            pl.BlockSpec((1, B_KV_WIDTH), const2),
            pl.BlockSpec((C_CONV, 2 * C_WIDTH), const2),
            pl.BlockSpec((1, 2 * C_WIDTH), const2),
            pl.BlockSpec((1, LANES), const2),
            pl.BlockSpec((1, C_WIDTH), const2),
        ],
        out_specs=[
            pl.BlockSpec((rows, Y_W), tok),
            pl.BlockSpec((rows, A_WIDTH), tok),
            pl.BlockSpec((nb, WINDOW, B_KV_WIDTH), b3),
            pl.BlockSpec((nb, WINDOW, B_KV_WIDTH), b3),
            pl.BlockSpec((nb, C_CONV - 1, 2 * C_WIDTH), b3),
            pl.BlockSpec((nb, C_HEADS, C_HEAD_DIM, C_HEAD_DIM), lambda i: (i, 0, 0, 0)),
            pl.BlockSpec((C_HEADS, rows, C_HEAD_DIM), lambda i: (0, i, 0)),
            pl.BlockSpec((rows, LANES), tok),
        ],
        out_shape=[
            jax.ShapeDtypeStruct((nbatch * t, Y_W), BF16),
            jax.ShapeDtypeStruct((nbatch * t, A_WIDTH), F32),
            jax.ShapeDtypeStruct((nbatch, WINDOW, B_KV_WIDTH), F32),
            jax.ShapeDtypeStruct((nbatch, WINDOW, B_KV_WIDTH), F32),
            jax.ShapeDtypeStruct((nbatch, C_CONV - 1, 2 * C_WIDTH), F32),
            jax.ShapeDtypeStruct((nbatch, C_HEADS, C_HEAD_DIM, C_HEAD_DIM), F32),
            jax.ShapeDtypeStruct((C_HEADS, nbatch * t, C_HEAD_DIM), F32),
            jax.ShapeDtypeStruct((nbatch * t, LANES), F32),
        ],
        scratch_shapes=[pltpu.VMEM((nb, 2 * SUBLANES, 2 * C_WIDTH), F32)],
        compiler_params=pltpu.CompilerParams(
            dimension_semantics=("arbitrary",), vmem_limit_bytes=VMEM_LIMIT),
        name="sample_mixer",
    )(lw["sinks"], za, zb, zc, kc, vc, cs, c0, n0t, m0tok, lw["vg"], lw["gwb"], lw["gbs_tok"],
      lw["qg"], lw["kg"], lw["cw"], lw["cb"], lw["fb"], lw["hg"])


def _layer_weights(l, wcat_all, wmg_all, b_in, gmlp_vnorm_g, gmlp_ws, gmlp_bs, swa_qnorm_g,
                   swa_knorm_g, swa_sinks, mlstm_conv_w, mlstm_conv_b, mlstm_f_bias, mlstm_hnorm_g,
                   w_branch_a, w_branch_b, w_branch_c, w_out, norm_g, dec_seq):
    bl = b_in[l]
    bcat = jnp.concatenate([bl[:COL_CI], bl[COL_CO:COL_MG], bl[COL_CI:COL_CO],
                            jnp.zeros((LANES - 2 * C_HEADS,), F32)])
    t = dec_seq
    nb = SAMPLE_NB
    ws_t = gmlp_ws[l][:, :t, :t] * jnp.tril(jnp.ones((t, t), F32))
    eye = jnp.eye(nb, dtype=F32)
    gwb = jnp.einsum("bc,gts->gbtcs", eye, ws_t).reshape(A_GROUPS, nb * t, nb * t).astype(BF16)
    gbs_col = jnp.pad(gmlp_bs[l].T, ((0, 0), (0, LANES - A_GROUPS)))
    gbs_tok = jnp.pad(jnp.tile(gmlp_bs[l][:, :t].T, (nb, 1)), ((0, 0), (0, LANES - A_GROUPS)))
    fb = jnp.pad(mlstm_f_bias[l], (C_HEADS, LANES - 2 * C_HEADS)).reshape(1, LANES)
    return dict(
        ng=norm_g[l].reshape(1, D_MODEL),
        wcat=wcat_all, bcat=bcat.reshape(1, ZCAT_W),
        wmg=wmg_all, bmg=bl[COL_MG:].reshape(1, 3 * D_MODEL),
        wa=w_branch_a[l].astype(BF16), wb=w_branch_b[l].astype(BF16),
        wc=w_branch_c[l].astype(BF16), wo=w_out[l].astype(BF16),
        vg=gmlp_vnorm_g[l].reshape(1, A_WIDTH), gws=gmlp_ws[l], gwb=gwb,
        gbs_col=gbs_col, gbs_tok=gbs_tok,
        qg=jnp.tile(swa_qnorm_g[l], B_HEADS).reshape(1, B_WIDTH),
        kg=jnp.tile(swa_knorm_g[l], B_KV_HEADS).reshape(1, B_KV_WIDTH),
        sinks=swa_sinks[l],
        cw=mlstm_conv_w[l], cb=mlstm_conv_b[l].reshape(1, 2 * C_WIDTH), fb=fb,
        hg=mlstm_hnorm_g[l].reshape(1, C_WIDTH),
    )


def kernel(x_prompt, x_sample, cache_swa_k, cache_swa_v, state_mlstm_conv, state_mlstm_C, state_mlstm_n, state_mlstm_m, c_prompt, c_sample, ada_w, ada_b, norm_g, w_in, b_in, gmlp_vnorm_g, gmlp_ws, gmlp_bs, swa_qnorm_g, swa_knorm_g, swa_sinks, mlstm_conv_w, mlstm_conv_b, mlstm_f_bias, mlstm_hnorm_g, w_branch_a, w_branch_b, w_branch_c, w_out):
    batch, seq, _ = x_prompt.shape
    nbatch, dec_seq, _ = x_sample.shape
    assert dec_seq == SUBLANES and seq % PROMPT_TILE == 0 and nbatch % SAMPLE_NB == 0
    assert seq % PROJ_TILE == 0 and (nbatch * dec_seq) % PROJ_TILE == 0
    wb_len = cache_swa_k.shape[2]
    assert wb_len == WINDOW

    nc = batch + nbatch
    nc_pad = -(-nc // SUBLANES) * SUBLANES
    c_all = jnp.concatenate([c_prompt, c_sample, jnp.zeros((nc_pad - nc, D_MODEL), F32)], axis=0)
    mod_all = _ada_call(c_all, ada_w, ada_b)

    xp = x_prompt.reshape(batch * seq, D_MODEL)
    xs = x_sample.reshape(nbatch * dec_seq, D_MODEL)
    kc_all = cache_swa_k.reshape(DEPTH, nbatch, WINDOW, B_KV_WIDTH)
    vc_all = cache_swa_v.reshape(DEPTH, nbatch, WINDOW, B_KV_WIDTH)
    n0t_all = jnp.transpose(state_mlstm_n, (0, 2, 1, 3))
    m0tok_all = jnp.pad(jnp.repeat(state_mlstm_m, dec_seq, axis=1),
                        ((0, 0), (0, 0), (0, LANES - C_HEADS)))
    wcat_all, wmg_all = _weight_prep_call(w_in)
    outs_p = [[] for _ in range(6)]
    outs_s = [[] for _ in range(6)]
    vrows = []
    for l in range(DEPTH):
        lw = _layer_weights(l, wcat_all, wmg_all, b_in, gmlp_vnorm_g, gmlp_ws, gmlp_bs, swa_qnorm_g,
                            swa_knorm_g, swa_sinks, mlstm_conv_w, mlstm_conv_b, mlstm_f_bias,
                            mlstm_hnorm_g, w_branch_a, w_branch_b, w_branch_c, w_out, norm_g,
                            dec_seq)
        mod_p = mod_all[l, :batch].reshape(batch, 1, 3 * D_MODEL)
        mod_s = jnp.repeat(mod_all[l, batch:nc], dec_seq, axis=0)

        xp, ko, vo, convo, c1, n1, m1 = _prompt_layer_call(l, xp, mod_p, lw, batch, seq)
        outs_p[0].append(ko.reshape(batch, WINDOW, B_KV_HEADS, B_HEAD_DIM))
        outs_p[1].append(vo.reshape(batch, WINDOW, B_KV_HEADS, B_HEAD_DIM))
        outs_p[2].append(convo[:, SUBLANES - (C_CONV - 1):, :])
        outs_p[3].append(c1)
        outs_p[4].append(n1)
        outs_p[5].append(m1[:, 0, :C_HEADS])

        za, zb, zc = _inproj_call(l, xs, mod_s, lw["ng"], lw["wcat"], lw["bcat"], None)
        y, vrow, ko, vo, convo, c1, n1tok, m1tok = _sample_mix_call(
            l, za, zb, zc, kc_all, vc_all, state_mlstm_conv, state_mlstm_C, n0t_all, m0tok_all,
            lw, nbatch)
        xs = _outproj_call(l, xs, mod_s, lw["ng"], y, lw["wmg"], lw["bmg"], lw["wa"], lw["wb"],
                           lw["wc"], lw["wo"], None)
        outs_s[0].append(ko.reshape(nbatch, WINDOW, B_KV_HEADS, B_HEAD_DIM))
        outs_s[1].append(vo.reshape(nbatch, WINDOW, B_KV_HEADS, B_HEAD_DIM))
        outs_s[2].append(convo)
        outs_s[3].append(c1)
        outs_s[4].append(jnp.transpose(n1tok[:, ::dec_seq, :], (1, 0, 2)))
        outs_s[5].append(m1tok[::dec_seq, :C_HEADS])
        vrows.append(vrow.reshape(nbatch, dec_seq, A_WIDTH))

    sp = [jnp.stack(o) for o in outs_p]
    ss = [jnp.stack(o) for o in outs_s]
    return (xp.reshape(batch, seq, D_MODEL), xs.reshape(nbatch, dec_seq, D_MODEL),
            sp[0], sp[1], sp[2], sp[3], sp[4], sp[5],
            ss[0], ss[1], ss[2], ss[3], ss[4], ss[5], jnp.stack(vrows))
```

```python
import functools

import numpy as np
import jax
import jax.numpy as jnp
from jax import lax
from jax.experimental import pallas as pl
from jax.experimental.pallas import tpu as pltpu

F32 = jnp.float32
BF16 = jnp.bfloat16

D_MODEL = 1024
DEPTH = 2
A_WIDTH = 512
A_GROUPS = 4
GROUP_DIM = 128
B_HEADS = 8
B_KV_HEADS = 2
B_HEAD_DIM = 64
B_WIDTH = 512
B_KV_WIDTH = 128
WINDOW = 128
C_HEADS = 4
C_HEAD_DIM = 128
C_WIDTH = 512
C_CONV = 4
EPS = 1e-6
NEG = -1e30

LANES = 128
SUBLANES = 8
VMEM_LIMIT = 56 * 1024 * 1024

ZA_W = 3 * A_WIDTH
ZB_W = 2 * B_WIDTH + 2 * B_KV_WIDTH
ZC_W = 2 * C_WIDTH + 3 * C_WIDTH + LANES
ZCAT_W = ZA_W + ZB_W + ZC_W
Y_W = A_WIDTH + B_WIDTH + C_WIDTH

PROMPT_TILE = 256
MLSTM_CHUNK = PROMPT_TILE
SAMPLE_NB = 16
PROJ_TILE = 512


def _sigmoid(x):
    return 0.5 * jnp.tanh(0.5 * x) + 0.5


def _silu(x):
    t = 0.5 * x
    return t * (jnp.tanh(t) + 1.0)


def _log_sigmoid(x):
    return jnp.minimum(x, 0.0) - jnp.log1p(jnp.exp(-jnp.abs(x)))


def _rms(x):
    return x * lax.rsqrt(jnp.mean(x * x, axis=-1, keepdims=True) + EPS)


def _dot(a, b):
    return jnp.dot(a, b, preferred_element_type=F32)


def _dot_nt(a, b):
    return lax.dot_general(a, b, (((1,), (1,)), ((), ())), preferred_element_type=F32)


def _dot_exact01(m01, x):
    hi = x.astype(BF16)
    r1 = x - hi.astype(F32)
    mid = r1.astype(BF16)
    lo = (r1 - mid.astype(F32)).astype(BF16)
    return _dot(m01, hi) + _dot(m01, mid) + _dot(m01, lo)


def _modulated_norm(x, mod_ref, ng_ref):
    xn = _rms(x) * ng_ref[...]
    shift = mod_ref[:, 0:D_MODEL]
    scale = mod_ref[:, D_MODEL:2 * D_MODEL]
    return (xn * (1.0 + scale) + shift).astype(BF16)


def _head_rms_scale(x2, lane_lo):
    s0 = jnp.sum(jnp.where(lane_lo, x2, 0.0), axis=-1, keepdims=True)
    s1 = jnp.sum(jnp.where(lane_lo, 0.0, x2), axis=-1, keepdims=True)
    r0 = lax.rsqrt(s0 * (1.0 / B_HEAD_DIM) + EPS)
    r1 = lax.rsqrt(s1 * (1.0 / B_HEAD_DIM) + EPS)
    return jnp.where(lane_lo, r0, r1)


def _qk_norm(x, g_row):
    rows, width = x.shape
    lane_lo = lax.broadcasted_iota(jnp.int32, (rows, LANES), 1) < B_HEAD_DIM
    outs = []
    for j in range(width // LANES):
        slab = x[:, j * LANES:(j + 1) * LANES]
        outs.append(slab * _head_rms_scale(slab * slab, lane_lo))
    y = outs[0] if len(outs) == 1 else jnp.concatenate(outs, axis=1)
    return y * g_row


def _ada_kernel(c_ref, w_ref, b_ref, o_ref):
    c = c_ref[...]
    o_ref[...] = _dot(_silu(c).astype(BF16), w_ref[...].astype(BF16)) + b_ref[...]


def _ada_call(c_all, ada_w, ada_b):
    rows = c_all.shape[0]
    return pl.pallas_call(
        _ada_kernel,
        grid=(DEPTH, 3),
        in_specs=[
            pl.BlockSpec((rows, D_MODEL), lambda l, j: (0, 0)),
            pl.BlockSpec((None, D_MODEL, D_MODEL), lambda l, j: (l, 0, j)),
            pl.BlockSpec((None, 1, D_MODEL), lambda l, j: (l, 0, j)),
        ],
        out_specs=pl.BlockSpec((None, rows, D_MODEL), lambda l, j: (l, 0, j)),
        out_shape=jax.ShapeDtypeStruct((DEPTH, rows, 3 * D_MODEL), F32),
        compiler_params=pltpu.CompilerParams(
            dimension_semantics=("arbitrary", "arbitrary"), vmem_limit_bytes=VMEM_LIMIT),
        name="adaln_mod",
    )(c_all, ada_w, ada_b.reshape(DEPTH, 1, 3 * D_MODEL))


COL_CI = ZA_W + ZB_W + 3 * C_WIDTH
COL_CO = COL_CI + 2 * C_HEADS
COL_MG = COL_CO + 2 * C_WIDTH
PREP_CHUNK = 256
PREP_SHIFT = 2 * C_HEADS
N_MAIN = COL_CI // PREP_CHUNK
N_CO = (2 * C_WIDTH) // PREP_CHUNK
N_MG = (3 * D_MODEL) // PREP_CHUNK
J_CIF = N_MAIN + N_CO
J_MG = J_CIF + 1


def _weight_prep_kernel(wa_ref, wb_ref, wcat_ref, wmg_ref):
    j = pl.program_id(1)

    def shifted_t():
        rows = jnp.concatenate([wa_ref[PREP_SHIFT:PREP_CHUNK, :], wb_ref[...]], axis=0)
        return rows.T.astype(BF16)

    @pl.when(j < N_MAIN)
    def _():
        wcat_ref[...] = wa_ref[...].T.astype(BF16)

    @pl.when((j >= N_MAIN) & (j < J_CIF))
    def _():
        wcat_ref[...] = shifted_t()

    @pl.when(j == J_CIF)
    def _():
        row = lax.broadcasted_iota(jnp.int32, (PREP_CHUNK, D_MODEL), 0)
        wcat_ref[...] = jnp.where(row < PREP_SHIFT, wa_ref[...], 0.0).T.astype(BF16)

    @pl.when(j >= J_MG)
    def _():
        wmg_ref[...] = shifted_t()


def _weight_prep_call(w_in):
    in_width = w_in.shape[-1]
    assert in_width == COL_MG + 3 * D_MODEL
    assert COL_CI % PREP_CHUNK == 0 and COL_CO % PREP_CHUNK == PREP_SHIFT == COL_MG % PREP_CHUNK
    w_t = jnp.swapaxes(w_in, 1, 2)
    assert in_width % PREP_SHIFT == 0 and PREP_SHIFT == SUBLANES
    last_rows = in_width // PREP_SHIFT - 1
    groups_per_chunk = PREP_CHUNK // PREP_SHIFT

    def src_block(j):
        return jnp.where(j < J_CIF, j, jnp.where(j == J_CIF, N_MAIN, j - 1))

    return pl.pallas_call(
        _weight_prep_kernel,
        grid=(DEPTH, J_MG + N_MG),
        in_specs=[
            pl.BlockSpec((None, PREP_CHUNK, D_MODEL), lambda l, j: (l, src_block(j), 0)),
            pl.BlockSpec((None, PREP_SHIFT, D_MODEL),
                         lambda l, j: (l, jnp.minimum((src_block(j) + 1) * groups_per_chunk,
                                                      last_rows), 0)),
        ],
        out_specs=[
            pl.BlockSpec((None, D_MODEL, PREP_CHUNK), lambda l, j: (l, 0, jnp.minimum(j, J_CIF))),
            pl.BlockSpec((None, D_MODEL, PREP_CHUNK), lambda l, j: (l, 0, jnp.maximum(j - J_MG, 0))),
        ],
        out_shape=[
            jax.ShapeDtypeStruct((DEPTH, D_MODEL, ZCAT_W), BF16),
            jax.ShapeDtypeStruct((DEPTH, D_MODEL, 3 * D_MODEL), BF16),
        ],
        compiler_params=pltpu.CompilerParams(
            dimension_semantics=("arbitrary", "arbitrary"), vmem_limit_bytes=VMEM_LIMIT),
        name="weight_prep",
    )(w_t, w_t)


def _col_chunks(width, step):
    return [(o, min(step, width - o)) for o in range(0, width, step)]


def _inproj_pieces(get_h, w_ref, b_ref, za_ref, zb_ref, zc_ref, step):
    def piece(o_ref, off, woff, w):
        def run():
            o_ref[:, off:off + w] = _dot(get_h(), w_ref[:, woff:woff + w]) + b_ref[:, woff:woff + w]
        return run
    pieces = []
    base = 0
    for o_ref, width in ((za_ref, ZA_W), (zb_ref, ZB_W), (zc_ref, ZC_W)):
        pieces += [piece(o_ref, off, base + off, w) for off, w in _col_chunks(width, step)]
        base += width
    return pieces


def _inproj_kernel(x_ref, mod_ref, ng_ref, w_ref, b_ref, za_ref, zb_ref, zc_ref):
    h = _modulated_norm(x_ref[...], mod_ref, ng_ref)
    for piece in _inproj_pieces(lambda: h, w_ref, b_ref, za_ref, zb_ref, zc_ref, 512):
        piece()


def _mod_spec(tm, tokens_per_batch):
    if tokens_per_batch is None:
        return pl.BlockSpec((tm, 3 * D_MODEL), lambda i: (i, 0))
    tiles_per_batch = tokens_per_batch // tm
    return pl.BlockSpec((None, 1, 3 * D_MODEL), lambda i: (i // tiles_per_batch, 0, 0))


def _layer_weight_spec(layer, rows, cols):
    return pl.BlockSpec((None, rows, cols), lambda i: (layer, 0, 0), pipeline_mode=pl.Buffered(1))


def _inproj_call(layer, x2, mod, ng, wcat, bcat, tokens_per_batch):
    ntok = x2.shape[0]
    tm = PROJ_TILE
    const = lambda i: (0, 0)
    return pl.pallas_call(
        _inproj_kernel,
        grid=(ntok // tm,),
        in_specs=[
            pl.BlockSpec((tm, D_MODEL), lambda i: (i, 0)),
            _mod_spec(tm, tokens_per_batch),
            pl.BlockSpec((1, D_MODEL), const),
            _layer_weight_spec(layer, D_MODEL, ZCAT_W),
            pl.BlockSpec((1, ZCAT_W), const),
        ],
        out_specs=[
            pl.BlockSpec((tm, ZA_W), lambda i: (i, 0)),
            pl.BlockSpec((tm, ZB_W), lambda i: (i, 0)),
            pl.BlockSpec((tm, ZC_W), lambda i: (i, 0)),
        ],
        out_shape=[
            jax.ShapeDtypeStruct((ntok, ZA_W), F32),
            jax.ShapeDtypeStruct((ntok, ZB_W), F32),
            jax.ShapeDtypeStruct((ntok, ZC_W), F32),
        ],
        compiler_params=pltpu.CompilerParams(
            dimension_semantics=("arbitrary",), vmem_limit_bytes=VMEM_LIMIT),
        name="in_projection",
    )(x2, mod, ng, wcat, bcat)


def _outproj_kernel(x_ref, mod_ref, ng_ref, y_ref, wmg_ref, bmg_ref, wa_ref, wb_ref, wc_ref,
                    wo_ref, o_ref):
    x = x_ref[...]
    h = _modulated_norm(x, mod_ref, ng_ref)
    merged = None
    for i, wbr_ref in enumerate((wa_ref, wb_ref, wc_ref)):
        cols = slice(i * D_MODEL, (i + 1) * D_MODEL)
        gate = _sigmoid(_dot(h, wmg_ref[:, cols]) + bmg_ref[:, cols])
        term = gate * _dot(y_ref[:, i * A_WIDTH:(i + 1) * A_WIDTH], wbr_ref[...])
        merged = term if merged is None else merged + term
    ada_gate = mod_ref[:, 2 * D_MODEL:3 * D_MODEL]
    o_ref[...] = x + ada_gate * _dot(merged.astype(BF16), wo_ref[...])


def _outproj_call(layer, x2, mod, ng, y, wmg, bmg, wa, wb, wc, wo, tokens_per_batch):
    ntok = x2.shape[0]
    tm = PROJ_TILE
    const = lambda i: (0, 0)
    once = pl.Buffered(1)
    return pl.pallas_call(
        _outproj_kernel,
        grid=(ntok // tm,),
        in_specs=[
            pl.BlockSpec((tm, D_MODEL), lambda i: (i, 0)),
            _mod_spec(tm, tokens_per_batch),
            pl.BlockSpec((1, D_MODEL), const),
            pl.BlockSpec((tm, Y_W), lambda i: (i, 0)),
            _layer_weight_spec(layer, D_MODEL, 3 * D_MODEL),
            pl.BlockSpec((1, 3 * D_MODEL), const),
            pl.BlockSpec((A_WIDTH, D_MODEL), const, pipeline_mode=once),
            pl.BlockSpec((B_WIDTH, D_MODEL), const, pipeline_mode=once),
            pl.BlockSpec((C_WIDTH, D_MODEL), const, pipeline_mode=once),
            pl.BlockSpec((D_MODEL, D_MODEL), const, pipeline_mode=once),
        ],
        out_specs=pl.BlockSpec((tm, D_MODEL), lambda i: (i, 0)),
        out_shape=jax.ShapeDtypeStruct((ntok, D_MODEL), F32),
        compiler_params=pltpu.CompilerParams(
            dimension_semantics=("arbitrary",), vmem_limit_bytes=VMEM_LIMIT),
        name="out_projection",
    )(x2, mod, ng, y, wmg, bmg, wa, wb, wc, wo)


def _place_q_head(qn, h, rows):
    lane = lax.broadcasted_iota(jnp.int32, (rows, LANES), 1)
    slab = qn[:, (h // 2) * LANES:(h // 2 + 1) * LANES]
    src_hi = h % 2
    dst_hi = h // (B_HEADS // B_KV_HEADS)
    keep = (lane >= B_HEAD_DIM) if src_hi else (lane < B_HEAD_DIM)
    slab = jnp.where(keep, slab, 0.0)
    if src_hi != dst_hi:
        slab = pltpu.roll(slab, B_HEAD_DIM, 1)
    return slab


def _merge_head_pair(o_even, o_odd, h_even, rows):
    lane_lo = lax.broadcasted_iota(jnp.int32, (rows, LANES), 1) < B_HEAD_DIM
    kv_hi = h_even // (B_HEADS // B_KV_HEADS)
    if kv_hi:
        o_even = pltpu.roll(o_even, B_HEAD_DIM, 1)
    else:
        o_odd = pltpu.roll(o_odd, B_HEAD_DIM, 1)
    return jnp.where(lane_lo, o_even, o_odd)


def _conv_taps(xbuf, cw_ref, cb_ref, cols, ts):
    y = cb_ref[:, cols]
    for j in range(C_CONV):
        lo = SUBLANES - (C_CONV - 1) + j
        y = y + cw_ref[j:j + 1, cols] * xbuf[lo:lo + ts, cols]
    return y


def _prompt_mix_kernel(sink_ref, za_ref, zb_ref, zc_ref, vg_ref, gw_ref, gbs_ref, qg_ref, kg_ref,
                       cw_ref, cb_ref, fb_ref, hg_ref, tril_ref, band_ref, tri01_ref, tribias_ref,
                       y_ref, ko_ref, vo_ref, convo_ref, c_ref, n_ref, m_ref,
                       kprev, vprev, xbuf, first_tile, pump):
    ts = PROMPT_TILE

    u = za_ref[:, 0:A_WIDTH]
    vn = _rms(za_ref[:, A_WIDTH:2 * A_WIDTH]) * vg_ref[...]
    sg = _silu(za_ref[:, 2 * A_WIDTH:3 * A_WIDTH])
    vnb = vn.astype(BF16)
    wts = [(gw_ref[gi] * tril_ref[...]).astype(BF16) for gi in range(A_GROUPS)]
    s_rows = []
    for c in range(ts // WINDOW):
        s_cols = []
        for gi in range(A_GROUPS):
            vblk = vnb[c * WINDOW:(c + 1) * WINDOW, gi * GROUP_DIM:(gi + 1) * GROUP_DIM]
            s_cols.append(_dot(wts[gi], vblk) + gbs_ref[:, gi:gi + 1])
        s_rows.append(jnp.concatenate(s_cols, axis=1))
    s = jnp.concatenate(s_rows, axis=0)
    pump()
    y_ref[:, 0:A_WIDTH] = (u * s * sg).astype(BF16)
    pump()

    qn = _qk_norm(zb_ref[:, 0:B_WIDTH], qg_ref[...]) * (B_HEAD_DIM ** -0.5)
    pump()
    kn = _qk_norm(zb_ref[:, B_WIDTH:B_WIDTH + B_KV_WIDTH], kg_ref[...])
    vv = zb_ref[:, B_WIDTH + B_KV_WIDTH:B_WIDTH + 2 * B_KV_WIDTH]
    sgb = _silu(zb_ref[:, B_WIDTH + 2 * B_KV_WIDTH:ZB_W])
    pump()
    grp = B_HEADS // B_KV_HEADS
    nblk = ts // WINDOW
    lane_lo2 = lax.broadcasted_iota(jnp.int32, (2 * WINDOW, LANES), 1) < B_HEAD_DIM
    kblocks = [kprev[...]] + [kn[b * WINDOW:(b + 1) * WINDOW] for b in range(nblk)]
    vblocks = [vprev[...]] + [vv[b * WINDOW:(b + 1) * WINDOW] for b in range(nblk)]
    bias0 = band_ref[0] if first_tile is False else jnp.where(first_tile, band_ref[1], band_ref[0])
    bias = [bias0] + [band_ref[0]] * (nblk - 1)
    combos = [(blk, kh) for blk in range(nblk) for kh in range(B_KV_HEADS)]
    heads = [(blk, kh, g) for blk, kh in combos for g in range(grp)]
    kdup, vdup = {}, {}
    for blk in range(nblk):
        kcat = jnp.concatenate([kblocks[blk], kblocks[blk + 1]], axis=0)
        vcat = jnp.concatenate([vblocks[blk], vblocks[blk + 1]], axis=0)
        krol = pltpu.roll(kcat, B_HEAD_DIM, 1)
        vrol = pltpu.roll(vcat, B_HEAD_DIM, 1)
        for kh in range(B_KV_HEADS):
            own = lane_lo2 if kh == 0 else jnp.logical_not(lane_lo2)
            kdup[blk, kh] = jnp.where(own, kcat, krol).astype(BF16)
            vdup[blk, kh] = jnp.where(own, vcat, vrol).astype(BF16)
    pump()
    qs = {(blk, kh): jnp.concatenate(
        [_place_q_head(qn[blk * WINDOW:(blk + 1) * WINDOW], kh * grp + g, WINDOW) for g in range(grp)],
        axis=0).astype(BF16) for blk, kh in combos}
    pump()
    logits = {c: _dot_nt(qs[c], kdup[c]) for c in combos}
    pump()
    snk = {k: sink_ref[k[1] * grp + k[2]] for k in heads}
    lg = {(blk, kh, g): logits[blk, kh][g * WINDOW:(g + 1) * WINDOW] + bias[blk]
          for blk, kh, g in heads}
    pump()
    mx = {k: jnp.maximum(jnp.max(lg[k], axis=-1, keepdims=True), snk[k]) for k in heads}
    pump()
    p = {k: jnp.exp(lg[k] - mx[k]) for k in heads}
    pump()
    rden = {k: 1.0 / (jnp.sum(p[k], axis=-1, keepdims=True) + jnp.exp(snk[k] - mx[k])) for k in heads}
    pump()
    pv = {c: _dot(jnp.concatenate([p[c + (g,)].astype(BF16) for g in range(grp)], axis=0), vdup[c])
          for c in combos}
    pump()
    outs = {(blk, kh, g): pv[blk, kh][g * WINDOW:(g + 1) * WINDOW] * rden[blk, kh, g]
            for blk, kh, g in heads}
    pump()
    yb = jnp.concatenate([jnp.concatenate(
        [_merge_head_pair(outs[blk, (2 * j) // grp, (2 * j) % grp],
                          outs[blk, (2 * j + 1) // grp, (2 * j + 1) % grp], 2 * j, WINDOW)
         for j in range(B_HEADS // 2)], axis=1) for blk in range(nblk)], axis=0)
    y_ref[:, A_WIDTH:A_WIDTH + B_WIDTH] = (yb * sgb).astype(BF16)
    pump()
    kprev[...] = kblocks[nblk]
    vprev[...] = vblocks[nblk]
    ko_ref[...] = kblocks[nblk]
    vo_ref[...] = vblocks[nblk]
    pump()

    xbuf[SUBLANES:SUBLANES + ts, :] = zc_ref[:, 0:2 * C_WIDTH]
    qk = _silu(_conv_taps(xbuf, cw_ref, cb_ref, slice(0, 2 * C_WIDTH), ts))
    pump()
    tail = xbuf[ts:ts + SUBLANES, :]
    xbuf[0:SUBLANES, :] = tail
    convo_ref[...] = tail
    qall = qk[:, 0:C_WIDTH].astype(BF16)
    kall = qk[:, C_WIDTH:2 * C_WIDTH] * (C_HEAD_DIM ** -0.5)
    vall = zc_ref[:, 2 * C_WIDTH:3 * C_WIDTH].astype(BF16)
    gate_o = _sigmoid(zc_ref[:, 3 * C_WIDTH:4 * C_WIDTH]) * _silu(zc_ref[:, 4 * C_WIDTH:5 * C_WIDTH])
    pump()
    ifp = zc_ref[:, 5 * C_WIDTH:5 * C_WIDTH + LANES]
    lf = _log_sigmoid(ifp + fb_ref[...])
    pump()
    cl = MLSTM_CHUNK
    hds = range(C_HEADS)
    lane_c = lax.broadcasted_iota(jnp.int32, (cl, LANES), 1)
    lane_1 = lax.broadcasted_iota(jnp.int32, (1, LANES), 1)
    m_row = m_ref[...]
    cum_all = _dot_exact01(tri01_ref[...], lf)
    st_col = jnp.where(lane_c < C_HEADS, ifp, cum_all)
    st_row = st_col.T
    pump()
    hs = [slice(hd * C_HEAD_DIM, (hd + 1) * C_HEAD_DIM) for hd in hds]
    i_c = [st_col[:, hd:hd + 1] for hd in hds]
    cum_c = [st_col[:, C_HEADS + hd:C_HEADS + hd + 1] for hd in hds]
    i_r = [st_row[hd:hd + 1, :] for hd in hds]
    cum_r = [st_row[C_HEADS + hd:C_HEADS + hd + 1, :] for hd in hds]
    m_prev = [m_row[:, hd:hd + 1] for hd in hds]
    tribias = tribias_ref[...]
    dmat = [cum_c[hd] - cum_r[hd] + i_r[hd] + tribias for hd in hds]
    pump()
    m_inter = [cum_c[hd] + m_prev[hd] for hd in hds]
    m_t = [jnp.maximum(m_inter[hd], jnp.max(dmat[hd], axis=-1, keepdims=True)) for hd in hds]
    pump()
    q_h = [qall[:, hs[hd]] for hd in hds]
    k_h = [kall[:, hs[hd]] for hd in hds]
    v_h = [vall[:, hs[hd]] for hd in hds]
    s_qk = [_dot_nt(q_h[hd], k_h[hd].astype(BF16)) for hd in hds]
    pump()
    a = [jnp.exp(dmat[hd] - m_t[hd]) * s_qk[hd] for hd in hds]
    pump()
    w_inter = [jnp.exp(m_inter[hd] - m_t[hd]) for hd in hds]
    c_prev = [c_ref[hd] for hd in hds]
    n_prev = [n_ref[hd:hd + 1, :] for hd in hds]
    inter = [_dot(q_h[hd], c_prev[hd].astype(BF16)) for hd in hds]
    pump()
    intra = [_dot(a[hd].astype(BF16), v_h[hd]) for hd in hds]
    pump()
    den = [jnp.sum(a[hd], axis=-1, keepdims=True)
           + w_inter[hd] * jnp.sum(q_h[hd].astype(F32) * n_prev[hd], axis=-1, keepdims=True)
           for hd in hds]
    pump()
    rnorm = [1.0 / jnp.maximum(jnp.abs(den[hd]), jnp.exp(-m_t[hd])) for hd in hds]
    hh = [(intra[hd] + w_inter[hd] * inter[hd]) * rnorm[hd] for hd in hds]
    pump()
    hn = jnp.concatenate([_rms(hh[hd]) for hd in hds], axis=1) * hg_ref[...]
    y_ref[:, A_WIDTH + B_WIDTH:Y_W] = (hn * gate_o).astype(BF16)
    pump()
    total = [cum_r[hd][:, cl - 1:cl] for hd in hds]
    g_r = [total[hd] - cum_r[hd] + i_r[hd] for hd in hds]
    g_c = [total[hd] - cum_c[hd] + i_c[hd] for hd in hds]
    m_new = [jnp.maximum(total[hd] + m_prev[hd], jnp.max(g_r[hd], axis=-1, keepdims=True))
             for hd in hds]
    pump()
    kw = [jnp.exp(g_c[hd] - m_new[hd]) * k_h[hd] for hd in hds]
    decay = [jnp.exp(total[hd] + m_prev[hd] - m_new[hd]) for hd in hds]
    pump()
    upd = [_dot(kw[hd].T.astype(BF16), v_h[hd]) for hd in hds]
    pump()
    for hd in hds:
        c_ref[hd] = decay[hd] * c_prev[hd] + upd[hd]
        n_ref[hd:hd + 1, :] = decay[hd] * n_prev[hd] + jnp.sum(kw[hd], axis=0, keepdims=True)
        m_row = jnp.where(lane_1 == hd, m_new[hd], m_row)
    m_ref[...] = m_row


def _prompt_mask_constants():
    r = np.arange(WINDOW)[:, None]
    c = np.arange(2 * WINDOW)[None, :]
    band = (c > r) & (c <= r + WINDOW)
    band_first = band & (c >= WINDOW)
    band_bias = np.where(np.stack([band, band_first]), 0.0, NEG).astype(np.float32)
    tril = (np.arange(WINDOW)[:, None] >= np.arange(WINDOW)[None, :]).astype(np.float32)
    tri = np.arange(MLSTM_CHUNK)[:, None] >= np.arange(MLSTM_CHUNK)[None, :]
    return (jnp.asarray(tril), jnp.asarray(band_bias), jnp.asarray(tri, dtype=BF16),
            jnp.asarray(np.where(tri, 0.0, NEG).astype(np.float32)))


N_MIX_PARAMS = 13
MIX_PUMP_CALLS = 31
TAIL_FILL_PIECES = 8
MXU_PIECE_COLS = 256


class _Interleaver:
    def __init__(self, pieces, calls, hold_back=0):
        self._pieces = list(pieces)
        self._hold_back = hold_back
        self._spread = len(self._pieces) - hold_back
        self._emitted = 0
        self._calls = calls
        self._call = 0

    def __call__(self):
        self._call += 1
        target = (self._call * self._spread) // self._calls
        while self._emitted < target:
            self._pieces.pop(0)()
            self._emitted += 1

    def finish(self):
        assert self._call == self._calls and len(self._pieces) == self._hold_back, self._call
        return self._pieces


def _gate_pieces(h_ref, wmg_ref, bmg_ref, g_ref):
    def piece(off):
        cols = slice(off, off + MXU_PIECE_COLS)
        def run():
            g_ref[:, cols] = _sigmoid(_dot(h_ref[...], wmg_ref[:, cols]) + bmg_ref[:, cols])
        return run
    return [piece(off) for off in range(0, 3 * D_MODEL, MXU_PIECE_COLS)]


def _merge_and_project(x, mod_ref, g_ref, y_ref, wa_ref, wb_ref, wc_ref, wo_ref, fillers=()):
    fillers = list(fillers)
    per_stage = -(-len(fillers) // 4)
    merged = None
    for i, wbr_ref in enumerate((wa_ref, wb_ref, wc_ref)):
        for piece in fillers[i * per_stage:(i + 1) * per_stage]:
            piece()
        term = (g_ref[:, i * D_MODEL:(i + 1) * D_MODEL]
                * _dot(y_ref[:, i * A_WIDTH:(i + 1) * A_WIDTH], wbr_ref[...]))
        merged = term if merged is None else merged + term
    for piece in fillers[3 * per_stage:]:
        piece()
    ada_gate = mod_ref[:, 2 * D_MODEL:3 * D_MODEL]
    return x + ada_gate * _dot(merged.astype(BF16), wo_ref[...])


def _prompt_layer_kernel(tiles_per_seq, sink_ref, x2_ref, xn_ref, mod_ref, modn_ref, ng_ref,
                         wcat_ref, bcat_ref, *rest):
    mix_params = rest[:N_MIX_PARAMS]
    wmg_ref, bmg_ref, wa_ref, wb_ref, wc_ref, wo_ref = rest[N_MIX_PARAMS:N_MIX_PARAMS + 6]
    o_ref, ko_ref, vo_ref, convo_ref, c_ref, n_ref, m_ref = rest[N_MIX_PARAMS + 6:N_MIX_PARAMS + 13]
    (za0, zb0, zc0, za1, zb1, zc1, h0, h1, y_scr, g_scr, kprev, vprev, xbuf) = rest[N_MIX_PARAMS + 13:]
    ts = PROMPT_TILE
    z = ((za0, zb0, zc0), (za1, zb1, zc1))
    h = (h0, h1)
    k = pl.program_id(0)
    seq_start = (k % (tiles_per_seq // 2)) == 0

    @pl.when(k == 0)
    def _():
        h0[...] = _modulated_norm(x2_ref[0:ts, :], mod_ref, ng_ref)
        for piece in _inproj_pieces(lambda: h0[...], wcat_ref, bcat_ref, *z[0], 512):
            piece()

    @pl.when(seq_start)
    def _():
        kprev[...] = jnp.zeros_like(kprev)
        vprev[...] = jnp.zeros_like(vprev)
        xbuf[0:SUBLANES, :] = jnp.zeros((SUBLANES, 2 * C_WIDTH), F32)
        c_ref[...] = jnp.zeros_like(c_ref)
        n_ref[...] = jnp.zeros_like(n_ref)
        m_ref[...] = jnp.zeros_like(m_ref)

    for half in range(2):
        cur, nxt = half, 1 - half
        rows = slice(half * ts, (half + 1) * ts)
        if half == 0:
            h[nxt][...] = _modulated_norm(x2_ref[ts:2 * ts, :], mod_ref, ng_ref)
        else:
            h[nxt][...] = _modulated_norm(xn_ref[...], modn_ref, ng_ref)
        get_h_next = functools.partial(lambda r: r[...], h[nxt])
        hold = TAIL_FILL_PIECES if half == 1 else 0
        proj = _inproj_pieces(get_h_next, wcat_ref, bcat_ref, *z[nxt], MXU_PIECE_COLS)
        pump = _Interleaver(
            proj[:len(proj) - hold] + _gate_pieces(h[cur], wmg_ref, bmg_ref, g_scr)
            + proj[len(proj) - hold:], MIX_PUMP_CALLS, hold_back=hold)
        _prompt_mix_kernel(sink_ref, *z[cur], *mix_params,
                           y_scr, ko_ref, vo_ref, convo_ref, c_ref, n_ref, m_ref, kprev, vprev, xbuf,
                           first_tile=seq_start if half == 0 else False, pump=pump)
        o_ref[rows, :] = _merge_and_project(x2_ref[rows, :], mod_ref, g_scr, y_scr,
                                            wa_ref, wb_ref, wc_ref, wo_ref, fillers=pump.finish())


def _prompt_layer_call(layer, x2, mod, lw, batch, seq):
    ts = PROMPT_TILE
    nt = seq // ts
    assert nt % 2 == 0
    last_tile = batch * nt - 1
    const2 = lambda k: (0, 0)
    const3 = lambda k: (0, 0, 0)
    per_b3 = lambda k: ((2 * k) // nt, 0, 0)
    next_tile = lambda k: jnp.minimum(2 * k + 2, last_tile)
    once = pl.Buffered(1)
    return pl.pallas_call(
        functools.partial(_prompt_layer_kernel, nt),
        grid=(batch * nt // 2,),
        in_specs=[
            pl.BlockSpec(memory_space=pltpu.SMEM),
            pl.BlockSpec((2 * ts, D_MODEL), lambda k: (k, 0)),
            pl.BlockSpec((ts, D_MODEL), lambda k: (next_tile(k), 0)),
            pl.BlockSpec((None, 1, 3 * D_MODEL), per_b3),
            pl.BlockSpec((None, 1, 3 * D_MODEL), lambda k: (next_tile(k) // nt, 0, 0)),
            pl.BlockSpec((1, D_MODEL), const2),
            _layer_weight_spec(layer, D_MODEL, ZCAT_W),
            pl.BlockSpec((1, ZCAT_W), const2),
            pl.BlockSpec((1, A_WIDTH), const2),
            pl.BlockSpec((A_GROUPS, WINDOW, WINDOW), const3),
            pl.BlockSpec((WINDOW, LANES), const2),
            pl.BlockSpec((1, B_WIDTH), const2),
            pl.BlockSpec((1, B_KV_WIDTH), const2),
            pl.BlockSpec((C_CONV, 2 * C_WIDTH), const2),
            pl.BlockSpec((1, 2 * C_WIDTH), const2),
            pl.BlockSpec((1, LANES), const2),
            pl.BlockSpec((1, C_WIDTH), const2),
            pl.BlockSpec((WINDOW, WINDOW), const2),
            pl.BlockSpec((2, WINDOW, 2 * WINDOW), const3),
            pl.BlockSpec((MLSTM_CHUNK, MLSTM_CHUNK), const2),
            pl.BlockSpec((MLSTM_CHUNK, MLSTM_CHUNK), const2),
            _layer_weight_spec(layer, D_MODEL, 3 * D_MODEL),
            pl.BlockSpec((1, 3 * D_MODEL), const2),
            pl.BlockSpec((A_WIDTH, D_MODEL), const2, pipeline_mode=once),
            pl.BlockSpec((B_WIDTH, D_MODEL), const2, pipeline_mode=once),
            pl.BlockSpec((C_WIDTH, D_MODEL), const2, pipeline_mode=once),
            pl.BlockSpec((D_MODEL, D_MODEL), const2, pipeline_mode=once),
        ],
        out_specs=[
            pl.BlockSpec((2 * ts, D_MODEL), lambda k: (k, 0)),
            pl.BlockSpec((None, WINDOW, B_KV_WIDTH), per_b3),
            pl.BlockSpec((None, WINDOW, B_KV_WIDTH), per_b3),
            pl.BlockSpec((None, SUBLANES, 2 * C_WIDTH), per_b3),
            pl.BlockSpec((None, C_HEADS, C_HEAD_DIM, C_HEAD_DIM), lambda k: ((2 * k) // nt, 0, 0, 0)),
            pl.BlockSpec((None, C_HEADS, C_HEAD_DIM), per_b3),
            pl.BlockSpec((None, 1, LANES), per_b3),
        ],
        out_shape=[
            jax.ShapeDtypeStruct((batch * seq, D_MODEL), F32),
            jax.ShapeDtypeStruct((batch, WINDOW, B_KV_WIDTH), F32),
            jax.ShapeDtypeStruct((batch, WINDOW, B_KV_WIDTH), F32),
            jax.ShapeDtypeStruct((batch, SUBLANES, 2 * C_WIDTH), F32),
            jax.ShapeDtypeStruct((batch, C_HEADS, C_HEAD_DIM, C_HEAD_DIM), F32),
            jax.ShapeDtypeStruct((batch, C_HEADS, C_HEAD_DIM), F32),
            jax.ShapeDtypeStruct((batch, 1, LANES), F32),
        ],
        scratch_shapes=(
            [pltpu.VMEM((ts, w), F32) for w in (ZA_W, ZB_W, ZC_W)] * 2
            + [pltpu.VMEM((ts, D_MODEL), BF16)] * 2
            + [pltpu.VMEM((ts, Y_W), BF16),
               pltpu.VMEM((ts, 3 * D_MODEL), F32),
               pltpu.VMEM((WINDOW, B_KV_WIDTH), F32),
               pltpu.VMEM((WINDOW, B_KV_WIDTH), F32),
               pltpu.VMEM((ts + SUBLANES, 2 * C_WIDTH), F32)]),
        compiler_params=pltpu.CompilerParams(
            dimension_semantics=("arbitrary",), vmem_limit_bytes=VMEM_LIMIT),
        name="prompt_layer",
    )(lw["sinks"], x2, x2, mod, mod, lw["ng"], lw["wcat"], lw["bcat"],
      lw["vg"], lw["gws"], lw["gbs_col"], lw["qg"], lw["kg"], lw["cw"], lw["cb"], lw["fb"], lw["hg"],
      *_prompt_mask_constants(),
      lw["wmg"], lw["bmg"], lw["wa"], lw["wb"], lw["wc"], lw["wo"])


def _sample_mix_kernel(sink_ref, za_ref, zb_ref, zc_ref, kc_ref, vc_ref, cs_ref, c0_ref, n0_ref,
                       m0_ref, vg_ref, gwb_ref, gbs_ref, qg_ref, kg_ref, cw_ref, cb_ref, fb_ref,
                       hg_ref,
                       y_ref, vrow_ref, ko_ref, vo_ref, convo_ref, c1_ref, n1_ref, m1_ref,
                       xbuf):
    nb = SAMPLE_NB
    t = SUBLANES
    rows = nb * t
    tok_r = lax.broadcasted_iota(jnp.int32, (rows, rows), 0)
    tok_c = lax.broadcasted_iota(jnp.int32, (rows, rows), 1)
    same_b = (tok_r // t) == (tok_c // t)
    causal_b = same_b & (tok_c <= tok_r)

    u = za_ref[:, 0:A_WIDTH]
    vn = _rms(za_ref[:, A_WIDTH:2 * A_WIDTH]) * vg_ref[...]
    sg = _silu(za_ref[:, 2 * A_WIDTH:3 * A_WIDTH])
    vrow_ref[...] = vn
    vnb = vn.astype(BF16)
    s_cols = []
    for gi in range(A_GROUPS):
        s_cols.append(_dot(gwb_ref[gi], vnb[:, gi * GROUP_DIM:(gi + 1) * GROUP_DIM])
                      + gbs_ref[:, gi:gi + 1])
    y_ref[:, 0:A_WIDTH] = (u * jnp.concatenate(s_cols, axis=1) * sg).astype(BF16)

    qn = _qk_norm(zb_ref[:, 0:B_WIDTH], qg_ref[...]) * (B_HEAD_DIM ** -0.5)
    kn = _qk_norm(zb_ref[:, B_WIDTH:B_WIDTH + B_KV_WIDTH], kg_ref[...])
    vv = zb_ref[:, B_WIDTH + B_KV_WIDTH:B_WIDTH + 2 * B_KV_WIDTH]
    sgb = _silu(zb_ref[:, B_WIDTH + 2 * B_KV_WIDTH:ZB_W])
    kn3 = kn.reshape(nb, t, B_KV_WIDTH)
    vv3 = vv.reshape(nb, t, B_KV_WIDTH)
    kcache = kc_ref[...]
    vcache = vc_ref[...]
    pad = jnp.zeros((nb, WINDOW - t, B_KV_WIDTH), F32)
    kall = jnp.concatenate([kcache, kn3, pad], axis=1).astype(BF16)
    vall = jnp.concatenate([vcache, vv3, pad], axis=1).astype(BF16)
    qp = jnp.concatenate([_place_q_head(qn, h, rows).reshape(nb, t, LANES) for h in range(B_HEADS)],
                         axis=1).astype(BF16)
    logits = lax.dot_general(qp, kall, (((2,), (2,)), ((0,), (0,))), preferred_element_type=F32)
    qrow = lax.broadcasted_iota(jnp.int32, (nb, B_HEADS * t, 2 * WINDOW), 1)
    kcol = lax.broadcasted_iota(jnp.int32, (nb, B_HEADS * t, 2 * WINDOW), 2)
    qt = qrow % t
    valid = ((kcol < WINDOW) & (kcol > qt)) | ((kcol >= WINDOW) & ((kcol - WINDOW) <= qt))
    hrow = lax.broadcasted_iota(jnp.int32, (B_HEADS * t, 1), 0) // t
    snk = jnp.zeros((B_HEADS * t, 1), F32)
    for h in range(B_HEADS):
        snk = jnp.where(hrow == h, sink_ref[h], snk)
    lg = jnp.where(valid, logits, NEG)
    mx = jnp.maximum(jnp.max(lg, axis=-1, keepdims=True), snk[None])
    p = jnp.exp(lg - mx)
    den = jnp.sum(p, axis=-1, keepdims=True) + jnp.exp(snk[None] - mx)
    pv = lax.dot_general(p.astype(BF16), vall, (((2,), (1,)), ((0,), (0,))),
                         preferred_element_type=F32) / den
    head_out = [pv[:, h * t:(h + 1) * t, :].reshape(rows, LANES) for h in range(B_HEADS)]
    yb = jnp.concatenate(
        [_merge_head_pair(head_out[2 * j], head_out[2 * j + 1], 2 * j, rows)
         for j in range(B_HEADS // 2)], axis=1)
    y_ref[:, A_WIDTH:A_WIDTH + B_WIDTH] = (yb * sgb).astype(BF16)
    ko_ref[...] = jnp.concatenate([kcache[:, t:, :], kn3], axis=1)
    vo_ref[...] = jnp.concatenate([vcache[:, t:, :], vv3], axis=1)

    xbuf[:, SUBLANES - (C_CONV - 1):SUBLANES, :] = cs_ref[...]
    xbuf[:, SUBLANES:2 * SUBLANES, :] = zc_ref[:, 0:2 * C_WIDTH].reshape(nb, t, 2 * C_WIDTH)
    y3 = cb_ref[...][None]
    for j in range(C_CONV):
        lo = SUBLANES - (C_CONV - 1) + j
        y3 = y3 + cw_ref[j:j + 1, :][None] * xbuf[:, lo:lo + t, :]
    convo_ref[...] = xbuf[:, 2 * SUBLANES - (C_CONV - 1):2 * SUBLANES, :]
    qk = _silu(y3.reshape(rows, 2 * C_WIDTH))
    qall = qk[:, 0:C_WIDTH].astype(BF16)
    kall_c = qk[:, C_WIDTH:2 * C_WIDTH] * (C_HEAD_DIM ** -0.5)
    vall_c = zc_ref[:, 2 * C_WIDTH:3 * C_WIDTH].astype(BF16)
    gate_o = _sigmoid(zc_ref[:, 3 * C_WIDTH:4 * C_WIDTH]) * _silu(zc_ref[:, 4 * C_WIDTH:5 * C_WIDTH])
    ifp = zc_ref[:, 5 * C_WIDTH:5 * C_WIDTH + LANES]
    lf = _log_sigmoid(ifp + fb_ref[...])
    lane_t = lax.broadcasted_iota(jnp.int32, (rows, LANES), 1)
    cum_all = _dot_exact01(jnp.where(causal_b, 1.0, 0.0).astype(BF16), lf)
    tot_all = _dot_exact01(jnp.where(same_b, 1.0, 0.0).astype(BF16), lf)
    st_col = jnp.where(lane_t < C_HEADS, ifp, cum_all)
    st_row = st_col.T
    tot_row = tot_all.T
    m0 = m0_ref[...]
    same_b_bf = jnp.where(same_b, 1.0, 0.0).astype(BF16)
    batch_of_lane = lax.broadcasted_iota(jnp.int32, (nb, 1, rows), 2) // t
    batch_id = lax.broadcasted_iota(jnp.int32, (nb, 1, rows), 0)
    own_tok = batch_of_lane == batch_id
    h_cols = []
    m_out = jnp.zeros((rows, LANES), F32)
    for hd in range(C_HEADS):
        hs = slice(hd * C_HEAD_DIM, (hd + 1) * C_HEAD_DIM)
        i_c = st_col[:, hd:hd + 1]
        cum_c = st_col[:, C_HEADS + hd:C_HEADS + hd + 1]
        tot_c = tot_all[:, C_HEADS + hd:C_HEADS + hd + 1]
        i_r = st_row[hd:hd + 1, :]
        cum_r = st_row[C_HEADS + hd:C_HEADS + hd + 1, :]
        tot_r = tot_row[C_HEADS + hd:C_HEADS + hd + 1, :]
        m_prev = m0[:, hd:hd + 1]
        dmat = jnp.where(causal_b, cum_c - cum_r + i_r, NEG)
        m_inter = cum_c + m_prev
        m_t = jnp.maximum(m_inter, jnp.max(dmat, axis=-1, keepdims=True))
        q_h = qall[:, hs]
        k_h = kall_c[:, hs]
        v_h = vall_c[:, hs]
        a = jnp.exp(dmat - m_t) * _dot_nt(q_h, k_h.astype(BF16))
        w_inter = jnp.exp(m_inter - m_t)
        c_prev = c0_ref[:, hd]
        n_tok = jnp.broadcast_to(n0_ref[hd][:, None, :], (nb, t, C_HEAD_DIM)).reshape(rows, C_HEAD_DIM)
        inter = lax.dot_general(q_h.reshape(nb, t, C_HEAD_DIM), c_prev.astype(BF16),
                                (((2,), (1,)), ((0,), (0,))), preferred_element_type=F32)
        num = _dot(a.astype(BF16), v_h) + w_inter * inter.reshape(rows, C_HEAD_DIM)
        den = (jnp.sum(a, axis=-1, keepdims=True)
               + w_inter * jnp.sum(q_h.astype(F32) * n_tok, axis=-1, keepdims=True))
        hh = num / jnp.maximum(jnp.abs(den), jnp.exp(-m_t))
        h_cols.append(_rms(hh))
        g_r = tot_r - cum_r + i_r
        g_c = tot_c - cum_c + i_c
        m_new = jnp.maximum(tot_c + m_prev,
                            jnp.max(jnp.where(same_b, g_r, NEG), axis=-1, keepdims=True))
        kw = jnp.exp(g_c - m_new) * k_h
        decay = jnp.exp(tot_c + m_prev - m_new)
        kwt = kw.T
        lhs = jnp.where(own_tok, kwt[None], 0.0).astype(BF16).reshape(nb * C_HEAD_DIM, rows)
        upd = _dot(lhs, v_h).reshape(nb, C_HEAD_DIM, C_HEAD_DIM)
        dec_b = jnp.broadcast_to(decay, (rows, C_HEAD_DIM)).reshape(nb, t, C_HEAD_DIM)[:, 0:1, :]
        c1_ref[:, hd] = dec_b * c_prev + upd
        n1_ref[hd] = decay * n_tok + _dot(same_b_bf, kw.astype(BF16))
        m_out = jnp.where(lane_t == hd, m_new, m_out)
    m1_ref[...] = m_out
    hn = jnp.concatenate(h_cols, axis=1) * hg_ref[...]
    y_ref[:, A_WIDTH + B_WIDTH:Y_W] = (hn * gate_o).astype(BF16)


def _sample_mix_call(l, za, zb, zc, kc, vc, cs, c0, n0t, m0tok, lw, nbatch):
    nb = SAMPLE_NB
    t = SUBLANES
    rows = nb * t
    tok = lambda i: (i, 0)
    const2 = lambda i: (0, 0)
    const3 = lambda i: (0, 0, 0)
    b3 = lambda i: (i, 0, 0)
    lb4 = lambda i: (l, i, 0, 0)
    return pl.pallas_call(
        _sample_mix_kernel,
        grid=(nbatch // nb,),
        in_specs=[
            pl.BlockSpec(memory_space=pltpu.SMEM),
            pl.BlockSpec((rows, ZA_W), tok),
            pl.BlockSpec((rows, ZB_W), tok),
            pl.BlockSpec((rows, ZC_W), tok),
            pl.BlockSpec((None, nb, WINDOW, B_KV_WIDTH), lb4),
            pl.BlockSpec((None, nb, WINDOW, B_KV_WIDTH), lb4),
            pl.BlockSpec((None, nb, C_CONV - 1, 2 * C_WIDTH), lb4),
            pl.BlockSpec((None, nb, C_HEADS, C_HEAD_DIM, C_HEAD_DIM), lambda i: (l, i, 0, 0, 0)),
            pl.BlockSpec((None, C_HEADS, nb, C_HEAD_DIM), lambda i: (l, 0, i, 0)),
            pl.BlockSpec((None, rows, LANES), lambda i: (l, i, 0)),
            pl.BlockSpec((1, A_WIDTH), const2),
            pl.BlockSpec((A_GROUPS, rows, rows), const3),
            pl.BlockSpec((rows, LANES), const2),
            pl.BlockSpec((1, B_WIDTH), const2),
            pl.BlockSpec((1, B_KV_WIDTH), const2),
            pl.BlockSpec((C_CONV, 2 * C_WIDTH), const2),
            pl.BlockSpec((1, 2 * C_WIDTH), const2),
            pl.BlockSpec((1, LANES), const2),
            pl.BlockSpec((1, C_WIDTH), const2),
        ],
        out_specs=[
            pl.BlockSpec((rows, Y_W), tok),
            pl.BlockSpec((rows, A_WIDTH), tok),
            pl.BlockSpec((nb, WINDOW, B_KV_WIDTH), b3),
            pl.BlockSpec((nb, WINDOW, B_KV_WIDTH), b3),
            pl.BlockSpec((nb, C_CONV - 1, 2 * C_WIDTH), b3),
            pl.BlockSpec((nb, C_HEADS, C_HEAD_DIM, C_HEAD_DIM), lambda i: (i, 0, 0, 0)),
            pl.BlockSpec((C_HEADS, rows, C_HEAD_DIM), lambda i: (0, i, 0)),
            pl.BlockSpec((rows, LANES), tok),
        ],
        out_shape=[
            jax.ShapeDtypeStruct((nbatch * t, Y_W), BF16),
            jax.ShapeDtypeStruct((nbatch * t, A_WIDTH), F32),
            jax.ShapeDtypeStruct((nbatch, WINDOW, B_KV_WIDTH), F32),
            jax.ShapeDtypeStruct((nbatch, WINDOW, B_KV_WIDTH), F32),
            jax.ShapeDtypeStruct((nbatch, C_CONV - 1, 2 * C_WIDTH), F32),
            jax.ShapeDtypeStruct((nbatch, C_HEADS, C_HEAD_DIM, C_HEAD_DIM), F32),
            jax.ShapeDtypeStruct((C_HEADS, nbatch * t, C_HEAD_DIM), F32),
            jax.ShapeDtypeStruct((nbatch * t, LANES), F32),
        ],
        scratch_shapes=[pltpu.VMEM((nb, 2 * SUBLANES, 2 * C_WIDTH), F32)],
        compiler_params=pltpu.CompilerParams(
            dimension_semantics=("arbitrary",), vmem_limit_bytes=VMEM_LIMIT),
        name="sample_mixer",
    )(lw["sinks"], za, zb, zc, kc, vc, cs, c0, n0t, m0tok, lw["vg"], lw["gwb"], lw["gbs_tok"],
      lw["qg"], lw["kg"], lw["cw"], lw["cb"], lw["fb"], lw["hg"])


def _layer_weights(l, wcat_all, wmg_all, b_in, gmlp_vnorm_g, gmlp_ws, gmlp_bs, swa_qnorm_g,
                   swa_knorm_g, swa_sinks, mlstm_conv_w, mlstm_conv_b, mlstm_f_bias, mlstm_hnorm_g,
                   w_branch_a, w_branch_b, w_branch_c, w_out, norm_g, dec_seq):
    bl = b_in[l]
    bcat = jnp.concatenate([bl[:COL_CI], bl[COL_CO:COL_MG], bl[COL_CI:COL_CO],
                            jnp.zeros((LANES - 2 * C_HEADS,), F32)])
    t = dec_seq
    nb = SAMPLE_NB
    ws_t = gmlp_ws[l][:, :t, :t] * jnp.tril(jnp.ones((t, t), F32))
    eye = jnp.eye(nb, dtype=F32)
    gwb = jnp.einsum("bc,gts->gbtcs", eye, ws_t).reshape(A_GROUPS, nb * t, nb * t).astype(BF16)
    gbs_col = jnp.pad(gmlp_bs[l].T, ((0, 0), (0, LANES - A_GROUPS)))
    gbs_tok = jnp.pad(jnp.tile(gmlp_bs[l][:, :t].T, (nb, 1)), ((0, 0), (0, LANES - A_GROUPS)))
    fb = jnp.pad(mlstm_f_bias[l], (C_HEADS, LANES - 2 * C_HEADS)).reshape(1, LANES)
    return dict(
        ng=norm_g[l].reshape(1, D_MODEL),
        wcat=wcat_all, bcat=bcat.reshape(1, ZCAT_W),
        wmg=wmg_all, bmg=bl[COL_MG:].reshape(1, 3 * D_MODEL),
        wa=w_branch_a[l].astype(BF16), wb=w_branch_b[l].astype(BF16),
        wc=w_branch_c[l].astype(BF16), wo=w_out[l].astype(BF16),
        vg=gmlp_vnorm_g[l].reshape(1, A_WIDTH), gws=gmlp_ws[l], gwb=gwb,
        gbs_col=gbs_col, gbs_tok=gbs_tok,
        qg=jnp.tile(swa_qnorm_g[l], B_HEADS).reshape(1, B_WIDTH),
        kg=jnp.tile(swa_knorm_g[l], B_KV_HEADS).reshape(1, B_KV_WIDTH),
        sinks=swa_sinks[l],
        cw=mlstm_conv_w[l], cb=mlstm_conv_b[l].reshape(1, 2 * C_WIDTH), fb=fb,
        hg=mlstm_hnorm_g[l].reshape(1, C_WIDTH),
    )


def kernel(x_prompt, x_sample, cache_swa_k, cache_swa_v, state_mlstm_conv, state_mlstm_C, state_mlstm_n, state_mlstm_m, c_prompt, c_sample, ada_w, ada_b, norm_g, w_in, b_in, gmlp_vnorm_g, gmlp_ws, gmlp_bs, swa_qnorm_g, swa_knorm_g, swa_sinks, mlstm_conv_w, mlstm_conv_b, mlstm_f_bias, mlstm_hnorm_g, w_branch_a, w_branch_b, w_branch_c, w_out):
    batch, seq, _ = x_prompt.shape
    nbatch, dec_seq, _ = x_sample.shape
    assert dec_seq == SUBLANES and seq % PROMPT_TILE == 0 and nbatch % SAMPLE_NB == 0
    assert seq % PROJ_TILE == 0 and (nbatch * dec_seq) % PROJ_TILE == 0
    wb_len = cache_swa_k.shape[2]
    assert wb_len == WINDOW

    nc = batch + nbatch
    nc_pad = -(-nc // SUBLANES) * SUBLANES
    c_all = jnp.concatenate([c_prompt, c_sample, jnp.zeros((nc_pad - nc, D_MODEL), F32)], axis=0)
    mod_all = _ada_call(c_all, ada_w, ada_b)

    xp = x_prompt.reshape(batch * seq, D_MODEL)
    xs = x_sample.reshape(nbatch * dec_seq, D_MODEL)
    kc_all = cache_swa_k.reshape(DEPTH, nbatch, WINDOW, B_KV_WIDTH)
    vc_all = cache_swa_v.reshape(DEPTH, nbatch, WINDOW, B_KV_WIDTH)
    n0t_all = jnp.transpose(state_mlstm_n, (0, 2, 1, 3))
    m0tok_all = jnp.pad(jnp.repeat(state_mlstm_m, dec_seq, axis=1),
                        ((0, 0), (0, 0), (0, LANES - C_HEADS)))
    wcat_all, wmg_all = _weight_prep_call(w_in)
    outs_p = [[] for _ in range(6)]
    outs_s = [[] for _ in range(6)]
    vrows = []
    for l in range(DEPTH):
        lw = _layer_weights(l, wcat_all, wmg_all, b_in, gmlp_vnorm_g, gmlp_ws, gmlp_bs, swa_qnorm_g,
                            swa_knorm_g, swa_sinks, mlstm_conv_w, mlstm_conv_b, mlstm_f_bias,
                            mlstm_hnorm_g, w_branch_a, w_branch_b, w_branch_c, w_out, norm_g,
                            dec_seq)
        mod_p = mod_all[l, :batch].reshape(batch, 1, 3 * D_MODEL)
        mod_s = jnp.repeat(mod_all[l, batch:nc], dec_seq, axis=0)

        xp, ko, vo, convo, c1, n1, m1 = _prompt_layer_call(l, xp, mod_p, lw, batch, seq)
        outs_p[0].append(ko.reshape(batch, WINDOW, B_KV_HEADS, B_HEAD_DIM))
        outs_p[1].append(vo.reshape(batch, WINDOW, B_KV_HEADS, B_HEAD_DIM))
        outs_p[2].append(convo[:, SUBLANES - (C_CONV - 1):, :])
        outs_p[3].append(c1)
        outs_p[4].append(n1)
        outs_p[5].append(m1[:, 0, :C_HEADS])

        za, zb, zc = _inproj_call(l, xs, mod_s, lw["ng"], lw["wcat"], lw["bcat"], None)
        y, vrow, ko, vo, convo, c1, n1tok, m1tok = _sample_mix_call(
            l, za, zb, zc, kc_all, vc_all, state_mlstm_conv, state_mlstm_C, n0t_all, m0tok_all,
            lw, nbatch)
        xs = _outproj_call(l, xs, mod_s, lw["ng"], y, lw["wmg"], lw["bmg"], lw["wa"], lw["wb"],
                           lw["wc"], lw["wo"], None)
        outs_s[0].append(ko.reshape(nbatch, WINDOW, B_KV_HEADS, B_HEAD_DIM))
        outs_s[1].append(vo.reshape(nbatch, WINDOW, B_KV_HEADS, B_HEAD_DIM))
        outs_s[2].append(convo)
        outs_s[3].append(c1)
        outs_s[4].append(jnp.transpose(n1tok[:, ::dec_seq, :], (1, 0, 2)))
        outs_s[5].append(m1tok[::dec_seq, :C_HEADS])
        vrows.append(vrow.reshape(nbatch, dec_seq, A_WIDTH))

    sp = [jnp.stack(o) for o in outs_p]
    ss = [jnp.stack(o) for o in outs_s]
    return (xp.reshape(batch, seq, D_MODEL), xs.reshape(nbatch, dec_seq, D_MODEL),
            sp[0], sp[1], sp[2], sp[3], sp[4], sp[5],
            ss[0], ss[1], ss[2], ss[3], ss[4], ss[5], jnp.stack(vrows))
```

```python
import functools

import numpy as np
import jax
import jax.numpy as jnp
from jax import lax
from jax.experimental import pallas as pl
from jax.experimental.pallas import tpu as pltpu

F32 = jnp.float32
BF16 = jnp.bfloat16

D_MODEL = 1024
DEPTH = 2
A_WIDTH = 512
A_GROUPS = 4
GROUP_DIM = 128
B_HEADS = 8
B_KV_HEADS = 2
B_HEAD_DIM = 64
B_WIDTH = 512
B_KV_WIDTH = 128
WINDOW = 128
C_HEADS = 4
C_HEAD_DIM = 128
C_WIDTH = 512
C_CONV = 4
EPS = 1e-6
NEG = -1e30

LANES = 128
SUBLANES = 8
VMEM_LIMIT = 56 * 1024 * 1024

ZA_W = 3 * A_WIDTH
ZB_W = 2 * B_WIDTH + 2 * B_KV_WIDTH
ZC_W = 2 * C_WIDTH + 3 * C_WIDTH + LANES
ZCAT_W = ZA_W + ZB_W + ZC_W
Y_W = A_WIDTH + B_WIDTH + C_WIDTH

PROMPT_TILE = 256
MLSTM_CHUNK = PROMPT_TILE
SAMPLE_NB = 16
PROJ_TILE = 512


def _sigmoid(x):
    return 0.5 * jnp.tanh(0.5 * x) + 0.5


def _silu(x):
    t = 0.5 * x
    return t * (jnp.tanh(t) + 1.0)


def _log_sigmoid(x):
    return jnp.minimum(x, 0.0) - jnp.log1p(jnp.exp(-jnp.abs(x)))


def _rms(x):
    return x * lax.rsqrt(jnp.mean(x * x, axis=-1, keepdims=True) + EPS)


def _dot(a, b):
    return jnp.dot(a, b, preferred_element_type=F32)


def _dot_nt(a, b):
    return lax.dot_general(a, b, (((1,), (1,)), ((), ())), preferred_element_type=F32)


def _dot_exact01(m01, x):
    hi = x.astype(BF16)
    r1 = x - hi.astype(F32)
    mid = r1.astype(BF16)
    lo = (r1 - mid.astype(F32)).astype(BF16)
    return _dot(m01, hi) + _dot(m01, mid) + _dot(m01, lo)


def _modulated_norm(x, mod_ref, ng_ref):
    xn = _rms(x) * ng_ref[...]
    shift = mod_ref[:, 0:D_MODEL]
    scale = mod_ref[:, D_MODEL:2 * D_MODEL]
    return (xn * (1.0 + scale) + shift).astype(BF16)


def _head_rms_scale(x2, lane_lo):
    s0 = jnp.sum(jnp.where(lane_lo, x2, 0.0), axis=-1, keepdims=True)
    s1 = jnp.sum(jnp.where(lane_lo, 0.0, x2), axis=-1, keepdims=True)
    r0 = lax.rsqrt(s0 * (1.0 / B_HEAD_DIM) + EPS)
    r1 = lax.rsqrt(s1 * (1.0 / B_HEAD_DIM) + EPS)
    return jnp.where(lane_lo, r0, r1)


def _qk_norm(x, g_row):
    rows, width = x.shape
    lane_lo = lax.broadcasted_iota(jnp.int32, (rows, LANES), 1) < B_HEAD_DIM
    outs = []
    for j in range(width // LANES):
        slab = x[:, j * LANES:(j + 1) * LANES]
        outs.append(slab * _head_rms_scale(slab * slab, lane_lo))
    y = outs[0] if len(outs) == 1 else jnp.concatenate(outs, axis=1)
    return y * g_row


def _ada_kernel(c_ref, w_ref, b_ref, o_ref):
    c = c_ref[...]
    o_ref[...] = _dot(_silu(c).astype(BF16), w_ref[...].astype(BF16)) + b_ref[...]


def _ada_call(c_all, ada_w, ada_b):
    rows = c_all.shape[0]
    return pl.pallas_call(
        _ada_kernel,
        grid=(DEPTH, 3),
        in_specs=[
            pl.BlockSpec((rows, D_MODEL), lambda l, j: (0, 0)),
            pl.BlockSpec((None, D_MODEL, D_MODEL), lambda l, j: (l, 0, j)),
            pl.BlockSpec((None, 1, D_MODEL), lambda l, j: (l, 0, j)),
        ],
        out_specs=pl.BlockSpec((None, rows, D_MODEL), lambda l, j: (l, 0, j)),
        out_shape=jax.ShapeDtypeStruct((DEPTH, rows, 3 * D_MODEL), F32),
        compiler_params=pltpu.CompilerParams(
            dimension_semantics=("arbitrary", "arbitrary"), vmem_limit_bytes=VMEM_LIMIT),
        name="adaln_mod",
    )(c_all, ada_w, ada_b.reshape(DEPTH, 1, 3 * D_MODEL))


COL_CI = ZA_W + ZB_W + 3 * C_WIDTH
COL_CO = COL_CI + 2 * C_HEADS
COL_MG = COL_CO + 2 * C_WIDTH
PREP_CHUNK = 256
PREP_SHIFT = 2 * C_HEADS
N_MAIN = COL_CI // PREP_CHUNK
N_CO = (2 * C_WIDTH) // PREP_CHUNK
N_MG = (3 * D_MODEL) // PREP_CHUNK
J_CIF = N_MAIN + N_CO
J_MG = J_CIF + 1


def _weight_prep_kernel(wa_ref, wb_ref, wcat_ref, wmg_ref):
    j = pl.program_id(1)

    def shifted_t():
        rows = jnp.concatenate([wa_ref[PREP_SHIFT:PREP_CHUNK, :], wb_ref[...]], axis=0)
        return rows.astype(BF16).T

    @pl.when(j < N_MAIN)
    def _():
        wcat_ref[...] = wa_ref[...].astype(BF16).T

    @pl.when((j >= N_MAIN) & (j < J_CIF))
    def _():
        wcat_ref[...] = shifted_t()

    @pl.when(j == J_CIF)
    def _():
        row = lax.broadcasted_iota(jnp.int32, (PREP_CHUNK, D_MODEL), 0)
        wcat_ref[...] = jnp.where(row < PREP_SHIFT, wa_ref[...], 0.0).astype(BF16).T

    @pl.when(j >= J_MG)
    def _():
        wmg_ref[...] = shifted_t()


def _weight_prep_call(w_in):
    in_width = w_in.shape[-1]
    assert in_width == COL_MG + 3 * D_MODEL
    assert COL_CI % PREP_CHUNK == 0 and COL_CO % PREP_CHUNK == PREP_SHIFT == COL_MG % PREP_CHUNK
    w_t = jnp.swapaxes(w_in, 1, 2)
    assert in_width % PREP_SHIFT == 0 and PREP_SHIFT == SUBLANES
    last_rows = in_width // PREP_SHIFT - 1
    groups_per_chunk = PREP_CHUNK // PREP_SHIFT

    def src_block(j):
        return jnp.where(j < J_CIF, j, jnp.where(j == J_CIF, N_MAIN, j - 1))

    return pl.pallas_call(
        _weight_prep_kernel,
        grid=(DEPTH, J_MG + N_MG),
        in_specs=[
            pl.BlockSpec((None, PREP_CHUNK, D_MODEL), lambda l, j: (l, src_block(j), 0)),
            pl.BlockSpec((None, PREP_SHIFT, D_MODEL),
                         lambda l, j: (l, jnp.minimum((src_block(j) + 1) * groups_per_chunk,
                                                      last_rows), 0)),
        ],
        out_specs=[
            pl.BlockSpec((None, D_MODEL, PREP_CHUNK), lambda l, j: (l, 0, jnp.minimum(j, J_CIF))),
            pl.BlockSpec((None, D_MODEL, PREP_CHUNK), lambda l, j: (l, 0, jnp.maximum(j - J_MG, 0))),
        ],
        out_shape=[
            jax.ShapeDtypeStruct((DEPTH, D_MODEL, ZCAT_W), BF16),
            jax.ShapeDtypeStruct((DEPTH, D_MODEL, 3 * D_MODEL), BF16),
        ],
        compiler_params=pltpu.CompilerParams(
            dimension_semantics=("arbitrary", "arbitrary"), vmem_limit_bytes=VMEM_LIMIT),
        name="weight_prep",
    )(w_t, w_t)


def _col_chunks(width, step):
    return [(o, min(step, width - o)) for o in range(0, width, step)]


def _inproj_pieces(get_h, w_ref, b_ref, za_ref, zb_ref, zc_ref, step):
    def piece(o_ref, off, woff, w):
        def run():
            o_ref[:, off:off + w] = _dot(get_h(), w_ref[:, woff:woff + w]) + b_ref[:, woff:woff + w]
        return run
    pieces = []
    base = 0
    for o_ref, width in ((za_ref, ZA_W), (zb_ref, ZB_W), (zc_ref, ZC_W)):
        pieces += [piece(o_ref, off, base + off, w) for off, w in _col_chunks(width, step)]
        base += width
    return pieces


def _inproj_kernel(x_ref, mod_ref, ng_ref, w_ref, b_ref, za_ref, zb_ref, zc_ref):
    h = _modulated_norm(x_ref[...], mod_ref, ng_ref)
    for piece in _inproj_pieces(lambda: h, w_ref, b_ref, za_ref, zb_ref, zc_ref, 512):
        piece()


def _mod_spec(tm, tokens_per_batch):
    if tokens_per_batch is None:
        return pl.BlockSpec((tm, 3 * D_MODEL), lambda i: (i, 0))
    tiles_per_batch = tokens_per_batch // tm
    return pl.BlockSpec((None, 1, 3 * D_MODEL), lambda i: (i // tiles_per_batch, 0, 0))


def _layer_weight_spec(layer, rows, cols):
    return pl.BlockSpec((None, rows, cols), lambda i: (layer, 0, 0), pipeline_mode=pl.Buffered(1))


def _inproj_call(layer, x2, mod, ng, wcat, bcat, tokens_per_batch):
    ntok = x2.shape[0]
    tm = PROJ_TILE
    const = lambda i: (0, 0)
    return pl.pallas_call(
        _inproj_kernel,
        grid=(ntok // tm,),
        in_specs=[
            pl.BlockSpec((tm, D_MODEL), lambda i: (i, 0)),
            _mod_spec(tm, tokens_per_batch),
            pl.BlockSpec((1, D_MODEL), const),
            _layer_weight_spec(layer, D_MODEL, ZCAT_W),
            pl.BlockSpec((1, ZCAT_W), const),
        ],
        out_specs=[
            pl.BlockSpec((tm, ZA_W), lambda i: (i, 0)),
            pl.BlockSpec((tm, ZB_W), lambda i: (i, 0)),
            pl.BlockSpec((tm, ZC_W), lambda i: (i, 0)),
        ],
        out_shape=[
            jax.ShapeDtypeStruct((ntok, ZA_W), F32),
            jax.ShapeDtypeStruct((ntok, ZB_W), F32),
            jax.ShapeDtypeStruct((ntok, ZC_W), F32),
        ],
        compiler_params=pltpu.CompilerParams(
            dimension_semantics=("arbitrary",), vmem_limit_bytes=VMEM_LIMIT),
        name="in_projection",
    )(x2, mod, ng, wcat, bcat)


def _outproj_kernel(x_ref, mod_ref, ng_ref, y_ref, wmg_ref, bmg_ref, wa_ref, wb_ref, wc_ref,
                    wo_ref, o_ref):
    x = x_ref[...]
    h = _modulated_norm(x, mod_ref, ng_ref)
    merged = None
    for i, wbr_ref in enumerate((wa_ref, wb_ref, wc_ref)):
        cols = slice(i * D_MODEL, (i + 1) * D_MODEL)
        gate = _sigmoid(_dot(h, wmg_ref[:, cols]) + bmg_ref[:, cols])
        term = gate * _dot(y_ref[:, i * A_WIDTH:(i + 1) * A_WIDTH], wbr_ref[...])
        merged = term if merged is None else merged + term
    ada_gate = mod_ref[:, 2 * D_MODEL:3 * D_MODEL]
    o_ref[...] = x + ada_gate * _dot(merged.astype(BF16), wo_ref[...])


def _outproj_call(layer, x2, mod, ng, y, wmg, bmg, wa, wb, wc, wo, tokens_per_batch):
    ntok = x2.shape[0]
    tm = PROJ_TILE
    const = lambda i: (0, 0)
    once = pl.Buffered(1)
    return pl.pallas_call(
        _outproj_kernel,
        grid=(ntok // tm,),
        in_specs=[
            pl.BlockSpec((tm, D_MODEL), lambda i: (i, 0)),
            _mod_spec(tm, tokens_per_batch),
            pl.BlockSpec((1, D_MODEL), const),
            pl.BlockSpec((tm, Y_W), lambda i: (i, 0)),
            _layer_weight_spec(layer, D_MODEL, 3 * D_MODEL),
            pl.BlockSpec((1, 3 * D_MODEL), const),
            pl.BlockSpec((A_WIDTH, D_MODEL), const, pipeline_mode=once),
            pl.BlockSpec((B_WIDTH, D_MODEL), const, pipeline_mode=once),
            pl.BlockSpec((C_WIDTH, D_MODEL), const, pipeline_mode=once),
            pl.BlockSpec((D_MODEL, D_MODEL), const, pipeline_mode=once),
        ],
        out_specs=pl.BlockSpec((tm, D_MODEL), lambda i: (i, 0)),
        out_shape=jax.ShapeDtypeStruct((ntok, D_MODEL), F32),
        compiler_params=pltpu.CompilerParams(
            dimension_semantics=("arbitrary",), vmem_limit_bytes=VMEM_LIMIT),
        name="out_projection",
    )(x2, mod, ng, y, wmg, bmg, wa, wb, wc, wo)


def _place_q_head(qn, h, rows):
    lane = lax.broadcasted_iota(jnp.int32, (rows, LANES), 1)
    slab = qn[:, (h // 2) * LANES:(h // 2 + 1) * LANES]
    src_hi = h % 2
    dst_hi = h // (B_HEADS // B_KV_HEADS)
    keep = (lane >= B_HEAD_DIM) if src_hi else (lane < B_HEAD_DIM)
    slab = jnp.where(keep, slab, 0.0)
    if src_hi != dst_hi:
        slab = pltpu.roll(slab, B_HEAD_DIM, 1)
    return slab


def _merge_head_pair(o_even, o_odd, h_even, rows):
    lane_lo = lax.broadcasted_iota(jnp.int32, (rows, LANES), 1) < B_HEAD_DIM
    kv_hi = h_even // (B_HEADS // B_KV_HEADS)
    if kv_hi:
        o_even = pltpu.roll(o_even, B_HEAD_DIM, 1)
    else:
        o_odd = pltpu.roll(o_odd, B_HEAD_DIM, 1)
    return jnp.where(lane_lo, o_even, o_odd)


def _conv_taps(xbuf, cw_ref, cb_ref, cols, ts):
    y = cb_ref[:, cols]
    for j in range(C_CONV):
        lo = SUBLANES - (C_CONV - 1) + j
        y = y + cw_ref[j:j + 1, cols] * xbuf[lo:lo + ts, cols]
    return y


def _prompt_mix_kernel(sink_ref, za_ref, zb_ref, zc_ref, vg_ref, gw_ref, gbs_ref, qg_ref, kg_ref,
                       cw_ref, cb_ref, fb_ref, hg_ref, tril_ref, band_ref, tri01_ref, tribias_ref,
                       y_ref, ko_ref, vo_ref, convo_ref, c_ref, n_ref, m_ref,
                       kprev, vprev, xbuf, first_tile, pump):
    ts = PROMPT_TILE

    u = za_ref[:, 0:A_WIDTH]
    vn = _rms(za_ref[:, A_WIDTH:2 * A_WIDTH]) * vg_ref[...]
    sg = _silu(za_ref[:, 2 * A_WIDTH:3 * A_WIDTH])
    vnb = vn.astype(BF16)
    wts = [(gw_ref[gi] * tril_ref[...]).astype(BF16) for gi in range(A_GROUPS)]
    s_rows = []
    for c in range(ts // WINDOW):
        s_cols = []
        for gi in range(A_GROUPS):
            vblk = vnb[c * WINDOW:(c + 1) * WINDOW, gi * GROUP_DIM:(gi + 1) * GROUP_DIM]
            s_cols.append(_dot(wts[gi], vblk) + gbs_ref[:, gi:gi + 1])
        s_rows.append(jnp.concatenate(s_cols, axis=1))
    s = jnp.concatenate(s_rows, axis=0)
    pump()
    y_ref[:, 0:A_WIDTH] = (u * s * sg).astype(BF16)
    pump()

    qn = _qk_norm(zb_ref[:, 0:B_WIDTH], qg_ref[...]) * (B_HEAD_DIM ** -0.5)
    pump()
    kn = _qk_norm(zb_ref[:, B_WIDTH:B_WIDTH + B_KV_WIDTH], kg_ref[...])
    vv = zb_ref[:, B_WIDTH + B_KV_WIDTH:B_WIDTH + 2 * B_KV_WIDTH]
    sgb = _silu(zb_ref[:, B_WIDTH + 2 * B_KV_WIDTH:ZB_W])
    pump()
    grp = B_HEADS // B_KV_HEADS
    nblk = ts // WINDOW
    lane_lo2 = lax.broadcasted_iota(jnp.int32, (2 * WINDOW, LANES), 1) < B_HEAD_DIM
    kblocks = [kprev[...]] + [kn[b * WINDOW:(b + 1) * WINDOW] for b in range(nblk)]
    vblocks = [vprev[...]] + [vv[b * WINDOW:(b + 1) * WINDOW] for b in range(nblk)]
    bias0 = band_ref[0] if first_tile is False else jnp.where(first_tile, band_ref[1], band_ref[0])
    bias = [bias0] + [band_ref[0]] * (nblk - 1)
    combos = [(blk, kh) for blk in range(nblk) for kh in range(B_KV_HEADS)]
    heads = [(blk, kh, g) for blk, kh in combos for g in range(grp)]
    kdup, vdup = {}, {}
    for blk in range(nblk):
        kcat = jnp.concatenate([kblocks[blk], kblocks[blk + 1]], axis=0)
        vcat = jnp.concatenate([vblocks[blk], vblocks[blk + 1]], axis=0)
        krol = pltpu.roll(kcat, B_HEAD_DIM, 1)
        vrol = pltpu.roll(vcat, B_HEAD_DIM, 1)
        for kh in range(B_KV_HEADS):
            own = lane_lo2 if kh == 0 else jnp.logical_not(lane_lo2)
            kdup[blk, kh] = jnp.where(own, kcat, krol).astype(BF16)
            vdup[blk, kh] = jnp.where(own, vcat, vrol).astype(BF16)
    pump()
    qs = {(blk, kh): jnp.concatenate(
        [_place_q_head(qn[blk * WINDOW:(blk + 1) * WINDOW], kh * grp + g, WINDOW) for g in range(grp)],
        axis=0).astype(BF16) for blk, kh in combos}
    pump()
    logits = {c: _dot_nt(qs[c], kdup[c]) for c in combos}
    pump()
    snk = {k: sink_ref[k[1] * grp + k[2]] for k in heads}
    lg = {(blk, kh, g): logits[blk, kh][g * WINDOW:(g + 1) * WINDOW] + bias[blk]
          for blk, kh, g in heads}
    pump()
    mx = {k: jnp.maximum(jnp.max(lg[k], axis=-1, keepdims=True), snk[k]) for k in heads}
    pump()
    p = {k: jnp.exp(lg[k] - mx[k]) for k in heads}
    pump()
    rden = {k: 1.0 / (jnp.sum(p[k], axis=-1, keepdims=True) + jnp.exp(snk[k] - mx[k])) for k in heads}
    pump()
    pv = {c: _dot(jnp.concatenate([p[c + (g,)].astype(BF16) for g in range(grp)], axis=0), vdup[c])
          for c in combos}
    pump()
    outs = {(blk, kh, g): pv[blk, kh][g * WINDOW:(g + 1) * WINDOW] * rden[blk, kh, g]
            for blk, kh, g in heads}
    pump()
    yb = jnp.concatenate([jnp.concatenate(
        [_merge_head_pair(outs[blk, (2 * j) // grp, (2 * j) % grp],
                          outs[blk, (2 * j + 1) // grp, (2 * j + 1) % grp], 2 * j, WINDOW)
         for j in range(B_HEADS // 2)], axis=1) for blk in range(nblk)], axis=0)
    y_ref[:, A_WIDTH:A_WIDTH + B_WIDTH] = (yb * sgb).astype(BF16)
    pump()
    kprev[...] = kblocks[nblk]
    vprev[...] = vblocks[nblk]
    ko_ref[...] = kblocks[nblk]
    vo_ref[...] = vblocks[nblk]
    pump()

    xbuf[SUBLANES:SUBLANES + ts, :] = zc_ref[:, 0:2 * C_WIDTH]
    qk = _silu(_conv_taps(xbuf, cw_ref, cb_ref, slice(0, 2 * C_WIDTH), ts))
    pump()
    tail = xbuf[ts:ts + SUBLANES, :]
    xbuf[0:SUBLANES, :] = tail
    convo_ref[...] = tail
    qall = qk[:, 0:C_WIDTH].astype(BF16)
    kall = qk[:, C_WIDTH:2 * C_WIDTH] * (C_HEAD_DIM ** -0.5)
    vall = zc_ref[:, 2 * C_WIDTH:3 * C_WIDTH].astype(BF16)
    gate_o = _sigmoid(zc_ref[:, 3 * C_WIDTH:4 * C_WIDTH]) * _silu(zc_ref[:, 4 * C_WIDTH:5 * C_WIDTH])
    pump()
    ifp = zc_ref[:, 5 * C_WIDTH:5 * C_WIDTH + LANES]
    lf = _log_sigmoid(ifp + fb_ref[...])
    pump()
    cl = MLSTM_CHUNK
    hds = range(C_HEADS)
    lane_c = lax.broadcasted_iota(jnp.int32, (cl, LANES), 1)
    lane_1 = lax.broadcasted_iota(jnp.int32, (1, LANES), 1)
    m_row = m_ref[...]
    cum_all = _dot_exact01(tri01_ref[...], lf)
    st_col = jnp.where(lane_c < C_HEADS, ifp, cum_all)
    st_row = st_col.T
    pump()
    hs = [slice(hd * C_HEAD_DIM, (hd + 1) * C_HEAD_DIM) for hd in hds]
    i_c = [st_col[:, hd:hd + 1] for hd in hds]
    cum_c = [st_col[:, C_HEADS + hd:C_HEADS + hd + 1] for hd in hds]
    i_r = [st_row[hd:hd + 1, :] for hd in hds]
    cum_r = [st_row[C_HEADS + hd:C_HEADS + hd + 1, :] for hd in hds]
    m_prev = [m_row[:, hd:hd + 1] for hd in hds]
    tribias = tribias_ref[...]
    dmat = [cum_c[hd] - cum_r[hd] + i_r[hd] + tribias for hd in hds]
    pump()
    m_inter = [cum_c[hd] + m_prev[hd] for hd in hds]
    m_t = [jnp.maximum(m_inter[hd], jnp.max(dmat[hd], axis=-1, keepdims=True)) for hd in hds]
    pump()
    q_h = [qall[:, hs[hd]] for hd in hds]
    k_h = [kall[:, hs[hd]] for hd in hds]
    v_h = [vall[:, hs[hd]] for hd in hds]
    s_qk = [_dot_nt(q_h[hd], k_h[hd].astype(BF16)) for hd in hds]
    pump()
    a = [jnp.exp(dmat[hd] - m_t[hd]) * s_qk[hd] for hd in hds]
    pump()
    w_inter = [jnp.exp(m_inter[hd] - m_t[hd]) for hd in hds]
    c_prev = [c_ref[hd] for hd in hds]
    n_prev = [n_ref[hd:hd + 1, :] for hd in hds]
    inter = [_dot(q_h[hd], c_prev[hd].astype(BF16)) for hd in hds]
    pump()
    intra = [_dot(a[hd].astype(BF16), v_h[hd]) for hd in hds]
    pump()
    den = [jnp.sum(a[hd], axis=-1, keepdims=True)
           + w_inter[hd] * jnp.sum(q_h[hd].astype(F32) * n_prev[hd], axis=-1, keepdims=True)
           for hd in hds]
    pump()
    rnorm = [1.0 / jnp.maximum(jnp.abs(den[hd]), jnp.exp(-m_t[hd])) for hd in hds]
    hh = [(intra[hd] + w_inter[hd] * inter[hd]) * rnorm[hd] for hd in hds]
    pump()
    hn = jnp.concatenate([_rms(hh[hd]) for hd in hds], axis=1) * hg_ref[...]
    y_ref[:, A_WIDTH + B_WIDTH:Y_W] = (hn * gate_o).astype(BF16)
    pump()
    total = [cum_r[hd][:, cl - 1:cl] for hd in hds]
    g_r = [total[hd] - cum_r[hd] + i_r[hd] for hd in hds]
    g_c = [total[hd] - cum_c[hd] + i_c[hd] for hd in hds]
    m_new = [jnp.maximum(total[hd] + m_prev[hd], jnp.max(g_r[hd], axis=-1, keepdims=True))
             for hd in hds]
    pump()
    kw = [jnp.exp(g_c[hd] - m_new[hd]) * k_h[hd] for hd in hds]
    decay = [jnp.exp(total[hd] + m_prev[hd] - m_new[hd]) for hd in hds]
    pump()
    upd = [_dot(kw[hd].T.astype(BF16), v_h[hd]) for hd in hds]
    pump()
    for hd in hds:
        c_ref[hd] = decay[hd] * c_prev[hd] + upd[hd]
        n_ref[hd:hd + 1, :] = decay[hd] * n_prev[hd] + jnp.sum(kw[hd], axis=0, keepdims=True)
        m_row = jnp.where(lane_1 == hd, m_new[hd], m_row)
    m_ref[...] = m_row


def _prompt_mask_constants():
    r = np.arange(WINDOW)[:, None]
    c = np.arange(2 * WINDOW)[None, :]
    band = (c > r) & (c <= r + WINDOW)
    band_first = band & (c >= WINDOW)
    band_bias = np.where(np.stack([band, band_first]), 0.0, NEG).astype(np.float32)
    tril = (np.arange(WINDOW)[:, None] >= np.arange(WINDOW)[None, :]).astype(np.float32)
    tri = np.arange(MLSTM_CHUNK)[:, None] >= np.arange(MLSTM_CHUNK)[None, :]
    return (jnp.asarray(tril), jnp.asarray(band_bias), jnp.asarray(tri, dtype=BF16),
            jnp.asarray(np.where(tri, 0.0, NEG).astype(np.float32)))


N_MIX_PARAMS = 13
MIX_PUMP_CALLS = 31
TAIL_FILL_PIECES = 8
MXU_PIECE_COLS = 256


MIX_PHASE_WORK = (3812, 1100, 800, 700, 600, 100, 512, 1030, 1024, 2050, 320, 512, 600, 100, 5100,
                  1700, 300, 300, 1040, 640, 100, 768, 516, 128, 772, 768, 1000, 160, 150, 100, 290)
PHASE_WORK_BLEND = 0.5


class _Interleaver:
    def __init__(self, pieces, calls, hold_back=0):
        assert calls == len(MIX_PHASE_WORK)
        self._pieces = list(pieces)
        self._hold_back = hold_back
        self._spread = len(self._pieces) - hold_back
        self._emitted = 0
        self._calls = calls
        self._call = 0
        work = np.asarray(MIX_PHASE_WORK, np.float64)
        share = (1.0 - PHASE_WORK_BLEND) / calls + PHASE_WORK_BLEND * work / work.sum()
        self._cum = np.cumsum(share)

    def __call__(self):
        target = int(round(self._cum[self._call] * self._spread))
        self._call += 1
        while self._emitted < target:
            self._pieces.pop(0)()
            self._emitted += 1

    def finish(self):
        assert self._call == self._calls and len(self._pieces) == self._hold_back, self._call
        return self._pieces


def _gate_pieces(h_ref, wmg_ref, bmg_ref, g_ref):
    def piece(off):
        cols = slice(off, off + MXU_PIECE_COLS)
        def run():
            g_ref[:, cols] = _sigmoid(_dot(h_ref[...], wmg_ref[:, cols]) + bmg_ref[:, cols])
        return run
    return [piece(off) for off in range(0, 3 * D_MODEL, MXU_PIECE_COLS)]


def _merge_and_project(x, mod_ref, g_ref, y_ref, wa_ref, wb_ref, wc_ref, wo_ref, fillers=()):
    fillers = list(fillers)
    per_stage = -(-len(fillers) // 4)
    merged = None
    for i, wbr_ref in enumerate((wa_ref, wb_ref, wc_ref)):
        for piece in fillers[i * per_stage:(i + 1) * per_stage]:
            piece()
        term = (g_ref[:, i * D_MODEL:(i + 1) * D_MODEL]
                * _dot(y_ref[:, i * A_WIDTH:(i + 1) * A_WIDTH], wbr_ref[...]))
        merged = term if merged is None else merged + term
    for piece in fillers[3 * per_stage:]:
        piece()
    ada_gate = mod_ref[:, 2 * D_MODEL:3 * D_MODEL]
    return x + ada_gate * _dot(merged.astype(BF16), wo_ref[...])


def _prompt_layer_kernel(tiles_per_seq, sink_ref, x2_ref, xn_ref, mod_ref, modn_ref, ng_ref,
                         wcat_ref, bcat_ref, *rest):
    mix_params = rest[:N_MIX_PARAMS]
    wmg_ref, bmg_ref, wa_ref, wb_ref, wc_ref, wo_ref = rest[N_MIX_PARAMS:N_MIX_PARAMS + 6]
    o_ref, ko_ref, vo_ref, convo_ref, c_ref, n_ref, m_ref = rest[N_MIX_PARAMS + 6:N_MIX_PARAMS + 13]
    (za0, zb0, zc0, za1, zb1, zc1, h0, h1, y_scr, g_scr, kprev, vprev, xbuf) = rest[N_MIX_PARAMS + 13:]
    ts = PROMPT_TILE
    z = ((za0, zb0, zc0), (za1, zb1, zc1))
    h = (h0, h1)
    k = pl.program_id(0)
    seq_start = (k % (tiles_per_seq // 2)) == 0

    @pl.when(k == 0)
    def _():
        h0[...] = _modulated_norm(x2_ref[0:ts, :], mod_ref, ng_ref)
        for piece in _inproj_pieces(lambda: h0[...], wcat_ref, bcat_ref, *z[0], 512):
            piece()

    @pl.when(seq_start)
    def _():
        kprev[...] = jnp.zeros_like(kprev)
        vprev[...] = jnp.zeros_like(vprev)
        xbuf[0:SUBLANES, :] = jnp.zeros((SUBLANES, 2 * C_WIDTH), F32)
        c_ref[...] = jnp.zeros_like(c_ref)
        n_ref[...] = jnp.zeros_like(n_ref)
        m_ref[...] = jnp.zeros_like(m_ref)

    for half in range(2):
        cur, nxt = half, 1 - half
        rows = slice(half * ts, (half + 1) * ts)
        if half == 0:
            h[nxt][...] = _modulated_norm(x2_ref[ts:2 * ts, :], mod_ref, ng_ref)
        else:
            h[nxt][...] = _modulated_norm(xn_ref[...], modn_ref, ng_ref)
        get_h_next = functools.partial(lambda r: r[...], h[nxt])
        hold = TAIL_FILL_PIECES if half == 1 else 0
        proj = _inproj_pieces(get_h_next, wcat_ref, bcat_ref, *z[nxt], MXU_PIECE_COLS)
        pump = _Interleaver(
            proj[:len(proj) - hold] + _gate_pieces(h[cur], wmg_ref, bmg_ref, g_scr)
            + proj[len(proj) - hold:], MIX_PUMP_CALLS, hold_back=hold)
        _prompt_mix_kernel(sink_ref, *z[cur], *mix_params,
                           y_scr, ko_ref, vo_ref, convo_ref, c_ref, n_ref, m_ref, kprev, vprev, xbuf,
                           first_tile=seq_start if half == 0 else False, pump=pump)
        o_ref[rows, :] = _merge_and_project(x2_ref[rows, :], mod_ref, g_scr, y_scr,
                                            wa_ref, wb_ref, wc_ref, wo_ref, fillers=pump.finish())


def _prompt_layer_call(layer, x2, mod, lw, batch, seq):
    ts = PROMPT_TILE
    nt = seq // ts
    assert nt % 2 == 0
    last_tile = batch * nt - 1
    const2 = lambda k: (0, 0)
    const3 = lambda k: (0, 0, 0)
    per_b3 = lambda k: ((2 * k) // nt, 0, 0)
    next_tile = lambda k: jnp.minimum(2 * k + 2, last_tile)
    once = pl.Buffered(1)
    return pl.pallas_call(
        functools.partial(_prompt_layer_kernel, nt),
        grid=(batch * nt // 2,),
        in_specs=[
            pl.BlockSpec(memory_space=pltpu.SMEM),
            pl.BlockSpec((2 * ts, D_MODEL), lambda k: (k, 0)),
            pl.BlockSpec((ts, D_MODEL), lambda k: (next_tile(k), 0)),
            pl.BlockSpec((None, 1, 3 * D_MODEL), per_b3),
            pl.BlockSpec((None, 1, 3 * D_MODEL), lambda k: (next_tile(k) // nt, 0, 0)),
            pl.BlockSpec((1, D_MODEL), const2),
            _layer_weight_spec(layer, D_MODEL, ZCAT_W),
            pl.BlockSpec((1, ZCAT_W), const2),
            pl.BlockSpec((1, A_WIDTH), const2),
            pl.BlockSpec((A_GROUPS, WINDOW, WINDOW), const3),
            pl.BlockSpec((WINDOW, LANES), const2),
            pl.BlockSpec((1, B_WIDTH), const2),
            pl.BlockSpec((1, B_KV_WIDTH), const2),
            pl.BlockSpec((C_CONV, 2 * C_WIDTH), const2),
            pl.BlockSpec((1, 2 * C_WIDTH), const2),
            pl.BlockSpec((1, LANES), const2),
            pl.BlockSpec((1, C_WIDTH), const2),
            pl.BlockSpec((WINDOW, WINDOW), const2),
            pl.BlockSpec((2, WINDOW, 2 * WINDOW), const3),
            pl.BlockSpec((MLSTM_CHUNK, MLSTM_CHUNK), const2),
            pl.BlockSpec((MLSTM_CHUNK, MLSTM_CHUNK), const2),
            _layer_weight_spec(layer, D_MODEL, 3 * D_MODEL),
            pl.BlockSpec((1, 3 * D_MODEL), const2),
            pl.BlockSpec((A_WIDTH, D_MODEL), const2, pipeline_mode=once),
            pl.BlockSpec((B_WIDTH, D_MODEL), const2, pipeline_mode=once),
            pl.BlockSpec((C_WIDTH, D_MODEL), const2, pipeline_mode=once),
            pl.BlockSpec((D_MODEL, D_MODEL), const2, pipeline_mode=once),
        ],
        out_specs=[
            pl.BlockSpec((2 * ts, D_MODEL), lambda k: (k, 0)),
            pl.BlockSpec((None, WINDOW, B_KV_WIDTH), per_b3),
            pl.BlockSpec((None, WINDOW, B_KV_WIDTH), per_b3),
            pl.BlockSpec((None, SUBLANES, 2 * C_WIDTH), per_b3),
            pl.BlockSpec((None, C_HEADS, C_HEAD_DIM, C_HEAD_DIM), lambda k: ((2 * k) // nt, 0, 0, 0)),
            pl.BlockSpec((None, C_HEADS, C_HEAD_DIM), per_b3),
            pl.BlockSpec((None, 1, LANES), per_b3),
        ],
        out_shape=[
            jax.ShapeDtypeStruct((batch * seq, D_MODEL), F32),
            jax.ShapeDtypeStruct((batch, WINDOW, B_KV_WIDTH), F32),
            jax.ShapeDtypeStruct((batch, WINDOW, B_KV_WIDTH), F32),
            jax.ShapeDtypeStruct((batch, SUBLANES, 2 * C_WIDTH), F32),
            jax.ShapeDtypeStruct((batch, C_HEADS, C_HEAD_DIM, C_HEAD_DIM), F32),
            jax.ShapeDtypeStruct((batch, C_HEADS, C_HEAD_DIM), F32),
            jax.ShapeDtypeStruct((batch, 1, LANES), F32),
        ],
        scratch_shapes=(
            [pltpu.VMEM((ts, w), F32) for w in (ZA_W, ZB_W, ZC_W)] * 2
            + [pltpu.VMEM((ts, D_MODEL), BF16)] * 2
            + [pltpu.VMEM((ts, Y_W), BF16),
               pltpu.VMEM((ts, 3 * D_MODEL), F32),
               pltpu.VMEM((WINDOW, B_KV_WIDTH), F32),
               pltpu.VMEM((WINDOW, B_KV_WIDTH), F32),
               pltpu.VMEM((ts + SUBLANES, 2 * C_WIDTH), F32)]),
        compiler_params=pltpu.CompilerParams(
            dimension_semantics=("arbitrary",), vmem_limit_bytes=VMEM_LIMIT),
        name="prompt_layer",
    )(lw["sinks"], x2, x2, mod, mod, lw["ng"], lw["wcat"], lw["bcat"],
      lw["vg"], lw["gws"], lw["gbs_col"], lw["qg"], lw["kg"], lw["cw"], lw["cb"], lw["fb"], lw["hg"],
      *_prompt_mask_constants(),
      lw["wmg"], lw["bmg"], lw["wa"], lw["wb"], lw["wc"], lw["wo"])


def _sample_mix_kernel(sink_ref, za_ref, zb_ref, zc_ref, kc_ref, vc_ref, cs_ref, c0_ref, n0_ref,
                       m0_ref, vg_ref, gwb_ref, gbs_ref, qg_ref, kg_ref, cw_ref, cb_ref, fb_ref,
                       hg_ref,
                       y_ref, vrow_ref, ko_ref, vo_ref, convo_ref, c1_ref, n1_ref, m1_ref,
                       xbuf):
    nb = SAMPLE_NB
    t = SUBLANES
    rows = nb * t
    tok_r = lax.broadcasted_iota(jnp.int32, (rows, rows), 0)
    tok_c = lax.broadcasted_iota(jnp.int32, (rows, rows), 1)
    same_b = (tok_r // t) == (tok_c // t)
    causal_b = same_b & (tok_c <= tok_r)

    u = za_ref[:, 0:A_WIDTH]
    vn = _rms(za_ref[:, A_WIDTH:2 * A_WIDTH]) * vg_ref[...]
    sg = _silu(za_ref[:, 2 * A_WIDTH:3 * A_WIDTH])
    vrow_ref[...] = vn
    vnb = vn.astype(BF16)
    s_cols = []
    for gi in range(A_GROUPS):
        s_cols.append(_dot(gwb_ref[gi], vnb[:, gi * GROUP_DIM:(gi + 1) * GROUP_DIM])
                      + gbs_ref[:, gi:gi + 1])
    y_ref[:, 0:A_WIDTH] = (u * jnp.concatenate(s_cols, axis=1) * sg).astype(BF16)

    qn = _qk_norm(zb_ref[:, 0:B_WIDTH], qg_ref[...]) * (B_HEAD_DIM ** -0.5)
    kn = _qk_norm(zb_ref[:, B_WIDTH:B_WIDTH + B_KV_WIDTH], kg_ref[...])
    vv = zb_ref[:, B_WIDTH + B_KV_WIDTH:B_WIDTH + 2 * B_KV_WIDTH]
    sgb = _silu(zb_ref[:, B_WIDTH + 2 * B_KV_WIDTH:ZB_W])
    kn3 = kn.reshape(nb, t, B_KV_WIDTH)
    vv3 = vv.reshape(nb, t, B_KV_WIDTH)
    kcache = kc_ref[...]
    vcache = vc_ref[...]
    pad = jnp.zeros((nb, WINDOW - t, B_KV_WIDTH), F32)
    kall = jnp.concatenate([kcache, kn3, pad], axis=1).astype(BF16)
    vall = jnp.concatenate([vcache, vv3, pad], axis=1).astype(BF16)
    qp = jnp.concatenate([_place_q_head(qn, h, rows).reshape(nb, t, LANES) for h in range(B_HEADS)],
                         axis=1).astype(BF16)
    logits = lax.dot_general(qp, kall, (((2,), (2,)), ((0,), (0,))), preferred_element_type=F32)
    qrow = lax.broadcasted_iota(jnp.int32, (nb, B_HEADS * t, 2 * WINDOW), 1)
    kcol = lax.broadcasted_iota(jnp.int32, (nb, B_HEADS * t, 2 * WINDOW), 2)
    qt = qrow % t
    valid = ((kcol < WINDOW) & (kcol > qt)) | ((kcol >= WINDOW) & ((kcol - WINDOW) <= qt))
    hrow = lax.broadcasted_iota(jnp.int32, (B_HEADS * t, 1), 0) // t
    snk = jnp.zeros((B_HEADS * t, 1), F32)
    for h in range(B_HEADS):
        snk = jnp.where(hrow == h, sink_ref[h], snk)
    lg = jnp.where(valid, logits, NEG)
    mx = jnp.maximum(jnp.max(lg, axis=-1, keepdims=True), snk[None])
    p = jnp.exp(lg - mx)
    den = jnp.sum(p, axis=-1, keepdims=True) + jnp.exp(snk[None] - mx)
    pv = lax.dot_general(p.astype(BF16), vall, (((2,), (1,)), ((0,), (0,))),
                         preferred_element_type=F32) / den
    head_out = [pv[:, h * t:(h + 1) * t, :].reshape(rows, LANES) for h in range(B_HEADS)]
    yb = jnp.concatenate(
        [_merge_head_pair(head_out[2 * j], head_out[2 * j + 1], 2 * j, rows)
         for j in range(B_HEADS // 2)], axis=1)
    y_ref[:, A_WIDTH:A_WIDTH + B_WIDTH] = (yb * sgb).astype(BF16)
    ko_ref[...] = jnp.concatenate([kcache[:, t:, :], kn3], axis=1)
    vo_ref[...] = jnp.concatenate([vcache[:, t:, :], vv3], axis=1)

    xbuf[:, SUBLANES - (C_CONV - 1):SUBLANES, :] = cs_ref[...]
    xbuf[:, SUBLANES:2 * SUBLANES, :] = zc_ref[:, 0:2 * C_WIDTH].reshape(nb, t, 2 * C_WIDTH)
    y3 = cb_ref[...][None]
    for j in range(C_CONV):
        lo = SUBLANES - (C_CONV - 1) + j
        y3 = y3 + cw_ref[j:j + 1, :][None] * xbuf[:, lo:lo + t, :]
    convo_ref[...] = xbuf[:, 2 * SUBLANES - (C_CONV - 1):2 * SUBLANES, :]
    qk = _silu(y3.reshape(rows, 2 * C_WIDTH))
    qall = qk[:, 0:C_WIDTH].astype(BF16)
    kall_c = qk[:, C_WIDTH:2 * C_WIDTH] * (C_HEAD_DIM ** -0.5)
    vall_c = zc_ref[:, 2 * C_WIDTH:3 * C_WIDTH].astype(BF16)
    gate_o = _sigmoid(zc_ref[:, 3 * C_WIDTH:4 * C_WIDTH]) * _silu(zc_ref[:, 4 * C_WIDTH:5 * C_WIDTH])
    ifp = zc_ref[:, 5 * C_WIDTH:5 * C_WIDTH + LANES]
    lf = _log_sigmoid(ifp + fb_ref[...])
    lane_t = lax.broadcasted_iota(jnp.int32, (rows, LANES), 1)
    cum_all = _dot_exact01(jnp.where(causal_b, 1.0, 0.0).astype(BF16), lf)
    tot_all = _dot_exact01(jnp.where(same_b, 1.0, 0.0).astype(BF16), lf)
    st_col = jnp.where(lane_t < C_HEADS, ifp, cum_all)
    st_row = st_col.T
    tot_row = tot_all.T
    m0 = m0_ref[...]
    same_b_bf = jnp.where(same_b, 1.0, 0.0).astype(BF16)
    batch_of_lane = lax.broadcasted_iota(jnp.int32, (nb, 1, rows), 2) // t
    batch_id = lax.broadcasted_iota(jnp.int32, (nb, 1, rows), 0)
    own_tok = batch_of_lane == batch_id
    h_cols = []
    m_out = jnp.zeros((rows, LANES), F32)
    for hd in range(C_HEADS):
        hs = slice(hd * C_HEAD_DIM, (hd + 1) * C_HEAD_DIM)
        i_c = st_col[:, hd:hd + 1]
        cum_c = st_col[:, C_HEADS + hd:C_HEADS + hd + 1]
        tot_c = tot_all[:, C_HEADS + hd:C_HEADS + hd + 1]
        i_r = st_row[hd:hd + 1, :]
        cum_r = st_row[C_HEADS + hd:C_HEADS + hd + 1, :]
        tot_r = tot_row[C_HEADS + hd:C_HEADS + hd + 1, :]
        m_prev = m0[:, hd:hd + 1]
        dmat = jnp.where(causal_b, cum_c - cum_r + i_r, NEG)
        m_inter = cum_c + m_prev
        m_t = jnp.maximum(m_inter, jnp.max(dmat, axis=-1, keepdims=True))
        q_h = qall[:, hs]
        k_h = kall_c[:, hs]
        v_h = vall_c[:, hs]
        a = jnp.exp(dmat - m_t) * _dot_nt(q_h, k_h.astype(BF16))
        w_inter = jnp.exp(m_inter - m_t)
        c_prev = c0_ref[:, hd]
        n_tok = jnp.broadcast_to(n0_ref[hd][:, None, :], (nb, t, C_HEAD_DIM)).reshape(rows, C_HEAD_DIM)
        inter = lax.dot_general(q_h.reshape(nb, t, C_HEAD_DIM), c_prev.astype(BF16),
                                (((2,), (1,)), ((0,), (0,))), preferred_element_type=F32)
        num = _dot(a.astype(BF16), v_h) + w_inter * inter.reshape(rows, C_HEAD_DIM)
        den = (jnp.sum(a, axis=-1, keepdims=True)
               + w_inter * jnp.sum(q_h.astype(F32) * n_tok, axis=-1, keepdims=True))
        hh = num / jnp.maximum(jnp.abs(den), jnp.exp(-m_t))
        h_cols.append(_rms(hh))
        g_r = tot_r - cum_r + i_r
        g_c = tot_c - cum_c + i_c
        m_new = jnp.maximum(tot_c + m_prev,
                            jnp.max(jnp.where(same_b, g_r, NEG), axis=-1, keepdims=True))
        kw = jnp.exp(g_c - m_new) * k_h
        decay = jnp.exp(tot_c + m_prev - m_new)
        kwt = kw.T
        lhs = jnp.where(own_tok, kwt[None], 0.0).astype(BF16).reshape(nb * C_HEAD_DIM, rows)
        upd = _dot(lhs, v_h).reshape(nb, C_HEAD_DIM, C_HEAD_DIM)
        dec_b = jnp.broadcast_to(decay, (rows, C_HEAD_DIM)).reshape(nb, t, C_HEAD_DIM)[:, 0:1, :]
        c1_ref[:, hd] = dec_b * c_prev + upd
        n1_ref[hd] = decay * n_tok + _dot(same_b_bf, kw.astype(BF16))
        m_out = jnp.where(lane_t == hd, m_new, m_out)
    m1_ref[...] = m_out
    hn = jnp.concatenate(h_cols, axis=1) * hg_ref[...]
    y_ref[:, A_WIDTH + B_WIDTH:Y_W] = (hn * gate_o).astype(BF16)


def _sample_mix_call(l, za, zb, zc, kc, vc, cs, c0, n0t, m0tok, lw, nbatch):
    nb = SAMPLE_NB
    t = SUBLANES
    rows = nb * t
    tok = lambda i: (i, 0)
    const2 = lambda i: (0, 0)
    const3 = lambda i: (0, 0, 0)
    b3 = lambda i: (i, 0, 0)
    lb4 = lambda i: (l, i, 0, 0)
    return pl.pallas_call(
        _sample_mix_kernel,
        grid=(nbatch // nb,),
        in_specs=[
            pl.BlockSpec(memory_space=pltpu.SMEM),
            pl.BlockSpec((rows, ZA_W), tok),
            pl.BlockSpec((rows, ZB_W), tok),
            pl.BlockSpec((rows, ZC_W), tok),
            pl.BlockSpec((None, nb, WINDOW, B_KV_WIDTH), lb4),
            pl.BlockSpec((None, nb, WINDOW, B_KV_WIDTH), lb4),
            pl.BlockSpec((None, nb, C_CONV - 1, 2 * C_WIDTH), lb4),
            pl.BlockSpec((None, nb, C_HEADS, C_HEAD_DIM, C_HEAD_DIM), lambda i: (l, i, 0, 0, 0)),
            pl.BlockSpec((None, C_HEADS, nb, C_HEAD_DIM), lambda i: (l, 0, i, 0)),
            pl.BlockSpec((None, rows, LANES), lambda i: (l, i, 0)),
            pl.BlockSpec((1, A_WIDTH), const2),
            pl.BlockSpec((A_GROUPS, rows, rows), const3),
            pl.BlockSpec((rows, LANES), const2),
            pl.BlockSpec((1, B_WIDTH), const2),
            pl.BlockSpec((1, B_KV_WIDTH), const2),
            pl.BlockSpec((C_CONV, 2 * C_WIDTH), const2),
            pl.BlockSpec((1, 2 * C_WIDTH), const2),
            pl.BlockSpec((1, LANES), const2),
            pl.BlockSpec((1, C_WIDTH), const2),
        ],
        out_specs=[
            pl.BlockSpec((rows, Y_W), tok),
            pl.BlockSpec((rows, A_WIDTH), tok),
            pl.BlockSpec((nb, WINDOW, B_KV_WIDTH), b3),
            pl.BlockSpec((nb, WINDOW, B_KV_WIDTH), b3),
            pl.BlockSpec((nb, C_CONV - 1, 2 * C_WIDTH), b3),
            pl.BlockSpec((nb, C_HEADS, C_HEAD_DIM, C_HEAD_DIM), lambda i: (i, 0, 0, 0)),
            pl.BlockSpec((C_HEADS, rows, C_HEAD_DIM), lambda i: (0, i, 0)),
            pl.BlockSpec((rows, LANES), tok),
        ],
        out_shape=[
            jax.ShapeDtypeStruct((nbatch * t, Y_W), BF16),
            jax.ShapeDtypeStruct((nbatch * t, A_WIDTH), F32),
            jax.ShapeDtypeStruct((nbatch, WINDOW, B_KV_WIDTH), F32),
            jax.ShapeDtypeStruct((nbatch, WINDOW, B_KV_WIDTH), F32),
            jax.ShapeDtypeStruct((nbatch, C_CONV - 1, 2 * C_WIDTH), F32),
            jax.ShapeDtypeStruct((nbatch, C_HEADS, C_HEAD_DIM, C_HEAD_DIM), F32),
            jax.ShapeDtypeStruct((C_HEADS, nbatch * t, C_HEAD_DIM), F32),
            jax.ShapeDtypeStruct((nbatch * t, LANES), F32),
        ],
        scratch_shapes=[pltpu.VMEM((nb, 2 * SUBLANES, 2 * C_WIDTH), F32)],
        compiler_params=pltpu.CompilerParams(
            dimension_semantics=("arbitrary",), vmem_limit_bytes=VMEM_LIMIT),
        name="sample_mixer",
    )(lw["sinks"], za, zb, zc, kc, vc, cs, c0, n0t, m0tok, lw["vg"], lw["gwb"], lw["gbs_tok"],
      lw["qg"], lw["kg"], lw["cw"], lw["cb"], lw["fb"], lw["hg"])


def _layer_weights(l, wcat_all, wmg_all, b_in, gmlp_vnorm_g, gmlp_ws, gmlp_bs, swa_qnorm_g,
                   swa_knorm_g, swa_sinks, mlstm_conv_w, mlstm_conv_b, mlstm_f_bias, mlstm_hnorm_g,
                   w_branch_a, w_branch_b, w_branch_c, w_out, norm_g, dec_seq):
    bl = b_in[l]
    bcat = jnp.concatenate([bl[:COL_CI], bl[COL_CO:COL_MG], bl[COL_CI:COL_CO],
                            jnp.zeros((LANES - 2 * C_HEADS,), F32)])
    t = dec_seq
    nb = SAMPLE_NB
    ws_t = gmlp_ws[l][:, :t, :t] * jnp.tril(jnp.ones((t, t), F32))
    eye = jnp.eye(nb, dtype=F32)
    gwb = jnp.einsum("bc,gts->gbtcs", eye, ws_t).reshape(A_GROUPS, nb * t, nb * t).astype(BF16)
    gbs_col = jnp.pad(gmlp_bs[l].T, ((0, 0), (0, LANES - A_GROUPS)))
    gbs_tok = jnp.pad(jnp.tile(gmlp_bs[l][:, :t].T, (nb, 1)), ((0, 0), (0, LANES - A_GROUPS)))
    fb = jnp.pad(mlstm_f_bias[l], (C_HEADS, LANES - 2 * C_HEADS)).reshape(1, LANES)
    return dict(
        ng=norm_g[l].reshape(1, D_MODEL),
        wcat=wcat_all, bcat=bcat.reshape(1, ZCAT_W),
        wmg=wmg_all, bmg=bl[COL_MG:].reshape(1, 3 * D_MODEL),
        wa=w_branch_a[l].astype(BF16), wb=w_branch_b[l].astype(BF16),
        wc=w_branch_c[l].astype(BF16), wo=w_out[l].astype(BF16),
        vg=gmlp_vnorm_g[l].reshape(1, A_WIDTH), gws=gmlp_ws[l], gwb=gwb,
        gbs_col=gbs_col, gbs_tok=gbs_tok,
        qg=jnp.tile(swa_qnorm_g[l], B_HEADS).reshape(1, B_WIDTH),
        kg=jnp.tile(swa_knorm_g[l], B_KV_HEADS).reshape(1, B_KV_WIDTH),
        sinks=swa_sinks[l],
        cw=mlstm_conv_w[l], cb=mlstm_conv_b[l].reshape(1, 2 * C_WIDTH), fb=fb,
        hg=mlstm_hnorm_g[l].reshape(1, C_WIDTH),
    )


def kernel(x_prompt, x_sample, cache_swa_k, cache_swa_v, state_mlstm_conv, state_mlstm_C, state_mlstm_n, state_mlstm_m, c_prompt, c_sample, ada_w, ada_b, norm_g, w_in, b_in, gmlp_vnorm_g, gmlp_ws, gmlp_bs, swa_qnorm_g, swa_knorm_g, swa_sinks, mlstm_conv_w, mlstm_conv_b, mlstm_f_bias, mlstm_hnorm_g, w_branch_a, w_branch_b, w_branch_c, w_out):
    batch, seq, _ = x_prompt.shape
    nbatch, dec_seq, _ = x_sample.shape
    assert dec_seq == SUBLANES and seq % PROMPT_TILE == 0 and nbatch % SAMPLE_NB == 0
    assert seq % PROJ_TILE == 0 and (nbatch * dec_seq) % PROJ_TILE == 0
    wb_len = cache_swa_k.shape[2]
    assert wb_len == WINDOW

    nc = batch + nbatch
    nc_pad = -(-nc // SUBLANES) * SUBLANES
    c_all = jnp.concatenate([c_prompt, c_sample, jnp.zeros((nc_pad - nc, D_MODEL), F32)], axis=0)
    mod_all = _ada_call(c_all, ada_w, ada_b)

    xp = x_prompt.reshape(batch * seq, D_MODEL)
    xs = x_sample.reshape(nbatch * dec_seq, D_MODEL)
    kc_all = cache_swa_k.reshape(DEPTH, nbatch, WINDOW, B_KV_WIDTH)
    vc_all = cache_swa_v.reshape(DEPTH, nbatch, WINDOW, B_KV_WIDTH)
    n0t_all = jnp.transpose(state_mlstm_n, (0, 2, 1, 3))
    m0tok_all = jnp.pad(jnp.repeat(state_mlstm_m, dec_seq, axis=1),
                        ((0, 0), (0, 0), (0, LANES - C_HEADS)))
    wcat_all, wmg_all = _weight_prep_call(w_in)
    outs_p = [[] for _ in range(6)]
    outs_s = [[] for _ in range(6)]
    vrows = []
    for l in range(DEPTH):
        lw = _layer_weights(l, wcat_all, wmg_all, b_in, gmlp_vnorm_g, gmlp_ws, gmlp_bs, swa_qnorm_g,
                            swa_knorm_g, swa_sinks, mlstm_conv_w, mlstm_conv_b, mlstm_f_bias,
                            mlstm_hnorm_g, w_branch_a, w_branch_b, w_branch_c, w_out, norm_g,
                            dec_seq)
        mod_p = mod_all[l, :batch].reshape(batch, 1, 3 * D_MODEL)
        mod_s = jnp.repeat(mod_all[l, batch:nc], dec_seq, axis=0)

        xp, ko, vo, convo, c1, n1, m1 = _prompt_layer_call(l, xp, mod_p, lw, batch, seq)
        outs_p[0].append(ko.reshape(batch, WINDOW, B_KV_HEADS, B_HEAD_DIM))
        outs_p[1].append(vo.reshape(batch, WINDOW, B_KV_HEADS, B_HEAD_DIM))
        outs_p[2].append(convo[:, SUBLANES - (C_CONV - 1):, :])
        outs_p[3].append(c1)
        outs_p[4].append(n1)
        outs_p[5].append(m1[:, 0, :C_HEADS])

        za, zb, zc = _inproj_call(l, xs, mod_s, lw["ng"], lw["wcat"], lw["bcat"], None)
        y, vrow, ko, vo, convo, c1, n1tok, m1tok = _sample_mix_call(
            l, za, zb, zc, kc_all, vc_all, state_mlstm_conv, state_mlstm_C, n0t_all, m0tok_all,
            lw, nbatch)
        xs = _outproj_call(l, xs, mod_s, lw["ng"], y, lw["wmg"], lw["bmg"], lw["wa"], lw["wb"],
                           lw["wc"], lw["wo"], None)
        outs_s[0].append(ko.reshape(nbatch, WINDOW, B_KV_HEADS, B_HEAD_DIM))
        outs_s[1].append(vo.reshape(nbatch, WINDOW, B_KV_HEADS, B_HEAD_DIM))
        outs_s[2].append(convo)
        outs_s[3].append(c1)
        outs_s[4].append(jnp.transpose(n1tok[:, ::dec_seq, :], (1, 0, 2)))
        outs_s[5].append(m1tok[::dec_seq, :C_HEADS])
        vrows.append(vrow.reshape(nbatch, dec_seq, A_WIDTH))

    sp = [jnp.stack(o) for o in outs_p]
    ss = [jnp.stack(o) for o in outs_s]
    return (xp.reshape(batch, seq, D_MODEL), xs.reshape(nbatch, dec_seq, D_MODEL),
            sp[0], sp[1], sp[2], sp[3], sp[4], sp[5],
            ss[0], ss[1], ss[2], ss[3], ss[4], ss[5], jnp.stack(vrows))
```

```python
import functools

import numpy as np
import jax
import jax.numpy as jnp
from jax import lax
from jax.experimental import pallas as pl
from jax.experimental.pallas import tpu as pltpu

F32 = jnp.float32
BF16 = jnp.bfloat16

D_MODEL = 1024
DEPTH = 2
A_WIDTH = 512
A_GROUPS = 4
GROUP_DIM = 128
B_HEADS = 8
B_KV_HEADS = 2
B_HEAD_DIM = 64
B_WIDTH = 512
B_KV_WIDTH = 128
WINDOW = 128
C_HEADS = 4
C_HEAD_DIM = 128
C_WIDTH = 512
C_CONV = 4
EPS = 1e-6
NEG = -1e30

LANES = 128
SUBLANES = 8
VMEM_LIMIT = 56 * 1024 * 1024

ZA_W = 3 * A_WIDTH
ZB_W = 2 * B_WIDTH + 2 * B_KV_WIDTH
ZC_W = 2 * C_WIDTH + 3 * C_WIDTH + LANES
ZCAT_W = ZA_W + ZB_W + ZC_W
Y_W = A_WIDTH + B_WIDTH + C_WIDTH

PROMPT_TILE = 256
MLSTM_CHUNK = PROMPT_TILE
SAMPLE_NB = 16
PROJ_TILE = 512


def _sigmoid(x):
    return 0.5 * jnp.tanh(0.5 * x) + 0.5


def _silu(x):
    t = 0.5 * x
    return t * (jnp.tanh(t) + 1.0)


def _log_sigmoid(x):
    return jnp.minimum(x, 0.0) - jnp.log1p(jnp.exp(-jnp.abs(x)))


def _rms(x):
    return x * lax.rsqrt(jnp.mean(x * x, axis=-1, keepdims=True) + EPS)


def _dot(a, b):
    return jnp.dot(a, b, preferred_element_type=F32)


def _dot_nt(a, b):
    return lax.dot_general(a, b, (((1,), (1,)), ((), ())), preferred_element_type=F32)


def _dot_exact01(m01, x):
    hi = x.astype(BF16)
    r1 = x - hi.astype(F32)
    mid = r1.astype(BF16)
    lo = (r1 - mid.astype(F32)).astype(BF16)
    return _dot(m01, hi) + _dot(m01, mid) + _dot(m01, lo)


def _modulated_norm(x, mod_ref, ng_ref):
    xn = _rms(x) * ng_ref[...]
    shift = mod_ref[:, 0:D_MODEL]
    scale = mod_ref[:, D_MODEL:2 * D_MODEL]
    return (xn * (1.0 + scale) + shift).astype(BF16)


def _head_rms_scale(x2, lane_lo):
    s0 = jnp.sum(jnp.where(lane_lo, x2, 0.0), axis=-1, keepdims=True)
    s1 = jnp.sum(jnp.where(lane_lo, 0.0, x2), axis=-1, keepdims=True)
    r0 = lax.rsqrt(s0 * (1.0 / B_HEAD_DIM) + EPS)
    r1 = lax.rsqrt(s1 * (1.0 / B_HEAD_DIM) + EPS)
    return jnp.where(lane_lo, r0, r1)


def _qk_norm(x, g_row):
    rows, width = x.shape
    lane_lo = lax.broadcasted_iota(jnp.int32, (rows, LANES), 1) < B_HEAD_DIM
    outs = []
    for j in range(width // LANES):
        slab = x[:, j * LANES:(j + 1) * LANES]
        outs.append(slab * _head_rms_scale(slab * slab, lane_lo))
    y = outs[0] if len(outs) == 1 else jnp.concatenate(outs, axis=1)
    return y * g_row


def _ada_kernel(c_ref, w_ref, b_ref, o_ref):
    c = c_ref[...]
    o_ref[...] = _dot(_silu(c).astype(BF16), w_ref[...].astype(BF16)) + b_ref[...]


def _ada_call(c_all, ada_w, ada_b):
    rows = c_all.shape[0]
    return pl.pallas_call(
        _ada_kernel,
        grid=(DEPTH, 3),
        in_specs=[
            pl.BlockSpec((rows, D_MODEL), lambda l, j: (0, 0)),
            pl.BlockSpec((None, D_MODEL, D_MODEL), lambda l, j: (l, 0, j)),
            pl.BlockSpec((None, 1, D_MODEL), lambda l, j: (l, 0, j)),
        ],
        out_specs=pl.BlockSpec((None, rows, D_MODEL), lambda l, j: (l, 0, j)),
        out_shape=jax.ShapeDtypeStruct((DEPTH, rows, 3 * D_MODEL), F32),
        compiler_params=pltpu.CompilerParams(
            dimension_semantics=("arbitrary", "arbitrary"), vmem_limit_bytes=VMEM_LIMIT),
        name="adaln_mod",
    )(c_all, ada_w, ada_b.reshape(DEPTH, 1, 3 * D_MODEL))


COL_CI = ZA_W + ZB_W + 3 * C_WIDTH
COL_CO = COL_CI + 2 * C_HEADS
COL_MG = COL_CO + 2 * C_WIDTH
PREP_CHUNK = 256
PREP_SHIFT = 2 * C_HEADS
N_MAIN = COL_CI // PREP_CHUNK
N_CO = (2 * C_WIDTH) // PREP_CHUNK
N_MG = (3 * D_MODEL) // PREP_CHUNK
J_CIF = N_MAIN + N_CO
J_MG = J_CIF + 1


def _weight_prep_kernel(wa_ref, wb_ref, wcat_ref, wmg_ref):
    j = pl.program_id(1)

    def shifted_t():
        rows = jnp.concatenate([wa_ref[PREP_SHIFT:PREP_CHUNK, :], wb_ref[...]], axis=0)
        return rows.astype(BF16).T

    @pl.when(j < N_MAIN)
    def _():
        wcat_ref[...] = wa_ref[...].astype(BF16).T

    @pl.when((j >= N_MAIN) & (j < J_CIF))
    def _():
        wcat_ref[...] = shifted_t()

    @pl.when(j == J_CIF)
    def _():
        row = lax.broadcasted_iota(jnp.int32, (PREP_CHUNK, D_MODEL), 0)
        wcat_ref[...] = jnp.where(row < PREP_SHIFT, wa_ref[...], 0.0).astype(BF16).T

    @pl.when(j >= J_MG)
    def _():
        wmg_ref[...] = shifted_t()


def _weight_prep_call(w_in):
    in_width = w_in.shape[-1]
    assert in_width == COL_MG + 3 * D_MODEL
    assert COL_CI % PREP_CHUNK == 0 and COL_CO % PREP_CHUNK == PREP_SHIFT == COL_MG % PREP_CHUNK
    w_t = jnp.swapaxes(w_in, 1, 2)
    assert in_width % PREP_SHIFT == 0 and PREP_SHIFT == SUBLANES
    last_rows = in_width // PREP_SHIFT - 1
    groups_per_chunk = PREP_CHUNK // PREP_SHIFT

    def src_block(j):
        return jnp.where(j < J_CIF, j, jnp.where(j == J_CIF, N_MAIN, j - 1))

    return pl.pallas_call(
        _weight_prep_kernel,
        grid=(DEPTH, J_MG + N_MG),
        in_specs=[
            pl.BlockSpec((None, PREP_CHUNK, D_MODEL), lambda l, j: (l, src_block(j), 0)),
            pl.BlockSpec((None, PREP_SHIFT, D_MODEL),
                         lambda l, j: (l, jnp.minimum((src_block(j) + 1) * groups_per_chunk,
                                                      last_rows), 0)),
        ],
        out_specs=[
            pl.BlockSpec((None, D_MODEL, PREP_CHUNK), lambda l, j: (l, 0, jnp.minimum(j, J_CIF))),
            pl.BlockSpec((None, D_MODEL, PREP_CHUNK), lambda l, j: (l, 0, jnp.maximum(j - J_MG, 0))),
        ],
        out_shape=[
            jax.ShapeDtypeStruct((DEPTH, D_MODEL, ZCAT_W), BF16),
            jax.ShapeDtypeStruct((DEPTH, D_MODEL, 3 * D_MODEL), BF16),
        ],
        compiler_params=pltpu.CompilerParams(
            dimension_semantics=("arbitrary", "arbitrary"), vmem_limit_bytes=VMEM_LIMIT),
        name="weight_prep",
    )(w_t, w_t)


def _col_chunks(width, step):
    return [(o, min(step, width - o)) for o in range(0, width, step)]


def _inproj_pieces(get_h, w_ref, b_ref, za_ref, zb_ref, zc_ref, step):
    def piece(o_ref, off, woff, w):
        def run():
            o_ref[:, off:off + w] = _dot(get_h(), w_ref[:, woff:woff + w]) + b_ref[:, woff:woff + w]
        return run
    pieces = []
    base = 0
    for o_ref, width in ((za_ref, ZA_W), (zb_ref, ZB_W), (zc_ref, ZC_W)):
        pieces += [piece(o_ref, off, base + off, w) for off, w in _col_chunks(width, step)]
        base += width
    return pieces


def _inproj_kernel(x_ref, mod_ref, ng_ref, w_ref, b_ref, za_ref, zb_ref, zc_ref):
    h = _modulated_norm(x_ref[...], mod_ref, ng_ref)
    for piece in _inproj_pieces(lambda: h, w_ref, b_ref, za_ref, zb_ref, zc_ref, 512):
        piece()


def _mod_spec(tm, tokens_per_batch):
    if tokens_per_batch is None:
        return pl.BlockSpec((tm, 3 * D_MODEL), lambda i: (i, 0))
    tiles_per_batch = tokens_per_batch // tm
    return pl.BlockSpec((None, 1, 3 * D_MODEL), lambda i: (i // tiles_per_batch, 0, 0))


def _layer_weight_spec(layer, rows, cols):
    return pl.BlockSpec((None, rows, cols), lambda i: (layer, 0, 0), pipeline_mode=pl.Buffered(1))


def _inproj_call(layer, x2, mod, ng, wcat, bcat, tokens_per_batch):
    ntok = x2.shape[0]
    tm = PROJ_TILE
    const = lambda i: (0, 0)
    return pl.pallas_call(
        _inproj_kernel,
        grid=(ntok // tm,),
        in_specs=[
            pl.BlockSpec((tm, D_MODEL), lambda i: (i, 0)),
            _mod_spec(tm, tokens_per_batch),
            pl.BlockSpec((1, D_MODEL), const),
            _layer_weight_spec(layer, D_MODEL, ZCAT_W),
            pl.BlockSpec((1, ZCAT_W), const),
        ],
        out_specs=[
            pl.BlockSpec((tm, ZA_W), lambda i: (i, 0)),
            pl.BlockSpec((tm, ZB_W), lambda i: (i, 0)),
            pl.BlockSpec((tm, ZC_W), lambda i: (i, 0)),
        ],
        out_shape=[
            jax.ShapeDtypeStruct((ntok, ZA_W), F32),
            jax.ShapeDtypeStruct((ntok, ZB_W), F32),
            jax.ShapeDtypeStruct((ntok, ZC_W), F32),
        ],
        compiler_params=pltpu.CompilerParams(
            dimension_semantics=("arbitrary",), vmem_limit_bytes=VMEM_LIMIT),
        name="in_projection",
    )(x2, mod, ng, wcat, bcat)


def _outproj_kernel(x_ref, mod_ref, ng_ref, y_ref, wmg_ref, bmg_ref, wa_ref, wb_ref, wc_ref,
                    wo_ref, o_ref):
    x = x_ref[...]
    h = _modulated_norm(x, mod_ref, ng_ref)
    merged = None
    for i, wbr_ref in enumerate((wa_ref, wb_ref, wc_ref)):
        cols = slice(i * D_MODEL, (i + 1) * D_MODEL)
        gate = _sigmoid(_dot(h, wmg_ref[:, cols]) + bmg_ref[:, cols])
        term = gate * _dot(y_ref[:, i * A_WIDTH:(i + 1) * A_WIDTH], wbr_ref[...])
        merged = term if merged is None else merged + term
    ada_gate = mod_ref[:, 2 * D_MODEL:3 * D_MODEL]
    o_ref[...] = x + ada_gate * _dot(merged.astype(BF16), wo_ref[...])


def _outproj_call(layer, x2, mod, ng, y, wmg, bmg, wa, wb, wc, wo, tokens_per_batch):
    ntok = x2.shape[0]
    tm = PROJ_TILE
    const = lambda i: (0, 0)
    once = pl.Buffered(1)
    return pl.pallas_call(
        _outproj_kernel,
        grid=(ntok // tm,),
        in_specs=[
            pl.BlockSpec((tm, D_MODEL), lambda i: (i, 0)),
            _mod_spec(tm, tokens_per_batch),
            pl.BlockSpec((1, D_MODEL), const),
            pl.BlockSpec((tm, Y_W), lambda i: (i, 0)),
            _layer_weight_spec(layer, D_MODEL, 3 * D_MODEL),
            pl.BlockSpec((1, 3 * D_MODEL), const),
            pl.BlockSpec((A_WIDTH, D_MODEL), const, pipeline_mode=once),
            pl.BlockSpec((B_WIDTH, D_MODEL), const, pipeline_mode=once),
            pl.BlockSpec((C_WIDTH, D_MODEL), const, pipeline_mode=once),
            pl.BlockSpec((D_MODEL, D_MODEL), const, pipeline_mode=once),
        ],
        out_specs=pl.BlockSpec((tm, D_MODEL), lambda i: (i, 0)),
        out_shape=jax.ShapeDtypeStruct((ntok, D_MODEL), F32),
        compiler_params=pltpu.CompilerParams(
            dimension_semantics=("arbitrary",), vmem_limit_bytes=VMEM_LIMIT),
        name="out_projection",
    )(x2, mod, ng, y, wmg, bmg, wa, wb, wc, wo)


def _place_q_head(qn, h, rows):
    lane = lax.broadcasted_iota(jnp.int32, (rows, LANES), 1)
    slab = qn[:, (h // 2) * LANES:(h // 2 + 1) * LANES]
    src_hi = h % 2
    dst_hi = h // (B_HEADS // B_KV_HEADS)
    keep = (lane >= B_HEAD_DIM) if src_hi else (lane < B_HEAD_DIM)
    slab = jnp.where(keep, slab, 0.0)
    if src_hi != dst_hi:
        slab = pltpu.roll(slab, B_HEAD_DIM, 1)
    return slab


def _merge_head_pair(o_even, o_odd, h_even, rows):
    lane_lo = lax.broadcasted_iota(jnp.int32, (rows, LANES), 1) < B_HEAD_DIM
    kv_hi = h_even // (B_HEADS // B_KV_HEADS)
    if kv_hi:
        o_even = pltpu.roll(o_even, B_HEAD_DIM, 1)
    else:
        o_odd = pltpu.roll(o_odd, B_HEAD_DIM, 1)
    return jnp.where(lane_lo, o_even, o_odd)


def _conv_taps(xbuf, cw_ref, cb_ref, cols, ts):
    y = cb_ref[:, cols]
    for j in range(C_CONV):
        lo = SUBLANES - (C_CONV - 1) + j
        y = y + cw_ref[j:j + 1, cols] * xbuf[lo:lo + ts, cols]
    return y


def _prompt_mix_kernel(sink_ref, za_ref, zb_ref, zc_ref, vg_ref, gw_ref, gbs_ref, qg_ref, kg_ref,
                       cw_ref, cb_ref, fb_ref, hg_ref, tril_ref, band_ref, tri01_ref, tribias_ref,
                       y_ref, ko_ref, vo_ref, convo_ref, c_ref, n_ref, m_ref,
                       kprev, vprev, xbuf, first_tile, pump):
    ts = PROMPT_TILE

    u = za_ref[:, 0:A_WIDTH]
    vn = _rms(za_ref[:, A_WIDTH:2 * A_WIDTH]) * vg_ref[...]
    sg = _silu(za_ref[:, 2 * A_WIDTH:3 * A_WIDTH])
    vnb = vn.astype(BF16)
    wts = [(gw_ref[gi] * tril_ref[...]).astype(BF16) for gi in range(A_GROUPS)]
    s_rows = []
    for c in range(ts // WINDOW):
        s_cols = []
        for gi in range(A_GROUPS):
            vblk = vnb[c * WINDOW:(c + 1) * WINDOW, gi * GROUP_DIM:(gi + 1) * GROUP_DIM]
            s_cols.append(_dot(wts[gi], vblk) + gbs_ref[:, gi:gi + 1])
        s_rows.append(jnp.concatenate(s_cols, axis=1))
    s = jnp.concatenate(s_rows, axis=0)
    pump()
    y_ref[:, 0:A_WIDTH] = (u * s * sg).astype(BF16)
    pump()

    qn = _qk_norm(zb_ref[:, 0:B_WIDTH], qg_ref[...]) * (B_HEAD_DIM ** -0.5)
    pump()
    kn = _qk_norm(zb_ref[:, B_WIDTH:B_WIDTH + B_KV_WIDTH], kg_ref[...])
    vv = zb_ref[:, B_WIDTH + B_KV_WIDTH:B_WIDTH + 2 * B_KV_WIDTH]
    pump()
    grp = B_HEADS // B_KV_HEADS
    nblk = ts // WINDOW
    lane_lo2 = lax.broadcasted_iota(jnp.int32, (2 * WINDOW, LANES), 1) < B_HEAD_DIM
    kblocks = [kprev[...]] + [kn[b * WINDOW:(b + 1) * WINDOW] for b in range(nblk)]
    vblocks = [vprev[...]] + [vv[b * WINDOW:(b + 1) * WINDOW] for b in range(nblk)]
    heads = [(kh, g) for kh in range(B_KV_HEADS) for g in range(grp)]
    snk = {k: sink_ref[k[0] * grp + k[1]] for k in heads}
    for blk in range(nblk):
        rows = slice(blk * WINDOW, (blk + 1) * WINDOW)
        if blk == 0 and first_tile is not False:
            bias = jnp.where(first_tile, band_ref[1], band_ref[0])
        else:
            bias = band_ref[0]
        kcat = jnp.concatenate([kblocks[blk], kblocks[blk + 1]], axis=0)
        vcat = jnp.concatenate([vblocks[blk], vblocks[blk + 1]], axis=0)
        krol = pltpu.roll(kcat, B_HEAD_DIM, 1)
        vrol = pltpu.roll(vcat, B_HEAD_DIM, 1)
        kdup, vdup = [], []
        for kh in range(B_KV_HEADS):
            own = lane_lo2 if kh == 0 else jnp.logical_not(lane_lo2)
            kdup.append(jnp.where(own, kcat, krol).astype(BF16))
            vdup.append(jnp.where(own, vcat, vrol).astype(BF16))
        pump()
        qs = [jnp.concatenate([_place_q_head(qn[rows], kh * grp + g, WINDOW) for g in range(grp)],
                              axis=0).astype(BF16) for kh in range(B_KV_HEADS)]
        logits = [_dot_nt(qs[kh], kdup[kh]) for kh in range(B_KV_HEADS)]
        pump()
        lg = {(kh, g): logits[kh][g * WINDOW:(g + 1) * WINDOW] + bias for kh, g in heads}
        mx = {k: jnp.maximum(jnp.max(lg[k], axis=-1, keepdims=True), snk[k]) for k in heads}
        pump()
        p = {k: jnp.exp(lg[k] - mx[k]) for k in heads}
        pump()
        rden = {k: 1.0 / (jnp.sum(p[k], axis=-1, keepdims=True) + jnp.exp(snk[k] - mx[k]))
                for k in heads}
        pump()
        pv = [_dot(jnp.concatenate([p[kh, g].astype(BF16) for g in range(grp)], axis=0), vdup[kh])
              for kh in range(B_KV_HEADS)]
        pump()
        outs = {(kh, g): pv[kh][g * WINDOW:(g + 1) * WINDOW] * rden[kh, g] for kh, g in heads}
        yb = jnp.concatenate(
            [_merge_head_pair(outs[(2 * j) // grp, (2 * j) % grp],
                              outs[(2 * j + 1) // grp, (2 * j + 1) % grp], 2 * j, WINDOW)
             for j in range(B_HEADS // 2)], axis=1)
        sgb = _silu(zb_ref[rows, B_WIDTH + 2 * B_KV_WIDTH:ZB_W])
        y_ref[rows, A_WIDTH:A_WIDTH + B_WIDTH] = (yb * sgb).astype(BF16)
        pump()
    kprev[...] = kblocks[nblk]
    vprev[...] = vblocks[nblk]
    ko_ref[...] = kblocks[nblk]
    vo_ref[...] = vblocks[nblk]
    pump()

    xbuf[SUBLANES:SUBLANES + ts, :] = zc_ref[:, 0:2 * C_WIDTH]
    qk = _silu(_conv_taps(xbuf, cw_ref, cb_ref, slice(0, 2 * C_WIDTH), ts))
    pump()
    tail = xbuf[ts:ts + SUBLANES, :]
    xbuf[0:SUBLANES, :] = tail
    convo_ref[...] = tail
    qall = qk[:, 0:C_WIDTH].astype(BF16)
    kall = qk[:, C_WIDTH:2 * C_WIDTH] * (C_HEAD_DIM ** -0.5)
    pump()
    ifp = zc_ref[:, 5 * C_WIDTH:5 * C_WIDTH + LANES]
    lf = _log_sigmoid(ifp + fb_ref[...])
    pump()
    cl = MLSTM_CHUNK
    lane_c = lax.broadcasted_iota(jnp.int32, (cl, LANES), 1)
    lane_1 = lax.broadcasted_iota(jnp.int32, (1, LANES), 1)
    m_row = m_ref[...]
    m_out = m_row
    cum_all = _dot_exact01(tri01_ref[...], lf)
    st_col = jnp.where(lane_c < C_HEADS, ifp, cum_all)
    st_row = st_col.T
    tribias = tribias_ref[...]
    pump()
    for hds in MLSTM_HEAD_GROUPS:
        hs = {hd: slice(hd * C_HEAD_DIM, (hd + 1) * C_HEAD_DIM) for hd in hds}
        i_c = {hd: st_col[:, hd:hd + 1] for hd in hds}
        cum_c = {hd: st_col[:, C_HEADS + hd:C_HEADS + hd + 1] for hd in hds}
        i_r = {hd: st_row[hd:hd + 1, :] for hd in hds}
        cum_r = {hd: st_row[C_HEADS + hd:C_HEADS + hd + 1, :] for hd in hds}
        m_prev = {hd: m_row[:, hd:hd + 1] for hd in hds}
        dmat = {hd: cum_c[hd] - cum_r[hd] + i_r[hd] + tribias for hd in hds}
        m_inter = {hd: cum_c[hd] + m_prev[hd] for hd in hds}
        m_t = {hd: jnp.maximum(m_inter[hd], jnp.max(dmat[hd], axis=-1, keepdims=True)) for hd in hds}
        pump()
        q_h = {hd: qall[:, hs[hd]] for hd in hds}
        k_h = {hd: kall[:, hs[hd]] for hd in hds}
        v_h = {hd: zc_ref[:, 2 * C_WIDTH + hd * C_HEAD_DIM:2 * C_WIDTH + (hd + 1) * C_HEAD_DIM].astype(BF16)
               for hd in hds}
        s_qk = {hd: _dot_nt(q_h[hd], k_h[hd].astype(BF16)) for hd in hds}
        a = {hd: jnp.exp(dmat[hd] - m_t[hd]) * s_qk[hd] for hd in hds}
        pump()
        w_inter = {hd: jnp.exp(m_inter[hd] - m_t[hd]) for hd in hds}
        c_prev = {hd: c_ref[hd] for hd in hds}
        n_prev = {hd: n_ref[hd:hd + 1, :] for hd in hds}
        inter = {hd: _dot(q_h[hd], c_prev[hd].astype(BF16)) for hd in hds}
        intra = {hd: _dot(a[hd].astype(BF16), v_h[hd]) for hd in hds}
        pump()
        den = {hd: jnp.sum(a[hd], axis=-1, keepdims=True)
               + w_inter[hd] * jnp.sum(q_h[hd].astype(F32) * n_prev[hd], axis=-1, keepdims=True)
               for hd in hds}
        rnorm = {hd: 1.0 / jnp.maximum(jnp.abs(den[hd]), jnp.exp(-m_t[hd])) for hd in hds}
        hh = {hd: (intra[hd] + w_inter[hd] * inter[hd]) * rnorm[hd] for hd in hds}
        pump()
        for hd in hds:
            o_cols = slice(3 * C_WIDTH + hd * C_HEAD_DIM, 3 * C_WIDTH + (hd + 1) * C_HEAD_DIM)
            g_cols = slice(4 * C_WIDTH + hd * C_HEAD_DIM, 4 * C_WIDTH + (hd + 1) * C_HEAD_DIM)
            gate_o = _sigmoid(zc_ref[:, o_cols]) * _silu(zc_ref[:, g_cols])
            y_cols = slice(A_WIDTH + B_WIDTH + hd * C_HEAD_DIM, A_WIDTH + B_WIDTH + (hd + 1) * C_HEAD_DIM)
            y_ref[:, y_cols] = (_rms(hh[hd]) * hg_ref[:, hs[hd]] * gate_o).astype(BF16)
        pump()
        total = {hd: cum_r[hd][:, cl - 1:cl] for hd in hds}
        g_r = {hd: total[hd] - cum_r[hd] + i_r[hd] for hd in hds}
        g_c = {hd: total[hd] - cum_c[hd] + i_c[hd] for hd in hds}
        m_new = {hd: jnp.maximum(total[hd] + m_prev[hd], jnp.max(g_r[hd], axis=-1, keepdims=True))
                 for hd in hds}
        kw = {hd: jnp.exp(g_c[hd] - m_new[hd]) * k_h[hd] for hd in hds}
        decay = {hd: jnp.exp(total[hd] + m_prev[hd] - m_new[hd]) for hd in hds}
        pump()
        upd = {hd: _dot(kw[hd].T.astype(BF16), v_h[hd]) for hd in hds}
        for hd in hds:
            c_ref[hd] = decay[hd] * c_prev[hd] + upd[hd]
            n_ref[hd:hd + 1, :] = decay[hd] * n_prev[hd] + jnp.sum(kw[hd], axis=0, keepdims=True)
            m_out = jnp.where(lane_1 == hd, m_new[hd], m_out)
        pump()
    m_ref[...] = m_out


def _prompt_mask_constants():
    r = np.arange(WINDOW)[:, None]
    c = np.arange(2 * WINDOW)[None, :]
    band = (c > r) & (c <= r + WINDOW)
    band_first = band & (c >= WINDOW)
    band_bias = np.where(np.stack([band, band_first]), 0.0, NEG).astype(np.float32)
    tril = (np.arange(WINDOW)[:, None] >= np.arange(WINDOW)[None, :]).astype(np.float32)
    tri = np.arange(MLSTM_CHUNK)[:, None] >= np.arange(MLSTM_CHUNK)[None, :]
    return (jnp.asarray(tril), jnp.asarray(band_bias), jnp.asarray(tri, dtype=BF16),
            jnp.asarray(np.where(tri, 0.0, NEG).astype(np.float32)))


N_MIX_PARAMS = 13
MLSTM_HEAD_GROUPS = ((0, 1), (2, 3))
MIX_PUMP_CALLS = 37
TAIL_FILL_PIECES = 8
MXU_PIECE_COLS = 256


class _Interleaver:
    def __init__(self, pieces, calls, hold_back=0):
        self._pieces = list(pieces)
        self._hold_back = hold_back
        self._spread = len(self._pieces) - hold_back
        self._emitted = 0
        self._calls = calls
        self._call = 0

    def __call__(self):
        self._call += 1
        target = (self._call * self._spread) // self._calls
        while self._emitted < target:
            self._pieces.pop(0)()
            self._emitted += 1

    def finish(self):
        assert self._call == self._calls and len(self._pieces) == self._hold_back, self._call
        return self._pieces


def _gate_pieces(h_ref, wmg_ref, bmg_ref, g_ref):
    def piece(off):
        cols = slice(off, off + MXU_PIECE_COLS)
        def run():
            g_ref[:, cols] = _sigmoid(_dot(h_ref[...], wmg_ref[:, cols]) + bmg_ref[:, cols])
        return run
    return [piece(off) for off in range(0, 3 * D_MODEL, MXU_PIECE_COLS)]


def _merge_and_project(x, mod_ref, g_ref, y_ref, wa_ref, wb_ref, wc_ref, wo_ref, fillers=()):
    fillers = list(fillers)
    per_stage = -(-len(fillers) // 4)
    merged = None
    for i, wbr_ref in enumerate((wa_ref, wb_ref, wc_ref)):
        for piece in fillers[i * per_stage:(i + 1) * per_stage]:
            piece()
        term = (g_ref[:, i * D_MODEL:(i + 1) * D_MODEL]
                * _dot(y_ref[:, i * A_WIDTH:(i + 1) * A_WIDTH], wbr_ref[...]))
        merged = term if merged is None else merged + term
    for piece in fillers[3 * per_stage:]:
        piece()
    ada_gate = mod_ref[:, 2 * D_MODEL:3 * D_MODEL]
    return x + ada_gate * _dot(merged.astype(BF16), wo_ref[...])


def _prompt_layer_kernel(tiles_per_seq, sink_ref, x2_ref, xn_ref, mod_ref, modn_ref, ng_ref,
                         wcat_ref, bcat_ref, *rest):
    mix_params = rest[:N_MIX_PARAMS]
    wmg_ref, bmg_ref, wa_ref, wb_ref, wc_ref, wo_ref = rest[N_MIX_PARAMS:N_MIX_PARAMS + 6]
    o_ref, ko_ref, vo_ref, convo_ref, c_ref, n_ref, m_ref = rest[N_MIX_PARAMS + 6:N_MIX_PARAMS + 13]
    (za0, zb0, zc0, za1, zb1, zc1, h0, h1, y_scr, g_scr, kprev, vprev, xbuf) = rest[N_MIX_PARAMS + 13:]
    ts = PROMPT_TILE
    z = ((za0, zb0, zc0), (za1, zb1, zc1))
    h = (h0, h1)
    k = pl.program_id(0)
    seq_start = (k % (tiles_per_seq // 2)) == 0

    @pl.when(k == 0)
    def _():
        h0[...] = _modulated_norm(x2_ref[0:ts, :], mod_ref, ng_ref)
        for piece in _inproj_pieces(lambda: h0[...], wcat_ref, bcat_ref, *z[0], 512):
            piece()

    @pl.when(seq_start)
    def _():
        kprev[...] = jnp.zeros_like(kprev)
        vprev[...] = jnp.zeros_like(vprev)
        xbuf[0:SUBLANES, :] = jnp.zeros((SUBLANES, 2 * C_WIDTH), F32)
        c_ref[...] = jnp.zeros_like(c_ref)
        n_ref[...] = jnp.zeros_like(n_ref)
        m_ref[...] = jnp.zeros_like(m_ref)

    for half in range(2):
        cur, nxt = half, 1 - half
        rows = slice(half * ts, (half + 1) * ts)
        if half == 0:
            h[nxt][...] = _modulated_norm(x2_ref[ts:2 * ts, :], mod_ref, ng_ref)
        else:
            h[nxt][...] = _modulated_norm(xn_ref[...], modn_ref, ng_ref)
        get_h_next = functools.partial(lambda r: r[...], h[nxt])
        hold = TAIL_FILL_PIECES if half == 1 else 0
        proj = _inproj_pieces(get_h_next, wcat_ref, bcat_ref, *z[nxt], MXU_PIECE_COLS)
        pump = _Interleaver(
            proj[:len(proj) - hold] + _gate_pieces(h[cur], wmg_ref, bmg_ref, g_scr)
            + proj[len(proj) - hold:], MIX_PUMP_CALLS, hold_back=hold)
        _prompt_mix_kernel(sink_ref, *z[cur], *mix_params,
                           y_scr, ko_ref, vo_ref, convo_ref, c_ref, n_ref, m_ref, kprev, vprev, xbuf,
                           first_tile=seq_start if half == 0 else False, pump=pump)
        o_ref[rows, :] = _merge_and_project(x2_ref[rows, :], mod_ref, g_scr, y_scr,
                                            wa_ref, wb_ref, wc_ref, wo_ref, fillers=pump.finish())


def _prompt_layer_call(layer, x2, mod, lw, batch, seq):
    ts = PROMPT_TILE
    nt = seq // ts
    assert nt % 2 == 0
    last_tile = batch * nt - 1
    const2 = lambda k: (0, 0)
    const3 = lambda k: (0, 0, 0)
    per_b3 = lambda k: ((2 * k) // nt, 0, 0)
    next_tile = lambda k: jnp.minimum(2 * k + 2, last_tile)
    once = pl.Buffered(1)
    return pl.pallas_call(
        functools.partial(_prompt_layer_kernel, nt),
        grid=(batch * nt // 2,),
        in_specs=[
            pl.BlockSpec(memory_space=pltpu.SMEM),
            pl.BlockSpec((2 * ts, D_MODEL), lambda k: (k, 0)),
            pl.BlockSpec((ts, D_MODEL), lambda k: (next_tile(k), 0)),
            pl.BlockSpec((None, 1, 3 * D_MODEL), per_b3),
            pl.BlockSpec((None, 1, 3 * D_MODEL), lambda k: (next_tile(k) // nt, 0, 0)),
            pl.BlockSpec((1, D_MODEL), const2),
            _layer_weight_spec(layer, D_MODEL, ZCAT_W),
            pl.BlockSpec((1, ZCAT_W), const2),
            pl.BlockSpec((1, A_WIDTH), const2),
            pl.BlockSpec((A_GROUPS, WINDOW, WINDOW), const3),
            pl.BlockSpec((WINDOW, LANES), const2),
            pl.BlockSpec((1, B_WIDTH), const2),
            pl.BlockSpec((1, B_KV_WIDTH), const2),
            pl.BlockSpec((C_CONV, 2 * C_WIDTH), const2),
            pl.BlockSpec((1, 2 * C_WIDTH), const2),
            pl.BlockSpec((1, LANES), const2),
            pl.BlockSpec((1, C_WIDTH), const2),
            pl.BlockSpec((WINDOW, WINDOW), const2),
            pl.BlockSpec((2, WINDOW, 2 * WINDOW), const3),
            pl.BlockSpec((MLSTM_CHUNK, MLSTM_CHUNK), const2),
            pl.BlockSpec((MLSTM_CHUNK, MLSTM_CHUNK), const2),
            _layer_weight_spec(layer, D_MODEL, 3 * D_MODEL),
            pl.BlockSpec((1, 3 * D_MODEL), const2),
            pl.BlockSpec((A_WIDTH, D_MODEL), const2, pipeline_mode=once),
            pl.BlockSpec((B_WIDTH, D_MODEL), const2, pipeline_mode=once),
            pl.BlockSpec((C_WIDTH, D_MODEL), const2, pipeline_mode=once),
            pl.BlockSpec((D_MODEL, D_MODEL), const2, pipeline_mode=once),
        ],
        out_specs=[
            pl.BlockSpec((2 * ts, D_MODEL), lambda k: (k, 0)),
            pl.BlockSpec((None, WINDOW, B_KV_WIDTH), per_b3),
            pl.BlockSpec((None, WINDOW, B_KV_WIDTH), per_b3),
            pl.BlockSpec((None, SUBLANES, 2 * C_WIDTH), per_b3),
            pl.BlockSpec((None, C_HEADS, C_HEAD_DIM, C_HEAD_DIM), lambda k: ((2 * k) // nt, 0, 0, 0)),
            pl.BlockSpec((None, C_HEADS, C_HEAD_DIM), per_b3),
            pl.BlockSpec((None, 1, LANES), per_b3),
        ],
        out_shape=[
            jax.ShapeDtypeStruct((batch * seq, D_MODEL), F32),
            jax.ShapeDtypeStruct((batch, WINDOW, B_KV_WIDTH), F32),
            jax.ShapeDtypeStruct((batch, WINDOW, B_KV_WIDTH), F32),
            jax.ShapeDtypeStruct((batch, SUBLANES, 2 * C_WIDTH), F32),
            jax.ShapeDtypeStruct((batch, C_HEADS, C_HEAD_DIM, C_HEAD_DIM), F32),
            jax.ShapeDtypeStruct((batch, C_HEADS, C_HEAD_DIM), F32),
            jax.ShapeDtypeStruct((batch, 1, LANES), F32),
        ],
        scratch_shapes=(
            [pltpu.VMEM((ts, w), F32) for w in (ZA_W, ZB_W, ZC_W)] * 2
            + [pltpu.VMEM((ts, D_MODEL), BF16)] * 2
            + [pltpu.VMEM((ts, Y_W), BF16),
               pltpu.VMEM((ts, 3 * D_MODEL), F32),
               pltpu.VMEM((WINDOW, B_KV_WIDTH), F32),
               pltpu.VMEM((WINDOW, B_KV_WIDTH), F32),
               pltpu.VMEM((ts + SUBLANES, 2 * C_WIDTH), F32)]),
        compiler_params=pltpu.CompilerParams(
            dimension_semantics=("arbitrary",), vmem_limit_bytes=VMEM_LIMIT),
        name="prompt_layer",
    )(lw["sinks"], x2, x2, mod, mod, lw["ng"], lw["wcat"], lw["bcat"],
      lw["vg"], lw["gws"], lw["gbs_col"], lw["qg"], lw["kg"], lw["cw"], lw["cb"], lw["fb"], lw["hg"],
      *_prompt_mask_constants(),
      lw["wmg"], lw["bmg"], lw["wa"], lw["wb"], lw["wc"], lw["wo"])


def _sample_mix_kernel(sink_ref, za_ref, zb_ref, zc_ref, kc_ref, vc_ref, cs_ref, c0_ref, n0_ref,
                       m0_ref, vg_ref, gwb_ref, gbs_ref, qg_ref, kg_ref, cw_ref, cb_ref, fb_ref,
                       hg_ref,
                       y_ref, vrow_ref, ko_ref, vo_ref, convo_ref, c1_ref, n1_ref, m1_ref,
                       xbuf):
    nb = SAMPLE_NB
    t = SUBLANES
    rows = nb * t
    tok_r = lax.broadcasted_iota(jnp.int32, (rows, rows), 0)
    tok_c = lax.broadcasted_iota(jnp.int32, (rows, rows), 1)
    same_b = (tok_r // t) == (tok_c // t)
    causal_b = same_b & (tok_c <= tok_r)

    u = za_ref[:, 0:A_WIDTH]
    vn = _rms(za_ref[:, A_WIDTH:2 * A_WIDTH]) * vg_ref[...]
    sg = _silu(za_ref[:, 2 * A_WIDTH:3 * A_WIDTH])
    vrow_ref[...] = vn
    vnb = vn.astype(BF16)
    s_cols = []
    for gi in range(A_GROUPS):
        s_cols.append(_dot(gwb_ref[gi], vnb[:, gi * GROUP_DIM:(gi + 1) * GROUP_DIM])
                      + gbs_ref[:, gi:gi + 1])
    y_ref[:, 0:A_WIDTH] = (u * jnp.concatenate(s_cols, axis=1) * sg).astype(BF16)

    qn = _qk_norm(zb_ref[:, 0:B_WIDTH], qg_ref[...]) * (B_HEAD_DIM ** -0.5)
    kn = _qk_norm(zb_ref[:, B_WIDTH:B_WIDTH + B_KV_WIDTH], kg_ref[...])
    vv = zb_ref[:, B_WIDTH + B_KV_WIDTH:B_WIDTH + 2 * B_KV_WIDTH]
    sgb = _silu(zb_ref[:, B_WIDTH + 2 * B_KV_WIDTH:ZB_W])
    kn3 = kn.reshape(nb, t, B_KV_WIDTH)
    vv3 = vv.reshape(nb, t, B_KV_WIDTH)
    kcache = kc_ref[...]
    vcache = vc_ref[...]
    pad = jnp.zeros((nb, WINDOW - t, B_KV_WIDTH), F32)
    kall = jnp.concatenate([kcache, kn3, pad], axis=1).astype(BF16)
    vall = jnp.concatenate([vcache, vv3, pad], axis=1).astype(BF16)
    qp = jnp.concatenate([_place_q_head(qn, h, rows).reshape(nb, t, LANES) for h in range(B_HEADS)],
                         axis=1).astype(BF16)
    logits = lax.dot_general(qp, kall, (((2,), (2,)), ((0,), (0,))), preferred_element_type=F32)
    qrow = lax.broadcasted_iota(jnp.int32, (nb, B_HEADS * t, 2 * WINDOW), 1)
    kcol = lax.broadcasted_iota(jnp.int32, (nb, B_HEADS * t, 2 * WINDOW), 2)
    qt = qrow % t
    valid = ((kcol < WINDOW) & (kcol > qt)) | ((kcol >= WINDOW) & ((kcol - WINDOW) <= qt))
    hrow = lax.broadcasted_iota(jnp.int32, (B_HEADS * t, 1), 0) // t
    snk = jnp.zeros((B_HEADS * t, 1), F32)
    for h in range(B_HEADS):
        snk = jnp.where(hrow == h, sink_ref[h], snk)
    lg = jnp.where(valid, logits, NEG)
    mx = jnp.maximum(jnp.max(lg, axis=-1, keepdims=True), snk[None])
    p = jnp.exp(lg - mx)
    den = jnp.sum(p, axis=-1, keepdims=True) + jnp.exp(snk[None] - mx)
    pv = lax.dot_general(p.astype(BF16), vall, (((2,), (1,)), ((0,), (0,))),
                         preferred_element_type=F32) / den
    head_out = [pv[:, h * t:(h + 1) * t, :].reshape(rows, LANES) for h in range(B_HEADS)]
    yb = jnp.concatenate(
        [_merge_head_pair(head_out[2 * j], head_out[2 * j + 1], 2 * j, rows)
         for j in range(B_HEADS // 2)], axis=1)
    y_ref[:, A_WIDTH:A_WIDTH + B_WIDTH] = (yb * sgb).astype(BF16)
    ko_ref[...] = jnp.concatenate([kcache[:, t:, :], kn3], axis=1)
    vo_ref[...] = jnp.concatenate([vcache[:, t:, :], vv3], axis=1)

    xbuf[:, SUBLANES - (C_CONV - 1):SUBLANES, :] = cs_ref[...]
    xbuf[:, SUBLANES:2 * SUBLANES, :] = zc_ref[:, 0:2 * C_WIDTH].reshape(nb, t, 2 * C_WIDTH)
    y3 = cb_ref[...][None]
    for j in range(C_CONV):
        lo = SUBLANES - (C_CONV - 1) + j
        y3 = y3 + cw_ref[j:j + 1, :][None] * xbuf[:, lo:lo + t, :]
    convo_ref[...] = xbuf[:, 2 * SUBLANES - (C_CONV - 1):2 * SUBLANES, :]
    qk = _silu(y3.reshape(rows, 2 * C_WIDTH))
    qall = qk[:, 0:C_WIDTH].astype(BF16)
    kall_c = qk[:, C_WIDTH:2 * C_WIDTH] * (C_HEAD_DIM ** -0.5)
    vall_c = zc_ref[:, 2 * C_WIDTH:3 * C_WIDTH].astype(BF16)
    gate_o = _sigmoid(zc_ref[:, 3 * C_WIDTH:4 * C_WIDTH]) * _silu(zc_ref[:, 4 * C_WIDTH:5 * C_WIDTH])
    ifp = zc_ref[:, 5 * C_WIDTH:5 * C_WIDTH + LANES]
    lf = _log_sigmoid(ifp + fb_ref[...])
    lane_t = lax.broadcasted_iota(jnp.int32, (rows, LANES), 1)
    cum_all = _dot_exact01(jnp.where(causal_b, 1.0, 0.0).astype(BF16), lf)
    tot_all = _dot_exact01(jnp.where(same_b, 1.0, 0.0).astype(BF16), lf)
    st_col = jnp.where(lane_t < C_HEADS, ifp, cum_all)
    st_row = st_col.T
    tot_row = tot_all.T
    m0 = m0_ref[...]
    same_b_bf = jnp.where(same_b, 1.0, 0.0).astype(BF16)
    batch_of_lane = lax.broadcasted_iota(jnp.int32, (nb, 1, rows), 2) // t
    batch_id = lax.broadcasted_iota(jnp.int32, (nb, 1, rows), 0)
    own_tok = batch_of_lane == batch_id
    h_cols = []
    m_out = jnp.zeros((rows, LANES), F32)
    for hd in range(C_HEADS):
        hs = slice(hd * C_HEAD_DIM, (hd + 1) * C_HEAD_DIM)
        i_c = st_col[:, hd:hd + 1]
        cum_c = st_col[:, C_HEADS + hd:C_HEADS + hd + 1]
        tot_c = tot_all[:, C_HEADS + hd:C_HEADS + hd + 1]
        i_r = st_row[hd:hd + 1, :]
        cum_r = st_row[C_HEADS + hd:C_HEADS + hd + 1, :]
        tot_r = tot_row[C_HEADS + hd:C_HEADS + hd + 1, :]
        m_prev = m0[:, hd:hd + 1]
        dmat = jnp.where(causal_b, cum_c - cum_r + i_r, NEG)
        m_inter = cum_c + m_prev
        m_t = jnp.maximum(m_inter, jnp.max(dmat, axis=-1, keepdims=True))
        q_h = qall[:, hs]
        k_h = kall_c[:, hs]
        v_h = vall_c[:, hs]
        a = jnp.exp(dmat - m_t) * _dot_nt(q_h, k_h.astype(BF16))
        w_inter = jnp.exp(m_inter - m_t)
        c_prev = c0_ref[:, hd]
        n_tok = jnp.broadcast_to(n0_ref[hd][:, None, :], (nb, t, C_HEAD_DIM)).reshape(rows, C_HEAD_DIM)
        inter = lax.dot_general(q_h.reshape(nb, t, C_HEAD_DIM), c_prev.astype(BF16),
                                (((2,), (1,)), ((0,), (0,))), preferred_element_type=F32)
        num = _dot(a.astype(BF16), v_h) + w_inter * inter.reshape(rows, C_HEAD_DIM)
        den = (jnp.sum(a, axis=-1, keepdims=True)
               + w_inter * jnp.sum(q_h.astype(F32) * n_tok, axis=-1, keepdims=True))
        hh = num / jnp.maximum(jnp.abs(den), jnp.exp(-m_t))
        h_cols.append(_rms(hh))
        g_r = tot_r - cum_r + i_r
        g_c = tot_c - cum_c + i_c
        m_new = jnp.maximum(tot_c + m_prev,
                            jnp.max(jnp.where(same_b, g_r, NEG), axis=-1, keepdims=True))
        kw = jnp.exp(g_c - m_new) * k_h
        decay = jnp.exp(tot_c + m_prev - m_new)
        kwt = kw.T
        lhs = jnp.where(own_tok, kwt[None], 0.0).astype(BF16).reshape(nb * C_HEAD_DIM, rows)
        upd = _dot(lhs, v_h).reshape(nb, C_HEAD_DIM, C_HEAD_DIM)
        dec_b = jnp.broadcast_to(decay, (rows, C_HEAD_DIM)).reshape(nb, t, C_HEAD_DIM)[:, 0:1, :]
        c1_ref[:, hd] = dec_b * c_prev + upd
        n1_ref[hd] = decay * n_tok + _dot(same_b_bf, kw.astype(BF16))
        m_out = jnp.where(lane_t == hd, m_new, m_out)
    m1_ref[...] = m_out
    hn = jnp.concatenate(h_cols, axis=1) * hg_ref[...]
    y_ref[:, A_WIDTH + B_WIDTH:Y_W] = (hn * gate_o).astype(BF16)


def _sample_mix_call(l, za, zb, zc, kc, vc, cs, c0, n0t, m0tok, lw, nbatch):
    nb = SAMPLE_NB
    t = SUBLANES
    rows = nb * t
    tok = lambda i: (i, 0)
    const2 = lambda i: (0, 0)
    const3 = lambda i: (0, 0, 0)
    b3 = lambda i: (i, 0, 0)
    lb4 = lambda i: (l, i, 0, 0)
    return pl.pallas_call(
        _sample_mix_kernel,
        grid=(nbatch // nb,),
        in_specs=[
            pl.BlockSpec(memory_space=pltpu.SMEM),
            pl.BlockSpec((rows, ZA_W), tok),
            pl.BlockSpec((rows, ZB_W), tok),
            pl.BlockSpec((rows, ZC_W), tok),
            pl.BlockSpec((None, nb, WINDOW, B_KV_WIDTH), lb4),
            pl.BlockSpec((None, nb, WINDOW, B_KV_WIDTH), lb4),
            pl.BlockSpec((None, nb, C_CONV - 1, 2 * C_WIDTH), lb4),
            pl.BlockSpec((None, nb, C_HEADS, C_HEAD_DIM, C_HEAD_DIM), lambda i: (l, i, 0, 0, 0)),
            pl.BlockSpec((None, C_HEADS, nb, C_HEAD_DIM), lambda i: (l, 0, i, 0)),
            pl.BlockSpec((None, rows, LANES), lambda i: (l, i, 0)),
            pl.BlockSpec((1, A_WIDTH), const2),
            pl.BlockSpec((A_GROUPS, rows, rows), const3),
            pl.BlockSpec((rows, LANES), const2),
            pl.BlockSpec((1, B_WIDTH), const2),
            pl.BlockSpec((1, B_KV_WIDTH), const2),
            pl.BlockSpec((C_CONV, 2 * C_WIDTH), const2),
            pl.BlockSpec((1, 2 * C_WIDTH), const2),
            pl.BlockSpec((1, LANES), const2),
            pl.BlockSpec((1, C_WIDTH), const2),
        ],
        out_specs=[
            pl.BlockSpec((rows, Y_W), tok),
            pl.BlockSpec((rows, A_WIDTH), tok),
            pl.BlockSpec((nb, WINDOW, B_KV_WIDTH), b3),
            pl.BlockSpec((nb, WINDOW, B_KV_WIDTH), b3),
            pl.BlockSpec((nb, C_CONV - 1, 2 * C_WIDTH), b3),
            pl.BlockSpec((nb, C_HEADS, C_HEAD_DIM, C_HEAD_DIM), lambda i: (i, 0, 0, 0)),
            pl.BlockSpec((C_HEADS, rows, C_HEAD_DIM), lambda i: (0, i, 0)),
            pl.BlockSpec((rows, LANES), tok),
        ],
        out_shape=[
            jax.ShapeDtypeStruct((nbatch * t, Y_W), BF16),
            jax.ShapeDtypeStruct((nbatch * t, A_WIDTH), F32),
            jax.ShapeDtypeStruct((nbatch, WINDOW, B_KV_WIDTH), F32),
            jax.ShapeDtypeStruct((nbatch, WINDOW, B_KV_WIDTH), F32),
            jax.ShapeDtypeStruct((nbatch, C_CONV - 1, 2 * C_WIDTH), F32),
            jax.ShapeDtypeStruct((nbatch, C_HEADS, C_HEAD_DIM, C_HEAD_DIM), F32),
            jax.ShapeDtypeStruct((C_HEADS, nbatch * t, C_HEAD_DIM), F32),
            jax.ShapeDtypeStruct((nbatch * t, LANES), F32),
        ],
        scratch_shapes=[pltpu.VMEM((nb, 2 * SUBLANES, 2 * C_WIDTH), F32)],
        compiler_params=pltpu.CompilerParams(
            dimension_semantics=("arbitrary",), vmem_limit_bytes=VMEM_LIMIT),
        name="sample_mixer",
    )(lw["sinks"], za, zb, zc, kc, vc, cs, c0, n0t, m0tok, lw["vg"], lw["gwb"], lw["gbs_tok"],
      lw["qg"], lw["kg"], lw["cw"], lw["cb"], lw["fb"], lw["hg"])


def _layer_weights(l, wcat_all, wmg_all, b_in, gmlp_vnorm_g, gmlp_ws, gmlp_bs, swa_qnorm_g,
                   swa_knorm_g, swa_sinks, mlstm_conv_w, mlstm_conv_b, mlstm_f_bias, mlstm_hnorm_g,
                   w_branch_a, w_branch_b, w_branch_c, w_out, norm_g, dec_seq):
    bl = b_in[l]
    bcat = jnp.concatenate([bl[:COL_CI], bl[COL_CO:COL_MG], bl[COL_CI:COL_CO],
                            jnp.zeros((LANES - 2 * C_HEADS,), F32)])
    t = dec_seq
    nb = SAMPLE_NB
    ws_t = gmlp_ws[l][:, :t, :t] * jnp.tril(jnp.ones((t, t), F32))
    eye = jnp.eye(nb, dtype=F32)
    gwb = jnp.einsum("bc,gts->gbtcs", eye, ws_t).reshape(A_GROUPS, nb * t, nb * t).astype(BF16)
    gbs_col = jnp.pad(gmlp_bs[l].T, ((0, 0), (0, LANES - A_GROUPS)))
    gbs_tok = jnp.pad(jnp.tile(gmlp_bs[l][:, :t].T, (nb, 1)), ((0, 0), (0, LANES - A_GROUPS)))
    fb = jnp.pad(mlstm_f_bias[l], (C_HEADS, LANES - 2 * C_HEADS)).reshape(1, LANES)
    return dict(
        ng=norm_g[l].reshape(1, D_MODEL),
        wcat=wcat_all, bcat=bcat.reshape(1, ZCAT_W),
        wmg=wmg_all, bmg=bl[COL_MG:].reshape(1, 3 * D_MODEL),
        wa=w_branch_a[l].astype(BF16), wb=w_branch_b[l].astype(BF16),
        wc=w_branch_c[l].astype(BF16), wo=w_out[l].astype(BF16),
        vg=gmlp_vnorm_g[l].reshape(1, A_WIDTH), gws=gmlp_ws[l], gwb=gwb,
        gbs_col=gbs_col, gbs_tok=gbs_tok,
        qg=jnp.tile(swa_qnorm_g[l], B_HEADS).reshape(1, B_WIDTH),
        kg=jnp.tile(swa_knorm_g[l], B_KV_HEADS).reshape(1, B_KV_WIDTH),
        sinks=swa_sinks[l],
        cw=mlstm_conv_w[l], cb=mlstm_conv_b[l].reshape(1, 2 * C_WIDTH), fb=fb,
        hg=mlstm_hnorm_g[l].reshape(1, C_WIDTH),
    )


def kernel(x_prompt, x_sample, cache_swa_k, cache_swa_v, state_mlstm_conv, state_mlstm_C, state_mlstm_n, state_mlstm_m, c_prompt, c_sample, ada_w, ada_b, norm_g, w_in, b_in, gmlp_vnorm_g, gmlp_ws, gmlp_bs, swa_qnorm_g, swa_knorm_g, swa_sinks, mlstm_conv_w, mlstm_conv_b, mlstm_f_bias, mlstm_hnorm_g, w_branch_a, w_branch_b, w_branch_c, w_out):
    batch, seq, _ = x_prompt.shape
    nbatch, dec_seq, _ = x_sample.shape
    assert dec_seq == SUBLANES and seq % PROMPT_TILE == 0 and nbatch % SAMPLE_NB == 0
    assert seq % PROJ_TILE == 0 and (nbatch * dec_seq) % PROJ_TILE == 0
    wb_len = cache_swa_k.shape[2]
    assert wb_len == WINDOW

    nc = batch + nbatch
    nc_pad = -(-nc // SUBLANES) * SUBLANES
    c_all = jnp.concatenate([c_prompt, c_sample, jnp.zeros((nc_pad - nc, D_MODEL), F32)], axis=0)
    mod_all = _ada_call(c_all, ada_w, ada_b)

    xp = x_prompt.reshape(batch * seq, D_MODEL)
    xs = x_sample.reshape(nbatch * dec_seq, D_MODEL)
    kc_all = cache_swa_k.reshape(DEPTH, nbatch, WINDOW, B_KV_WIDTH)
    vc_all = cache_swa_v.reshape(DEPTH, nbatch, WINDOW, B_KV_WIDTH)
    n0t_all = jnp.transpose(state_mlstm_n, (0, 2, 1, 3))
    m0tok_all = jnp.pad(jnp.repeat(state_mlstm_m, dec_seq, axis=1),
                        ((0, 0), (0, 0), (0, LANES - C_HEADS)))
    wcat_all, wmg_all = _weight_prep_call(w_in)
    outs_p = [[] for _ in range(6)]
    outs_s = [[] for _ in range(6)]
    vrows = []
    for l in range(DEPTH):
        lw = _layer_weights(l, wcat_all, wmg_all, b_in, gmlp_vnorm_g, gmlp_ws, gmlp_bs, swa_qnorm_g,
                            swa_knorm_g, swa_sinks, mlstm_conv_w, mlstm_conv_b, mlstm_f_bias,
                            mlstm_hnorm_g, w_branch_a, w_branch_b, w_branch_c, w_out, norm_g,
                            dec_seq)
        mod_p = mod_all[l, :batch].reshape(batch, 1, 3 * D_MODEL)
        mod_s = jnp.repeat(mod_all[l, batch:nc], dec_seq, axis=0)

        xp, ko, vo, convo, c1, n1, m1 = _prompt_layer_call(l, xp, mod_p, lw, batch, seq)
        outs_p[0].append(ko.reshape(batch, WINDOW, B_KV_HEADS, B_HEAD_DIM))
        outs_p[1].append(vo.reshape(batch, WINDOW, B_KV_HEADS, B_HEAD_DIM))
        outs_p[2].append(convo[:, SUBLANES - (C_CONV - 1):, :])
        outs_p[3].append(c1)
        outs_p[4].append(n1)
        outs_p[5].append(m1[:, 0, :C_HEADS])

        za, zb, zc = _inproj_call(l, xs, mod_s, lw["ng"], lw["wcat"], lw["bcat"], None)
        y, vrow, ko, vo, convo, c1, n1tok, m1tok = _sample_mix_call(
            l, za, zb, zc, kc_all, vc_all, state_mlstm_conv, state_mlstm_C, n0t_all, m0tok_all,
            lw, nbatch)
        xs = _outproj_call(l, xs, mod_s, lw["ng"], y, lw["wmg"], lw["bmg"], lw["wa"], lw["wb"],
                           lw["wc"], lw["wo"], None)
        outs_s[0].append(ko.reshape(nbatch, WINDOW, B_KV_HEADS, B_HEAD_DIM))
        outs_s[1].append(vo.reshape(nbatch, WINDOW, B_KV_HEADS, B_HEAD_DIM))
        outs_s[2].append(convo)
        outs_s[3].append(c1)
        outs_s[4].append(jnp.transpose(n1tok[:, ::dec_seq, :], (1, 0, 2)))
        outs_s[5].append(m1tok[::dec_seq, :C_HEADS])
        vrows.append(vrow.reshape(nbatch, dec_seq, A_WIDTH))

    sp = [jnp.stack(o) for o in outs_p]
    ss = [jnp.stack(o) for o in outs_s]
    return (xp.reshape(batch, seq, D_MODEL), xs.reshape(nbatch, dec_seq, D_MODEL),
            sp[0], sp[1], sp[2], sp[3], sp[4], sp[5],
            ss[0], ss[1], ss[2], ss[3], ss[4], ss[5], jnp.stack(vrows))
```

```python
import functools

import numpy as np
import jax
import jax.numpy as jnp
from jax import lax
from jax.experimental import pallas as pl
from jax.experimental.pallas import tpu as pltpu

F32 = jnp.float32
BF16 = jnp.bfloat16

D_MODEL = 1024
DEPTH = 2
A_WIDTH = 512
A_GROUPS = 4
GROUP_DIM = 128
B_HEADS = 8
B_KV_HEADS = 2
B_HEAD_DIM = 64
B_WIDTH = 512
B_KV_WIDTH = 128
WINDOW = 128
C_HEADS = 4
C_HEAD_DIM = 128
C_WIDTH = 512
C_CONV = 4
EPS = 1e-6
NEG = -1e30

LANES = 128
SUBLANES = 8
VMEM_LIMIT = 56 * 1024 * 1024

ZA_W = 3 * A_WIDTH
ZB_W = 2 * B_WIDTH + 2 * B_KV_WIDTH
ZC_W = 2 * C_WIDTH + 3 * C_WIDTH + LANES
ZCAT_W = ZA_W + ZB_W + ZC_W
Y_W = A_WIDTH + B_WIDTH + C_WIDTH

PROMPT_TILE = 256
MLSTM_CHUNK = PROMPT_TILE
SAMPLE_NB = 16
PROJ_TILE = 512


def _sigmoid(x):
    return 0.5 * jnp.tanh(0.5 * x) + 0.5


def _silu(x):
    t = 0.5 * x
    return t * (jnp.tanh(t) + 1.0)


def _log_sigmoid(x):
    return jnp.minimum(x, 0.0) - jnp.log1p(jnp.exp(-jnp.abs(x)))


def _rms(x):
    return x * lax.rsqrt(jnp.mean(x * x, axis=-1, keepdims=True) + EPS)


def _dot(a, b):
    return jnp.dot(a, b, preferred_element_type=F32)


def _dot_nt(a, b):
    return lax.dot_general(a, b, (((1,), (1,)), ((), ())), preferred_element_type=F32)


def _dot_exact01(m01, x):
    hi = x.astype(BF16)
    r1 = x - hi.astype(F32)
    mid = r1.astype(BF16)
    lo = (r1 - mid.astype(F32)).astype(BF16)
    return _dot(m01, hi) + _dot(m01, mid) + _dot(m01, lo)


def _modulated_norm(x, mod_ref, ng_ref):
    xn = _rms(x) * ng_ref[...]
    shift = mod_ref[:, 0:D_MODEL]
    scale = mod_ref[:, D_MODEL:2 * D_MODEL]
    return (xn * (1.0 + scale) + shift).astype(BF16)


def _head_rms_scale(x2, lane_lo):
    s0 = jnp.sum(jnp.where(lane_lo, x2, 0.0), axis=-1, keepdims=True)
    s1 = jnp.sum(jnp.where(lane_lo, 0.0, x2), axis=-1, keepdims=True)
    r0 = lax.rsqrt(s0 * (1.0 / B_HEAD_DIM) + EPS)
    r1 = lax.rsqrt(s1 * (1.0 / B_HEAD_DIM) + EPS)
    return jnp.where(lane_lo, r0, r1)


def _qk_norm(x, g_row):
    rows, width = x.shape
    lane_lo = lax.broadcasted_iota(jnp.int32, (rows, LANES), 1) < B_HEAD_DIM
    outs = []
    for j in range(width // LANES):
        slab = x[:, j * LANES:(j + 1) * LANES]
        outs.append(slab * _head_rms_scale(slab * slab, lane_lo))
    y = outs[0] if len(outs) == 1 else jnp.concatenate(outs, axis=1)
    return y * g_row


def _ada_kernel(c_ref, w_ref, b_ref, o_ref):
    c = c_ref[...]
    o_ref[...] = _dot(_silu(c).astype(BF16), w_ref[...].astype(BF16)) + b_ref[...]


def _ada_call(c_all, ada_w, ada_b):
    rows = c_all.shape[0]
    return pl.pallas_call(
        _ada_kernel,
        grid=(DEPTH, 3),
        in_specs=[
            pl.BlockSpec((rows, D_MODEL), lambda l, j: (0, 0)),
            pl.BlockSpec((None, D_MODEL, D_MODEL), lambda l, j: (l, 0, j)),
            pl.BlockSpec((None, 1, D_MODEL), lambda l, j: (l, 0, j)),
        ],
        out_specs=pl.BlockSpec((None, rows, D_MODEL), lambda l, j: (l, 0, j)),
        out_shape=jax.ShapeDtypeStruct((DEPTH, rows, 3 * D_MODEL), F32),
        compiler_params=pltpu.CompilerParams(
            dimension_semantics=("arbitrary", "arbitrary"), vmem_limit_bytes=VMEM_LIMIT),
        name="adaln_mod",
    )(c_all, ada_w, ada_b.reshape(DEPTH, 1, 3 * D_MODEL))


COL_CI = ZA_W + ZB_W + 3 * C_WIDTH
COL_CO = COL_CI + 2 * C_HEADS
COL_MG = COL_CO + 2 * C_WIDTH
PREP_CHUNK = 256
PREP_SHIFT = 2 * C_HEADS
N_MAIN = COL_CI // PREP_CHUNK
N_CO = (2 * C_WIDTH) // PREP_CHUNK
N_MG = (3 * D_MODEL) // PREP_CHUNK
J_CIF = N_MAIN + N_CO
J_MG = J_CIF + 1


def _weight_prep_kernel(wa_ref, wb_ref, wcat_ref, wmg_ref):
    j = pl.program_id(1)

    def shifted_t():
        rows = jnp.concatenate([wa_ref[PREP_SHIFT:PREP_CHUNK, :], wb_ref[...]], axis=0)
        return rows.astype(BF16).T

    @pl.when(j < N_MAIN)
    def _():
        wcat_ref[...] = wa_ref[...].astype(BF16).T

    @pl.when((j >= N_MAIN) & (j < J_CIF))
    def _():
        wcat_ref[...] = shifted_t()

    @pl.when(j == J_CIF)
    def _():
        row = lax.broadcasted_iota(jnp.int32, (PREP_CHUNK, D_MODEL), 0)
        wcat_ref[...] = jnp.where(row < PREP_SHIFT, wa_ref[...], 0.0).astype(BF16).T

    @pl.when(j >= J_MG)
    def _():
        wmg_ref[...] = shifted_t()


def _weight_prep_call(w_in):
    in_width = w_in.shape[-1]
    assert in_width == COL_MG + 3 * D_MODEL
    assert COL_CI % PREP_CHUNK == 0 and COL_CO % PREP_CHUNK == PREP_SHIFT == COL_MG % PREP_CHUNK
    w_t = jnp.swapaxes(w_in, 1, 2)
    assert in_width % PREP_SHIFT == 0 and PREP_SHIFT == SUBLANES
    last_rows = in_width // PREP_SHIFT - 1
    groups_per_chunk = PREP_CHUNK // PREP_SHIFT

    def src_block(j):
        return jnp.where(j < J_CIF, j, jnp.where(j == J_CIF, N_MAIN, j - 1))

    return pl.pallas_call(
        _weight_prep_kernel,
        grid=(DEPTH, J_MG + N_MG),
        in_specs=[
            pl.BlockSpec((None, PREP_CHUNK, D_MODEL), lambda l, j: (l, src_block(j), 0)),
            pl.BlockSpec((None, PREP_SHIFT, D_MODEL),
                         lambda l, j: (l, jnp.minimum((src_block(j) + 1) * groups_per_chunk,
                                                      last_rows), 0)),
        ],
        out_specs=[
            pl.BlockSpec((None, D_MODEL, PREP_CHUNK), lambda l, j: (l, 0, jnp.minimum(j, J_CIF))),
            pl.BlockSpec((None, D_MODEL, PREP_CHUNK), lambda l, j: (l, 0, jnp.maximum(j - J_MG, 0))),
        ],
        out_shape=[
            jax.ShapeDtypeStruct((DEPTH, D_MODEL, ZCAT_W), BF16),
            jax.ShapeDtypeStruct((DEPTH, D_MODEL, 3 * D_MODEL), BF16),
        ],
        compiler_params=pltpu.CompilerParams(
            dimension_semantics=("arbitrary", "arbitrary"), vmem_limit_bytes=VMEM_LIMIT),
        name="weight_prep",
    )(w_t, w_t)


def _col_chunks(width, step):
    return [(o, min(step, width - o)) for o in range(0, width, step)]


def _inproj_pieces(get_h, w_ref, b_ref, za_ref, zb_ref, zc_ref, step):
    def piece(o_ref, off, woff, w):
        def run():
            o_ref[:, off:off + w] = _dot(get_h(), w_ref[:, woff:woff + w]) + b_ref[:, woff:woff + w]
        return run
    pieces = []
    base = 0
    for o_ref, width in ((za_ref, ZA_W), (zb_ref, ZB_W), (zc_ref, ZC_W)):
        pieces += [piece(o_ref, off, base + off, w) for off, w in _col_chunks(width, step)]
        base += width
    return pieces


def _inproj_kernel(x_ref, mod_ref, ng_ref, w_ref, b_ref, za_ref, zb_ref, zc_ref):
    h = _modulated_norm(x_ref[...], mod_ref, ng_ref)
    for piece in _inproj_pieces(lambda: h, w_ref, b_ref, za_ref, zb_ref, zc_ref, 512):
        piece()


def _mod_spec(tm, tokens_per_batch):
    if tokens_per_batch is None:
        return pl.BlockSpec((tm, 3 * D_MODEL), lambda i: (i, 0))
    tiles_per_batch = tokens_per_batch // tm
    return pl.BlockSpec((None, 1, 3 * D_MODEL), lambda i: (i // tiles_per_batch, 0, 0))


def _layer_weight_spec(layer, rows, cols):
    return pl.BlockSpec((None, rows, cols), lambda i: (layer, 0, 0), pipeline_mode=pl.Buffered(1))


def _inproj_call(layer, x2, mod, ng, wcat, bcat, tokens_per_batch):
    ntok = x2.shape[0]
    tm = PROJ_TILE
    const = lambda i: (0, 0)
    return pl.pallas_call(
        _inproj_kernel,
        grid=(ntok // tm,),
        in_specs=[
            pl.BlockSpec((tm, D_MODEL), lambda i: (i, 0)),
            _mod_spec(tm, tokens_per_batch),
            pl.BlockSpec((1, D_MODEL), const),
            _layer_weight_spec(layer, D_MODEL, ZCAT_W),
            pl.BlockSpec((1, ZCAT_W), const),
        ],
        out_specs=[
            pl.BlockSpec((tm, ZA_W), lambda i: (i, 0)),
            pl.BlockSpec((tm, ZB_W), lambda i: (i, 0)),
            pl.BlockSpec((tm, ZC_W), lambda i: (i, 0)),
        ],
        out_shape=[
            jax.ShapeDtypeStruct((ntok, ZA_W), F32),
            jax.ShapeDtypeStruct((ntok, ZB_W), F32),
            jax.ShapeDtypeStruct((ntok, ZC_W), F32),
        ],
        compiler_params=pltpu.CompilerParams(
            dimension_semantics=("arbitrary",), vmem_limit_bytes=VMEM_LIMIT),
        name="in_projection",
    )(x2, mod, ng, wcat, bcat)


def _outproj_kernel(x_ref, mod_ref, ng_ref, y_ref, wmg_ref, bmg_ref, wa_ref, wb_ref, wc_ref,
                    wo_ref, o_ref):
    x = x_ref[...]
    h = _modulated_norm(x, mod_ref, ng_ref)
    merged = None
    for i, wbr_ref in enumerate((wa_ref, wb_ref, wc_ref)):
        cols = slice(i * D_MODEL, (i + 1) * D_MODEL)
        gate = _sigmoid(_dot(h, wmg_ref[:, cols]) + bmg_ref[:, cols])
        term = gate * _dot(y_ref[:, i * A_WIDTH:(i + 1) * A_WIDTH], wbr_ref[...])
        merged = term if merged is None else merged + term
    ada_gate = mod_ref[:, 2 * D_MODEL:3 * D_MODEL]
    o_ref[...] = x + ada_gate * _dot(merged.astype(BF16), wo_ref[...])


def _outproj_call(layer, x2, mod, ng, y, wmg, bmg, wa, wb, wc, wo, tokens_per_batch):
    ntok = x2.shape[0]
    tm = PROJ_TILE
    const = lambda i: (0, 0)
    once = pl.Buffered(1)
    return pl.pallas_call(
        _outproj_kernel,
        grid=(ntok // tm,),
        in_specs=[
            pl.BlockSpec((tm, D_MODEL), lambda i: (i, 0)),
            _mod_spec(tm, tokens_per_batch),
            pl.BlockSpec((1, D_MODEL), const),
            pl.BlockSpec((tm, Y_W), lambda i: (i, 0)),
            _layer_weight_spec(layer, D_MODEL, 3 * D_MODEL),
            pl.BlockSpec((1, 3 * D_MODEL), const),
            pl.BlockSpec((A_WIDTH, D_MODEL), const, pipeline_mode=once),
            pl.BlockSpec((B_WIDTH, D_MODEL), const, pipeline_mode=once),
            pl.BlockSpec((C_WIDTH, D_MODEL), const, pipeline_mode=once),
            pl.BlockSpec((D_MODEL, D_MODEL), const, pipeline_mode=once),
        ],
        out_specs=pl.BlockSpec((tm, D_MODEL), lambda i: (i, 0)),
        out_shape=jax.ShapeDtypeStruct((ntok, D_MODEL), F32),
        compiler_params=pltpu.CompilerParams(
            dimension_semantics=("arbitrary",), vmem_limit_bytes=VMEM_LIMIT),
        name="out_projection",
    )(x2, mod, ng, y, wmg, bmg, wa, wb, wc, wo)


def _place_q_head(qn, h, rows):
    lane = lax.broadcasted_iota(jnp.int32, (rows, LANES), 1)
    slab = qn[:, (h // 2) * LANES:(h // 2 + 1) * LANES]
    src_hi = h % 2
    dst_hi = h // (B_HEADS // B_KV_HEADS)
    keep = (lane >= B_HEAD_DIM) if src_hi else (lane < B_HEAD_DIM)
    slab = jnp.where(keep, slab, 0.0)
    if src_hi != dst_hi:
        slab = pltpu.roll(slab, B_HEAD_DIM, 1)
    return slab


def _merge_head_pair(o_even, o_odd, h_even, rows):
    lane_lo = lax.broadcasted_iota(jnp.int32, (rows, LANES), 1) < B_HEAD_DIM
    kv_hi = h_even // (B_HEADS // B_KV_HEADS)
    if kv_hi:
        o_even = pltpu.roll(o_even, B_HEAD_DIM, 1)
    else:
        o_odd = pltpu.roll(o_odd, B_HEAD_DIM, 1)
    return jnp.where(lane_lo, o_even, o_odd)


def _conv_taps(xbuf, cw_ref, cb_ref, cols, ts):
    y = cb_ref[:, cols]
    for j in range(C_CONV):
        lo = SUBLANES - (C_CONV - 1) + j
        y = y + cw_ref[j:j + 1, cols] * xbuf[lo:lo + ts, cols]
    return y


def _prompt_mix_kernel(sink_ref, za_ref, zb_ref, zc_ref, vg_ref, gw_ref, gbs_ref, qg_ref, kg_ref,
                       cw_ref, cb_ref, fb_ref, hg_ref, tril_ref, band_ref, tri01_ref, tribias_ref,
                       y_ref, ko_ref, vo_ref, convo_ref, c_ref, n_ref, m_ref,
                       kprev, vprev, xbuf, first_tile, pump):
    ts = PROMPT_TILE

    u = za_ref[:, 0:A_WIDTH]
    vn = _rms(za_ref[:, A_WIDTH:2 * A_WIDTH]) * vg_ref[...]
    sg = _silu(za_ref[:, 2 * A_WIDTH:3 * A_WIDTH])
    vnb = vn.astype(BF16)
    wts = [(gw_ref[gi] * tril_ref[...]).astype(BF16) for gi in range(A_GROUPS)]
    s_rows = []
    for c in range(ts // WINDOW):
        s_cols = []
        for gi in range(A_GROUPS):
            vblk = vnb[c * WINDOW:(c + 1) * WINDOW, gi * GROUP_DIM:(gi + 1) * GROUP_DIM]
            s_cols.append(_dot(wts[gi], vblk) + gbs_ref[:, gi:gi + 1])
        s_rows.append(jnp.concatenate(s_cols, axis=1))
    s = jnp.concatenate(s_rows, axis=0)
    pump()
    y_ref[:, 0:A_WIDTH] = (u * s * sg).astype(BF16)
    pump()

    qn = _qk_norm(zb_ref[:, 0:B_WIDTH], qg_ref[...]) * (B_HEAD_DIM ** -0.5)
    pump()
    kn = _qk_norm(zb_ref[:, B_WIDTH:B_WIDTH + B_KV_WIDTH], kg_ref[...])
    vv = zb_ref[:, B_WIDTH + B_KV_WIDTH:B_WIDTH + 2 * B_KV_WIDTH]
    pump()
    grp = B_HEADS // B_KV_HEADS
    nblk = ts // WINDOW
    lane_lo2 = lax.broadcasted_iota(jnp.int32, (2 * WINDOW, LANES), 1) < B_HEAD_DIM
    kblocks = [kprev[...]] + [kn[b * WINDOW:(b + 1) * WINDOW] for b in range(nblk)]
    vblocks = [vprev[...]] + [vv[b * WINDOW:(b + 1) * WINDOW] for b in range(nblk)]
    gs = range(grp)
    for blk in range(nblk):
        rows = slice(blk * WINDOW, (blk + 1) * WINDOW)
        if blk == 0 and first_tile is not False:
            bias = jnp.where(first_tile, band_ref[1], band_ref[0])
        else:
            bias = band_ref[0]
        kcat = jnp.concatenate([kblocks[blk], kblocks[blk + 1]], axis=0)
        vcat = jnp.concatenate([vblocks[blk], vblocks[blk + 1]], axis=0)
        krol = pltpu.roll(kcat, B_HEAD_DIM, 1)
        vrol = pltpu.roll(vcat, B_HEAD_DIM, 1)
        pump()
        for kh in range(B_KV_HEADS):
            own = lane_lo2 if kh == 0 else jnp.logical_not(lane_lo2)
            kdup = jnp.where(own, kcat, krol).astype(BF16)
            vdup = jnp.where(own, vcat, vrol).astype(BF16)
            snk = [sink_ref[kh * grp + g] for g in gs]
            qs = jnp.concatenate([_place_q_head(qn[rows], kh * grp + g, WINDOW) for g in gs],
                                 axis=0).astype(BF16)
            logits = _dot_nt(qs, kdup)
            pump()
            lg = [logits[g * WINDOW:(g + 1) * WINDOW] + bias for g in gs]
            mx = [jnp.maximum(jnp.max(lg[g], axis=-1, keepdims=True), snk[g]) for g in gs]
            p = [jnp.exp(lg[g] - mx[g]) for g in gs]
            pump()
            rden = [1.0 / (jnp.sum(p[g], axis=-1, keepdims=True) + jnp.exp(snk[g] - mx[g]))
                    for g in gs]
            pv = _dot(jnp.concatenate([p[g].astype(BF16) for g in gs], axis=0), vdup)
            pump()
            outs = [pv[g * WINDOW:(g + 1) * WINDOW] * rden[g] for g in gs]
            yb = jnp.concatenate([_merge_head_pair(outs[2 * j], outs[2 * j + 1], kh * grp + 2 * j, WINDOW)
                                  for j in range(grp // 2)], axis=1)
            b_cols = slice(kh * grp * B_HEAD_DIM, (kh + 1) * grp * B_HEAD_DIM)
            sgb = _silu(zb_ref[rows, B_WIDTH + 2 * B_KV_WIDTH + b_cols.start:
                               B_WIDTH + 2 * B_KV_WIDTH + b_cols.stop])
            y_ref[rows, A_WIDTH + b_cols.start:A_WIDTH + b_cols.stop] = (yb * sgb).astype(BF16)
            pump()
    kprev[...] = kblocks[nblk]
    vprev[...] = vblocks[nblk]
    ko_ref[...] = kblocks[nblk]
    vo_ref[...] = vblocks[nblk]
    pump()

    xbuf[SUBLANES:SUBLANES + ts, :] = zc_ref[:, 0:2 * C_WIDTH]
    qk = _silu(_conv_taps(xbuf, cw_ref, cb_ref, slice(0, 2 * C_WIDTH), ts))
    pump()
    tail = xbuf[ts:ts + SUBLANES, :]
    xbuf[0:SUBLANES, :] = tail
    convo_ref[...] = tail
    qall = qk[:, 0:C_WIDTH].astype(BF16)
    kall = qk[:, C_WIDTH:2 * C_WIDTH] * (C_HEAD_DIM ** -0.5)
    pump()
    ifp = zc_ref[:, 5 * C_WIDTH:5 * C_WIDTH + LANES]
    lf = _log_sigmoid(ifp + fb_ref[...])
    pump()
    cl = MLSTM_CHUNK
    lane_c = lax.broadcasted_iota(jnp.int32, (cl, LANES), 1)
    lane_1 = lax.broadcasted_iota(jnp.int32, (1, LANES), 1)
    m_row = m_ref[...]
    m_out = m_row
    cum_all = _dot_exact01(tri01_ref[...], lf)
    st_col = jnp.where(lane_c < C_HEADS, ifp, cum_all)
    st_row = st_col.T
    tribias = tribias_ref[...]
    pump()
    for hds in MLSTM_HEAD_GROUPS:
        hs = {hd: slice(hd * C_HEAD_DIM, (hd + 1) * C_HEAD_DIM) for hd in hds}
        i_c = {hd: st_col[:, hd:hd + 1] for hd in hds}
        cum_c = {hd: st_col[:, C_HEADS + hd:C_HEADS + hd + 1] for hd in hds}
        i_r = {hd: st_row[hd:hd + 1, :] for hd in hds}
        cum_r = {hd: st_row[C_HEADS + hd:C_HEADS + hd + 1, :] for hd in hds}
        m_prev = {hd: m_row[:, hd:hd + 1] for hd in hds}
        dmat = {hd: cum_c[hd] - cum_r[hd] + i_r[hd] + tribias for hd in hds}
        m_inter = {hd: cum_c[hd] + m_prev[hd] for hd in hds}
        m_t = {hd: jnp.maximum(m_inter[hd], jnp.max(dmat[hd], axis=-1, keepdims=True)) for hd in hds}
        pump()
        q_h = {hd: qall[:, hs[hd]] for hd in hds}
        k_h = {hd: kall[:, hs[hd]] for hd in hds}
        v_h = {hd: zc_ref[:, 2 * C_WIDTH + hd * C_HEAD_DIM:2 * C_WIDTH + (hd + 1) * C_HEAD_DIM].astype(BF16)
               for hd in hds}
        s_qk = {hd: _dot_nt(q_h[hd], k_h[hd].astype(BF16)) for hd in hds}
        a = {hd: jnp.exp(dmat[hd] - m_t[hd]) * s_qk[hd] for hd in hds}
        pump()
        w_inter = {hd: jnp.exp(m_inter[hd] - m_t[hd]) for hd in hds}
        c_prev = {hd: c_ref[hd] for hd in hds}
        n_prev = {hd: n_ref[hd:hd + 1, :] for hd in hds}
        inter = {hd: _dot(q_h[hd], c_prev[hd].astype(BF16)) for hd in hds}
        intra = {hd: _dot(a[hd].astype(BF16), v_h[hd]) for hd in hds}
        pump()
        den = {hd: jnp.sum(a[hd], axis=-1, keepdims=True)
               + w_inter[hd] * jnp.sum(q_h[hd].astype(F32) * n_prev[hd], axis=-1, keepdims=True)
               for hd in hds}
        rnorm = {hd: 1.0 / jnp.maximum(jnp.abs(den[hd]), jnp.exp(-m_t[hd])) for hd in hds}
        hh = {hd: (intra[hd] + w_inter[hd] * inter[hd]) * rnorm[hd] for hd in hds}
        pump()
        for hd in hds:
            o_cols = slice(3 * C_WIDTH + hd * C_HEAD_DIM, 3 * C_WIDTH + (hd + 1) * C_HEAD_DIM)
            g_cols = slice(4 * C_WIDTH + hd * C_HEAD_DIM, 4 * C_WIDTH + (hd + 1) * C_HEAD_DIM)
            gate_o = _sigmoid(zc_ref[:, o_cols]) * _silu(zc_ref[:, g_cols])
            y_cols = slice(A_WIDTH + B_WIDTH + hd * C_HEAD_DIM, A_WIDTH + B_WIDTH + (hd + 1) * C_HEAD_DIM)
            y_ref[:, y_cols] = (_rms(hh[hd]) * hg_ref[:, hs[hd]] * gate_o).astype(BF16)
        pump()
        total = {hd: cum_r[hd][:, cl - 1:cl] for hd in hds}
        g_r = {hd: total[hd] - cum_r[hd] + i_r[hd] for hd in hds}
        g_c = {hd: total[hd] - cum_c[hd] + i_c[hd] for hd in hds}
        m_new = {hd: jnp.maximum(total[hd] + m_prev[hd], jnp.max(g_r[hd], axis=-1, keepdims=True))
                 for hd in hds}
        kw = {hd: jnp.exp(g_c[hd] - m_new[hd]) * k_h[hd] for hd in hds}
        decay = {hd: jnp.exp(total[hd] + m_prev[hd] - m_new[hd]) for hd in hds}
        pump()
        upd = {hd: _dot(kw[hd].T.astype(BF16), v_h[hd]) for hd in hds}
        for hd in hds:
            c_ref[hd] = decay[hd] * c_prev[hd] + upd[hd]
            n_ref[hd:hd + 1, :] = decay[hd] * n_prev[hd] + jnp.sum(kw[hd], axis=0, keepdims=True)
            m_out = jnp.where(lane_1 == hd, m_new[hd], m_out)
        pump()
    m_ref[...] = m_out


def _prompt_mask_constants():
    r = np.arange(WINDOW)[:, None]
    c = np.arange(2 * WINDOW)[None, :]
    band = (c > r) & (c <= r + WINDOW)
    band_first = band & (c >= WINDOW)
    band_bias = np.where(np.stack([band, band_first]), 0.0, NEG).astype(np.float32)
    tril = (np.arange(WINDOW)[:, None] >= np.arange(WINDOW)[None, :]).astype(np.float32)
    tri = np.arange(MLSTM_CHUNK)[:, None] >= np.arange(MLSTM_CHUNK)[None, :]
    return (jnp.asarray(tril), jnp.asarray(band_bias), jnp.asarray(tri, dtype=BF16),
            jnp.asarray(np.where(tri, 0.0, NEG).astype(np.float32)))


N_MIX_PARAMS = 13
MLSTM_HEAD_GROUPS = ((0,), (1,), (2,), (3,))
MIX_PUMP_CALLS = 55
TAIL_FILL_PIECES = 8
MXU_PIECE_COLS = 256


class _Interleaver:
    def __init__(self, pieces, calls, hold_back=0):
        self._pieces = list(pieces)
        self._hold_back = hold_back
        self._spread = len(self._pieces) - hold_back
        self._emitted = 0
        self._calls = calls
        self._call = 0

    def __call__(self):
        self._call += 1
        target = (self._call * self._spread) // self._calls
        while self._emitted < target:
            self._pieces.pop(0)()
            self._emitted += 1

    def finish(self):
        assert self._call == self._calls and len(self._pieces) == self._hold_back, self._call
        return self._pieces


def _gate_pieces(h_ref, wmg_ref, bmg_ref, g_ref):
    def piece(off):
        cols = slice(off, off + MXU_PIECE_COLS)
        def run():
            g_ref[:, cols] = _sigmoid(_dot(h_ref[...], wmg_ref[:, cols]) + bmg_ref[:, cols])
        return run
    return [piece(off) for off in range(0, 3 * D_MODEL, MXU_PIECE_COLS)]


def _merge_and_project(x, mod_ref, g_ref, y_ref, wa_ref, wb_ref, wc_ref, wo_ref, fillers=()):
    fillers = list(fillers)
    per_stage = -(-len(fillers) // 4)
    merged = None
    for i, wbr_ref in enumerate((wa_ref, wb_ref, wc_ref)):
        for piece in fillers[i * per_stage:(i + 1) * per_stage]:
            piece()
        term = (g_ref[:, i * D_MODEL:(i + 1) * D_MODEL]
                * _dot(y_ref[:, i * A_WIDTH:(i + 1) * A_WIDTH], wbr_ref[...]))
        merged = term if merged is None else merged + term
    for piece in fillers[3 * per_stage:]:
        piece()
    ada_gate = mod_ref[:, 2 * D_MODEL:3 * D_MODEL]
    return x + ada_gate * _dot(merged.astype(BF16), wo_ref[...])


def _prompt_layer_kernel(tiles_per_seq, sink_ref, x2_ref, xn_ref, mod_ref, modn_ref, ng_ref,
                         wcat_ref, bcat_ref, *rest):
    mix_params = rest[:N_MIX_PARAMS]
    wmg_ref, bmg_ref, wa_ref, wb_ref, wc_ref, wo_ref = rest[N_MIX_PARAMS:N_MIX_PARAMS + 6]
    o_ref, ko_ref, vo_ref, convo_ref, c_ref, n_ref, m_ref = rest[N_MIX_PARAMS + 6:N_MIX_PARAMS + 13]
    (za0, zb0, zc0, za1, zb1, zc1, h0, h1, y_scr, g_scr, kprev, vprev, xbuf) = rest[N_MIX_PARAMS + 13:]
    ts = PROMPT_TILE
    z = ((za0, zb0, zc0), (za1, zb1, zc1))
    h = (h0, h1)
    k = pl.program_id(0)
    seq_start = (k % (tiles_per_seq // 2)) == 0

    @pl.when(k == 0)
    def _():
        h0[...] = _modulated_norm(x2_ref[0:ts, :], mod_ref, ng_ref)
        for piece in _inproj_pieces(lambda: h0[...], wcat_ref, bcat_ref, *z[0], 512):
            piece()

    @pl.when(seq_start)
    def _():
        kprev[...] = jnp.zeros_like(kprev)
        vprev[...] = jnp.zeros_like(vprev)
        xbuf[0:SUBLANES, :] = jnp.zeros((SUBLANES, 2 * C_WIDTH), F32)
        c_ref[...] = jnp.zeros_like(c_ref)
        n_ref[...] = jnp.zeros_like(n_ref)
        m_ref[...] = jnp.zeros_like(m_ref)

    for half in range(2):
        cur, nxt = half, 1 - half
        rows = slice(half * ts, (half + 1) * ts)
        if half == 0:
            h[nxt][...] = _modulated_norm(x2_ref[ts:2 * ts, :], mod_ref, ng_ref)
        else:
            h[nxt][...] = _modulated_norm(xn_ref[...], modn_ref, ng_ref)
        get_h_next = functools.partial(lambda r: r[...], h[nxt])
        hold = TAIL_FILL_PIECES if half == 1 else 0
        proj = _inproj_pieces(get_h_next, wcat_ref, bcat_ref, *z[nxt], MXU_PIECE_COLS)
        pump = _Interleaver(
            proj[:len(proj) - hold] + _gate_pieces(h[cur], wmg_ref, bmg_ref, g_scr)
            + proj[len(proj) - hold:], MIX_PUMP_CALLS, hold_back=hold)
        _prompt_mix_kernel(sink_ref, *z[cur], *mix_params,
                           y_scr, ko_ref, vo_ref, convo_ref, c_ref, n_ref, m_ref, kprev, vprev, xbuf,
                           first_tile=seq_start if half == 0 else False, pump=pump)
        o_ref[rows, :] = _merge_and_project(x2_ref[rows, :], mod_ref, g_scr, y_scr,
                                            wa_ref, wb_ref, wc_ref, wo_ref, fillers=pump.finish())


def _prompt_layer_call(layer, x2, mod, lw, batch, seq):
    ts = PROMPT_TILE
    nt = seq // ts
    assert nt % 2 == 0
    last_tile = batch * nt - 1
    const2 = lambda k: (0, 0)
    const3 = lambda k: (0, 0, 0)
    per_b3 = lambda k: ((2 * k) // nt, 0, 0)
    next_tile = lambda k: jnp.minimum(2 * k + 2, last_tile)
    once = pl.Buffered(1)
    return pl.pallas_call(
        functools.partial(_prompt_layer_kernel, nt),
        grid=(batch * nt // 2,),
        in_specs=[
            pl.BlockSpec(memory_space=pltpu.SMEM),
            pl.BlockSpec((2 * ts, D_MODEL), lambda k: (k, 0)),
            pl.BlockSpec((ts, D_MODEL), lambda k: (next_tile(k), 0)),
            pl.BlockSpec((None, 1, 3 * D_MODEL), per_b3),
            pl.BlockSpec((None, 1, 3 * D_MODEL), lambda k: (next_tile(k) // nt, 0, 0)),
            pl.BlockSpec((1, D_MODEL), const2),
            _layer_weight_spec(layer, D_MODEL, ZCAT_W),
            pl.BlockSpec((1, ZCAT_W), const2),
            pl.BlockSpec((1, A_WIDTH), const2),
            pl.BlockSpec((A_GROUPS, WINDOW, WINDOW), const3),
            pl.BlockSpec((WINDOW, LANES), const2),
            pl.BlockSpec((1, B_WIDTH), const2),
            pl.BlockSpec((1, B_KV_WIDTH), const2),
            pl.BlockSpec((C_CONV, 2 * C_WIDTH), const2),
            pl.BlockSpec((1, 2 * C_WIDTH), const2),
            pl.BlockSpec((1, LANES), const2),
            pl.BlockSpec((1, C_WIDTH), const2),
            pl.BlockSpec((WINDOW, WINDOW), const2),
            pl.BlockSpec((2, WINDOW, 2 * WINDOW), const3),
            pl.BlockSpec((MLSTM_CHUNK, MLSTM_CHUNK), const2),
            pl.BlockSpec((MLSTM_CHUNK, MLSTM_CHUNK), const2),
            _layer_weight_spec(layer, D_MODEL, 3 * D_MODEL),
            pl.BlockSpec((1, 3 * D_MODEL), const2),
            pl.BlockSpec((A_WIDTH, D_MODEL), const2, pipeline_mode=once),
            pl.BlockSpec((B_WIDTH, D_MODEL), const2, pipeline_mode=once),
            pl.BlockSpec((C_WIDTH, D_MODEL), const2, pipeline_mode=once),
            pl.BlockSpec((D_MODEL, D_MODEL), const2, pipeline_mode=once),
        ],
        out_specs=[
            pl.BlockSpec((2 * ts, D_MODEL), lambda k: (k, 0)),
            pl.BlockSpec((None, WINDOW, B_KV_WIDTH), per_b3),
            pl.BlockSpec((None, WINDOW, B_KV_WIDTH), per_b3),
            pl.BlockSpec((None, SUBLANES, 2 * C_WIDTH), per_b3),
            pl.BlockSpec((None, C_HEADS, C_HEAD_DIM, C_HEAD_DIM), lambda k: ((2 * k) // nt, 0, 0, 0)),
            pl.BlockSpec((None, C_HEADS, C_HEAD_DIM), per_b3),
            pl.BlockSpec((None, 1, LANES), per_b3),
        ],
        out_shape=[
            jax.ShapeDtypeStruct((batch * seq, D_MODEL), F32),
            jax.ShapeDtypeStruct((batch, WINDOW, B_KV_WIDTH), F32),
            jax.ShapeDtypeStruct((batch, WINDOW, B_KV_WIDTH), F32),
            jax.ShapeDtypeStruct((batch, SUBLANES, 2 * C_WIDTH), F32),
            jax.ShapeDtypeStruct((batch, C_HEADS, C_HEAD_DIM, C_HEAD_DIM), F32),
            jax.ShapeDtypeStruct((batch, C_HEADS, C_HEAD_DIM), F32),
            jax.ShapeDtypeStruct((batch, 1, LANES), F32),
        ],
        scratch_shapes=(
            [pltpu.VMEM((ts, w), F32) for w in (ZA_W, ZB_W, ZC_W)] * 2
            + [pltpu.VMEM((ts, D_MODEL), BF16)] * 2
            + [pltpu.VMEM((ts, Y_W), BF16),
               pltpu.VMEM((ts, 3 * D_MODEL), F32),
               pltpu.VMEM((WINDOW, B_KV_WIDTH), F32),
               pltpu.VMEM((WINDOW, B_KV_WIDTH), F32),
               pltpu.VMEM((ts + SUBLANES, 2 * C_WIDTH), F32)]),
        compiler_params=pltpu.CompilerParams(
            dimension_semantics=("arbitrary",), vmem_limit_bytes=VMEM_LIMIT),
        name="prompt_layer",
    )(lw["sinks"], x2, x2, mod, mod, lw["ng"], lw["wcat"], lw["bcat"],
      lw["vg"], lw["gws"], lw["gbs_col"], lw["qg"], lw["kg"], lw["cw"], lw["cb"], lw["fb"], lw["hg"],
      *_prompt_mask_constants(),
      lw["wmg"], lw["bmg"], lw["wa"], lw["wb"], lw["wc"], lw["wo"])


def _sample_mix_kernel(sink_ref, za_ref, zb_ref, zc_ref, kc_ref, vc_ref, cs_ref, c0_ref, n0_ref,
                       m0_ref, vg_ref, gwb_ref, gbs_ref, qg_ref, kg_ref, cw_ref, cb_ref, fb_ref,
                       hg_ref,
                       y_ref, vrow_ref, ko_ref, vo_ref, convo_ref, c1_ref, n1_ref, m1_ref,
                       xbuf):
    nb = SAMPLE_NB
    t = SUBLANES
    rows = nb * t
    tok_r = lax.broadcasted_iota(jnp.int32, (rows, rows), 0)
    tok_c = lax.broadcasted_iota(jnp.int32, (rows, rows), 1)
    same_b = (tok_r // t) == (tok_c // t)
    causal_b = same_b & (tok_c <= tok_r)

    u = za_ref[:, 0:A_WIDTH]
    vn = _rms(za_ref[:, A_WIDTH:2 * A_WIDTH]) * vg_ref[...]
    sg = _silu(za_ref[:, 2 * A_WIDTH:3 * A_WIDTH])
    vrow_ref[...] = vn
    vnb = vn.astype(BF16)
    s_cols = []
    for gi in range(A_GROUPS):
        s_cols.append(_dot(gwb_ref[gi], vnb[:, gi * GROUP_DIM:(gi + 1) * GROUP_DIM])
                      + gbs_ref[:, gi:gi + 1])
    y_ref[:, 0:A_WIDTH] = (u * jnp.concatenate(s_cols, axis=1) * sg).astype(BF16)

    qn = _qk_norm(zb_ref[:, 0:B_WIDTH], qg_ref[...]) * (B_HEAD_DIM ** -0.5)
    kn = _qk_norm(zb_ref[:, B_WIDTH:B_WIDTH + B_KV_WIDTH], kg_ref[...])
    vv = zb_ref[:, B_WIDTH + B_KV_WIDTH:B_WIDTH + 2 * B_KV_WIDTH]
    sgb = _silu(zb_ref[:, B_WIDTH + 2 * B_KV_WIDTH:ZB_W])
    kn3 = kn.reshape(nb, t, B_KV_WIDTH)
    vv3 = vv.reshape(nb, t, B_KV_WIDTH)
    kcache = kc_ref[...]
    vcache = vc_ref[...]
    pad = jnp.zeros((nb, WINDOW - t, B_KV_WIDTH), F32)
    kall = jnp.concatenate([kcache, kn3, pad], axis=1).astype(BF16)
    vall = jnp.concatenate([vcache, vv3, pad], axis=1).astype(BF16)
    qp = jnp.concatenate([_place_q_head(qn, h, rows).reshape(nb, t, LANES) for h in range(B_HEADS)],
                         axis=1).astype(BF16)
    logits = lax.dot_general(qp, kall, (((2,), (2,)), ((0,), (0,))), preferred_element_type=F32)
    qrow = lax.broadcasted_iota(jnp.int32, (nb, B_HEADS * t, 2 * WINDOW), 1)
    kcol = lax.broadcasted_iota(jnp.int32, (nb, B_HEADS * t, 2 * WINDOW), 2)
    qt = qrow % t
    valid = ((kcol < WINDOW) & (kcol > qt)) | ((kcol >= WINDOW) & ((kcol - WINDOW) <= qt))
    hrow = lax.broadcasted_iota(jnp.int32, (B_HEADS * t, 1), 0) // t
    snk = jnp.zeros((B_HEADS * t, 1), F32)
    for h in range(B_HEADS):
        snk = jnp.where(hrow == h, sink_ref[h], snk)
    lg = jnp.where(valid, logits, NEG)
    mx = jnp.maximum(jnp.max(lg, axis=-1, keepdims=True), snk[None])
    p = jnp.exp(lg - mx)
    den = jnp.sum(p, axis=-1, keepdims=True) + jnp.exp(snk[None] - mx)
    pv = lax.dot_general(p.astype(BF16), vall, (((2,), (1,)), ((0,), (0,))),
                         preferred_element_type=F32) / den
    head_out = [pv[:, h * t:(h + 1) * t, :].reshape(rows, LANES) for h in range(B_HEADS)]
    yb = jnp.concatenate(
        [_merge_head_pair(head_out[2 * j], head_out[2 * j + 1], 2 * j, rows)
         for j in range(B_HEADS // 2)], axis=1)
    y_ref[:, A_WIDTH:A_WIDTH + B_WIDTH] = (yb * sgb).astype(BF16)
    ko_ref[...] = jnp.concatenate([kcache[:, t:, :], kn3], axis=1)
    vo_ref[...] = jnp.concatenate([vcache[:, t:, :], vv3], axis=1)

    xbuf[:, SUBLANES - (C_CONV - 1):SUBLANES, :] = cs_ref[...]
    xbuf[:, SUBLANES:2 * SUBLANES, :] = zc_ref[:, 0:2 * C_WIDTH].reshape(nb, t, 2 * C_WIDTH)
    y3 = cb_ref[...][None]
    for j in range(C_CONV):
        lo = SUBLANES - (C_CONV - 1) + j
        y3 = y3 + cw_ref[j:j + 1, :][None] * xbuf[:, lo:lo + t, :]
    convo_ref[...] = xbuf[:, 2 * SUBLANES - (C_CONV - 1):2 * SUBLANES, :]
    qk = _silu(y3.reshape(rows, 2 * C_WIDTH))
    qall = qk[:, 0:C_WIDTH].astype(BF16)
    kall_c = qk[:, C_WIDTH:2 * C_WIDTH] * (C_HEAD_DIM ** -0.5)
    vall_c = zc_ref[:, 2 * C_WIDTH:3 * C_WIDTH].astype(BF16)
    gate_o = _sigmoid(zc_ref[:, 3 * C_WIDTH:4 * C_WIDTH]) * _silu(zc_ref[:, 4 * C_WIDTH:5 * C_WIDTH])
    ifp = zc_ref[:, 5 * C_WIDTH:5 * C_WIDTH + LANES]
    lf = _log_sigmoid(ifp + fb_ref[...])
    lane_t = lax.broadcasted_iota(jnp.int32, (rows, LANES), 1)
    cum_all = _dot_exact01(jnp.where(causal_b, 1.0, 0.0).astype(BF16), lf)
    tot_all = _dot_exact01(jnp.where(same_b, 1.0, 0.0).astype(BF16), lf)
    st_col = jnp.where(lane_t < C_HEADS, ifp, cum_all)
    st_row = st_col.T
    tot_row = tot_all.T
    m0 = m0_ref[...]
    same_b_bf = jnp.where(same_b, 1.0, 0.0).astype(BF16)
    batch_of_lane = lax.broadcasted_iota(jnp.int32, (nb, 1, rows), 2) // t
    batch_id = lax.broadcasted_iota(jnp.int32, (nb, 1, rows), 0)
    own_tok = batch_of_lane == batch_id
    h_cols = []
    m_out = jnp.zeros((rows, LANES), F32)
    for hd in range(C_HEADS):
        hs = slice(hd * C_HEAD_DIM, (hd + 1) * C_HEAD_DIM)
        i_c = st_col[:, hd:hd + 1]
        cum_c = st_col[:, C_HEADS + hd:C_HEADS + hd + 1]
        tot_c = tot_all[:, C_HEADS + hd:C_HEADS + hd + 1]
        i_r = st_row[hd:hd + 1, :]
        cum_r = st_row[C_HEADS + hd:C_HEADS + hd + 1, :]
        tot_r = tot_row[C_HEADS + hd:C_HEADS + hd + 1, :]
        m_prev = m0[:, hd:hd + 1]
        dmat = jnp.where(causal_b, cum_c - cum_r + i_r, NEG)
        m_inter = cum_c + m_prev
        m_t = jnp.maximum(m_inter, jnp.max(dmat, axis=-1, keepdims=True))
        q_h = qall[:, hs]
        k_h = kall_c[:, hs]
        v_h = vall_c[:, hs]
        a = jnp.exp(dmat - m_t) * _dot_nt(q_h, k_h.astype(BF16))
        w_inter = jnp.exp(m_inter - m_t)
        c_prev = c0_ref[:, hd]
        n_tok = jnp.broadcast_to(n0_ref[hd][:, None, :], (nb, t, C_HEAD_DIM)).reshape(rows, C_HEAD_DIM)
        inter = lax.dot_general(q_h.reshape(nb, t, C_HEAD_DIM), c_prev.astype(BF16),
                                (((2,), (1,)), ((0,), (0,))), preferred_element_type=F32)
        num = _dot(a.astype(BF16), v_h) + w_inter * inter.reshape(rows, C_HEAD_DIM)
        den = (jnp.sum(a, axis=-1, keepdims=True)
               + w_inter * jnp.sum(q_h.astype(F32) * n_tok, axis=-1, keepdims=True))
        hh = num / jnp.maximum(jnp.abs(den), jnp.exp(-m_t))
        h_cols.append(_rms(hh))
        g_r = tot_r - cum_r + i_r
        g_c = tot_c - cum_c + i_c
        m_new = jnp.maximum(tot_c + m_prev,
                            jnp.max(jnp.where(same_b, g_r, NEG), axis=-1, keepdims=True))
        kw = jnp.exp(g_c - m_new) * k_h
        decay = jnp.exp(tot_c + m_prev - m_new)
        kwt = kw.T
        lhs = jnp.where(own_tok, kwt[None], 0.0).astype(BF16).reshape(nb * C_HEAD_DIM, rows)
        upd = _dot(lhs, v_h).reshape(nb, C_HEAD_DIM, C_HEAD_DIM)
        dec_b = jnp.broadcast_to(decay, (rows, C_HEAD_DIM)).reshape(nb, t, C_HEAD_DIM)[:, 0:1, :]
        c1_ref[:, hd] = dec_b * c_prev + upd
        n1_ref[hd] = decay * n_tok + _dot(same_b_bf, kw.astype(BF16))
        m_out = jnp.where(lane_t == hd, m_new, m_out)
    m1_ref[...] = m_out
    hn = jnp.concatenate(h_cols, axis=1) * hg_ref[...]
    y_ref[:, A_WIDTH + B_WIDTH:Y_W] = (hn * gate_o).astype(BF16)


def _sample_mix_call(l, za, zb, zc, kc, vc, cs, c0, n0t, m0tok, lw, nbatch):
    nb = SAMPLE_NB
    t = SUBLANES
    rows = nb * t
    tok = lambda i: (i, 0)
    const2 = lambda i: (0, 0)
    const3 = lambda i: (0, 0, 0)
    b3 = lambda i: (i, 0, 0)
    lb4 = lambda i: (l, i, 0, 0)
    return pl.pallas_call(
        _sample_mix_kernel,
        grid=(nbatch // nb,),
        in_specs=[
            pl.BlockSpec(memory_space=pltpu.SMEM),
            pl.BlockSpec((rows, ZA_W), tok),
            pl.BlockSpec((rows, ZB_W), tok),
            pl.BlockSpec((rows, ZC_W), tok),
            pl.BlockSpec((None, nb, WINDOW, B_KV_WIDTH), lb4),
            pl.BlockSpec((None, nb, WINDOW, B_KV_WIDTH), lb4),
            pl.BlockSpec((None, nb, C_CONV - 1, 2 * C_WIDTH), lb4),
            pl.BlockSpec((None, nb, C_HEADS, C_HEAD_DIM, C_HEAD_DIM), lambda i: (l, i, 0, 0, 0)),
            pl.BlockSpec((None, C_HEADS, nb, C_HEAD_DIM), lambda i: (l, 0, i, 0)),
            pl.BlockSpec((None, rows, LANES), lambda i: (l, i, 0)),
            pl.BlockSpec((1, A_WIDTH), const2),
            pl.BlockSpec((A_GROUPS, rows, rows), const3),
            pl.BlockSpec((rows, LANES), const2),
            pl.BlockSpec((1, B_WIDTH), const2),
            pl.BlockSpec((1, B_KV_WIDTH), const2),
            pl.BlockSpec((C_CONV, 2 * C_WIDTH), const2),
            pl.BlockSpec((1, 2 * C_WIDTH), const2),
            pl.BlockSpec((1, LANES), const2),
            pl.BlockSpec((1, C_WIDTH), const2),
        ],
        out_specs=[
            pl.BlockSpec((rows, Y_W), tok),
            pl.BlockSpec((rows, A_WIDTH), tok),
            pl.BlockSpec((nb, WINDOW, B_KV_WIDTH), b3),
            pl.BlockSpec((nb, WINDOW, B_KV_WIDTH), b3),
            pl.BlockSpec((nb, C_CONV - 1, 2 * C_WIDTH), b3),
            pl.BlockSpec((nb, C_HEADS, C_HEAD_DIM, C_HEAD_DIM), lambda i: (i, 0, 0, 0)),
            pl.BlockSpec((C_HEADS, rows, C_HEAD_DIM), lambda i: (0, i, 0)),
            pl.BlockSpec((rows, LANES), tok),
        ],
        out_shape=[
            jax.ShapeDtypeStruct((nbatch * t, Y_W), BF16),
            jax.ShapeDtypeStruct((nbatch * t, A_WIDTH), F32),
            jax.ShapeDtypeStruct((nbatch, WINDOW, B_KV_WIDTH), F32),
            jax.ShapeDtypeStruct((nbatch, WINDOW, B_KV_WIDTH), F32),
            jax.ShapeDtypeStruct((nbatch, C_CONV - 1, 2 * C_WIDTH), F32),
            jax.ShapeDtypeStruct((nbatch, C_HEADS, C_HEAD_DIM, C_HEAD_DIM), F32),
            jax.ShapeDtypeStruct((C_HEADS, nbatch * t, C_HEAD_DIM), F32),
            jax.ShapeDtypeStruct((nbatch * t, LANES), F32),
        ],
        scratch_shapes=[pltpu.VMEM((nb, 2 * SUBLANES, 2 * C_WIDTH), F32)],
        compiler_params=pltpu.CompilerParams(
            dimension_semantics=("arbitrary",), vmem_limit_bytes=VMEM_LIMIT),
        name="sample_mixer",
    )(lw["sinks"], za, zb, zc, kc, vc, cs, c0, n0t, m0tok, lw["vg"], lw["gwb"], lw["gbs_tok"],
      lw["qg"], lw["kg"], lw["cw"], lw["cb"], lw["fb"], lw["hg"])


def _layer_weights(l, wcat_all, wmg_all, b_in, gmlp_vnorm_g, gmlp_ws, gmlp_bs, swa_qnorm_g,
                   swa_knorm_g, swa_sinks, mlstm_conv_w, mlstm_conv_b, mlstm_f_bias, mlstm_hnorm_g,
                   w_branch_a, w_branch_b, w_branch_c, w_out, norm_g, dec_seq):
    bl = b_in[l]
    bcat = jnp.concatenate([bl[:COL_CI], bl[COL_CO:COL_MG], bl[COL_CI:COL_CO],
                            jnp.zeros((LANES - 2 * C_HEADS,), F32)])
    t = dec_seq
    nb = SAMPLE_NB
    ws_t = gmlp_ws[l][:, :t, :t] * jnp.tril(jnp.ones((t, t), F32))
    eye = jnp.eye(nb, dtype=F32)
    gwb = jnp.einsum("bc,gts->gbtcs", eye, ws_t).reshape(A_GROUPS, nb * t, nb * t).astype(BF16)
    gbs_col = jnp.pad(gmlp_bs[l].T, ((0, 0), (0, LANES - A_GROUPS)))
    gbs_tok = jnp.pad(jnp.tile(gmlp_bs[l][:, :t].T, (nb, 1)), ((0, 0), (0, LANES - A_GROUPS)))
    fb = jnp.pad(mlstm_f_bias[l], (C_HEADS, LANES - 2 * C_HEADS)).reshape(1, LANES)
    return dict(
        ng=norm_g[l].reshape(1, D_MODEL),
        wcat=wcat_all, bcat=bcat.reshape(1, ZCAT_W),
        wmg=wmg_all, bmg=bl[COL_MG:].reshape(1, 3 * D_MODEL),
        wa=w_branch_a[l].astype(BF16), wb=w_branch_b[l].astype(BF16),
        wc=w_branch_c[l].astype(BF16), wo=w_out[l].astype(BF16),
        vg=gmlp_vnorm_g[l].reshape(1, A_WIDTH), gws=gmlp_ws[l], gwb=gwb,
        gbs_col=gbs_col, gbs_tok=gbs_tok,
        qg=jnp.tile(swa_qnorm_g[l], B_HEADS).reshape(1, B_WIDTH),
        kg=jnp.tile(swa_knorm_g[l], B_KV_HEADS).reshape(1, B_KV_WIDTH),
        sinks=swa_sinks[l],
        cw=mlstm_conv_w[l], cb=mlstm_conv_b[l].reshape(1, 2 * C_WIDTH), fb=fb,
        hg=mlstm_hnorm_g[l].reshape(1, C_WIDTH),
    )


def kernel(x_prompt, x_sample, cache_swa_k, cache_swa_v, state_mlstm_conv, state_mlstm_C, state_mlstm_n, state_mlstm_m, c_prompt, c_sample, ada_w, ada_b, norm_g, w_in, b_in, gmlp_vnorm_g, gmlp_ws, gmlp_bs, swa_qnorm_g, swa_knorm_g, swa_sinks, mlstm_conv_w, mlstm_conv_b, mlstm_f_bias, mlstm_hnorm_g, w_branch_a, w_branch_b, w_branch_c, w_out):
    batch, seq, _ = x_prompt.shape
    nbatch, dec_seq, _ = x_sample.shape
    assert dec_seq == SUBLANES and seq % PROMPT_TILE == 0 and nbatch % SAMPLE_NB == 0
    assert seq % PROJ_TILE == 0 and (nbatch * dec_seq) % PROJ_TILE == 0
    wb_len = cache_swa_k.shape[2]
    assert wb_len == WINDOW

    nc = batch + nbatch
    nc_pad = -(-nc // SUBLANES) * SUBLANES
    c_all = jnp.concatenate([c_prompt, c_sample, jnp.zeros((nc_pad - nc, D_MODEL), F32)], axis=0)
    mod_all = _ada_call(c_all, ada_w, ada_b)

    xp = x_prompt.reshape(batch * seq, D_MODEL)
    xs = x_sample.reshape(nbatch * dec_seq, D_MODEL)
    kc_all = cache_swa_k.reshape(DEPTH, nbatch, WINDOW, B_KV_WIDTH)
    vc_all = cache_swa_v.reshape(DEPTH, nbatch, WINDOW, B_KV_WIDTH)
    n0t_all = jnp.transpose(state_mlstm_n, (0, 2, 1, 3))
    m0tok_all = jnp.pad(jnp.repeat(state_mlstm_m, dec_seq, axis=1),
                        ((0, 0), (0, 0), (0, LANES - C_HEADS)))
    wcat_all, wmg_all = _weight_prep_call(w_in)
    outs_p = [[] for _ in range(6)]
    outs_s = [[] for _ in range(6)]
    vrows = []
    for l in range(DEPTH):
        lw = _layer_weights(l, wcat_all, wmg_all, b_in, gmlp_vnorm_g, gmlp_ws, gmlp_bs, swa_qnorm_g,
                            swa_knorm_g, swa_sinks, mlstm_conv_w, mlstm_conv_b, mlstm_f_bias,
                            mlstm_hnorm_g, w_branch_a, w_branch_b, w_branch_c, w_out, norm_g,
                            dec_seq)
        mod_p = mod_all[l, :batch].reshape(batch, 1, 3 * D_MODEL)
        mod_s = jnp.repeat(mod_all[l, batch:nc], dec_seq, axis=0)

        xp, ko, vo, convo, c1, n1, m1 = _prompt_layer_call(l, xp, mod_p, lw, batch, seq)
        outs_p[0].append(ko.reshape(batch, WINDOW, B_KV_HEADS, B_HEAD_DIM))
        outs_p[1].append(vo.reshape(batch, WINDOW, B_KV_HEADS, B_HEAD_DIM))
        outs_p[2].append(convo[:, SUBLANES - (C_CONV - 1):, :])
        outs_p[3].append(c1)
        outs_p[4].append(n1)
        outs_p[5].append(m1[:, 0, :C_HEADS])

        za, zb, zc = _inproj_call(l, xs, mod_s, lw["ng"], lw["wcat"], lw["bcat"], None)
        y, vrow, ko, vo, convo, c1, n1tok, m1tok = _sample_mix_call(
            l, za, zb, zc, kc_all, vc_all, state_mlstm_conv, state_mlstm_C, n0t_all, m0tok_all,
            lw, nbatch)
        xs = _outproj_call(l, xs, mod_s, lw["ng"], y, lw["wmg"], lw["bmg"], lw["wa"], lw["wb"],
                           lw["wc"], lw["wo"], None)
        outs_s[0].append(ko.reshape(nbatch, WINDOW, B_KV_HEADS, B_HEAD_DIM))
        outs_s[1].append(vo.reshape(nbatch, WINDOW, B_KV_HEADS, B_HEAD_DIM))
        outs_s[2].append(convo)
        outs_s[3].append(c1)
        outs_s[4].append(jnp.transpose(n1tok[:, ::dec_seq, :], (1, 0, 2)))
        outs_s[5].append(m1tok[::dec_seq, :C_HEADS])
        vrows.append(vrow.reshape(nbatch, dec_seq, A_WIDTH))

    sp = [jnp.stack(o) for o in outs_p]
    ss = [jnp.stack(o) for o in outs_s]
    return (xp.reshape(batch, seq, D_MODEL), xs.reshape(nbatch, dec_seq, D_MODEL),
            sp[0], sp[1], sp[2], sp[3], sp[4], sp[5],
            ss[0], ss[1], ss[2], ss[3], ss[4], ss[5], jnp.stack(vrows))
```

```python
import functools

import numpy as np
import jax
import jax.numpy as jnp
from jax import lax
from jax.experimental import pallas as pl
from jax.experimental.pallas import tpu as pltpu

F32 = jnp.float32
BF16 = jnp.bfloat16

D_MODEL = 1024
DEPTH = 2
A_WIDTH = 512
A_GROUPS = 4
GROUP_DIM = 128
B_HEADS = 8
B_KV_HEADS = 2
B_HEAD_DIM = 64
B_WIDTH = 512
B_KV_WIDTH = 128
WINDOW = 128
C_HEADS = 4
C_HEAD_DIM = 128
C_WIDTH = 512
C_CONV = 4
EPS = 1e-6
NEG = -1e30

LANES = 128
SUBLANES = 8
VMEM_LIMIT = 56 * 1024 * 1024

ZA_W = 3 * A_WIDTH
ZB_W = 2 * B_WIDTH + 2 * B_KV_WIDTH
ZC_W = 2 * C_WIDTH + 3 * C_WIDTH + LANES
ZCAT_W = ZA_W + ZB_W + ZC_W
Y_W = A_WIDTH + B_WIDTH + C_WIDTH

PROMPT_TILE = 256
MLSTM_CHUNK = PROMPT_TILE
SAMPLE_NB = 16
PROJ_TILE = 512


def _sigmoid(x):
    return 0.5 * jnp.tanh(0.5 * x) + 0.5


def _silu(x):
    t = 0.5 * x
    return t * (jnp.tanh(t) + 1.0)


def _log_sigmoid(x):
    return jnp.minimum(x, 0.0) - jnp.log1p(jnp.exp(-jnp.abs(x)))


def _rms(x):
    return x * lax.rsqrt(jnp.mean(x * x, axis=-1, keepdims=True) + EPS)


def _dot(a, b):
    return jnp.dot(a, b, preferred_element_type=F32)


def _dot_nt(a, b):
    return lax.dot_general(a, b, (((1,), (1,)), ((), ())), preferred_element_type=F32)


def _dot_exact01(m01, x):
    hi = x.astype(BF16)
    r1 = x - hi.astype(F32)
    mid = r1.astype(BF16)
    lo = (r1 - mid.astype(F32)).astype(BF16)
    return _dot(m01, hi) + _dot(m01, mid) + _dot(m01, lo)


def _modulated_norm(x, mod_ref, ng_ref):
    xn = _rms(x) * ng_ref[...]
    shift = mod_ref[:, 0:D_MODEL]
    scale = mod_ref[:, D_MODEL:2 * D_MODEL]
    return (xn * (1.0 + scale) + shift).astype(BF16)


def _head_rms_scale(x2, lane_lo):
    s0 = jnp.sum(jnp.where(lane_lo, x2, 0.0), axis=-1, keepdims=True)
    s1 = jnp.sum(jnp.where(lane_lo, 0.0, x2), axis=-1, keepdims=True)
    r0 = lax.rsqrt(s0 * (1.0 / B_HEAD_DIM) + EPS)
    r1 = lax.rsqrt(s1 * (1.0 / B_HEAD_DIM) + EPS)
    return jnp.where(lane_lo, r0, r1)


def _qk_norm(x, g_row):
    rows, width = x.shape
    lane_lo = lax.broadcasted_iota(jnp.int32, (rows, LANES), 1) < B_HEAD_DIM
    outs = []
    for j in range(width // LANES):
        slab = x[:, j * LANES:(j + 1) * LANES]
        outs.append(slab * _head_rms_scale(slab * slab, lane_lo))
    y = outs[0] if len(outs) == 1 else jnp.concatenate(outs, axis=1)
    return y * g_row


def _ada_kernel(c_ref, w_ref, b_ref, o_ref):
    c = c_ref[...]
    o_ref[...] = _dot(_silu(c).astype(BF16), w_ref[...].astype(BF16)) + b_ref[...]


def _ada_call(c_all, ada_w, ada_b):
    rows = c_all.shape[0]
    return pl.pallas_call(
        _ada_kernel,
        grid=(DEPTH, 3),
        in_specs=[
            pl.BlockSpec((rows, D_MODEL), lambda l, j: (0, 0)),
            pl.BlockSpec((None, D_MODEL, D_MODEL), lambda l, j: (l, 0, j)),
            pl.BlockSpec((None, 1, D_MODEL), lambda l, j: (l, 0, j)),
        ],
        out_specs=pl.BlockSpec((None, rows, D_MODEL), lambda l, j: (l, 0, j)),
        out_shape=jax.ShapeDtypeStruct((DEPTH, rows, 3 * D_MODEL), F32),
        compiler_params=pltpu.CompilerParams(
            dimension_semantics=("arbitrary", "arbitrary"), vmem_limit_bytes=VMEM_LIMIT),
        name="adaln_mod",
    )(c_all, ada_w, ada_b.reshape(DEPTH, 1, 3 * D_MODEL))


COL_CI = ZA_W + ZB_W + 3 * C_WIDTH
COL_CO = COL_CI + 2 * C_HEADS
COL_MG = COL_CO + 2 * C_WIDTH
PREP_CHUNK = 256
PREP_SHIFT = 2 * C_HEADS
N_MAIN = COL_CI // PREP_CHUNK
N_CO = (2 * C_WIDTH) // PREP_CHUNK
N_MG = (3 * D_MODEL) // PREP_CHUNK
J_CIF = N_MAIN + N_CO
J_MG = J_CIF + 1


def _weight_prep_kernel(wa_ref, wb_ref, wcat_ref, wmg_ref):
    j = pl.program_id(1)

    def shifted_t():
        rows = jnp.concatenate([wa_ref[PREP_SHIFT:PREP_CHUNK, :], wb_ref[...]], axis=0)
        return rows.astype(BF16).T

    @pl.when(j < N_MAIN)
    def _():
        wcat_ref[...] = wa_ref[...].astype(BF16).T

    @pl.when((j >= N_MAIN) & (j < J_CIF))
    def _():
        wcat_ref[...] = shifted_t()

    @pl.when(j == J_CIF)
    def _():
        row = lax.broadcasted_iota(jnp.int32, (PREP_CHUNK, D_MODEL), 0)
        wcat_ref[...] = jnp.where(row < PREP_SHIFT, wa_ref[...], 0.0).astype(BF16).T

    @pl.when(j >= J_MG)
    def _():
        wmg_ref[...] = shifted_t()


def _weight_prep_call(w_in):
    in_width = w_in.shape[-1]
    assert in_width == COL_MG + 3 * D_MODEL
    assert COL_CI % PREP_CHUNK == 0 and COL_CO % PREP_CHUNK == PREP_SHIFT == COL_MG % PREP_CHUNK
    w_t = jnp.swapaxes(w_in, 1, 2)
    assert in_width % PREP_SHIFT == 0 and PREP_SHIFT == SUBLANES
    last_rows = in_width // PREP_SHIFT - 1
    groups_per_chunk = PREP_CHUNK // PREP_SHIFT

    def src_block(j):
        return jnp.where(j < J_CIF, j, jnp.where(j == J_CIF, N_MAIN, j - 1))

    return pl.pallas_call(
        _weight_prep_kernel,
        grid=(DEPTH, J_MG + N_MG),
        in_specs=[
            pl.BlockSpec((None, PREP_CHUNK, D_MODEL), lambda l, j: (l, src_block(j), 0)),
            pl.BlockSpec((None, PREP_SHIFT, D_MODEL),
                         lambda l, j: (l, jnp.minimum((src_block(j) + 1) * groups_per_chunk,
                                                      last_rows), 0)),
        ],
        out_specs=[
            pl.BlockSpec((None, D_MODEL, PREP_CHUNK), lambda l, j: (l, 0, jnp.minimum(j, J_CIF))),
            pl.BlockSpec((None, D_MODEL, PREP_CHUNK), lambda l, j: (l, 0, jnp.maximum(j - J_MG, 0))),
        ],
        out_shape=[
            jax.ShapeDtypeStruct((DEPTH, D_MODEL, ZCAT_W), BF16),
            jax.ShapeDtypeStruct((DEPTH, D_MODEL, 3 * D_MODEL), BF16),
        ],
        compiler_params=pltpu.CompilerParams(
            dimension_semantics=("arbitrary", "arbitrary"), vmem_limit_bytes=VMEM_LIMIT),
        name="weight_prep",
    )(w_t, w_t)


def _col_chunks(width, step):
    return [(o, min(step, width - o)) for o in range(0, width, step)]


def _inproj_pieces(get_h, w_ref, b_ref, za_ref, zb_ref, zc_ref, step):
    def piece(o_ref, off, woff, w):
        def run():
            o_ref[:, off:off + w] = _dot(get_h(), w_ref[:, woff:woff + w]) + b_ref[:, woff:woff + w]
        return run
    pieces = []
    base = 0
    for o_ref, width in ((za_ref, ZA_W), (zb_ref, ZB_W), (zc_ref, ZC_W)):
        pieces += [piece(o_ref, off, base + off, w) for off, w in _col_chunks(width, step)]
        base += width
    return pieces


def _inproj_kernel(x_ref, mod_ref, ng_ref, w_ref, b_ref, za_ref, zb_ref, zc_ref):
    h = _modulated_norm(x_ref[...], mod_ref, ng_ref)
    for piece in _inproj_pieces(lambda: h, w_ref, b_ref, za_ref, zb_ref, zc_ref, 512):
        piece()


def _mod_spec(tm, tokens_per_batch):
    if tokens_per_batch is None:
        return pl.BlockSpec((tm, 3 * D_MODEL), lambda i: (i, 0))
    tiles_per_batch = tokens_per_batch // tm
    return pl.BlockSpec((None, 1, 3 * D_MODEL), lambda i: (i // tiles_per_batch, 0, 0))


def _layer_weight_spec(layer, rows, cols):
    return pl.BlockSpec((None, rows, cols), lambda i: (layer, 0, 0), pipeline_mode=pl.Buffered(1))


def _inproj_call(layer, x2, mod, ng, wcat, bcat, tokens_per_batch):
    ntok = x2.shape[0]
    tm = PROJ_TILE
    const = lambda i: (0, 0)
    return pl.pallas_call(
        _inproj_kernel,
        grid=(ntok // tm,),
        in_specs=[
            pl.BlockSpec((tm, D_MODEL), lambda i: (i, 0)),
            _mod_spec(tm, tokens_per_batch),
            pl.BlockSpec((1, D_MODEL), const),
            _layer_weight_spec(layer, D_MODEL, ZCAT_W),
            pl.BlockSpec((1, ZCAT_W), const),
        ],
        out_specs=[
            pl.BlockSpec((tm, ZA_W), lambda i: (i, 0)),
            pl.BlockSpec((tm, ZB_W), lambda i: (i, 0)),
            pl.BlockSpec((tm, ZC_W), lambda i: (i, 0)),
        ],
        out_shape=[
            jax.ShapeDtypeStruct((ntok, ZA_W), F32),
            jax.ShapeDtypeStruct((ntok, ZB_W), F32),
            jax.ShapeDtypeStruct((ntok, ZC_W), F32),
        ],
        compiler_params=pltpu.CompilerParams(
            dimension_semantics=("arbitrary",), vmem_limit_bytes=VMEM_LIMIT),
        name="in_projection",
    )(x2, mod, ng, wcat, bcat)


def _outproj_kernel(x_ref, mod_ref, ng_ref, y_ref, wmg_ref, bmg_ref, wa_ref, wb_ref, wc_ref,
                    wo_ref, o_ref):
    x = x_ref[...]
    h = _modulated_norm(x, mod_ref, ng_ref)
    merged = None
    for i, wbr_ref in enumerate((wa_ref, wb_ref, wc_ref)):
        cols = slice(i * D_MODEL, (i + 1) * D_MODEL)
        gate = _sigmoid(_dot(h, wmg_ref[:, cols]) + bmg_ref[:, cols])
        term = gate * _dot(y_ref[:, i * A_WIDTH:(i + 1) * A_WIDTH], wbr_ref[...])
        merged = term if merged is None else merged + term
    ada_gate = mod_ref[:, 2 * D_MODEL:3 * D_MODEL]
    o_ref[...] = x + ada_gate * _dot(merged.astype(BF16), wo_ref[...])


def _outproj_call(layer, x2, mod, ng, y, wmg, bmg, wa, wb, wc, wo, tokens_per_batch):
    ntok = x2.shape[0]
    tm = PROJ_TILE
    const = lambda i: (0, 0)
    once = pl.Buffered(1)
    return pl.pallas_call(
        _outproj_kernel,
        grid=(ntok // tm,),
        in_specs=[
            pl.BlockSpec((tm, D_MODEL), lambda i: (i, 0)),
            _mod_spec(tm, tokens_per_batch),
            pl.BlockSpec((1, D_MODEL), const),
            pl.BlockSpec((tm, Y_W), lambda i: (i, 0)),
            _layer_weight_spec(layer, D_MODEL, 3 * D_MODEL),
            pl.BlockSpec((1, 3 * D_MODEL), const),
            pl.BlockSpec((A_WIDTH, D_MODEL), const, pipeline_mode=once),
            pl.BlockSpec((B_WIDTH, D_MODEL), const, pipeline_mode=once),
            pl.BlockSpec((C_WIDTH, D_MODEL), const, pipeline_mode=once),
            pl.BlockSpec((D_MODEL, D_MODEL), const, pipeline_mode=once),
        ],
        out_specs=pl.BlockSpec((tm, D_MODEL), lambda i: (i, 0)),
        out_shape=jax.ShapeDtypeStruct((ntok, D_MODEL), F32),
        compiler_params=pltpu.CompilerParams(
            dimension_semantics=("arbitrary",), vmem_limit_bytes=VMEM_LIMIT),
        name="out_projection",
    )(x2, mod, ng, y, wmg, bmg, wa, wb, wc, wo)


def _place_q_head(qn, h, rows):
    lane = lax.broadcasted_iota(jnp.int32, (rows, LANES), 1)
    slab = qn[:, (h // 2) * LANES:(h // 2 + 1) * LANES]
    src_hi = h % 2
    dst_hi = h // (B_HEADS // B_KV_HEADS)
    keep = (lane >= B_HEAD_DIM) if src_hi else (lane < B_HEAD_DIM)
    slab = jnp.where(keep, slab, 0.0)
    if src_hi != dst_hi:
        slab = pltpu.roll(slab, B_HEAD_DIM, 1)
    return slab


def _merge_head_pair(o_even, o_odd, h_even, rows):
    lane_lo = lax.broadcasted_iota(jnp.int32, (rows, LANES), 1) < B_HEAD_DIM
    kv_hi = h_even // (B_HEADS // B_KV_HEADS)
    if kv_hi:
        o_even = pltpu.roll(o_even, B_HEAD_DIM, 1)
    else:
        o_odd = pltpu.roll(o_odd, B_HEAD_DIM, 1)
    return jnp.where(lane_lo, o_even, o_odd)


def _conv_taps(xbuf, cw_ref, cb_ref, cols, ts):
    y = cb_ref[:, cols]
    for j in range(C_CONV):
        lo = SUBLANES - (C_CONV - 1) + j
        y = y + cw_ref[j:j + 1, cols] * xbuf[lo:lo + ts, cols]
    return y


def _prompt_mix_kernel(sink_ref, za_ref, zb_ref, zc_ref, vg_ref, gw_ref, gbs_ref, qg_ref, kg_ref,
                       cw_ref, cb_ref, fb_ref, hg_ref, tril_ref, band_ref, tri01_ref, tribias_ref,
                       y_ref, ko_ref, vo_ref, convo_ref, c_ref, n_ref, m_ref,
                       kprev, vprev, xbuf, first_tile, pump):
    ts = PROMPT_TILE

    u = za_ref[:, 0:A_WIDTH]
    vn = _rms(za_ref[:, A_WIDTH:2 * A_WIDTH]) * vg_ref[...]
    sg = _silu(za_ref[:, 2 * A_WIDTH:3 * A_WIDTH])
    vnb = vn.astype(BF16)
    wts = [(gw_ref[gi] * tril_ref[...]).astype(BF16) for gi in range(A_GROUPS)]
    s_rows = []
    for c in range(ts // WINDOW):
        s_cols = []
        for gi in range(A_GROUPS):
            vblk = vnb[c * WINDOW:(c + 1) * WINDOW, gi * GROUP_DIM:(gi + 1) * GROUP_DIM]
            s_cols.append(_dot(wts[gi], vblk) + gbs_ref[:, gi:gi + 1])
        s_rows.append(jnp.concatenate(s_cols, axis=1))
    s = jnp.concatenate(s_rows, axis=0)
    pump()
    y_ref[:, 0:A_WIDTH] = (u * s * sg).astype(BF16)
    pump()

    qn = _qk_norm(zb_ref[:, 0:B_WIDTH], qg_ref[...]) * (B_HEAD_DIM ** -0.5)
    pump()
    kn = _qk_norm(zb_ref[:, B_WIDTH:B_WIDTH + B_KV_WIDTH], kg_ref[...])
    vv = zb_ref[:, B_WIDTH + B_KV_WIDTH:B_WIDTH + 2 * B_KV_WIDTH]
    pump()
    grp = B_HEADS // B_KV_HEADS
    nblk = ts // WINDOW
    lane_lo2 = lax.broadcasted_iota(jnp.int32, (2 * WINDOW, LANES), 1) < B_HEAD_DIM
    kblocks = [kprev[...]] + [kn[b * WINDOW:(b + 1) * WINDOW] for b in range(nblk)]
    vblocks = [vprev[...]] + [vv[b * WINDOW:(b + 1) * WINDOW] for b in range(nblk)]
    heads = [(kh, g) for kh in range(B_KV_HEADS) for g in range(grp)]
    snk = {k: sink_ref[k[0] * grp + k[1]] for k in heads}
    for blk in range(nblk):
        rows = slice(blk * WINDOW, (blk + 1) * WINDOW)
        if blk == 0 and first_tile is not False:
            bias = jnp.where(first_tile, band_ref[1], band_ref[0])
        else:
            bias = band_ref[0]
        kcat = jnp.concatenate([kblocks[blk], kblocks[blk + 1]], axis=0)
        vcat = jnp.concatenate([vblocks[blk], vblocks[blk + 1]], axis=0)
        krol = pltpu.roll(kcat, B_HEAD_DIM, 1)
        vrol = pltpu.roll(vcat, B_HEAD_DIM, 1)
        kdup, vdup = [], []
        for kh in range(B_KV_HEADS):
            own = lane_lo2 if kh == 0 else jnp.logical_not(lane_lo2)
            kdup.append(jnp.where(own, kcat, krol).astype(BF16))
            vdup.append(jnp.where(own, vcat, vrol).astype(BF16))
        pump()
        qs = [jnp.concatenate([_place_q_head(qn[rows], kh * grp + g, WINDOW) for g in range(grp)],
                              axis=0).astype(BF16) for kh in range(B_KV_HEADS)]
        logits = [_dot_nt(qs[kh], kdup[kh]) for kh in range(B_KV_HEADS)]
        pump()
        lg = {(kh, g): logits[kh][g * WINDOW:(g + 1) * WINDOW] + bias for kh, g in heads}
        mx = {k: jnp.maximum(jnp.max(lg[k], axis=-1, keepdims=True), snk[k]) for k in heads}
        pump()
        p = {k: jnp.exp(lg[k] - mx[k]) for k in heads}
        pump()
        rden = {k: 1.0 / (jnp.sum(p[k], axis=-1, keepdims=True) + jnp.exp(snk[k] - mx[k]))
                for k in heads}
        pump()
        pv = [_dot(jnp.concatenate([p[kh, g].astype(BF16) for g in range(grp)], axis=0), vdup[kh])
              for kh in range(B_KV_HEADS)]
        pump()
        outs = {(kh, g): pv[kh][g * WINDOW:(g + 1) * WINDOW] * rden[kh, g] for kh, g in heads}
        yb = jnp.concatenate(
            [_merge_head_pair(outs[(2 * j) // grp, (2 * j) % grp],
                              outs[(2 * j + 1) // grp, (2 * j + 1) % grp], 2 * j, WINDOW)
             for j in range(B_HEADS // 2)], axis=1)
        sgb = _silu(zb_ref[rows, B_WIDTH + 2 * B_KV_WIDTH:ZB_W])
        y_ref[rows, A_WIDTH:A_WIDTH + B_WIDTH] = (yb * sgb).astype(BF16)
        pump()
    kprev[...] = kblocks[nblk]
    vprev[...] = vblocks[nblk]
    ko_ref[...] = kblocks[nblk]
    vo_ref[...] = vblocks[nblk]
    pump()

    xbuf[SUBLANES:SUBLANES + ts, :] = zc_ref[:, 0:2 * C_WIDTH]
    qk = _silu(_conv_taps(xbuf, cw_ref, cb_ref, slice(0, 2 * C_WIDTH), ts))
    pump()
    tail = xbuf[ts:ts + SUBLANES, :]
    xbuf[0:SUBLANES, :] = tail
    convo_ref[...] = tail
    qall = qk[:, 0:C_WIDTH].astype(BF16)
    kall = qk[:, C_WIDTH:2 * C_WIDTH] * (C_HEAD_DIM ** -0.5)
    pump()
    ifp = zc_ref[:, 5 * C_WIDTH:5 * C_WIDTH + LANES]
    lf = _log_sigmoid(ifp + fb_ref[...])
    pump()
    cl = MLSTM_CHUNK
    lane_c = lax.broadcasted_iota(jnp.int32, (cl, LANES), 1)
    lane_1 = lax.broadcasted_iota(jnp.int32, (1, LANES), 1)
    m_row = m_ref[...]
    m_out = m_row
    cum_all = _dot_exact01(tri01_ref[...], lf)
    st_col = jnp.where(lane_c < C_HEADS, ifp, cum_all)
    st_row = st_col.T
    tribias = tribias_ref[...]
    pump()
    for hds in MLSTM_HEAD_GROUPS:
        hs = {hd: slice(hd * C_HEAD_DIM, (hd + 1) * C_HEAD_DIM) for hd in hds}
        i_c = {hd: st_col[:, hd:hd + 1] for hd in hds}
        cum_c = {hd: st_col[:, C_HEADS + hd:C_HEADS + hd + 1] for hd in hds}
        i_r = {hd: st_row[hd:hd + 1, :] for hd in hds}
        cum_r = {hd: st_row[C_HEADS + hd:C_HEADS + hd + 1, :] for hd in hds}
        m_prev = {hd: m_row[:, hd:hd + 1] for hd in hds}
        dmat = {hd: cum_c[hd] - cum_r[hd] + i_r[hd] + tribias for hd in hds}
        m_inter = {hd: cum_c[hd] + m_prev[hd] for hd in hds}
        m_t = {hd: jnp.maximum(m_inter[hd], jnp.max(dmat[hd], axis=-1, keepdims=True)) for hd in hds}
        pump()
        q_h = {hd: qall[:, hs[hd]] for hd in hds}
        k_h = {hd: kall[:, hs[hd]] for hd in hds}
        v_h = {hd: zc_ref[:, 2 * C_WIDTH + hd * C_HEAD_DIM:2 * C_WIDTH + (hd + 1) * C_HEAD_DIM].astype(BF16)
               for hd in hds}
        s_qk = {hd: _dot_nt(q_h[hd], k_h[hd].astype(BF16)) for hd in hds}
        a = {hd: jnp.exp(dmat[hd] - m_t[hd]) * s_qk[hd] for hd in hds}
        pump()
        w_inter = {hd: jnp.exp(m_inter[hd] - m_t[hd]) for hd in hds}
        c_prev = {hd: c_ref[hd] for hd in hds}
        n_prev = {hd: n_ref[hd:hd + 1, :] for hd in hds}
        inter = {hd: _dot(q_h[hd], c_prev[hd].astype(BF16)) for hd in hds}
        intra = {hd: _dot(a[hd].astype(BF16), v_h[hd]) for hd in hds}
        pump()
        den = {hd: jnp.sum(a[hd], axis=-1, keepdims=True)
               + w_inter[hd] * jnp.sum(q_h[hd].astype(F32) * n_prev[hd], axis=-1, keepdims=True)
               for hd in hds}
        rnorm = {hd: 1.0 / jnp.maximum(jnp.abs(den[hd]), jnp.exp(-m_t[hd])) for hd in hds}
        hh = {hd: (intra[hd] + w_inter[hd] * inter[hd]) * rnorm[hd] for hd in hds}
        pump()
        for hd in hds:
            o_cols = slice(3 * C_WIDTH + hd * C_HEAD_DIM, 3 * C_WIDTH + (hd + 1) * C_HEAD_DIM)
            g_cols = slice(4 * C_WIDTH + hd * C_HEAD_DIM, 4 * C_WIDTH + (hd + 1) * C_HEAD_DIM)
            gate_o = _sigmoid(zc_ref[:, o_cols]) * _silu(zc_ref[:, g_cols])
            y_cols = slice(A_WIDTH + B_WIDTH + hd * C_HEAD_DIM, A_WIDTH + B_WIDTH + (hd + 1) * C_HEAD_DIM)
            y_ref[:, y_cols] = (_rms(hh[hd]) * hg_ref[:, hs[hd]] * gate_o).astype(BF16)
        pump()
        total = {hd: cum_r[hd][:, cl - 1:cl] for hd in hds}
        g_r = {hd: total[hd] - cum_r[hd] + i_r[hd] for hd in hds}
        g_c = {hd: total[hd] - cum_c[hd] + i_c[hd] for hd in hds}
        m_new = {hd: jnp.maximum(total[hd] + m_prev[hd], jnp.max(g_r[hd], axis=-1, keepdims=True))
                 for hd in hds}
        kw = {hd: jnp.exp(g_c[hd] - m_new[hd]) * k_h[hd] for hd in hds}
        decay = {hd: jnp.exp(total[hd] + m_prev[hd] - m_new[hd]) for hd in hds}
        pump()
        upd = {hd: _dot(kw[hd].T.astype(BF16), v_h[hd]) for hd in hds}
        for hd in hds:
            c_ref[hd] = decay[hd] * c_prev[hd] + upd[hd]
            n_ref[hd:hd + 1, :] = decay[hd] * n_prev[hd] + jnp.sum(kw[hd], axis=0, keepdims=True)
            m_out = jnp.where(lane_1 == hd, m_new[hd], m_out)
        pump()
    m_ref[...] = m_out


def _prompt_mask_constants():
    r = np.arange(WINDOW)[:, None]
    c = np.arange(2 * WINDOW)[None, :]
    band = (c > r) & (c <= r + WINDOW)
    band_first = band & (c >= WINDOW)
    band_bias = np.where(np.stack([band, band_first]), 0.0, NEG).astype(np.float32)
    tril = (np.arange(WINDOW)[:, None] >= np.arange(WINDOW)[None, :]).astype(np.float32)
    tri = np.arange(MLSTM_CHUNK)[:, None] >= np.arange(MLSTM_CHUNK)[None, :]
    return (jnp.asarray(tril), jnp.asarray(band_bias), jnp.asarray(tri, dtype=BF16),
            jnp.asarray(np.where(tri, 0.0, NEG).astype(np.float32)))


N_MIX_PARAMS = 13
MLSTM_HEAD_GROUPS = ((0,), (1,), (2,), (3,))
MIX_PUMP_CALLS = 51
TAIL_FILL_PIECES = 8
MXU_PIECE_COLS = 256


class _Interleaver:
    def __init__(self, pieces, calls, hold_back=0):
        self._pieces = list(pieces)
        self._hold_back = hold_back
        self._spread = len(self._pieces) - hold_back
        self._emitted = 0
        self._calls = calls
        self._call = 0

    def __call__(self):
        self._call += 1
        target = (self._call * self._spread) // self._calls
        while self._emitted < target:
            self._pieces.pop(0)()
            self._emitted += 1

    def finish(self):
        assert self._call == self._calls and len(self._pieces) == self._hold_back, self._call
        return self._pieces


def _gate_pieces(h_ref, wmg_ref, bmg_ref, g_ref):
    def piece(off):
        cols = slice(off, off + MXU_PIECE_COLS)
        def run():
            g_ref[:, cols] = _sigmoid(_dot(h_ref[...], wmg_ref[:, cols]) + bmg_ref[:, cols])
        return run
    return [piece(off) for off in range(0, 3 * D_MODEL, MXU_PIECE_COLS)]


def _merge_and_project(x, mod_ref, g_ref, y_ref, wa_ref, wb_ref, wc_ref, wo_ref, fillers=()):
    fillers = list(fillers)
    per_stage = -(-len(fillers) // 4)
    merged = None
    for i, wbr_ref in enumerate((wa_ref, wb_ref, wc_ref)):
        for piece in fillers[i * per_stage:(i + 1) * per_stage]:
            piece()
        term = (g_ref[:, i * D_MODEL:(i + 1) * D_MODEL]
                * _dot(y_ref[:, i * A_WIDTH:(i + 1) * A_WIDTH], wbr_ref[...]))
        merged = term if merged is None else merged + term
    for piece in fillers[3 * per_stage:]:
        piece()
    ada_gate = mod_ref[:, 2 * D_MODEL:3 * D_MODEL]
    return x + ada_gate * _dot(merged.astype(BF16), wo_ref[...])


def _prompt_layer_kernel(tiles_per_seq, sink_ref, x2_ref, xn_ref, mod_ref, modn_ref, ng_ref,
                         wcat_ref, bcat_ref, *rest):
    mix_params = rest[:N_MIX_PARAMS]
    wmg_ref, bmg_ref, wa_ref, wb_ref, wc_ref, wo_ref = rest[N_MIX_PARAMS:N_MIX_PARAMS + 6]
    o_ref, ko_ref, vo_ref, convo_ref, c_ref, n_ref, m_ref = rest[N_MIX_PARAMS + 6:N_MIX_PARAMS + 13]
    (za0, zb0, zc0, za1, zb1, zc1, h0, h1, y_scr, g_scr, kprev, vprev, xbuf) = rest[N_MIX_PARAMS + 13:]
    ts = PROMPT_TILE
    z = ((za0, zb0, zc0), (za1, zb1, zc1))
    h = (h0, h1)
    k = pl.program_id(0)
    seq_start = (k % (tiles_per_seq // 2)) == 0

    @pl.when(k == 0)
    def _():
        h0[...] = _modulated_norm(x2_ref[0:ts, :], mod_ref, ng_ref)
        for piece in _inproj_pieces(lambda: h0[...], wcat_ref, bcat_ref, *z[0], 512):
            piece()

    @pl.when(seq_start)
    def _():
        kprev[...] = jnp.zeros_like(kprev)
        vprev[...] = jnp.zeros_like(vprev)
        xbuf[0:SUBLANES, :] = jnp.zeros((SUBLANES, 2 * C_WIDTH), F32)
        c_ref[...] = jnp.zeros_like(c_ref)
        n_ref[...] = jnp.zeros_like(n_ref)
        m_ref[...] = jnp.zeros_like(m_ref)

    for half in range(2):
        cur, nxt = half, 1 - half
        rows = slice(half * ts, (half + 1) * ts)
        if half == 0:
            h[nxt][...] = _modulated_norm(x2_ref[ts:2 * ts, :], mod_ref, ng_ref)
        else:
            h[nxt][...] = _modulated_norm(xn_ref[...], modn_ref, ng_ref)
        get_h_next = functools.partial(lambda r: r[...], h[nxt])
        hold = TAIL_FILL_PIECES if half == 1 else 0
        proj = _inproj_pieces(get_h_next, wcat_ref, bcat_ref, *z[nxt], MXU_PIECE_COLS)
        pump = _Interleaver(
            proj[:len(proj) - hold] + _gate_pieces(h[cur], wmg_ref, bmg_ref, g_scr)
            + proj[len(proj) - hold:], MIX_PUMP_CALLS, hold_back=hold)
        _prompt_mix_kernel(sink_ref, *z[cur], *mix_params,
                           y_scr, ko_ref, vo_ref, convo_ref, c_ref, n_ref, m_ref, kprev, vprev, xbuf,
                           first_tile=seq_start if half == 0 else False, pump=pump)
        o_ref[rows, :] = _merge_and_project(x2_ref[rows, :], mod_ref, g_scr, y_scr,
                                            wa_ref, wb_ref, wc_ref, wo_ref, fillers=pump.finish())


def _prompt_layer_call(layer, x2, mod, lw, batch, seq):
    ts = PROMPT_TILE
    nt = seq // ts
    assert nt % 2 == 0
    last_tile = batch * nt - 1
    const2 = lambda k: (0, 0)
    const3 = lambda k: (0, 0, 0)
    per_b3 = lambda k: ((2 * k) // nt, 0, 0)
    next_tile = lambda k: jnp.minimum(2 * k + 2, last_tile)
    once = pl.Buffered(1)
    return pl.pallas_call(
        functools.partial(_prompt_layer_kernel, nt),
        grid=(batch * nt // 2,),
        in_specs=[
            pl.BlockSpec(memory_space=pltpu.SMEM),
            pl.BlockSpec((2 * ts, D_MODEL), lambda k: (k, 0)),
            pl.BlockSpec((ts, D_MODEL), lambda k: (next_tile(k), 0)),
            pl.BlockSpec((None, 1, 3 * D_MODEL), per_b3),
            pl.BlockSpec((None, 1, 3 * D_MODEL), lambda k: (next_tile(k) // nt, 0, 0)),
            pl.BlockSpec((1, D_MODEL), const2),
            _layer_weight_spec(layer, D_MODEL, ZCAT_W),
            pl.BlockSpec((1, ZCAT_W), const2),
            pl.BlockSpec((1, A_WIDTH), const2),
            pl.BlockSpec((A_GROUPS, WINDOW, WINDOW), const3),
            pl.BlockSpec((WINDOW, LANES), const2),
            pl.BlockSpec((1, B_WIDTH), const2),
            pl.BlockSpec((1, B_KV_WIDTH), const2),
            pl.BlockSpec((C_CONV, 2 * C_WIDTH), const2),
            pl.BlockSpec((1, 2 * C_WIDTH), const2),
            pl.BlockSpec((1, LANES), const2),
            pl.BlockSpec((1, C_WIDTH), const2),
            pl.BlockSpec((WINDOW, WINDOW), const2),
            pl.BlockSpec((2, WINDOW, 2 * WINDOW), const3),
            pl.BlockSpec((MLSTM_CHUNK, MLSTM_CHUNK), const2),
            pl.BlockSpec((MLSTM_CHUNK, MLSTM_CHUNK), const2),
            _layer_weight_spec(layer, D_MODEL, 3 * D_MODEL),
            pl.BlockSpec((1, 3 * D_MODEL), const2),
            pl.BlockSpec((A_WIDTH, D_MODEL), const2, pipeline_mode=once),
            pl.BlockSpec((B_WIDTH, D_MODEL), const2, pipeline_mode=once),
            pl.BlockSpec((C_WIDTH, D_MODEL), const2, pipeline_mode=once),
            pl.BlockSpec((D_MODEL, D_MODEL), const2, pipeline_mode=once),
        ],
        out_specs=[
            pl.BlockSpec((2 * ts, D_MODEL), lambda k: (k, 0)),
            pl.BlockSpec((None, WINDOW, B_KV_WIDTH), per_b3),
            pl.BlockSpec((None, WINDOW, B_KV_WIDTH), per_b3),
            pl.BlockSpec((None, SUBLANES, 2 * C_WIDTH), per_b3),
            pl.BlockSpec((None, C_HEADS, C_HEAD_DIM, C_HEAD_DIM), lambda k: ((2 * k) // nt, 0, 0, 0)),
            pl.BlockSpec((None, C_HEADS, C_HEAD_DIM), per_b3),
            pl.BlockSpec((None, 1, LANES), per_b3),
        ],
        out_shape=[
            jax.ShapeDtypeStruct((batch * seq, D_MODEL), F32),
            jax.ShapeDtypeStruct((batch, WINDOW, B_KV_WIDTH), F32),
            jax.ShapeDtypeStruct((batch, WINDOW, B_KV_WIDTH), F32),
            jax.ShapeDtypeStruct((batch, SUBLANES, 2 * C_WIDTH), F32),
            jax.ShapeDtypeStruct((batch, C_HEADS, C_HEAD_DIM, C_HEAD_DIM), F32),
            jax.ShapeDtypeStruct((batch, C_HEADS, C_HEAD_DIM), F32),
            jax.ShapeDtypeStruct((batch, 1, LANES), F32),
        ],
        scratch_shapes=(
            [pltpu.VMEM((ts, w), F32) for w in (ZA_W, ZB_W, ZC_W)] * 2
            + [pltpu.VMEM((ts, D_MODEL), BF16)] * 2
            + [pltpu.VMEM((ts, Y_W), BF16),
               pltpu.VMEM((ts, 3 * D_MODEL), F32),
               pltpu.VMEM((WINDOW, B_KV_WIDTH), F32),
               pltpu.VMEM((WINDOW, B_KV_WIDTH), F32),
               pltpu.VMEM((ts + SUBLANES, 2 * C_WIDTH), F32)]),
        compiler_params=pltpu.CompilerParams(
            dimension_semantics=("arbitrary",), vmem_limit_bytes=VMEM_LIMIT),
        name="prompt_layer",
    )(lw["sinks"], x2, x2, mod, mod, lw["ng"], lw["wcat"], lw["bcat"],
      lw["vg"], lw["gws"], lw["gbs_col"], lw["qg"], lw["kg"], lw["cw"], lw["cb"], lw["fb"], lw["hg"],
      *_prompt_mask_constants(),
      lw["wmg"], lw["bmg"], lw["wa"], lw["wb"], lw["wc"], lw["wo"])


def _sample_mix_kernel(sink_ref, za_ref, zb_ref, zc_ref, kc_ref, vc_ref, cs_ref, c0_ref, n0_ref,
                       m0_ref, vg_ref, gwb_ref, gbs_ref, qg_ref, kg_ref, cw_ref, cb_ref, fb_ref,
                       hg_ref,
                       y_ref, vrow_ref, ko_ref, vo_ref, convo_ref, c1_ref, n1_ref, m1_ref,
                       xbuf):
    nb = SAMPLE_NB
    t = SUBLANES
    rows = nb * t
    tok_r = lax.broadcasted_iota(jnp.int32, (rows, rows), 0)
    tok_c = lax.broadcasted_iota(jnp.int32, (rows, rows), 1)
    same_b = (tok_r // t) == (tok_c // t)
    causal_b = same_b & (tok_c <= tok_r)

    u = za_ref[:, 0:A_WIDTH]
    vn = _rms(za_ref[:, A_WIDTH:2 * A_WIDTH]) * vg_ref[...]
    sg = _silu(za_ref[:, 2 * A_WIDTH:3 * A_WIDTH])
    vrow_ref[...] = vn
    vnb = vn.astype(BF16)
    s_cols = []
    for gi in range(A_GROUPS):
        s_cols.append(_dot(gwb_ref[gi], vnb[:, gi * GROUP_DIM:(gi + 1) * GROUP_DIM])
                      + gbs_ref[:, gi:gi + 1])
    y_ref[:, 0:A_WIDTH] = (u * jnp.concatenate(s_cols, axis=1) * sg).astype(BF16)

    qn = _qk_norm(zb_ref[:, 0:B_WIDTH], qg_ref[...]) * (B_HEAD_DIM ** -0.5)
    kn = _qk_norm(zb_ref[:, B_WIDTH:B_WIDTH + B_KV_WIDTH], kg_ref[...])
    vv = zb_ref[:, B_WIDTH + B_KV_WIDTH:B_WIDTH + 2 * B_KV_WIDTH]
    sgb = _silu(zb_ref[:, B_WIDTH + 2 * B_KV_WIDTH:ZB_W])
    kn3 = kn.reshape(nb, t, B_KV_WIDTH)
    vv3 = vv.reshape(nb, t, B_KV_WIDTH)
    kcache = kc_ref[...]
    vcache = vc_ref[...]
    pad = jnp.zeros((nb, WINDOW - t, B_KV_WIDTH), F32)
    kall = jnp.concatenate([kcache, kn3, pad], axis=1).astype(BF16)
    vall = jnp.concatenate([vcache, vv3, pad], axis=1).astype(BF16)
    qp = jnp.concatenate([_place_q_head(qn, h, rows).reshape(nb, t, LANES) for h in range(B_HEADS)],
                         axis=1).astype(BF16)
    logits = lax.dot_general(qp, kall, (((2,), (2,)), ((0,), (0,))), preferred_element_type=F32)
    qrow = lax.broadcasted_iota(jnp.int32, (nb, B_HEADS * t, 2 * WINDOW), 1)
    kcol = lax.broadcasted_iota(jnp.int32, (nb, B_HEADS * t, 2 * WINDOW), 2)
    qt = qrow % t
    valid = ((kcol < WINDOW) & (kcol > qt)) | ((kcol >= WINDOW) & ((kcol - WINDOW) <= qt))
    hrow = lax.broadcasted_iota(jnp.int32, (B_HEADS * t, 1), 0) // t
    snk = jnp.zeros((B_HEADS * t, 1), F32)
    for h in range(B_HEADS):
        snk = jnp.where(hrow == h, sink_ref[h], snk)
    lg = jnp.where(valid, logits, NEG)
    mx = jnp.maximum(jnp.max(lg, axis=-1, keepdims=True), snk[None])
    p = jnp.exp(lg - mx)
    den = jnp.sum(p, axis=-1, keepdims=True) + jnp.exp(snk[None] - mx)
    pv = lax.dot_general(p.astype(BF16), vall, (((2,), (1,)), ((0,), (0,))),
                         preferred_element_type=F32) / den
    head_out = [pv[:, h * t:(h + 1) * t, :].reshape(rows, LANES) for h in range(B_HEADS)]
    yb = jnp.concatenate(
        [_merge_head_pair(head_out[2 * j], head_out[2 * j + 1], 2 * j, rows)
         for j in range(B_HEADS // 2)], axis=1)
    y_ref[:, A_WIDTH:A_WIDTH + B_WIDTH] = (yb * sgb).astype(BF16)
    ko_ref[...] = jnp.concatenate([kcache[:, t:, :], kn3], axis=1)
    vo_ref[...] = jnp.concatenate([vcache[:, t:, :], vv3], axis=1)

    xbuf[:, SUBLANES - (C_CONV - 1):SUBLANES, :] = cs_ref[...]
    xbuf[:, SUBLANES:2 * SUBLANES, :] = zc_ref[:, 0:2 * C_WIDTH].reshape(nb, t, 2 * C_WIDTH)
    y3 = cb_ref[...][None]
    for j in range(C_CONV):
        lo = SUBLANES - (C_CONV - 1) + j
        y3 = y3 + cw_ref[j:j + 1, :][None] * xbuf[:, lo:lo + t, :]
    convo_ref[...] = xbuf[:, 2 * SUBLANES - (C_CONV - 1):2 * SUBLANES, :]
    qk = _silu(y3.reshape(rows, 2 * C_WIDTH))
    qall = qk[:, 0:C_WIDTH].astype(BF16)
    kall_c = qk[:, C_WIDTH:2 * C_WIDTH] * (C_HEAD_DIM ** -0.5)
    vall_c = zc_ref[:, 2 * C_WIDTH:3 * C_WIDTH].astype(BF16)
    gate_o = _sigmoid(zc_ref[:, 3 * C_WIDTH:4 * C_WIDTH]) * _silu(zc_ref[:, 4 * C_WIDTH:5 * C_WIDTH])
    ifp = zc_ref[:, 5 * C_WIDTH:5 * C_WIDTH + LANES]
    lf = _log_sigmoid(ifp + fb_ref[...])
    lane_t = lax.broadcasted_iota(jnp.int32, (rows, LANES), 1)
    cum_all = _dot_exact01(jnp.where(causal_b, 1.0, 0.0).astype(BF16), lf)
    tot_all = _dot_exact01(jnp.where(same_b, 1.0, 0.0).astype(BF16), lf)
    st_col = jnp.where(lane_t < C_HEADS, ifp, cum_all)
    st_row = st_col.T
    tot_row = tot_all.T
    m0 = m0_ref[...]
    same_b_bf = jnp.where(same_b, 1.0, 0.0).astype(BF16)
    batch_of_lane = lax.broadcasted_iota(jnp.int32, (nb, 1, rows), 2) // t
    batch_id = lax.broadcasted_iota(jnp.int32, (nb, 1, rows), 0)
    own_tok = batch_of_lane == batch_id
    h_cols = []
    m_out = jnp.zeros((rows, LANES), F32)
    for hd in range(C_HEADS):
        hs = slice(hd * C_HEAD_DIM, (hd + 1) * C_HEAD_DIM)
        i_c = st_col[:, hd:hd + 1]
        cum_c = st_col[:, C_HEADS + hd:C_HEADS + hd + 1]
        tot_c = tot_all[:, C_HEADS + hd:C_HEADS + hd + 1]
        i_r = st_row[hd:hd + 1, :]
        cum_r = st_row[C_HEADS + hd:C_HEADS + hd + 1, :]
        tot_r = tot_row[C_HEADS + hd:C_HEADS + hd + 1, :]
        m_prev = m0[:, hd:hd + 1]
        dmat = jnp.where(causal_b, cum_c - cum_r + i_r, NEG)
        m_inter = cum_c + m_prev
        m_t = jnp.maximum(m_inter, jnp.max(dmat, axis=-1, keepdims=True))
        q_h = qall[:, hs]
        k_h = kall_c[:, hs]
        v_h = vall_c[:, hs]
        a = jnp.exp(dmat - m_t) * _dot_nt(q_h, k_h.astype(BF16))
        w_inter = jnp.exp(m_inter - m_t)
        c_prev = c0_ref[:, hd]
        n_tok = jnp.broadcast_to(n0_ref[hd][:, None, :], (nb, t, C_HEAD_DIM)).reshape(rows, C_HEAD_DIM)
        inter = lax.dot_general(q_h.reshape(nb, t, C_HEAD_DIM), c_prev.astype(BF16),
                                (((2,), (1,)), ((0,), (0,))), preferred_element_type=F32)
        num = _dot(a.astype(BF16), v_h) + w_inter * inter.reshape(rows, C_HEAD_DIM)
        den = (jnp.sum(a, axis=-1, keepdims=True)
               + w_inter * jnp.sum(q_h.astype(F32) * n_tok, axis=-1, keepdims=True))
        hh = num / jnp.maximum(jnp.abs(den), jnp.exp(-m_t))
        h_cols.append(_rms(hh))
        g_r = tot_r - cum_r + i_r
        g_c = tot_c - cum_c + i_c
        m_new = jnp.maximum(tot_c + m_prev,
                            jnp.max(jnp.where(same_b, g_r, NEG), axis=-1, keepdims=True))
        kw = jnp.exp(g_c - m_new) * k_h
        decay = jnp.exp(tot_c + m_prev - m_new)
        kwt = kw.T
        lhs = jnp.where(own_tok, kwt[None], 0.0).astype(BF16).reshape(nb * C_HEAD_DIM, rows)
        upd = _dot(lhs, v_h).reshape(nb, C_HEAD_DIM, C_HEAD_DIM)
        dec_b = jnp.broadcast_to(decay, (rows, C_HEAD_DIM)).reshape(nb, t, C_HEAD_DIM)[:, 0:1, :]
        c1_ref[:, hd] = dec_b * c_prev + upd
        n1_ref[hd] = decay * n_tok + _dot(same_b_bf, kw.astype(BF16))
        m_out = jnp.where(lane_t == hd, m_new, m_out)
    m1_ref[...] = m_out
    hn = jnp.concatenate(h_cols, axis=1) * hg_ref[...]
    y_ref[:, A_WIDTH + B_WIDTH:Y_W] = (hn * gate_o).astype(BF16)


def _sample_mix_call(l, za, zb, zc, kc, vc, cs, c0, n0t, m0tok, lw, nbatch):
    nb = SAMPLE_NB
    t = SUBLANES
    rows = nb * t
    tok = lambda i: (i, 0)
    const2 = lambda i: (0, 0)
    const3 = lambda i: (0, 0, 0)
    b3 = lambda i: (i, 0, 0)
    lb4 = lambda i: (l, i, 0, 0)
    return pl.pallas_call(
        _sample_mix_kernel,
        grid=(nbatch // nb,),
        in_specs=[
            pl.BlockSpec(memory_space=pltpu.SMEM),
            pl.BlockSpec((rows, ZA_W), tok),
            pl.BlockSpec((rows, ZB_W), tok),
            pl.BlockSpec((rows, ZC_W), tok),
            pl.BlockSpec((None, nb, WINDOW, B_KV_WIDTH), lb4),
            pl.BlockSpec((None, nb, WINDOW, B_KV_WIDTH), lb4),
            pl.BlockSpec((None, nb, C_CONV - 1, 2 * C_WIDTH), lb4),
            pl.BlockSpec((None, nb, C_HEADS, C_HEAD_DIM, C_HEAD_DIM), lambda i: (l, i, 0, 0, 0)),
            pl.BlockSpec((None, C_HEADS, nb, C_HEAD_DIM), lambda i: (l, 0, i, 0)),
            pl.BlockSpec((None, rows, LANES), lambda i: (l, i, 0)),
            pl.BlockSpec((1, A_WIDTH), const2),
            pl.BlockSpec((A_GROUPS, rows, rows), const3),
            pl.BlockSpec((rows, LANES), const2),
            pl.BlockSpec((1, B_WIDTH), const2),
            pl.BlockSpec((1, B_KV_WIDTH), const2),
            pl.BlockSpec((C_CONV, 2 * C_WIDTH), const2),
            pl.BlockSpec((1, 2 * C_WIDTH), const2),
            pl.BlockSpec((1, LANES), const2),
            pl.BlockSpec((1, C_WIDTH), const2),
        ],
        out_specs=[
            pl.BlockSpec((rows, Y_W), tok),
            pl.BlockSpec((rows, A_WIDTH), tok),
            pl.BlockSpec((nb, WINDOW, B_KV_WIDTH), b3),
            pl.BlockSpec((nb, WINDOW, B_KV_WIDTH), b3),
            pl.BlockSpec((nb, C_CONV - 1, 2 * C_WIDTH), b3),
            pl.BlockSpec((nb, C_HEADS, C_HEAD_DIM, C_HEAD_DIM), lambda i: (i, 0, 0, 0)),
            pl.BlockSpec((C_HEADS, rows, C_HEAD_DIM), lambda i: (0, i, 0)),
            pl.BlockSpec((rows, LANES), tok),
        ],
        out_shape=[
            jax.ShapeDtypeStruct((nbatch * t, Y_W), BF16),
            jax.ShapeDtypeStruct((nbatch * t, A_WIDTH), F32),
            jax.ShapeDtypeStruct((nbatch, WINDOW, B_KV_WIDTH), F32),
            jax.ShapeDtypeStruct((nbatch, WINDOW, B_KV_WIDTH), F32),
            jax.ShapeDtypeStruct((nbatch, C_CONV - 1, 2 * C_WIDTH), F32),
            jax.ShapeDtypeStruct((nbatch, C_HEADS, C_HEAD_DIM, C_HEAD_DIM), F32),
            jax.ShapeDtypeStruct((C_HEADS, nbatch * t, C_HEAD_DIM), F32),
            jax.ShapeDtypeStruct((nbatch * t, LANES), F32),
        ],
        scratch_shapes=[pltpu.VMEM((nb, 2 * SUBLANES, 2 * C_WIDTH), F32)],
        compiler_params=pltpu.CompilerParams(
            dimension_semantics=("arbitrary",), vmem_limit_bytes=VMEM_LIMIT),
        name="sample_mixer",
    )(lw["sinks"], za, zb, zc, kc, vc, cs, c0, n0t, m0tok, lw["vg"], lw["gwb"], lw["gbs_tok"],
      lw["qg"], lw["kg"], lw["cw"], lw["cb"], lw["fb"], lw["hg"])


def _layer_weights(l, wcat_all, wmg_all, b_in, gmlp_vnorm_g, gmlp_ws, gmlp_bs, swa_qnorm_g,
                   swa_knorm_g, swa_sinks, mlstm_conv_w, mlstm_conv_b, mlstm_f_bias, mlstm_hnorm_g,
                   w_branch_a, w_branch_b, w_branch_c, w_out, norm_g, dec_seq):
    bl = b_in[l]
    bcat = jnp.concatenate([bl[:COL_CI], bl[COL_CO:COL_MG], bl[COL_CI:COL_CO],
                            jnp.zeros((LANES - 2 * C_HEADS,), F32)])
    t = dec_seq
    nb = SAMPLE_NB
    ws_t = gmlp_ws[l][:, :t, :t] * jnp.tril(jnp.ones((t, t), F32))
    eye = jnp.eye(nb, dtype=F32)
    gwb = jnp.einsum("bc,gts->gbtcs", eye, ws_t).reshape(A_GROUPS, nb * t, nb * t).astype(BF16)
    gbs_col = jnp.pad(gmlp_bs[l].T, ((0, 0), (0, LANES - A_GROUPS)))
    gbs_tok = jnp.pad(jnp.tile(gmlp_bs[l][:, :t].T, (nb, 1)), ((0, 0), (0, LANES - A_GROUPS)))
    fb = jnp.pad(mlstm_f_bias[l], (C_HEADS, LANES - 2 * C_HEADS)).reshape(1, LANES)
    return dict(
        ng=norm_g[l].reshape(1, D_MODEL),
        wcat=wcat_all, bcat=bcat.reshape(1, ZCAT_W),
        wmg=wmg_all, bmg=bl[COL_MG:].reshape(1, 3 * D_MODEL),
        wa=w_branch_a[l].astype(BF16), wb=w_branch_b[l].astype(BF16),
        wc=w_branch_c[l].astype(BF16), wo=w_out[l].astype(BF16),
        vg=gmlp_vnorm_g[l].reshape(1, A_WIDTH), gws=gmlp_ws[l], gwb=gwb,
        gbs_col=gbs_col, gbs_tok=gbs_tok,
        qg=jnp.tile(swa_qnorm_g[l], B_HEADS).reshape(1, B_WIDTH),
        kg=jnp.tile(swa_knorm_g[l], B_KV_HEADS).reshape(1, B_KV_WIDTH),
        sinks=swa_sinks[l],
        cw=mlstm_conv_w[l], cb=mlstm_conv_b[l].reshape(1, 2 * C_WIDTH), fb=fb,
        hg=mlstm_hnorm_g[l].reshape(1, C_WIDTH),
    )


def kernel(x_prompt, x_sample, cache_swa_k, cache_swa_v, state_mlstm_conv, state_mlstm_C, state_mlstm_n, state_mlstm_m, c_prompt, c_sample, ada_w, ada_b, norm_g, w_in, b_in, gmlp_vnorm_g, gmlp_ws, gmlp_bs, swa_qnorm_g, swa_knorm_g, swa_sinks, mlstm_conv_w, mlstm_conv_b, mlstm_f_bias, mlstm_hnorm_g, w_branch_a, w_branch_b, w_branch_c, w_out):
    batch, seq, _ = x_prompt.shape
    nbatch, dec_seq, _ = x_sample.shape
    assert dec_seq == SUBLANES and seq % PROMPT_TILE == 0 and nbatch % SAMPLE_NB == 0
    assert seq % PROJ_TILE == 0 and (nbatch * dec_seq) % PROJ_TILE == 0
    wb_len = cache_swa_k.shape[2]
    assert wb_len == WINDOW

    nc = batch + nbatch
    nc_pad = -(-nc // SUBLANES) * SUBLANES
    c_all = jnp.concatenate([c_prompt, c_sample, jnp.zeros((nc_pad - nc, D_MODEL), F32)], axis=0)
    mod_all = _ada_call(c_all, ada_w, ada_b)

    xp = x_prompt.reshape(batch * seq, D_MODEL)
    xs = x_sample.reshape(nbatch * dec_seq, D_MODEL)
    kc_all = cache_swa_k.reshape(DEPTH, nbatch, WINDOW, B_KV_WIDTH)
    vc_all = cache_swa_v.reshape(DEPTH, nbatch, WINDOW, B_KV_WIDTH)
    n0t_all = jnp.transpose(state_mlstm_n, (0, 2, 1, 3))
    m0tok_all = jnp.pad(jnp.repeat(state_mlstm_m, dec_seq, axis=1),
                        ((0, 0), (0, 0), (0, LANES - C_HEADS)))
    wcat_all, wmg_all = _weight_prep_call(w_in)
    outs_p = [[] for _ in range(6)]
    outs_s = [[] for _ in range(6)]
    vrows = []
    for l in range(DEPTH):
        lw = _layer_weights(l, wcat_all, wmg_all, b_in, gmlp_vnorm_g, gmlp_ws, gmlp_bs, swa_qnorm_g,
                            swa_knorm_g, swa_sinks, mlstm_conv_w, mlstm_conv_b, mlstm_f_bias,
                            mlstm_hnorm_g, w_branch_a, w_branch_b, w_branch_c, w_out, norm_g,
                            dec_seq)
        mod_p = mod_all[l, :batch].reshape(batch, 1, 3 * D_MODEL)
        mod_s = jnp.repeat(mod_all[l, batch:nc], dec_seq, axis=0)

        xp, ko, vo, convo, c1, n1, m1 = _prompt_layer_call(l, xp, mod_p, lw, batch, seq)
        outs_p[0].append(ko.reshape(batch, WINDOW, B_KV_HEADS, B_HEAD_DIM))
        outs_p[1].append(vo.reshape(batch, WINDOW, B_KV_HEADS, B_HEAD_DIM))
        outs_p[2].append(convo[:, SUBLANES - (C_CONV - 1):, :])
        outs_p[3].append(c1)
        outs_p[4].append(n1)
        outs_p[5].append(m1[:, 0, :C_HEADS])

        za, zb, zc = _inproj_call(l, xs, mod_s, lw["ng"], lw["wcat"], lw["bcat"], None)
        y, vrow, ko, vo, convo, c1, n1tok, m1tok = _sample_mix_call(
            l, za, zb, zc, kc_all, vc_all, state_mlstm_conv, state_mlstm_C, n0t_all, m0tok_all,
            lw, nbatch)
        xs = _outproj_call(l, xs, mod_s, lw["ng"], y, lw["wmg"], lw["bmg"], lw["wa"], lw["wb"],
                           lw["wc"], lw["wo"], None)
        outs_s[0].append(ko.reshape(nbatch, WINDOW, B_KV_HEADS, B_HEAD_DIM))
        outs_s[1].append(vo.reshape(nbatch, WINDOW, B_KV_HEADS, B_HEAD_DIM))
        outs_s[2].append(convo)
        outs_s[3].append(c1)
        outs_s[4].append(jnp.transpose(n1tok[:, ::dec_seq, :], (1, 0, 2)))
        outs_s[5].append(m1tok[::dec_seq, :C_HEADS])
        vrows.append(vrow.reshape(nbatch, dec_seq, A_WIDTH))

    sp = [jnp.stack(o) for o in outs_p]
    ss = [jnp.stack(o) for o in outs_s]
    return (xp.reshape(batch, seq, D_MODEL), xs.reshape(nbatch, dec_seq, D_MODEL),
            sp[0], sp[1], sp[2], sp[3], sp[4], sp[5],
            ss[0], ss[1], ss[2], ss[3], ss[4], ss[5], jnp.stack(vrows))
```

```python
import functools

import numpy as np
import jax
import jax.numpy as jnp
from jax import lax
from jax.experimental import pallas as pl
from jax.experimental.pallas import tpu as pltpu

F32 = jnp.float32
BF16 = jnp.bfloat16

D_MODEL = 1024
DEPTH = 2
A_WIDTH = 512
A_GROUPS = 4
GROUP_DIM = 128
B_HEADS = 8
B_KV_HEADS = 2
B_HEAD_DIM = 64
B_WIDTH = 512
B_KV_WIDTH = 128
WINDOW = 128
C_HEADS = 4
C_HEAD_DIM = 128
C_WIDTH = 512
C_CONV = 4
EPS = 1e-6
NEG = -1e30

LANES = 128
SUBLANES = 8
VMEM_LIMIT = 56 * 1024 * 1024

ZA_W = 3 * A_WIDTH
ZB_W = 2 * B_WIDTH + 2 * B_KV_WIDTH
ZC_W = 2 * C_WIDTH + 3 * C_WIDTH + LANES
ZCAT_W = ZA_W + ZB_W + ZC_W
Y_W = A_WIDTH + B_WIDTH + C_WIDTH

PROMPT_TILE = 256
MLSTM_CHUNK = PROMPT_TILE
SAMPLE_NB = 16
PROJ_TILE = 512


def _sigmoid(x):
    return 0.5 * jnp.tanh(0.5 * x) + 0.5


def _silu(x):
    t = 0.5 * x
    return t * (jnp.tanh(t) + 1.0)


def _log_sigmoid(x):
    return jnp.minimum(x, 0.0) - jnp.log1p(jnp.exp(-jnp.abs(x)))


def _rms(x):
    return x * lax.rsqrt(jnp.mean(x * x, axis=-1, keepdims=True) + EPS)


def _dot(a, b):
    return jnp.dot(a, b, preferred_element_type=F32)


def _dot_nt(a, b):
    return lax.dot_general(a, b, (((1,), (1,)), ((), ())), preferred_element_type=F32)


def _dot_exact01(m01, x):
    hi = x.astype(BF16)
    r1 = x - hi.astype(F32)
    mid = r1.astype(BF16)
    lo = (r1 - mid.astype(F32)).astype(BF16)
    return _dot(m01, hi) + _dot(m01, mid) + _dot(m01, lo)


def _modulated_norm(x, mod_ref, ng_ref):
    xn = _rms(x) * ng_ref[...]
    shift = mod_ref[:, 0:D_MODEL]
    scale = mod_ref[:, D_MODEL:2 * D_MODEL]
    return (xn * (1.0 + scale) + shift).astype(BF16)


def _head_rms_scale(x2, lane_lo):
    s0 = jnp.sum(jnp.where(lane_lo, x2, 0.0), axis=-1, keepdims=True)
    s1 = jnp.sum(jnp.where(lane_lo, 0.0, x2), axis=-1, keepdims=True)
    r0 = lax.rsqrt(s0 * (1.0 / B_HEAD_DIM) + EPS)
    r1 = lax.rsqrt(s1 * (1.0 / B_HEAD_DIM) + EPS)
    return jnp.where(lane_lo, r0, r1)


def _qk_norm(x, g_row):
    rows, width = x.shape
    lane_lo = lax.broadcasted_iota(jnp.int32, (rows, LANES), 1) < B_HEAD_DIM
    outs = []
    for j in range(width // LANES):
        slab = x[:, j * LANES:(j + 1) * LANES]
        outs.append(slab * _head_rms_scale(slab * slab, lane_lo))
    y = outs[0] if len(outs) == 1 else jnp.concatenate(outs, axis=1)
    return y * g_row


def _ada_kernel(c_ref, w_ref, b_ref, o_ref):
    c = c_ref[...]
    o_ref[...] = _dot(_silu(c).astype(BF16), w_ref[...].astype(BF16)) + b_ref[...]


def _ada_call(c_all, ada_w, ada_b):
    rows = c_all.shape[0]
    return pl.pallas_call(
        _ada_kernel,
        grid=(DEPTH, 3),
        in_specs=[
            pl.BlockSpec((rows, D_MODEL), lambda l, j: (0, 0)),
            pl.BlockSpec((None, D_MODEL, D_MODEL), lambda l, j: (l, 0, j)),
            pl.BlockSpec((None, 1, D_MODEL), lambda l, j: (l, 0, j)),
        ],
        out_specs=pl.BlockSpec((None, rows, D_MODEL), lambda l, j: (l, 0, j)),
        out_shape=jax.ShapeDtypeStruct((DEPTH, rows, 3 * D_MODEL), F32),
        compiler_params=pltpu.CompilerParams(
            dimension_semantics=("arbitrary", "arbitrary"), vmem_limit_bytes=VMEM_LIMIT),
        name="adaln_mod",
    )(c_all, ada_w, ada_b.reshape(DEPTH, 1, 3 * D_MODEL))


COL_CI = ZA_W + ZB_W + 3 * C_WIDTH
COL_CO = COL_CI + 2 * C_HEADS
COL_MG = COL_CO + 2 * C_WIDTH
PREP_CHUNK = 256
PREP_SHIFT = 2 * C_HEADS
N_MAIN = COL_CI // PREP_CHUNK
N_CO = (2 * C_WIDTH) // PREP_CHUNK
N_MG = (3 * D_MODEL) // PREP_CHUNK
J_CIF = N_MAIN + N_CO
J_MG = J_CIF + 1


def _weight_prep_kernel(wa_ref, wb_ref, wcat_ref, wmg_ref):
    j = pl.program_id(1)

    def shifted_t():
        rows = jnp.concatenate([wa_ref[PREP_SHIFT:PREP_CHUNK, :], wb_ref[...]], axis=0)
        return rows.astype(BF16).T

    @pl.when(j < N_MAIN)
    def _():
        wcat_ref[...] = wa_ref[...].astype(BF16).T

    @pl.when((j >= N_MAIN) & (j < J_CIF))
    def _():
        wcat_ref[...] = shifted_t()

    @pl.when(j == J_CIF)
    def _():
        row = lax.broadcasted_iota(jnp.int32, (PREP_CHUNK, D_MODEL), 0)
        wcat_ref[...] = jnp.where(row < PREP_SHIFT, wa_ref[...], 0.0).astype(BF16).T

    @pl.when(j >= J_MG)
    def _():
        wmg_ref[...] = shifted_t()


def _weight_prep_call(w_in):
    in_width = w_in.shape[-1]
    assert in_width == COL_MG + 3 * D_MODEL
    assert COL_CI % PREP_CHUNK == 0 and COL_CO % PREP_CHUNK == PREP_SHIFT == COL_MG % PREP_CHUNK
    w_t = jnp.swapaxes(w_in, 1, 2)
    assert in_width % PREP_SHIFT == 0 and PREP_SHIFT == SUBLANES
    last_rows = in_width // PREP_SHIFT - 1
    groups_per_chunk = PREP_CHUNK // PREP_SHIFT

    def src_block(j):
        return jnp.where(j < J_CIF, j, jnp.where(j == J_CIF, N_MAIN, j - 1))

    return pl.pallas_call(
        _weight_prep_kernel,
        grid=(DEPTH, J_MG + N_MG),
        in_specs=[
            pl.BlockSpec((None, PREP_CHUNK, D_MODEL), lambda l, j: (l, src_block(j), 0)),
            pl.BlockSpec((None, PREP_SHIFT, D_MODEL),
                         lambda l, j: (l, jnp.minimum((src_block(j) + 1) * groups_per_chunk,
                                                      last_rows), 0)),
        ],
        out_specs=[
            pl.BlockSpec((None, D_MODEL, PREP_CHUNK), lambda l, j: (l, 0, jnp.minimum(j, J_CIF))),
            pl.BlockSpec((None, D_MODEL, PREP_CHUNK), lambda l, j: (l, 0, jnp.maximum(j - J_MG, 0))),
        ],
        out_shape=[
            jax.ShapeDtypeStruct((DEPTH, D_MODEL, ZCAT_W), BF16),
            jax.ShapeDtypeStruct((DEPTH, D_MODEL, 3 * D_MODEL), BF16),
        ],
        compiler_params=pltpu.CompilerParams(
            dimension_semantics=("arbitrary", "arbitrary"), vmem_limit_bytes=VMEM_LIMIT),
        name="weight_prep",
    )(w_t, w_t)


def _col_chunks(width, step):
    return [(o, min(step, width - o)) for o in range(0, width, step)]


def _inproj_pieces(get_h, w_ref, b_ref, za_ref, zb_ref, zc_ref, step):
    def piece(o_ref, off, woff, w):
        def run():
            o_ref[:, off:off + w] = _dot(get_h(), w_ref[:, woff:woff + w]) + b_ref[:, woff:woff + w]
        return run
    pieces = []
    base = 0
    for o_ref, width in ((za_ref, ZA_W), (zb_ref, ZB_W), (zc_ref, ZC_W)):
        pieces += [piece(o_ref, off, base + off, w) for off, w in _col_chunks(width, step)]
        base += width
    return pieces


def _inproj_kernel(x_ref, mod_ref, ng_ref, w_ref, b_ref, za_ref, zb_ref, zc_ref):
    h = _modulated_norm(x_ref[...], mod_ref, ng_ref)
    for piece in _inproj_pieces(lambda: h, w_ref, b_ref, za_ref, zb_ref, zc_ref, 512):
        piece()


def _mod_spec(tm, tokens_per_batch):
    if tokens_per_batch is None:
        return pl.BlockSpec((tm, 3 * D_MODEL), lambda i: (i, 0))
    tiles_per_batch = tokens_per_batch // tm
    return pl.BlockSpec((None, 1, 3 * D_MODEL), lambda i: (i // tiles_per_batch, 0, 0))


def _layer_weight_spec(layer, rows, cols):
    return pl.BlockSpec((None, rows, cols), lambda i: (layer, 0, 0), pipeline_mode=pl.Buffered(1))


def _inproj_call(layer, x2, mod, ng, wcat, bcat, tokens_per_batch):
    ntok = x2.shape[0]
    tm = PROJ_TILE
    const = lambda i: (0, 0)
    return pl.pallas_call(
        _inproj_kernel,
        grid=(ntok // tm,),
        in_specs=[
            pl.BlockSpec((tm, D_MODEL), lambda i: (i, 0)),
            _mod_spec(tm, tokens_per_batch),
            pl.BlockSpec((1, D_MODEL), const),
            _layer_weight_spec(layer, D_MODEL, ZCAT_W),
            pl.BlockSpec((1, ZCAT_W), const),
        ],
        out_specs=[
            pl.BlockSpec((tm, ZA_W), lambda i: (i, 0)),
            pl.BlockSpec((tm, ZB_W), lambda i: (i, 0)),
            pl.BlockSpec((tm, ZC_W), lambda i: (i, 0)),
        ],
        out_shape=[
            jax.ShapeDtypeStruct((ntok, ZA_W), F32),
            jax.ShapeDtypeStruct((ntok, ZB_W), F32),
            jax.ShapeDtypeStruct((ntok, ZC_W), F32),
        ],
        compiler_params=pltpu.CompilerParams(
            dimension_semantics=("arbitrary",), vmem_limit_bytes=VMEM_LIMIT),
        name="in_projection",
    )(x2, mod, ng, wcat, bcat)


def _outproj_kernel(x_ref, mod_ref, ng_ref, y_ref, wmg_ref, bmg_ref, wa_ref, wb_ref, wc_ref,
                    wo_ref, o_ref):
    x = x_ref[...]
    h = _modulated_norm(x, mod_ref, ng_ref)
    merged = None
    for i, wbr_ref in enumerate((wa_ref, wb_ref, wc_ref)):
        cols = slice(i * D_MODEL, (i + 1) * D_MODEL)
        gate = _sigmoid(_dot(h, wmg_ref[:, cols]) + bmg_ref[:, cols])
        term = gate * _dot(y_ref[:, i * A_WIDTH:(i + 1) * A_WIDTH], wbr_ref[...])
        merged = term if merged is None else merged + term
    ada_gate = mod_ref[:, 2 * D_MODEL:3 * D_MODEL]
    o_ref[...] = x + ada_gate * _dot(merged.astype(BF16), wo_ref[...])


def _outproj_call(layer, x2, mod, ng, y, wmg, bmg, wa, wb, wc, wo, tokens_per_batch):
    ntok = x2.shape[0]
    tm = PROJ_TILE
    const = lambda i: (0, 0)
    once = pl.Buffered(1)
    return pl.pallas_call(
        _outproj_kernel,
        grid=(ntok // tm,),
        in_specs=[
            pl.BlockSpec((tm, D_MODEL), lambda i: (i, 0)),
            _mod_spec(tm, tokens_per_batch),
            pl.BlockSpec((1, D_MODEL), const),
            pl.BlockSpec((tm, Y_W), lambda i: (i, 0)),
            _layer_weight_spec(layer, D_MODEL, 3 * D_MODEL),
            pl.BlockSpec((1, 3 * D_MODEL), const),
            pl.BlockSpec((A_WIDTH, D_MODEL), const, pipeline_mode=once),
            pl.BlockSpec((B_WIDTH, D_MODEL), const, pipeline_mode=once),
            pl.BlockSpec((C_WIDTH, D_MODEL), const, pipeline_mode=once),
            pl.BlockSpec((D_MODEL, D_MODEL), const, pipeline_mode=once),
        ],
        out_specs=pl.BlockSpec((tm, D_MODEL), lambda i: (i, 0)),
        out_shape=jax.ShapeDtypeStruct((ntok, D_MODEL), F32),
        compiler_params=pltpu.CompilerParams(
            dimension_semantics=("arbitrary",), vmem_limit_bytes=VMEM_LIMIT),
        name="out_projection",
    )(x2, mod, ng, y, wmg, bmg, wa, wb, wc, wo)


def _place_q_head(qn, h, rows):
    lane = lax.broadcasted_iota(jnp.int32, (rows, LANES), 1)
    slab = qn[:, (h // 2) * LANES:(h // 2 + 1) * LANES]
    src_hi = h % 2
    dst_hi = h // (B_HEADS // B_KV_HEADS)
    keep = (lane >= B_HEAD_DIM) if src_hi else (lane < B_HEAD_DIM)
    slab = jnp.where(keep, slab, 0.0)
    if src_hi != dst_hi:
        slab = pltpu.roll(slab, B_HEAD_DIM, 1)
    return slab


def _merge_head_pair(o_even, o_odd, h_even, rows):
    lane_lo = lax.broadcasted_iota(jnp.int32, (rows, LANES), 1) < B_HEAD_DIM
    kv_hi = h_even // (B_HEADS // B_KV_HEADS)
    if kv_hi:
        o_even = pltpu.roll(o_even, B_HEAD_DIM, 1)
    else:
        o_odd = pltpu.roll(o_odd, B_HEAD_DIM, 1)
    return jnp.where(lane_lo, o_even, o_odd)


def _conv_taps(xbuf, cw_ref, cb_ref, cols, ts):
    y = cb_ref[:, cols]
    for j in range(C_CONV):
        lo = SUBLANES - (C_CONV - 1) + j
        y = y + cw_ref[j:j + 1, cols] * xbuf[lo:lo + ts, cols]
    return y


def _prompt_mix_kernel(sink_ref, za_ref, zb_ref, zc_ref, vg_ref, gw_ref, gbs_ref, qg_ref, kg_ref,
                       cw_ref, cb_ref, fb_ref, hg_ref, tril_ref, band_ref, tri01_ref, tribias_ref,
                       y_ref, ko_ref, vo_ref, convo_ref, c_ref, n_ref, m_ref,
                       kprev, vprev, xbuf, first_tile, pump):
    ts = PROMPT_TILE

    wts = [(gw_ref[gi] * tril_ref[...]).astype(BF16) for gi in range(A_GROUPS)]
    for c in range(ts // WINDOW):
        rows = slice(c * WINDOW, (c + 1) * WINDOW)
        vnb = (_rms(za_ref[rows, A_WIDTH:2 * A_WIDTH]) * vg_ref[...]).astype(BF16)
        s = jnp.concatenate(
            [_dot(wts[gi], vnb[:, gi * GROUP_DIM:(gi + 1) * GROUP_DIM]) + gbs_ref[:, gi:gi + 1]
             for gi in range(A_GROUPS)], axis=1)
        sg = _silu(za_ref[rows, 2 * A_WIDTH:3 * A_WIDTH])
        y_ref[rows, 0:A_WIDTH] = (za_ref[rows, 0:A_WIDTH] * s * sg).astype(BF16)
        pump()

    kn = _qk_norm(zb_ref[:, B_WIDTH:B_WIDTH + B_KV_WIDTH], kg_ref[...])
    vv = zb_ref[:, B_WIDTH + B_KV_WIDTH:B_WIDTH + 2 * B_KV_WIDTH]
    pump()
    grp = B_HEADS // B_KV_HEADS
    nblk = ts // WINDOW
    lane_lo2 = lax.broadcasted_iota(jnp.int32, (2 * WINDOW, LANES), 1) < B_HEAD_DIM
    kblocks = [kprev[...]] + [kn[b * WINDOW:(b + 1) * WINDOW] for b in range(nblk)]
    vblocks = [vprev[...]] + [vv[b * WINDOW:(b + 1) * WINDOW] for b in range(nblk)]
    heads = [(kh, g) for kh in range(B_KV_HEADS) for g in range(grp)]
    snk = {k: sink_ref[k[0] * grp + k[1]] for k in heads}
    for blk in range(nblk):
        rows = slice(blk * WINDOW, (blk + 1) * WINDOW)
        if blk == 0 and first_tile is not False:
            bias = jnp.where(first_tile, band_ref[1], band_ref[0])
        else:
            bias = band_ref[0]
        kcat = jnp.concatenate([kblocks[blk], kblocks[blk + 1]], axis=0)
        vcat = jnp.concatenate([vblocks[blk], vblocks[blk + 1]], axis=0)
        krol = pltpu.roll(kcat, B_HEAD_DIM, 1)
        vrol = pltpu.roll(vcat, B_HEAD_DIM, 1)
        kdup, vdup = [], []
        for kh in range(B_KV_HEADS):
            own = lane_lo2 if kh == 0 else jnp.logical_not(lane_lo2)
            kdup.append(jnp.where(own, kcat, krol).astype(BF16))
            vdup.append(jnp.where(own, vcat, vrol).astype(BF16))
        qn = _qk_norm(zb_ref[rows, 0:B_WIDTH], qg_ref[...]) * (B_HEAD_DIM ** -0.5)
        pump()
        qs = [jnp.concatenate([_place_q_head(qn, kh * grp + g, WINDOW) for g in range(grp)],
                              axis=0).astype(BF16) for kh in range(B_KV_HEADS)]
        logits = [_dot_nt(qs[kh], kdup[kh]) for kh in range(B_KV_HEADS)]
        pump()
        lg = {(kh, g): logits[kh][g * WINDOW:(g + 1) * WINDOW] + bias for kh, g in heads}
        mx = {k: jnp.maximum(jnp.max(lg[k], axis=-1, keepdims=True), snk[k]) for k in heads}
        pump()
        p = {k: jnp.exp(lg[k] - mx[k]) for k in heads}
        pump()
        rden = {k: 1.0 / (jnp.sum(p[k], axis=-1, keepdims=True) + jnp.exp(snk[k] - mx[k]))
                for k in heads}
        pump()
        pv = [_dot(jnp.concatenate([p[kh, g].astype(BF16) for g in range(grp)], axis=0), vdup[kh])
              for kh in range(B_KV_HEADS)]
        pump()
        outs = {(kh, g): pv[kh][g * WINDOW:(g + 1) * WINDOW] * rden[kh, g] for kh, g in heads}
        yb = jnp.concatenate(
            [_merge_head_pair(outs[(2 * j) // grp, (2 * j) % grp],
                              outs[(2 * j + 1) // grp, (2 * j + 1) % grp], 2 * j, WINDOW)
             for j in range(B_HEADS // 2)], axis=1)
        sgb = _silu(zb_ref[rows, B_WIDTH + 2 * B_KV_WIDTH:ZB_W])
        y_ref[rows, A_WIDTH:A_WIDTH + B_WIDTH] = (yb * sgb).astype(BF16)
        pump()
    kprev[...] = kblocks[nblk]
    vprev[...] = vblocks[nblk]
    ko_ref[...] = kblocks[nblk]
    vo_ref[...] = vblocks[nblk]
    pump()

    xbuf[SUBLANES:SUBLANES + ts, :] = zc_ref[:, 0:2 * C_WIDTH]
    pump()
    ifp = zc_ref[:, 5 * C_WIDTH:5 * C_WIDTH + LANES]
    lf = _log_sigmoid(ifp + fb_ref[...])
    pump()
    cl = MLSTM_CHUNK
    lane_c = lax.broadcasted_iota(jnp.int32, (cl, LANES), 1)
    lane_1 = lax.broadcasted_iota(jnp.int32, (1, LANES), 1)
    m_row = m_ref[...]
    m_out = m_row
    cum_all = _dot_exact01(tri01_ref[...], lf)
    st_col = jnp.where(lane_c < C_HEADS, ifp, cum_all)
    st_row = st_col.T
    tribias = tribias_ref[...]
    pump()
    for hds in MLSTM_HEAD_GROUPS:
        hs = {hd: slice(hd * C_HEAD_DIM, (hd + 1) * C_HEAD_DIM) for hd in hds}
        i_c = {hd: st_col[:, hd:hd + 1] for hd in hds}
        cum_c = {hd: st_col[:, C_HEADS + hd:C_HEADS + hd + 1] for hd in hds}
        i_r = {hd: st_row[hd:hd + 1, :] for hd in hds}
        cum_r = {hd: st_row[C_HEADS + hd:C_HEADS + hd + 1, :] for hd in hds}
        m_prev = {hd: m_row[:, hd:hd + 1] for hd in hds}
        dmat = {hd: cum_c[hd] - cum_r[hd] + i_r[hd] + tribias for hd in hds}
        m_inter = {hd: cum_c[hd] + m_prev[hd] for hd in hds}
        m_t = {hd: jnp.maximum(m_inter[hd], jnp.max(dmat[hd], axis=-1, keepdims=True)) for hd in hds}
        pump()
        q_h = {hd: _silu(_conv_taps(xbuf, cw_ref, cb_ref, hs[hd], ts)).astype(BF16) for hd in hds}
        k_h = {hd: _silu(_conv_taps(xbuf, cw_ref, cb_ref,
                                    slice(C_WIDTH + hs[hd].start, C_WIDTH + hs[hd].stop), ts))
               * (C_HEAD_DIM ** -0.5) for hd in hds}
        pump()
        v_h = {hd: zc_ref[:, 2 * C_WIDTH + hd * C_HEAD_DIM:2 * C_WIDTH + (hd + 1) * C_HEAD_DIM].astype(BF16)
               for hd in hds}
        s_qk = {hd: _dot_nt(q_h[hd], k_h[hd].astype(BF16)) for hd in hds}
        a = {hd: jnp.exp(dmat[hd] - m_t[hd]) * s_qk[hd] for hd in hds}
        pump()
        w_inter = {hd: jnp.exp(m_inter[hd] - m_t[hd]) for hd in hds}
        c_prev = {hd: c_ref[hd] for hd in hds}
        n_prev = {hd: n_ref[hd:hd + 1, :] for hd in hds}
        inter = {hd: _dot(q_h[hd], c_prev[hd].astype(BF16)) for hd in hds}
        intra = {hd: _dot(a[hd].astype(BF16), v_h[hd]) for hd in hds}
        pump()
        den = {hd: jnp.sum(a[hd], axis=-1, keepdims=True)
               + w_inter[hd] * jnp.sum(q_h[hd].astype(F32) * n_prev[hd], axis=-1, keepdims=True)
               for hd in hds}
        rnorm = {hd: 1.0 / jnp.maximum(jnp.abs(den[hd]), jnp.exp(-m_t[hd])) for hd in hds}
        hh = {hd: (intra[hd] + w_inter[hd] * inter[hd]) * rnorm[hd] for hd in hds}
        pump()
        for hd in hds:
            o_cols = slice(3 * C_WIDTH + hd * C_HEAD_DIM, 3 * C_WIDTH + (hd + 1) * C_HEAD_DIM)
            g_cols = slice(4 * C_WIDTH + hd * C_HEAD_DIM, 4 * C_WIDTH + (hd + 1) * C_HEAD_DIM)
            gate_o = _sigmoid(zc_ref[:, o_cols]) * _silu(zc_ref[:, g_cols])
            y_cols = slice(A_WIDTH + B_WIDTH + hd * C_HEAD_DIM, A_WIDTH + B_WIDTH + (hd + 1) * C_HEAD_DIM)
            y_ref[:, y_cols] = (_rms(hh[hd]) * hg_ref[:, hs[hd]] * gate_o).astype(BF16)
        pump()
        total = {hd: cum_r[hd][:, cl - 1:cl] for hd in hds}
        g_r = {hd: total[hd] - cum_r[hd] + i_r[hd] for hd in hds}
        g_c = {hd: total[hd] - cum_c[hd] + i_c[hd] for hd in hds}
        m_new = {hd: jnp.maximum(total[hd] + m_prev[hd], jnp.max(g_r[hd], axis=-1, keepdims=True))
                 for hd in hds}
        kw = {hd: jnp.exp(g_c[hd] - m_new[hd]) * k_h[hd] for hd in hds}
        decay = {hd: jnp.exp(total[hd] + m_prev[hd] - m_new[hd]) for hd in hds}
        pump()
        upd = {hd: _dot(kw[hd].T.astype(BF16), v_h[hd]) for hd in hds}
        for hd in hds:
            c_ref[hd] = decay[hd] * c_prev[hd] + upd[hd]
            n_ref[hd:hd + 1, :] = decay[hd] * n_prev[hd] + jnp.sum(kw[hd], axis=0, keepdims=True)
            m_out = jnp.where(lane_1 == hd, m_new[hd], m_out)
        pump()
    m_ref[...] = m_out
    tail = xbuf[ts:ts + SUBLANES, :]
    xbuf[0:SUBLANES, :] = tail
    convo_ref[...] = tail


def _prompt_mask_constants():
    r = np.arange(WINDOW)[:, None]
    c = np.arange(2 * WINDOW)[None, :]
    band = (c > r) & (c <= r + WINDOW)
    band_first = band & (c >= WINDOW)
    band_bias = np.where(np.stack([band, band_first]), 0.0, NEG).astype(np.float32)
    tril = (np.arange(WINDOW)[:, None] >= np.arange(WINDOW)[None, :]).astype(np.float32)
    tri = np.arange(MLSTM_CHUNK)[:, None] >= np.arange(MLSTM_CHUNK)[None, :]
    return (jnp.asarray(tril), jnp.asarray(band_bias), jnp.asarray(tri, dtype=BF16),
            jnp.asarray(np.where(tri, 0.0, NEG).astype(np.float32)))


N_MIX_PARAMS = 13
MLSTM_HEAD_GROUPS = ((0, 1), (2, 3))
MIX_PUMP_CALLS = 37
TAIL_FILL_PIECES = 8
MXU_PIECE_COLS = 256


class _Interleaver:
    def __init__(self, pieces, calls, hold_back=0):
        self._pieces = list(pieces)
        self._hold_back = hold_back
        self._spread = len(self._pieces) - hold_back
        self._emitted = 0
        self._calls = calls
        self._call = 0

    def __call__(self):
        self._call += 1
        target = (self._call * self._spread) // self._calls
        while self._emitted < target:
            self._pieces.pop(0)()
            self._emitted += 1

    def finish(self):
        assert self._call == self._calls and len(self._pieces) == self._hold_back, self._call
        return self._pieces


def _gate_pieces(h_ref, wmg_ref, bmg_ref, g_ref):
    def piece(off):
        cols = slice(off, off + MXU_PIECE_COLS)
        def run():
            g_ref[:, cols] = _sigmoid(_dot(h_ref[...], wmg_ref[:, cols]) + bmg_ref[:, cols])
        return run
    return [piece(off) for off in range(0, 3 * D_MODEL, MXU_PIECE_COLS)]


def _merge_and_project(x, mod_ref, g_ref, y_ref, wa_ref, wb_ref, wc_ref, wo_ref, fillers=()):
    fillers = list(fillers)
    per_stage = -(-len(fillers) // 4)
    merged = None
    for i, wbr_ref in enumerate((wa_ref, wb_ref, wc_ref)):
        for piece in fillers[i * per_stage:(i + 1) * per_stage]:
            piece()
        term = (g_ref[:, i * D_MODEL:(i + 1) * D_MODEL]
                * _dot(y_ref[:, i * A_WIDTH:(i + 1) * A_WIDTH], wbr_ref[...]))
        merged = term if merged is None else merged + term
    for piece in fillers[3 * per_stage:]:
        piece()
    ada_gate = mod_ref[:, 2 * D_MODEL:3 * D_MODEL]
    return x + ada_gate * _dot(merged.astype(BF16), wo_ref[...])


def _prompt_layer_kernel(tiles_per_seq, sink_ref, x2_ref, xn_ref, mod_ref, modn_ref, ng_ref,
                         wcat_ref, bcat_ref, *rest):
    mix_params = rest[:N_MIX_PARAMS]
    wmg_ref, bmg_ref, wa_ref, wb_ref, wc_ref, wo_ref = rest[N_MIX_PARAMS:N_MIX_PARAMS + 6]
    o_ref, ko_ref, vo_ref, convo_ref, c_ref, n_ref, m_ref = rest[N_MIX_PARAMS + 6:N_MIX_PARAMS + 13]
    (za0, zb0, zc0, za1, zb1, zc1, h0, h1, y_scr, g_scr, kprev, vprev, xbuf) = rest[N_MIX_PARAMS + 13:]
    ts = PROMPT_TILE
    z = ((za0, zb0, zc0), (za1, zb1, zc1))
    h = (h0, h1)
    k = pl.program_id(0)
    seq_start = (k % (tiles_per_seq // 2)) == 0

    @pl.when(k == 0)
    def _():
        h0[...] = _modulated_norm(x2_ref[0:ts, :], mod_ref, ng_ref)
        for piece in _inproj_pieces(lambda: h0[...], wcat_ref, bcat_ref, *z[0], 512):
            piece()

    @pl.when(seq_start)
    def _():
        kprev[...] = jnp.zeros_like(kprev)
        vprev[...] = jnp.zeros_like(vprev)
        xbuf[0:SUBLANES, :] = jnp.zeros((SUBLANES, 2 * C_WIDTH), F32)
        c_ref[...] = jnp.zeros_like(c_ref)
        n_ref[...] = jnp.zeros_like(n_ref)
        m_ref[...] = jnp.zeros_like(m_ref)

    for half in range(2):
        cur, nxt = half, 1 - half
        rows = slice(half * ts, (half + 1) * ts)
        if half == 0:
            h[nxt][...] = _modulated_norm(x2_ref[ts:2 * ts, :], mod_ref, ng_ref)
        else:
            h[nxt][...] = _modulated_norm(xn_ref[...], modn_ref, ng_ref)
        get_h_next = functools.partial(lambda r: r[...], h[nxt])
        hold = TAIL_FILL_PIECES if half == 1 else 0
        proj = _inproj_pieces(get_h_next, wcat_ref, bcat_ref, *z[nxt], MXU_PIECE_COLS)
        pump = _Interleaver(
            proj[:len(proj) - hold] + _gate_pieces(h[cur], wmg_ref, bmg_ref, g_scr)
            + proj[len(proj) - hold:], MIX_PUMP_CALLS, hold_back=hold)
        _prompt_mix_kernel(sink_ref, *z[cur], *mix_params,
                           y_scr, ko_ref, vo_ref, convo_ref, c_ref, n_ref, m_ref, kprev, vprev, xbuf,
                           first_tile=seq_start if half == 0 else False, pump=pump)
        o_ref[rows, :] = _merge_and_project(x2_ref[rows, :], mod_ref, g_scr, y_scr,
                                            wa_ref, wb_ref, wc_ref, wo_ref, fillers=pump.finish())


def _prompt_layer_call(layer, x2, mod, lw, batch, seq):
    ts = PROMPT_TILE
    nt = seq // ts
    assert nt % 2 == 0
    last_tile = batch * nt - 1
    const2 = lambda k: (0, 0)
    const3 = lambda k: (0, 0, 0)
    per_b3 = lambda k: ((2 * k) // nt, 0, 0)
    next_tile = lambda k: jnp.minimum(2 * k + 2, last_tile)
    once = pl.Buffered(1)
    return pl.pallas_call(
        functools.partial(_prompt_layer_kernel, nt),
        grid=(batch * nt // 2,),
        in_specs=[
            pl.BlockSpec(memory_space=pltpu.SMEM),
            pl.BlockSpec((2 * ts, D_MODEL), lambda k: (k, 0)),
            pl.BlockSpec((ts, D_MODEL), lambda k: (next_tile(k), 0)),
            pl.BlockSpec((None, 1, 3 * D_MODEL), per_b3),
            pl.BlockSpec((None, 1, 3 * D_MODEL), lambda k: (next_tile(k) // nt, 0, 0)),
            pl.BlockSpec((1, D_MODEL), const2),
            _layer_weight_spec(layer, D_MODEL, ZCAT_W),
            pl.BlockSpec((1, ZCAT_W), const2),
            pl.BlockSpec((1, A_WIDTH), const2),
            pl.BlockSpec((A_GROUPS, WINDOW, WINDOW), const3),
            pl.BlockSpec((WINDOW, LANES), const2),
            pl.BlockSpec((1, B_WIDTH), const2),
            pl.BlockSpec((1, B_KV_WIDTH), const2),
            pl.BlockSpec((C_CONV, 2 * C_WIDTH), const2),
            pl.BlockSpec((1, 2 * C_WIDTH), const2),
            pl.BlockSpec((1, LANES), const2),
            pl.BlockSpec((1, C_WIDTH), const2),
            pl.BlockSpec((WINDOW, WINDOW), const2),
            pl.BlockSpec((2, WINDOW, 2 * WINDOW), const3),
            pl.BlockSpec((MLSTM_CHUNK, MLSTM_CHUNK), const2),
            pl.BlockSpec((MLSTM_CHUNK, MLSTM_CHUNK), const2),
            _layer_weight_spec(layer, D_MODEL, 3 * D_MODEL),
            pl.BlockSpec((1, 3 * D_MODEL), const2),
            pl.BlockSpec((A_WIDTH, D_MODEL), const2, pipeline_mode=once),
            pl.BlockSpec((B_WIDTH, D_MODEL), const2, pipeline_mode=once),
            pl.BlockSpec((C_WIDTH, D_MODEL), const2, pipeline_mode=once),
            pl.BlockSpec((D_MODEL, D_MODEL), const2, pipeline_mode=once),
        ],
        out_specs=[
            pl.BlockSpec((2 * ts, D_MODEL), lambda k: (k, 0)),
            pl.BlockSpec((None, WINDOW, B_KV_WIDTH), per_b3),
            pl.BlockSpec((None, WINDOW, B_KV_WIDTH), per_b3),
            pl.BlockSpec((None, SUBLANES, 2 * C_WIDTH), per_b3),
            pl.BlockSpec((None, C_HEADS, C_HEAD_DIM, C_HEAD_DIM), lambda k: ((2 * k) // nt, 0, 0, 0)),
            pl.BlockSpec((None, C_HEADS, C_HEAD_DIM), per_b3),
            pl.BlockSpec((None, 1, LANES), per_b3),
        ],
        out_shape=[
            jax.ShapeDtypeStruct((batch * seq, D_MODEL), F32),
            jax.ShapeDtypeStruct((batch, WINDOW, B_KV_WIDTH), F32),
            jax.ShapeDtypeStruct((batch, WINDOW, B_KV_WIDTH), F32),
            jax.ShapeDtypeStruct((batch, SUBLANES, 2 * C_WIDTH), F32),
            jax.ShapeDtypeStruct((batch, C_HEADS, C_HEAD_DIM, C_HEAD_DIM), F32),
            jax.ShapeDtypeStruct((batch, C_HEADS, C_HEAD_DIM), F32),
            jax.ShapeDtypeStruct((batch, 1, LANES), F32),
        ],
        scratch_shapes=(
            [pltpu.VMEM((ts, w), F32) for w in (ZA_W, ZB_W, ZC_W)] * 2
            + [pltpu.VMEM((ts, D_MODEL), BF16)] * 2
            + [pltpu.VMEM((ts, Y_W), BF16),
               pltpu.VMEM((ts, 3 * D_MODEL), F32),
               pltpu.VMEM((WINDOW, B_KV_WIDTH), F32),
               pltpu.VMEM((WINDOW, B_KV_WIDTH), F32),
               pltpu.VMEM((ts + SUBLANES, 2 * C_WIDTH), F32)]),
        compiler_params=pltpu.CompilerParams(
            dimension_semantics=("arbitrary",), vmem_limit_bytes=VMEM_LIMIT),
        name="prompt_layer",
    )(lw["sinks"], x2, x2, mod, mod, lw["ng"], lw["wcat"], lw["bcat"],
      lw["vg"], lw["gws"], lw["gbs_col"], lw["qg"], lw["kg"], lw["cw"], lw["cb"], lw["fb"], lw["hg"],
      *_prompt_mask_constants(),
      lw["wmg"], lw["bmg"], lw["wa"], lw["wb"], lw["wc"], lw["wo"])


def _sample_mix_kernel(sink_ref, za_ref, zb_ref, zc_ref, kc_ref, vc_ref, cs_ref, c0_ref, n0_ref,
                       m0_ref, vg_ref, gwb_ref, gbs_ref, qg_ref, kg_ref, cw_ref, cb_ref, fb_ref,
                       hg_ref,
                       y_ref, vrow_ref, ko_ref, vo_ref, convo_ref, c1_ref, n1_ref, m1_ref,
                       xbuf):
    nb = SAMPLE_NB
    t = SUBLANES
    rows = nb * t
    tok_r = lax.broadcasted_iota(jnp.int32, (rows, rows), 0)
    tok_c = lax.broadcasted_iota(jnp.int32, (rows, rows), 1)
    same_b = (tok_r // t) == (tok_c // t)
    causal_b = same_b & (tok_c <= tok_r)

    u = za_ref[:, 0:A_WIDTH]
    vn = _rms(za_ref[:, A_WIDTH:2 * A_WIDTH]) * vg_ref[...]
    sg = _silu(za_ref[:, 2 * A_WIDTH:3 * A_WIDTH])
    vrow_ref[...] = vn
    vnb = vn.astype(BF16)
    s_cols = []
    for gi in range(A_GROUPS):
        s_cols.append(_dot(gwb_ref[gi], vnb[:, gi * GROUP_DIM:(gi + 1) * GROUP_DIM])
                      + gbs_ref[:, gi:gi + 1])
    y_ref[:, 0:A_WIDTH] = (u * jnp.concatenate(s_cols, axis=1) * sg).astype(BF16)

    qn = _qk_norm(zb_ref[:, 0:B_WIDTH], qg_ref[...]) * (B_HEAD_DIM ** -0.5)
    kn = _qk_norm(zb_ref[:, B_WIDTH:B_WIDTH + B_KV_WIDTH], kg_ref[...])
    vv = zb_ref[:, B_WIDTH + B_KV_WIDTH:B_WIDTH + 2 * B_KV_WIDTH]
    sgb = _silu(zb_ref[:, B_WIDTH + 2 * B_KV_WIDTH:ZB_W])
    kn3 = kn.reshape(nb, t, B_KV_WIDTH)
    vv3 = vv.reshape(nb, t, B_KV_WIDTH)
    kcache = kc_ref[...]
    vcache = vc_ref[...]
    pad = jnp.zeros((nb, WINDOW - t, B_KV_WIDTH), F32)
    kall = jnp.concatenate([kcache, kn3, pad], axis=1).astype(BF16)
    vall = jnp.concatenate([vcache, vv3, pad], axis=1).astype(BF16)
    qp = jnp.concatenate([_place_q_head(qn, h, rows).reshape(nb, t, LANES) for h in range(B_HEADS)],
                         axis=1).astype(BF16)
    logits = lax.dot_general(qp, kall, (((2,), (2,)), ((0,), (0,))), preferred_element_type=F32)
    qrow = lax.broadcasted_iota(jnp.int32, (nb, B_HEADS * t, 2 * WINDOW), 1)
    kcol = lax.broadcasted_iota(jnp.int32, (nb, B_HEADS * t, 2 * WINDOW), 2)
    qt = qrow % t
    valid = ((kcol < WINDOW) & (kcol > qt)) | ((kcol >= WINDOW) & ((kcol - WINDOW) <= qt))
    hrow = lax.broadcasted_iota(jnp.int32, (B_HEADS * t, 1), 0) // t
    snk = jnp.zeros((B_HEADS * t, 1), F32)
    for h in range(B_HEADS):
        snk = jnp.where(hrow == h, sink_ref[h], snk)
    lg = jnp.where(valid, logits, NEG)
    mx = jnp.maximum(jnp.max(lg, axis=-1, keepdims=True), snk[None])
    p = jnp.exp(lg - mx)
    den = jnp.sum(p, axis=-1, keepdims=True) + jnp.exp(snk[None] - mx)
    pv = lax.dot_general(p.astype(BF16), vall, (((2,), (1,)), ((0,), (0,))),
                         preferred_element_type=F32) / den
    head_out = [pv[:, h * t:(h + 1) * t, :].reshape(rows, LANES) for h in range(B_HEADS)]
    yb = jnp.concatenate(
        [_merge_head_pair(head_out[2 * j], head_out[2 * j + 1], 2 * j, rows)
         for j in range(B_HEADS // 2)], axis=1)
    y_ref[:, A_WIDTH:A_WIDTH + B_WIDTH] = (yb * sgb).astype(BF16)
    ko_ref[...] = jnp.concatenate([kcache[:, t:, :], kn3], axis=1)
    vo_ref[...] = jnp.concatenate([vcache[:, t:, :], vv3], axis=1)

    xbuf[:, SUBLANES - (C_CONV - 1):SUBLANES, :] = cs_ref[...]
    xbuf[:, SUBLANES:2 * SUBLANES, :] = zc_ref[:, 0:2 * C_WIDTH].reshape(nb, t, 2 * C_WIDTH)
    y3 = cb_ref[...][None]
    for j in range(C_CONV):
        lo = SUBLANES - (C_CONV - 1) + j
        y3 = y3 + cw_ref[j:j + 1, :][None] * xbuf[:, lo:lo + t, :]
    convo_ref[...] = xbuf[:, 2 * SUBLANES - (C_CONV - 1):2 * SUBLANES, :]
    qk = _silu(y3.reshape(rows, 2 * C_WIDTH))
    qall = qk[:, 0:C_WIDTH].astype(BF16)
    kall_c = qk[:, C_WIDTH:2 * C_WIDTH] * (C_HEAD_DIM ** -0.5)
    vall_c = zc_ref[:, 2 * C_WIDTH:3 * C_WIDTH].astype(BF16)
    gate_o = _sigmoid(zc_ref[:, 3 * C_WIDTH:4 * C_WIDTH]) * _silu(zc_ref[:, 4 * C_WIDTH:5 * C_WIDTH])
    ifp = zc_ref[:, 5 * C_WIDTH:5 * C_WIDTH + LANES]
    lf = _log_sigmoid(ifp + fb_ref[...])
    lane_t = lax.broadcasted_iota(jnp.int32, (rows, LANES), 1)
    cum_all = _dot_exact01(jnp.where(causal_b, 1.0, 0.0).astype(BF16), lf)
    tot_all = _dot_exact01(jnp.where(same_b, 1.0, 0.0).astype(BF16), lf)
    st_col = jnp.where(lane_t < C_HEADS, ifp, cum_all)
    st_row = st_col.T
    tot_row = tot_all.T
    m0 = m0_ref[...]
    same_b_bf = jnp.where(same_b, 1.0, 0.0).astype(BF16)
    batch_of_lane = lax.broadcasted_iota(jnp.int32, (nb, 1, rows), 2) // t
    batch_id = lax.broadcasted_iota(jnp.int32, (nb, 1, rows), 0)
    own_tok = batch_of_lane == batch_id
    h_cols = []
    m_out = jnp.zeros((rows, LANES), F32)
    for hd in range(C_HEADS):
        hs = slice(hd * C_HEAD_DIM, (hd + 1) * C_HEAD_DIM)
        i_c = st_col[:, hd:hd + 1]
        cum_c = st_col[:, C_HEADS + hd:C_HEADS + hd + 1]
        tot_c = tot_all[:, C_HEADS + hd:C_HEADS + hd + 1]
        i_r = st_row[hd:hd + 1, :]
        cum_r = st_row[C_HEADS + hd:C_HEADS + hd + 1, :]
        tot_r = tot_row[C_HEADS + hd:C_HEADS + hd + 1, :]
        m_prev = m0[:, hd:hd + 1]
        dmat = jnp.where(causal_b, cum_c - cum_r + i_r, NEG)
        m_inter = cum_c + m_prev
        m_t = jnp.maximum(m_inter, jnp.max(dmat, axis=-1, keepdims=True))
        q_h = qall[:, hs]
        k_h = kall_c[:, hs]
        v_h = vall_c[:, hs]
        a = jnp.exp(dmat - m_t) * _dot_nt(q_h, k_h.astype(BF16))
        w_inter = jnp.exp(m_inter - m_t)
        c_prev = c0_ref[:, hd]
        n_tok = jnp.broadcast_to(n0_ref[hd][:, None, :], (nb, t, C_HEAD_DIM)).reshape(rows, C_HEAD_DIM)
        inter = lax.dot_general(q_h.reshape(nb, t, C_HEAD_DIM), c_prev.astype(BF16),
                                (((2,), (1,)), ((0,), (0,))), preferred_element_type=F32)
        num = _dot(a.astype(BF16), v_h) + w_inter * inter.reshape(rows, C_HEAD_DIM)
        den = (jnp.sum(a, axis=-1, keepdims=True)
               + w_inter * jnp.sum(q_h.astype(F32) * n_tok, axis=-1, keepdims=True))
        hh = num / jnp.maximum(jnp.abs(den), jnp.exp(-m_t))
        h_cols.append(_rms(hh))
        g_r = tot_r - cum_r + i_r
        g_c = tot_c - cum_c + i_c
        m_new = jnp.maximum(tot_c + m_prev,
                            jnp.max(jnp.where(same_b, g_r, NEG), axis=-1, keepdims=True))
        kw = jnp.exp(g_c - m_new) * k_h
        decay = jnp.exp(tot_c + m_prev - m_new)
        kwt = kw.T
        lhs = jnp.where(own_tok, kwt[None], 0.0).astype(BF16).reshape(nb * C_HEAD_DIM, rows)
        upd = _dot(lhs, v_h).reshape(nb, C_HEAD_DIM, C_HEAD_DIM)
        dec_b = jnp.broadcast_to(decay, (rows, C_HEAD_DIM)).reshape(nb, t, C_HEAD_DIM)[:, 0:1, :]
        c1_ref[:, hd] = dec_b * c_prev + upd
        n1_ref[hd] = decay * n_tok + _dot(same_b_bf, kw.astype(BF16))
        m_out = jnp.where(lane_t == hd, m_new, m_out)
    m1_ref[...] = m_out
    hn = jnp.concatenate(h_cols, axis=1) * hg_ref[...]
    y_ref[:, A_WIDTH + B_WIDTH:Y_W] = (hn * gate_o).astype(BF16)


def _sample_mix_call(l, za, zb, zc, kc, vc, cs, c0, n0t, m0tok, lw, nbatch):
    nb = SAMPLE_NB
    t = SUBLANES
    rows = nb * t
    tok = lambda i: (i, 0)
    const2 = lambda i: (0, 0)
    const3 = lambda i: (0, 0, 0)
    b3 = lambda i: (i, 0, 0)
    lb4 = lambda i: (l, i, 0, 0)
    return pl.pallas_call(
        _sample_mix_kernel,
        grid=(nbatch // nb,),
        in_specs=[
            pl.BlockSpec(memory_space=pltpu.SMEM),
            pl.BlockSpec((rows, ZA_W), tok),
            pl.BlockSpec((rows, ZB_W), tok),
            pl.BlockSpec((rows, ZC_W), tok),
            pl.BlockSpec((None, nb, WINDOW, B_KV_WIDTH), lb4),
            pl.BlockSpec((None, nb, WINDOW, B_KV_WIDTH), lb4),
            pl.BlockSpec((None, nb, C_CONV - 1, 2 * C_WIDTH), lb4),
            pl.BlockSpec((None, nb, C_HEADS, C_HEAD_DIM, C_HEAD_DIM), lambda i: (l, i, 0, 0, 0)),
            pl.BlockSpec((None, C_HEADS, nb, C_HEAD_DIM), lambda i: (l, 0, i, 0)),
            pl.BlockSpec((None, rows, LANES), lambda i: (l, i, 0)),
            pl.BlockSpec((1, A_WIDTH), const2),
            pl.BlockSpec((A_GROUPS, rows, rows), const3),
            pl.BlockSpec((rows, LANES), const2),
            pl.BlockSpec((1, B_WIDTH), const2),
            pl.BlockSpec((1, B_KV_WIDTH), const2),
            pl.BlockSpec((C_CONV, 2 * C_WIDTH), const2),
            pl.BlockSpec((1, 2 * C_WIDTH), const2),
            pl.BlockSpec((1, LANES), const2),
            pl.BlockSpec((1, C_WIDTH), const2),
        ],
        out_specs=[
            pl.BlockSpec((rows, Y_W), tok),
            pl.BlockSpec((rows, A_WIDTH), tok),
            pl.BlockSpec((nb, WINDOW, B_KV_WIDTH), b3),
            pl.BlockSpec((nb, WINDOW, B_KV_WIDTH), b3),
            pl.BlockSpec((nb, C_CONV - 1, 2 * C_WIDTH), b3),
            pl.BlockSpec((nb, C_HEADS, C_HEAD_DIM, C_HEAD_DIM), lambda i: (i, 0, 0, 0)),
            pl.BlockSpec((C_HEADS, rows, C_HEAD_DIM), lambda i: (0, i, 0)),
            pl.BlockSpec((rows, LANES), tok),
        ],
        out_shape=[
            jax.ShapeDtypeStruct((nbatch * t, Y_W), BF16),
            jax.ShapeDtypeStruct((nbatch * t, A_WIDTH), F32),
            jax.ShapeDtypeStruct((nbatch, WINDOW, B_KV_WIDTH), F32),
            jax.ShapeDtypeStruct((nbatch, WINDOW, B_KV_WIDTH), F32),
            jax.ShapeDtypeStruct((nbatch, C_CONV - 1, 2 * C_WIDTH), F32),
            jax.ShapeDtypeStruct((nbatch, C_HEADS, C_HEAD_DIM, C_HEAD_DIM), F32),
            jax.ShapeDtypeStruct((C_HEADS, nbatch * t, C_HEAD_DIM), F32),
            jax.ShapeDtypeStruct((nbatch * t, LANES), F32),
        ],
        scratch_shapes=[pltpu.VMEM((nb, 2 * SUBLANES, 2 * C_WIDTH), F32)],
        compiler_params=pltpu.CompilerParams(
            dimension_semantics=("arbitrary",), vmem_limit_bytes=VMEM_LIMIT),
        name="sample_mixer",
    )(lw["sinks"], za, zb, zc, kc, vc, cs, c0, n0t, m0tok, lw["vg"], lw["gwb"], lw["gbs_tok"],
      lw["qg"], lw["kg"], lw["cw"], lw["cb"], lw["fb"], lw["hg"])


def _layer_weights(l, wcat_all, wmg_all, b_in, gmlp_vnorm_g, gmlp_ws, gmlp_bs, swa_qnorm_g,
                   swa_knorm_g, swa_sinks, mlstm_conv_w, mlstm_conv_b, mlstm_f_bias, mlstm_hnorm_g,
                   w_branch_a, w_branch_b, w_branch_c, w_out, norm_g, dec_seq):
    bl = b_in[l]
    bcat = jnp.concatenate([bl[:COL_CI], bl[COL_CO:COL_MG], bl[COL_CI:COL_CO],
                            jnp.zeros((LANES - 2 * C_HEADS,), F32)])
    t = dec_seq
    nb = SAMPLE_NB
    ws_t = gmlp_ws[l][:, :t, :t] * jnp.tril(jnp.ones((t, t), F32))
    eye = jnp.eye(nb, dtype=F32)
    gwb = jnp.einsum("bc,gts->gbtcs", eye, ws_t).reshape(A_GROUPS, nb * t, nb * t).astype(BF16)
    gbs_col = jnp.pad(gmlp_bs[l].T, ((0, 0), (0, LANES - A_GROUPS)))
    gbs_tok = jnp.pad(jnp.tile(gmlp_bs[l][:, :t].T, (nb, 1)), ((0, 0), (0, LANES - A_GROUPS)))
    fb = jnp.pad(mlstm_f_bias[l], (C_HEADS, LANES - 2 * C_HEADS)).reshape(1, LANES)
    return dict(
        ng=norm_g[l].reshape(1, D_MODEL),
        wcat=wcat_all, bcat=bcat.reshape(1, ZCAT_W),
        wmg=wmg_all, bmg=bl[COL_MG:].reshape(1, 3 * D_MODEL),
        wa=w_branch_a[l].astype(BF16), wb=w_branch_b[l].astype(BF16),
        wc=w_branch_c[l].astype(BF16), wo=w_out[l].astype(BF16),
        vg=gmlp_vnorm_g[l].reshape(1, A_WIDTH), gws=gmlp_ws[l], gwb=gwb,
        gbs_col=gbs_col, gbs_tok=gbs_tok,
        qg=jnp.tile(swa_qnorm_g[l], B_HEADS).reshape(1, B_WIDTH),
        kg=jnp.tile(swa_knorm_g[l], B_KV_HEADS).reshape(1, B_KV_WIDTH),
        sinks=swa_sinks[l],
        cw=mlstm_conv_w[l], cb=mlstm_conv_b[l].reshape(1, 2 * C_WIDTH), fb=fb,
        hg=mlstm_hnorm_g[l].reshape(1, C_WIDTH),
    )


def kernel(x_prompt, x_sample, cache_swa_k, cache_swa_v, state_mlstm_conv, state_mlstm_C, state_mlstm_n, state_mlstm_m, c_prompt, c_sample, ada_w, ada_b, norm_g, w_in, b_in, gmlp_vnorm_g, gmlp_ws, gmlp_bs, swa_qnorm_g, swa_knorm_g, swa_sinks, mlstm_conv_w, mlstm_conv_b, mlstm_f_bias, mlstm_hnorm_g, w_branch_a, w_branch_b, w_branch_c, w_out):
    batch, seq, _ = x_prompt.shape
    nbatch, dec_seq, _ = x_sample.shape
    assert dec_seq == SUBLANES and seq % PROMPT_TILE == 0 and nbatch % SAMPLE_NB == 0
    assert seq % PROJ_TILE == 0 and (nbatch * dec_seq) % PROJ_TILE == 0
    wb_len = cache_swa_k.shape[2]
    assert wb_len == WINDOW

    nc = batch + nbatch
    nc_pad = -(-nc // SUBLANES) * SUBLANES
    c_all = jnp.concatenate([c_prompt, c_sample, jnp.zeros((nc_pad - nc, D_MODEL), F32)], axis=0)
    mod_all = _ada_call(c_all, ada_w, ada_b)

    xp = x_prompt.reshape(batch * seq, D_MODEL)
    xs = x_sample.reshape(nbatch * dec_seq, D_MODEL)
    kc_all = cache_swa_k.reshape(DEPTH, nbatch, WINDOW, B_KV_WIDTH)
    vc_all = cache_swa_v.reshape(DEPTH, nbatch, WINDOW, B_KV_WIDTH)
    n0t_all = jnp.transpose(state_mlstm_n, (0, 2, 1, 3))
    m0tok_all = jnp.pad(jnp.repeat(state_mlstm_m, dec_seq, axis=1),
                        ((0, 0), (0, 0), (0, LANES - C_HEADS)))
    wcat_all, wmg_all = _weight_prep_call(w_in)
    outs_p = [[] for _ in range(6)]
    outs_s = [[] for _ in range(6)]
    vrows = []
    for l in range(DEPTH):
        lw = _layer_weights(l, wcat_all, wmg_all, b_in, gmlp_vnorm_g, gmlp_ws, gmlp_bs, swa_qnorm_g,
                            swa_knorm_g, swa_sinks, mlstm_conv_w, mlstm_conv_b, mlstm_f_bias,
                            mlstm_hnorm_g, w_branch_a, w_branch_b, w_branch_c, w_out, norm_g,
                            dec_seq)
        mod_p = mod_all[l, :batch].reshape(batch, 1, 3 * D_MODEL)
        mod_s = jnp.repeat(mod_all[l, batch:nc], dec_seq, axis=0)

        xp, ko, vo, convo, c1, n1, m1 = _prompt_layer_call(l, xp, mod_p, lw, batch, seq)
        outs_p[0].append(ko.reshape(batch, WINDOW, B_KV_HEADS, B_HEAD_DIM))
        outs_p[1].append(vo.reshape(batch, WINDOW, B_KV_HEADS, B_HEAD_DIM))
        outs_p[2].append(convo[:, SUBLANES - (C_CONV - 1):, :])
        outs_p[3].append(c1)
        outs_p[4].append(n1)
        outs_p[5].append(m1[:, 0, :C_HEADS])

        za, zb, zc = _inproj_call(l, xs, mod_s, lw["ng"], lw["wcat"], lw["bcat"], None)
        y, vrow, ko, vo, convo, c1, n1tok, m1tok = _sample_mix_call(
            l, za, zb, zc, kc_all, vc_all, state_mlstm_conv, state_mlstm_C, n0t_all, m0tok_all,
            lw, nbatch)
        xs = _outproj_call(l, xs, mod_s, lw["ng"], y, lw["wmg"], lw["bmg"], lw["wa"], lw["wb"],
                           lw["wc"], lw["wo"], None)
        outs_s[0].append(ko.reshape(nbatch, WINDOW, B_KV_HEADS, B_HEAD_DIM))
        outs_s[1].append(vo.reshape(nbatch, WINDOW, B_KV_HEADS, B_HEAD_DIM))
        outs_s[2].append(convo)
        outs_s[3].append(c1)
        outs_s[4].append(jnp.transpose(n1tok[:, ::dec_seq, :], (1, 0, 2)))
        outs_s[5].append(m1tok[::dec_seq, :C_HEADS])
        vrows.append(vrow.reshape(nbatch, dec_seq, A_WIDTH))

    sp = [jnp.stack(o) for o in outs_p]
    ss = [jnp.stack(o) for o in outs_s]
    return (xp.reshape(batch, seq, D_MODEL), xs.reshape(nbatch, dec_seq, D_MODEL),
            sp[0], sp[1], sp[2], sp[3], sp[4], sp[5],
            ss[0], ss[1], ss[2], ss[3], ss[4], ss[5], jnp.stack(vrows))
```

```python
import functools

import numpy as np
import jax
import jax.numpy as jnp
from jax import lax
from jax.experimental import pallas as pl
from jax.experimental.pallas import tpu as pltpu

F32 = jnp.float32
BF16 = jnp.bfloat16

D_MODEL = 1024
DEPTH = 2
A_WIDTH = 512
A_GROUPS = 4
GROUP_DIM = 128
B_HEADS = 8
B_KV_HEADS = 2
B_HEAD_DIM = 64
B_WIDTH = 512
B_KV_WIDTH = 128
WINDOW = 128
C_HEADS = 4
C_HEAD_DIM = 128
C_WIDTH = 512
C_CONV = 4
EPS = 1e-6
NEG = -1e30

LANES = 128
SUBLANES = 8
VMEM_LIMIT = 56 * 1024 * 1024

ZA_W = 3 * A_WIDTH
ZB_W = 2 * B_WIDTH + 2 * B_KV_WIDTH
ZC_W = 2 * C_WIDTH + 3 * C_WIDTH + LANES
ZCAT_W = ZA_W + ZB_W + ZC_W
Y_W = A_WIDTH + B_WIDTH + C_WIDTH

PROMPT_TILE = 256
MLSTM_CHUNK = PROMPT_TILE
SAMPLE_NB = 16
PROJ_TILE = 512


def _sigmoid(x):
    return 0.5 * jnp.tanh(0.5 * x) + 0.5


def _silu(x):
    t = 0.5 * x
    return t * (jnp.tanh(t) + 1.0)


def _log_sigmoid(x):
    return jnp.minimum(x, 0.0) - jnp.log1p(jnp.exp(-jnp.abs(x)))


def _rms(x):
    return x * lax.rsqrt(jnp.mean(x * x, axis=-1, keepdims=True) + EPS)


def _dot(a, b):
    return jnp.dot(a, b, preferred_element_type=F32)


def _dot_nt(a, b):
    return lax.dot_general(a, b, (((1,), (1,)), ((), ())), preferred_element_type=F32)


def _dot_exact01(m01, x):
    hi = x.astype(BF16)
    r1 = x - hi.astype(F32)
    mid = r1.astype(BF16)
    lo = (r1 - mid.astype(F32)).astype(BF16)
    return _dot(m01, hi) + _dot(m01, mid) + _dot(m01, lo)


def _modulated_norm(x, mod_ref, ng_ref):
    xn = _rms(x) * ng_ref[...]
    shift = mod_ref[:, 0:D_MODEL]
    scale = mod_ref[:, D_MODEL:2 * D_MODEL]
    return (xn * (1.0 + scale) + shift).astype(BF16)


def _head_rms_scale(x2, lane_lo):
    s0 = jnp.sum(jnp.where(lane_lo, x2, 0.0), axis=-1, keepdims=True)
    s1 = jnp.sum(jnp.where(lane_lo, 0.0, x2), axis=-1, keepdims=True)
    r0 = lax.rsqrt(s0 * (1.0 / B_HEAD_DIM) + EPS)
    r1 = lax.rsqrt(s1 * (1.0 / B_HEAD_DIM) + EPS)
    return jnp.where(lane_lo, r0, r1)


def _qk_norm(x, g_row):
    rows, width = x.shape
    lane_lo = lax.broadcasted_iota(jnp.int32, (rows, LANES), 1) < B_HEAD_DIM
    outs = []
    for j in range(width // LANES):
        slab = x[:, j * LANES:(j + 1) * LANES]
        outs.append(slab * _head_rms_scale(slab * slab, lane_lo))
    y = outs[0] if len(outs) == 1 else jnp.concatenate(outs, axis=1)
    return y * g_row


def _ada_kernel(c_ref, w_ref, b_ref, o_ref):
    c = c_ref[...]
    o_ref[...] = _dot(_silu(c).astype(BF16), w_ref[...].astype(BF16)) + b_ref[...]


def _ada_call(c_all, ada_w, ada_b):
    rows = c_all.shape[0]
    return pl.pallas_call(
        _ada_kernel,
        grid=(DEPTH, 3),
        in_specs=[
            pl.BlockSpec((rows, D_MODEL), lambda l, j: (0, 0)),
            pl.BlockSpec((None, D_MODEL, D_MODEL), lambda l, j: (l, 0, j)),
            pl.BlockSpec((None, 1, D_MODEL), lambda l, j: (l, 0, j)),
        ],
        out_specs=pl.BlockSpec((None, rows, D_MODEL), lambda l, j: (l, 0, j)),
        out_shape=jax.ShapeDtypeStruct((DEPTH, rows, 3 * D_MODEL), F32),
        compiler_params=pltpu.CompilerParams(
            dimension_semantics=("arbitrary", "arbitrary"), vmem_limit_bytes=VMEM_LIMIT),
        name="adaln_mod",
    )(c_all, ada_w, ada_b.reshape(DEPTH, 1, 3 * D_MODEL))


COL_CI = ZA_W + ZB_W + 3 * C_WIDTH
COL_CO = COL_CI + 2 * C_HEADS
COL_MG = COL_CO + 2 * C_WIDTH
PREP_CHUNK = 256
PREP_SHIFT = 2 * C_HEADS
N_MAIN = COL_CI // PREP_CHUNK
N_CO = (2 * C_WIDTH) // PREP_CHUNK
N_MG = (3 * D_MODEL) // PREP_CHUNK
J_CIF = N_MAIN + N_CO
J_MG = J_CIF + 1


def _weight_prep_kernel(wa_ref, wb_ref, wcat_ref, wmg_ref):
    j = pl.program_id(1)

    def shifted_t():
        rows = jnp.concatenate([wa_ref[PREP_SHIFT:PREP_CHUNK, :], wb_ref[...]], axis=0)
        return rows.astype(BF16).T

    @pl.when(j < N_MAIN)
    def _():
        wcat_ref[...] = wa_ref[...].astype(BF16).T

    @pl.when((j >= N_MAIN) & (j < J_CIF))
    def _():
        wcat_ref[...] = shifted_t()

    @pl.when(j == J_CIF)
    def _():
        row = lax.broadcasted_iota(jnp.int32, (PREP_CHUNK, D_MODEL), 0)
        wcat_ref[...] = jnp.where(row < PREP_SHIFT, wa_ref[...], 0.0).astype(BF16).T

    @pl.when(j >= J_MG)
    def _():
        wmg_ref[...] = shifted_t()


def _weight_prep_call(w_in):
    in_width = w_in.shape[-1]
    assert in_width == COL_MG + 3 * D_MODEL
    assert COL_CI % PREP_CHUNK == 0 and COL_CO % PREP_CHUNK == PREP_SHIFT == COL_MG % PREP_CHUNK
    w_t = jnp.swapaxes(w_in, 1, 2)
    assert in_width % PREP_SHIFT == 0 and PREP_SHIFT == SUBLANES
    last_rows = in_width // PREP_SHIFT - 1
    groups_per_chunk = PREP_CHUNK // PREP_SHIFT

    def src_block(j):
        return jnp.where(j < J_CIF, j, jnp.where(j == J_CIF, N_MAIN, j - 1))

    return pl.pallas_call(
        _weight_prep_kernel,
        grid=(DEPTH, J_MG + N_MG),
        in_specs=[
            pl.BlockSpec((None, PREP_CHUNK, D_MODEL), lambda l, j: (l, src_block(j), 0)),
            pl.BlockSpec((None, PREP_SHIFT, D_MODEL),
                         lambda l, j: (l, jnp.minimum((src_block(j) + 1) * groups_per_chunk,
                                                      last_rows), 0)),
        ],
        out_specs=[
            pl.BlockSpec((None, D_MODEL, PREP_CHUNK), lambda l, j: (l, 0, jnp.minimum(j, J_CIF))),
            pl.BlockSpec((None, D_MODEL, PREP_CHUNK), lambda l, j: (l, 0, jnp.maximum(j - J_MG, 0))),
        ],
        out_shape=[
            jax.ShapeDtypeStruct((DEPTH, D_MODEL, ZCAT_W), BF16),
            jax.ShapeDtypeStruct((DEPTH, D_MODEL, 3 * D_MODEL), BF16),
        ],
        compiler_params=pltpu.CompilerParams(
            dimension_semantics=("arbitrary", "arbitrary"), vmem_limit_bytes=VMEM_LIMIT),
        name="weight_prep",
    )(w_t, w_t)


def _col_chunks(width, step):
    return [(o, min(step, width - o)) for o in range(0, width, step)]


class _BiasedCols:
    def __init__(self, ref, b_ref, base):
        self._ref, self._b_ref, self._base = ref, b_ref, base

    def __getitem__(self, idx):
        rows, cols = idx
        return self._ref[rows, cols] + self._b_ref[:, self._base + cols.start:self._base + cols.stop]


def _inproj_pieces(get_h, w_ref, b_ref, za_ref, zb_ref, zc_ref, step):
    def piece(o_ref, off, woff, w):
        def run():
            acc = _dot(get_h(), w_ref[:, woff:woff + w])
            o_ref[:, off:off + w] = acc if b_ref is None else acc + b_ref[:, woff:woff + w]
        return run
    pieces = []
    base = 0
    for o_ref, width in ((za_ref, ZA_W), (zb_ref, ZB_W), (zc_ref, ZC_W)):
        pieces += [piece(o_ref, off, base + off, w) for off, w in _col_chunks(width, step)]
        base += width
    return pieces


def _inproj_kernel(x_ref, mod_ref, ng_ref, w_ref, b_ref, za_ref, zb_ref, zc_ref):
    h = _modulated_norm(x_ref[...], mod_ref, ng_ref)
    for piece in _inproj_pieces(lambda: h, w_ref, b_ref, za_ref, zb_ref, zc_ref, 512):
        piece()


def _mod_spec(tm, tokens_per_batch):
    if tokens_per_batch is None:
        return pl.BlockSpec((tm, 3 * D_MODEL), lambda i: (i, 0))
    tiles_per_batch = tokens_per_batch // tm
    return pl.BlockSpec((None, 1, 3 * D_MODEL), lambda i: (i // tiles_per_batch, 0, 0))


def _layer_weight_spec(layer, rows, cols):
    return pl.BlockSpec((None, rows, cols), lambda i: (layer, 0, 0), pipeline_mode=pl.Buffered(1))


def _inproj_call(layer, x2, mod, ng, wcat, bcat, tokens_per_batch):
    ntok = x2.shape[0]
    tm = PROJ_TILE
    const = lambda i: (0, 0)
    return pl.pallas_call(
        _inproj_kernel,
        grid=(ntok // tm,),
        in_specs=[
            pl.BlockSpec((tm, D_MODEL), lambda i: (i, 0)),
            _mod_spec(tm, tokens_per_batch),
            pl.BlockSpec((1, D_MODEL), const),
            _layer_weight_spec(layer, D_MODEL, ZCAT_W),
            pl.BlockSpec((1, ZCAT_W), const),
        ],
        out_specs=[
            pl.BlockSpec((tm, ZA_W), lambda i: (i, 0)),
            pl.BlockSpec((tm, ZB_W), lambda i: (i, 0)),
            pl.BlockSpec((tm, ZC_W), lambda i: (i, 0)),
        ],
        out_shape=[
            jax.ShapeDtypeStruct((ntok, ZA_W), F32),
            jax.ShapeDtypeStruct((ntok, ZB_W), F32),
            jax.ShapeDtypeStruct((ntok, ZC_W), F32),
        ],
        compiler_params=pltpu.CompilerParams(
            dimension_semantics=("arbitrary",), vmem_limit_bytes=VMEM_LIMIT),
        name="in_projection",
    )(x2, mod, ng, wcat, bcat)


def _outproj_kernel(x_ref, mod_ref, ng_ref, y_ref, wmg_ref, bmg_ref, wa_ref, wb_ref, wc_ref,
                    wo_ref, o_ref):
    x = x_ref[...]
    h = _modulated_norm(x, mod_ref, ng_ref)
    merged = None
    for i, wbr_ref in enumerate((wa_ref, wb_ref, wc_ref)):
        cols = slice(i * D_MODEL, (i + 1) * D_MODEL)
        gate = _sigmoid(_dot(h, wmg_ref[:, cols]) + bmg_ref[:, cols])
        term = gate * _dot(y_ref[:, i * A_WIDTH:(i + 1) * A_WIDTH], wbr_ref[...])
        merged = term if merged is None else merged + term
    ada_gate = mod_ref[:, 2 * D_MODEL:3 * D_MODEL]
    o_ref[...] = x + ada_gate * _dot(merged.astype(BF16), wo_ref[...])


def _outproj_call(layer, x2, mod, ng, y, wmg, bmg, wa, wb, wc, wo, tokens_per_batch):
    ntok = x2.shape[0]
    tm = PROJ_TILE
    const = lambda i: (0, 0)
    once = pl.Buffered(1)
    return pl.pallas_call(
        _outproj_kernel,
        grid=(ntok // tm,),
        in_specs=[
            pl.BlockSpec((tm, D_MODEL), lambda i: (i, 0)),
            _mod_spec(tm, tokens_per_batch),
            pl.BlockSpec((1, D_MODEL), const),
            pl.BlockSpec((tm, Y_W), lambda i: (i, 0)),
            _layer_weight_spec(layer, D_MODEL, 3 * D_MODEL),
            pl.BlockSpec((1, 3 * D_MODEL), const),
            pl.BlockSpec((A_WIDTH, D_MODEL), const, pipeline_mode=once),
            pl.BlockSpec((B_WIDTH, D_MODEL), const, pipeline_mode=once),
            pl.BlockSpec((C_WIDTH, D_MODEL), const, pipeline_mode=once),
            pl.BlockSpec((D_MODEL, D_MODEL), const, pipeline_mode=once),
        ],
        out_specs=pl.BlockSpec((tm, D_MODEL), lambda i: (i, 0)),
        out_shape=jax.ShapeDtypeStruct((ntok, D_MODEL), F32),
        compiler_params=pltpu.CompilerParams(
            dimension_semantics=("arbitrary",), vmem_limit_bytes=VMEM_LIMIT),
        name="out_projection",
    )(x2, mod, ng, y, wmg, bmg, wa, wb, wc, wo)


def _place_q_head(qn, h, rows):
    lane = lax.broadcasted_iota(jnp.int32, (rows, LANES), 1)
    slab = qn[:, (h // 2) * LANES:(h // 2 + 1) * LANES]
    src_hi = h % 2
    dst_hi = h // (B_HEADS // B_KV_HEADS)
    keep = (lane >= B_HEAD_DIM) if src_hi else (lane < B_HEAD_DIM)
    slab = jnp.where(keep, slab, 0.0)
    if src_hi != dst_hi:
        slab = pltpu.roll(slab, B_HEAD_DIM, 1)
    return slab


def _merge_head_pair(o_even, o_odd, h_even, rows):
    lane_lo = lax.broadcasted_iota(jnp.int32, (rows, LANES), 1) < B_HEAD_DIM
    kv_hi = h_even // (B_HEADS // B_KV_HEADS)
    if kv_hi:
        o_even = pltpu.roll(o_even, B_HEAD_DIM, 1)
    else:
        o_odd = pltpu.roll(o_odd, B_HEAD_DIM, 1)
    return jnp.where(lane_lo, o_even, o_odd)


def _conv_taps(xbuf, cw_ref, cb_ref, cols, ts):
    y = cb_ref[:, cols]
    for j in range(C_CONV):
        lo = SUBLANES - (C_CONV - 1) + j
        y = y + cw_ref[j:j + 1, cols] * xbuf[lo:lo + ts, cols]
    return y


def _prompt_mix_kernel(sink_ref, za_ref, zb_ref, zc_ref, vg_ref, gw_ref, gbs_ref, qg_ref, kg_ref,
                       cw_ref, cb_ref, fb_ref, hg_ref, tril_ref, band_ref, tri01_ref, tribias_ref,
                       y_ref, ko_ref, vo_ref, convo_ref, c_ref, n_ref, m_ref,
                       kprev, vprev, xbuf, first_tile, pump):
    ts = PROMPT_TILE

    wts = [(gw_ref[gi] * tril_ref[...]).astype(BF16) for gi in range(A_GROUPS)]
    for c in range(ts // WINDOW):
        rows = slice(c * WINDOW, (c + 1) * WINDOW)
        vnb = (_rms(za_ref[rows, A_WIDTH:2 * A_WIDTH]) * vg_ref[...]).astype(BF16)
        s = jnp.concatenate(
            [_dot(wts[gi], vnb[:, gi * GROUP_DIM:(gi + 1) * GROUP_DIM]) + gbs_ref[:, gi:gi + 1]
             for gi in range(A_GROUPS)], axis=1)
        sg = _silu(za_ref[rows, 2 * A_WIDTH:3 * A_WIDTH])
        y_ref[rows, 0:A_WIDTH] = (za_ref[rows, 0:A_WIDTH] * s * sg).astype(BF16)
        pump()

    kn = _qk_norm(zb_ref[:, B_WIDTH:B_WIDTH + B_KV_WIDTH], kg_ref[...])
    vv = zb_ref[:, B_WIDTH + B_KV_WIDTH:B_WIDTH + 2 * B_KV_WIDTH]
    pump()
    grp = B_HEADS // B_KV_HEADS
    nblk = ts // WINDOW
    lane_lo2 = lax.broadcasted_iota(jnp.int32, (2 * WINDOW, LANES), 1) < B_HEAD_DIM
    kblocks = [kprev[...]] + [kn[b * WINDOW:(b + 1) * WINDOW] for b in range(nblk)]
    vblocks = [vprev[...]] + [vv[b * WINDOW:(b + 1) * WINDOW] for b in range(nblk)]
    heads = [(kh, g) for kh in range(B_KV_HEADS) for g in range(grp)]
    snk = {k: sink_ref[k[0] * grp + k[1]] for k in heads}
    for blk in range(nblk):
        rows = slice(blk * WINDOW, (blk + 1) * WINDOW)
        if blk == 0 and first_tile is not False:
            bias = jnp.where(first_tile, band_ref[1], band_ref[0])
        else:
            bias = band_ref[0]
        kcat = jnp.concatenate([kblocks[blk], kblocks[blk + 1]], axis=0)
        vcat = jnp.concatenate([vblocks[blk], vblocks[blk + 1]], axis=0)
        krol = pltpu.roll(kcat, B_HEAD_DIM, 1)
        vrol = pltpu.roll(vcat, B_HEAD_DIM, 1)
        kdup, vdup = [], []
        for kh in range(B_KV_HEADS):
            own = lane_lo2 if kh == 0 else jnp.logical_not(lane_lo2)
            kdup.append(jnp.where(own, kcat, krol).astype(BF16))
            vdup.append(jnp.where(own, vcat, vrol).astype(BF16))
        qn = _qk_norm(zb_ref[rows, 0:B_WIDTH], qg_ref[...]) * (B_HEAD_DIM ** -0.5)
        pump()
        qs = [jnp.concatenate([_place_q_head(qn, kh * grp + g, WINDOW) for g in range(grp)],
                              axis=0).astype(BF16) for kh in range(B_KV_HEADS)]
        logits = [_dot_nt(qs[kh], kdup[kh]) for kh in range(B_KV_HEADS)]
        pump()
        lg = {(kh, g): logits[kh][g * WINDOW:(g + 1) * WINDOW] + bias for kh, g in heads}
        mx = {k: jnp.maximum(jnp.max(lg[k], axis=-1, keepdims=True), snk[k]) for k in heads}
        pump()
        p = {k: jnp.exp(lg[k] - mx[k]) for k in heads}
        pump()
        rden = {k: 1.0 / (jnp.sum(p[k], axis=-1, keepdims=True) + jnp.exp(snk[k] - mx[k]))
                for k in heads}
        pump()
        pv = [_dot(jnp.concatenate([p[kh, g].astype(BF16) for g in range(grp)], axis=0), vdup[kh])
              for kh in range(B_KV_HEADS)]
        pump()
        outs = {(kh, g): pv[kh][g * WINDOW:(g + 1) * WINDOW] * rden[kh, g] for kh, g in heads}
        yb = jnp.concatenate(
            [_merge_head_pair(outs[(2 * j) // grp, (2 * j) % grp],
                              outs[(2 * j + 1) // grp, (2 * j + 1) % grp], 2 * j, WINDOW)
             for j in range(B_HEADS // 2)], axis=1)
        sgb = _silu(zb_ref[rows, B_WIDTH + 2 * B_KV_WIDTH:ZB_W])
        y_ref[rows, A_WIDTH:A_WIDTH + B_WIDTH] = (yb * sgb).astype(BF16)
        pump()
    kprev[...] = kblocks[nblk]
    vprev[...] = vblocks[nblk]
    ko_ref[...] = kblocks[nblk]
    vo_ref[...] = vblocks[nblk]
    pump()

    xbuf[SUBLANES:SUBLANES + ts, :] = zc_ref[:, 0:2 * C_WIDTH]
    pump()
    ifp = zc_ref[:, 5 * C_WIDTH:5 * C_WIDTH + LANES]
    lf = _log_sigmoid(ifp + fb_ref[...])
    pump()
    cl = MLSTM_CHUNK
    lane_c = lax.broadcasted_iota(jnp.int32, (cl, LANES), 1)
    lane_1 = lax.broadcasted_iota(jnp.int32, (1, LANES), 1)
    m_row = m_ref[...]
    m_out = m_row
    cum_all = _dot_exact01(tri01_ref[...], lf)
    st_col = jnp.where(lane_c < C_HEADS, ifp, cum_all)
    st_row = st_col.T
    tribias = tribias_ref[...]
    pump()
    for hds in MLSTM_HEAD_GROUPS:
        hs = {hd: slice(hd * C_HEAD_DIM, (hd + 1) * C_HEAD_DIM) for hd in hds}
        i_c = {hd: st_col[:, hd:hd + 1] for hd in hds}
        cum_c = {hd: st_col[:, C_HEADS + hd:C_HEADS + hd + 1] for hd in hds}
        i_r = {hd: st_row[hd:hd + 1, :] for hd in hds}
        cum_r = {hd: st_row[C_HEADS + hd:C_HEADS + hd + 1, :] for hd in hds}
        m_prev = {hd: m_row[:, hd:hd + 1] for hd in hds}
        dmat = {hd: cum_c[hd] - cum_r[hd] + i_r[hd] + tribias for hd in hds}
        m_inter = {hd: cum_c[hd] + m_prev[hd] for hd in hds}
        m_t = {hd: jnp.maximum(m_inter[hd], jnp.max(dmat[hd], axis=-1, keepdims=True)) for hd in hds}
        pump()
        q_h = {hd: _silu(_conv_taps(xbuf, cw_ref, cb_ref, hs[hd], ts)).astype(BF16) for hd in hds}
        k_h = {hd: _silu(_conv_taps(xbuf, cw_ref, cb_ref,
                                    slice(C_WIDTH + hs[hd].start, C_WIDTH + hs[hd].stop), ts))
               * (C_HEAD_DIM ** -0.5) for hd in hds}
        pump()
        v_h = {hd: zc_ref[:, 2 * C_WIDTH + hd * C_HEAD_DIM:2 * C_WIDTH + (hd + 1) * C_HEAD_DIM].astype(BF16)
               for hd in hds}
        s_qk = {hd: _dot_nt(q_h[hd], k_h[hd].astype(BF16)) for hd in hds}
        a = {hd: jnp.exp(dmat[hd] - m_t[hd]) * s_qk[hd] for hd in hds}
        pump()
        w_inter = {hd: jnp.exp(m_inter[hd] - m_t[hd]) for hd in hds}
        c_prev = {hd: c_ref[hd] for hd in hds}
        n_prev = {hd: n_ref[hd:hd + 1, :] for hd in hds}
        inter = {hd: _dot(q_h[hd], c_prev[hd].astype(BF16)) for hd in hds}
        intra = {hd: _dot(a[hd].astype(BF16), v_h[hd]) for hd in hds}
        pump()
        den = {hd: jnp.sum(a[hd], axis=-1, keepdims=True)
               + w_inter[hd] * jnp.sum(q_h[hd].astype(F32) * n_prev[hd], axis=-1, keepdims=True)
               for hd in hds}
        rnorm = {hd: 1.0 / jnp.maximum(jnp.abs(den[hd]), jnp.exp(-m_t[hd])) for hd in hds}
        hh = {hd: (intra[hd] + w_inter[hd] * inter[hd]) * rnorm[hd] for hd in hds}
        pump()
        for hd in hds:
            o_cols = slice(3 * C_WIDTH + hd * C_HEAD_DIM, 3 * C_WIDTH + (hd + 1) * C_HEAD_DIM)
            g_cols = slice(4 * C_WIDTH + hd * C_HEAD_DIM, 4 * C_WIDTH + (hd + 1) * C_HEAD_DIM)
            gate_o = _sigmoid(zc_ref[:, o_cols]) * _silu(zc_ref[:, g_cols])
            y_cols = slice(A_WIDTH + B_WIDTH + hd * C_HEAD_DIM, A_WIDTH + B_WIDTH + (hd + 1) * C_HEAD_DIM)
            y_ref[:, y_cols] = (_rms(hh[hd]) * hg_ref[:, hs[hd]] * gate_o).astype(BF16)
        pump()
        total = {hd: cum_r[hd][:, cl - 1:cl] for hd in hds}
        g_r = {hd: total[hd] - cum_r[hd] + i_r[hd] for hd in hds}
        g_c = {hd: total[hd] - cum_c[hd] + i_c[hd] for hd in hds}
        m_new = {hd: jnp.maximum(total[hd] + m_prev[hd], jnp.max(g_r[hd], axis=-1, keepdims=True))
                 for hd in hds}
        kw = {hd: jnp.exp(g_c[hd] - m_new[hd]) * k_h[hd] for hd in hds}
        decay = {hd: jnp.exp(total[hd] + m_prev[hd] - m_new[hd]) for hd in hds}
        pump()
        upd = {hd: _dot(kw[hd].T.astype(BF16), v_h[hd]) for hd in hds}
        for hd in hds:
            c_ref[hd] = decay[hd] * c_prev[hd] + upd[hd]
            n_ref[hd:hd + 1, :] = decay[hd] * n_prev[hd] + jnp.sum(kw[hd], axis=0, keepdims=True)
            m_out = jnp.where(lane_1 == hd, m_new[hd], m_out)
        pump()
    m_ref[...] = m_out
    tail = xbuf[ts:ts + SUBLANES, :]
    xbuf[0:SUBLANES, :] = tail
    convo_ref[...] = tail


def _prompt_mask_constants():
    r = np.arange(WINDOW)[:, None]
    c = np.arange(2 * WINDOW)[None, :]
    band = (c > r) & (c <= r + WINDOW)
    band_first = band & (c >= WINDOW)
    band_bias = np.where(np.stack([band, band_first]), 0.0, NEG).astype(np.float32)
    tril = (np.arange(WINDOW)[:, None] >= np.arange(WINDOW)[None, :]).astype(np.float32)
    tri = np.arange(MLSTM_CHUNK)[:, None] >= np.arange(MLSTM_CHUNK)[None, :]
    return (jnp.asarray(tril), jnp.asarray(band_bias), jnp.asarray(tri, dtype=BF16),
            jnp.asarray(np.where(tri, 0.0, NEG).astype(np.float32)))


N_MIX_PARAMS = 13
MLSTM_HEAD_GROUPS = ((0, 1), (2, 3))
MIX_PUMP_CALLS = 37
TAIL_FILL_PIECES = 8
MXU_PIECE_COLS = 256


class _Interleaver:
    def __init__(self, pieces, calls, hold_back=0):
        self._pieces = list(pieces)
        self._hold_back = hold_back
        self._spread = len(self._pieces) - hold_back
        self._emitted = 0
        self._calls = calls
        self._call = 0

    def __call__(self):
        self._call += 1
        target = (self._call * self._spread) // self._calls
        while self._emitted < target:
            self._pieces.pop(0)()
            self._emitted += 1

    def finish(self):
        assert self._call == self._calls and len(self._pieces) == self._hold_back, self._call
        return self._pieces


def _gate_pieces(h_ref, wmg_ref, bmg_ref, g_ref):
    def piece(off):
        cols = slice(off, off + MXU_PIECE_COLS)
        def run():
            g_ref[:, cols] = _sigmoid(_dot(h_ref[...], wmg_ref[:, cols]) + bmg_ref[:, cols])
        return run
    return [piece(off) for off in range(0, 3 * D_MODEL, MXU_PIECE_COLS)]


def _merge_and_project(x, mod_ref, g_ref, y_ref, wa_ref, wb_ref, wc_ref, wo_ref, fillers=()):
    fillers = list(fillers)
    per_stage = -(-len(fillers) // 4)
    merged = None
    for i, wbr_ref in enumerate((wa_ref, wb_ref, wc_ref)):
        for piece in fillers[i * per_stage:(i + 1) * per_stage]:
            piece()
        term = (g_ref[:, i * D_MODEL:(i + 1) * D_MODEL]
                * _dot(y_ref[:, i * A_WIDTH:(i + 1) * A_WIDTH], wbr_ref[...]))
        merged = term if merged is None else merged + term
    for piece in fillers[3 * per_stage:]:
        piece()
    ada_gate = mod_ref[:, 2 * D_MODEL:3 * D_MODEL]
    return x + ada_gate * _dot(merged.astype(BF16), wo_ref[...])


def _prompt_layer_kernel(tiles_per_seq, sink_ref, x2_ref, xn_ref, mod_ref, modn_ref, ng_ref,
                         wcat_ref, bcat_ref, *rest):
    mix_params = rest[:N_MIX_PARAMS]
    wmg_ref, bmg_ref, wa_ref, wb_ref, wc_ref, wo_ref = rest[N_MIX_PARAMS:N_MIX_PARAMS + 6]
    o_ref, ko_ref, vo_ref, convo_ref, c_ref, n_ref, m_ref = rest[N_MIX_PARAMS + 6:N_MIX_PARAMS + 13]
    (za0, zb0, zc0, za1, zb1, zc1, h0, h1, y_scr, g_scr, kprev, vprev, xbuf) = rest[N_MIX_PARAMS + 13:]
    ts = PROMPT_TILE
    z = ((za0, zb0, zc0), (za1, zb1, zc1))
    h = (h0, h1)
    k = pl.program_id(0)
    seq_start = (k % (tiles_per_seq // 2)) == 0

    @pl.when(k == 0)
    def _():
        h0[...] = _modulated_norm(x2_ref[0:ts, :], mod_ref, ng_ref)
        for piece in _inproj_pieces(lambda: h0[...], wcat_ref, None, *z[0], 512):
            piece()

    @pl.when(seq_start)
    def _():
        kprev[...] = jnp.zeros_like(kprev)
        vprev[...] = jnp.zeros_like(vprev)
        xbuf[0:SUBLANES, :] = jnp.zeros((SUBLANES, 2 * C_WIDTH), F32)
        c_ref[...] = jnp.zeros_like(c_ref)
        n_ref[...] = jnp.zeros_like(n_ref)
        m_ref[...] = jnp.zeros_like(m_ref)

    for half in range(2):
        cur, nxt = half, 1 - half
        rows = slice(half * ts, (half + 1) * ts)
        if half == 0:
            h[nxt][...] = _modulated_norm(x2_ref[ts:2 * ts, :], mod_ref, ng_ref)
        else:
            h[nxt][...] = _modulated_norm(xn_ref[...], modn_ref, ng_ref)
        get_h_next = functools.partial(lambda r: r[...], h[nxt])
        hold = TAIL_FILL_PIECES if half == 1 else 0
        proj = _inproj_pieces(get_h_next, wcat_ref, None, *z[nxt], MXU_PIECE_COLS)
        pump = _Interleaver(
            proj[:len(proj) - hold] + _gate_pieces(h[cur], wmg_ref, bmg_ref, g_scr)
            + proj[len(proj) - hold:], MIX_PUMP_CALLS, hold_back=hold)
        z_cur = [_BiasedCols(ref, bcat_ref, base)
                 for ref, base in zip(z[cur], (0, ZA_W, ZA_W + ZB_W))]
        _prompt_mix_kernel(sink_ref, *z_cur, *mix_params,
                           y_scr, ko_ref, vo_ref, convo_ref, c_ref, n_ref, m_ref, kprev, vprev, xbuf,
                           first_tile=seq_start if half == 0 else False, pump=pump)
        o_ref[rows, :] = _merge_and_project(x2_ref[rows, :], mod_ref, g_scr, y_scr,
                                            wa_ref, wb_ref, wc_ref, wo_ref, fillers=pump.finish())


def _prompt_layer_call(layer, x2, mod, lw, batch, seq):
    ts = PROMPT_TILE
    nt = seq // ts
    assert nt % 2 == 0
    last_tile = batch * nt - 1
    const2 = lambda k: (0, 0)
    const3 = lambda k: (0, 0, 0)
    per_b3 = lambda k: ((2 * k) // nt, 0, 0)
    next_tile = lambda k: jnp.minimum(2 * k + 2, last_tile)
    once = pl.Buffered(1)
    return pl.pallas_call(
        functools.partial(_prompt_layer_kernel, nt),
        grid=(batch * nt // 2,),
        in_specs=[
            pl.BlockSpec(memory_space=pltpu.SMEM),
            pl.BlockSpec((2 * ts, D_MODEL), lambda k: (k, 0)),
            pl.BlockSpec((ts, D_MODEL), lambda k: (next_tile(k), 0)),
            pl.BlockSpec((None, 1, 3 * D_MODEL), per_b3),
            pl.BlockSpec((None, 1, 3 * D_MODEL), lambda k: (next_tile(k) // nt, 0, 0)),
            pl.BlockSpec((1, D_MODEL), const2),
            _layer_weight_spec(layer, D_MODEL, ZCAT_W),
            pl.BlockSpec((1, ZCAT_W), const2),
            pl.BlockSpec((1, A_WIDTH), const2),
            pl.BlockSpec((A_GROUPS, WINDOW, WINDOW), const3),
            pl.BlockSpec((WINDOW, LANES), const2),
            pl.BlockSpec((1, B_WIDTH), const2),
            pl.BlockSpec((1, B_KV_WIDTH), const2),
            pl.BlockSpec((C_CONV, 2 * C_WIDTH), const2),
            pl.BlockSpec((1, 2 * C_WIDTH), const2),
            pl.BlockSpec((1, LANES), const2),
            pl.BlockSpec((1, C_WIDTH), const2),
            pl.BlockSpec((WINDOW, WINDOW), const2),
            pl.BlockSpec((2, WINDOW, 2 * WINDOW), const3),
            pl.BlockSpec((MLSTM_CHUNK, MLSTM_CHUNK), const2),
            pl.BlockSpec((MLSTM_CHUNK, MLSTM_CHUNK), const2),
            _layer_weight_spec(layer, D_MODEL, 3 * D_MODEL),
            pl.BlockSpec((1, 3 * D_MODEL), const2),
            pl.BlockSpec((A_WIDTH, D_MODEL), const2, pipeline_mode=once),
            pl.BlockSpec((B_WIDTH, D_MODEL), const2, pipeline_mode=once),
            pl.BlockSpec((C_WIDTH, D_MODEL), const2, pipeline_mode=once),
            pl.BlockSpec((D_MODEL, D_MODEL), const2, pipeline_mode=once),
        ],
        out_specs=[
            pl.BlockSpec((2 * ts, D_MODEL), lambda k: (k, 0)),
            pl.BlockSpec((None, WINDOW, B_KV_WIDTH), per_b3),
            pl.BlockSpec((None, WINDOW, B_KV_WIDTH), per_b3),
            pl.BlockSpec((None, SUBLANES, 2 * C_WIDTH), per_b3),
            pl.BlockSpec((None, C_HEADS, C_HEAD_DIM, C_HEAD_DIM), lambda k: ((2 * k) // nt, 0, 0, 0)),
            pl.BlockSpec((None, C_HEADS, C_HEAD_DIM), per_b3),
            pl.BlockSpec((None, 1, LANES), per_b3),
        ],
        out_shape=[
            jax.ShapeDtypeStruct((batch * seq, D_MODEL), F32),
            jax.ShapeDtypeStruct((batch, WINDOW, B_KV_WIDTH), F32),
            jax.ShapeDtypeStruct((batch, WINDOW, B_KV_WIDTH), F32),
            jax.ShapeDtypeStruct((batch, SUBLANES, 2 * C_WIDTH), F32),
            jax.ShapeDtypeStruct((batch, C_HEADS, C_HEAD_DIM, C_HEAD_DIM), F32),
            jax.ShapeDtypeStruct((batch, C_HEADS, C_HEAD_DIM), F32),
            jax.ShapeDtypeStruct((batch, 1, LANES), F32),
        ],
        scratch_shapes=(
            [pltpu.VMEM((ts, w), F32) for w in (ZA_W, ZB_W, ZC_W)] * 2
            + [pltpu.VMEM((ts, D_MODEL), BF16)] * 2
            + [pltpu.VMEM((ts, Y_W), BF16),
               pltpu.VMEM((ts, 3 * D_MODEL), F32),
               pltpu.VMEM((WINDOW, B_KV_WIDTH), F32),
               pltpu.VMEM((WINDOW, B_KV_WIDTH), F32),
               pltpu.VMEM((ts + SUBLANES, 2 * C_WIDTH), F32)]),
        compiler_params=pltpu.CompilerParams(
            dimension_semantics=("arbitrary",), vmem_limit_bytes=VMEM_LIMIT),
        name="prompt_layer",
    )(lw["sinks"], x2, x2, mod, mod, lw["ng"], lw["wcat"], lw["bcat"],
      lw["vg"], lw["gws"], lw["gbs_col"], lw["qg"], lw["kg"], lw["cw"], lw["cb"], lw["fb"], lw["hg"],
      *_prompt_mask_constants(),
      lw["wmg"], lw["bmg"], lw["wa"], lw["wb"], lw["wc"], lw["wo"])


def _sample_mix_kernel(sink_ref, za_ref, zb_ref, zc_ref, kc_ref, vc_ref, cs_ref, c0_ref, n0_ref,
                       m0_ref, vg_ref, gwb_ref, gbs_ref, qg_ref, kg_ref, cw_ref, cb_ref, fb_ref,
                       hg_ref,
                       y_ref, vrow_ref, ko_ref, vo_ref, convo_ref, c1_ref, n1_ref, m1_ref,
                       xbuf):
    nb = SAMPLE_NB
    t = SUBLANES
    rows = nb * t
    tok_r = lax.broadcasted_iota(jnp.int32, (rows, rows), 0)
    tok_c = lax.broadcasted_iota(jnp.int32, (rows, rows), 1)
    same_b = (tok_r // t) == (tok_c // t)
    causal_b = same_b & (tok_c <= tok_r)

    u = za_ref[:, 0:A_WIDTH]
    vn = _rms(za_ref[:, A_WIDTH:2 * A_WIDTH]) * vg_ref[...]
    sg = _silu(za_ref[:, 2 * A_WIDTH:3 * A_WIDTH])
    vrow_ref[...] = vn
    vnb = vn.astype(BF16)
    s_cols = []
    for gi in range(A_GROUPS):
        s_cols.append(_dot(gwb_ref[gi], vnb[:, gi * GROUP_DIM:(gi + 1) * GROUP_DIM])
                      + gbs_ref[:, gi:gi + 1])
    y_ref[:, 0:A_WIDTH] = (u * jnp.concatenate(s_cols, axis=1) * sg).astype(BF16)

    qn = _qk_norm(zb_ref[:, 0:B_WIDTH], qg_ref[...]) * (B_HEAD_DIM ** -0.5)
    kn = _qk_norm(zb_ref[:, B_WIDTH:B_WIDTH + B_KV_WIDTH], kg_ref[...])
    vv = zb_ref[:, B_WIDTH + B_KV_WIDTH:B_WIDTH + 2 * B_KV_WIDTH]
    sgb = _silu(zb_ref[:, B_WIDTH + 2 * B_KV_WIDTH:ZB_W])
    kn3 = kn.reshape(nb, t, B_KV_WIDTH)
    vv3 = vv.reshape(nb, t, B_KV_WIDTH)
    kcache = kc_ref[...]
    vcache = vc_ref[...]
    pad = jnp.zeros((nb, WINDOW - t, B_KV_WIDTH), F32)
    kall = jnp.concatenate([kcache, kn3, pad], axis=1).astype(BF16)
    vall = jnp.concatenate([vcache, vv3, pad], axis=1).astype(BF16)
    qp = jnp.concatenate([_place_q_head(qn, h, rows).reshape(nb, t, LANES) for h in range(B_HEADS)],
                         axis=1).astype(BF16)
    logits = lax.dot_general(qp, kall, (((2,), (2,)), ((0,), (0,))), preferred_element_type=F32)
    qrow = lax.broadcasted_iota(jnp.int32, (nb, B_HEADS * t, 2 * WINDOW), 1)
    kcol = lax.broadcasted_iota(jnp.int32, (nb, B_HEADS * t, 2 * WINDOW), 2)
    qt = qrow % t
    valid = ((kcol < WINDOW) & (kcol > qt)) | ((kcol >= WINDOW) & ((kcol - WINDOW) <= qt))
    hrow = lax.broadcasted_iota(jnp.int32, (B_HEADS * t, 1), 0) // t
    snk = jnp.zeros((B_HEADS * t, 1), F32)
    for h in range(B_HEADS):
        snk = jnp.where(hrow == h, sink_ref[h], snk)
    lg = jnp.where(valid, logits, NEG)
    mx = jnp.maximum(jnp.max(lg, axis=-1, keepdims=True), snk[None])
    p = jnp.exp(lg - mx)
    den = jnp.sum(p, axis=-1, keepdims=True) + jnp.exp(snk[None] - mx)
    pv = lax.dot_general(p.astype(BF16), vall, (((2,), (1,)), ((0,), (0,))),
                         preferred_element_type=F32) / den
    head_out = [pv[:, h * t:(h + 1) * t, :].reshape(rows, LANES) for h in range(B_HEADS)]
    yb = jnp.concatenate(
        [_merge_head_pair(head_out[2 * j], head_out[2 * j + 1], 2 * j, rows)
         for j in range(B_HEADS // 2)], axis=1)
    y_ref[:, A_WIDTH:A_WIDTH + B_WIDTH] = (yb * sgb).astype(BF16)
    ko_ref[...] = jnp.concatenate([kcache[:, t:, :], kn3], axis=1)
    vo_ref[...] = jnp.concatenate([vcache[:, t:, :], vv3], axis=1)

    xbuf[:, SUBLANES - (C_CONV - 1):SUBLANES, :] = cs_ref[...]
    xbuf[:, SUBLANES:2 * SUBLANES, :] = zc_ref[:, 0:2 * C_WIDTH].reshape(nb, t, 2 * C_WIDTH)
    y3 = cb_ref[...][None]
    for j in range(C_CONV):
        lo = SUBLANES - (C_CONV - 1) + j
        y3 = y3 + cw_ref[j:j + 1, :][None] * xbuf[:, lo:lo + t, :]
    convo_ref[...] = xbuf[:, 2 * SUBLANES - (C_CONV - 1):2 * SUBLANES, :]
    qk = _silu(y3.reshape(rows, 2 * C_WIDTH))
    qall = qk[:, 0:C_WIDTH].astype(BF16)
    kall_c = qk[:, C_WIDTH:2 * C_WIDTH] * (C_HEAD_DIM ** -0.5)
    vall_c = zc_ref[:, 2 * C_WIDTH:3 * C_WIDTH].astype(BF16)
    gate_o = _sigmoid(zc_ref[:, 3 * C_WIDTH:4 * C_WIDTH]) * _silu(zc_ref[:, 4 * C_WIDTH:5 * C_WIDTH])
    ifp = zc_ref[:, 5 * C_WIDTH:5 * C_WIDTH + LANES]
    lf = _log_sigmoid(ifp + fb_ref[...])
    lane_t = lax.broadcasted_iota(jnp.int32, (rows, LANES), 1)
    cum_all = _dot_exact01(jnp.where(causal_b, 1.0, 0.0).astype(BF16), lf)
    tot_all = _dot_exact01(jnp.where(same_b, 1.0, 0.0).astype(BF16), lf)
    st_col = jnp.where(lane_t < C_HEADS, ifp, cum_all)
    st_row = st_col.T
    tot_row = tot_all.T
    m0 = m0_ref[...]
    same_b_bf = jnp.where(same_b, 1.0, 0.0).astype(BF16)
    batch_of_lane = lax.broadcasted_iota(jnp.int32, (nb, 1, rows), 2) // t
    batch_id = lax.broadcasted_iota(jnp.int32, (nb, 1, rows), 0)
    own_tok = batch_of_lane == batch_id
    h_cols = []
    m_out = jnp.zeros((rows, LANES), F32)
    for hd in range(C_HEADS):
        hs = slice(hd * C_HEAD_DIM, (hd + 1) * C_HEAD_DIM)
        i_c = st_col[:, hd:hd + 1]
        cum_c = st_col[:, C_HEADS + hd:C_HEADS + hd + 1]
        tot_c = tot_all[:, C_HEADS + hd:C_HEADS + hd + 1]
        i_r = st_row[hd:hd + 1, :]
        cum_r = st_row[C_HEADS + hd:C_HEADS + hd + 1, :]
        tot_r = tot_row[C_HEADS + hd:C_HEADS + hd + 1, :]
        m_prev = m0[:, hd:hd + 1]
        dmat = jnp.where(causal_b, cum_c - cum_r + i_r, NEG)
        m_inter = cum_c + m_prev
        m_t = jnp.maximum(m_inter, jnp.max(dmat, axis=-1, keepdims=True))
        q_h = qall[:, hs]
        k_h = kall_c[:, hs]
        v_h = vall_c[:, hs]
        a = jnp.exp(dmat - m_t) * _dot_nt(q_h, k_h.astype(BF16))
        w_inter = jnp.exp(m_inter - m_t)
        c_prev = c0_ref[:, hd]
        n_tok = jnp.broadcast_to(n0_ref[hd][:, None, :], (nb, t, C_HEAD_DIM)).reshape(rows, C_HEAD_DIM)
        inter = lax.dot_general(q_h.reshape(nb, t, C_HEAD_DIM), c_prev.astype(BF16),
                                (((2,), (1,)), ((0,), (0,))), preferred_element_type=F32)
        num = _dot(a.astype(BF16), v_h) + w_inter * inter.reshape(rows, C_HEAD_DIM)
        den = (jnp.sum(a, axis=-1, keepdims=True)
               + w_inter * jnp.sum(q_h.astype(F32) * n_tok, axis=-1, keepdims=True))
        hh = num / jnp.maximum(jnp.abs(den), jnp.exp(-m_t))
        h_cols.append(_rms(hh))
        g_r = tot_r - cum_r + i_r
        g_c = tot_c - cum_c + i_c
        m_new = jnp.maximum(tot_c + m_prev,
                            jnp.max(jnp.where(same_b, g_r, NEG), axis=-1, keepdims=True))
        kw = jnp.exp(g_c - m_new) * k_h
        decay = jnp.exp(tot_c + m_prev - m_new)
        kwt = kw.T
        lhs = jnp.where(own_tok, kwt[None], 0.0).astype(BF16).reshape(nb * C_HEAD_DIM, rows)
        upd = _dot(lhs, v_h).reshape(nb, C_HEAD_DIM, C_HEAD_DIM)
        dec_b = jnp.broadcast_to(decay, (rows, C_HEAD_DIM)).reshape(nb, t, C_HEAD_DIM)[:, 0:1, :]
        c1_ref[:, hd] = dec_b * c_prev + upd
        n1_ref[hd] = decay * n_tok + _dot(same_b_bf, kw.astype(BF16))
        m_out = jnp.where(lane_t == hd, m_new, m_out)
    m1_ref[...] = m_out
    hn = jnp.concatenate(h_cols, axis=1) * hg_ref[...]
    y_ref[:, A_WIDTH + B_WIDTH:Y_W] = (hn * gate_o).astype(BF16)


def _sample_mix_call(l, za, zb, zc, kc, vc, cs, c0, n0t, m0tok, lw, nbatch):
    nb = SAMPLE_NB
    t = SUBLANES
    rows = nb * t
    tok = lambda i: (i, 0)
    const2 = lambda i: (0, 0)
    const3 = lambda i: (0, 0, 0)
    b3 = lambda i: (i, 0, 0)
    lb4 = lambda i: (l, i, 0, 0)
    return pl.pallas_call(
        _sample_mix_kernel,
        grid=(nbatch // nb,),
        in_specs=[
            pl.BlockSpec(memory_space=pltpu.SMEM),
            pl.BlockSpec((rows, ZA_W), tok),
            pl.BlockSpec((rows, ZB_W), tok),
            pl.BlockSpec((rows, ZC_W), tok),
            pl.BlockSpec((None, nb, WINDOW, B_KV_WIDTH), lb4),
            pl.BlockSpec((None, nb, WINDOW, B_KV_WIDTH), lb4),
            pl.BlockSpec((None, nb, C_CONV - 1, 2 * C_WIDTH), lb4),
            pl.BlockSpec((None, nb, C_HEADS, C_HEAD_DIM, C_HEAD_DIM), lambda i: (l, i, 0, 0, 0)),
            pl.BlockSpec((None, C_HEADS, nb, C_HEAD_DIM), lambda i: (l, 0, i, 0)),
            pl.BlockSpec((None, rows, LANES), lambda i: (l, i, 0)),
            pl.BlockSpec((1, A_WIDTH), const2),
            pl.BlockSpec((A_GROUPS, rows, rows), const3),
            pl.BlockSpec((rows, LANES), const2),
            pl.BlockSpec((1, B_WIDTH), const2),
            pl.BlockSpec((1, B_KV_WIDTH), const2),
            pl.BlockSpec((C_CONV, 2 * C_WIDTH), const2),
            pl.BlockSpec((1, 2 * C_WIDTH), const2),
            pl.BlockSpec((1, LANES), const2),
            pl.BlockSpec((1, C_WIDTH), const2),
        ],
        out_specs=[
            pl.BlockSpec((rows, Y_W), tok),
            pl.BlockSpec((rows, A_WIDTH), tok),
            pl.BlockSpec((nb, WINDOW, B_KV_WIDTH), b3),
            pl.BlockSpec((nb, WINDOW, B_KV_WIDTH), b3),
            pl.BlockSpec((nb, C_CONV - 1, 2 * C_WIDTH), b3),
            pl.BlockSpec((nb, C_HEADS, C_HEAD_DIM, C_HEAD_DIM), lambda i: (i, 0, 0, 0)),
            pl.BlockSpec((C_HEADS, rows, C_HEAD_DIM), lambda i: (0, i, 0)),
            pl.BlockSpec((rows, LANES), tok),
        ],
        out_shape=[
            jax.ShapeDtypeStruct((nbatch * t, Y_W), BF16),
            jax.ShapeDtypeStruct((nbatch * t, A_WIDTH), F32),
            jax.ShapeDtypeStruct((nbatch, WINDOW, B_KV_WIDTH), F32),
            jax.ShapeDtypeStruct((nbatch, WINDOW, B_KV_WIDTH), F32),
            jax.ShapeDtypeStruct((nbatch, C_CONV - 1, 2 * C_WIDTH), F32),
            jax.ShapeDtypeStruct((nbatch, C_HEADS, C_HEAD_DIM, C_HEAD_DIM), F32),
            jax.ShapeDtypeStruct((C_HEADS, nbatch * t, C_HEAD_DIM), F32),
            jax.ShapeDtypeStruct((nbatch * t, LANES), F32),
        ],
        scratch_shapes=[pltpu.VMEM((nb, 2 * SUBLANES, 2 * C_WIDTH), F32)],
        compiler_params=pltpu.CompilerParams(
            dimension_semantics=("arbitrary",), vmem_limit_bytes=VMEM_LIMIT),
        name="sample_mixer",
    )(lw["sinks"], za, zb, zc, kc, vc, cs, c0, n0t, m0tok, lw["vg"], lw["gwb"], lw["gbs_tok"],
      lw["qg"], lw["kg"], lw["cw"], lw["cb"], lw["fb"], lw["hg"])


def _layer_weights(l, wcat_all, wmg_all, b_in, gmlp_vnorm_g, gmlp_ws, gmlp_bs, swa_qnorm_g,
                   swa_knorm_g, swa_sinks, mlstm_conv_w, mlstm_conv_b, mlstm_f_bias, mlstm_hnorm_g,
                   w_branch_a, w_branch_b, w_branch_c, w_out, norm_g, dec_seq):
    bl = b_in[l]
    bcat = jnp.concatenate([bl[:COL_CI], bl[COL_CO:COL_MG], bl[COL_CI:COL_CO],
                            jnp.zeros((LANES - 2 * C_HEADS,), F32)])
    t = dec_seq
    nb = SAMPLE_NB
    ws_t = gmlp_ws[l][:, :t, :t] * jnp.tril(jnp.ones((t, t), F32))
    eye = jnp.eye(nb, dtype=F32)
    gwb = jnp.einsum("bc,gts->gbtcs", eye, ws_t).reshape(A_GROUPS, nb * t, nb * t).astype(BF16)
    gbs_col = jnp.pad(gmlp_bs[l].T, ((0, 0), (0, LANES - A_GROUPS)))
    gbs_tok = jnp.pad(jnp.tile(gmlp_bs[l][:, :t].T, (nb, 1)), ((0, 0), (0, LANES - A_GROUPS)))
    fb = jnp.pad(mlstm_f_bias[l], (C_HEADS, LANES - 2 * C_HEADS)).reshape(1, LANES)
    return dict(
        ng=norm_g[l].reshape(1, D_MODEL),
        wcat=wcat_all, bcat=bcat.reshape(1, ZCAT_W),
        wmg=wmg_all, bmg=bl[COL_MG:].reshape(1, 3 * D_MODEL),
        wa=w_branch_a[l].astype(BF16), wb=w_branch_b[l].astype(BF16),
        wc=w_branch_c[l].astype(BF16), wo=w_out[l].astype(BF16),
        vg=gmlp_vnorm_g[l].reshape(1, A_WIDTH), gws=gmlp_ws[l], gwb=gwb,
        gbs_col=gbs_col, gbs_tok=gbs_tok,
        qg=jnp.tile(swa_qnorm_g[l], B_HEADS).reshape(1, B_WIDTH),
        kg=jnp.tile(swa_knorm_g[l], B_KV_HEADS).reshape(1, B_KV_WIDTH),
        sinks=swa_sinks[l],
        cw=mlstm_conv_w[l], cb=mlstm_conv_b[l].reshape(1, 2 * C_WIDTH), fb=fb,
        hg=mlstm_hnorm_g[l].reshape(1, C_WIDTH),
    )


def kernel(x_prompt, x_sample, cache_swa_k, cache_swa_v, state_mlstm_conv, state_mlstm_C, state_mlstm_n, state_mlstm_m, c_prompt, c_sample, ada_w, ada_b, norm_g, w_in, b_in, gmlp_vnorm_g, gmlp_ws, gmlp_bs, swa_qnorm_g, swa_knorm_g, swa_sinks, mlstm_conv_w, mlstm_conv_b, mlstm_f_bias, mlstm_hnorm_g, w_branch_a, w_branch_b, w_branch_c, w_out):
    batch, seq, _ = x_prompt.shape
    nbatch, dec_seq, _ = x_sample.shape
    assert dec_seq == SUBLANES and seq % PROMPT_TILE == 0 and nbatch % SAMPLE_NB == 0
    assert seq % PROJ_TILE == 0 and (nbatch * dec_seq) % PROJ_TILE == 0
    wb_len = cache_swa_k.shape[2]
    assert wb_len == WINDOW

    nc = batch + nbatch
    nc_pad = -(-nc // SUBLANES) * SUBLANES
    c_all = jnp.concatenate([c_prompt, c_sample, jnp.zeros((nc_pad - nc, D_MODEL), F32)], axis=0)
    mod_all = _ada_call(c_all, ada_w, ada_b)

    xp = x_prompt.reshape(batch * seq, D_MODEL)
    xs = x_sample.reshape(nbatch * dec_seq, D_MODEL)
    kc_all = cache_swa_k.reshape(DEPTH, nbatch, WINDOW, B_KV_WIDTH)
    vc_all = cache_swa_v.reshape(DEPTH, nbatch, WINDOW, B_KV_WIDTH)
    n0t_all = jnp.transpose(state_mlstm_n, (0, 2, 1, 3))
    m0tok_all = jnp.pad(jnp.repeat(state_mlstm_m, dec_seq, axis=1),
                        ((0, 0), (0, 0), (0, LANES - C_HEADS)))
    wcat_all, wmg_all = _weight_prep_call(w_in)
    outs_p = [[] for _ in range(6)]
    outs_s = [[] for _ in range(6)]
    vrows = []
    for l in range(DEPTH):
        lw = _layer_weights(l, wcat_all, wmg_all, b_in, gmlp_vnorm_g, gmlp_ws, gmlp_bs, swa_qnorm_g,
                            swa_knorm_g, swa_sinks, mlstm_conv_w, mlstm_conv_b, mlstm_f_bias,
                            mlstm_hnorm_g, w_branch_a, w_branch_b, w_branch_c, w_out, norm_g,
                            dec_seq)
        mod_p = mod_all[l, :batch].reshape(batch, 1, 3 * D_MODEL)
        mod_s = jnp.repeat(mod_all[l, batch:nc], dec_seq, axis=0)

        xp, ko, vo, convo, c1, n1, m1 = _prompt_layer_call(l, xp, mod_p, lw, batch, seq)
        outs_p[0].append(ko.reshape(batch, WINDOW, B_KV_HEADS, B_HEAD_DIM))
        outs_p[1].append(vo.reshape(batch, WINDOW, B_KV_HEADS, B_HEAD_DIM))
        outs_p[2].append(convo[:, SUBLANES - (C_CONV - 1):, :])
        outs_p[3].append(c1)
        outs_p[4].append(n1)
        outs_p[5].append(m1[:, 0, :C_HEADS])

        za, zb, zc = _inproj_call(l, xs, mod_s, lw["ng"], lw["wcat"], lw["bcat"], None)
        y, vrow, ko, vo, convo, c1, n1tok, m1tok = _sample_mix_call(
            l, za, zb, zc, kc_all, vc_all, state_mlstm_conv, state_mlstm_C, n0t_all, m0tok_all,
            lw, nbatch)
        xs = _outproj_call(l, xs, mod_s, lw["ng"], y, lw["wmg"], lw["bmg"], lw["wa"], lw["wb"],
                           lw["wc"], lw["wo"], None)
        outs_s[0].append(ko.reshape(nbatch, WINDOW, B_KV_HEADS, B_HEAD_DIM))
        outs_s[1].append(vo.reshape(nbatch, WINDOW, B_KV_HEADS, B_HEAD_DIM))
        outs_s[2].append(convo)
        outs_s[3].append(c1)
        outs_s[4].append(jnp.transpose(n1tok[:, ::dec_seq, :], (1, 0, 2)))
        outs_s[5].append(m1tok[::dec_seq, :C_HEADS])
        vrows.append(vrow.reshape(nbatch, dec_seq, A_WIDTH))

    sp = [jnp.stack(o) for o in outs_p]
    ss = [jnp.stack(o) for o in outs_s]
    return (xp.reshape(batch, seq, D_MODEL), xs.reshape(nbatch, dec_seq, D_MODEL),
            sp[0], sp[1], sp[2], sp[3], sp[4], sp[5],
            ss[0], ss[1], ss[2], ss[3], ss[4], ss[5], jnp.stack(vrows))
```

```python
import functools

import numpy as np
import jax
import jax.numpy as jnp
from jax import lax
from jax.experimental import pallas as pl
from jax.experimental.pallas import tpu as pltpu

F32 = jnp.float32
BF16 = jnp.bfloat16

D_MODEL = 1024
DEPTH = 2
A_WIDTH = 512
A_GROUPS = 4
GROUP_DIM = 128
B_HEADS = 8
B_KV_HEADS = 2
B_HEAD_DIM = 64
B_WIDTH = 512
B_KV_WIDTH = 128
WINDOW = 128
C_HEADS = 4
C_HEAD_DIM = 128
C_WIDTH = 512
C_CONV = 4
EPS = 1e-6
NEG = -1e30

LANES = 128
SUBLANES = 8
VMEM_LIMIT = 56 * 1024 * 1024

ZA_W = 3 * A_WIDTH
ZB_W = 2 * B_WIDTH + 2 * B_KV_WIDTH
ZC_W = 2 * C_WIDTH + 3 * C_WIDTH + LANES
ZCAT_W = ZA_W + ZB_W + ZC_W
Y_W = A_WIDTH + B_WIDTH + C_WIDTH

PROMPT_TILE = 256
MLSTM_CHUNK = PROMPT_TILE
SAMPLE_NB = 16
PROJ_TILE = 512


def _sigmoid(x):
    return 0.5 * jnp.tanh(0.5 * x) + 0.5


def _silu(x):
    t = 0.5 * x
    return t * (jnp.tanh(t) + 1.0)


def _log_sigmoid(x):
    return jnp.minimum(x, 0.0) - jnp.log1p(jnp.exp(-jnp.abs(x)))


def _rms(x):
    return x * lax.rsqrt(jnp.mean(x * x, axis=-1, keepdims=True) + EPS)


def _dot(a, b):
    return jnp.dot(a, b, preferred_element_type=F32)


def _dot_nt(a, b):
    return lax.dot_general(a, b, (((1,), (1,)), ((), ())), preferred_element_type=F32)


def _dot_exact01(m01, x):
    hi = x.astype(BF16)
    r1 = x - hi.astype(F32)
    mid = r1.astype(BF16)
    lo = (r1 - mid.astype(F32)).astype(BF16)
    return _dot(m01, hi) + _dot(m01, mid) + _dot(m01, lo)


def _modulated_norm(x, mod_ref, ng_ref):
    xn = _rms(x) * ng_ref[...]
    shift = mod_ref[:, 0:D_MODEL]
    scale = mod_ref[:, D_MODEL:2 * D_MODEL]
    return (xn * (1.0 + scale) + shift).astype(BF16)


def _head_rms_scale(x2, lane_lo):
    s0 = jnp.sum(jnp.where(lane_lo, x2, 0.0), axis=-1, keepdims=True)
    s1 = jnp.sum(jnp.where(lane_lo, 0.0, x2), axis=-1, keepdims=True)
    r0 = lax.rsqrt(s0 * (1.0 / B_HEAD_DIM) + EPS)
    r1 = lax.rsqrt(s1 * (1.0 / B_HEAD_DIM) + EPS)
    return jnp.where(lane_lo, r0, r1)


def _qk_norm(x, g_row):
    rows, width = x.shape
    lane_lo = lax.broadcasted_iota(jnp.int32, (rows, LANES), 1) < B_HEAD_DIM
    outs = []
    for j in range(width // LANES):
        slab = x[:, j * LANES:(j + 1) * LANES]
        outs.append(slab * _head_rms_scale(slab * slab, lane_lo))
    y = outs[0] if len(outs) == 1 else jnp.concatenate(outs, axis=1)
    return y * g_row


def _ada_kernel(c_ref, w_ref, b_ref, o_ref):
    c = c_ref[...]
    o_ref[...] = _dot(_silu(c).astype(BF16), w_ref[...].astype(BF16)) + b_ref[...]


def _ada_call(c_all, ada_w, ada_b):
    rows = c_all.shape[0]
    return pl.pallas_call(
        _ada_kernel,
        grid=(DEPTH, 3),
        in_specs=[
            pl.BlockSpec((rows, D_MODEL), lambda l, j: (0, 0)),
            pl.BlockSpec((None, D_MODEL, D_MODEL), lambda l, j: (l, 0, j)),
            pl.BlockSpec((None, 1, D_MODEL), lambda l, j: (l, 0, j)),
        ],
        out_specs=pl.BlockSpec((None, rows, D_MODEL), lambda l, j: (l, 0, j)),
        out_shape=jax.ShapeDtypeStruct((DEPTH, rows, 3 * D_MODEL), F32),
        compiler_params=pltpu.CompilerParams(
            dimension_semantics=("arbitrary", "arbitrary"), vmem_limit_bytes=VMEM_LIMIT),
        name="adaln_mod",
    )(c_all, ada_w, ada_b.reshape(DEPTH, 1, 3 * D_MODEL))


COL_CI = ZA_W + ZB_W + 3 * C_WIDTH
COL_CO = COL_CI + 2 * C_HEADS
COL_MG = COL_CO + 2 * C_WIDTH
PREP_CHUNK = 256
PREP_SHIFT = 2 * C_HEADS
N_MAIN = COL_CI // PREP_CHUNK
N_CO = (2 * C_WIDTH) // PREP_CHUNK
N_MG = (3 * D_MODEL) // PREP_CHUNK
J_CIF = N_MAIN + N_CO
J_MG = J_CIF + 1


def _weight_prep_kernel(wa_ref, wb_ref, wcat_ref, wmg_ref):
    j = pl.program_id(1)

    def shifted_t():
        rows = jnp.concatenate([wa_ref[PREP_SHIFT:PREP_CHUNK, :], wb_ref[...]], axis=0)
        return rows.astype(BF16).T

    @pl.when(j < N_MAIN)
    def _():
        wcat_ref[...] = wa_ref[...].astype(BF16).T

    @pl.when((j >= N_MAIN) & (j < J_CIF))
    def _():
        wcat_ref[...] = shifted_t()

    @pl.when(j == J_CIF)
    def _():
        row = lax.broadcasted_iota(jnp.int32, (PREP_CHUNK, D_MODEL), 0)
        wcat_ref[...] = jnp.where(row < PREP_SHIFT, wa_ref[...], 0.0).astype(BF16).T

    @pl.when(j >= J_MG)
    def _():
        wmg_ref[...] = shifted_t()


def _weight_prep_call(w_in):
    in_width = w_in.shape[-1]
    assert in_width == COL_MG + 3 * D_MODEL
    assert COL_CI % PREP_CHUNK == 0 and COL_CO % PREP_CHUNK == PREP_SHIFT == COL_MG % PREP_CHUNK
    w_t = jnp.swapaxes(w_in, 1, 2)
    assert in_width % PREP_SHIFT == 0 and PREP_SHIFT == SUBLANES
    last_rows = in_width // PREP_SHIFT - 1
    groups_per_chunk = PREP_CHUNK // PREP_SHIFT

    def src_block(j):
        return jnp.where(j < J_CIF, j, jnp.where(j == J_CIF, N_MAIN, j - 1))

    return pl.pallas_call(
        _weight_prep_kernel,
        grid=(DEPTH, J_MG + N_MG),
        in_specs=[
            pl.BlockSpec((None, PREP_CHUNK, D_MODEL), lambda l, j: (l, src_block(j), 0)),
            pl.BlockSpec((None, PREP_SHIFT, D_MODEL),
                         lambda l, j: (l, jnp.minimum((src_block(j) + 1) * groups_per_chunk,
                                                      last_rows), 0)),
        ],
        out_specs=[
            pl.BlockSpec((None, D_MODEL, PREP_CHUNK), lambda l, j: (l, 0, jnp.minimum(j, J_CIF))),
            pl.BlockSpec((None, D_MODEL, PREP_CHUNK), lambda l, j: (l, 0, jnp.maximum(j - J_MG, 0))),
        ],
        out_shape=[
            jax.ShapeDtypeStruct((DEPTH, D_MODEL, ZCAT_W), BF16),
            jax.ShapeDtypeStruct((DEPTH, D_MODEL, 3 * D_MODEL), BF16),
        ],
        compiler_params=pltpu.CompilerParams(
            dimension_semantics=("arbitrary", "arbitrary"), vmem_limit_bytes=VMEM_LIMIT),
        name="weight_prep",
    )(w_t, w_t)


def _col_chunks(width, step):
    return [(o, min(step, width - o)) for o in range(0, width, step)]


def _inproj_pieces(get_h, w_ref, b_ref, za_ref, zb_ref, zc_ref, step):
    def piece(o_ref, off, woff, w):
        def run():
            o_ref[:, off:off + w] = _dot(get_h(), w_ref[:, woff:woff + w]) + b_ref[:, woff:woff + w]
        return run
    pieces = []
    base = 0
    for o_ref, width in ((za_ref, ZA_W), (zb_ref, ZB_W), (zc_ref, ZC_W)):
        pieces += [piece(o_ref, off, base + off, w) for off, w in _col_chunks(width, step)]
        base += width
    return pieces


def _inproj_kernel(x_ref, mod_ref, ng_ref, w_ref, b_ref, za_ref, zb_ref, zc_ref):
    h = _modulated_norm(x_ref[...], mod_ref, ng_ref)
    for piece in _inproj_pieces(lambda: h, w_ref, b_ref, za_ref, zb_ref, zc_ref, 512):
        piece()


def _mod_spec(tm, tokens_per_batch):
    if tokens_per_batch is None:
        return pl.BlockSpec((tm, 3 * D_MODEL), lambda i: (i, 0))
    tiles_per_batch = tokens_per_batch // tm
    return pl.BlockSpec((None, 1, 3 * D_MODEL), lambda i: (i // tiles_per_batch, 0, 0))


def _layer_weight_spec(layer, rows, cols):
    return pl.BlockSpec((None, rows, cols), lambda i: (layer, 0, 0), pipeline_mode=pl.Buffered(1))


def _inproj_call(layer, x2, mod, ng, wcat, bcat, tokens_per_batch):
    ntok = x2.shape[0]
    tm = PROJ_TILE
    const = lambda i: (0, 0)
    return pl.pallas_call(
        _inproj_kernel,
        grid=(ntok // tm,),
        in_specs=[
            pl.BlockSpec((tm, D_MODEL), lambda i: (i, 0)),
            _mod_spec(tm, tokens_per_batch),
            pl.BlockSpec((1, D_MODEL), const),
            _layer_weight_spec(layer, D_MODEL, ZCAT_W),
            pl.BlockSpec((1, ZCAT_W), const),
        ],
        out_specs=[
            pl.BlockSpec((tm, ZA_W), lambda i: (i, 0)),
            pl.BlockSpec((tm, ZB_W), lambda i: (i, 0)),
            pl.BlockSpec((tm, ZC_W), lambda i: (i, 0)),
        ],
        out_shape=[
            jax.ShapeDtypeStruct((ntok, ZA_W), F32),
            jax.ShapeDtypeStruct((ntok, ZB_W), F32),
            jax.ShapeDtypeStruct((ntok, ZC_W), F32),
        ],
        compiler_params=pltpu.CompilerParams(
            dimension_semantics=("arbitrary",), vmem_limit_bytes=VMEM_LIMIT),
        name="in_projection",
    )(x2, mod, ng, wcat, bcat)


def _outproj_kernel(x_ref, mod_ref, ng_ref, y_ref, wmg_ref, bmg_ref, wa_ref, wb_ref, wc_ref,
                    wo_ref, o_ref):
    x = x_ref[...]
    h = _modulated_norm(x, mod_ref, ng_ref)
    merged = None
    for i, wbr_ref in enumerate((wa_ref, wb_ref, wc_ref)):
        cols = slice(i * D_MODEL, (i + 1) * D_MODEL)
        gate = _sigmoid(_dot(h, wmg_ref[:, cols]) + bmg_ref[:, cols])
        term = gate * _dot(y_ref[:, i * A_WIDTH:(i + 1) * A_WIDTH], wbr_ref[...])
        merged = term if merged is None else merged + term
    ada_gate = mod_ref[:, 2 * D_MODEL:3 * D_MODEL]
    o_ref[...] = x + ada_gate * _dot(merged.astype(BF16), wo_ref[...])


def _outproj_call(layer, x2, mod, ng, y, wmg, bmg, wa, wb, wc, wo, tokens_per_batch):
    ntok = x2.shape[0]
    tm = PROJ_TILE
    const = lambda i: (0, 0)
    once = pl.Buffered(1)
    return pl.pallas_call(
        _outproj_kernel,
        grid=(ntok // tm,),
        in_specs=[
            pl.BlockSpec((tm, D_MODEL), lambda i: (i, 0)),
            _mod_spec(tm, tokens_per_batch),
            pl.BlockSpec((1, D_MODEL), const),
            pl.BlockSpec((tm, Y_W), lambda i: (i, 0)),
            _layer_weight_spec(layer, D_MODEL, 3 * D_MODEL),
            pl.BlockSpec((1, 3 * D_MODEL), const),
            pl.BlockSpec((A_WIDTH, D_MODEL), const, pipeline_mode=once),
            pl.BlockSpec((B_WIDTH, D_MODEL), const, pipeline_mode=once),
            pl.BlockSpec((C_WIDTH, D_MODEL), const, pipeline_mode=once),
            pl.BlockSpec((D_MODEL, D_MODEL), const, pipeline_mode=once),
        ],
        out_specs=pl.BlockSpec((tm, D_MODEL), lambda i: (i, 0)),
        out_shape=jax.ShapeDtypeStruct((ntok, D_MODEL), F32),
        compiler_params=pltpu.CompilerParams(
            dimension_semantics=("arbitrary",), vmem_limit_bytes=VMEM_LIMIT),
        name="out_projection",
    )(x2, mod, ng, y, wmg, bmg, wa, wb, wc, wo)


def _place_q_head(qn, h, rows):
    lane = lax.broadcasted_iota(jnp.int32, (rows, LANES), 1)
    slab = qn[:, (h // 2) * LANES:(h // 2 + 1) * LANES]
    src_hi = h % 2
    dst_hi = h // (B_HEADS // B_KV_HEADS)
    keep = (lane >= B_HEAD_DIM) if src_hi else (lane < B_HEAD_DIM)
    slab = jnp.where(keep, slab, 0.0)
    if src_hi != dst_hi:
        slab = pltpu.roll(slab, B_HEAD_DIM, 1)
    return slab


def _merge_head_pair(o_even, o_odd, h_even, rows):
    lane_lo = lax.broadcasted_iota(jnp.int32, (rows, LANES), 1) < B_HEAD_DIM
    kv_hi = h_even // (B_HEADS // B_KV_HEADS)
    if kv_hi:
        o_even = pltpu.roll(o_even, B_HEAD_DIM, 1)
    else:
        o_odd = pltpu.roll(o_odd, B_HEAD_DIM, 1)
    return jnp.where(lane_lo, o_even, o_odd)


def _conv_taps(xbuf, cw_ref, cb_ref, cols, ts):
    y = cb_ref[:, cols]
    for j in range(C_CONV):
        lo = SUBLANES - (C_CONV - 1) + j
        y = y + cw_ref[j:j + 1, cols] * xbuf[lo:lo + ts, cols]
    return y


def _prompt_mix_kernel(sink_ref, za_ref, zb_ref, zc_ref, vg_ref, gw_ref, gbs_ref, qg_ref, kg_ref,
                       cw_ref, cb_ref, fb_ref, hg_ref, tril_ref, band_ref, tri01_ref, tribias_ref,
                       y_ref, ko_ref, vo_ref, convo_ref, c_ref, n_ref, m_ref,
                       kprev, vprev, xbuf, first_tile, pump):
    ts = PROMPT_TILE

    wts = [(gw_ref[gi] * tril_ref[...]).astype(BF16) for gi in range(A_GROUPS)]
    for c in range(ts // WINDOW):
        rows = slice(c * WINDOW, (c + 1) * WINDOW)
        vnb = (_rms(za_ref[rows, A_WIDTH:2 * A_WIDTH]) * vg_ref[...]).astype(BF16)
        s = jnp.concatenate(
            [_dot(wts[gi], vnb[:, gi * GROUP_DIM:(gi + 1) * GROUP_DIM]) + gbs_ref[:, gi:gi + 1]
             for gi in range(A_GROUPS)], axis=1)
        sg = _silu(za_ref[rows, 2 * A_WIDTH:3 * A_WIDTH])
        y_ref[rows, 0:A_WIDTH] = (za_ref[rows, 0:A_WIDTH] * s * sg).astype(BF16)
        pump()

    kn = _qk_norm(zb_ref[:, B_WIDTH:B_WIDTH + B_KV_WIDTH], kg_ref[...])
    vv = zb_ref[:, B_WIDTH + B_KV_WIDTH:B_WIDTH + 2 * B_KV_WIDTH]
    pump()
    grp = B_HEADS // B_KV_HEADS
    nblk = ts // WINDOW
    lane_lo2 = lax.broadcasted_iota(jnp.int32, (2 * WINDOW, LANES), 1) < B_HEAD_DIM
    kblocks = [kprev[...]] + [kn[b * WINDOW:(b + 1) * WINDOW] for b in range(nblk)]
    vblocks = [vprev[...]] + [vv[b * WINDOW:(b + 1) * WINDOW] for b in range(nblk)]
    heads = [(kh, g) for kh in range(B_KV_HEADS) for g in range(grp)]
    snk = {k: sink_ref[k[0] * grp + k[1]] for k in heads}
    for blk in range(nblk):
        rows = slice(blk * WINDOW, (blk + 1) * WINDOW)
        if blk == 0 and first_tile is not False:
            bias = jnp.where(first_tile, band_ref[1], band_ref[0])
        else:
            bias = band_ref[0]
        kcat = jnp.concatenate([kblocks[blk], kblocks[blk + 1]], axis=0)
        vcat = jnp.concatenate([vblocks[blk], vblocks[blk + 1]], axis=0)
        krol = pltpu.roll(kcat, B_HEAD_DIM, 1)
        vrol = pltpu.roll(vcat, B_HEAD_DIM, 1)
        kdup, vdup = [], []
        for kh in range(B_KV_HEADS):
            own = lane_lo2 if kh == 0 else jnp.logical_not(lane_lo2)
            kdup.append(jnp.where(own, kcat, krol).astype(BF16))
            vdup.append(jnp.where(own, vcat, vrol).astype(BF16))
        qn = _qk_norm(zb_ref[rows, 0:B_WIDTH], qg_ref[...]) * (B_HEAD_DIM ** -0.5)
        pump()
        qs = [jnp.concatenate([_place_q_head(qn, kh * grp + g, WINDOW) for g in range(grp)],
                              axis=0).astype(BF16) for kh in range(B_KV_HEADS)]
        logits = [_dot_nt(qs[kh], kdup[kh]) for kh in range(B_KV_HEADS)]
        pump()
        lg = {(kh, g): logits[kh][g * WINDOW:(g + 1) * WINDOW] + bias for kh, g in heads}
        mx = {k: jnp.maximum(jnp.max(lg[k], axis=-1, keepdims=True), snk[k]) for k in heads}
        pump()
        p = {k: jnp.exp(lg[k] - mx[k]) for k in heads}
        pump()
        rden = {k: 1.0 / (jnp.sum(p[k], axis=-1, keepdims=True) + jnp.exp(snk[k] - mx[k]))
                for k in heads}
        pump()
        pv = [_dot(jnp.concatenate([p[kh, g].astype(BF16) for g in range(grp)], axis=0), vdup[kh])
              for kh in range(B_KV_HEADS)]
        pump()
        outs = {(kh, g): pv[kh][g * WINDOW:(g + 1) * WINDOW] * rden[kh, g] for kh, g in heads}
        yb = jnp.concatenate(
            [_merge_head_pair(outs[(2 * j) // grp, (2 * j) % grp],
                              outs[(2 * j + 1) // grp, (2 * j + 1) % grp], 2 * j, WINDOW)
             for j in range(B_HEADS // 2)], axis=1)
        sgb = _silu(zb_ref[rows, B_WIDTH + 2 * B_KV_WIDTH:ZB_W])
        y_ref[rows, A_WIDTH:A_WIDTH + B_WIDTH] = (yb * sgb).astype(BF16)
        pump()
    kprev[...] = kblocks[nblk]
    vprev[...] = vblocks[nblk]
    ko_ref[...] = kblocks[nblk]
    vo_ref[...] = vblocks[nblk]
    pump()

    xbuf[SUBLANES:SUBLANES + ts, :] = zc_ref[:, 0:2 * C_WIDTH]
    pump()
    ifp = zc_ref[:, 5 * C_WIDTH:5 * C_WIDTH + LANES]
    lf = _log_sigmoid(ifp + fb_ref[...])
    pump()
    cl = MLSTM_CHUNK
    lane_c = lax.broadcasted_iota(jnp.int32, (cl, LANES), 1)
    lane_1 = lax.broadcasted_iota(jnp.int32, (1, LANES), 1)
    m_row = m_ref[...]
    m_out = m_row
    cum_all = _dot_exact01(tri01_ref[...], lf)
    st_col = jnp.where(lane_c < C_HEADS, ifp, cum_all)
    st_row = st_col.T
    tribias = tribias_ref[...]
    pump()
    for hds in MLSTM_HEAD_GROUPS:
        hs = {hd: slice(hd * C_HEAD_DIM, (hd + 1) * C_HEAD_DIM) for hd in hds}
        i_c = {hd: st_col[:, hd:hd + 1] for hd in hds}
        cum_c = {hd: st_col[:, C_HEADS + hd:C_HEADS + hd + 1] for hd in hds}
        i_r = {hd: st_row[hd:hd + 1, :] for hd in hds}
        cum_r = {hd: st_row[C_HEADS + hd:C_HEADS + hd + 1, :] for hd in hds}
        m_prev = {hd: m_row[:, hd:hd + 1] for hd in hds}
        dmat = {hd: cum_c[hd] - cum_r[hd] + i_r[hd] + tribias for hd in hds}
        m_inter = {hd: cum_c[hd] + m_prev[hd] for hd in hds}
        m_t = {hd: jnp.maximum(m_inter[hd], jnp.max(dmat[hd], axis=-1, keepdims=True)) for hd in hds}
        pump()
        q_h = {hd: _silu(_conv_taps(xbuf, cw_ref, cb_ref, hs[hd], ts)).astype(BF16) for hd in hds}
        k_h = {hd: _silu(_conv_taps(xbuf, cw_ref, cb_ref,
                                    slice(C_WIDTH + hs[hd].start, C_WIDTH + hs[hd].stop), ts))
               * (C_HEAD_DIM ** -0.5) for hd in hds}
        pump()
        v_h = {hd: zc_ref[:, 2 * C_WIDTH + hd * C_HEAD_DIM:2 * C_WIDTH + (hd + 1) * C_HEAD_DIM].astype(BF16)
               for hd in hds}
        s_qk = {hd: _dot_nt(q_h[hd], k_h[hd].astype(BF16)) for hd in hds}
        a = {hd: jnp.exp(dmat[hd] - m_t[hd]) * s_qk[hd] for hd in hds}
        pump()
        w_inter = {hd: jnp.exp(m_inter[hd] - m_t[hd]) for hd in hds}
        c_prev = {hd: c_ref[hd] for hd in hds}
        n_prev = {hd: n_ref[hd:hd + 1, :] for hd in hds}
        inter = {hd: _dot(q_h[hd], c_prev[hd].astype(BF16)) for hd in hds}
        intra = {hd: _dot(a[hd].astype(BF16), v_h[hd]) for hd in hds}
        pump()
        den = {hd: jnp.sum(a[hd], axis=-1, keepdims=True)
               + w_inter[hd] * jnp.sum(q_h[hd].astype(F32) * n_prev[hd], axis=-1, keepdims=True)
               for hd in hds}
        rnorm = {hd: 1.0 / jnp.maximum(jnp.abs(den[hd]), jnp.exp(-m_t[hd])) for hd in hds}
        hh = {hd: (intra[hd] + w_inter[hd] * inter[hd]) * rnorm[hd] for hd in hds}
        pump()
        for hd in hds:
            o_cols = slice(3 * C_WIDTH + hd * C_HEAD_DIM, 3 * C_WIDTH + (hd + 1) * C_HEAD_DIM)
            g_cols = slice(4 * C_WIDTH + hd * C_HEAD_DIM, 4 * C_WIDTH + (hd + 1) * C_HEAD_DIM)
            gate_o = _sigmoid(zc_ref[:, o_cols]) * _silu(zc_ref[:, g_cols])
            y_cols = slice(A_WIDTH + B_WIDTH + hd * C_HEAD_DIM, A_WIDTH + B_WIDTH + (hd + 1) * C_HEAD_DIM)
            y_ref[:, y_cols] = (_rms(hh[hd]) * hg_ref[:, hs[hd]] * gate_o).astype(BF16)
        pump()
        total = {hd: cum_r[hd][:, cl - 1:cl] for hd in hds}
        g_r = {hd: total[hd] - cum_r[hd] + i_r[hd] for hd in hds}
        g_c = {hd: total[hd] - cum_c[hd] + i_c[hd] for hd in hds}
        m_new = {hd: jnp.maximum(total[hd] + m_prev[hd], jnp.max(g_r[hd], axis=-1, keepdims=True))
                 for hd in hds}
        kw = {hd: jnp.exp(g_c[hd] - m_new[hd]) * k_h[hd] for hd in hds}
        decay = {hd: jnp.exp(total[hd] + m_prev[hd] - m_new[hd]) for hd in hds}
        pump()
        upd = {hd: _dot(kw[hd].T.astype(BF16), v_h[hd]) for hd in hds}
        for hd in hds:
            c_ref[hd] = decay[hd] * c_prev[hd] + upd[hd]
            n_ref[hd:hd + 1, :] = decay[hd] * n_prev[hd] + jnp.sum(kw[hd], axis=0, keepdims=True)
            m_out = jnp.where(lane_1 == hd, m_new[hd], m_out)
        pump()
    m_ref[...] = m_out
    tail = xbuf[ts:ts + SUBLANES, :]
    xbuf[0:SUBLANES, :] = tail
    convo_ref[...] = tail


def _prompt_mask_constants():
    r = np.arange(WINDOW)[:, None]
    c = np.arange(2 * WINDOW)[None, :]
    band = (c > r) & (c <= r + WINDOW)
    band_first = band & (c >= WINDOW)
    band_bias = np.where(np.stack([band, band_first]), 0.0, NEG).astype(np.float32)
    tril = (np.arange(WINDOW)[:, None] >= np.arange(WINDOW)[None, :]).astype(np.float32)
    tri = np.arange(MLSTM_CHUNK)[:, None] >= np.arange(MLSTM_CHUNK)[None, :]
    return (jnp.asarray(tril), jnp.asarray(band_bias), jnp.asarray(tri, dtype=BF16),
            jnp.asarray(np.where(tri, 0.0, NEG).astype(np.float32)))


N_MIX_PARAMS = 13
MLSTM_HEAD_GROUPS = ((0, 1), (2, 3))
MIX_PUMP_CALLS = 37
TAIL_FILL_PIECES = 8
MXU_PIECE_COLS = 256


class _Interleaver:
    def __init__(self, pieces, calls, hold_back=0):
        self._pieces = list(pieces)
        self._hold_back = hold_back
        self._spread = len(self._pieces) - hold_back
        self._emitted = 0
        self._calls = calls
        self._call = 0

    def __call__(self):
        self._call += 1
        target = (self._call * self._spread) // self._calls
        while self._emitted < target:
            self._pieces.pop(0)()
            self._emitted += 1

    def finish(self):
        assert self._call == self._calls and len(self._pieces) == self._hold_back, self._call
        return self._pieces


def _gate_pieces(h_ref, wmg_ref, bmg_ref, g_ref):
    def piece(off):
        cols = slice(off, off + MXU_PIECE_COLS)
        def run():
            g_ref[:, cols] = _sigmoid(_dot(h_ref[...], wmg_ref[:, cols]) + bmg_ref[:, cols])
        return run
    return [piece(off) for off in range(0, 3 * D_MODEL, MXU_PIECE_COLS)]


def _merge_and_project(x, mod_ref, g_ref, y_ref, wa_ref, wb_ref, wc_ref, wo_ref, fillers=()):
    fillers = list(fillers)
    per_stage = -(-len(fillers) // 4)
    merged = None
    for i, wbr_ref in enumerate((wa_ref, wb_ref, wc_ref)):
        for piece in fillers[i * per_stage:(i + 1) * per_stage]:
            piece()
        term = (g_ref[:, i * D_MODEL:(i + 1) * D_MODEL]
                * _dot(y_ref[:, i * A_WIDTH:(i + 1) * A_WIDTH], wbr_ref[...]))
        merged = term if merged is None else merged + term
    for piece in fillers[3 * per_stage:]:
        piece()
    ada_gate = mod_ref[:, 2 * D_MODEL:3 * D_MODEL]
    return x + ada_gate * _dot(merged.astype(BF16), wo_ref[...])


def _prompt_layer_kernel(tiles_per_seq, sink_ref, x2_ref, xn_ref, mod_ref, modn_ref, ng_ref,
                         wcat_ref, bcat_ref, *rest):
    mix_params = rest[:N_MIX_PARAMS]
    wmg_ref, bmg_ref, wa_ref, wb_ref, wc_ref, wo_ref = rest[N_MIX_PARAMS:N_MIX_PARAMS + 6]
    o_ref, ko_ref, vo_ref, convo_ref, c_ref, n_ref, m_ref = rest[N_MIX_PARAMS + 6:N_MIX_PARAMS + 13]
    (za0, zb0, zc0, za1, zb1, zc1, h0, h1, y_scr, g_scr, kprev, vprev, xbuf) = rest[N_MIX_PARAMS + 13:]
    ts = PROMPT_TILE
    z = ((za0, zb0, zc0), (za1, zb1, zc1))
    h = (h0, h1)
    k = pl.program_id(0)
    seq_start = (k % (tiles_per_seq // 2)) == 0

    @pl.when(k == 0)
    def _():
        h0[...] = _modulated_norm(x2_ref[0:ts, :], mod_ref, ng_ref)
        for piece in _inproj_pieces(lambda: h0[...], wcat_ref, bcat_ref, *z[0], 512):
            piece()

    @pl.when(seq_start)
    def _():
        kprev[...] = jnp.zeros_like(kprev)
        vprev[...] = jnp.zeros_like(vprev)
        xbuf[0:SUBLANES, :] = jnp.zeros((SUBLANES, 2 * C_WIDTH), F32)
        c_ref[...] = jnp.zeros_like(c_ref)
        n_ref[...] = jnp.zeros_like(n_ref)
        m_ref[...] = jnp.zeros_like(m_ref)

    for half in range(2):
        cur, nxt = half, 1 - half
        rows = slice(half * ts, (half + 1) * ts)
        if half == 0:
            h[nxt][...] = _modulated_norm(x2_ref[ts:2 * ts, :], mod_ref, ng_ref)
        else:
            h[nxt][...] = _modulated_norm(xn_ref[...], modn_ref, ng_ref)
        get_h_next = functools.partial(lambda r: r[...], h[nxt])
        hold = TAIL_FILL_PIECES if half == 1 else 0
        proj = _inproj_pieces(get_h_next, wcat_ref, bcat_ref, *z[nxt], MXU_PIECE_COLS)
        pump = _Interleaver(
            proj[:len(proj) - hold] + _gate_pieces(h[cur], wmg_ref, bmg_ref, g_scr)
            + proj[len(proj) - hold:], MIX_PUMP_CALLS, hold_back=hold)
        _prompt_mix_kernel(sink_ref, *z[cur], *mix_params,
                           y_scr, ko_ref, vo_ref, convo_ref, c_ref, n_ref, m_ref, kprev, vprev, xbuf,
                           first_tile=seq_start if half == 0 else False, pump=pump)
        o_ref[rows, :] = _merge_and_project(x2_ref[rows, :], mod_ref, g_scr, y_scr,
                                            wa_ref, wb_ref, wc_ref, wo_ref, fillers=pump.finish())


def _prompt_layer_call(layer, x2, mod, lw, batch, seq):
    ts = PROMPT_TILE
    nt = seq // ts
    assert nt % 2 == 0
    last_tile = batch * nt - 1
    const2 = lambda k: (0, 0)
    const3 = lambda k: (0, 0, 0)
    per_b3 = lambda k: ((2 * k) // nt, 0, 0)
    next_tile = lambda k: jnp.minimum(2 * k + 2, last_tile)
    once = pl.Buffered(1)
    return pl.pallas_call(
        functools.partial(_prompt_layer_kernel, nt),
        grid=(batch * nt // 2,),
        in_specs=[
            pl.BlockSpec(memory_space=pltpu.SMEM),
            pl.BlockSpec((2 * ts, D_MODEL), lambda k: (k, 0)),
            pl.BlockSpec((ts, D_MODEL), lambda k: (next_tile(k), 0)),
            pl.BlockSpec((None, 1, 3 * D_MODEL), per_b3),
            pl.BlockSpec((None, 1, 3 * D_MODEL), lambda k: (next_tile(k) // nt, 0, 0)),
            pl.BlockSpec((1, D_MODEL), const2),
            _layer_weight_spec(layer, D_MODEL, ZCAT_W),
            pl.BlockSpec((1, ZCAT_W), const2),
            pl.BlockSpec((1, A_WIDTH), const2),
            pl.BlockSpec((A_GROUPS, WINDOW, WINDOW), const3),
            pl.BlockSpec((WINDOW, LANES), const2),
            pl.BlockSpec((1, B_WIDTH), const2),
            pl.BlockSpec((1, B_KV_WIDTH), const2),
            pl.BlockSpec((C_CONV, 2 * C_WIDTH), const2),
            pl.BlockSpec((1, 2 * C_WIDTH), const2),
            pl.BlockSpec((1, LANES), const2),
            pl.BlockSpec((1, C_WIDTH), const2),
            pl.BlockSpec((WINDOW, WINDOW), const2),
            pl.BlockSpec((2, WINDOW, 2 * WINDOW), const3),
            pl.BlockSpec((MLSTM_CHUNK, MLSTM_CHUNK), const2),
            pl.BlockSpec((MLSTM_CHUNK, MLSTM_CHUNK), const2),
            _layer_weight_spec(layer, D_MODEL, 3 * D_MODEL),
            pl.BlockSpec((1, 3 * D_MODEL), const2),
            pl.BlockSpec((A_WIDTH, D_MODEL), const2, pipeline_mode=once),
            pl.BlockSpec((B_WIDTH, D_MODEL), const2, pipeline_mode=once),
            pl.BlockSpec((C_WIDTH, D_MODEL), const2, pipeline_mode=once),
            pl.BlockSpec((D_MODEL, D_MODEL), const2, pipeline_mode=once),
        ],
        out_specs=[
            pl.BlockSpec((2 * ts, D_MODEL), lambda k: (k, 0)),
            pl.BlockSpec((None, WINDOW, B_KV_WIDTH), per_b3),
            pl.BlockSpec((None, WINDOW, B_KV_WIDTH), per_b3),
            pl.BlockSpec((None, SUBLANES, 2 * C_WIDTH), per_b3),
            pl.BlockSpec((None, C_HEADS, C_HEAD_DIM, C_HEAD_DIM), lambda k: ((2 * k) // nt, 0, 0, 0)),
            pl.BlockSpec((None, C_HEADS, C_HEAD_DIM), per_b3),
            pl.BlockSpec((None, 1, LANES), per_b3),
        ],
        out_shape=[
            jax.ShapeDtypeStruct((batch * seq, D_MODEL), F32),
            jax.ShapeDtypeStruct((batch, WINDOW, B_KV_WIDTH), F32),
            jax.ShapeDtypeStruct((batch, WINDOW, B_KV_WIDTH), F32),
            jax.ShapeDtypeStruct((batch, SUBLANES, 2 * C_WIDTH), F32),
            jax.ShapeDtypeStruct((batch, C_HEADS, C_HEAD_DIM, C_HEAD_DIM), F32),
            jax.ShapeDtypeStruct((batch, C_HEADS, C_HEAD_DIM), F32),
            jax.ShapeDtypeStruct((batch, 1, LANES), F32),
        ],
        scratch_shapes=(
            [pltpu.VMEM((ts, w), F32) for w in (ZA_W, ZB_W, ZC_W)] * 2
            + [pltpu.VMEM((ts, D_MODEL), BF16)] * 2
            + [pltpu.VMEM((ts, Y_W), BF16),
               pltpu.VMEM((ts, 3 * D_MODEL), F32),
               pltpu.VMEM((WINDOW, B_KV_WIDTH), F32),
               pltpu.VMEM((WINDOW, B_KV_WIDTH), F32),
               pltpu.VMEM((ts + SUBLANES, 2 * C_WIDTH), F32)]),
        compiler_params=pltpu.CompilerParams(
            dimension_semantics=("arbitrary",), vmem_limit_bytes=VMEM_LIMIT),
        name="prompt_layer",
    )(lw["sinks"], x2, x2, mod, mod, lw["ng"], lw["wcat"], lw["bcat"],
      lw["vg"], lw["gws"], lw["gbs_col"], lw["qg"], lw["kg"], lw["cw"], lw["cb"], lw["fb"], lw["hg"],
      *_prompt_mask_constants(),
      lw["wmg"], lw["bmg"], lw["wa"], lw["wb"], lw["wc"], lw["wo"])


def _sample_mix_kernel(sink_ref, za_ref, zb_ref, zc_ref, kc_ref, vc_ref, cs_ref, c0_ref, n0_ref,
                       m0_ref, vg_ref, gwb_ref, gbs_ref, qg_ref, kg_ref, cw_ref, cb_ref, fb_ref,
                       hg_ref,
                       y_ref, vrow_ref, ko_ref, vo_ref, convo_ref, c1_ref, n1_ref, m1_ref,
                       xbuf, first_layer=None):
    nb = SAMPLE_NB
    t = SUBLANES
    rows = nb * t
    if first_layer is not None:
        for other in range(DEPTH):
            if other != first_layer:
                c1_ref[other] = jnp.zeros(c1_ref.shape[1:], F32)
        c1_ref = c1_ref.at[first_layer]
    tok_r = lax.broadcasted_iota(jnp.int32, (rows, rows), 0)
    tok_c = lax.broadcasted_iota(jnp.int32, (rows, rows), 1)
    same_b = (tok_r // t) == (tok_c // t)
    causal_b = same_b & (tok_c <= tok_r)

    u = za_ref[:, 0:A_WIDTH]
    vn = _rms(za_ref[:, A_WIDTH:2 * A_WIDTH]) * vg_ref[...]
    sg = _silu(za_ref[:, 2 * A_WIDTH:3 * A_WIDTH])
    vrow_ref[...] = vn
    vnb = vn.astype(BF16)
    s_cols = []
    for gi in range(A_GROUPS):
        s_cols.append(_dot(gwb_ref[gi], vnb[:, gi * GROUP_DIM:(gi + 1) * GROUP_DIM])
                      + gbs_ref[:, gi:gi + 1])
    y_ref[:, 0:A_WIDTH] = (u * jnp.concatenate(s_cols, axis=1) * sg).astype(BF16)

    qn = _qk_norm(zb_ref[:, 0:B_WIDTH], qg_ref[...]) * (B_HEAD_DIM ** -0.5)
    kn = _qk_norm(zb_ref[:, B_WIDTH:B_WIDTH + B_KV_WIDTH], kg_ref[...])
    vv = zb_ref[:, B_WIDTH + B_KV_WIDTH:B_WIDTH + 2 * B_KV_WIDTH]
    sgb = _silu(zb_ref[:, B_WIDTH + 2 * B_KV_WIDTH:ZB_W])
    kn3 = kn.reshape(nb, t, B_KV_WIDTH)
    vv3 = vv.reshape(nb, t, B_KV_WIDTH)
    kcache = kc_ref[...]
    vcache = vc_ref[...]
    pad = jnp.zeros((nb, WINDOW - t, B_KV_WIDTH), F32)
    kall = jnp.concatenate([kcache, kn3, pad], axis=1).astype(BF16)
    vall = jnp.concatenate([vcache, vv3, pad], axis=1).astype(BF16)
    qp = jnp.concatenate([_place_q_head(qn, h, rows).reshape(nb, t, LANES) for h in range(B_HEADS)],
                         axis=1).astype(BF16)
    logits = lax.dot_general(qp, kall, (((2,), (2,)), ((0,), (0,))), preferred_element_type=F32)
    qrow = lax.broadcasted_iota(jnp.int32, (nb, B_HEADS * t, 2 * WINDOW), 1)
    kcol = lax.broadcasted_iota(jnp.int32, (nb, B_HEADS * t, 2 * WINDOW), 2)
    qt = qrow % t
    valid = ((kcol < WINDOW) & (kcol > qt)) | ((kcol >= WINDOW) & ((kcol - WINDOW) <= qt))
    hrow = lax.broadcasted_iota(jnp.int32, (B_HEADS * t, 1), 0) // t
    snk = jnp.zeros((B_HEADS * t, 1), F32)
    for h in range(B_HEADS):
        snk = jnp.where(hrow == h, sink_ref[h], snk)
    lg = jnp.where(valid, logits, NEG)
    mx = jnp.maximum(jnp.max(lg, axis=-1, keepdims=True), snk[None])
    p = jnp.exp(lg - mx)
    den = jnp.sum(p, axis=-1, keepdims=True) + jnp.exp(snk[None] - mx)
    pv = lax.dot_general(p.astype(BF16), vall, (((2,), (1,)), ((0,), (0,))),
                         preferred_element_type=F32) / den
    head_out = [pv[:, h * t:(h + 1) * t, :].reshape(rows, LANES) for h in range(B_HEADS)]
    yb = jnp.concatenate(
        [_merge_head_pair(head_out[2 * j], head_out[2 * j + 1], 2 * j, rows)
         for j in range(B_HEADS // 2)], axis=1)
    y_ref[:, A_WIDTH:A_WIDTH + B_WIDTH] = (yb * sgb).astype(BF16)
    ko_ref[...] = jnp.concatenate([kcache[:, t:, :], kn3], axis=1)
    vo_ref[...] = jnp.concatenate([vcache[:, t:, :], vv3], axis=1)

    xbuf[:, SUBLANES - (C_CONV - 1):SUBLANES, :] = cs_ref[...]
    xbuf[:, SUBLANES:2 * SUBLANES, :] = zc_ref[:, 0:2 * C_WIDTH].reshape(nb, t, 2 * C_WIDTH)
    y3 = cb_ref[...][None]
    for j in range(C_CONV):
        lo = SUBLANES - (C_CONV - 1) + j
        y3 = y3 + cw_ref[j:j + 1, :][None] * xbuf[:, lo:lo + t, :]
    convo_ref[...] = xbuf[:, 2 * SUBLANES - (C_CONV - 1):2 * SUBLANES, :]
    qk = _silu(y3.reshape(rows, 2 * C_WIDTH))
    qall = qk[:, 0:C_WIDTH].astype(BF16)
    kall_c = qk[:, C_WIDTH:2 * C_WIDTH] * (C_HEAD_DIM ** -0.5)
    vall_c = zc_ref[:, 2 * C_WIDTH:3 * C_WIDTH].astype(BF16)
    gate_o = _sigmoid(zc_ref[:, 3 * C_WIDTH:4 * C_WIDTH]) * _silu(zc_ref[:, 4 * C_WIDTH:5 * C_WIDTH])
    ifp = zc_ref[:, 5 * C_WIDTH:5 * C_WIDTH + LANES]
    lf = _log_sigmoid(ifp + fb_ref[...])
    lane_t = lax.broadcasted_iota(jnp.int32, (rows, LANES), 1)
    cum_all = _dot_exact01(jnp.where(causal_b, 1.0, 0.0).astype(BF16), lf)
    tot_all = _dot_exact01(jnp.where(same_b, 1.0, 0.0).astype(BF16), lf)
    st_col = jnp.where(lane_t < C_HEADS, ifp, cum_all)
    st_row = st_col.T
    tot_row = tot_all.T
    m0 = m0_ref[...]
    same_b_bf = jnp.where(same_b, 1.0, 0.0).astype(BF16)
    batch_of_lane = lax.broadcasted_iota(jnp.int32, (nb, 1, rows), 2) // t
    batch_id = lax.broadcasted_iota(jnp.int32, (nb, 1, rows), 0)
    own_tok = batch_of_lane == batch_id
    h_cols = []
    m_out = jnp.zeros((rows, LANES), F32)
    for hd in range(C_HEADS):
        hs = slice(hd * C_HEAD_DIM, (hd + 1) * C_HEAD_DIM)
        i_c = st_col[:, hd:hd + 1]
        cum_c = st_col[:, C_HEADS + hd:C_HEADS + hd + 1]
        tot_c = tot_all[:, C_HEADS + hd:C_HEADS + hd + 1]
        i_r = st_row[hd:hd + 1, :]
        cum_r = st_row[C_HEADS + hd:C_HEADS + hd + 1, :]
        tot_r = tot_row[C_HEADS + hd:C_HEADS + hd + 1, :]
        m_prev = m0[:, hd:hd + 1]
        dmat = jnp.where(causal_b, cum_c - cum_r + i_r, NEG)
        m_inter = cum_c + m_prev
        m_t = jnp.maximum(m_inter, jnp.max(dmat, axis=-1, keepdims=True))
        q_h = qall[:, hs]
        k_h = kall_c[:, hs]
        v_h = vall_c[:, hs]
        a = jnp.exp(dmat - m_t) * _dot_nt(q_h, k_h.astype(BF16))
        w_inter = jnp.exp(m_inter - m_t)
        c_prev = c0_ref[:, hd]
        n_tok = jnp.broadcast_to(n0_ref[hd][:, None, :], (nb, t, C_HEAD_DIM)).reshape(rows, C_HEAD_DIM)
        inter = lax.dot_general(q_h.reshape(nb, t, C_HEAD_DIM), c_prev.astype(BF16),
                                (((2,), (1,)), ((0,), (0,))), preferred_element_type=F32)
        num = _dot(a.astype(BF16), v_h) + w_inter * inter.reshape(rows, C_HEAD_DIM)
        den = (jnp.sum(a, axis=-1, keepdims=True)
               + w_inter * jnp.sum(q_h.astype(F32) * n_tok, axis=-1, keepdims=True))
        hh = num / jnp.maximum(jnp.abs(den), jnp.exp(-m_t))
        h_cols.append(_rms(hh))
        g_r = tot_r - cum_r + i_r
        g_c = tot_c - cum_c + i_c
        m_new = jnp.maximum(tot_c + m_prev,
                            jnp.max(jnp.where(same_b, g_r, NEG), axis=-1, keepdims=True))
        kw = jnp.exp(g_c - m_new) * k_h
        decay = jnp.exp(tot_c + m_prev - m_new)
        kwt = kw.T
        lhs = jnp.where(own_tok, kwt[None], 0.0).astype(BF16).reshape(nb * C_HEAD_DIM, rows)
        upd = _dot(lhs, v_h).reshape(nb, C_HEAD_DIM, C_HEAD_DIM)
        dec_b = jnp.broadcast_to(decay, (rows, C_HEAD_DIM)).reshape(nb, t, C_HEAD_DIM)[:, 0:1, :]
        c1_ref[:, hd] = dec_b * c_prev + upd
        n1_ref[hd] = decay * n_tok + _dot(same_b_bf, kw.astype(BF16))
        m_out = jnp.where(lane_t == hd, m_new, m_out)
    m1_ref[...] = m_out
    hn = jnp.concatenate(h_cols, axis=1) * hg_ref[...]
    y_ref[:, A_WIDTH + B_WIDTH:Y_W] = (hn * gate_o).astype(BF16)


def _sample_mix_call(l, za, zb, zc, kc, vc, cs, c0, n0t, m0tok, lw, nbatch, c1_all=None):
    nb = SAMPLE_NB
    t = SUBLANES
    rows = nb * t
    tok = lambda i: (i, 0)
    const2 = lambda i: (0, 0)
    const3 = lambda i: (0, 0, 0)
    b3 = lambda i: (i, 0, 0)
    lb4 = lambda i: (l, i, 0, 0)
    operands = [lw["sinks"], za, zb, zc, kc, vc, cs, c0, n0t, m0tok, lw["vg"], lw["gwb"],
                lw["gbs_tok"], lw["qg"], lw["kg"], lw["cw"], lw["cb"], lw["fb"], lw["hg"]]
    c_block = (nb, C_HEADS, C_HEAD_DIM, C_HEAD_DIM)
    if c1_all is None:
        kernel_fn, extra_specs, aliases = functools.partial(_sample_mix_kernel, first_layer=l), [], {}
        c1_spec = pl.BlockSpec((DEPTH,) + c_block, lambda i: (0, i, 0, 0, 0))
    else:
        n_in = len(operands)
        operands.append(c1_all)
        extra_specs = [pl.BlockSpec(memory_space=pl.ANY)]
        aliases = {n_in: 5}
        kernel_fn = lambda *refs: _sample_mix_kernel(*refs[:n_in], *refs[n_in + 1:])
        c1_spec = pl.BlockSpec((None,) + c_block, lambda i: (l, i, 0, 0, 0))
    return pl.pallas_call(
        kernel_fn,
        grid=(nbatch // nb,),
        input_output_aliases=aliases,
        in_specs=[
            pl.BlockSpec(memory_space=pltpu.SMEM),
            pl.BlockSpec((rows, ZA_W), tok),
            pl.BlockSpec((rows, ZB_W), tok),
            pl.BlockSpec((rows, ZC_W), tok),
            pl.BlockSpec((None, nb, WINDOW, B_KV_WIDTH), lb4),
            pl.BlockSpec((None, nb, WINDOW, B_KV_WIDTH), lb4),
            pl.BlockSpec((None, nb, C_CONV - 1, 2 * C_WIDTH), lb4),
            pl.BlockSpec((None, nb, C_HEADS, C_HEAD_DIM, C_HEAD_DIM), lambda i: (l, i, 0, 0, 0)),
            pl.BlockSpec((None, C_HEADS, nb, C_HEAD_DIM), lambda i: (l, 0, i, 0)),
            pl.BlockSpec((None, rows, LANES), lambda i: (l, i, 0)),
            pl.BlockSpec((1, A_WIDTH), const2),
            pl.BlockSpec((A_GROUPS, rows, rows), const3),
            pl.BlockSpec((rows, LANES), const2),
            pl.BlockSpec((1, B_WIDTH), const2),
            pl.BlockSpec((1, B_KV_WIDTH), const2),
            pl.BlockSpec((C_CONV, 2 * C_WIDTH), const2),
            pl.BlockSpec((1, 2 * C_WIDTH), const2),
            pl.BlockSpec((1, LANES), const2),
            pl.BlockSpec((1, C_WIDTH), const2),
        ] + extra_specs,
        out_specs=[
            pl.BlockSpec((rows, Y_W), tok),
            pl.BlockSpec((rows, A_WIDTH), tok),
            pl.BlockSpec((nb, WINDOW, B_KV_WIDTH), b3),
            pl.BlockSpec((nb, WINDOW, B_KV_WIDTH), b3),
            pl.BlockSpec((nb, C_CONV - 1, 2 * C_WIDTH), b3),
            c1_spec,
            pl.BlockSpec((C_HEADS, rows, C_HEAD_DIM), lambda i: (0, i, 0)),
            pl.BlockSpec((rows, LANES), tok),
        ],
        out_shape=[
            jax.ShapeDtypeStruct((nbatch * t, Y_W), BF16),
            jax.ShapeDtypeStruct((nbatch * t, A_WIDTH), F32),
            jax.ShapeDtypeStruct((nbatch, WINDOW, B_KV_WIDTH), F32),
            jax.ShapeDtypeStruct((nbatch, WINDOW, B_KV_WIDTH), F32),
            jax.ShapeDtypeStruct((nbatch, C_CONV - 1, 2 * C_WIDTH), F32),
            jax.ShapeDtypeStruct((DEPTH, nbatch, C_HEADS, C_HEAD_DIM, C_HEAD_DIM), F32),
            jax.ShapeDtypeStruct((C_HEADS, nbatch * t, C_HEAD_DIM), F32),
            jax.ShapeDtypeStruct((nbatch * t, LANES), F32),
        ],
        scratch_shapes=[pltpu.VMEM((nb, 2 * SUBLANES, 2 * C_WIDTH), F32)],
        compiler_params=pltpu.CompilerParams(
            dimension_semantics=("arbitrary",), vmem_limit_bytes=VMEM_LIMIT),
        name="sample_mixer",
    )(*operands)


def _layer_weights(l, wcat_all, wmg_all, b_in, gmlp_vnorm_g, gmlp_ws, gmlp_bs, swa_qnorm_g,
                   swa_knorm_g, swa_sinks, mlstm_conv_w, mlstm_conv_b, mlstm_f_bias, mlstm_hnorm_g,
                   w_branch_a, w_branch_b, w_branch_c, w_out, norm_g, dec_seq):
    bl = b_in[l]
    bcat = jnp.concatenate([bl[:COL_CI], bl[COL_CO:COL_MG], bl[COL_CI:COL_CO],
                            jnp.zeros((LANES - 2 * C_HEADS,), F32)])
    t = dec_seq
    nb = SAMPLE_NB
    ws_t = gmlp_ws[l][:, :t, :t] * jnp.tril(jnp.ones((t, t), F32))
    eye = jnp.eye(nb, dtype=F32)
    gwb = jnp.einsum("bc,gts->gbtcs", eye, ws_t).reshape(A_GROUPS, nb * t, nb * t).astype(BF16)
    gbs_col = jnp.pad(gmlp_bs[l].T, ((0, 0), (0, LANES - A_GROUPS)))
    gbs_tok = jnp.pad(jnp.tile(gmlp_bs[l][:, :t].T, (nb, 1)), ((0, 0), (0, LANES - A_GROUPS)))
    fb = jnp.pad(mlstm_f_bias[l], (C_HEADS, LANES - 2 * C_HEADS)).reshape(1, LANES)
    return dict(
        ng=norm_g[l].reshape(1, D_MODEL),
        wcat=wcat_all, bcat=bcat.reshape(1, ZCAT_W),
        wmg=wmg_all, bmg=bl[COL_MG:].reshape(1, 3 * D_MODEL),
        wa=w_branch_a[l].astype(BF16), wb=w_branch_b[l].astype(BF16),
        wc=w_branch_c[l].astype(BF16), wo=w_out[l].astype(BF16),
        vg=gmlp_vnorm_g[l].reshape(1, A_WIDTH), gws=gmlp_ws[l], gwb=gwb,
        gbs_col=gbs_col, gbs_tok=gbs_tok,
        qg=jnp.tile(swa_qnorm_g[l], B_HEADS).reshape(1, B_WIDTH),
        kg=jnp.tile(swa_knorm_g[l], B_KV_HEADS).reshape(1, B_KV_WIDTH),
        sinks=swa_sinks[l],
        cw=mlstm_conv_w[l], cb=mlstm_conv_b[l].reshape(1, 2 * C_WIDTH), fb=fb,
        hg=mlstm_hnorm_g[l].reshape(1, C_WIDTH),
    )


def kernel(x_prompt, x_sample, cache_swa_k, cache_swa_v, state_mlstm_conv, state_mlstm_C, state_mlstm_n, state_mlstm_m, c_prompt, c_sample, ada_w, ada_b, norm_g, w_in, b_in, gmlp_vnorm_g, gmlp_ws, gmlp_bs, swa_qnorm_g, swa_knorm_g, swa_sinks, mlstm_conv_w, mlstm_conv_b, mlstm_f_bias, mlstm_hnorm_g, w_branch_a, w_branch_b, w_branch_c, w_out):
    batch, seq, _ = x_prompt.shape
    nbatch, dec_seq, _ = x_sample.shape
    assert dec_seq == SUBLANES and seq % PROMPT_TILE == 0 and nbatch % SAMPLE_NB == 0
    assert seq % PROJ_TILE == 0 and (nbatch * dec_seq) % PROJ_TILE == 0
    wb_len = cache_swa_k.shape[2]
    assert wb_len == WINDOW

    nc = batch + nbatch
    nc_pad = -(-nc // SUBLANES) * SUBLANES
    c_all = jnp.concatenate([c_prompt, c_sample, jnp.zeros((nc_pad - nc, D_MODEL), F32)], axis=0)
    mod_all = _ada_call(c_all, ada_w, ada_b)

    xp = x_prompt.reshape(batch * seq, D_MODEL)
    xs = x_sample.reshape(nbatch * dec_seq, D_MODEL)
    kc_all = cache_swa_k.reshape(DEPTH, nbatch, WINDOW, B_KV_WIDTH)
    vc_all = cache_swa_v.reshape(DEPTH, nbatch, WINDOW, B_KV_WIDTH)
    n0t_all = jnp.transpose(state_mlstm_n, (0, 2, 1, 3))
    m0tok_all = jnp.pad(jnp.repeat(state_mlstm_m, dec_seq, axis=1),
                        ((0, 0), (0, 0), (0, LANES - C_HEADS)))
    wcat_all, wmg_all = _weight_prep_call(w_in)
    outs_p = [[] for _ in range(6)]
    outs_s = [[] for _ in range(6)]
    vrows = []
    c1_all = None
    for l in range(DEPTH):
        lw = _layer_weights(l, wcat_all, wmg_all, b_in, gmlp_vnorm_g, gmlp_ws, gmlp_bs, swa_qnorm_g,
                            swa_knorm_g, swa_sinks, mlstm_conv_w, mlstm_conv_b, mlstm_f_bias,
                            mlstm_hnorm_g, w_branch_a, w_branch_b, w_branch_c, w_out, norm_g,
                            dec_seq)
        mod_p = mod_all[l, :batch].reshape(batch, 1, 3 * D_MODEL)
        mod_s = jnp.repeat(mod_all[l, batch:nc], dec_seq, axis=0)

        xp, ko, vo, convo, c1, n1, m1 = _prompt_layer_call(l, xp, mod_p, lw, batch, seq)
        outs_p[0].append(ko.reshape(batch, WINDOW, B_KV_HEADS, B_HEAD_DIM))
        outs_p[1].append(vo.reshape(batch, WINDOW, B_KV_HEADS, B_HEAD_DIM))
        outs_p[2].append(convo[:, SUBLANES - (C_CONV - 1):, :])
        outs_p[3].append(c1)
        outs_p[4].append(n1)
        outs_p[5].append(m1[:, 0, :C_HEADS])

        za, zb, zc = _inproj_call(l, xs, mod_s, lw["ng"], lw["wcat"], lw["bcat"], None)
        y, vrow, ko, vo, convo, c1_all, n1tok, m1tok = _sample_mix_call(
            l, za, zb, zc, kc_all, vc_all, state_mlstm_conv, state_mlstm_C, n0t_all, m0tok_all,
            lw, nbatch, c1_all)
        xs = _outproj_call(l, xs, mod_s, lw["ng"], y, lw["wmg"], lw["bmg"], lw["wa"], lw["wb"],
                           lw["wc"], lw["wo"], None)
        outs_s[0].append(ko.reshape(nbatch, WINDOW, B_KV_HEADS, B_HEAD_DIM))
        outs_s[1].append(vo.reshape(nbatch, WINDOW, B_KV_HEADS, B_HEAD_DIM))
        outs_s[2].append(convo)
        outs_s[4].append(jnp.transpose(n1tok[:, ::dec_seq, :], (1, 0, 2)))
        outs_s[5].append(m1tok[::dec_seq, :C_HEADS])
        vrows.append(vrow.reshape(nbatch, dec_seq, A_WIDTH))

    sp = [jnp.stack(o) for o in outs_p]
    ss = [jnp.stack(o) if o else None for o in outs_s]
    return (xp.reshape(batch, seq, D_MODEL), xs.reshape(nbatch, dec_seq, D_MODEL),
            sp[0], sp[1], sp[2], sp[3], sp[4], sp[5],
            ss[0], ss[1], ss[2], c1_all, ss[4], ss[5], jnp.stack(vrows))
```

```python
import functools

import numpy as np
import jax
import jax.numpy as jnp
from jax import lax
from jax.experimental import pallas as pl
from jax.experimental.pallas import tpu as pltpu

F32 = jnp.float32
BF16 = jnp.bfloat16

D_MODEL = 1024
DEPTH = 2
A_WIDTH = 512
A_GROUPS = 4
GROUP_DIM = 128
B_HEADS = 8
B_KV_HEADS = 2
B_HEAD_DIM = 64
B_WIDTH = 512
B_KV_WIDTH = 128
WINDOW = 128
C_HEADS = 4
C_HEAD_DIM = 128
C_WIDTH = 512
C_CONV = 4
EPS = 1e-6
NEG = -1e30

LANES = 128
SUBLANES = 8
VMEM_LIMIT = 56 * 1024 * 1024

ZA_W = 3 * A_WIDTH
ZB_W = 2 * B_WIDTH + 2 * B_KV_WIDTH
ZC_W = 2 * C_WIDTH + 3 * C_WIDTH + LANES
ZCAT_W = ZA_W + ZB_W + ZC_W
Y_W = A_WIDTH + B_WIDTH + C_WIDTH

PROMPT_TILE = 256
MLSTM_CHUNK = PROMPT_TILE
SAMPLE_NB = 16
PROJ_TILE = 512


def _sigmoid(x):
    return 0.5 * jnp.tanh(0.5 * x) + 0.5


def _silu(x):
    t = 0.5 * x
    return t * (jnp.tanh(t) + 1.0)


def _log_sigmoid(x):
    return jnp.minimum(x, 0.0) - jnp.log1p(jnp.exp(-jnp.abs(x)))


def _rms(x):
    return x * lax.rsqrt(jnp.mean(x * x, axis=-1, keepdims=True) + EPS)


def _dot(a, b):
    return jnp.dot(a, b, preferred_element_type=F32)


def _dot_nt(a, b):
    return lax.dot_general(a, b, (((1,), (1,)), ((), ())), preferred_element_type=F32)


def _dot_exact01(m01, x):
    hi = x.astype(BF16)
    r1 = x - hi.astype(F32)
    mid = r1.astype(BF16)
    lo = (r1 - mid.astype(F32)).astype(BF16)
    return _dot(m01, hi) + _dot(m01, mid) + _dot(m01, lo)


def _modulated_norm(x, mod_ref, ng_ref):
    xn = _rms(x) * ng_ref[...]
    shift = mod_ref[:, 0:D_MODEL]
    scale = mod_ref[:, D_MODEL:2 * D_MODEL]
    return (xn * (1.0 + scale) + shift).astype(BF16)


def _head_rms_scale(x2, lane_lo):
    s0 = jnp.sum(jnp.where(lane_lo, x2, 0.0), axis=-1, keepdims=True)
    s1 = jnp.sum(jnp.where(lane_lo, 0.0, x2), axis=-1, keepdims=True)
    r0 = lax.rsqrt(s0 * (1.0 / B_HEAD_DIM) + EPS)
    r1 = lax.rsqrt(s1 * (1.0 / B_HEAD_DIM) + EPS)
    return jnp.where(lane_lo, r0, r1)


def _qk_norm(x, g_row):
    rows, width = x.shape
    lane_lo = lax.broadcasted_iota(jnp.int32, (rows, LANES), 1) < B_HEAD_DIM
    outs = []
    for j in range(width // LANES):
        slab = x[:, j * LANES:(j + 1) * LANES]
        outs.append(slab * _head_rms_scale(slab * slab, lane_lo))
    y = outs[0] if len(outs) == 1 else jnp.concatenate(outs, axis=1)
    return y * g_row


def _ada_kernel(c_ref, w_ref, b_ref, o_ref):
    c = c_ref[...]
    o_ref[...] = _dot(_silu(c).astype(BF16), w_ref[...].astype(BF16)) + b_ref[...]


def _ada_call(c_all, ada_w, ada_b):
    rows = c_all.shape[0]
    return pl.pallas_call(
        _ada_kernel,
        grid=(DEPTH, 3),
        in_specs=[
            pl.BlockSpec((rows, D_MODEL), lambda l, j: (0, 0)),
            pl.BlockSpec((None, D_MODEL, D_MODEL), lambda l, j: (l, 0, j)),
            pl.BlockSpec((None, 1, D_MODEL), lambda l, j: (l, 0, j)),
        ],
        out_specs=pl.BlockSpec((None, rows, D_MODEL), lambda l, j: (l, 0, j)),
        out_shape=jax.ShapeDtypeStruct((DEPTH, rows, 3 * D_MODEL), F32),
        compiler_params=pltpu.CompilerParams(
            dimension_semantics=("arbitrary", "arbitrary"), vmem_limit_bytes=VMEM_LIMIT),
        name="adaln_mod",
    )(c_all, ada_w, ada_b.reshape(DEPTH, 1, 3 * D_MODEL))


COL_CI = ZA_W + ZB_W + 3 * C_WIDTH
COL_CO = COL_CI + 2 * C_HEADS
COL_MG = COL_CO + 2 * C_WIDTH
PREP_CHUNK = 256
PREP_SHIFT = 2 * C_HEADS
N_MAIN = COL_CI // PREP_CHUNK
N_CO = (2 * C_WIDTH) // PREP_CHUNK
N_MG = (3 * D_MODEL) // PREP_CHUNK
J_CIF = N_MAIN + N_CO
J_MG = J_CIF + 1


def _weight_prep_kernel(wa_ref, wb_ref, wcat_ref, wmg_ref):
    j = pl.program_id(1)

    def shifted_t():
        rows = jnp.concatenate([wa_ref[PREP_SHIFT:PREP_CHUNK, :], wb_ref[...]], axis=0)
        return rows.astype(BF16).T

    @pl.when(j < N_MAIN)
    def _():
        wcat_ref[...] = wa_ref[...].astype(BF16).T

    @pl.when((j >= N_MAIN) & (j < J_CIF))
    def _():
        wcat_ref[...] = shifted_t()

    @pl.when(j == J_CIF)
    def _():
        row = lax.broadcasted_iota(jnp.int32, (PREP_CHUNK, D_MODEL), 0)
        wcat_ref[...] = jnp.where(row < PREP_SHIFT, wa_ref[...], 0.0).astype(BF16).T

    @pl.when(j >= J_MG)
    def _():
        wmg_ref[...] = shifted_t()


def _weight_prep_call(w_in):
    in_width = w_in.shape[-1]
    assert in_width == COL_MG + 3 * D_MODEL
    assert COL_CI % PREP_CHUNK == 0 and COL_CO % PREP_CHUNK == PREP_SHIFT == COL_MG % PREP_CHUNK
    w_t = jnp.swapaxes(w_in, 1, 2)
    assert in_width % PREP_SHIFT == 0 and PREP_SHIFT == SUBLANES
    last_rows = in_width // PREP_SHIFT - 1
    groups_per_chunk = PREP_CHUNK // PREP_SHIFT

    def src_block(j):
        return jnp.where(j < J_CIF, j, jnp.where(j == J_CIF, N_MAIN, j - 1))

    return pl.pallas_call(
        _weight_prep_kernel,
        grid=(DEPTH, J_MG + N_MG),
        in_specs=[
            pl.BlockSpec((None, PREP_CHUNK, D_MODEL), lambda l, j: (l, src_block(j), 0)),
            pl.BlockSpec((None, PREP_SHIFT, D_MODEL),
                         lambda l, j: (l, jnp.minimum((src_block(j) + 1) * groups_per_chunk,
                                                      last_rows), 0)),
        ],
        out_specs=[
            pl.BlockSpec((None, D_MODEL, PREP_CHUNK), lambda l, j: (l, 0, jnp.minimum(j, J_CIF))),
            pl.BlockSpec((None, D_MODEL, PREP_CHUNK), lambda l, j: (l, 0, jnp.maximum(j - J_MG, 0))),
        ],
        out_shape=[
            jax.ShapeDtypeStruct((DEPTH, D_MODEL, ZCAT_W), BF16),
            jax.ShapeDtypeStruct((DEPTH, D_MODEL, 3 * D_MODEL), BF16),
        ],
        compiler_params=pltpu.CompilerParams(
            dimension_semantics=("arbitrary", "arbitrary"), vmem_limit_bytes=VMEM_LIMIT),
        name="weight_prep",
    )(w_t, w_t)


def _col_chunks(width, step):
    return [(o, min(step, width - o)) for o in range(0, width, step)]


def _inproj_pieces(get_h, w_ref, b_ref, za_ref, zb_ref, zc_ref, step):
    def piece(o_ref, off, woff, w):
        def run():
            o_ref[:, off:off + w] = _dot(get_h(), w_ref[:, woff:woff + w]) + b_ref[:, woff:woff + w]
        return run
    pieces = []
    base = 0
    for o_ref, width in ((za_ref, ZA_W), (zb_ref, ZB_W), (zc_ref, ZC_W)):
        pieces += [piece(o_ref, off, base + off, w) for off, w in _col_chunks(width, step)]
        base += width
    return pieces


def _inproj_kernel(x_ref, mod_ref, ng_ref, w_ref, b_ref, za_ref, zb_ref, zc_ref):
    h = _modulated_norm(x_ref[...], mod_ref, ng_ref)
    for piece in _inproj_pieces(lambda: h, w_ref, b_ref, za_ref, zb_ref, zc_ref, 512):
        piece()


def _mod_spec(tm, tokens_per_batch):
    if tokens_per_batch is None:
        return pl.BlockSpec((tm, 3 * D_MODEL), lambda i: (i, 0))
    tiles_per_batch = tokens_per_batch // tm
    return pl.BlockSpec((None, 1, 3 * D_MODEL), lambda i: (i // tiles_per_batch, 0, 0))


def _layer_weight_spec(layer, rows, cols):
    return pl.BlockSpec((None, rows, cols), lambda i: (layer, 0, 0), pipeline_mode=pl.Buffered(1))


def _inproj_call(layer, x2, mod, ng, wcat, bcat, tokens_per_batch):
    ntok = x2.shape[0]
    tm = PROJ_TILE
    const = lambda i: (0, 0)
    return pl.pallas_call(
        _inproj_kernel,
        grid=(ntok // tm,),
        in_specs=[
            pl.BlockSpec((tm, D_MODEL), lambda i: (i, 0)),
            _mod_spec(tm, tokens_per_batch),
            pl.BlockSpec((1, D_MODEL), const),
            _layer_weight_spec(layer, D_MODEL, ZCAT_W),
            pl.BlockSpec((1, ZCAT_W), const),
        ],
        out_specs=[
            pl.BlockSpec((tm, ZA_W), lambda i: (i, 0)),
            pl.BlockSpec((tm, ZB_W), lambda i: (i, 0)),
            pl.BlockSpec((tm, ZC_W), lambda i: (i, 0)),
        ],
        out_shape=[
            jax.ShapeDtypeStruct((ntok, ZA_W), F32),
            jax.ShapeDtypeStruct((ntok, ZB_W), F32),
            jax.ShapeDtypeStruct((ntok, ZC_W), F32),
        ],
        compiler_params=pltpu.CompilerParams(
            dimension_semantics=("arbitrary",), vmem_limit_bytes=VMEM_LIMIT),
        name="in_projection",
    )(x2, mod, ng, wcat, bcat)


def _outproj_kernel(x_ref, mod_ref, ng_ref, y_ref, wmg_ref, bmg_ref, wa_ref, wb_ref, wc_ref,
                    wo_ref, o_ref):
    x = x_ref[...]
    h = _modulated_norm(x, mod_ref, ng_ref)
    merged = None
    for i, wbr_ref in enumerate((wa_ref, wb_ref, wc_ref)):
        cols = slice(i * D_MODEL, (i + 1) * D_MODEL)
        gate = _sigmoid(_dot(h, wmg_ref[:, cols]) + bmg_ref[:, cols])
        term = gate * _dot(y_ref[:, i * A_WIDTH:(i + 1) * A_WIDTH], wbr_ref[...])
        merged = term if merged is None else merged + term
    ada_gate = mod_ref[:, 2 * D_MODEL:3 * D_MODEL]
    o_ref[...] = x + ada_gate * _dot(merged.astype(BF16), wo_ref[...])


def _outproj_call(layer, x2, mod, ng, y, wmg, bmg, wa, wb, wc, wo, tokens_per_batch):
    ntok = x2.shape[0]
    tm = PROJ_TILE
    const = lambda i: (0, 0)
    once = pl.Buffered(1)
    return pl.pallas_call(
        _outproj_kernel,
        grid=(ntok // tm,),
        in_specs=[
            pl.BlockSpec((tm, D_MODEL), lambda i: (i, 0)),
            _mod_spec(tm, tokens_per_batch),
            pl.BlockSpec((1, D_MODEL), const),
            pl.BlockSpec((tm, Y_W), lambda i: (i, 0)),
            _layer_weight_spec(layer, D_MODEL, 3 * D_MODEL),
            pl.BlockSpec((1, 3 * D_MODEL), const),
            pl.BlockSpec((A_WIDTH, D_MODEL), const, pipeline_mode=once),
            pl.BlockSpec((B_WIDTH, D_MODEL), const, pipeline_mode=once),
            pl.BlockSpec((C_WIDTH, D_MODEL), const, pipeline_mode=once),
            pl.BlockSpec((D_MODEL, D_MODEL), const, pipeline_mode=once),
        ],
        out_specs=pl.BlockSpec((tm, D_MODEL), lambda i: (i, 0)),
        out_shape=jax.ShapeDtypeStruct((ntok, D_MODEL), F32),
        compiler_params=pltpu.CompilerParams(
            dimension_semantics=("arbitrary",), vmem_limit_bytes=VMEM_LIMIT),
        name="out_projection",
    )(x2, mod, ng, y, wmg, bmg, wa, wb, wc, wo)


def _place_q_head(qn, h, rows):
    lane = lax.broadcasted_iota(jnp.int32, (rows, LANES), 1)
    slab = qn[:, (h // 2) * LANES:(h // 2 + 1) * LANES]
    src_hi = h % 2
    dst_hi = h // (B_HEADS // B_KV_HEADS)
    keep = (lane >= B_HEAD_DIM) if src_hi else (lane < B_HEAD_DIM)
    slab = jnp.where(keep, slab, 0.0)
    if src_hi != dst_hi:
        slab = pltpu.roll(slab, B_HEAD_DIM, 1)
    return slab


def _merge_head_pair(o_even, o_odd, h_even, rows):
    lane_lo = lax.broadcasted_iota(jnp.int32, (rows, LANES), 1) < B_HEAD_DIM
    kv_hi = h_even // (B_HEADS // B_KV_HEADS)
    if kv_hi:
        o_even = pltpu.roll(o_even, B_HEAD_DIM, 1)
    else:
        o_odd = pltpu.roll(o_odd, B_HEAD_DIM, 1)
    return jnp.where(lane_lo, o_even, o_odd)


def _conv_taps(xbuf, cw_ref, cb_ref, cols, ts):
    y = cb_ref[:, cols]
    for j in range(C_CONV):
        lo = SUBLANES - (C_CONV - 1) + j
        y = y + cw_ref[j:j + 1, cols] * xbuf[lo:lo + ts, cols]
    return y


def _prompt_mix_kernel(sink_ref, za_ref, zb_ref, zc_ref, vg_ref, gw_ref, gbs_ref, qg_ref, kg_ref,
                       cw_ref, cb_ref, fb_ref, hg_ref, tril_ref, band_ref, tri01_ref, tribias_ref,
                       y_ref, ko_ref, vo_ref, convo_ref, c_ref, n_ref, m_ref,
                       kprev, vprev, xbuf, first_tile, pump):
    ts = PROMPT_TILE

    wts = [(gw_ref[gi] * tril_ref[...]).astype(BF16) for gi in range(A_GROUPS)]
    for c in range(ts // WINDOW):
        rows = slice(c * WINDOW, (c + 1) * WINDOW)
        vnb = (_rms(za_ref[rows, A_WIDTH:2 * A_WIDTH]) * vg_ref[...]).astype(BF16)
        s = jnp.concatenate(
            [_dot(wts[gi], vnb[:, gi * GROUP_DIM:(gi + 1) * GROUP_DIM]) + gbs_ref[:, gi:gi + 1]
             for gi in range(A_GROUPS)], axis=1)
        sg = _silu(za_ref[rows, 2 * A_WIDTH:3 * A_WIDTH])
        y_ref[rows, 0:A_WIDTH] = (za_ref[rows, 0:A_WIDTH] * s * sg).astype(BF16)
        pump()

    kn = _qk_norm(zb_ref[:, B_WIDTH:B_WIDTH + B_KV_WIDTH], kg_ref[...])
    vv = zb_ref[:, B_WIDTH + B_KV_WIDTH:B_WIDTH + 2 * B_KV_WIDTH]
    pump()
    grp = B_HEADS // B_KV_HEADS
    nblk = ts // WINDOW
    lane_lo2 = lax.broadcasted_iota(jnp.int32, (2 * WINDOW, LANES), 1) < B_HEAD_DIM
    kblocks = [kprev[...]] + [kn[b * WINDOW:(b + 1) * WINDOW] for b in range(nblk)]
    vblocks = [vprev[...]] + [vv[b * WINDOW:(b + 1) * WINDOW] for b in range(nblk)]
    heads = [(kh, g) for kh in range(B_KV_HEADS) for g in range(grp)]
    snk = {k: sink_ref[k[0] * grp + k[1]] for k in heads}
    for blk in range(nblk):
        rows = slice(blk * WINDOW, (blk + 1) * WINDOW)
        if blk == 0 and first_tile is not False:
            bias = jnp.where(first_tile, band_ref[1], band_ref[0])
        else:
            bias = band_ref[0]
        kcat = jnp.concatenate([kblocks[blk], kblocks[blk + 1]], axis=0)
        vcat = jnp.concatenate([vblocks[blk], vblocks[blk + 1]], axis=0)
        krol = pltpu.roll(kcat, B_HEAD_DIM, 1)
        vrol = pltpu.roll(vcat, B_HEAD_DIM, 1)
        kdup, vdup = [], []
        for kh in range(B_KV_HEADS):
            own = lane_lo2 if kh == 0 else jnp.logical_not(lane_lo2)
            kdup.append(jnp.where(own, kcat, krol).astype(BF16))
            vdup.append(jnp.where(own, vcat, vrol).astype(BF16))
        qn = _qk_norm(zb_ref[rows, 0:B_WIDTH], qg_ref[...]) * (B_HEAD_DIM ** -0.5)
        pump()
        qs = [jnp.concatenate([_place_q_head(qn, kh * grp + g, WINDOW) for g in range(grp)],
                              axis=0).astype(BF16) for kh in range(B_KV_HEADS)]
        logits = [_dot_nt(qs[kh], kdup[kh]) for kh in range(B_KV_HEADS)]
        pump()
        lg = {(kh, g): logits[kh][g * WINDOW:(g + 1) * WINDOW] + bias for kh, g in heads}
        mx = {k: jnp.maximum(jnp.max(lg[k], axis=-1, keepdims=True), snk[k]) for k in heads}
        pump()
        p = {k: jnp.exp(lg[k] - mx[k]) for k in heads}
        pump()
        rden = {k: 1.0 / (jnp.sum(p[k], axis=-1, keepdims=True) + jnp.exp(snk[k] - mx[k]))
                for k in heads}
        pump()
        pv = [_dot(jnp.concatenate([p[kh, g].astype(BF16) for g in range(grp)], axis=0), vdup[kh])
              for kh in range(B_KV_HEADS)]
        pump()
        outs = {(kh, g): pv[kh][g * WINDOW:(g + 1) * WINDOW] * rden[kh, g] for kh, g in heads}
        yb = jnp.concatenate(
            [_merge_head_pair(outs[(2 * j) // grp, (2 * j) % grp],
                              outs[(2 * j + 1) // grp, (2 * j + 1) % grp], 2 * j, WINDOW)
             for j in range(B_HEADS // 2)], axis=1)
        sgb = _silu(zb_ref[rows, B_WIDTH + 2 * B_KV_WIDTH:ZB_W])
        y_ref[rows, A_WIDTH:A_WIDTH + B_WIDTH] = (yb * sgb).astype(BF16)
        pump()
    kprev[...] = kblocks[nblk]
    vprev[...] = vblocks[nblk]
    ko_ref[...] = kblocks[nblk]
    vo_ref[...] = vblocks[nblk]
    pump()

    xbuf[SUBLANES:SUBLANES + ts, :] = zc_ref[:, 0:2 * C_WIDTH]
    pump()
    ifp = zc_ref[:, 5 * C_WIDTH:5 * C_WIDTH + LANES]
    lf = _log_sigmoid(ifp + fb_ref[...])
    pump()
    cl = MLSTM_CHUNK
    lane_c = lax.broadcasted_iota(jnp.int32, (cl, LANES), 1)
    lane_1 = lax.broadcasted_iota(jnp.int32, (1, LANES), 1)
    m_row = m_ref[...]
    m_out = m_row
    cum_all = _dot_exact01(tri01_ref[...], lf)
    st_col = jnp.where(lane_c < C_HEADS, ifp, cum_all)
    st_row = st_col.T
    tribias = tribias_ref[...]
    pump()
    for hds in MLSTM_HEAD_GROUPS:
        hs = {hd: slice(hd * C_HEAD_DIM, (hd + 1) * C_HEAD_DIM) for hd in hds}
        i_c = {hd: st_col[:, hd:hd + 1] for hd in hds}
        cum_c = {hd: st_col[:, C_HEADS + hd:C_HEADS + hd + 1] for hd in hds}
        i_r = {hd: st_row[hd:hd + 1, :] for hd in hds}
        cum_r = {hd: st_row[C_HEADS + hd:C_HEADS + hd + 1, :] for hd in hds}
        m_prev = {hd: m_row[:, hd:hd + 1] for hd in hds}
        dmat = {hd: cum_c[hd] - cum_r[hd] + i_r[hd] + tribias for hd in hds}
        m_inter = {hd: cum_c[hd] + m_prev[hd] for hd in hds}
        m_t = {hd: jnp.maximum(m_inter[hd], jnp.max(dmat[hd], axis=-1, keepdims=True)) for hd in hds}
        pump()
        q_h = {hd: _silu(_conv_taps(xbuf, cw_ref, cb_ref, hs[hd], ts)).astype(BF16) for hd in hds}
        k_h = {hd: _silu(_conv_taps(xbuf, cw_ref, cb_ref,
                                    slice(C_WIDTH + hs[hd].start, C_WIDTH + hs[hd].stop), ts))
               * (C_HEAD_DIM ** -0.5) for hd in hds}
        pump()
        v_h = {hd: zc_ref[:, 2 * C_WIDTH + hd * C_HEAD_DIM:2 * C_WIDTH + (hd + 1) * C_HEAD_DIM].astype(BF16)
               for hd in hds}
        s_qk = {hd: _dot_nt(q_h[hd], k_h[hd].astype(BF16)) for hd in hds}
        a = {hd: jnp.exp(dmat[hd] - m_t[hd]) * s_qk[hd] for hd in hds}
        pump()
        w_inter = {hd: jnp.exp(m_inter[hd] - m_t[hd]) for hd in hds}
        c_prev = {hd: c_ref[hd] for hd in hds}
        n_prev = {hd: n_ref[hd:hd + 1, :] for hd in hds}
        inter = {hd: _dot(q_h[hd], c_prev[hd].astype(BF16)) for hd in hds}
        intra = {hd: _dot(a[hd].astype(BF16), v_h[hd]) for hd in hds}
        pump()
        den = {hd: jnp.sum(a[hd], axis=-1, keepdims=True)
               + w_inter[hd] * jnp.sum(q_h[hd].astype(F32) * n_prev[hd], axis=-1, keepdims=True)
               for hd in hds}
        rnorm = {hd: 1.0 / jnp.maximum(jnp.abs(den[hd]), jnp.exp(-m_t[hd])) for hd in hds}
        hh = {hd: (intra[hd] + w_inter[hd] * inter[hd]) * rnorm[hd] for hd in hds}
        pump()
        for hd in hds:
            o_cols = slice(3 * C_WIDTH + hd * C_HEAD_DIM, 3 * C_WIDTH + (hd + 1) * C_HEAD_DIM)
            g_cols = slice(4 * C_WIDTH + hd * C_HEAD_DIM, 4 * C_WIDTH + (hd + 1) * C_HEAD_DIM)
            gate_o = _sigmoid(zc_ref[:, o_cols]) * _silu(zc_ref[:, g_cols])
            y_cols = slice(A_WIDTH + B_WIDTH + hd * C_HEAD_DIM, A_WIDTH + B_WIDTH + (hd + 1) * C_HEAD_DIM)
            y_ref[:, y_cols] = (_rms(hh[hd]) * hg_ref[:, hs[hd]] * gate_o).astype(BF16)
        pump()
        total = {hd: cum_r[hd][:, cl - 1:cl] for hd in hds}
        g_r = {hd: total[hd] - cum_r[hd] + i_r[hd] for hd in hds}
        g_c = {hd: total[hd] - cum_c[hd] + i_c[hd] for hd in hds}
        m_new = {hd: jnp.maximum(total[hd] + m_prev[hd], jnp.max(g_r[hd], axis=-1, keepdims=True))
                 for hd in hds}
        kw = {hd: jnp.exp(g_c[hd] - m_new[hd]) * k_h[hd] for hd in hds}
        decay = {hd: jnp.exp(total[hd] + m_prev[hd] - m_new[hd]) for hd in hds}
        pump()
        upd = {hd: _dot(kw[hd].T.astype(BF16), v_h[hd]) for hd in hds}
        for hd in hds:
            c_ref[hd] = decay[hd] * c_prev[hd] + upd[hd]
            n_ref[hd:hd + 1, :] = decay[hd] * n_prev[hd] + jnp.sum(kw[hd], axis=0, keepdims=True)
            m_out = jnp.where(lane_1 == hd, m_new[hd], m_out)
        pump()
    m_ref[...] = m_out
    tail = xbuf[ts:ts + SUBLANES, :]
    xbuf[0:SUBLANES, :] = tail
    convo_ref[...] = tail


def _prompt_mask_constants():
    r = np.arange(WINDOW)[:, None]
    c = np.arange(2 * WINDOW)[None, :]
    band = (c > r) & (c <= r + WINDOW)
    band_first = band & (c >= WINDOW)
    band_bias = np.where(np.stack([band, band_first]), 0.0, NEG).astype(np.float32)
    tril = (np.arange(WINDOW)[:, None] >= np.arange(WINDOW)[None, :]).astype(np.float32)
    tri = np.arange(MLSTM_CHUNK)[:, None] >= np.arange(MLSTM_CHUNK)[None, :]
    return (jnp.asarray(tril), jnp.asarray(band_bias), jnp.asarray(tri, dtype=BF16),
            jnp.asarray(np.where(tri, 0.0, NEG).astype(np.float32)))


N_MIX_PARAMS = 13
MLSTM_HEAD_GROUPS = ((0, 1), (2, 3))
MIX_PUMP_CALLS = 37
TAIL_FILL_PIECES = 4
MXU_PIECE_COLS = 512


class _Interleaver:
    def __init__(self, pieces, calls, hold_back=0):
        self._pieces = list(pieces)
        self._hold_back = hold_back
        self._spread = len(self._pieces) - hold_back
        self._emitted = 0
        self._calls = calls
        self._call = 0

    def __call__(self):
        self._call += 1
        target = (self._call * self._spread) // self._calls
        while self._emitted < target:
            self._pieces.pop(0)()
            self._emitted += 1

    def finish(self):
        assert self._call == self._calls and len(self._pieces) == self._hold_back, self._call
        return self._pieces


def _gate_pieces(h_ref, wmg_ref, bmg_ref, g_ref):
    def piece(off):
        cols = slice(off, off + MXU_PIECE_COLS)
        def run():
            g_ref[:, cols] = _sigmoid(_dot(h_ref[...], wmg_ref[:, cols]) + bmg_ref[:, cols])
        return run
    return [piece(off) for off in range(0, 3 * D_MODEL, MXU_PIECE_COLS)]


def _merge_and_project(x, mod_ref, g_ref, y_ref, wa_ref, wb_ref, wc_ref, wo_ref, fillers=()):
    fillers = list(fillers)
    per_stage = -(-len(fillers) // 4)
    merged = None
    for i, wbr_ref in enumerate((wa_ref, wb_ref, wc_ref)):
        for piece in fillers[i * per_stage:(i + 1) * per_stage]:
            piece()
        term = (g_ref[:, i * D_MODEL:(i + 1) * D_MODEL]
                * _dot(y_ref[:, i * A_WIDTH:(i + 1) * A_WIDTH], wbr_ref[...]))
        merged = term if merged is None else merged + term
    for piece in fillers[3 * per_stage:]:
        piece()
    ada_gate = mod_ref[:, 2 * D_MODEL:3 * D_MODEL]
    return x + ada_gate * _dot(merged.astype(BF16), wo_ref[...])


def _prompt_layer_kernel(tiles_per_seq, sink_ref, x2_ref, xn_ref, mod_ref, modn_ref, ng_ref,
                         wcat_ref, bcat_ref, *rest):
    mix_params = rest[:N_MIX_PARAMS]
    wmg_ref, bmg_ref, wa_ref, wb_ref, wc_ref, wo_ref = rest[N_MIX_PARAMS:N_MIX_PARAMS + 6]
    o_ref, ko_ref, vo_ref, convo_ref, c_ref, n_ref, m_ref = rest[N_MIX_PARAMS + 6:N_MIX_PARAMS + 13]
    (za0, zb0, zc0, za1, zb1, zc1, h0, h1, y_scr, g_scr, kprev, vprev, xbuf) = rest[N_MIX_PARAMS + 13:]
    ts = PROMPT_TILE
    z = ((za0, zb0, zc0), (za1, zb1, zc1))
    h = (h0, h1)
    k = pl.program_id(0)
    seq_start = (k % (tiles_per_seq // 2)) == 0

    @pl.when(k == 0)
    def _():
        h0[...] = _modulated_norm(x2_ref[0:ts, :], mod_ref, ng_ref)
        for piece in _inproj_pieces(lambda: h0[...], wcat_ref, bcat_ref, *z[0], 512):
            piece()

    @pl.when(seq_start)
    def _():
        kprev[...] = jnp.zeros_like(kprev)
        vprev[...] = jnp.zeros_like(vprev)
        xbuf[0:SUBLANES, :] = jnp.zeros((SUBLANES, 2 * C_WIDTH), F32)
        c_ref[...] = jnp.zeros_like(c_ref)
        n_ref[...] = jnp.zeros_like(n_ref)
        m_ref[...] = jnp.zeros_like(m_ref)

    for half in range(2):
        cur, nxt = half, 1 - half
        rows = slice(half * ts, (half + 1) * ts)
        if half == 0:
            h[nxt][...] = _modulated_norm(x2_ref[ts:2 * ts, :], mod_ref, ng_ref)
        else:
            h[nxt][...] = _modulated_norm(xn_ref[...], modn_ref, ng_ref)
        get_h_next = functools.partial(lambda r: r[...], h[nxt])
        hold = TAIL_FILL_PIECES if half == 1 else 0
        proj = _inproj_pieces(get_h_next, wcat_ref, bcat_ref, *z[nxt], MXU_PIECE_COLS)
        pump = _Interleaver(
            proj[:len(proj) - hold] + _gate_pieces(h[cur], wmg_ref, bmg_ref, g_scr)
            + proj[len(proj) - hold:], MIX_PUMP_CALLS, hold_back=hold)
        _prompt_mix_kernel(sink_ref, *z[cur], *mix_params,
                           y_scr, ko_ref, vo_ref, convo_ref, c_ref, n_ref, m_ref, kprev, vprev, xbuf,
                           first_tile=seq_start if half == 0 else False, pump=pump)
        o_ref[rows, :] = _merge_and_project(x2_ref[rows, :], mod_ref, g_scr, y_scr,
                                            wa_ref, wb_ref, wc_ref, wo_ref, fillers=pump.finish())


def _prompt_layer_call(layer, x2, mod, lw, batch, seq):
    ts = PROMPT_TILE
    nt = seq // ts
    assert nt % 2 == 0
    last_tile = batch * nt - 1
    const2 = lambda k: (0, 0)
    const3 = lambda k: (0, 0, 0)
    per_b3 = lambda k: ((2 * k) // nt, 0, 0)
    next_tile = lambda k: jnp.minimum(2 * k + 2, last_tile)
    once = pl.Buffered(1)
    return pl.pallas_call(
        functools.partial(_prompt_layer_kernel, nt),
        grid=(batch * nt // 2,),
        in_specs=[
            pl.BlockSpec(memory_space=pltpu.SMEM),
            pl.BlockSpec((2 * ts, D_MODEL), lambda k: (k, 0)),
            pl.BlockSpec((ts, D_MODEL), lambda k: (next_tile(k), 0)),
            pl.BlockSpec((None, 1, 3 * D_MODEL), per_b3),
            pl.BlockSpec((None, 1, 3 * D_MODEL), lambda k: (next_tile(k) // nt, 0, 0)),
            pl.BlockSpec((1, D_MODEL), const2),
            _layer_weight_spec(layer, D_MODEL, ZCAT_W),
            pl.BlockSpec((1, ZCAT_W), const2),
            pl.BlockSpec((1, A_WIDTH), const2),
            pl.BlockSpec((A_GROUPS, WINDOW, WINDOW), const3),
            pl.BlockSpec((WINDOW, LANES), const2),
            pl.BlockSpec((1, B_WIDTH), const2),
            pl.BlockSpec((1, B_KV_WIDTH), const2),
            pl.BlockSpec((C_CONV, 2 * C_WIDTH), const2),
            pl.BlockSpec((1, 2 * C_WIDTH), const2),
            pl.BlockSpec((1, LANES), const2),
            pl.BlockSpec((1, C_WIDTH), const2),
            pl.BlockSpec((WINDOW, WINDOW), const2),
            pl.BlockSpec((2, WINDOW, 2 * WINDOW), const3),
            pl.BlockSpec((MLSTM_CHUNK, MLSTM_CHUNK), const2),
            pl.BlockSpec((MLSTM_CHUNK, MLSTM_CHUNK), const2),
            _layer_weight_spec(layer, D_MODEL, 3 * D_MODEL),
            pl.BlockSpec((1, 3 * D_MODEL), const2),
            pl.BlockSpec((A_WIDTH, D_MODEL), const2, pipeline_mode=once),
            pl.BlockSpec((B_WIDTH, D_MODEL), const2, pipeline_mode=once),
            pl.BlockSpec((C_WIDTH, D_MODEL), const2, pipeline_mode=once),
            pl.BlockSpec((D_MODEL, D_MODEL), const2, pipeline_mode=once),
        ],
        out_specs=[
            pl.BlockSpec((2 * ts, D_MODEL), lambda k: (k, 0)),
            pl.BlockSpec((None, WINDOW, B_KV_WIDTH), per_b3),
            pl.BlockSpec((None, WINDOW, B_KV_WIDTH), per_b3),
            pl.BlockSpec((None, SUBLANES, 2 * C_WIDTH), per_b3),
            pl.BlockSpec((None, C_HEADS, C_HEAD_DIM, C_HEAD_DIM), lambda k: ((2 * k) // nt, 0, 0, 0)),
            pl.BlockSpec((None, C_HEADS, C_HEAD_DIM), per_b3),
            pl.BlockSpec((None, 1, LANES), per_b3),
        ],
        out_shape=[
            jax.ShapeDtypeStruct((batch * seq, D_MODEL), F32),
            jax.ShapeDtypeStruct((batch, WINDOW, B_KV_WIDTH), F32),
            jax.ShapeDtypeStruct((batch, WINDOW, B_KV_WIDTH), F32),
            jax.ShapeDtypeStruct((batch, SUBLANES, 2 * C_WIDTH), F32),
            jax.ShapeDtypeStruct((batch, C_HEADS, C_HEAD_DIM, C_HEAD_DIM), F32),
            jax.ShapeDtypeStruct((batch, C_HEADS, C_HEAD_DIM), F32),
            jax.ShapeDtypeStruct((batch, 1, LANES), F32),
        ],
        scratch_shapes=(
            [pltpu.VMEM((ts, w), F32) for w in (ZA_W, ZB_W, ZC_W)] * 2
            + [pltpu.VMEM((ts, D_MODEL), BF16)] * 2
            + [pltpu.VMEM((ts, Y_W), BF16),
               pltpu.VMEM((ts, 3 * D_MODEL), F32),
               pltpu.VMEM((WINDOW, B_KV_WIDTH), F32),
               pltpu.VMEM((WINDOW, B_KV_WIDTH), F32),
               pltpu.VMEM((ts + SUBLANES, 2 * C_WIDTH), F32)]),
        compiler_params=pltpu.CompilerParams(
            dimension_semantics=("arbitrary",), vmem_limit_bytes=VMEM_LIMIT),
        name="prompt_layer",
    )(lw["sinks"], x2, x2, mod, mod, lw["ng"], lw["wcat"], lw["bcat"],
      lw["vg"], lw["gws"], lw["gbs_col"], lw["qg"], lw["kg"], lw["cw"], lw["cb"], lw["fb"], lw["hg"],
      *_prompt_mask_constants(),
      lw["wmg"], lw["bmg"], lw["wa"], lw["wb"], lw["wc"], lw["wo"])


def _sample_mix_kernel(sink_ref, za_ref, zb_ref, zc_ref, kc_ref, vc_ref, cs_ref, c0_ref, n0_ref,
                       m0_ref, vg_ref, gwb_ref, gbs_ref, qg_ref, kg_ref, cw_ref, cb_ref, fb_ref,
                       hg_ref,
                       y_ref, vrow_ref, ko_ref, vo_ref, convo_ref, c1_ref, n1_ref, m1_ref,
                       xbuf, first_layer=None):
    nb = SAMPLE_NB
    t = SUBLANES
    rows = nb * t
    if first_layer is not None:
        for other in range(DEPTH):
            if other != first_layer:
                c1_ref[other] = jnp.zeros(c1_ref.shape[1:], F32)
        c1_ref = c1_ref.at[first_layer]
    tok_r = lax.broadcasted_iota(jnp.int32, (rows, rows), 0)
    tok_c = lax.broadcasted_iota(jnp.int32, (rows, rows), 1)
    same_b = (tok_r // t) == (tok_c // t)
    causal_b = same_b & (tok_c <= tok_r)

    u = za_ref[:, 0:A_WIDTH]
    vn = _rms(za_ref[:, A_WIDTH:2 * A_WIDTH]) * vg_ref[...]
    sg = _silu(za_ref[:, 2 * A_WIDTH:3 * A_WIDTH])
    vrow_ref[...] = vn
    vnb = vn.astype(BF16)
    s_cols = []
    for gi in range(A_GROUPS):
        s_cols.append(_dot(gwb_ref[gi], vnb[:, gi * GROUP_DIM:(gi + 1) * GROUP_DIM])
                      + gbs_ref[:, gi:gi + 1])
    y_ref[:, 0:A_WIDTH] = (u * jnp.concatenate(s_cols, axis=1) * sg).astype(BF16)

    qn = _qk_norm(zb_ref[:, 0:B_WIDTH], qg_ref[...]) * (B_HEAD_DIM ** -0.5)
    kn = _qk_norm(zb_ref[:, B_WIDTH:B_WIDTH + B_KV_WIDTH], kg_ref[...])
    vv = zb_ref[:, B_WIDTH + B_KV_WIDTH:B_WIDTH + 2 * B_KV_WIDTH]
    sgb = _silu(zb_ref[:, B_WIDTH + 2 * B_KV_WIDTH:ZB_W])
    kn3 = kn.reshape(nb, t, B_KV_WIDTH)
    vv3 = vv.reshape(nb, t, B_KV_WIDTH)
    kcache = kc_ref[...]
    vcache = vc_ref[...]
    pad = jnp.zeros((nb, WINDOW - t, B_KV_WIDTH), F32)
    kall = jnp.concatenate([kcache, kn3, pad], axis=1).astype(BF16)
    vall = jnp.concatenate([vcache, vv3, pad], axis=1).astype(BF16)
    qp = jnp.concatenate([_place_q_head(qn, h, rows).reshape(nb, t, LANES) for h in range(B_HEADS)],
                         axis=1).astype(BF16)
    logits = lax.dot_general(qp, kall, (((2,), (2,)), ((0,), (0,))), preferred_element_type=F32)
    qrow = lax.broadcasted_iota(jnp.int32, (nb, B_HEADS * t, 2 * WINDOW), 1)
    kcol = lax.broadcasted_iota(jnp.int32, (nb, B_HEADS * t, 2 * WINDOW), 2)
    qt = qrow % t
    valid = ((kcol < WINDOW) & (kcol > qt)) | ((kcol >= WINDOW) & ((kcol - WINDOW) <= qt))
    hrow = lax.broadcasted_iota(jnp.int32, (B_HEADS * t, 1), 0) // t
    snk = jnp.zeros((B_HEADS * t, 1), F32)
    for h in range(B_HEADS):
        snk = jnp.where(hrow == h, sink_ref[h], snk)
    lg = jnp.where(valid, logits, NEG)
    mx = jnp.maximum(jnp.max(lg, axis=-1, keepdims=True), snk[None])
    p = jnp.exp(lg - mx)
    den = jnp.sum(p, axis=-1, keepdims=True) + jnp.exp(snk[None] - mx)
    pv = lax.dot_general(p.astype(BF16), vall, (((2,), (1,)), ((0,), (0,))),
                         preferred_element_type=F32) / den
    head_out = [pv[:, h * t:(h + 1) * t, :].reshape(rows, LANES) for h in range(B_HEADS)]
    yb = jnp.concatenate(
        [_merge_head_pair(head_out[2 * j], head_out[2 * j + 1], 2 * j, rows)
         for j in range(B_HEADS // 2)], axis=1)
    y_ref[:, A_WIDTH:A_WIDTH + B_WIDTH] = (yb * sgb).astype(BF16)
    ko_ref[...] = jnp.concatenate([kcache[:, t:, :], kn3], axis=1)
    vo_ref[...] = jnp.concatenate([vcache[:, t:, :], vv3], axis=1)

    xbuf[:, SUBLANES - (C_CONV - 1):SUBLANES, :] = cs_ref[...]
    xbuf[:, SUBLANES:2 * SUBLANES, :] = zc_ref[:, 0:2 * C_WIDTH].reshape(nb, t, 2 * C_WIDTH)
    y3 = cb_ref[...][None]
    for j in range(C_CONV):
        lo = SUBLANES - (C_CONV - 1) + j
        y3 = y3 + cw_ref[j:j + 1, :][None] * xbuf[:, lo:lo + t, :]
    convo_ref[...] = xbuf[:, 2 * SUBLANES - (C_CONV - 1):2 * SUBLANES, :]
    qk = _silu(y3.reshape(rows, 2 * C_WIDTH))
    qall = qk[:, 0:C_WIDTH].astype(BF16)
    kall_c = qk[:, C_WIDTH:2 * C_WIDTH] * (C_HEAD_DIM ** -0.5)
    vall_c = zc_ref[:, 2 * C_WIDTH:3 * C_WIDTH].astype(BF16)
    gate_o = _sigmoid(zc_ref[:, 3 * C_WIDTH:4 * C_WIDTH]) * _silu(zc_ref[:, 4 * C_WIDTH:5 * C_WIDTH])
    ifp = zc_ref[:, 5 * C_WIDTH:5 * C_WIDTH + LANES]
    lf = _log_sigmoid(ifp + fb_ref[...])
    lane_t = lax.broadcasted_iota(jnp.int32, (rows, LANES), 1)
    cum_all = _dot_exact01(jnp.where(causal_b, 1.0, 0.0).astype(BF16), lf)
    tot_all = _dot_exact01(jnp.where(same_b, 1.0, 0.0).astype(BF16), lf)
    st_col = jnp.where(lane_t < C_HEADS, ifp, cum_all)
    st_row = st_col.T
    tot_row = tot_all.T
    m0 = m0_ref[...]
    same_b_bf = jnp.where(same_b, 1.0, 0.0).astype(BF16)
    batch_of_lane = lax.broadcasted_iota(jnp.int32, (nb, 1, rows), 2) // t
    batch_id = lax.broadcasted_iota(jnp.int32, (nb, 1, rows), 0)
    own_tok = batch_of_lane == batch_id
    h_cols = []
    m_out = jnp.zeros((rows, LANES), F32)
    for hd in range(C_HEADS):
        hs = slice(hd * C_HEAD_DIM, (hd + 1) * C_HEAD_DIM)
        i_c = st_col[:, hd:hd + 1]
        cum_c = st_col[:, C_HEADS + hd:C_HEADS + hd + 1]
        tot_c = tot_all[:, C_HEADS + hd:C_HEADS + hd + 1]
        i_r = st_row[hd:hd + 1, :]
        cum_r = st_row[C_HEADS + hd:C_HEADS + hd + 1, :]
        tot_r = tot_row[C_HEADS + hd:C_HEADS + hd + 1, :]
        m_prev = m0[:, hd:hd + 1]
        dmat = jnp.where(causal_b, cum_c - cum_r + i_r, NEG)
        m_inter = cum_c + m_prev
        m_t = jnp.maximum(m_inter, jnp.max(dmat, axis=-1, keepdims=True))
        q_h = qall[:, hs]
        k_h = kall_c[:, hs]
        v_h = vall_c[:, hs]
        a = jnp.exp(dmat - m_t) * _dot_nt(q_h, k_h.astype(BF16))
        w_inter = jnp.exp(m_inter - m_t)
        c_prev = c0_ref[:, hd]
        n_tok = jnp.broadcast_to(n0_ref[hd][:, None, :], (nb, t, C_HEAD_DIM)).reshape(rows, C_HEAD_DIM)
        inter = lax.dot_general(q_h.reshape(nb, t, C_HEAD_DIM), c_prev.astype(BF16),
                                (((2,), (1,)), ((0,), (0,))), preferred_element_type=F32)
        num = _dot(a.astype(BF16), v_h) + w_inter * inter.reshape(rows, C_HEAD_DIM)
        den = (jnp.sum(a, axis=-1, keepdims=True)
               + w_inter * jnp.sum(q_h.astype(F32) * n_tok, axis=-1, keepdims=True))
        hh = num / jnp.maximum(jnp.abs(den), jnp.exp(-m_t))
        h_cols.append(_rms(hh))
        g_r = tot_r - cum_r + i_r
        g_c = tot_c - cum_c + i_c
        m_new = jnp.maximum(tot_c + m_prev,
                            jnp.max(jnp.where(same_b, g_r, NEG), axis=-1, keepdims=True))
        kw = jnp.exp(g_c - m_new) * k_h
        decay = jnp.exp(tot_c + m_prev - m_new)
        kwt = kw.T
        lhs = jnp.where(own_tok, kwt[None], 0.0).astype(BF16).reshape(nb * C_HEAD_DIM, rows)
        upd = _dot(lhs, v_h).reshape(nb, C_HEAD_DIM, C_HEAD_DIM)
        dec_b = jnp.broadcast_to(decay, (rows, C_HEAD_DIM)).reshape(nb, t, C_HEAD_DIM)[:, 0:1, :]
        c1_ref[:, hd] = dec_b * c_prev + upd
        n1_ref[hd] = decay * n_tok + _dot(same_b_bf, kw.astype(BF16))
        m_out = jnp.where(lane_t == hd, m_new, m_out)
    m1_ref[...] = m_out
    hn = jnp.concatenate(h_cols, axis=1) * hg_ref[...]
    y_ref[:, A_WIDTH + B_WIDTH:Y_W] = (hn * gate_o).astype(BF16)


def _sample_mix_call(l, za, zb, zc, kc, vc, cs, c0, n0t, m0tok, lw, nbatch, c1_all=None):
    nb = SAMPLE_NB
    t = SUBLANES
    rows = nb * t
    tok = lambda i: (i, 0)
    const2 = lambda i: (0, 0)
    const3 = lambda i: (0, 0, 0)
    b3 = lambda i: (i, 0, 0)
    lb4 = lambda i: (l, i, 0, 0)
    operands = [lw["sinks"], za, zb, zc, kc, vc, cs, c0, n0t, m0tok, lw["vg"], lw["gwb"],
                lw["gbs_tok"], lw["qg"], lw["kg"], lw["cw"], lw["cb"], lw["fb"], lw["hg"]]
    c_block = (nb, C_HEADS, C_HEAD_DIM, C_HEAD_DIM)
    if c1_all is None:
        kernel_fn, extra_specs, aliases = functools.partial(_sample_mix_kernel, first_layer=l), [], {}
        c1_spec = pl.BlockSpec((DEPTH,) + c_block, lambda i: (0, i, 0, 0, 0))
    else:
        n_in = len(operands)
        operands.append(c1_all)
        extra_specs = [pl.BlockSpec(memory_space=pl.ANY)]
        aliases = {n_in: 5}
        kernel_fn = lambda *refs: _sample_mix_kernel(*refs[:n_in], *refs[n_in + 1:])
        c1_spec = pl.BlockSpec((None,) + c_block, lambda i: (l, i, 0, 0, 0))
    return pl.pallas_call(
        kernel_fn,
        grid=(nbatch // nb,),
        input_output_aliases=aliases,
        in_specs=[
            pl.BlockSpec(memory_space=pltpu.SMEM),
            pl.BlockSpec((rows, ZA_W), tok),
            pl.BlockSpec((rows, ZB_W), tok),
            pl.BlockSpec((rows, ZC_W), tok),
            pl.BlockSpec((None, nb, WINDOW, B_KV_WIDTH), lb4),
            pl.BlockSpec((None, nb, WINDOW, B_KV_WIDTH), lb4),
            pl.BlockSpec((None, nb, C_CONV - 1, 2 * C_WIDTH), lb4),
            pl.BlockSpec((None, nb, C_HEADS, C_HEAD_DIM, C_HEAD_DIM), lambda i: (l, i, 0, 0, 0)),
            pl.BlockSpec((None, C_HEADS, nb, C_HEAD_DIM), lambda i: (l, 0, i, 0)),
            pl.BlockSpec((None, rows, LANES), lambda i: (l, i, 0)),
            pl.BlockSpec((1, A_WIDTH), const2),
            pl.BlockSpec((A_GROUPS, rows, rows), const3),
            pl.BlockSpec((rows, LANES), const2),
            pl.BlockSpec((1, B_WIDTH), const2),
            pl.BlockSpec((1, B_KV_WIDTH), const2),
            pl.BlockSpec((C_CONV, 2 * C_WIDTH), const2),
            pl.BlockSpec((1, 2 * C_WIDTH), const2),
            pl.BlockSpec((1, LANES), const2),
            pl.BlockSpec((1, C_WIDTH), const2),
        ] + extra_specs,
        out_specs=[
            pl.BlockSpec((rows, Y_W), tok),
            pl.BlockSpec((rows, A_WIDTH), tok),
            pl.BlockSpec((nb, WINDOW, B_KV_WIDTH), b3),
            pl.BlockSpec((nb, WINDOW, B_KV_WIDTH), b3),
            pl.BlockSpec((nb, C_CONV - 1, 2 * C_WIDTH), b3),
            c1_spec,
            pl.BlockSpec((C_HEADS, rows, C_HEAD_DIM), lambda i: (0, i, 0)),
            pl.BlockSpec((rows, LANES), tok),
        ],
        out_shape=[
            jax.ShapeDtypeStruct((nbatch * t, Y_W), BF16),
            jax.ShapeDtypeStruct((nbatch * t, A_WIDTH), F32),
            jax.ShapeDtypeStruct((nbatch, WINDOW, B_KV_WIDTH), F32),
            jax.ShapeDtypeStruct((nbatch, WINDOW, B_KV_WIDTH), F32),
            jax.ShapeDtypeStruct((nbatch, C_CONV - 1, 2 * C_WIDTH), F32),
            jax.ShapeDtypeStruct((DEPTH, nbatch, C_HEADS, C_HEAD_DIM, C_HEAD_DIM), F32),
            jax.ShapeDtypeStruct((C_HEADS, nbatch * t, C_HEAD_DIM), F32),
            jax.ShapeDtypeStruct((nbatch * t, LANES), F32),
        ],
        scratch_shapes=[pltpu.VMEM((nb, 2 * SUBLANES, 2 * C_WIDTH), F32)],
        compiler_params=pltpu.CompilerParams(
            dimension_semantics=("arbitrary",), vmem_limit_bytes=VMEM_LIMIT),
        name="sample_mixer",
    )(*operands)


def _layer_weights(l, wcat_all, wmg_all, b_in, gmlp_vnorm_g, gmlp_ws, gmlp_bs, swa_qnorm_g,
                   swa_knorm_g, swa_sinks, mlstm_conv_w, mlstm_conv_b, mlstm_f_bias, mlstm_hnorm_g,
                   w_branch_a, w_branch_b, w_branch_c, w_out, norm_g, dec_seq):
    bl = b_in[l]
    bcat = jnp.concatenate([bl[:COL_CI], bl[COL_CO:COL_MG], bl[COL_CI:COL_CO],
                            jnp.zeros((LANES - 2 * C_HEADS,), F32)])
    t = dec_seq
    nb = SAMPLE_NB
    ws_t = gmlp_ws[l][:, :t, :t] * jnp.tril(jnp.ones((t, t), F32))
    eye = jnp.eye(nb, dtype=F32)
    gwb = jnp.einsum("bc,gts->gbtcs", eye, ws_t).reshape(A_GROUPS, nb * t, nb * t).astype(BF16)
    gbs_col = jnp.pad(gmlp_bs[l].T, ((0, 0), (0, LANES - A_GROUPS)))
    gbs_tok = jnp.pad(jnp.tile(gmlp_bs[l][:, :t].T, (nb, 1)), ((0, 0), (0, LANES - A_GROUPS)))
    fb = jnp.pad(mlstm_f_bias[l], (C_HEADS, LANES - 2 * C_HEADS)).reshape(1, LANES)
    return dict(
        ng=norm_g[l].reshape(1, D_MODEL),
        wcat=wcat_all, bcat=bcat.reshape(1, ZCAT_W),
        wmg=wmg_all, bmg=bl[COL_MG:].reshape(1, 3 * D_MODEL),
        wa=w_branch_a[l].astype(BF16), wb=w_branch_b[l].astype(BF16),
        wc=w_branch_c[l].astype(BF16), wo=w_out[l].astype(BF16),
        vg=gmlp_vnorm_g[l].reshape(1, A_WIDTH), gws=gmlp_ws[l], gwb=gwb,
        gbs_col=gbs_col, gbs_tok=gbs_tok,
        qg=jnp.tile(swa_qnorm_g[l], B_HEADS).reshape(1, B_WIDTH),
        kg=jnp.tile(swa_knorm_g[l], B_KV_HEADS).reshape(1, B_KV_WIDTH),
        sinks=swa_sinks[l],
        cw=mlstm_conv_w[l], cb=mlstm_conv_b[l].reshape(1, 2 * C_WIDTH), fb=fb,
        hg=mlstm_hnorm_g[l].reshape(1, C_WIDTH),
    )


def kernel(x_prompt, x_sample, cache_swa_k, cache_swa_v, state_mlstm_conv, state_mlstm_C, state_mlstm_n, state_mlstm_m, c_prompt, c_sample, ada_w, ada_b, norm_g, w_in, b_in, gmlp_vnorm_g, gmlp_ws, gmlp_bs, swa_qnorm_g, swa_knorm_g, swa_sinks, mlstm_conv_w, mlstm_conv_b, mlstm_f_bias, mlstm_hnorm_g, w_branch_a, w_branch_b, w_branch_c, w_out):
    batch, seq, _ = x_prompt.shape
    nbatch, dec_seq, _ = x_sample.shape
    assert dec_seq == SUBLANES and seq % PROMPT_TILE == 0 and nbatch % SAMPLE_NB == 0
    assert seq % PROJ_TILE == 0 and (nbatch * dec_seq) % PROJ_TILE == 0
    wb_len = cache_swa_k.shape[2]
    assert wb_len == WINDOW

    nc = batch + nbatch
    nc_pad = -(-nc // SUBLANES) * SUBLANES
    c_all = jnp.concatenate([c_prompt, c_sample, jnp.zeros((nc_pad - nc, D_MODEL), F32)], axis=0)
    mod_all = _ada_call(c_all, ada_w, ada_b)

    xp = x_prompt.reshape(batch * seq, D_MODEL)
    xs = x_sample.reshape(nbatch * dec_seq, D_MODEL)
    kc_all = cache_swa_k.reshape(DEPTH, nbatch, WINDOW, B_KV_WIDTH)
    vc_all = cache_swa_v.reshape(DEPTH, nbatch, WINDOW, B_KV_WIDTH)
    n0t_all = jnp.transpose(state_mlstm_n, (0, 2, 1, 3))
    m0tok_all = jnp.pad(jnp.repeat(state_mlstm_m, dec_seq, axis=1),
                        ((0, 0), (0, 0), (0, LANES - C_HEADS)))
    wcat_all, wmg_all = _weight_prep_call(w_in)
    outs_p = [[] for _ in range(6)]
    outs_s = [[] for _ in range(6)]
    vrows = []
    c1_all = None
    for l in range(DEPTH):
        lw = _layer_weights(l, wcat_all, wmg_all, b_in, gmlp_vnorm_g, gmlp_ws, gmlp_bs, swa_qnorm_g,
                            swa_knorm_g, swa_sinks, mlstm_conv_w, mlstm_conv_b, mlstm_f_bias,
                            mlstm_hnorm_g, w_branch_a, w_branch_b, w_branch_c, w_out, norm_g,
                            dec_seq)
        mod_p = mod_all[l, :batch].reshape(batch, 1, 3 * D_MODEL)
        mod_s = jnp.repeat(mod_all[l, batch:nc], dec_seq, axis=0)

        xp, ko, vo, convo, c1, n1, m1 = _prompt_layer_call(l, xp, mod_p, lw, batch, seq)
        outs_p[0].append(ko.reshape(batch, WINDOW, B_KV_HEADS, B_HEAD_DIM))
        outs_p[1].append(vo.reshape(batch, WINDOW, B_KV_HEADS, B_HEAD_DIM))
        outs_p[2].append(convo[:, SUBLANES - (C_CONV - 1):, :])
        outs_p[3].append(c1)
        outs_p[4].append(n1)
        outs_p[5].append(m1[:, 0, :C_HEADS])

        za, zb, zc = _inproj_call(l, xs, mod_s, lw["ng"], lw["wcat"], lw["bcat"], None)
        y, vrow, ko, vo, convo, c1_all, n1tok, m1tok = _sample_mix_call(
            l, za, zb, zc, kc_all, vc_all, state_mlstm_conv, state_mlstm_C, n0t_all, m0tok_all,
            lw, nbatch, c1_all)
        xs = _outproj_call(l, xs, mod_s, lw["ng"], y, lw["wmg"], lw["bmg"], lw["wa"], lw["wb"],
                           lw["wc"], lw["wo"], None)
        outs_s[0].append(ko.reshape(nbatch, WINDOW, B_KV_HEADS, B_HEAD_DIM))
        outs_s[1].append(vo.reshape(nbatch, WINDOW, B_KV_HEADS, B_HEAD_DIM))
        outs_s[2].append(convo)
        outs_s[4].append(jnp.transpose(n1tok[:, ::dec_seq, :], (1, 0, 2)))
        outs_s[5].append(m1tok[::dec_seq, :C_HEADS])
        vrows.append(vrow.reshape(nbatch, dec_seq, A_WIDTH))

    sp = [jnp.stack(o) for o in outs_p]
    ss = [jnp.stack(o) if o else None for o in outs_s]
    return (xp.reshape(batch, seq, D_MODEL), xs.reshape(nbatch, dec_seq, D_MODEL),
            sp[0], sp[1], sp[2], sp[3], sp[4], sp[5],
            ss[0], ss[1], ss[2], c1_all, ss[4], ss[5], jnp.stack(vrows))
```

```python
import functools

import numpy as np
import jax
import jax.numpy as jnp
from jax import lax
from jax.experimental import pallas as pl
from jax.experimental.pallas import tpu as pltpu

F32 = jnp.float32
BF16 = jnp.bfloat16

D_MODEL = 1024
DEPTH = 2
A_WIDTH = 512
A_GROUPS = 4
GROUP_DIM = 128
B_HEADS = 8
B_KV_HEADS = 2
B_HEAD_DIM = 64
B_WIDTH = 512
B_KV_WIDTH = 128
WINDOW = 128
C_HEADS = 4
C_HEAD_DIM = 128
C_WIDTH = 512
C_CONV = 4
EPS = 1e-6
NEG = -1e30

LANES = 128
SUBLANES = 8
VMEM_LIMIT = 56 * 1024 * 1024

ZA_W = 3 * A_WIDTH
ZB_W = 2 * B_WIDTH + 2 * B_KV_WIDTH
ZC_W = 2 * C_WIDTH + 3 * C_WIDTH + LANES
ZCAT_W = ZA_W + ZB_W + ZC_W
Y_W = A_WIDTH + B_WIDTH + C_WIDTH

PROMPT_TILE = 256
MLSTM_CHUNK = PROMPT_TILE
SAMPLE_NB = 16
PROJ_TILE = 512


def _sigmoid(x):
    return 0.5 * jnp.tanh(0.5 * x) + 0.5


def _silu(x):
    t = 0.5 * x
    return t * (jnp.tanh(t) + 1.0)


def _log_sigmoid(x):
    return jnp.minimum(x, 0.0) - jnp.log1p(jnp.exp(-jnp.abs(x)))


def _rms(x):
    return x * lax.rsqrt(jnp.mean(x * x, axis=-1, keepdims=True) + EPS)


def _dot(a, b):
    return jnp.dot(a, b, preferred_element_type=F32)


def _dot_nt(a, b):
    return lax.dot_general(a, b, (((1,), (1,)), ((), ())), preferred_element_type=F32)


def _dot_exact01(m01, x):
    hi = x.astype(BF16)
    r1 = x - hi.astype(F32)
    mid = r1.astype(BF16)
    lo = (r1 - mid.astype(F32)).astype(BF16)
    return _dot(m01, hi) + _dot(m01, mid) + _dot(m01, lo)


def _modulated_norm(x, mod_ref, ng_ref):
    xn = _rms(x) * ng_ref[...]
    shift = mod_ref[:, 0:D_MODEL]
    scale = mod_ref[:, D_MODEL:2 * D_MODEL]
    return (xn * (1.0 + scale) + shift).astype(BF16)


def _head_rms_scale(x2, lane_lo):
    s0 = jnp.sum(jnp.where(lane_lo, x2, 0.0), axis=-1, keepdims=True)
    s1 = jnp.sum(jnp.where(lane_lo, 0.0, x2), axis=-1, keepdims=True)
    r0 = lax.rsqrt(s0 * (1.0 / B_HEAD_DIM) + EPS)
    r1 = lax.rsqrt(s1 * (1.0 / B_HEAD_DIM) + EPS)
    return jnp.where(lane_lo, r0, r1)


def _qk_norm(x, g_row):
    rows, width = x.shape
    lane_lo = lax.broadcasted_iota(jnp.int32, (rows, LANES), 1) < B_HEAD_DIM
    outs = []
    for j in range(width // LANES):
        slab = x[:, j * LANES:(j + 1) * LANES]
        outs.append(slab * _head_rms_scale(slab * slab, lane_lo))
    y = outs[0] if len(outs) == 1 else jnp.concatenate(outs, axis=1)
    return y * g_row


def _ada_kernel(c_ref, w_ref, b_ref, o_ref):
    c = c_ref[...]
    o_ref[...] = _dot(_silu(c).astype(BF16), w_ref[...].astype(BF16)) + b_ref[...]


def _ada_call(c_all, ada_w, ada_b):
    rows = c_all.shape[0]
    return pl.pallas_call(
        _ada_kernel,
        grid=(DEPTH, 3),
        in_specs=[
            pl.BlockSpec((rows, D_MODEL), lambda l, j: (0, 0)),
            pl.BlockSpec((None, D_MODEL, D_MODEL), lambda l, j: (l, 0, j)),
            pl.BlockSpec((None, 1, D_MODEL), lambda l, j: (l, 0, j)),
        ],
        out_specs=pl.BlockSpec((None, rows, D_MODEL), lambda l, j: (l, 0, j)),
        out_shape=jax.ShapeDtypeStruct((DEPTH, rows, 3 * D_MODEL), F32),
        compiler_params=pltpu.CompilerParams(
            dimension_semantics=("arbitrary", "arbitrary"), vmem_limit_bytes=VMEM_LIMIT),
        name="adaln_mod",
    )(c_all, ada_w, ada_b.reshape(DEPTH, 1, 3 * D_MODEL))


COL_CI = ZA_W + ZB_W + 3 * C_WIDTH
COL_CO = COL_CI + 2 * C_HEADS
COL_MG = COL_CO + 2 * C_WIDTH
PREP_CHUNK = 256
PREP_SHIFT = 2 * C_HEADS
N_MAIN = COL_CI // PREP_CHUNK
N_CO = (2 * C_WIDTH) // PREP_CHUNK
N_MG = (3 * D_MODEL) // PREP_CHUNK
J_CIF = N_MAIN + N_CO
J_MG = J_CIF + 1


def _weight_prep_kernel(wa_ref, wb_ref, wcat_ref, wmg_ref):
    j = pl.program_id(1)

    def shifted_t():
        rows = jnp.concatenate([wa_ref[PREP_SHIFT:PREP_CHUNK, :], wb_ref[...]], axis=0)
        return rows.astype(BF16).T

    @pl.when(j < N_MAIN)
    def _():
        wcat_ref[...] = wa_ref[...].astype(BF16).T

    @pl.when((j >= N_MAIN) & (j < J_CIF))
    def _():
        wcat_ref[...] = shifted_t()

    @pl.when(j == J_CIF)
    def _():
        row = lax.broadcasted_iota(jnp.int32, (PREP_CHUNK, D_MODEL), 0)
        wcat_ref[...] = jnp.where(row < PREP_SHIFT, wa_ref[...], 0.0).astype(BF16).T

    @pl.when(j >= J_MG)
    def _():
        wmg_ref[...] = shifted_t()


def _weight_prep_call(w_in):
    in_width = w_in.shape[-1]
    assert in_width == COL_MG + 3 * D_MODEL
    assert COL_CI % PREP_CHUNK == 0 and COL_CO % PREP_CHUNK == PREP_SHIFT == COL_MG % PREP_CHUNK
    w_t = jnp.swapaxes(w_in, 1, 2)
    assert in_width % PREP_SHIFT == 0 and PREP_SHIFT == SUBLANES
    last_rows = in_width // PREP_SHIFT - 1
    groups_per_chunk = PREP_CHUNK // PREP_SHIFT

    def src_block(j):
        return jnp.where(j < J_CIF, j, jnp.where(j == J_CIF, N_MAIN, j - 1))

    return pl.pallas_call(
        _weight_prep_kernel,
        grid=(DEPTH, J_MG + N_MG),
        in_specs=[
            pl.BlockSpec((None, PREP_CHUNK, D_MODEL), lambda l, j: (l, src_block(j), 0)),
            pl.BlockSpec((None, PREP_SHIFT, D_MODEL),
                         lambda l, j: (l, jnp.minimum((src_block(j) + 1) * groups_per_chunk,
                                                      last_rows), 0)),
        ],
        out_specs=[
            pl.BlockSpec((None, D_MODEL, PREP_CHUNK), lambda l, j: (l, 0, jnp.minimum(j, J_CIF))),
            pl.BlockSpec((None, D_MODEL, PREP_CHUNK), lambda l, j: (l, 0, jnp.maximum(j - J_MG, 0))),
        ],
        out_shape=[
            jax.ShapeDtypeStruct((DEPTH, D_MODEL, ZCAT_W), BF16),
            jax.ShapeDtypeStruct((DEPTH, D_MODEL, 3 * D_MODEL), BF16),
        ],
        compiler_params=pltpu.CompilerParams(
            dimension_semantics=("arbitrary", "arbitrary"), vmem_limit_bytes=VMEM_LIMIT),
        name="weight_prep",
    )(w_t, w_t)


def _col_chunks(width, step):
    return [(o, min(step, width - o)) for o in range(0, width, step)]


def _inproj_pieces(get_h, w_ref, b_ref, za_ref, zb_ref, zc_ref, step):
    def piece(o_ref, off, woff, w):
        def run():
            o_ref[:, off:off + w] = _dot(get_h(), w_ref[:, woff:woff + w]) + b_ref[:, woff:woff + w]
        return run
    pieces = []
    base = 0
    for o_ref, width in ((za_ref, ZA_W), (zb_ref, ZB_W), (zc_ref, ZC_W)):
        pieces += [piece(o_ref, off, base + off, w) for off, w in _col_chunks(width, step)]
        base += width
    return pieces


def _inproj_kernel(x_ref, mod_ref, ng_ref, w_ref, b_ref, za_ref, zb_ref, zc_ref):
    h = _modulated_norm(x_ref[...], mod_ref, ng_ref)
    for piece in _inproj_pieces(lambda: h, w_ref, b_ref, za_ref, zb_ref, zc_ref, 512):
        piece()


def _mod_spec(tm, tokens_per_batch):
    if tokens_per_batch is None:
        return pl.BlockSpec((tm, 3 * D_MODEL), lambda i: (i, 0))
    tiles_per_batch = tokens_per_batch // tm
    return pl.BlockSpec((None, 1, 3 * D_MODEL), lambda i: (i // tiles_per_batch, 0, 0))


def _layer_weight_spec(layer, rows, cols):
    return pl.BlockSpec((None, rows, cols), lambda i: (layer, 0, 0), pipeline_mode=pl.Buffered(1))


def _inproj_call(layer, x2, mod, ng, wcat, bcat, tokens_per_batch):
    ntok = x2.shape[0]
    tm = PROJ_TILE
    const = lambda i: (0, 0)
    return pl.pallas_call(
        _inproj_kernel,
        grid=(ntok // tm,),
        in_specs=[
            pl.BlockSpec((tm, D_MODEL), lambda i: (i, 0)),
            _mod_spec(tm, tokens_per_batch),
            pl.BlockSpec((1, D_MODEL), const),
            _layer_weight_spec(layer, D_MODEL, ZCAT_W),
            pl.BlockSpec((1, ZCAT_W), const),
        ],
        out_specs=[
            pl.BlockSpec((tm, ZA_W), lambda i: (i, 0)),
            pl.BlockSpec((tm, ZB_W), lambda i: (i, 0)),
            pl.BlockSpec((tm, ZC_W), lambda i: (i, 0)),
        ],
        out_shape=[
            jax.ShapeDtypeStruct((ntok, ZA_W), F32),
            jax.ShapeDtypeStruct((ntok, ZB_W), F32),
            jax.ShapeDtypeStruct((ntok, ZC_W), F32),
        ],
        compiler_params=pltpu.CompilerParams(
            dimension_semantics=("arbitrary",), vmem_limit_bytes=VMEM_LIMIT),
        name="in_projection",
    )(x2, mod, ng, wcat, bcat)


def _outproj_kernel(x_ref, mod_ref, ng_ref, y_ref, wmg_ref, bmg_ref, wa_ref, wb_ref, wc_ref,
                    wo_ref, o_ref):
    x = x_ref[...]
    h = _modulated_norm(x, mod_ref, ng_ref)
    merged = None
    for i, wbr_ref in enumerate((wa_ref, wb_ref, wc_ref)):
        cols = slice(i * D_MODEL, (i + 1) * D_MODEL)
        gate = _sigmoid(_dot(h, wmg_ref[:, cols]) + bmg_ref[:, cols])
        term = gate * _dot(y_ref[:, i * A_WIDTH:(i + 1) * A_WIDTH], wbr_ref[...])
        merged = term if merged is None else merged + term
    ada_gate = mod_ref[:, 2 * D_MODEL:3 * D_MODEL]
    o_ref[...] = x + ada_gate * _dot(merged.astype(BF16), wo_ref[...])


def _outproj_call(layer, x2, mod, ng, y, wmg, bmg, wa, wb, wc, wo, tokens_per_batch):
    ntok = x2.shape[0]
    tm = PROJ_TILE
    const = lambda i: (0, 0)
    once = pl.Buffered(1)
    return pl.pallas_call(
        _outproj_kernel,
        grid=(ntok // tm,),
        in_specs=[
            pl.BlockSpec((tm, D_MODEL), lambda i: (i, 0)),
            _mod_spec(tm, tokens_per_batch),
            pl.BlockSpec((1, D_MODEL), const),
            pl.BlockSpec((tm, Y_W), lambda i: (i, 0)),
            _layer_weight_spec(layer, D_MODEL, 3 * D_MODEL),
            pl.BlockSpec((1, 3 * D_MODEL), const),
            pl.BlockSpec((A_WIDTH, D_MODEL), const, pipeline_mode=once),
            pl.BlockSpec((B_WIDTH, D_MODEL), const, pipeline_mode=once),
            pl.BlockSpec((C_WIDTH, D_MODEL), const, pipeline_mode=once),
            pl.BlockSpec((D_MODEL, D_MODEL), const, pipeline_mode=once),
        ],
        out_specs=pl.BlockSpec((tm, D_MODEL), lambda i: (i, 0)),
        out_shape=jax.ShapeDtypeStruct((ntok, D_MODEL), F32),
        compiler_params=pltpu.CompilerParams(
            dimension_semantics=("arbitrary",), vmem_limit_bytes=VMEM_LIMIT),
        name="out_projection",
    )(x2, mod, ng, y, wmg, bmg, wa, wb, wc, wo)


def _place_q_head(qn, h, rows):
    lane = lax.broadcasted_iota(jnp.int32, (rows, LANES), 1)
    slab = qn[:, (h // 2) * LANES:(h // 2 + 1) * LANES]
    src_hi = h % 2
    dst_hi = h // (B_HEADS // B_KV_HEADS)
    keep = (lane >= B_HEAD_DIM) if src_hi else (lane < B_HEAD_DIM)
    slab = jnp.where(keep, slab, 0.0)
    if src_hi != dst_hi:
        slab = pltpu.roll(slab, B_HEAD_DIM, 1)
    return slab


def _merge_head_pair(o_even, o_odd, h_even, rows):
    lane_lo = lax.broadcasted_iota(jnp.int32, (rows, LANES), 1) < B_HEAD_DIM
    kv_hi = h_even // (B_HEADS // B_KV_HEADS)
    if kv_hi:
        o_even = pltpu.roll(o_even, B_HEAD_DIM, 1)
    else:
        o_odd = pltpu.roll(o_odd, B_HEAD_DIM, 1)
    return jnp.where(lane_lo, o_even, o_odd)


def _conv_taps(xbuf, cw_ref, cb_ref, cols, ts):
    y = cb_ref[:, cols]
    for j in range(C_CONV):
        lo = SUBLANES - (C_CONV - 1) + j
        y = y + cw_ref[j:j + 1, cols] * xbuf[lo:lo + ts, cols]
    return y


def _prompt_mix_kernel(sink_ref, za_ref, zb_ref, zc_ref, vg_ref, gw_ref, gbs_ref, qg_ref, kg_ref,
                       cw_ref, cb_ref, fb_ref, hg_ref, tril_ref, band_ref, tri01_ref, tribias_ref,
                       y_ref, ko_ref, vo_ref, convo_ref, c_ref, n_ref, m_ref,
                       kprev, vprev, xbuf, first_tile, pump):
    ts = PROMPT_TILE

    wts = [(gw_ref[gi] * tril_ref[...]).astype(BF16) for gi in range(A_GROUPS)]
    for c in range(ts // WINDOW):
        rows = slice(c * WINDOW, (c + 1) * WINDOW)
        vnb = (_rms(za_ref[rows, A_WIDTH:2 * A_WIDTH]) * vg_ref[...]).astype(BF16)
        s = jnp.concatenate(
            [_dot(wts[gi], vnb[:, gi * GROUP_DIM:(gi + 1) * GROUP_DIM]) + gbs_ref[:, gi:gi + 1]
             for gi in range(A_GROUPS)], axis=1)
        sg = _silu(za_ref[rows, 2 * A_WIDTH:3 * A_WIDTH])
        y_ref[rows, 0:A_WIDTH] = (za_ref[rows, 0:A_WIDTH] * s * sg).astype(BF16)
        pump()

    kn = _qk_norm(zb_ref[:, B_WIDTH:B_WIDTH + B_KV_WIDTH], kg_ref[...])
    vv = zb_ref[:, B_WIDTH + B_KV_WIDTH:B_WIDTH + 2 * B_KV_WIDTH]
    pump()
    grp = B_HEADS // B_KV_HEADS
    nblk = ts // WINDOW
    lane_lo2 = lax.broadcasted_iota(jnp.int32, (2 * WINDOW, LANES), 1) < B_HEAD_DIM
    kblocks = [kprev[...]] + [kn[b * WINDOW:(b + 1) * WINDOW] for b in range(nblk)]
    vblocks = [vprev[...]] + [vv[b * WINDOW:(b + 1) * WINDOW] for b in range(nblk)]
    heads = [(kh, g) for kh in range(B_KV_HEADS) for g in range(grp)]
    snk = {k: sink_ref[k[0] * grp + k[1]] for k in heads}
    for blk in range(nblk):
        rows = slice(blk * WINDOW, (blk + 1) * WINDOW)
        if blk == 0 and first_tile is not False:
            bias = jnp.where(first_tile, band_ref[1], band_ref[0])
        else:
            bias = band_ref[0]
        kcat = jnp.concatenate([kblocks[blk], kblocks[blk + 1]], axis=0)
        vcat = jnp.concatenate([vblocks[blk], vblocks[blk + 1]], axis=0)
        krol = pltpu.roll(kcat, B_HEAD_DIM, 1)
        vrol = pltpu.roll(vcat, B_HEAD_DIM, 1)
        kdup, vdup = [], []
        for kh in range(B_KV_HEADS):
            own = lane_lo2 if kh == 0 else jnp.logical_not(lane_lo2)
            kdup.append(jnp.where(own, kcat, krol).astype(BF16))
            vdup.append(jnp.where(own, vcat, vrol).astype(BF16))
        qn = _qk_norm(zb_ref[rows, 0:B_WIDTH], qg_ref[...]) * (B_HEAD_DIM ** -0.5)
        pump()
        qs = [jnp.concatenate([_place_q_head(qn, kh * grp + g, WINDOW) for g in range(grp)],
                              axis=0).astype(BF16) for kh in range(B_KV_HEADS)]
        logits = [_dot_nt(qs[kh], kdup[kh]) for kh in range(B_KV_HEADS)]
        pump()
        lg = {(kh, g): logits[kh][g * WINDOW:(g + 1) * WINDOW] + bias for kh, g in heads}
        mx = {k: jnp.maximum(jnp.max(lg[k], axis=-1, keepdims=True), snk[k]) for k in heads}
        pump()
        p = {k: jnp.exp(lg[k] - mx[k]) for k in heads}
        pump()
        rden = {k: 1.0 / (jnp.sum(p[k], axis=-1, keepdims=True) + jnp.exp(snk[k] - mx[k]))
                for k in heads}
        pump()
        pv = [_dot(jnp.concatenate([p[kh, g].astype(BF16) for g in range(grp)], axis=0), vdup[kh])
              for kh in range(B_KV_HEADS)]
        pump()
        outs = {(kh, g): pv[kh][g * WINDOW:(g + 1) * WINDOW] * rden[kh, g] for kh, g in heads}
        yb = jnp.concatenate(
            [_merge_head_pair(outs[(2 * j) // grp, (2 * j) % grp],
                              outs[(2 * j + 1) // grp, (2 * j + 1) % grp], 2 * j, WINDOW)
             for j in range(B_HEADS // 2)], axis=1)
        sgb = _silu(zb_ref[rows, B_WIDTH + 2 * B_KV_WIDTH:ZB_W])
        y_ref[rows, A_WIDTH:A_WIDTH + B_WIDTH] = (yb * sgb).astype(BF16)
        pump()
    kprev[...] = kblocks[nblk]
    vprev[...] = vblocks[nblk]
    ko_ref[...] = kblocks[nblk]
    vo_ref[...] = vblocks[nblk]
    pump()

    xbuf[SUBLANES:SUBLANES + ts, :] = zc_ref[:, 0:2 * C_WIDTH]
    pump()
    ifp = zc_ref[:, 5 * C_WIDTH:5 * C_WIDTH + LANES]
    lf = _log_sigmoid(ifp + fb_ref[...])
    pump()
    cl = MLSTM_CHUNK
    lane_c = lax.broadcasted_iota(jnp.int32, (cl, LANES), 1)
    lane_1 = lax.broadcasted_iota(jnp.int32, (1, LANES), 1)
    m_row = m_ref[...]
    m_out = m_row
    cum_all = _dot_exact01(tri01_ref[...], lf)
    st_col = jnp.where(lane_c < C_HEADS, ifp, cum_all)
    st_row = st_col.T
    tribias = tribias_ref[...]
    pump()
    for hds in MLSTM_HEAD_GROUPS:
        hs = {hd: slice(hd * C_HEAD_DIM, (hd + 1) * C_HEAD_DIM) for hd in hds}
        i_c = {hd: st_col[:, hd:hd + 1] for hd in hds}
        cum_c = {hd: st_col[:, C_HEADS + hd:C_HEADS + hd + 1] for hd in hds}
        i_r = {hd: st_row[hd:hd + 1, :] for hd in hds}
        cum_r = {hd: st_row[C_HEADS + hd:C_HEADS + hd + 1, :] for hd in hds}
        m_prev = {hd: m_row[:, hd:hd + 1] for hd in hds}
        dmat = {hd: cum_c[hd] - cum_r[hd] + i_r[hd] + tribias for hd in hds}
        m_inter = {hd: cum_c[hd] + m_prev[hd] for hd in hds}
        m_t = {hd: jnp.maximum(m_inter[hd], jnp.max(dmat[hd], axis=-1, keepdims=True)) for hd in hds}
        pump()
        q_h = {hd: _silu(_conv_taps(xbuf, cw_ref, cb_ref, hs[hd], ts)).astype(BF16) for hd in hds}
        k_h = {hd: _silu(_conv_taps(xbuf, cw_ref, cb_ref,
                                    slice(C_WIDTH + hs[hd].start, C_WIDTH + hs[hd].stop), ts))
               * (C_HEAD_DIM ** -0.5) for hd in hds}
        pump()
        v_h = {hd: zc_ref[:, 2 * C_WIDTH + hd * C_HEAD_DIM:2 * C_WIDTH + (hd + 1) * C_HEAD_DIM].astype(BF16)
               for hd in hds}
        s_qk = {hd: _dot_nt(q_h[hd], k_h[hd].astype(BF16)) for hd in hds}
        a = {hd: jnp.exp(dmat[hd] - m_t[hd]) * s_qk[hd] for hd in hds}
        pump()
        w_inter = {hd: jnp.exp(m_inter[hd] - m_t[hd]) for hd in hds}
        c_prev = {hd: c_ref[hd] for hd in hds}
        n_prev = {hd: n_ref[hd:hd + 1, :] for hd in hds}
        inter = {hd: _dot(q_h[hd], c_prev[hd].astype(BF16)) for hd in hds}
        intra = {hd: _dot(a[hd].astype(BF16), v_h[hd]) for hd in hds}
        pump()
        den = {hd: jnp.sum(a[hd], axis=-1, keepdims=True)
               + w_inter[hd] * jnp.sum(q_h[hd].astype(F32) * n_prev[hd], axis=-1, keepdims=True)
               for hd in hds}
        rnorm = {hd: 1.0 / jnp.maximum(jnp.abs(den[hd]), jnp.exp(-m_t[hd])) for hd in hds}
        hh = {hd: (intra[hd] + w_inter[hd] * inter[hd]) * rnorm[hd] for hd in hds}
        pump()
        for hd in hds:
            o_cols = slice(3 * C_WIDTH + hd * C_HEAD_DIM, 3 * C_WIDTH + (hd + 1) * C_HEAD_DIM)
            g_cols = slice(4 * C_WIDTH + hd * C_HEAD_DIM, 4 * C_WIDTH + (hd + 1) * C_HEAD_DIM)
            gate_o = _sigmoid(zc_ref[:, o_cols]) * _silu(zc_ref[:, g_cols])
            y_cols = slice(A_WIDTH + B_WIDTH + hd * C_HEAD_DIM, A_WIDTH + B_WIDTH + (hd + 1) * C_HEAD_DIM)
            y_ref[:, y_cols] = (_rms(hh[hd]) * hg_ref[:, hs[hd]] * gate_o).astype(BF16)
        pump()
        total = {hd: cum_r[hd][:, cl - 1:cl] for hd in hds}
        g_r = {hd: total[hd] - cum_r[hd] + i_r[hd] for hd in hds}
        g_c = {hd: total[hd] - cum_c[hd] + i_c[hd] for hd in hds}
        m_new = {hd: jnp.maximum(total[hd] + m_prev[hd], jnp.max(g_r[hd], axis=-1, keepdims=True))
                 for hd in hds}
        kw = {hd: jnp.exp(g_c[hd] - m_new[hd]) * k_h[hd] for hd in hds}
        decay = {hd: jnp.exp(total[hd] + m_prev[hd] - m_new[hd]) for hd in hds}
        pump()
        upd = {hd: _dot(kw[hd].T.astype(BF16), v_h[hd]) for hd in hds}
        for hd in hds:
            c_ref[hd] = decay[hd] * c_prev[hd] + upd[hd]
            n_ref[hd:hd + 1, :] = decay[hd] * n_prev[hd] + jnp.sum(kw[hd], axis=0, keepdims=True)
            m_out = jnp.where(lane_1 == hd, m_new[hd], m_out)
        pump()
    m_ref[...] = m_out
    tail = xbuf[ts:ts + SUBLANES, :]
    xbuf[0:SUBLANES, :] = tail
    convo_ref[...] = tail


def _prompt_mask_constants():
    r = np.arange(WINDOW)[:, None]
    c = np.arange(2 * WINDOW)[None, :]
    band = (c > r) & (c <= r + WINDOW)
    band_first = band & (c >= WINDOW)
    band_bias = np.where(np.stack([band, band_first]), 0.0, NEG).astype(np.float32)
    tril = (np.arange(WINDOW)[:, None] >= np.arange(WINDOW)[None, :]).astype(np.float32)
    tri = np.arange(MLSTM_CHUNK)[:, None] >= np.arange(MLSTM_CHUNK)[None, :]
    return (jnp.asarray(tril), jnp.asarray(band_bias), jnp.asarray(tri, dtype=BF16),
            jnp.asarray(np.where(tri, 0.0, NEG).astype(np.float32)))


N_MIX_PARAMS = 13
MLSTM_HEAD_GROUPS = ((0, 1, 2, 3),)
MIX_PUMP_CALLS = 29
TAIL_FILL_PIECES = 8
MXU_PIECE_COLS = 256


class _Interleaver:
    def __init__(self, pieces, calls, hold_back=0):
        self._pieces = list(pieces)
        self._hold_back = hold_back
        self._spread = len(self._pieces) - hold_back
        self._emitted = 0
        self._calls = calls
        self._call = 0

    def __call__(self):
        self._call += 1
        target = (self._call * self._spread) // self._calls
        while self._emitted < target:
            self._pieces.pop(0)()
            self._emitted += 1

    def finish(self):
        assert self._call == self._calls and len(self._pieces) == self._hold_back, self._call
        return self._pieces


def _gate_pieces(h_ref, wmg_ref, bmg_ref, g_ref):
    def piece(off):
        cols = slice(off, off + MXU_PIECE_COLS)
        def run():
            g_ref[:, cols] = _sigmoid(_dot(h_ref[...], wmg_ref[:, cols]) + bmg_ref[:, cols])
        return run
    return [piece(off) for off in range(0, 3 * D_MODEL, MXU_PIECE_COLS)]


def _merge_and_project(x, mod_ref, g_ref, y_ref, wa_ref, wb_ref, wc_ref, wo_ref, fillers=()):
    fillers = list(fillers)
    per_stage = -(-len(fillers) // 4)
    merged = None
    for i, wbr_ref in enumerate((wa_ref, wb_ref, wc_ref)):
        for piece in fillers[i * per_stage:(i + 1) * per_stage]:
            piece()
        term = (g_ref[:, i * D_MODEL:(i + 1) * D_MODEL]
                * _dot(y_ref[:, i * A_WIDTH:(i + 1) * A_WIDTH], wbr_ref[...]))
        merged = term if merged is None else merged + term
    for piece in fillers[3 * per_stage:]:
        piece()
    ada_gate = mod_ref[:, 2 * D_MODEL:3 * D_MODEL]
    return x + ada_gate * _dot(merged.astype(BF16), wo_ref[...])


def _prompt_layer_kernel(tiles_per_seq, sink_ref, x2_ref, xn_ref, mod_ref, modn_ref, ng_ref,
                         wcat_ref, bcat_ref, *rest):
    mix_params = rest[:N_MIX_PARAMS]
    wmg_ref, bmg_ref, wa_ref, wb_ref, wc_ref, wo_ref = rest[N_MIX_PARAMS:N_MIX_PARAMS + 6]
    o_ref, ko_ref, vo_ref, convo_ref, c_ref, n_ref, m_ref = rest[N_MIX_PARAMS + 6:N_MIX_PARAMS + 13]
    (za0, zb0, zc0, za1, zb1, zc1, h0, h1, y_scr, g_scr, kprev, vprev, xbuf) = rest[N_MIX_PARAMS + 13:]
    ts = PROMPT_TILE
    z = ((za0, zb0, zc0), (za1, zb1, zc1))
    h = (h0, h1)
    k = pl.program_id(0)
    seq_start = (k % (tiles_per_seq // 2)) == 0

    @pl.when(k == 0)
    def _():
        h0[...] = _modulated_norm(x2_ref[0:ts, :], mod_ref, ng_ref)
        for piece in _inproj_pieces(lambda: h0[...], wcat_ref, bcat_ref, *z[0], 512):
            piece()

    @pl.when(seq_start)
    def _():
        kprev[...] = jnp.zeros_like(kprev)
        vprev[...] = jnp.zeros_like(vprev)
        xbuf[0:SUBLANES, :] = jnp.zeros((SUBLANES, 2 * C_WIDTH), F32)
        c_ref[...] = jnp.zeros_like(c_ref)
        n_ref[...] = jnp.zeros_like(n_ref)
        m_ref[...] = jnp.zeros_like(m_ref)

    for half in range(2):
        cur, nxt = half, 1 - half
        rows = slice(half * ts, (half + 1) * ts)
        if half == 0:
            h[nxt][...] = _modulated_norm(x2_ref[ts:2 * ts, :], mod_ref, ng_ref)
        else:
            h[nxt][...] = _modulated_norm(xn_ref[...], modn_ref, ng_ref)
        get_h_next = functools.partial(lambda r: r[...], h[nxt])
        hold = TAIL_FILL_PIECES if half == 1 else 0
        proj = _inproj_pieces(get_h_next, wcat_ref, bcat_ref, *z[nxt], MXU_PIECE_COLS)
        pump = _Interleaver(
            proj[:len(proj) - hold] + _gate_pieces(h[cur], wmg_ref, bmg_ref, g_scr)
            + proj[len(proj) - hold:], MIX_PUMP_CALLS, hold_back=hold)
        _prompt_mix_kernel(sink_ref, *z[cur], *mix_params,
                           y_scr, ko_ref, vo_ref, convo_ref, c_ref, n_ref, m_ref, kprev, vprev, xbuf,
                           first_tile=seq_start if half == 0 else False, pump=pump)
        o_ref[rows, :] = _merge_and_project(x2_ref[rows, :], mod_ref, g_scr, y_scr,
                                            wa_ref, wb_ref, wc_ref, wo_ref, fillers=pump.finish())


def _prompt_layer_call(layer, x2, mod, lw, batch, seq):
    ts = PROMPT_TILE
    nt = seq // ts
    assert nt % 2 == 0
    last_tile = batch * nt - 1
    const2 = lambda k: (0, 0)
    const3 = lambda k: (0, 0, 0)
    per_b3 = lambda k: ((2 * k) // nt, 0, 0)
    next_tile = lambda k: jnp.minimum(2 * k + 2, last_tile)
    once = pl.Buffered(1)
    return pl.pallas_call(
        functools.partial(_prompt_layer_kernel, nt),
        grid=(batch * nt // 2,),
        in_specs=[
            pl.BlockSpec(memory_space=pltpu.SMEM),
            pl.BlockSpec((2 * ts, D_MODEL), lambda k: (k, 0)),
            pl.BlockSpec((ts, D_MODEL), lambda k: (next_tile(k), 0)),
            pl.BlockSpec((None, 1, 3 * D_MODEL), per_b3),
            pl.BlockSpec((None, 1, 3 * D_MODEL), lambda k: (next_tile(k) // nt, 0, 0)),
            pl.BlockSpec((1, D_MODEL), const2),
            _layer_weight_spec(layer, D_MODEL, ZCAT_W),
            pl.BlockSpec((1, ZCAT_W), const2),
            pl.BlockSpec((1, A_WIDTH), const2),
            pl.BlockSpec((A_GROUPS, WINDOW, WINDOW), const3),
            pl.BlockSpec((WINDOW, LANES), const2),
            pl.BlockSpec((1, B_WIDTH), const2),
            pl.BlockSpec((1, B_KV_WIDTH), const2),
            pl.BlockSpec((C_CONV, 2 * C_WIDTH), const2),
            pl.BlockSpec((1, 2 * C_WIDTH), const2),
            pl.BlockSpec((1, LANES), const2),
            pl.BlockSpec((1, C_WIDTH), const2),
            pl.BlockSpec((WINDOW, WINDOW), const2),
            pl.BlockSpec((2, WINDOW, 2 * WINDOW), const3),
            pl.BlockSpec((MLSTM_CHUNK, MLSTM_CHUNK), const2),
            pl.BlockSpec((MLSTM_CHUNK, MLSTM_CHUNK), const2),
            _layer_weight_spec(layer, D_MODEL, 3 * D_MODEL),
            pl.BlockSpec((1, 3 * D_MODEL), const2),
            pl.BlockSpec((A_WIDTH, D_MODEL), const2, pipeline_mode=once),
            pl.BlockSpec((B_WIDTH, D_MODEL), const2, pipeline_mode=once),
            pl.BlockSpec((C_WIDTH, D_MODEL), const2, pipeline_mode=once),
            pl.BlockSpec((D_MODEL, D_MODEL), const2, pipeline_mode=once),
        ],
        out_specs=[
            pl.BlockSpec((2 * ts, D_MODEL), lambda k: (k, 0)),
            pl.BlockSpec((None, WINDOW, B_KV_WIDTH), per_b3),
            pl.BlockSpec((None, WINDOW, B_KV_WIDTH), per_b3),
            pl.BlockSpec((None, SUBLANES, 2 * C_WIDTH), per_b3),
            pl.BlockSpec((None, C_HEADS, C_HEAD_DIM, C_HEAD_DIM), lambda k: ((2 * k) // nt, 0, 0, 0)),
            pl.BlockSpec((None, C_HEADS, C_HEAD_DIM), per_b3),
            pl.BlockSpec((None, 1, LANES), per_b3),
        ],
        out_shape=[
            jax.ShapeDtypeStruct((batch * seq, D_MODEL), F32),
            jax.ShapeDtypeStruct((batch, WINDOW, B_KV_WIDTH), F32),
            jax.ShapeDtypeStruct((batch, WINDOW, B_KV_WIDTH), F32),
            jax.ShapeDtypeStruct((batch, SUBLANES, 2 * C_WIDTH), F32),
            jax.ShapeDtypeStruct((batch, C_HEADS, C_HEAD_DIM, C_HEAD_DIM), F32),
            jax.ShapeDtypeStruct((batch, C_HEADS, C_HEAD_DIM), F32),
            jax.ShapeDtypeStruct((batch, 1, LANES), F32),
        ],
        scratch_shapes=(
            [pltpu.VMEM((ts, w), F32) for w in (ZA_W, ZB_W, ZC_W)] * 2
            + [pltpu.VMEM((ts, D_MODEL), BF16)] * 2
            + [pltpu.VMEM((ts, Y_W), BF16),
               pltpu.VMEM((ts, 3 * D_MODEL), F32),
               pltpu.VMEM((WINDOW, B_KV_WIDTH), F32),
               pltpu.VMEM((WINDOW, B_KV_WIDTH), F32),
               pltpu.VMEM((ts + SUBLANES, 2 * C_WIDTH), F32)]),
        compiler_params=pltpu.CompilerParams(
            dimension_semantics=("arbitrary",), vmem_limit_bytes=VMEM_LIMIT),
        name="prompt_layer",
    )(lw["sinks"], x2, x2, mod, mod, lw["ng"], lw["wcat"], lw["bcat"],
      lw["vg"], lw["gws"], lw["gbs_col"], lw["qg"], lw["kg"], lw["cw"], lw["cb"], lw["fb"], lw["hg"],
      *_prompt_mask_constants(),
      lw["wmg"], lw["bmg"], lw["wa"], lw["wb"], lw["wc"], lw["wo"])


def _sample_mix_kernel(sink_ref, za_ref, zb_ref, zc_ref, kc_ref, vc_ref, cs_ref, c0_ref, n0_ref,
                       m0_ref, vg_ref, gwb_ref, gbs_ref, qg_ref, kg_ref, cw_ref, cb_ref, fb_ref,
                       hg_ref,
                       y_ref, vrow_ref, ko_ref, vo_ref, convo_ref, c1_ref, n1_ref, m1_ref,
                       xbuf, first_layer=None):
    nb = SAMPLE_NB
    t = SUBLANES
    rows = nb * t
    if first_layer is not None:
        for other in range(DEPTH):
            if other != first_layer:
                c1_ref[other] = jnp.zeros(c1_ref.shape[1:], F32)
        c1_ref = c1_ref.at[first_layer]
    tok_r = lax.broadcasted_iota(jnp.int32, (rows, rows), 0)
    tok_c = lax.broadcasted_iota(jnp.int32, (rows, rows), 1)
    same_b = (tok_r // t) == (tok_c // t)
    causal_b = same_b & (tok_c <= tok_r)

    u = za_ref[:, 0:A_WIDTH]
    vn = _rms(za_ref[:, A_WIDTH:2 * A_WIDTH]) * vg_ref[...]
    sg = _silu(za_ref[:, 2 * A_WIDTH:3 * A_WIDTH])
    vrow_ref[...] = vn
    vnb = vn.astype(BF16)
    s_cols = []
    for gi in range(A_GROUPS):
        s_cols.append(_dot(gwb_ref[gi], vnb[:, gi * GROUP_DIM:(gi + 1) * GROUP_DIM])
                      + gbs_ref[:, gi:gi + 1])
    y_ref[:, 0:A_WIDTH] = (u * jnp.concatenate(s_cols, axis=1) * sg).astype(BF16)

    qn = _qk_norm(zb_ref[:, 0:B_WIDTH], qg_ref[...]) * (B_HEAD_DIM ** -0.5)
    kn = _qk_norm(zb_ref[:, B_WIDTH:B_WIDTH + B_KV_WIDTH], kg_ref[...])
    vv = zb_ref[:, B_WIDTH + B_KV_WIDTH:B_WIDTH + 2 * B_KV_WIDTH]
    sgb = _silu(zb_ref[:, B_WIDTH + 2 * B_KV_WIDTH:ZB_W])
    kn3 = kn.reshape(nb, t, B_KV_WIDTH)
    vv3 = vv.reshape(nb, t, B_KV_WIDTH)
    kcache = kc_ref[...]
    vcache = vc_ref[...]
    pad = jnp.zeros((nb, WINDOW - t, B_KV_WIDTH), F32)
    kall = jnp.concatenate([kcache, kn3, pad], axis=1).astype(BF16)
    vall = jnp.concatenate([vcache, vv3, pad], axis=1).astype(BF16)
    qp = jnp.concatenate([_place_q_head(qn, h, rows).reshape(nb, t, LANES) for h in range(B_HEADS)],
                         axis=1).astype(BF16)
    logits = lax.dot_general(qp, kall, (((2,), (2,)), ((0,), (0,))), preferred_element_type=F32)
    qrow = lax.broadcasted_iota(jnp.int32, (nb, B_HEADS * t, 2 * WINDOW), 1)
    kcol = lax.broadcasted_iota(jnp.int32, (nb, B_HEADS * t, 2 * WINDOW), 2)
    qt = qrow % t
    valid = ((kcol < WINDOW) & (kcol > qt)) | ((kcol >= WINDOW) & ((kcol - WINDOW) <= qt))
    hrow = lax.broadcasted_iota(jnp.int32, (B_HEADS * t, 1), 0) // t
    snk = jnp.zeros((B_HEADS * t, 1), F32)
    for h in range(B_HEADS):
        snk = jnp.where(hrow == h, sink_ref[h], snk)
    lg = jnp.where(valid, logits, NEG)
    mx = jnp.maximum(jnp.max(lg, axis=-1, keepdims=True), snk[None])
    p = jnp.exp(lg - mx)
    den = jnp.sum(p, axis=-1, keepdims=True) + jnp.exp(snk[None] - mx)
    pv = lax.dot_general(p.astype(BF16), vall, (((2,), (1,)), ((0,), (0,))),
                         preferred_element_type=F32) / den
    head_out = [pv[:, h * t:(h + 1) * t, :].reshape(rows, LANES) for h in range(B_HEADS)]
    yb = jnp.concatenate(
        [_merge_head_pair(head_out[2 * j], head_out[2 * j + 1], 2 * j, rows)
         for j in range(B_HEADS // 2)], axis=1)
    y_ref[:, A_WIDTH:A_WIDTH + B_WIDTH] = (yb * sgb).astype(BF16)
    ko_ref[...] = jnp.concatenate([kcache[:, t:, :], kn3], axis=1)
    vo_ref[...] = jnp.concatenate([vcache[:, t:, :], vv3], axis=1)

    xbuf[:, SUBLANES - (C_CONV - 1):SUBLANES, :] = cs_ref[...]
    xbuf[:, SUBLANES:2 * SUBLANES, :] = zc_ref[:, 0:2 * C_WIDTH].reshape(nb, t, 2 * C_WIDTH)
    y3 = cb_ref[...][None]
    for j in range(C_CONV):
        lo = SUBLANES - (C_CONV - 1) + j
        y3 = y3 + cw_ref[j:j + 1, :][None] * xbuf[:, lo:lo + t, :]
    convo_ref[...] = xbuf[:, 2 * SUBLANES - (C_CONV - 1):2 * SUBLANES, :]
    qk = _silu(y3.reshape(rows, 2 * C_WIDTH))
    qall = qk[:, 0:C_WIDTH].astype(BF16)
    kall_c = qk[:, C_WIDTH:2 * C_WIDTH] * (C_HEAD_DIM ** -0.5)
    vall_c = zc_ref[:, 2 * C_WIDTH:3 * C_WIDTH].astype(BF16)
    gate_o = _sigmoid(zc_ref[:, 3 * C_WIDTH:4 * C_WIDTH]) * _silu(zc_ref[:, 4 * C_WIDTH:5 * C_WIDTH])
    ifp = zc_ref[:, 5 * C_WIDTH:5 * C_WIDTH + LANES]
    lf = _log_sigmoid(ifp + fb_ref[...])
    lane_t = lax.broadcasted_iota(jnp.int32, (rows, LANES), 1)
    cum_all = _dot_exact01(jnp.where(causal_b, 1.0, 0.0).astype(BF16), lf)
    tot_all = _dot_exact01(jnp.where(same_b, 1.0, 0.0).astype(BF16), lf)
    st_col = jnp.where(lane_t < C_HEADS, ifp, cum_all)
    st_row = st_col.T
    tot_row = tot_all.T
    m0 = m0_ref[...]
    same_b_bf = jnp.where(same_b, 1.0, 0.0).astype(BF16)
    batch_of_lane = lax.broadcasted_iota(jnp.int32, (nb, 1, rows), 2) // t
    batch_id = lax.broadcasted_iota(jnp.int32, (nb, 1, rows), 0)
    own_tok = batch_of_lane == batch_id
    h_cols = []
    m_out = jnp.zeros((rows, LANES), F32)
    for hd in range(C_HEADS):
        hs = slice(hd * C_HEAD_DIM, (hd + 1) * C_HEAD_DIM)
        i_c = st_col[:, hd:hd + 1]
        cum_c = st_col[:, C_HEADS + hd:C_HEADS + hd + 1]
        tot_c = tot_all[:, C_HEADS + hd:C_HEADS + hd + 1]
        i_r = st_row[hd:hd + 1, :]
        cum_r = st_row[C_HEADS + hd:C_HEADS + hd + 1, :]
        tot_r = tot_row[C_HEADS + hd:C_HEADS + hd + 1, :]
        m_prev = m0[:, hd:hd + 1]
        dmat = jnp.where(causal_b, cum_c - cum_r + i_r, NEG)
        m_inter = cum_c + m_prev
        m_t = jnp.maximum(m_inter, jnp.max(dmat, axis=-1, keepdims=True))
        q_h = qall[:, hs]
        k_h = kall_c[:, hs]
        v_h = vall_c[:, hs]
        a = jnp.exp(dmat - m_t) * _dot_nt(q_h, k_h.astype(BF16))
        w_inter = jnp.exp(m_inter - m_t)
        c_prev = c0_ref[:, hd]
        n_tok = jnp.broadcast_to(n0_ref[hd][:, None, :], (nb, t, C_HEAD_DIM)).reshape(rows, C_HEAD_DIM)
        inter = lax.dot_general(q_h.reshape(nb, t, C_HEAD_DIM), c_prev.astype(BF16),
                                (((2,), (1,)), ((0,), (0,))), preferred_element_type=F32)
        num = _dot(a.astype(BF16), v_h) + w_inter * inter.reshape(rows, C_HEAD_DIM)
        den = (jnp.sum(a, axis=-1, keepdims=True)
               + w_inter * jnp.sum(q_h.astype(F32) * n_tok, axis=-1, keepdims=True))
        hh = num / jnp.maximum(jnp.abs(den), jnp.exp(-m_t))
        h_cols.append(_rms(hh))
        g_r = tot_r - cum_r + i_r
        g_c = tot_c - cum_c + i_c
        m_new = jnp.maximum(tot_c + m_prev,
                            jnp.max(jnp.where(same_b, g_r, NEG), axis=-1, keepdims=True))
        kw = jnp.exp(g_c - m_new) * k_h
        decay = jnp.exp(tot_c + m_prev - m_new)
        kwt = kw.T
        lhs = jnp.where(own_tok, kwt[None], 0.0).astype(BF16).reshape(nb * C_HEAD_DIM, rows)
        upd = _dot(lhs, v_h).reshape(nb, C_HEAD_DIM, C_HEAD_DIM)
        dec_b = jnp.broadcast_to(decay, (rows, C_HEAD_DIM)).reshape(nb, t, C_HEAD_DIM)[:, 0:1, :]
        c1_ref[:, hd] = dec_b * c_prev + upd
        n1_ref[hd] = decay * n_tok + _dot(same_b_bf, kw.astype(BF16))
        m_out = jnp.where(lane_t == hd, m_new, m_out)
    m1_ref[...] = m_out
    hn = jnp.concatenate(h_cols, axis=1) * hg_ref[...]
    y_ref[:, A_WIDTH + B_WIDTH:Y_W] = (hn * gate_o).astype(BF16)


def _sample_mix_call(l, za, zb, zc, kc, vc, cs, c0, n0t, m0tok, lw, nbatch, c1_all=None):
    nb = SAMPLE_NB
    t = SUBLANES
    rows = nb * t
    tok = lambda i: (i, 0)
    const2 = lambda i: (0, 0)
    const3 = lambda i: (0, 0, 0)
    b3 = lambda i: (i, 0, 0)
    lb4 = lambda i: (l, i, 0, 0)
    operands = [lw["sinks"], za, zb, zc, kc, vc, cs, c0, n0t, m0tok, lw["vg"], lw["gwb"],
                lw["gbs_tok"], lw["qg"], lw["kg"], lw["cw"], lw["cb"], lw["fb"], lw["hg"]]
    c_block = (nb, C_HEADS, C_HEAD_DIM, C_HEAD_DIM)
    if c1_all is None:
        kernel_fn, extra_specs, aliases = functools.partial(_sample_mix_kernel, first_layer=l), [], {}
        c1_spec = pl.BlockSpec((DEPTH,) + c_block, lambda i: (0, i, 0, 0, 0))
    else:
        n_in = len(operands)
        operands.append(c1_all)
        extra_specs = [pl.BlockSpec(memory_space=pl.ANY)]
        aliases = {n_in: 5}
        kernel_fn = lambda *refs: _sample_mix_kernel(*refs[:n_in], *refs[n_in + 1:])
        c1_spec = pl.BlockSpec((None,) + c_block, lambda i: (l, i, 0, 0, 0))
    return pl.pallas_call(
        kernel_fn,
        grid=(nbatch // nb,),
        input_output_aliases=aliases,
        in_specs=[
            pl.BlockSpec(memory_space=pltpu.SMEM),
            pl.BlockSpec((rows, ZA_W), tok),
            pl.BlockSpec((rows, ZB_W), tok),
            pl.BlockSpec((rows, ZC_W), tok),
            pl.BlockSpec((None, nb, WINDOW, B_KV_WIDTH), lb4),
            pl.BlockSpec((None, nb, WINDOW, B_KV_WIDTH), lb4),
            pl.BlockSpec((None, nb, C_CONV - 1, 2 * C_WIDTH), lb4),
            pl.BlockSpec((None, nb, C_HEADS, C_HEAD_DIM, C_HEAD_DIM), lambda i: (l, i, 0, 0, 0)),
            pl.BlockSpec((None, C_HEADS, nb, C_HEAD_DIM), lambda i: (l, 0, i, 0)),
            pl.BlockSpec((None, rows, LANES), lambda i: (l, i, 0)),
            pl.BlockSpec((1, A_WIDTH), const2),
            pl.BlockSpec((A_GROUPS, rows, rows), const3),
            pl.BlockSpec((rows, LANES), const2),
            pl.BlockSpec((1, B_WIDTH), const2),
            pl.BlockSpec((1, B_KV_WIDTH), const2),
            pl.BlockSpec((C_CONV, 2 * C_WIDTH), const2),
            pl.BlockSpec((1, 2 * C_WIDTH), const2),
            pl.BlockSpec((1, LANES), const2),
            pl.BlockSpec((1, C_WIDTH), const2),
        ] + extra_specs,
        out_specs=[
            pl.BlockSpec((rows, Y_W), tok),
            pl.BlockSpec((rows, A_WIDTH), tok),
            pl.BlockSpec((nb, WINDOW, B_KV_WIDTH), b3),
            pl.BlockSpec((nb, WINDOW, B_KV_WIDTH), b3),
            pl.BlockSpec((nb, C_CONV - 1, 2 * C_WIDTH), b3),
            c1_spec,
            pl.BlockSpec((C_HEADS, rows, C_HEAD_DIM), lambda i: (0, i, 0)),
            pl.BlockSpec((rows, LANES), tok),
        ],
        out_shape=[
            jax.ShapeDtypeStruct((nbatch * t, Y_W), BF16),
            jax.ShapeDtypeStruct((nbatch * t, A_WIDTH), F32),
            jax.ShapeDtypeStruct((nbatch, WINDOW, B_KV_WIDTH), F32),
            jax.ShapeDtypeStruct((nbatch, WINDOW, B_KV_WIDTH), F32),
            jax.ShapeDtypeStruct((nbatch, C_CONV - 1, 2 * C_WIDTH), F32),
            jax.ShapeDtypeStruct((DEPTH, nbatch, C_HEADS, C_HEAD_DIM, C_HEAD_DIM), F32),
            jax.ShapeDtypeStruct((C_HEADS, nbatch * t, C_HEAD_DIM), F32),
            jax.ShapeDtypeStruct((nbatch * t, LANES), F32),
        ],
        scratch_shapes=[pltpu.VMEM((nb, 2 * SUBLANES, 2 * C_WIDTH), F32)],
        compiler_params=pltpu.CompilerParams(
            dimension_semantics=("arbitrary",), vmem_limit_bytes=VMEM_LIMIT),
        name="sample_mixer",
    )(*operands)


def _layer_weights(l, wcat_all, wmg_all, b_in, gmlp_vnorm_g, gmlp_ws, gmlp_bs, swa_qnorm_g,
                   swa_knorm_g, swa_sinks, mlstm_conv_w, mlstm_conv_b, mlstm_f_bias, mlstm_hnorm_g,
                   w_branch_a, w_branch_b, w_branch_c, w_out, norm_g, dec_seq):
    bl = b_in[l]
    bcat = jnp.concatenate([bl[:COL_CI], bl[COL_CO:COL_MG], bl[COL_CI:COL_CO],
                            jnp.zeros((LANES - 2 * C_HEADS,), F32)])
    t = dec_seq
    nb = SAMPLE_NB
    ws_t = gmlp_ws[l][:, :t, :t] * jnp.tril(jnp.ones((t, t), F32))
    eye = jnp.eye(nb, dtype=F32)
    gwb = jnp.einsum("bc,gts->gbtcs", eye, ws_t).reshape(A_GROUPS, nb * t, nb * t).astype(BF16)
    gbs_col = jnp.pad(gmlp_bs[l].T, ((0, 0), (0, LANES - A_GROUPS)))
    gbs_tok = jnp.pad(jnp.tile(gmlp_bs[l][:, :t].T, (nb, 1)), ((0, 0), (0, LANES - A_GROUPS)))
    fb = jnp.pad(mlstm_f_bias[l], (C_HEADS, LANES - 2 * C_HEADS)).reshape(1, LANES)
    return dict(
        ng=norm_g[l].reshape(1, D_MODEL),
        wcat=wcat_all, bcat=bcat.reshape(1, ZCAT_W),
        wmg=wmg_all, bmg=bl[COL_MG:].reshape(1, 3 * D_MODEL),
        wa=w_branch_a[l].astype(BF16), wb=w_branch_b[l].astype(BF16),
        wc=w_branch_c[l].astype(BF16), wo=w_out[l].astype(BF16),
        vg=gmlp_vnorm_g[l].reshape(1, A_WIDTH), gws=gmlp_ws[l], gwb=gwb,
        gbs_col=gbs_col, gbs_tok=gbs_tok,
        qg=jnp.tile(swa_qnorm_g[l], B_HEADS).reshape(1, B_WIDTH),
        kg=jnp.tile(swa_knorm_g[l], B_KV_HEADS).reshape(1, B_KV_WIDTH),
        sinks=swa_sinks[l],
        cw=mlstm_conv_w[l], cb=mlstm_conv_b[l].reshape(1, 2 * C_WIDTH), fb=fb,
        hg=mlstm_hnorm_g[l].reshape(1, C_WIDTH),
    )


def kernel(x_prompt, x_sample, cache_swa_k, cache_swa_v, state_mlstm_conv, state_mlstm_C, state_mlstm_n, state_mlstm_m, c_prompt, c_sample, ada_w, ada_b, norm_g, w_in, b_in, gmlp_vnorm_g, gmlp_ws, gmlp_bs, swa_qnorm_g, swa_knorm_g, swa_sinks, mlstm_conv_w, mlstm_conv_b, mlstm_f_bias, mlstm_hnorm_g, w_branch_a, w_branch_b, w_branch_c, w_out):
    batch, seq, _ = x_prompt.shape
    nbatch, dec_seq, _ = x_sample.shape
    assert dec_seq == SUBLANES and seq % PROMPT_TILE == 0 and nbatch % SAMPLE_NB == 0
    assert seq % PROJ_TILE == 0 and (nbatch * dec_seq) % PROJ_TILE == 0
    wb_len = cache_swa_k.shape[2]
    assert wb_len == WINDOW

    nc = batch + nbatch
    nc_pad = -(-nc // SUBLANES) * SUBLANES
    c_all = jnp.concatenate([c_prompt, c_sample, jnp.zeros((nc_pad - nc, D_MODEL), F32)], axis=0)
    mod_all = _ada_call(c_all, ada_w, ada_b)

    xp = x_prompt.reshape(batch * seq, D_MODEL)
    xs = x_sample.reshape(nbatch * dec_seq, D_MODEL)
    kc_all = cache_swa_k.reshape(DEPTH, nbatch, WINDOW, B_KV_WIDTH)
    vc_all = cache_swa_v.reshape(DEPTH, nbatch, WINDOW, B_KV_WIDTH)
    n0t_all = jnp.transpose(state_mlstm_n, (0, 2, 1, 3))
    m0tok_all = jnp.pad(jnp.repeat(state_mlstm_m, dec_seq, axis=1),
                        ((0, 0), (0, 0), (0, LANES - C_HEADS)))
    wcat_all, wmg_all = _weight_prep_call(w_in)
    outs_p = [[] for _ in range(6)]
    outs_s = [[] for _ in range(6)]
    vrows = []
    c1_all = None
    for l in range(DEPTH):
        lw = _layer_weights(l, wcat_all, wmg_all, b_in, gmlp_vnorm_g, gmlp_ws, gmlp_bs, swa_qnorm_g,
                            swa_knorm_g, swa_sinks, mlstm_conv_w, mlstm_conv_b, mlstm_f_bias,
                            mlstm_hnorm_g, w_branch_a, w_branch_b, w_branch_c, w_out, norm_g,
                            dec_seq)
        mod_p = mod_all[l, :batch].reshape(batch, 1, 3 * D_MODEL)
        mod_s = jnp.repeat(mod_all[l, batch:nc], dec_seq, axis=0)

        xp, ko, vo, convo, c1, n1, m1 = _prompt_layer_call(l, xp, mod_p, lw, batch, seq)
        outs_p[0].append(ko.reshape(batch, WINDOW, B_KV_HEADS, B_HEAD_DIM))
        outs_p[1].append(vo.reshape(batch, WINDOW, B_KV_HEADS, B_HEAD_DIM))
        outs_p[2].append(convo[:, SUBLANES - (C_CONV - 1):, :])
        outs_p[3].append(c1)
        outs_p[4].append(n1)
        outs_p[5].append(m1[:, 0, :C_HEADS])

        za, zb, zc = _inproj_call(l, xs, mod_s, lw["ng"], lw["wcat"], lw["bcat"], None)
        y, vrow, ko, vo, convo, c1_all, n1tok, m1tok = _sample_mix_call(
            l, za, zb, zc, kc_all, vc_all, state_mlstm_conv, state_mlstm_C, n0t_all, m0tok_all,
            lw, nbatch, c1_all)
        xs = _outproj_call(l, xs, mod_s, lw["ng"], y, lw["wmg"], lw["bmg"], lw["wa"], lw["wb"],
                           lw["wc"], lw["wo"], None)
        outs_s[0].append(ko.reshape(nbatch, WINDOW, B_KV_HEADS, B_HEAD_DIM))
        outs_s[1].append(vo.reshape(nbatch, WINDOW, B_KV_HEADS, B_HEAD_DIM))
        outs_s[2].append(convo)
        outs_s[4].append(jnp.transpose(n1tok[:, ::dec_seq, :], (1, 0, 2)))
        outs_s[5].append(m1tok[::dec_seq, :C_HEADS])
        vrows.append(vrow.reshape(nbatch, dec_seq, A_WIDTH))

    sp = [jnp.stack(o) for o in outs_p]
    ss = [jnp.stack(o) if o else None for o in outs_s]
    return (xp.reshape(batch, seq, D_MODEL), xs.reshape(nbatch, dec_seq, D_MODEL),
            sp[0], sp[1], sp[2], sp[3], sp[4], sp[5],
            ss[0], ss[1], ss[2], c1_all, ss[4], ss[5], jnp.stack(vrows))
```

```python
import functools

import numpy as np
import jax
import jax.numpy as jnp
from jax import lax
from jax.experimental import pallas as pl
from jax.experimental.pallas import tpu as pltpu

F32 = jnp.float32
BF16 = jnp.bfloat16

D_MODEL = 1024
DEPTH = 2
A_WIDTH = 512
A_GROUPS = 4
GROUP_DIM = 128
B_HEADS = 8
B_KV_HEADS = 2
B_HEAD_DIM = 64
B_WIDTH = 512
B_KV_WIDTH = 128
WINDOW = 128
C_HEADS = 4
C_HEAD_DIM = 128
C_WIDTH = 512
C_CONV = 4
EPS = 1e-6
NEG = -1e30

LANES = 128
SUBLANES = 8
VMEM_LIMIT = 56 * 1024 * 1024

ZA_W = 3 * A_WIDTH
ZB_W = 2 * B_WIDTH + 2 * B_KV_WIDTH
ZC_W = 2 * C_WIDTH + 3 * C_WIDTH + LANES
ZCAT_W = ZA_W + ZB_W + ZC_W
Y_W = A_WIDTH + B_WIDTH + C_WIDTH

PROMPT_TILE = 256
MLSTM_CHUNK = PROMPT_TILE
SAMPLE_NB = 16
PROJ_TILE = 256


def _sigmoid(x):
    return 0.5 * jnp.tanh(0.5 * x) + 0.5


def _silu(x):
    t = 0.5 * x
    return t * (jnp.tanh(t) + 1.0)


def _log_sigmoid(x):
    return jnp.minimum(x, 0.0) - jnp.log1p(jnp.exp(-jnp.abs(x)))


def _rms(x):
    return x * lax.rsqrt(jnp.mean(x * x, axis=-1, keepdims=True) + EPS)


def _dot(a, b):
    return jnp.dot(a, b, preferred_element_type=F32)


def _dot_nt(a, b):
    return lax.dot_general(a, b, (((1,), (1,)), ((), ())), preferred_element_type=F32)


def _dot_exact01(m01, x):
    hi = x.astype(BF16)
    r1 = x - hi.astype(F32)
    mid = r1.astype(BF16)
    lo = (r1 - mid.astype(F32)).astype(BF16)
    return _dot(m01, hi) + _dot(m01, mid) + _dot(m01, lo)


def _modulated_norm(x, mod_ref, ng_ref):
    xn = _rms(x) * ng_ref[...]
    shift = mod_ref[:, 0:D_MODEL]
    scale = mod_ref[:, D_MODEL:2 * D_MODEL]
    return (xn * (1.0 + scale) + shift).astype(BF16)


def _head_rms_scale(x2, lane_lo):
    s0 = jnp.sum(jnp.where(lane_lo, x2, 0.0), axis=-1, keepdims=True)
    s1 = jnp.sum(jnp.where(lane_lo, 0.0, x2), axis=-1, keepdims=True)
    r0 = lax.rsqrt(s0 * (1.0 / B_HEAD_DIM) + EPS)
    r1 = lax.rsqrt(s1 * (1.0 / B_HEAD_DIM) + EPS)
    return jnp.where(lane_lo, r0, r1)


def _qk_norm(x, g_row):
    rows, width = x.shape
    lane_lo = lax.broadcasted_iota(jnp.int32, (rows, LANES), 1) < B_HEAD_DIM
    outs = []
    for j in range(width // LANES):
        slab = x[:, j * LANES:(j + 1) * LANES]
        outs.append(slab * _head_rms_scale(slab * slab, lane_lo))
    y = outs[0] if len(outs) == 1 else jnp.concatenate(outs, axis=1)
    return y * g_row


def _ada_kernel(c_ref, w_ref, b_ref, o_ref):
    c = c_ref[...]
    o_ref[...] = _dot(_silu(c).astype(BF16), w_ref[...].astype(BF16)) + b_ref[...]


def _ada_call(c_all, ada_w, ada_b):
    rows = c_all.shape[0]
    return pl.pallas_call(
        _ada_kernel,
        grid=(DEPTH, 3),
        in_specs=[
            pl.BlockSpec((rows, D_MODEL), lambda l, j: (0, 0)),
            pl.BlockSpec((None, D_MODEL, D_MODEL), lambda l, j: (l, 0, j)),
            pl.BlockSpec((None, 1, D_MODEL), lambda l, j: (l, 0, j)),
        ],
        out_specs=pl.BlockSpec((None, rows, D_MODEL), lambda l, j: (l, 0, j)),
        out_shape=jax.ShapeDtypeStruct((DEPTH, rows, 3 * D_MODEL), F32),
        compiler_params=pltpu.CompilerParams(
            dimension_semantics=("arbitrary", "arbitrary"), vmem_limit_bytes=VMEM_LIMIT),
        name="adaln_mod",
    )(c_all, ada_w, ada_b.reshape(DEPTH, 1, 3 * D_MODEL))


COL_CI = ZA_W + ZB_W + 3 * C_WIDTH
COL_CO = COL_CI + 2 * C_HEADS
COL_MG = COL_CO + 2 * C_WIDTH
PREP_CHUNK = 256
PREP_SHIFT = 2 * C_HEADS
N_MAIN = COL_CI // PREP_CHUNK
N_CO = (2 * C_WIDTH) // PREP_CHUNK
N_MG = (3 * D_MODEL) // PREP_CHUNK
J_CIF = N_MAIN + N_CO
J_MG = J_CIF + 1


def _weight_prep_kernel(wa_ref, wb_ref, wcat_ref, wmg_ref):
    j = pl.program_id(1)

    def shifted_t():
        rows = jnp.concatenate([wa_ref[PREP_SHIFT:PREP_CHUNK, :], wb_ref[...]], axis=0)
        return rows.astype(BF16).T

    @pl.when(j < N_MAIN)
    def _():
        wcat_ref[...] = wa_ref[...].astype(BF16).T

    @pl.when((j >= N_MAIN) & (j < J_CIF))
    def _():
        wcat_ref[...] = shifted_t()

    @pl.when(j == J_CIF)
    def _():
        row = lax.broadcasted_iota(jnp.int32, (PREP_CHUNK, D_MODEL), 0)
        wcat_ref[...] = jnp.where(row < PREP_SHIFT, wa_ref[...], 0.0).astype(BF16).T

    @pl.when(j >= J_MG)
    def _():
        wmg_ref[...] = shifted_t()


def _weight_prep_call(w_in):
    in_width = w_in.shape[-1]
    assert in_width == COL_MG + 3 * D_MODEL
    assert COL_CI % PREP_CHUNK == 0 and COL_CO % PREP_CHUNK == PREP_SHIFT == COL_MG % PREP_CHUNK
    w_t = jnp.swapaxes(w_in, 1, 2)
    assert in_width % PREP_SHIFT == 0 and PREP_SHIFT == SUBLANES
    last_rows = in_width // PREP_SHIFT - 1
    groups_per_chunk = PREP_CHUNK // PREP_SHIFT

    def src_block(j):
        return jnp.where(j < J_CIF, j, jnp.where(j == J_CIF, N_MAIN, j - 1))

    return pl.pallas_call(
        _weight_prep_kernel,
        grid=(DEPTH, J_MG + N_MG),
        in_specs=[
            pl.BlockSpec((None, PREP_CHUNK, D_MODEL), lambda l, j: (l, src_block(j), 0)),
            pl.BlockSpec((None, PREP_SHIFT, D_MODEL),
                         lambda l, j: (l, jnp.minimum((src_block(j) + 1) * groups_per_chunk,
                                                      last_rows), 0)),
        ],
        out_specs=[
            pl.BlockSpec((None, D_MODEL, PREP_CHUNK), lambda l, j: (l, 0, jnp.minimum(j, J_CIF))),
            pl.BlockSpec((None, D_MODEL, PREP_CHUNK), lambda l, j: (l, 0, jnp.maximum(j - J_MG, 0))),
        ],
        out_shape=[
            jax.ShapeDtypeStruct((DEPTH, D_MODEL, ZCAT_W), BF16),
            jax.ShapeDtypeStruct((DEPTH, D_MODEL, 3 * D_MODEL), BF16),
        ],
        compiler_params=pltpu.CompilerParams(
            dimension_semantics=("arbitrary", "arbitrary"), vmem_limit_bytes=VMEM_LIMIT),
        name="weight_prep",
    )(w_t, w_t)


def _col_chunks(width, step):
    return [(o, min(step, width - o)) for o in range(0, width, step)]


def _inproj_pieces(get_h, w_ref, b_ref, za_ref, zb_ref, zc_ref, step):
    def piece(o_ref, off, woff, w):
        def run():
            o_ref[:, off:off + w] = _dot(get_h(), w_ref[:, woff:woff + w]) + b_ref[:, woff:woff + w]
        return run
    pieces = []
    base = 0
    for o_ref, width in ((za_ref, ZA_W), (zb_ref, ZB_W), (zc_ref, ZC_W)):
        pieces += [piece(o_ref, off, base + off, w) for off, w in _col_chunks(width, step)]
        base += width
    return pieces


def _inproj_kernel(x_ref, mod_ref, ng_ref, w_ref, b_ref, za_ref, zb_ref, zc_ref):
    h = _modulated_norm(x_ref[...], mod_ref, ng_ref)
    for piece in _inproj_pieces(lambda: h, w_ref, b_ref, za_ref, zb_ref, zc_ref, 512):
        piece()


def _mod_spec(tm, tokens_per_batch):
    if tokens_per_batch is None:
        return pl.BlockSpec((tm, 3 * D_MODEL), lambda i: (i, 0))
    tiles_per_batch = tokens_per_batch // tm
    return pl.BlockSpec((None, 1, 3 * D_MODEL), lambda i: (i // tiles_per_batch, 0, 0))


def _layer_weight_spec(layer, rows, cols):
    return pl.BlockSpec((None, rows, cols), lambda i: (layer, 0, 0), pipeline_mode=pl.Buffered(1))


def _inproj_call(layer, x2, mod, ng, wcat, bcat, tokens_per_batch):
    ntok = x2.shape[0]
    tm = PROJ_TILE
    const = lambda i: (0, 0)
    return pl.pallas_call(
        _inproj_kernel,
        grid=(ntok // tm,),
        in_specs=[
            pl.BlockSpec((tm, D_MODEL), lambda i: (i, 0)),
            _mod_spec(tm, tokens_per_batch),
            pl.BlockSpec((1, D_MODEL), const),
            _layer_weight_spec(layer, D_MODEL, ZCAT_W),
            pl.BlockSpec((1, ZCAT_W), const),
        ],
        out_specs=[
            pl.BlockSpec((tm, ZA_W), lambda i: (i, 0)),
            pl.BlockSpec((tm, ZB_W), lambda i: (i, 0)),
            pl.BlockSpec((tm, ZC_W), lambda i: (i, 0)),
        ],
        out_shape=[
            jax.ShapeDtypeStruct((ntok, ZA_W), F32),
            jax.ShapeDtypeStruct((ntok, ZB_W), F32),
            jax.ShapeDtypeStruct((ntok, ZC_W), F32),
        ],
        compiler_params=pltpu.CompilerParams(
            dimension_semantics=("arbitrary",), vmem_limit_bytes=VMEM_LIMIT),
        name="in_projection",
    )(x2, mod, ng, wcat, bcat)


def _outproj_kernel(x_ref, mod_ref, ng_ref, y_ref, wmg_ref, bmg_ref, wa_ref, wb_ref, wc_ref,
                    wo_ref, o_ref):
    x = x_ref[...]
    h = _modulated_norm(x, mod_ref, ng_ref)
    merged = None
    for i, wbr_ref in enumerate((wa_ref, wb_ref, wc_ref)):
        cols = slice(i * D_MODEL, (i + 1) * D_MODEL)
        gate = _sigmoid(_dot(h, wmg_ref[:, cols]) + bmg_ref[:, cols])
        term = gate * _dot(y_ref[:, i * A_WIDTH:(i + 1) * A_WIDTH], wbr_ref[...])
        merged = term if merged is None else merged + term
    ada_gate = mod_ref[:, 2 * D_MODEL:3 * D_MODEL]
    o_ref[...] = x + ada_gate * _dot(merged.astype(BF16), wo_ref[...])


def _outproj_call(layer, x2, mod, ng, y, wmg, bmg, wa, wb, wc, wo, tokens_per_batch):
    ntok = x2.shape[0]
    tm = PROJ_TILE
    const = lambda i: (0, 0)
    once = pl.Buffered(1)
    return pl.pallas_call(
        _outproj_kernel,
        grid=(ntok // tm,),
        in_specs=[
            pl.BlockSpec((tm, D_MODEL), lambda i: (i, 0)),
            _mod_spec(tm, tokens_per_batch),
            pl.BlockSpec((1, D_MODEL), const),
            pl.BlockSpec((tm, Y_W), lambda i: (i, 0)),
            _layer_weight_spec(layer, D_MODEL, 3 * D_MODEL),
            pl.BlockSpec((1, 3 * D_MODEL), const),
            pl.BlockSpec((A_WIDTH, D_MODEL), const, pipeline_mode=once),
            pl.BlockSpec((B_WIDTH, D_MODEL), const, pipeline_mode=once),
            pl.BlockSpec((C_WIDTH, D_MODEL), const, pipeline_mode=once),
            pl.BlockSpec((D_MODEL, D_MODEL), const, pipeline_mode=once),
        ],
        out_specs=pl.BlockSpec((tm, D_MODEL), lambda i: (i, 0)),
        out_shape=jax.ShapeDtypeStruct((ntok, D_MODEL), F32),
        compiler_params=pltpu.CompilerParams(
            dimension_semantics=("arbitrary",), vmem_limit_bytes=VMEM_LIMIT),
        name="out_projection",
    )(x2, mod, ng, y, wmg, bmg, wa, wb, wc, wo)


def _place_q_head(qn, h, rows):
    lane = lax.broadcasted_iota(jnp.int32, (rows, LANES), 1)
    slab = qn[:, (h // 2) * LANES:(h // 2 + 1) * LANES]
    src_hi = h % 2
    dst_hi = h // (B_HEADS // B_KV_HEADS)
    keep = (lane >= B_HEAD_DIM) if src_hi else (lane < B_HEAD_DIM)
    slab = jnp.where(keep, slab, 0.0)
    if src_hi != dst_hi:
        slab = pltpu.roll(slab, B_HEAD_DIM, 1)
    return slab


def _merge_head_pair(o_even, o_odd, h_even, rows):
    lane_lo = lax.broadcasted_iota(jnp.int32, (rows, LANES), 1) < B_HEAD_DIM
    kv_hi = h_even // (B_HEADS // B_KV_HEADS)
    if kv_hi:
        o_even = pltpu.roll(o_even, B_HEAD_DIM, 1)
    else:
        o_odd = pltpu.roll(o_odd, B_HEAD_DIM, 1)
    return jnp.where(lane_lo, o_even, o_odd)


def _conv_taps(xbuf, cw_ref, cb_ref, cols, ts):
    y = cb_ref[:, cols]
    for j in range(C_CONV):
        lo = SUBLANES - (C_CONV - 1) + j
        y = y + cw_ref[j:j + 1, cols] * xbuf[lo:lo + ts, cols]
    return y


def _prompt_mix_kernel(sink_ref, za_ref, zb_ref, zc_ref, vg_ref, gw_ref, gbs_ref, qg_ref, kg_ref,
                       cw_ref, cb_ref, fb_ref, hg_ref, tril_ref, band_ref, tri01_ref, tribias_ref,
                       y_ref, ko_ref, vo_ref, convo_ref, c_ref, n_ref, m_ref,
                       kprev, vprev, xbuf, first_tile, pump):
    ts = PROMPT_TILE

    wts = [(gw_ref[gi] * tril_ref[...]).astype(BF16) for gi in range(A_GROUPS)]
    for c in range(ts // WINDOW):
        rows = slice(c * WINDOW, (c + 1) * WINDOW)
        vnb = (_rms(za_ref[rows, A_WIDTH:2 * A_WIDTH]) * vg_ref[...]).astype(BF16)
        s = jnp.concatenate(
            [_dot(wts[gi], vnb[:, gi * GROUP_DIM:(gi + 1) * GROUP_DIM]) + gbs_ref[:, gi:gi + 1]
             for gi in range(A_GROUPS)], axis=1)
        sg = _silu(za_ref[rows, 2 * A_WIDTH:3 * A_WIDTH])
        y_ref[rows, 0:A_WIDTH] = (za_ref[rows, 0:A_WIDTH] * s * sg).astype(BF16)
        pump()

    kn = _qk_norm(zb_ref[:, B_WIDTH:B_WIDTH + B_KV_WIDTH], kg_ref[...])
    vv = zb_ref[:, B_WIDTH + B_KV_WIDTH:B_WIDTH + 2 * B_KV_WIDTH]
    pump()
    grp = B_HEADS // B_KV_HEADS
    nblk = ts // WINDOW
    lane_lo2 = lax.broadcasted_iota(jnp.int32, (2 * WINDOW, LANES), 1) < B_HEAD_DIM
    kblocks = [kprev[...]] + [kn[b * WINDOW:(b + 1) * WINDOW] for b in range(nblk)]
    vblocks = [vprev[...]] + [vv[b * WINDOW:(b + 1) * WINDOW] for b in range(nblk)]
    heads = [(kh, g) for kh in range(B_KV_HEADS) for g in range(grp)]
    snk = {k: sink_ref[k[0] * grp + k[1]] for k in heads}
    for blk in range(nblk):
        rows = slice(blk * WINDOW, (blk + 1) * WINDOW)
        if blk == 0 and first_tile is not False:
            bias = jnp.where(first_tile, band_ref[1], band_ref[0])
        else:
            bias = band_ref[0]
        kcat = jnp.concatenate([kblocks[blk], kblocks[blk + 1]], axis=0)
        vcat = jnp.concatenate([vblocks[blk], vblocks[blk + 1]], axis=0)
        krol = pltpu.roll(kcat, B_HEAD_DIM, 1)
        vrol = pltpu.roll(vcat, B_HEAD_DIM, 1)
        kdup, vdup = [], []
        for kh in range(B_KV_HEADS):
            own = lane_lo2 if kh == 0 else jnp.logical_not(lane_lo2)
            kdup.append(jnp.where(own, kcat, krol).astype(BF16))
            vdup.append(jnp.where(own, vcat, vrol).astype(BF16))
        qn = _qk_norm(zb_ref[rows, 0:B_WIDTH], qg_ref[...]) * (B_HEAD_DIM ** -0.5)
        pump()
        qs = [jnp.concatenate([_place_q_head(qn, kh * grp + g, WINDOW) for g in range(grp)],
                              axis=0).astype(BF16) for kh in range(B_KV_HEADS)]
        logits = [_dot_nt(qs[kh], kdup[kh]) for kh in range(B_KV_HEADS)]
        pump()
        lg = {(kh, g): logits[kh][g * WINDOW:(g + 1) * WINDOW] + bias for kh, g in heads}
        mx = {k: jnp.maximum(jnp.max(lg[k], axis=-1, keepdims=True), snk[k]) for k in heads}
        pump()
        p = {k: jnp.exp(lg[k] - mx[k]) for k in heads}
        pump()
        rden = {k: 1.0 / (jnp.sum(p[k], axis=-1, keepdims=True) + jnp.exp(snk[k] - mx[k]))
                for k in heads}
        pump()
        pv = [_dot(jnp.concatenate([p[kh, g].astype(BF16) for g in range(grp)], axis=0), vdup[kh])
              for kh in range(B_KV_HEADS)]
        pump()
        outs = {(kh, g): pv[kh][g * WINDOW:(g + 1) * WINDOW] * rden[kh, g] for kh, g in heads}
        yb = jnp.concatenate(
            [_merge_head_pair(outs[(2 * j) // grp, (2 * j) % grp],
                              outs[(2 * j + 1) // grp, (2 * j + 1) % grp], 2 * j, WINDOW)
             for j in range(B_HEADS // 2)], axis=1)
        sgb = _silu(zb_ref[rows, B_WIDTH + 2 * B_KV_WIDTH:ZB_W])
        y_ref[rows, A_WIDTH:A_WIDTH + B_WIDTH] = (yb * sgb).astype(BF16)
        pump()
    kprev[...] = kblocks[nblk]
    vprev[...] = vblocks[nblk]
    ko_ref[...] = kblocks[nblk]
    vo_ref[...] = vblocks[nblk]
    pump()

    xbuf[SUBLANES:SUBLANES + ts, :] = zc_ref[:, 0:2 * C_WIDTH]
    pump()
    ifp = zc_ref[:, 5 * C_WIDTH:5 * C_WIDTH + LANES]
    lf = _log_sigmoid(ifp + fb_ref[...])
    pump()
    cl = MLSTM_CHUNK
    lane_c = lax.broadcasted_iota(jnp.int32, (cl, LANES), 1)
    lane_1 = lax.broadcasted_iota(jnp.int32, (1, LANES), 1)
    m_row = m_ref[...]
    m_out = m_row
    cum_all = _dot_exact01(tri01_ref[...], lf)
    st_col = jnp.where(lane_c < C_HEADS, ifp, cum_all)
    st_row = st_col.T
    tribias = tribias_ref[...]
    pump()
    for hds in MLSTM_HEAD_GROUPS:
        hs = {hd: slice(hd * C_HEAD_DIM, (hd + 1) * C_HEAD_DIM) for hd in hds}
        i_c = {hd: st_col[:, hd:hd + 1] for hd in hds}
        cum_c = {hd: st_col[:, C_HEADS + hd:C_HEADS + hd + 1] for hd in hds}
        i_r = {hd: st_row[hd:hd + 1, :] for hd in hds}
        cum_r = {hd: st_row[C_HEADS + hd:C_HEADS + hd + 1, :] for hd in hds}
        m_prev = {hd: m_row[:, hd:hd + 1] for hd in hds}
        dmat = {hd: cum_c[hd] - cum_r[hd] + i_r[hd] + tribias for hd in hds}
        m_inter = {hd: cum_c[hd] + m_prev[hd] for hd in hds}
        m_t = {hd: jnp.maximum(m_inter[hd], jnp.max(dmat[hd], axis=-1, keepdims=True)) for hd in hds}
        pump()
        q_h = {hd: _silu(_conv_taps(xbuf, cw_ref, cb_ref, hs[hd], ts)).astype(BF16) for hd in hds}
        k_h = {hd: _silu(_conv_taps(xbuf, cw_ref, cb_ref,
                                    slice(C_WIDTH + hs[hd].start, C_WIDTH + hs[hd].stop), ts))
               * (C_HEAD_DIM ** -0.5) for hd in hds}
        pump()
        v_h = {hd: zc_ref[:, 2 * C_WIDTH + hd * C_HEAD_DIM:2 * C_WIDTH + (hd + 1) * C_HEAD_DIM].astype(BF16)
               for hd in hds}
        s_qk = {hd: _dot_nt(q_h[hd], k_h[hd].astype(BF16)) for hd in hds}
        a = {hd: jnp.exp(dmat[hd] - m_t[hd]) * s_qk[hd] for hd in hds}
        pump()
        w_inter = {hd: jnp.exp(m_inter[hd] - m_t[hd]) for hd in hds}
        c_prev = {hd: c_ref[hd] for hd in hds}
        n_prev = {hd: n_ref[hd:hd + 1, :] for hd in hds}
        inter = {hd: _dot(q_h[hd], c_prev[hd].astype(BF16)) for hd in hds}
        intra = {hd: _dot(a[hd].astype(BF16), v_h[hd]) for hd in hds}
        pump()
        den = {hd: jnp.sum(a[hd], axis=-1, keepdims=True)
               + w_inter[hd] * jnp.sum(q_h[hd].astype(F32) * n_prev[hd], axis=-1, keepdims=True)
               for hd in hds}
        rnorm = {hd: 1.0 / jnp.maximum(jnp.abs(den[hd]), jnp.exp(-m_t[hd])) for hd in hds}
        hh = {hd: (intra[hd] + w_inter[hd] * inter[hd]) * rnorm[hd] for hd in hds}
        pump()
        for hd in hds:
            o_cols = slice(3 * C_WIDTH + hd * C_HEAD_DIM, 3 * C_WIDTH + (hd + 1) * C_HEAD_DIM)
            g_cols = slice(4 * C_WIDTH + hd * C_HEAD_DIM, 4 * C_WIDTH + (hd + 1) * C_HEAD_DIM)
            gate_o = _sigmoid(zc_ref[:, o_cols]) * _silu(zc_ref[:, g_cols])
            y_cols = slice(A_WIDTH + B_WIDTH + hd * C_HEAD_DIM, A_WIDTH + B_WIDTH + (hd + 1) * C_HEAD_DIM)
            y_ref[:, y_cols] = (_rms(hh[hd]) * hg_ref[:, hs[hd]] * gate_o).astype(BF16)
        pump()
        total = {hd: cum_r[hd][:, cl - 1:cl] for hd in hds}
        g_r = {hd: total[hd] - cum_r[hd] + i_r[hd] for hd in hds}
        g_c = {hd: total[hd] - cum_c[hd] + i_c[hd] for hd in hds}
        m_new = {hd: jnp.maximum(total[hd] + m_prev[hd], jnp.max(g_r[hd], axis=-1, keepdims=True))
                 for hd in hds}
        kw = {hd: jnp.exp(g_c[hd] - m_new[hd]) * k_h[hd] for hd in hds}
        decay = {hd: jnp.exp(total[hd] + m_prev[hd] - m_new[hd]) for hd in hds}
        pump()
        upd = {hd: _dot(kw[hd].T.astype(BF16), v_h[hd]) for hd in hds}
        for hd in hds:
            c_ref[hd] = decay[hd] * c_prev[hd] + upd[hd]
            n_ref[hd:hd + 1, :] = decay[hd] * n_prev[hd] + jnp.sum(kw[hd], axis=0, keepdims=True)
            m_out = jnp.where(lane_1 == hd, m_new[hd], m_out)
        pump()
    m_ref[...] = m_out
    tail = xbuf[ts:ts + SUBLANES, :]
    xbuf[0:SUBLANES, :] = tail
    convo_ref[...] = tail


def _prompt_mask_constants():
    r = np.arange(WINDOW)[:, None]
    c = np.arange(2 * WINDOW)[None, :]
    band = (c > r) & (c <= r + WINDOW)
    band_first = band & (c >= WINDOW)
    band_bias = np.where(np.stack([band, band_first]), 0.0, NEG).astype(np.float32)
    tril = (np.arange(WINDOW)[:, None] >= np.arange(WINDOW)[None, :]).astype(np.float32)
    tri = np.arange(MLSTM_CHUNK)[:, None] >= np.arange(MLSTM_CHUNK)[None, :]
    return (jnp.asarray(tril), jnp.asarray(band_bias), jnp.asarray(tri, dtype=BF16),
            jnp.asarray(np.where(tri, 0.0, NEG).astype(np.float32)))


N_MIX_PARAMS = 13
MLSTM_HEAD_GROUPS = ((0, 1), (2, 3))
MIX_PUMP_CALLS = 37
TAIL_FILL_PIECES = 8
MXU_PIECE_COLS = 256


class _Interleaver:
    def __init__(self, pieces, calls, hold_back=0):
        self._pieces = list(pieces)
        self._hold_back = hold_back
        self._spread = len(self._pieces) - hold_back
        self._emitted = 0
        self._calls = calls
        self._call = 0

    def __call__(self):
        self._call += 1
        target = (self._call * self._spread) // self._calls
        while self._emitted < target:
            self._pieces.pop(0)()
            self._emitted += 1

    def finish(self):
        assert self._call == self._calls and len(self._pieces) == self._hold_back, self._call
        return self._pieces


def _gate_pieces(h_ref, wmg_ref, bmg_ref, g_ref):
    def piece(off):
        cols = slice(off, off + MXU_PIECE_COLS)
        def run():
            g_ref[:, cols] = _sigmoid(_dot(h_ref[...], wmg_ref[:, cols]) + bmg_ref[:, cols])
        return run
    return [piece(off) for off in range(0, 3 * D_MODEL, MXU_PIECE_COLS)]


def _merge_and_project(x, mod_ref, g_ref, y_ref, wa_ref, wb_ref, wc_ref, wo_ref, fillers=()):
    fillers = list(fillers)
    per_stage = -(-len(fillers) // 4)
    merged = None
    for i, wbr_ref in enumerate((wa_ref, wb_ref, wc_ref)):
        for piece in fillers[i * per_stage:(i + 1) * per_stage]:
            piece()
        term = (g_ref[:, i * D_MODEL:(i + 1) * D_MODEL]
                * _dot(y_ref[:, i * A_WIDTH:(i + 1) * A_WIDTH], wbr_ref[...]))
        merged = term if merged is None else merged + term
    for piece in fillers[3 * per_stage:]:
        piece()
    ada_gate = mod_ref[:, 2 * D_MODEL:3 * D_MODEL]
    return x + ada_gate * _dot(merged.astype(BF16), wo_ref[...])


def _prompt_layer_kernel(tiles_per_seq, sink_ref, x2_ref, xn_ref, mod_ref, modn_ref, ng_ref,
                         wcat_ref, bcat_ref, *rest):
    mix_params = rest[:N_MIX_PARAMS]
    wmg_ref, bmg_ref, wa_ref, wb_ref, wc_ref, wo_ref = rest[N_MIX_PARAMS:N_MIX_PARAMS + 6]
    o_ref, ko_ref, vo_ref, convo_ref, c_ref, n_ref, m_ref = rest[N_MIX_PARAMS + 6:N_MIX_PARAMS + 13]
    (za0, zb0, zc0, za1, zb1, zc1, h0, h1, y_scr, g_scr, kprev, vprev, xbuf) = rest[N_MIX_PARAMS + 13:]
    ts = PROMPT_TILE
    z = ((za0, zb0, zc0), (za1, zb1, zc1))
    h = (h0, h1)
    k = pl.program_id(0)
    seq_start = (k % (tiles_per_seq // 2)) == 0

    @pl.when(k == 0)
    def _():
        h0[...] = _modulated_norm(x2_ref[0:ts, :], mod_ref, ng_ref)
        for piece in _inproj_pieces(lambda: h0[...], wcat_ref, bcat_ref, *z[0], 512):
            piece()

    @pl.when(seq_start)
    def _():
        kprev[...] = jnp.zeros_like(kprev)
        vprev[...] = jnp.zeros_like(vprev)
        xbuf[0:SUBLANES, :] = jnp.zeros((SUBLANES, 2 * C_WIDTH), F32)
        c_ref[...] = jnp.zeros_like(c_ref)
        n_ref[...] = jnp.zeros_like(n_ref)
        m_ref[...] = jnp.zeros_like(m_ref)

    for half in range(2):
        cur, nxt = half, 1 - half
        rows = slice(half * ts, (half + 1) * ts)
        if half == 0:
            h[nxt][...] = _modulated_norm(x2_ref[ts:2 * ts, :], mod_ref, ng_ref)
        else:
            h[nxt][...] = _modulated_norm(xn_ref[...], modn_ref, ng_ref)
        get_h_next = functools.partial(lambda r: r[...], h[nxt])
        hold = TAIL_FILL_PIECES if half == 1 else 0
        proj = _inproj_pieces(get_h_next, wcat_ref, bcat_ref, *z[nxt], MXU_PIECE_COLS)
        pump = _Interleaver(
            proj[:len(proj) - hold] + _gate_pieces(h[cur], wmg_ref, bmg_ref, g_scr)
            + proj[len(proj) - hold:], MIX_PUMP_CALLS, hold_back=hold)
        _prompt_mix_kernel(sink_ref, *z[cur], *mix_params,
                           y_scr, ko_ref, vo_ref, convo_ref, c_ref, n_ref, m_ref, kprev, vprev, xbuf,
                           first_tile=seq_start if half == 0 else False, pump=pump)
        o_ref[rows, :] = _merge_and_project(x2_ref[rows, :], mod_ref, g_scr, y_scr,
                                            wa_ref, wb_ref, wc_ref, wo_ref, fillers=pump.finish())


def _prompt_layer_call(layer, x2, mod, lw, batch, seq):
    ts = PROMPT_TILE
    nt = seq // ts
    assert nt % 2 == 0
    last_tile = batch * nt - 1
    const2 = lambda k: (0, 0)
    const3 = lambda k: (0, 0, 0)
    per_b3 = lambda k: ((2 * k) // nt, 0, 0)
    next_tile = lambda k: jnp.minimum(2 * k + 2, last_tile)
    once = pl.Buffered(1)
    return pl.pallas_call(
        functools.partial(_prompt_layer_kernel, nt),
        grid=(batch * nt // 2,),
        in_specs=[
            pl.BlockSpec(memory_space=pltpu.SMEM),
            pl.BlockSpec((2 * ts, D_MODEL), lambda k: (k, 0)),
            pl.BlockSpec((ts, D_MODEL), lambda k: (next_tile(k), 0)),
            pl.BlockSpec((None, 1, 3 * D_MODEL), per_b3),
            pl.BlockSpec((None, 1, 3 * D_MODEL), lambda k: (next_tile(k) // nt, 0, 0)),
            pl.BlockSpec((1, D_MODEL), const2),
            _layer_weight_spec(layer, D_MODEL, ZCAT_W),
            pl.BlockSpec((1, ZCAT_W), const2),
            pl.BlockSpec((1, A_WIDTH), const2),
            pl.BlockSpec((A_GROUPS, WINDOW, WINDOW), const3),
            pl.BlockSpec((WINDOW, LANES), const2),
            pl.BlockSpec((1, B_WIDTH), const2),
            pl.BlockSpec((1, B_KV_WIDTH), const2),
            pl.BlockSpec((C_CONV, 2 * C_WIDTH), const2),
            pl.BlockSpec((1, 2 * C_WIDTH), const2),
            pl.BlockSpec((1, LANES), const2),
            pl.BlockSpec((1, C_WIDTH), const2),
            pl.BlockSpec((WINDOW, WINDOW), const2),
            pl.BlockSpec((2, WINDOW, 2 * WINDOW), const3),
            pl.BlockSpec((MLSTM_CHUNK, MLSTM_CHUNK), const2),
            pl.BlockSpec((MLSTM_CHUNK, MLSTM_CHUNK), const2),
            _layer_weight_spec(layer, D_MODEL, 3 * D_MODEL),
            pl.BlockSpec((1, 3 * D_MODEL), const2),
            pl.BlockSpec((A_WIDTH, D_MODEL), const2, pipeline_mode=once),
            pl.BlockSpec((B_WIDTH, D_MODEL), const2, pipeline_mode=once),
            pl.BlockSpec((C_WIDTH, D_MODEL), const2, pipeline_mode=once),
            pl.BlockSpec((D_MODEL, D_MODEL), const2, pipeline_mode=once),
        ],
        out_specs=[
            pl.BlockSpec((2 * ts, D_MODEL), lambda k: (k, 0)),
            pl.BlockSpec((None, WINDOW, B_KV_WIDTH), per_b3),
            pl.BlockSpec((None, WINDOW, B_KV_WIDTH), per_b3),
            pl.BlockSpec((None, SUBLANES, 2 * C_WIDTH), per_b3),
            pl.BlockSpec((None, C_HEADS, C_HEAD_DIM, C_HEAD_DIM), lambda k: ((2 * k) // nt, 0, 0, 0)),
            pl.BlockSpec((None, C_HEADS, C_HEAD_DIM), per_b3),
            pl.BlockSpec((None, 1, LANES), per_b3),
        ],
        out_shape=[
            jax.ShapeDtypeStruct((batch * seq, D_MODEL), F32),
            jax.ShapeDtypeStruct((batch, WINDOW, B_KV_WIDTH), F32),
            jax.ShapeDtypeStruct((batch, WINDOW, B_KV_WIDTH), F32),
            jax.ShapeDtypeStruct((batch, SUBLANES, 2 * C_WIDTH), F32),
            jax.ShapeDtypeStruct((batch, C_HEADS, C_HEAD_DIM, C_HEAD_DIM), F32),
            jax.ShapeDtypeStruct((batch, C_HEADS, C_HEAD_DIM), F32),
            jax.ShapeDtypeStruct((batch, 1, LANES), F32),
        ],
        scratch_shapes=(
            [pltpu.VMEM((ts, w), F32) for w in (ZA_W, ZB_W, ZC_W)] * 2
            + [pltpu.VMEM((ts, D_MODEL), BF16)] * 2
            + [pltpu.VMEM((ts, Y_W), BF16),
               pltpu.VMEM((ts, 3 * D_MODEL), F32),
               pltpu.VMEM((WINDOW, B_KV_WIDTH), F32),
               pltpu.VMEM((WINDOW, B_KV_WIDTH), F32),
               pltpu.VMEM((ts + SUBLANES, 2 * C_WIDTH), F32)]),
        compiler_params=pltpu.CompilerParams(
            dimension_semantics=("arbitrary",), vmem_limit_bytes=VMEM_LIMIT),
        name="prompt_layer",
    )(lw["sinks"], x2, x2, mod, mod, lw["ng"], lw["wcat"], lw["bcat"],
      lw["vg"], lw["gws"], lw["gbs_col"], lw["qg"], lw["kg"], lw["cw"], lw["cb"], lw["fb"], lw["hg"],
      *_prompt_mask_constants(),
      lw["wmg"], lw["bmg"], lw["wa"], lw["wb"], lw["wc"], lw["wo"])


def _sample_mix_kernel(sink_ref, za_ref, zb_ref, zc_ref, kc_ref, vc_ref, cs_ref, c0_ref, n0_ref,
                       m0_ref, vg_ref, gwb_ref, gbs_ref, qg_ref, kg_ref, cw_ref, cb_ref, fb_ref,
                       hg_ref,
                       y_ref, vrow_ref, ko_ref, vo_ref, convo_ref, c1_ref, n1_ref, m1_ref,
                       xbuf, first_layer=None):
    nb = SAMPLE_NB
    t = SUBLANES
    rows = nb * t
    if first_layer is not None:
        for other in range(DEPTH):
            if other != first_layer:
                c1_ref[other] = jnp.zeros(c1_ref.shape[1:], F32)
        c1_ref = c1_ref.at[first_layer]
    tok_r = lax.broadcasted_iota(jnp.int32, (rows, rows), 0)
    tok_c = lax.broadcasted_iota(jnp.int32, (rows, rows), 1)
    same_b = (tok_r // t) == (tok_c // t)
    causal_b = same_b & (tok_c <= tok_r)

    u = za_ref[:, 0:A_WIDTH]
    vn = _rms(za_ref[:, A_WIDTH:2 * A_WIDTH]) * vg_ref[...]
    sg = _silu(za_ref[:, 2 * A_WIDTH:3 * A_WIDTH])
    vrow_ref[...] = vn
    vnb = vn.astype(BF16)
    s_cols = []
    for gi in range(A_GROUPS):
        s_cols.append(_dot(gwb_ref[gi], vnb[:, gi * GROUP_DIM:(gi + 1) * GROUP_DIM])
                      + gbs_ref[:, gi:gi + 1])
    y_ref[:, 0:A_WIDTH] = (u * jnp.concatenate(s_cols, axis=1) * sg).astype(BF16)

    qn = _qk_norm(zb_ref[:, 0:B_WIDTH], qg_ref[...]) * (B_HEAD_DIM ** -0.5)
    kn = _qk_norm(zb_ref[:, B_WIDTH:B_WIDTH + B_KV_WIDTH], kg_ref[...])
    vv = zb_ref[:, B_WIDTH + B_KV_WIDTH:B_WIDTH + 2 * B_KV_WIDTH]
    sgb = _silu(zb_ref[:, B_WIDTH + 2 * B_KV_WIDTH:ZB_W])
    kn3 = kn.reshape(nb, t, B_KV_WIDTH)
    vv3 = vv.reshape(nb, t, B_KV_WIDTH)
    kcache = kc_ref[...]
    vcache = vc_ref[...]
    pad = jnp.zeros((nb, WINDOW - t, B_KV_WIDTH), F32)
    kall = jnp.concatenate([kcache, kn3, pad], axis=1).astype(BF16)
    vall = jnp.concatenate([vcache, vv3, pad], axis=1).astype(BF16)
    qp = jnp.concatenate([_place_q_head(qn, h, rows).reshape(nb, t, LANES) for h in range(B_HEADS)],
                         axis=1).astype(BF16)
    logits = lax.dot_general(qp, kall, (((2,), (2,)), ((0,), (0,))), preferred_element_type=F32)
    qrow = lax.broadcasted_iota(jnp.int32, (nb, B_HEADS * t, 2 * WINDOW), 1)
    kcol = lax.broadcasted_iota(jnp.int32, (nb, B_HEADS * t, 2 * WINDOW), 2)
    qt = qrow % t
    valid = ((kcol < WINDOW) & (kcol > qt)) | ((kcol >= WINDOW) & ((kcol - WINDOW) <= qt))
    hrow = lax.broadcasted_iota(jnp.int32, (B_HEADS * t, 1), 0) // t
    snk = jnp.zeros((B_HEADS * t, 1), F32)
    for h in range(B_HEADS):
        snk = jnp.where(hrow == h, sink_ref[h], snk)
    lg = jnp.where(valid, logits, NEG)
    mx = jnp.maximum(jnp.max(lg, axis=-1, keepdims=True), snk[None])
    p = jnp.exp(lg - mx)
    den = jnp.sum(p, axis=-1, keepdims=True) + jnp.exp(snk[None] - mx)
    pv = lax.dot_general(p.astype(BF16), vall, (((2,), (1,)), ((0,), (0,))),
                         preferred_element_type=F32) / den
    head_out = [pv[:, h * t:(h + 1) * t, :].reshape(rows, LANES) for h in range(B_HEADS)]
    yb = jnp.concatenate(
        [_merge_head_pair(head_out[2 * j], head_out[2 * j + 1], 2 * j, rows)
         for j in range(B_HEADS // 2)], axis=1)
    y_ref[:, A_WIDTH:A_WIDTH + B_WIDTH] = (yb * sgb).astype(BF16)
    ko_ref[...] = jnp.concatenate([kcache[:, t:, :], kn3], axis=1)
    vo_ref[...] = jnp.concatenate([vcache[:, t:, :], vv3], axis=1)

    xbuf[:, SUBLANES - (C_CONV - 1):SUBLANES, :] = cs_ref[...]
    xbuf[:, SUBLANES:2 * SUBLANES, :] = zc_ref[:, 0:2 * C_WIDTH].reshape(nb, t, 2 * C_WIDTH)
    y3 = cb_ref[...][None]
    for j in range(C_CONV):
        lo = SUBLANES - (C_CONV - 1) + j
        y3 = y3 + cw_ref[j:j + 1, :][None] * xbuf[:, lo:lo + t, :]
    convo_ref[...] = xbuf[:, 2 * SUBLANES - (C_CONV - 1):2 * SUBLANES, :]
    qk = _silu(y3.reshape(rows, 2 * C_WIDTH))
    qall = qk[:, 0:C_WIDTH].astype(BF16)
    kall_c = qk[:, C_WIDTH:2 * C_WIDTH] * (C_HEAD_DIM ** -0.5)
    vall_c = zc_ref[:, 2 * C_WIDTH:3 * C_WIDTH].astype(BF16)
    gate_o = _sigmoid(zc_ref[:, 3 * C_WIDTH:4 * C_WIDTH]) * _silu(zc_ref[:, 4 * C_WIDTH:5 * C_WIDTH])
    ifp = zc_ref[:, 5 * C_WIDTH:5 * C_WIDTH + LANES]
    lf = _log_sigmoid(ifp + fb_ref[...])
    lane_t = lax.broadcasted_iota(jnp.int32, (rows, LANES), 1)
    cum_all = _dot_exact01(jnp.where(causal_b, 1.0, 0.0).astype(BF16), lf)
    tot_all = _dot_exact01(jnp.where(same_b, 1.0, 0.0).astype(BF16), lf)
    st_col = jnp.where(lane_t < C_HEADS, ifp, cum_all)
    st_row = st_col.T
    tot_row = tot_all.T
    m0 = m0_ref[...]
    same_b_bf = jnp.where(same_b, 1.0, 0.0).astype(BF16)
    batch_of_lane = lax.broadcasted_iota(jnp.int32, (nb, 1, rows), 2) // t
    batch_id = lax.broadcasted_iota(jnp.int32, (nb, 1, rows), 0)
    own_tok = batch_of_lane == batch_id
    h_cols = []
    m_out = jnp.zeros((rows, LANES), F32)
    for hd in range(C_HEADS):
        hs = slice(hd * C_HEAD_DIM, (hd + 1) * C_HEAD_DIM)
        i_c = st_col[:, hd:hd + 1]
        cum_c = st_col[:, C_HEADS + hd:C_HEADS + hd + 1]
        tot_c = tot_all[:, C_HEADS + hd:C_HEADS + hd + 1]
        i_r = st_row[hd:hd + 1, :]
        cum_r = st_row[C_HEADS + hd:C_HEADS + hd + 1, :]
        tot_r = tot_row[C_HEADS + hd:C_HEADS + hd + 1, :]
        m_prev = m0[:, hd:hd + 1]
        dmat = jnp.where(causal_b, cum_c - cum_r + i_r, NEG)
        m_inter = cum_c + m_prev
        m_t = jnp.maximum(m_inter, jnp.max(dmat, axis=-1, keepdims=True))
        q_h = qall[:, hs]
        k_h = kall_c[:, hs]
        v_h = vall_c[:, hs]
        a = jnp.exp(dmat - m_t) * _dot_nt(q_h, k_h.astype(BF16))
        w_inter = jnp.exp(m_inter - m_t)
        c_prev = c0_ref[:, hd]
        n_tok = jnp.broadcast_to(n0_ref[hd][:, None, :], (nb, t, C_HEAD_DIM)).reshape(rows, C_HEAD_DIM)
        inter = lax.dot_general(q_h.reshape(nb, t, C_HEAD_DIM), c_prev.astype(BF16),
                                (((2,), (1,)), ((0,), (0,))), preferred_element_type=F32)
        num = _dot(a.astype(BF16), v_h) + w_inter * inter.reshape(rows, C_HEAD_DIM)
        den = (jnp.sum(a, axis=-1, keepdims=True)
               + w_inter * jnp.sum(q_h.astype(F32) * n_tok, axis=-1, keepdims=True))
        hh = num / jnp.maximum(jnp.abs(den), jnp.exp(-m_t))
        h_cols.append(_rms(hh))
        g_r = tot_r - cum_r + i_r
        g_c = tot_c - cum_c + i_c
        m_new = jnp.maximum(tot_c + m_prev,
                            jnp.max(jnp.where(same_b, g_r, NEG), axis=-1, keepdims=True))
        kw = jnp.exp(g_c - m_new) * k_h
        decay = jnp.exp(tot_c + m_prev - m_new)
        kwt = kw.T
        lhs = jnp.where(own_tok, kwt[None], 0.0).astype(BF16).reshape(nb * C_HEAD_DIM, rows)
        upd = _dot(lhs, v_h).reshape(nb, C_HEAD_DIM, C_HEAD_DIM)
        dec_b = jnp.broadcast_to(decay, (rows, C_HEAD_DIM)).reshape(nb, t, C_HEAD_DIM)[:, 0:1, :]
        c1_ref[:, hd] = dec_b * c_prev + upd
        n1_ref[hd] = decay * n_tok + _dot(same_b_bf, kw.astype(BF16))
        m_out = jnp.where(lane_t == hd, m_new, m_out)
    m1_ref[...] = m_out
    hn = jnp.concatenate(h_cols, axis=1) * hg_ref[...]
    y_ref[:, A_WIDTH + B_WIDTH:Y_W] = (hn * gate_o).astype(BF16)


def _sample_mix_call(l, za, zb, zc, kc, vc, cs, c0, n0t, m0tok, lw, nbatch, c1_all=None):
    nb = SAMPLE_NB
    t = SUBLANES
    rows = nb * t
    tok = lambda i: (i, 0)
    const2 = lambda i: (0, 0)
    const3 = lambda i: (0, 0, 0)
    b3 = lambda i: (i, 0, 0)
    lb4 = lambda i: (l, i, 0, 0)
    operands = [lw["sinks"], za, zb, zc, kc, vc, cs, c0, n0t, m0tok, lw["vg"], lw["gwb"],
                lw["gbs_tok"], lw["qg"], lw["kg"], lw["cw"], lw["cb"], lw["fb"], lw["hg"]]
    c_block = (nb, C_HEADS, C_HEAD_DIM, C_HEAD_DIM)
    if c1_all is None:
        kernel_fn, extra_specs, aliases = functools.partial(_sample_mix_kernel, first_layer=l), [], {}
        c1_spec = pl.BlockSpec((DEPTH,) + c_block, lambda i: (0, i, 0, 0, 0))
    else:
        n_in = len(operands)
        operands.append(c1_all)
        extra_specs = [pl.BlockSpec(memory_space=pl.ANY)]
        aliases = {n_in: 5}
        kernel_fn = lambda *refs: _sample_mix_kernel(*refs[:n_in], *refs[n_in + 1:])
        c1_spec = pl.BlockSpec((None,) + c_block, lambda i: (l, i, 0, 0, 0))
    return pl.pallas_call(
        kernel_fn,
        grid=(nbatch // nb,),
        input_output_aliases=aliases,
        in_specs=[
            pl.BlockSpec(memory_space=pltpu.SMEM),
            pl.BlockSpec((rows, ZA_W), tok),
            pl.BlockSpec((rows, ZB_W), tok),
            pl.BlockSpec((rows, ZC_W), tok),
            pl.BlockSpec((None, nb, WINDOW, B_KV_WIDTH), lb4),
            pl.BlockSpec((None, nb, WINDOW, B_KV_WIDTH), lb4),
            pl.BlockSpec((None, nb, C_CONV - 1, 2 * C_WIDTH), lb4),
            pl.BlockSpec((None, nb, C_HEADS, C_HEAD_DIM, C_HEAD_DIM), lambda i: (l, i, 0, 0, 0)),
            pl.BlockSpec((None, C_HEADS, nb, C_HEAD_DIM), lambda i: (l, 0, i, 0)),
            pl.BlockSpec((None, rows, LANES), lambda i: (l, i, 0)),
            pl.BlockSpec((1, A_WIDTH), const2),
            pl.BlockSpec((A_GROUPS, rows, rows), const3),
            pl.BlockSpec((rows, LANES), const2),
            pl.BlockSpec((1, B_WIDTH), const2),
            pl.BlockSpec((1, B_KV_WIDTH), const2),
            pl.BlockSpec((C_CONV, 2 * C_WIDTH), const2),
            pl.BlockSpec((1, 2 * C_WIDTH), const2),
            pl.BlockSpec((1, LANES), const2),
            pl.BlockSpec((1, C_WIDTH), const2),
        ] + extra_specs,
        out_specs=[
            pl.BlockSpec((rows, Y_W), tok),
            pl.BlockSpec((rows, A_WIDTH), tok),
            pl.BlockSpec((nb, WINDOW, B_KV_WIDTH), b3),
            pl.BlockSpec((nb, WINDOW, B_KV_WIDTH), b3),
            pl.BlockSpec((nb, C_CONV - 1, 2 * C_WIDTH), b3),
            c1_spec,
            pl.BlockSpec((C_HEADS, rows, C_HEAD_DIM), lambda i: (0, i, 0)),
            pl.BlockSpec((rows, LANES), tok),
        ],
        out_shape=[
            jax.ShapeDtypeStruct((nbatch * t, Y_W), BF16),
            jax.ShapeDtypeStruct((nbatch * t, A_WIDTH), F32),
            jax.ShapeDtypeStruct((nbatch, WINDOW, B_KV_WIDTH), F32),
            jax.ShapeDtypeStruct((nbatch, WINDOW, B_KV_WIDTH), F32),
            jax.ShapeDtypeStruct((nbatch, C_CONV - 1, 2 * C_WIDTH), F32),
            jax.ShapeDtypeStruct((DEPTH, nbatch, C_HEADS, C_HEAD_DIM, C_HEAD_DIM), F32),
            jax.ShapeDtypeStruct((C_HEADS, nbatch * t, C_HEAD_DIM), F32),
            jax.ShapeDtypeStruct((nbatch * t, LANES), F32),
        ],
        scratch_shapes=[pltpu.VMEM((nb, 2 * SUBLANES, 2 * C_WIDTH), F32)],
        compiler_params=pltpu.CompilerParams(
            dimension_semantics=("arbitrary",), vmem_limit_bytes=VMEM_LIMIT),
        name="sample_mixer",
    )(*operands)


def _layer_weights(l, wcat_all, wmg_all, b_in, gmlp_vnorm_g, gmlp_ws, gmlp_bs, swa_qnorm_g,
                   swa_knorm_g, swa_sinks, mlstm_conv_w, mlstm_conv_b, mlstm_f_bias, mlstm_hnorm_g,
                   w_branch_a, w_branch_b, w_branch_c, w_out, norm_g, dec_seq):
    bl = b_in[l]
    bcat = jnp.concatenate([bl[:COL_CI], bl[COL_CO:COL_MG], bl[COL_CI:COL_CO],
                            jnp.zeros((LANES - 2 * C_HEADS,), F32)])
    t = dec_seq
    nb = SAMPLE_NB
    ws_t = gmlp_ws[l][:, :t, :t] * jnp.tril(jnp.ones((t, t), F32))
    eye = jnp.eye(nb, dtype=F32)
    gwb = jnp.einsum("bc,gts->gbtcs", eye, ws_t).reshape(A_GROUPS, nb * t, nb * t).astype(BF16)
    gbs_col = jnp.pad(gmlp_bs[l].T, ((0, 0), (0, LANES - A_GROUPS)))
    gbs_tok = jnp.pad(jnp.tile(gmlp_bs[l][:, :t].T, (nb, 1)), ((0, 0), (0, LANES - A_GROUPS)))
    fb = jnp.pad(mlstm_f_bias[l], (C_HEADS, LANES - 2 * C_HEADS)).reshape(1, LANES)
    return dict(
        ng=norm_g[l].reshape(1, D_MODEL),
        wcat=wcat_all, bcat=bcat.reshape(1, ZCAT_W),
        wmg=wmg_all, bmg=bl[COL_MG:].reshape(1, 3 * D_MODEL),
        wa=w_branch_a[l].astype(BF16), wb=w_branch_b[l].astype(BF16),
        wc=w_branch_c[l].astype(BF16), wo=w_out[l].astype(BF16),
        vg=gmlp_vnorm_g[l].reshape(1, A_WIDTH), gws=gmlp_ws[l], gwb=gwb,
        gbs_col=gbs_col, gbs_tok=gbs_tok,
        qg=jnp.tile(swa_qnorm_g[l], B_HEADS).reshape(1, B_WIDTH),
        kg=jnp.tile(swa_knorm_g[l], B_KV_HEADS).reshape(1, B_KV_WIDTH),
        sinks=swa_sinks[l],
        cw=mlstm_conv_w[l], cb=mlstm_conv_b[l].reshape(1, 2 * C_WIDTH), fb=fb,
        hg=mlstm_hnorm_g[l].reshape(1, C_WIDTH),
    )


def kernel(x_prompt, x_sample, cache_swa_k, cache_swa_v, state_mlstm_conv, state_mlstm_C, state_mlstm_n, state_mlstm_m, c_prompt, c_sample, ada_w, ada_b, norm_g, w_in, b_in, gmlp_vnorm_g, gmlp_ws, gmlp_bs, swa_qnorm_g, swa_knorm_g, swa_sinks, mlstm_conv_w, mlstm_conv_b, mlstm_f_bias, mlstm_hnorm_g, w_branch_a, w_branch_b, w_branch_c, w_out):
    batch, seq, _ = x_prompt.shape
    nbatch, dec_seq, _ = x_sample.shape
    assert dec_seq == SUBLANES and seq % PROMPT_TILE == 0 and nbatch % SAMPLE_NB == 0
    assert seq % PROJ_TILE == 0 and (nbatch * dec_seq) % PROJ_TILE == 0
    wb_len = cache_swa_k.shape[2]
    assert wb_len == WINDOW

    nc = batch + nbatch
    nc_pad = -(-nc // SUBLANES) * SUBLANES
    c_all = jnp.concatenate([c_prompt, c_sample, jnp.zeros((nc_pad - nc, D_MODEL), F32)], axis=0)
    mod_all = _ada_call(c_all, ada_w, ada_b)

    xp = x_prompt.reshape(batch * seq, D_MODEL)
    xs = x_sample.reshape(nbatch * dec_seq, D_MODEL)
    kc_all = cache_swa_k.reshape(DEPTH, nbatch, WINDOW, B_KV_WIDTH)
    vc_all = cache_swa_v.reshape(DEPTH, nbatch, WINDOW, B_KV_WIDTH)
    n0t_all = jnp.transpose(state_mlstm_n, (0, 2, 1, 3))
    m0tok_all = jnp.pad(jnp.repeat(state_mlstm_m, dec_seq, axis=1),
                        ((0, 0), (0, 0), (0, LANES - C_HEADS)))
    wcat_all, wmg_all = _weight_prep_call(w_in)
    outs_p = [[] for _ in range(6)]
    outs_s = [[] for _ in range(6)]
    vrows = []
    c1_all = None
    for l in range(DEPTH):
        lw = _layer_weights(l, wcat_all, wmg_all, b_in, gmlp_vnorm_g, gmlp_ws, gmlp_bs, swa_qnorm_g,
                            swa_knorm_g, swa_sinks, mlstm_conv_w, mlstm_conv_b, mlstm_f_bias,
                            mlstm_hnorm_g, w_branch_a, w_branch_b, w_branch_c, w_out, norm_g,
                            dec_seq)
        mod_p = mod_all[l, :batch].reshape(batch, 1, 3 * D_MODEL)
        mod_s = jnp.repeat(mod_all[l, batch:nc], dec_seq, axis=0)

        xp, ko, vo, convo, c1, n1, m1 = _prompt_layer_call(l, xp, mod_p, lw, batch, seq)
        outs_p[0].append(ko.reshape(batch, WINDOW, B_KV_HEADS, B_HEAD_DIM))
        outs_p[1].append(vo.reshape(batch, WINDOW, B_KV_HEADS, B_HEAD_DIM))
        outs_p[2].append(convo[:, SUBLANES - (C_CONV - 1):, :])
        outs_p[3].append(c1)
        outs_p[4].append(n1)
        outs_p[5].append(m1[:, 0, :C_HEADS])

        za, zb, zc = _inproj_call(l, xs, mod_s, lw["ng"], lw["wcat"], lw["bcat"], None)
        y, vrow, ko, vo, convo, c1_all, n1tok, m1tok = _sample_mix_call(
            l, za, zb, zc, kc_all, vc_all, state_mlstm_conv, state_mlstm_C, n0t_all, m0tok_all,
            lw, nbatch, c1_all)
        xs = _outproj_call(l, xs, mod_s, lw["ng"], y, lw["wmg"], lw["bmg"], lw["wa"], lw["wb"],
                           lw["wc"], lw["wo"], None)
        outs_s[0].append(ko.reshape(nbatch, WINDOW, B_KV_HEADS, B_HEAD_DIM))
        outs_s[1].append(vo.reshape(nbatch, WINDOW, B_KV_HEADS, B_HEAD_DIM))
        outs_s[2].append(convo)
        outs_s[4].append(jnp.transpose(n1tok[:, ::dec_seq, :], (1, 0, 2)))
        outs_s[5].append(m1tok[::dec_seq, :C_HEADS])
        vrows.append(vrow.reshape(nbatch, dec_seq, A_WIDTH))

    sp = [jnp.stack(o) for o in outs_p]
    ss = [jnp.stack(o) if o else None for o in outs_s]
    return (xp.reshape(batch, seq, D_MODEL), xs.reshape(nbatch, dec_seq, D_MODEL),
            sp[0], sp[1], sp[2], sp[3], sp[4], sp[5],
            ss[0], ss[1], ss[2], c1_all, ss[4], ss[5], jnp.stack(vrows))
```

```python
import functools

import numpy as np
import jax
import jax.numpy as jnp
from jax import lax
from jax.experimental import pallas as pl
from jax.experimental.pallas import tpu as pltpu

F32 = jnp.float32
BF16 = jnp.bfloat16

D_MODEL = 1024
DEPTH = 2
A_WIDTH = 512
A_GROUPS = 4
GROUP_DIM = 128
B_HEADS = 8
B_KV_HEADS = 2
B_HEAD_DIM = 64
B_WIDTH = 512
B_KV_WIDTH = 128
WINDOW = 128
C_HEADS = 4
C_HEAD_DIM = 128
C_WIDTH = 512
C_CONV = 4
EPS = 1e-6
NEG = -1e30

LANES = 128
SUBLANES = 8
VMEM_LIMIT = 56 * 1024 * 1024

ZA_W = 3 * A_WIDTH
ZB_W = 2 * B_WIDTH + 2 * B_KV_WIDTH
ZC_W = 2 * C_WIDTH + 3 * C_WIDTH + LANES
ZCAT_W = ZA_W + ZB_W + ZC_W
Y_W = A_WIDTH + B_WIDTH + C_WIDTH

PROMPT_TILE = 256
MLSTM_CHUNK = PROMPT_TILE
SAMPLE_NB = 16
PROJ_TILE = 512


def _sigmoid(x):
    return 0.5 * jnp.tanh(0.5 * x) + 0.5


def _silu(x):
    t = 0.5 * x
    return t * (jnp.tanh(t) + 1.0)


def _log_sigmoid(x):
    return jnp.minimum(x, 0.0) - jnp.log1p(jnp.exp(-jnp.abs(x)))


def _rms(x):
    return x * lax.rsqrt(jnp.mean(x * x, axis=-1, keepdims=True) + EPS)


def _dot(a, b):
    return jnp.dot(a, b, preferred_element_type=F32)


def _dot_nt(a, b):
    return lax.dot_general(a, b, (((1,), (1,)), ((), ())), preferred_element_type=F32)


def _dot_exact01(m01, x):
    hi = x.astype(BF16)
    r1 = x - hi.astype(F32)
    mid = r1.astype(BF16)
    lo = (r1 - mid.astype(F32)).astype(BF16)
    return _dot(m01, hi) + _dot(m01, mid) + _dot(m01, lo)


def _modulated_norm(x, mod_ref, ng_ref):
    xn = _rms(x) * ng_ref[...]
    shift = mod_ref[:, 0:D_MODEL]
    scale = mod_ref[:, D_MODEL:2 * D_MODEL]
    return (xn * (1.0 + scale) + shift).astype(BF16)


def _head_rms_scale(x2, lane_lo):
    s0 = jnp.sum(jnp.where(lane_lo, x2, 0.0), axis=-1, keepdims=True)
    s1 = jnp.sum(jnp.where(lane_lo, 0.0, x2), axis=-1, keepdims=True)
    r0 = lax.rsqrt(s0 * (1.0 / B_HEAD_DIM) + EPS)
    r1 = lax.rsqrt(s1 * (1.0 / B_HEAD_DIM) + EPS)
    return jnp.where(lane_lo, r0, r1)


def _qk_norm(x, g_row):
    rows, width = x.shape
    lane_lo = lax.broadcasted_iota(jnp.int32, (rows, LANES), 1) < B_HEAD_DIM
    outs = []
    for j in range(width // LANES):
        slab = x[:, j * LANES:(j + 1) * LANES]
        outs.append(slab * _head_rms_scale(slab * slab, lane_lo))
    y = outs[0] if len(outs) == 1 else jnp.concatenate(outs, axis=1)
    return y * g_row


def _ada_kernel(c_ref, w_ref, b_ref, o_ref):
    c = c_ref[...]
    o_ref[...] = _dot(_silu(c).astype(BF16), w_ref[...].astype(BF16)) + b_ref[...]


def _ada_call(c_all, ada_w, ada_b):
    rows = c_all.shape[0]
    return pl.pallas_call(
        _ada_kernel,
        grid=(DEPTH, 3),
        in_specs=[
            pl.BlockSpec((rows, D_MODEL), lambda l, j: (0, 0)),
            pl.BlockSpec((None, D_MODEL, D_MODEL), lambda l, j: (l, 0, j)),
            pl.BlockSpec((None, 1, D_MODEL), lambda l, j: (l, 0, j)),
        ],
        out_specs=pl.BlockSpec((None, rows, D_MODEL), lambda l, j: (l, 0, j)),
        out_shape=jax.ShapeDtypeStruct((DEPTH, rows, 3 * D_MODEL), F32),
        compiler_params=pltpu.CompilerParams(
            dimension_semantics=("arbitrary", "arbitrary"), vmem_limit_bytes=VMEM_LIMIT),
        name="adaln_mod",
    )(c_all, ada_w, ada_b.reshape(DEPTH, 1, 3 * D_MODEL))


COL_CI = ZA_W + ZB_W + 3 * C_WIDTH
COL_CO = COL_CI + 2 * C_HEADS
COL_MG = COL_CO + 2 * C_WIDTH
PREP_CHUNK = 256
PREP_SHIFT = 2 * C_HEADS
N_MAIN = COL_CI // PREP_CHUNK
N_CO = (2 * C_WIDTH) // PREP_CHUNK
N_MG = (3 * D_MODEL) // PREP_CHUNK
J_CIF = N_MAIN + N_CO
J_MG = J_CIF + 1


def _weight_prep_kernel(wa_ref, wb_ref, wcat_ref, wmg_ref):
    j = pl.program_id(1)

    def shifted_t():
        rows = jnp.concatenate([wa_ref[PREP_SHIFT:PREP_CHUNK, :], wb_ref[...]], axis=0)
        return rows.astype(BF16).T

    @pl.when(j < N_MAIN)
    def _():
        wcat_ref[...] = wa_ref[...].astype(BF16).T

    @pl.when((j >= N_MAIN) & (j < J_CIF))
    def _():
        wcat_ref[...] = shifted_t()

    @pl.when(j == J_CIF)
    def _():
        row = lax.broadcasted_iota(jnp.int32, (PREP_CHUNK, D_MODEL), 0)
        wcat_ref[...] = jnp.where(row < PREP_SHIFT, wa_ref[...], 0.0).astype(BF16).T

    @pl.when(j >= J_MG)
    def _():
        wmg_ref[...] = shifted_t()


def _weight_prep_call(w_in):
    in_width = w_in.shape[-1]
    assert in_width == COL_MG + 3 * D_MODEL
    assert COL_CI % PREP_CHUNK == 0 and COL_CO % PREP_CHUNK == PREP_SHIFT == COL_MG % PREP_CHUNK
    w_t = jnp.swapaxes(w_in, 1, 2)
    assert in_width % PREP_SHIFT == 0 and PREP_SHIFT == SUBLANES
    last_rows = in_width // PREP_SHIFT - 1
    groups_per_chunk = PREP_CHUNK // PREP_SHIFT

    def src_block(j):
        return jnp.where(j < J_CIF, j, jnp.where(j == J_CIF, N_MAIN, j - 1))

    return pl.pallas_call(
        _weight_prep_kernel,
        grid=(DEPTH, J_MG + N_MG),
        in_specs=[
            pl.BlockSpec((None, PREP_CHUNK, D_MODEL), lambda l, j: (l, src_block(j), 0)),
            pl.BlockSpec((None, PREP_SHIFT, D_MODEL),
                         lambda l, j: (l, jnp.minimum((src_block(j) + 1) * groups_per_chunk,
                                                      last_rows), 0)),
        ],
        out_specs=[
            pl.BlockSpec((None, D_MODEL, PREP_CHUNK), lambda l, j: (l, 0, jnp.minimum(j, J_CIF))),
            pl.BlockSpec((None, D_MODEL, PREP_CHUNK), lambda l, j: (l, 0, jnp.maximum(j - J_MG, 0))),
        ],
        out_shape=[
            jax.ShapeDtypeStruct((DEPTH, D_MODEL, ZCAT_W), BF16),
            jax.ShapeDtypeStruct((DEPTH, D_MODEL, 3 * D_MODEL), BF16),
        ],
        compiler_params=pltpu.CompilerParams(
            dimension_semantics=("arbitrary", "arbitrary"), vmem_limit_bytes=VMEM_LIMIT),
        name="weight_prep",
    )(w_t, w_t)


def _col_chunks(width, step):
    return [(o, min(step, width - o)) for o in range(0, width, step)]


def _inproj_pieces(get_h, w_ref, b_ref, za_ref, zb_ref, zc_ref, step):
    def piece(o_ref, off, woff, w):
        def run():
            o_ref[:, off:off + w] = _dot(get_h(), w_ref[:, woff:woff + w]) + b_ref[:, woff:woff + w]
        return run
    pieces = []
    base = 0
    for o_ref, width in ((za_ref, ZA_W), (zb_ref, ZB_W), (zc_ref, ZC_W)):
        pieces += [piece(o_ref, off, base + off, w) for off, w in _col_chunks(width, step)]
        base += width
    return pieces


def _inproj_kernel(x_ref, mod_ref, ng_ref, w_ref, b_ref, za_ref, zb_ref, zc_ref):
    h = _modulated_norm(x_ref[...], mod_ref, ng_ref)
    for piece in _inproj_pieces(lambda: h, w_ref, b_ref, za_ref, zb_ref, zc_ref, 512):
        piece()


def _mod_spec(tm, tokens_per_batch):
    if tokens_per_batch is None:
        return pl.BlockSpec((tm, 3 * D_MODEL), lambda i: (i, 0))
    tiles_per_batch = tokens_per_batch // tm
    return pl.BlockSpec((None, 1, 3 * D_MODEL), lambda i: (i // tiles_per_batch, 0, 0))


def _layer_weight_spec(layer, rows, cols):
    return pl.BlockSpec((None, rows, cols), lambda i: (layer, 0, 0), pipeline_mode=pl.Buffered(1))


def _inproj_call(layer, x2, mod, ng, wcat, bcat, tokens_per_batch):
    ntok = x2.shape[0]
    tm = PROJ_TILE
    const = lambda i: (0, 0)
    return pl.pallas_call(
        _inproj_kernel,
        grid=(ntok // tm,),
        in_specs=[
            pl.BlockSpec((tm, D_MODEL), lambda i: (i, 0)),
            _mod_spec(tm, tokens_per_batch),
            pl.BlockSpec((1, D_MODEL), const),
            _layer_weight_spec(layer, D_MODEL, ZCAT_W),
            pl.BlockSpec((1, ZCAT_W), const),
        ],
        out_specs=[
            pl.BlockSpec((tm, ZA_W), lambda i: (i, 0)),
            pl.BlockSpec((tm, ZB_W), lambda i: (i, 0)),
            pl.BlockSpec((tm, ZC_W), lambda i: (i, 0)),
        ],
        out_shape=[
            jax.ShapeDtypeStruct((ntok, ZA_W), F32),
            jax.ShapeDtypeStruct((ntok, ZB_W), F32),
            jax.ShapeDtypeStruct((ntok, ZC_W), F32),
        ],
        compiler_params=pltpu.CompilerParams(
            dimension_semantics=("arbitrary",), vmem_limit_bytes=VMEM_LIMIT),
        name="in_projection",
    )(x2, mod, ng, wcat, bcat)


def _outproj_kernel(x_ref, mod_ref, ng_ref, y_ref, wmg_ref, bmg_ref, wa_ref, wb_ref, wc_ref,
                    wo_ref, o_ref):
    x = x_ref[...]
    h = _modulated_norm(x, mod_ref, ng_ref)
    merged = None
    for i, wbr_ref in enumerate((wa_ref, wb_ref, wc_ref)):
        cols = slice(i * D_MODEL, (i + 1) * D_MODEL)
        gate = _sigmoid(_dot(h, wmg_ref[:, cols]) + bmg_ref[:, cols])
        term = gate * _dot(y_ref[:, i * A_WIDTH:(i + 1) * A_WIDTH], wbr_ref[...])
        merged = term if merged is None else merged + term
    ada_gate = mod_ref[:, 2 * D_MODEL:3 * D_MODEL]
    o_ref[...] = x + ada_gate * _dot(merged.astype(BF16), wo_ref[...])


def _outproj_call(layer, x2, mod, ng, y, wmg, bmg, wa, wb, wc, wo, tokens_per_batch):
    ntok = x2.shape[0]
    tm = PROJ_TILE
    const = lambda i: (0, 0)
    once = pl.Buffered(1)
    return pl.pallas_call(
        _outproj_kernel,
        grid=(ntok // tm,),
        in_specs=[
            pl.BlockSpec((tm, D_MODEL), lambda i: (i, 0)),
            _mod_spec(tm, tokens_per_batch),
            pl.BlockSpec((1, D_MODEL), const),
            pl.BlockSpec((tm, Y_W), lambda i: (i, 0)),
            _layer_weight_spec(layer, D_MODEL, 3 * D_MODEL),
            pl.BlockSpec((1, 3 * D_MODEL), const),
            pl.BlockSpec((A_WIDTH, D_MODEL), const, pipeline_mode=once),
            pl.BlockSpec((B_WIDTH, D_MODEL), const, pipeline_mode=once),
            pl.BlockSpec((C_WIDTH, D_MODEL), const, pipeline_mode=once),
            pl.BlockSpec((D_MODEL, D_MODEL), const, pipeline_mode=once),
        ],
        out_specs=pl.BlockSpec((tm, D_MODEL), lambda i: (i, 0)),
        out_shape=jax.ShapeDtypeStruct((ntok, D_MODEL), F32),
        compiler_params=pltpu.CompilerParams(
            dimension_semantics=("arbitrary",), vmem_limit_bytes=VMEM_LIMIT),
        name="out_projection",
    )(x2, mod, ng, y, wmg, bmg, wa, wb, wc, wo)


def _place_q_head(qn, h, rows):
    lane = lax.broadcasted_iota(jnp.int32, (rows, LANES), 1)
    slab = qn[:, (h // 2) * LANES:(h // 2 + 1) * LANES]
    src_hi = h % 2
    dst_hi = h // (B_HEADS // B_KV_HEADS)
    keep = (lane >= B_HEAD_DIM) if src_hi else (lane < B_HEAD_DIM)
    slab = jnp.where(keep, slab, 0.0)
    if src_hi != dst_hi:
        slab = pltpu.roll(slab, B_HEAD_DIM, 1)
    return slab


def _merge_head_pair(o_even, o_odd, h_even, rows):
    lane_lo = lax.broadcasted_iota(jnp.int32, (rows, LANES), 1) < B_HEAD_DIM
    kv_hi = h_even // (B_HEADS // B_KV_HEADS)
    if kv_hi:
        o_even = pltpu.roll(o_even, B_HEAD_DIM, 1)
    else:
        o_odd = pltpu.roll(o_odd, B_HEAD_DIM, 1)
    return jnp.where(lane_lo, o_even, o_odd)


def _conv_taps(xbuf, cw_ref, cb_ref, cols, ts):
    y = cb_ref[:, cols]
    for j in range(C_CONV):
        lo = SUBLANES - (C_CONV - 1) + j
        y = y + cw_ref[j:j + 1, cols] * xbuf[lo:lo + ts, cols]
    return y


def _prompt_mix_kernel(sink_ref, za_ref, zb_ref, zc_ref, vg_ref, gw_ref, gbs_ref, qg_ref, kg_ref,
                       cw_ref, cb_ref, fb_ref, hg_ref, tril_ref, band_ref, tri01_ref, tribias_ref,
                       y_ref, ko_ref, vo_ref, convo_ref, c_ref, n_ref, m_ref,
                       kprev, vprev, xbuf, first_tile, pump):
    ts = PROMPT_TILE

    wts = [(gw_ref[gi] * tril_ref[...]).astype(BF16) for gi in range(A_GROUPS)]
    for c in range(ts // WINDOW):
        rows = slice(c * WINDOW, (c + 1) * WINDOW)
        vnb = (_rms(za_ref[rows, A_WIDTH:2 * A_WIDTH]) * vg_ref[...]).astype(BF16)
        s = jnp.concatenate(
            [_dot(wts[gi], vnb[:, gi * GROUP_DIM:(gi + 1) * GROUP_DIM]) + gbs_ref[:, gi:gi + 1]
             for gi in range(A_GROUPS)], axis=1)
        sg = _silu(za_ref[rows, 2 * A_WIDTH:3 * A_WIDTH])
        y_ref[rows, 0:A_WIDTH] = (za_ref[rows, 0:A_WIDTH] * s * sg).astype(BF16)
        pump()

    kn = _qk_norm(zb_ref[:, B_WIDTH:B_WIDTH + B_KV_WIDTH], kg_ref[...])
    vv = zb_ref[:, B_WIDTH + B_KV_WIDTH:B_WIDTH + 2 * B_KV_WIDTH]
    pump()
    grp = B_HEADS // B_KV_HEADS
    nblk = ts // WINDOW
    lane_lo2 = lax.broadcasted_iota(jnp.int32, (2 * WINDOW, LANES), 1) < B_HEAD_DIM
    kblocks = [kprev[...]] + [kn[b * WINDOW:(b + 1) * WINDOW] for b in range(nblk)]
    vblocks = [vprev[...]] + [vv[b * WINDOW:(b + 1) * WINDOW] for b in range(nblk)]
    heads = [(kh, g) for kh in range(B_KV_HEADS) for g in range(grp)]
    snk = {k: sink_ref[k[0] * grp + k[1]] for k in heads}
    for blk in range(nblk):
        rows = slice(blk * WINDOW, (blk + 1) * WINDOW)
        if blk == 0 and first_tile is not False:
            bias = jnp.where(first_tile, band_ref[1], band_ref[0])
        else:
            bias = band_ref[0]
        kcat = jnp.concatenate([kblocks[blk], kblocks[blk + 1]], axis=0)
        vcat = jnp.concatenate([vblocks[blk], vblocks[blk + 1]], axis=0)
        krol = pltpu.roll(kcat, B_HEAD_DIM, 1)
        vrol = pltpu.roll(vcat, B_HEAD_DIM, 1)
        kdup, vdup = [], []
        for kh in range(B_KV_HEADS):
            own = lane_lo2 if kh == 0 else jnp.logical_not(lane_lo2)
            kdup.append(jnp.where(own, kcat, krol).astype(BF16))
            vdup.append(jnp.where(own, vcat, vrol).astype(BF16))
        qn = _qk_norm(zb_ref[rows, 0:B_WIDTH], qg_ref[...]) * (B_HEAD_DIM ** -0.5)
        pump()
        qs = [jnp.concatenate([_place_q_head(qn, kh * grp + g, WINDOW) for g in range(grp)],
                              axis=0).astype(BF16) for kh in range(B_KV_HEADS)]
        logits = [_dot_nt(qs[kh], kdup[kh]) for kh in range(B_KV_HEADS)]
        pump()
        lg = {(kh, g): logits[kh][g * WINDOW:(g + 1) * WINDOW] + bias for kh, g in heads}
        mx = {k: jnp.maximum(jnp.max(lg[k], axis=-1, keepdims=True), snk[k]) for k in heads}
        pump()
        p = {k: jnp.exp(lg[k] - mx[k]) for k in heads}
        pump()
        rden = {k: 1.0 / (jnp.sum(p[k], axis=-1, keepdims=True) + jnp.exp(snk[k] - mx[k]))
                for k in heads}
        pump()
        pv = [_dot(jnp.concatenate([p[kh, g].astype(BF16) for g in range(grp)], axis=0), vdup[kh])
              for kh in range(B_KV_HEADS)]
        pump()
        outs = {(kh, g): pv[kh][g * WINDOW:(g + 1) * WINDOW] * rden[kh, g] for kh, g in heads}
        yb = jnp.concatenate(
            [_merge_head_pair(outs[(2 * j) // grp, (2 * j) % grp],
                              outs[(2 * j + 1) // grp, (2 * j + 1) % grp], 2 * j, WINDOW)
             for j in range(B_HEADS // 2)], axis=1)
        sgb = _silu(zb_ref[rows, B_WIDTH + 2 * B_KV_WIDTH:ZB_W])
        y_ref[rows, A_WIDTH:A_WIDTH + B_WIDTH] = (yb * sgb).astype(BF16)
        pump()
    kprev[...] = kblocks[nblk]
    vprev[...] = vblocks[nblk]
    ko_ref[...] = kblocks[nblk]
    vo_ref[...] = vblocks[nblk]
    pump()

    xbuf[SUBLANES:SUBLANES + ts, :] = zc_ref[:, 0:2 * C_WIDTH]
    pump()
    ifp = zc_ref[:, 5 * C_WIDTH:5 * C_WIDTH + LANES]
    lf = _log_sigmoid(ifp + fb_ref[...])
    pump()
    cl = MLSTM_CHUNK
    lane_c = lax.broadcasted_iota(jnp.int32, (cl, LANES), 1)
    lane_1 = lax.broadcasted_iota(jnp.int32, (1, LANES), 1)
    m_row = m_ref[...]
    m_out = m_row
    cum_all = _dot_exact01(tri01_ref[...], lf)
    st_col = jnp.where(lane_c < C_HEADS, ifp, cum_all)
    st_row = st_col.T
    tribias = tribias_ref[...]
    pump()
    for hds in MLSTM_HEAD_GROUPS:
        hs = {hd: slice(hd * C_HEAD_DIM, (hd + 1) * C_HEAD_DIM) for hd in hds}
        i_c = {hd: st_col[:, hd:hd + 1] for hd in hds}
        cum_c = {hd: st_col[:, C_HEADS + hd:C_HEADS + hd + 1] for hd in hds}
        i_r = {hd: st_row[hd:hd + 1, :] for hd in hds}
        cum_r = {hd: st_row[C_HEADS + hd:C_HEADS + hd + 1, :] for hd in hds}
        m_prev = {hd: m_row[:, hd:hd + 1] for hd in hds}
        dmat = {hd: cum_c[hd] - cum_r[hd] + i_r[hd] + tribias for hd in hds}
        m_inter = {hd: cum_c[hd] + m_prev[hd] for hd in hds}
        m_t = {hd: jnp.maximum(m_inter[hd], jnp.max(dmat[hd], axis=-1, keepdims=True)) for hd in hds}
        pump()
        q_h = {hd: _silu(_conv_taps(xbuf, cw_ref, cb_ref, hs[hd], ts)).astype(BF16) for hd in hds}
        k_h = {hd: _silu(_conv_taps(xbuf, cw_ref, cb_ref,
                                    slice(C_WIDTH + hs[hd].start, C_WIDTH + hs[hd].stop), ts))
               * (C_HEAD_DIM ** -0.5) for hd in hds}
        pump()
        v_h = {hd: zc_ref[:, 2 * C_WIDTH + hd * C_HEAD_DIM:2 * C_WIDTH + (hd + 1) * C_HEAD_DIM].astype(BF16)
               for hd in hds}
        s_qk = {hd: _dot_nt(q_h[hd], k_h[hd].astype(BF16)) for hd in hds}
        a = {hd: jnp.exp(dmat[hd] - m_t[hd]) * s_qk[hd] for hd in hds}
        pump()
        w_inter = {hd: jnp.exp(m_inter[hd] - m_t[hd]) for hd in hds}
        c_prev = {hd: c_ref[hd] for hd in hds}
        n_prev = {hd: n_ref[hd:hd + 1, :] for hd in hds}
        inter = {hd: _dot(q_h[hd], c_prev[hd].astype(BF16)) for hd in hds}
        intra = {hd: _dot(a[hd].astype(BF16), v_h[hd]) for hd in hds}
        pump()
        den = {hd: jnp.sum(a[hd], axis=-1, keepdims=True)
               + w_inter[hd] * jnp.sum(q_h[hd].astype(F32) * n_prev[hd], axis=-1, keepdims=True)
               for hd in hds}
        rnorm = {hd: 1.0 / jnp.maximum(jnp.abs(den[hd]), jnp.exp(-m_t[hd])) for hd in hds}
        hh = {hd: (intra[hd] + w_inter[hd] * inter[hd]) * rnorm[hd] for hd in hds}
        pump()
        for hd in hds:
            o_cols = slice(3 * C_WIDTH + hd * C_HEAD_DIM, 3 * C_WIDTH + (hd + 1) * C_HEAD_DIM)
            g_cols = slice(4 * C_WIDTH + hd * C_HEAD_DIM, 4 * C_WIDTH + (hd + 1) * C_HEAD_DIM)
            gate_o = _sigmoid(zc_ref[:, o_cols]) * _silu(zc_ref[:, g_cols])
            y_cols = slice(A_WIDTH + B_WIDTH + hd * C_HEAD_DIM, A_WIDTH + B_WIDTH + (hd + 1) * C_HEAD_DIM)
            y_ref[:, y_cols] = (_rms(hh[hd]) * hg_ref[:, hs[hd]] * gate_o).astype(BF16)
        pump()
        total = {hd: cum_r[hd][:, cl - 1:cl] for hd in hds}
        g_r = {hd: total[hd] - cum_r[hd] + i_r[hd] for hd in hds}
        g_c = {hd: total[hd] - cum_c[hd] + i_c[hd] for hd in hds}
        m_new = {hd: jnp.maximum(total[hd] + m_prev[hd], jnp.max(g_r[hd], axis=-1, keepdims=True))
                 for hd in hds}
        kw = {hd: jnp.exp(g_c[hd] - m_new[hd]) * k_h[hd] for hd in hds}
        decay = {hd: jnp.exp(total[hd] + m_prev[hd] - m_new[hd]) for hd in hds}
        pump()
        upd = {hd: _dot(kw[hd].T.astype(BF16), v_h[hd]) for hd in hds}
        for hd in hds:
            c_ref[hd] = decay[hd] * c_prev[hd] + upd[hd]
            n_ref[hd:hd + 1, :] = decay[hd] * n_prev[hd] + jnp.sum(kw[hd], axis=0, keepdims=True)
            m_out = jnp.where(lane_1 == hd, m_new[hd], m_out)
        pump()
    m_ref[...] = m_out
    tail = xbuf[ts:ts + SUBLANES, :]
    xbuf[0:SUBLANES, :] = tail
    convo_ref[...] = tail


def _prompt_mask_constants():
    r = np.arange(WINDOW)[:, None]
    c = np.arange(2 * WINDOW)[None, :]
    band = (c > r) & (c <= r + WINDOW)
    band_first = band & (c >= WINDOW)
    band_bias = np.where(np.stack([band, band_first]), 0.0, NEG).astype(np.float32)
    tril = (np.arange(WINDOW)[:, None] >= np.arange(WINDOW)[None, :]).astype(np.float32)
    tri = np.arange(MLSTM_CHUNK)[:, None] >= np.arange(MLSTM_CHUNK)[None, :]
    return (jnp.asarray(tril), jnp.asarray(band_bias), jnp.asarray(tri, dtype=BF16),
            jnp.asarray(np.where(tri, 0.0, NEG).astype(np.float32)))


N_MIX_PARAMS = 13
MLSTM_HEAD_GROUPS = ((0, 1), (2, 3))
MIX_PUMP_CALLS = 37
TAIL_FILL_PIECES = 8
MXU_PIECE_COLS = 256


class _Interleaver:
    def __init__(self, pieces, calls, hold_back=0):
        self._pieces = list(pieces)
        self._hold_back = hold_back
        self._spread = len(self._pieces) - hold_back
        self._emitted = 0
        self._calls = calls
        self._call = 0

    def __call__(self):
        self._call += 1
        target = (self._call * self._spread) // self._calls
        while self._emitted < target:
            self._pieces.pop(0)()
            self._emitted += 1

    def finish(self):
        assert self._call == self._calls and len(self._pieces) == self._hold_back, self._call
        return self._pieces


def _gate_pieces(h_ref, wmg_ref, bmg_ref, g_ref):
    def piece(off):
        cols = slice(off, off + MXU_PIECE_COLS)
        def run():
            g_ref[:, cols] = _sigmoid(_dot(h_ref[...], wmg_ref[:, cols]) + bmg_ref[:, cols])
        return run
    return [piece(off) for off in range(0, 3 * D_MODEL, MXU_PIECE_COLS)]


def _merge_and_project(x, mod_ref, g_ref, y_ref, wa_ref, wb_ref, wc_ref, wo_ref, fillers=()):
    fillers = list(fillers)
    per_stage = -(-len(fillers) // 4)
    merged = None
    for i, wbr_ref in enumerate((wa_ref, wb_ref, wc_ref)):
        for piece in fillers[i * per_stage:(i + 1) * per_stage]:
            piece()
        term = (g_ref[:, i * D_MODEL:(i + 1) * D_MODEL]
                * _dot(y_ref[:, i * A_WIDTH:(i + 1) * A_WIDTH], wbr_ref[...]))
        merged = term if merged is None else merged + term
    for piece in fillers[3 * per_stage:]:
        piece()
    ada_gate = mod_ref[:, 2 * D_MODEL:3 * D_MODEL]
    return x + ada_gate * _dot(merged.astype(BF16), wo_ref[...])


def _prompt_layer_kernel(tiles_per_seq, sink_ref, x2_ref, xn_ref, mod_ref, modn_ref, ng_ref,
                         wcat_ref, bcat_ref, *rest):
    mix_params = rest[:N_MIX_PARAMS]
    wmg_ref, bmg_ref, wa_ref, wb_ref, wc_ref, wo_ref = rest[N_MIX_PARAMS:N_MIX_PARAMS + 6]
    o_ref, ko_ref, vo_ref, convo_ref, c_ref, n_ref, m_ref = rest[N_MIX_PARAMS + 6:N_MIX_PARAMS + 13]
    (za0, zb0, zc0, za1, zb1, zc1, h0, h1, y_scr, g_scr, kprev, vprev, xbuf) = rest[N_MIX_PARAMS + 13:]
    ts = PROMPT_TILE
    z = ((za0, zb0, zc0), (za1, zb1, zc1))
    h = (h0, h1)
    k = pl.program_id(0)
    seq_start = (k % (tiles_per_seq // 2)) == 0

    @pl.when(k == 0)
    def _():
        h0[...] = _modulated_norm(x2_ref[0:ts, :], mod_ref, ng_ref)
        for piece in _inproj_pieces(lambda: h0[...], wcat_ref, bcat_ref, *z[0], 512):
            piece()

    @pl.when(seq_start)
    def _():
        kprev[...] = jnp.zeros_like(kprev)
        vprev[...] = jnp.zeros_like(vprev)
        xbuf[0:SUBLANES, :] = jnp.zeros((SUBLANES, 2 * C_WIDTH), F32)
        c_ref[...] = jnp.zeros_like(c_ref)
        n_ref[...] = jnp.zeros_like(n_ref)
        m_ref[...] = jnp.zeros_like(m_ref)

    for half in range(2):
        cur, nxt = half, 1 - half
        rows = slice(half * ts, (half + 1) * ts)
        if half == 0:
            h[nxt][...] = _modulated_norm(x2_ref[ts:2 * ts, :], mod_ref, ng_ref)
        else:
            h[nxt][...] = _modulated_norm(xn_ref[...], modn_ref, ng_ref)
        get_h_next = functools.partial(lambda r: r[...], h[nxt])
        hold = TAIL_FILL_PIECES if half == 1 else 0
        proj = _inproj_pieces(get_h_next, wcat_ref, bcat_ref, *z[nxt], MXU_PIECE_COLS)
        pump = _Interleaver(
            proj[:len(proj) - hold] + _gate_pieces(h[cur], wmg_ref, bmg_ref, g_scr)
            + proj[len(proj) - hold:], MIX_PUMP_CALLS, hold_back=hold)
        _prompt_mix_kernel(sink_ref, *z[cur], *mix_params,
                           y_scr, ko_ref, vo_ref, convo_ref, c_ref, n_ref, m_ref, kprev, vprev, xbuf,
                           first_tile=seq_start if half == 0 else False, pump=pump)
        o_ref[rows, :] = _merge_and_project(x2_ref[rows, :], mod_ref, g_scr, y_scr,
                                            wa_ref, wb_ref, wc_ref, wo_ref, fillers=pump.finish())


def _prompt_layer_call(layer, x2, mod, lw, batch, seq):
    ts = PROMPT_TILE
    nt = seq // ts
    assert nt % 2 == 0
    last_tile = batch * nt - 1
    const2 = lambda k: (0, 0)
    const3 = lambda k: (0, 0, 0)
    per_b3 = lambda k: ((2 * k) // nt, 0, 0)
    next_tile = lambda k: jnp.minimum(2 * k + 2, last_tile)
    once = pl.Buffered(1)
    return pl.pallas_call(
        functools.partial(_prompt_layer_kernel, nt),
        grid=(batch * nt // 2,),
        in_specs=[
            pl.BlockSpec(memory_space=pltpu.SMEM),
            pl.BlockSpec((2 * ts, D_MODEL), lambda k: (k, 0)),
            pl.BlockSpec((ts, D_MODEL), lambda k: (next_tile(k), 0)),
            pl.BlockSpec((None, 1, 3 * D_MODEL), per_b3),
            pl.BlockSpec((None, 1, 3 * D_MODEL), lambda k: (next_tile(k) // nt, 0, 0)),
            pl.BlockSpec((1, D_MODEL), const2),
            _layer_weight_spec(layer, D_MODEL, ZCAT_W),
            pl.BlockSpec((1, ZCAT_W), const2),
            pl.BlockSpec((1, A_WIDTH), const2),
            pl.BlockSpec((A_GROUPS, WINDOW, WINDOW), const3),
            pl.BlockSpec((WINDOW, LANES), const2),
            pl.BlockSpec((1, B_WIDTH), const2),
            pl.BlockSpec((1, B_KV_WIDTH), const2),
            pl.BlockSpec((C_CONV, 2 * C_WIDTH), const2),
            pl.BlockSpec((1, 2 * C_WIDTH), const2),
            pl.BlockSpec((1, LANES), const2),
            pl.BlockSpec((1, C_WIDTH), const2),
            pl.BlockSpec((WINDOW, WINDOW), const2),
            pl.BlockSpec((2, WINDOW, 2 * WINDOW), const3),
            pl.BlockSpec((MLSTM_CHUNK, MLSTM_CHUNK), const2),
            pl.BlockSpec((MLSTM_CHUNK, MLSTM_CHUNK), const2),
            _layer_weight_spec(layer, D_MODEL, 3 * D_MODEL),
            pl.BlockSpec((1, 3 * D_MODEL), const2),
            pl.BlockSpec((A_WIDTH, D_MODEL), const2, pipeline_mode=once),
            pl.BlockSpec((B_WIDTH, D_MODEL), const2, pipeline_mode=once),
            pl.BlockSpec((C_WIDTH, D_MODEL), const2, pipeline_mode=once),
            pl.BlockSpec((D_MODEL, D_MODEL), const2, pipeline_mode=once),
        ],
        out_specs=[
            pl.BlockSpec((2 * ts, D_MODEL), lambda k: (k, 0)),
            pl.BlockSpec((None, WINDOW, B_KV_WIDTH), per_b3),
            pl.BlockSpec((None, WINDOW, B_KV_WIDTH), per_b3),
            pl.BlockSpec((None, SUBLANES, 2 * C_WIDTH), per_b3),
            pl.BlockSpec((None, C_HEADS, C_HEAD_DIM, C_HEAD_DIM), lambda k: ((2 * k) // nt, 0, 0, 0)),
            pl.BlockSpec((None, C_HEADS, C_HEAD_DIM), per_b3),
            pl.BlockSpec((None, 1, LANES), per_b3),
        ],
        out_shape=[
            jax.ShapeDtypeStruct((batch * seq, D_MODEL), F32),
            jax.ShapeDtypeStruct((batch, WINDOW, B_KV_WIDTH), F32),
            jax.ShapeDtypeStruct((batch, WINDOW, B_KV_WIDTH), F32),
            jax.ShapeDtypeStruct((batch, SUBLANES, 2 * C_WIDTH), F32),
            jax.ShapeDtypeStruct((batch, C_HEADS, C_HEAD_DIM, C_HEAD_DIM), F32),
            jax.ShapeDtypeStruct((batch, C_HEADS, C_HEAD_DIM), F32),
            jax.ShapeDtypeStruct((batch, 1, LANES), F32),
        ],
        scratch_shapes=(
            [pltpu.VMEM((ts, w), F32) for w in (ZA_W, ZB_W, ZC_W)] * 2
            + [pltpu.VMEM((ts, D_MODEL), BF16)] * 2
            + [pltpu.VMEM((ts, Y_W), BF16),
               pltpu.VMEM((ts, 3 * D_MODEL), F32),
               pltpu.VMEM((WINDOW, B_KV_WIDTH), F32),
               pltpu.VMEM((WINDOW, B_KV_WIDTH), F32),
               pltpu.VMEM((ts + SUBLANES, 2 * C_WIDTH), F32)]),
        compiler_params=pltpu.CompilerParams(
            dimension_semantics=("arbitrary",), vmem_limit_bytes=VMEM_LIMIT),
        name="prompt_layer",
    )(lw["sinks"], x2, x2, mod, mod, lw["ng"], lw["wcat"], lw["bcat"],
      lw["vg"], lw["gws"], lw["gbs_col"], lw["qg"], lw["kg"], lw["cw"], lw["cb"], lw["fb"], lw["hg"],
      *_prompt_mask_constants(),
      lw["wmg"], lw["bmg"], lw["wa"], lw["wb"], lw["wc"], lw["wo"])


def _sample_mix_kernel(sink_ref, za_ref, zb_ref, zc_ref, kc_ref, vc_ref, cs_ref, c0_ref, n0_ref,
                       m0_ref, vg_ref, gwb_ref, gbs_ref, qg_ref, kg_ref, cw_ref, cb_ref, fb_ref,
                       hg_ref,
                       y_ref, vrow_ref, ko_ref, vo_ref, convo_ref, c1_ref, n1_ref, m1_ref,
                       xbuf, first_layer=None):
    nb = SAMPLE_NB
    t = SUBLANES
    rows = nb * t
    if first_layer is not None:
        for other in range(DEPTH):
            if other != first_layer:
                c1_ref[other] = jnp.zeros(c1_ref.shape[1:], F32)
        c1_ref = c1_ref.at[first_layer]
    tok_r = lax.broadcasted_iota(jnp.int32, (rows, rows), 0)
    tok_c = lax.broadcasted_iota(jnp.int32, (rows, rows), 1)
    same_b = (tok_r // t) == (tok_c // t)
    causal_b = same_b & (tok_c <= tok_r)

    u = za_ref[:, 0:A_WIDTH]
    vn = _rms(za_ref[:, A_WIDTH:2 * A_WIDTH]) * vg_ref[...]
    sg = _silu(za_ref[:, 2 * A_WIDTH:3 * A_WIDTH])
    vrow_ref[...] = vn
    vnb = vn.astype(BF16)
    s_cols = []
    for gi in range(A_GROUPS):
        s_cols.append(_dot(gwb_ref[gi], vnb[:, gi * GROUP_DIM:(gi + 1) * GROUP_DIM])
                      + gbs_ref[:, gi:gi + 1])
    y_ref[:, 0:A_WIDTH] = (u * jnp.concatenate(s_cols, axis=1) * sg).astype(BF16)

    qn = _qk_norm(zb_ref[:, 0:B_WIDTH], qg_ref[...]) * (B_HEAD_DIM ** -0.5)
    kn = _qk_norm(zb_ref[:, B_WIDTH:B_WIDTH + B_KV_WIDTH], kg_ref[...])
    vv = zb_ref[:, B_WIDTH + B_KV_WIDTH:B_WIDTH + 2 * B_KV_WIDTH]
    sgb = _silu(zb_ref[:, B_WIDTH + 2 * B_KV_WIDTH:ZB_W])
    kn3 = kn.reshape(nb, t, B_KV_WIDTH)
    vv3 = vv.reshape(nb, t, B_KV_WIDTH)
    kcache = kc_ref[...]
    vcache = vc_ref[...]
    pad = jnp.zeros((nb, WINDOW - t, B_KV_WIDTH), F32)
    kall = jnp.concatenate([kcache, kn3, pad], axis=1).astype(BF16)
    vall = jnp.concatenate([vcache, vv3, pad], axis=1).astype(BF16)
    qp = jnp.concatenate([_place_q_head(qn, h, rows).reshape(nb, t, LANES) for h in range(B_HEADS)],
                         axis=1).astype(BF16)
    logits = lax.dot_general(qp, kall, (((2,), (2,)), ((0,), (0,))), preferred_element_type=F32)
    qrow = lax.broadcasted_iota(jnp.int32, (nb, B_HEADS * t, 2 * WINDOW), 1)
    kcol = lax.broadcasted_iota(jnp.int32, (nb, B_HEADS * t, 2 * WINDOW), 2)
    qt = qrow % t
    valid = ((kcol < WINDOW) & (kcol > qt)) | ((kcol >= WINDOW) & ((kcol - WINDOW) <= qt))
    hrow = lax.broadcasted_iota(jnp.int32, (B_HEADS * t, 1), 0) // t
    snk = jnp.zeros((B_HEADS * t, 1), F32)
    for h in range(B_HEADS):
        snk = jnp.where(hrow == h, sink_ref[h], snk)
    lg = jnp.where(valid, logits, NEG)
    mx = jnp.maximum(jnp.max(lg, axis=-1, keepdims=True), snk[None])
    p = jnp.exp(lg - mx)
    den = jnp.sum(p, axis=-1, keepdims=True) + jnp.exp(snk[None] - mx)
    pv = lax.dot_general(p.astype(BF16), vall, (((2,), (1,)), ((0,), (0,))),
                         preferred_element_type=F32) / den
    head_out = [pv[:, h * t:(h + 1) * t, :].reshape(rows, LANES) for h in range(B_HEADS)]
    yb = jnp.concatenate(
        [_merge_head_pair(head_out[2 * j], head_out[2 * j + 1], 2 * j, rows)
         for j in range(B_HEADS // 2)], axis=1)
    y_ref[:, A_WIDTH:A_WIDTH + B_WIDTH] = (yb * sgb).astype(BF16)
    ko_ref[...] = jnp.concatenate([kcache[:, t:, :], kn3], axis=1)
    vo_ref[...] = jnp.concatenate([vcache[:, t:, :], vv3], axis=1)

    xbuf[:, SUBLANES - (C_CONV - 1):SUBLANES, :] = cs_ref[...]
    xbuf[:, SUBLANES:2 * SUBLANES, :] = zc_ref[:, 0:2 * C_WIDTH].reshape(nb, t, 2 * C_WIDTH)
    y3 = cb_ref[...][None]
    for j in range(C_CONV):
        lo = SUBLANES - (C_CONV - 1) + j
        y3 = y3 + cw_ref[j:j + 1, :][None] * xbuf[:, lo:lo + t, :]
    convo_ref[...] = xbuf[:, 2 * SUBLANES - (C_CONV - 1):2 * SUBLANES, :]
    qk = _silu(y3.reshape(rows, 2 * C_WIDTH))
    qall = qk[:, 0:C_WIDTH].astype(BF16)
    kall_c = qk[:, C_WIDTH:2 * C_WIDTH] * (C_HEAD_DIM ** -0.5)
    vall_c = zc_ref[:, 2 * C_WIDTH:3 * C_WIDTH].astype(BF16)
    gate_o = _sigmoid(zc_ref[:, 3 * C_WIDTH:4 * C_WIDTH]) * _silu(zc_ref[:, 4 * C_WIDTH:5 * C_WIDTH])
    ifp = zc_ref[:, 5 * C_WIDTH:5 * C_WIDTH + LANES]
    lf = _log_sigmoid(ifp + fb_ref[...])
    lane_t = lax.broadcasted_iota(jnp.int32, (rows, LANES), 1)
    cum_all = _dot_exact01(jnp.where(causal_b, 1.0, 0.0).astype(BF16), lf)
    tot_all = _dot_exact01(jnp.where(same_b, 1.0, 0.0).astype(BF16), lf)
    st_col = jnp.where(lane_t < C_HEADS, ifp, cum_all)
    st_row = st_col.T
    tot_row = tot_all.T
    m0 = m0_ref[...]
    same_b_bf = jnp.where(same_b, 1.0, 0.0).astype(BF16)
    batch_of_lane = lax.broadcasted_iota(jnp.int32, (nb, 1, rows), 2) // t
    batch_id = lax.broadcasted_iota(jnp.int32, (nb, 1, rows), 0)
    own_tok = batch_of_lane == batch_id
    h_cols = []
    m_out = jnp.zeros((rows, LANES), F32)
    for hd in range(C_HEADS):
        hs = slice(hd * C_HEAD_DIM, (hd + 1) * C_HEAD_DIM)
        i_c = st_col[:, hd:hd + 1]
        cum_c = st_col[:, C_HEADS + hd:C_HEADS + hd + 1]
        tot_c = tot_all[:, C_HEADS + hd:C_HEADS + hd + 1]
        i_r = st_row[hd:hd + 1, :]
        cum_r = st_row[C_HEADS + hd:C_HEADS + hd + 1, :]
        tot_r = tot_row[C_HEADS + hd:C_HEADS + hd + 1, :]
        m_prev = m0[:, hd:hd + 1]
        dmat = jnp.where(causal_b, cum_c - cum_r + i_r, NEG)
        m_inter = cum_c + m_prev
        m_t = jnp.maximum(m_inter, jnp.max(dmat, axis=-1, keepdims=True))
        q_h = qall[:, hs]
        k_h = kall_c[:, hs]
        v_h = vall_c[:, hs]
        a = jnp.exp(dmat - m_t) * _dot_nt(q_h, k_h.astype(BF16))
        w_inter = jnp.exp(m_inter - m_t)
        c_prev = c0_ref[:, hd]
        n_tok = jnp.broadcast_to(n0_ref[hd][:, None, :], (nb, t, C_HEAD_DIM)).reshape(rows, C_HEAD_DIM)
        inter = lax.dot_general(q_h.reshape(nb, t, C_HEAD_DIM), c_prev.astype(BF16),
                                (((2,), (1,)), ((0,), (0,))), preferred_element_type=F32)
        num = _dot(a.astype(BF16), v_h) + w_inter * inter.reshape(rows, C_HEAD_DIM)
        den = (jnp.sum(a, axis=-1, keepdims=True)
               + w_inter * jnp.sum(q_h.astype(F32) * n_tok, axis=-1, keepdims=True))
        hh = num / jnp.maximum(jnp.abs(den), jnp.exp(-m_t))
        h_cols.append(_rms(hh))
        g_r = tot_r - cum_r + i_r
        g_c = tot_c - cum_c + i_c
        m_new = jnp.maximum(tot_c + m_prev,
                            jnp.max(jnp.where(same_b, g_r, NEG), axis=-1, keepdims=True))
        kw = jnp.exp(g_c - m_new) * k_h
        decay = jnp.exp(tot_c + m_prev - m_new)
        kwt = kw.T
        lhs = jnp.where(own_tok, kwt[None], 0.0).astype(BF16).reshape(nb * C_HEAD_DIM, rows)
        upd = _dot(lhs, v_h).reshape(nb, C_HEAD_DIM, C_HEAD_DIM)
        dec_b = jnp.broadcast_to(decay, (rows, C_HEAD_DIM)).reshape(nb, t, C_HEAD_DIM)[:, 0:1, :]
        c1_ref[:, hd] = dec_b * c_prev + upd
        n1_ref[hd] = decay * n_tok + _dot(same_b_bf, kw.astype(BF16))
        m_out = jnp.where(lane_t == hd, m_new, m_out)
    m1_ref[...] = m_out
    hn = jnp.concatenate(h_cols, axis=1) * hg_ref[...]
    y_ref[:, A_WIDTH + B_WIDTH:Y_W] = (hn * gate_o).astype(BF16)


def _sample_mix_call(l, za, zb, zc, kc, vc, cs, c0, n0t, m0tok, lw, nbatch, c1_all=None):
    nb = SAMPLE_NB
    t = SUBLANES
    rows = nb * t
    tok = lambda i: (i, 0)
    const2 = lambda i: (0, 0)
    const3 = lambda i: (0, 0, 0)
    b3 = lambda i: (i, 0, 0)
    lb4 = lambda i: (l, i, 0, 0)
    operands = [lw["sinks"], za, zb, zc, kc, vc, cs, c0, n0t, m0tok, lw["vg"], lw["gwb"],
                lw["gbs_tok"], lw["qg"], lw["kg"], lw["cw"], lw["cb"], lw["fb"], lw["hg"]]
    c_block = (nb, C_HEADS, C_HEAD_DIM, C_HEAD_DIM)
    if c1_all is None:
        kernel_fn, extra_specs, aliases = functools.partial(_sample_mix_kernel, first_layer=l), [], {}
        c1_spec = pl.BlockSpec((DEPTH,) + c_block, lambda i: (0, i, 0, 0, 0))
    else:
        n_in = len(operands)
        operands.append(c1_all)
        extra_specs = [pl.BlockSpec(memory_space=pl.ANY)]
        aliases = {n_in: 5}
        kernel_fn = lambda *refs: _sample_mix_kernel(*refs[:n_in], *refs[n_in + 1:])
        c1_spec = pl.BlockSpec((None,) + c_block, lambda i: (l, i, 0, 0, 0))
    return pl.pallas_call(
        kernel_fn,
        grid=(nbatch // nb,),
        input_output_aliases=aliases,
        in_specs=[
            pl.BlockSpec(memory_space=pltpu.SMEM),
            pl.BlockSpec((rows, ZA_W), tok),
            pl.BlockSpec((rows, ZB_W), tok),
            pl.BlockSpec((rows, ZC_W), tok),
            pl.BlockSpec((None, nb, WINDOW, B_KV_WIDTH), lb4),
            pl.BlockSpec((None, nb, WINDOW, B_KV_WIDTH), lb4),
            pl.BlockSpec((None, nb, C_CONV - 1, 2 * C_WIDTH), lb4),
            pl.BlockSpec((None, nb, C_HEADS, C_HEAD_DIM, C_HEAD_DIM), lambda i: (l, i, 0, 0, 0)),
            pl.BlockSpec((None, C_HEADS, nb, C_HEAD_DIM), lambda i: (l, 0, i, 0)),
            pl.BlockSpec((None, rows, LANES), lambda i: (l, i, 0)),
            pl.BlockSpec((1, A_WIDTH), const2),
            pl.BlockSpec((A_GROUPS, rows, rows), const3),
            pl.BlockSpec((rows, LANES), const2),
            pl.BlockSpec((1, B_WIDTH), const2),
            pl.BlockSpec((1, B_KV_WIDTH), const2),
            pl.BlockSpec((C_CONV, 2 * C_WIDTH), const2),
            pl.BlockSpec((1, 2 * C_WIDTH), const2),
            pl.BlockSpec((1, LANES), const2),
            pl.BlockSpec((1, C_WIDTH), const2),
        ] + extra_specs,
        out_specs=[
            pl.BlockSpec((rows, Y_W), tok),
            pl.BlockSpec((rows, A_WIDTH), tok),
            pl.BlockSpec((nb, WINDOW, B_KV_WIDTH), b3),
            pl.BlockSpec((nb, WINDOW, B_KV_WIDTH), b3),
            pl.BlockSpec((nb, C_CONV - 1, 2 * C_WIDTH), b3),
            c1_spec,
            pl.BlockSpec((C_HEADS, rows, C_HEAD_DIM), lambda i: (0, i, 0)),
            pl.BlockSpec((rows, LANES), tok),
        ],
        out_shape=[
            jax.ShapeDtypeStruct((nbatch * t, Y_W), BF16),
            jax.ShapeDtypeStruct((nbatch * t, A_WIDTH), F32),
            jax.ShapeDtypeStruct((nbatch, WINDOW, B_KV_WIDTH), F32),
            jax.ShapeDtypeStruct((nbatch, WINDOW, B_KV_WIDTH), F32),
            jax.ShapeDtypeStruct((nbatch, C_CONV - 1, 2 * C_WIDTH), F32),
            jax.ShapeDtypeStruct((DEPTH, nbatch, C_HEADS, C_HEAD_DIM, C_HEAD_DIM), F32),
            jax.ShapeDtypeStruct((C_HEADS, nbatch * t, C_HEAD_DIM), F32),
            jax.ShapeDtypeStruct((nbatch * t, LANES), F32),
        ],
        scratch_shapes=[pltpu.VMEM((nb, 2 * SUBLANES, 2 * C_WIDTH), F32)],
        compiler_params=pltpu.CompilerParams(
            dimension_semantics=("arbitrary",), vmem_limit_bytes=VMEM_LIMIT),
        name="sample_mixer",
    )(*operands)


def _layer_weights(l, wcat_all, wmg_all, b_in, gmlp_vnorm_g, gmlp_ws, gmlp_bs, swa_qnorm_g,
                   swa_knorm_g, swa_sinks, mlstm_conv_w, mlstm_conv_b, mlstm_f_bias, mlstm_hnorm_g,
                   w_branch_a, w_branch_b, w_branch_c, w_out, norm_g, dec_seq):
    bl = b_in[l]
    bcat = jnp.concatenate([bl[:COL_CI], bl[COL_CO:COL_MG], bl[COL_CI:COL_CO],
                            jnp.zeros((LANES - 2 * C_HEADS,), F32)])
    t = dec_seq
    nb = SAMPLE_NB
    ws_t = gmlp_ws[l][:, :t, :t] * jnp.tril(jnp.ones((t, t), F32))
    eye = jnp.eye(nb, dtype=F32)
    gwb = jnp.einsum("bc,gts->gbtcs", eye, ws_t).reshape(A_GROUPS, nb * t, nb * t).astype(BF16)
    gbs_col = jnp.pad(gmlp_bs[l].T, ((0, 0), (0, LANES - A_GROUPS)))
    gbs_tok = jnp.pad(jnp.tile(gmlp_bs[l][:, :t].T, (nb, 1)), ((0, 0), (0, LANES - A_GROUPS)))
    fb = jnp.pad(mlstm_f_bias[l], (C_HEADS, LANES - 2 * C_HEADS)).reshape(1, LANES)
    return dict(
        ng=norm_g[l].reshape(1, D_MODEL),
        wcat=wcat_all, bcat=bcat.reshape(1, ZCAT_W),
        wmg=wmg_all, bmg=bl[COL_MG:].reshape(1, 3 * D_MODEL),
        wa=w_branch_a[l].astype(BF16), wb=w_branch_b[l].astype(BF16),
        wc=w_branch_c[l].astype(BF16), wo=w_out[l].astype(BF16),
        vg=gmlp_vnorm_g[l].reshape(1, A_WIDTH), gws=gmlp_ws[l], gwb=gwb,
        gbs_col=gbs_col, gbs_tok=gbs_tok,
        qg=jnp.tile(swa_qnorm_g[l], B_HEADS).reshape(1, B_WIDTH),
        kg=jnp.tile(swa_knorm_g[l], B_KV_HEADS).reshape(1, B_KV_WIDTH),
        sinks=swa_sinks[l],
        cw=mlstm_conv_w[l], cb=mlstm_conv_b[l].reshape(1, 2 * C_WIDTH), fb=fb,
        hg=mlstm_hnorm_g[l].reshape(1, C_WIDTH),
    )


def kernel(x_prompt, x_sample, cache_swa_k, cache_swa_v, state_mlstm_conv, state_mlstm_C, state_mlstm_n, state_mlstm_m, c_prompt, c_sample, ada_w, ada_b, norm_g, w_in, b_in, gmlp_vnorm_g, gmlp_ws, gmlp_bs, swa_qnorm_g, swa_knorm_g, swa_sinks, mlstm_conv_w, mlstm_conv_b, mlstm_f_bias, mlstm_hnorm_g, w_branch_a, w_branch_b, w_branch_c, w_out):
    batch, seq, _ = x_prompt.shape
    nbatch, dec_seq, _ = x_sample.shape
    assert dec_seq == SUBLANES and seq % PROMPT_TILE == 0 and nbatch % SAMPLE_NB == 0
    assert seq % PROJ_TILE == 0 and (nbatch * dec_seq) % PROJ_TILE == 0
    wb_len = cache_swa_k.shape[2]
    assert wb_len == WINDOW

    nc = batch + nbatch
    nc_pad = -(-nc // SUBLANES) * SUBLANES
    c_all = jnp.concatenate([c_prompt, c_sample, jnp.zeros((nc_pad - nc, D_MODEL), F32)], axis=0)
    mod_all = _ada_call(c_all, ada_w, ada_b)

    xp = x_prompt.reshape(batch * seq, D_MODEL)
    xs = x_sample.reshape(nbatch * dec_seq, D_MODEL)
    kc_all = cache_swa_k.reshape(DEPTH, nbatch, WINDOW, B_KV_WIDTH)
    vc_all = cache_swa_v.reshape(DEPTH, nbatch, WINDOW, B_KV_WIDTH)
    n0t_all = jnp.transpose(state_mlstm_n, (0, 2, 1, 3))
    m0tok_all = jnp.pad(jnp.repeat(state_mlstm_m, dec_seq, axis=1),
                        ((0, 0), (0, 0), (0, LANES - C_HEADS)))
    wcat_all, wmg_all = _weight_prep_call(w_in)
    outs_p = [[] for _ in range(6)]
    outs_s = [[] for _ in range(6)]
    vrows = []
    c1_all = None
    for l in range(DEPTH):
        lw = _layer_weights(l, wcat_all, wmg_all, b_in, gmlp_vnorm_g, gmlp_ws, gmlp_bs, swa_qnorm_g,
                            swa_knorm_g, swa_sinks, mlstm_conv_w, mlstm_conv_b, mlstm_f_bias,
                            mlstm_hnorm_g, w_branch_a, w_branch_b, w_branch_c, w_out, norm_g,
                            dec_seq)
        mod_p = mod_all[l, :batch].reshape(batch, 1, 3 * D_MODEL)
        mod_s = jnp.repeat(mod_all[l, batch:nc], dec_seq, axis=0)

        xp, ko, vo, convo, c1, n1, m1 = _prompt_layer_call(l, xp, mod_p, lw, batch, seq)
        outs_p[0].append(ko.reshape(batch, WINDOW, B_KV_HEADS, B_HEAD_DIM))
        outs_p[1].append(vo.reshape(batch, WINDOW, B_KV_HEADS, B_HEAD_DIM))
        outs_p[2].append(convo[:, SUBLANES - (C_CONV - 1):, :])
        outs_p[3].append(c1)
        outs_p[4].append(n1)
        outs_p[5].append(m1[:, 0, :C_HEADS])

        za, zb, zc = _inproj_call(l, xs, mod_s, lw["ng"], lw["wcat"], lw["bcat"], None)
        y, vrow, ko, vo, convo, c1_all, n1tok, m1tok = _sample_mix_call(
            l, za, zb, zc, kc_all, vc_all, state_mlstm_conv, state_mlstm_C, n0t_all, m0tok_all,
            lw, nbatch, c1_all)
        xs = _outproj_call(l, xs, mod_s, lw["ng"], y, lw["wmg"], lw["bmg"], lw["wa"], lw["wb"],
                           lw["wc"], lw["wo"], None)
        outs_s[0].append(ko.reshape(nbatch, WINDOW, B_KV_HEADS, B_HEAD_DIM))
        outs_s[1].append(vo.reshape(nbatch, WINDOW, B_KV_HEADS, B_HEAD_DIM))
        outs_s[2].append(convo)
        outs_s[4].append(jnp.transpose(n1tok[:, ::dec_seq, :], (1, 0, 2)))
        outs_s[5].append(m1tok[::dec_seq, :C_HEADS])
        vrows.append(vrow.reshape(nbatch, dec_seq, A_WIDTH))

    sp = [jnp.stack(o) for o in outs_p]
    ss = [jnp.stack(o) if o else None for o in outs_s]
    return (xp.reshape(batch, seq, D_MODEL), xs.reshape(nbatch, dec_seq, D_MODEL),
            sp[0], sp[1], sp[2], sp[3], sp[4], sp[5],
            ss[0], ss[1], ss[2], c1_all, ss[4], ss[5], jnp.stack(vrows))
```

```python
import functools

import numpy as np
import jax
import jax.numpy as jnp
from jax import lax
from jax.experimental import pallas as pl
from jax.experimental.pallas import tpu as pltpu

F32 = jnp.float32
BF16 = jnp.bfloat16

D_MODEL = 1024
DEPTH = 2
A_WIDTH = 512
A_GROUPS = 4
GROUP_DIM = 128
B_HEADS = 8
B_KV_HEADS = 2
B_HEAD_DIM = 64
B_WIDTH = 512
B_KV_WIDTH = 128
WINDOW = 128
C_HEADS = 4
C_HEAD_DIM = 128
C_WIDTH = 512
C_CONV = 4
EPS = 1e-6
NEG = -1e30

LANES = 128
SUBLANES = 8
VMEM_LIMIT = 56 * 1024 * 1024

ZA_W = 3 * A_WIDTH
ZB_W = 2 * B_WIDTH + 2 * B_KV_WIDTH
ZC_W = 2 * C_WIDTH + 3 * C_WIDTH + LANES
ZCAT_W = ZA_W + ZB_W + ZC_W
Y_W = A_WIDTH + B_WIDTH + C_WIDTH

PROMPT_TILE = 256
MLSTM_CHUNK = PROMPT_TILE
SAMPLE_NB = 16
PROJ_TILE = 512


def _sigmoid(x):
    return 0.5 * jnp.tanh(0.5 * x) + 0.5


def _silu(x):
    t = 0.5 * x
    return t * (jnp.tanh(t) + 1.0)


def _log_sigmoid(x):
    return jnp.minimum(x, 0.0) - jnp.log1p(jnp.exp(-jnp.abs(x)))


def _rms(x):
    return x * lax.rsqrt(jnp.mean(x * x, axis=-1, keepdims=True) + EPS)


def _dot(a, b):
    return jnp.dot(a, b, preferred_element_type=F32)


def _dot_nt(a, b):
    return lax.dot_general(a, b, (((1,), (1,)), ((), ())), preferred_element_type=F32)


def _dot_exact01(m01, x):
    hi = x.astype(BF16)
    r1 = x - hi.astype(F32)
    mid = r1.astype(BF16)
    lo = (r1 - mid.astype(F32)).astype(BF16)
    return _dot(m01, hi) + _dot(m01, mid) + _dot(m01, lo)


def _modulated_norm(x, mod_ref, ng_ref):
    xn = _rms(x) * ng_ref[...]
    shift = mod_ref[:, 0:D_MODEL]
    scale = mod_ref[:, D_MODEL:2 * D_MODEL]
    return (xn * (1.0 + scale) + shift).astype(BF16)


def _head_rms_scale(x2, lane_lo):
    s0 = jnp.sum(jnp.where(lane_lo, x2, 0.0), axis=-1, keepdims=True)
    s1 = jnp.sum(jnp.where(lane_lo, 0.0, x2), axis=-1, keepdims=True)
    r0 = lax.rsqrt(s0 * (1.0 / B_HEAD_DIM) + EPS)
    r1 = lax.rsqrt(s1 * (1.0 / B_HEAD_DIM) + EPS)
    return jnp.where(lane_lo, r0, r1)


def _qk_norm(x, g_row):
    rows, width = x.shape
    lane_lo = lax.broadcasted_iota(jnp.int32, (rows, LANES), 1) < B_HEAD_DIM
    outs = []
    for j in range(width // LANES):
        slab = x[:, j * LANES:(j + 1) * LANES]
        outs.append(slab * _head_rms_scale(slab * slab, lane_lo))
    y = outs[0] if len(outs) == 1 else jnp.concatenate(outs, axis=1)
    return y * g_row


def _ada_kernel(c_ref, w_ref, b_ref, o_ref):
    c = c_ref[...]
    o_ref[...] = _dot(_silu(c).astype(BF16), w_ref[...].astype(BF16)) + b_ref[...]


def _ada_call(c_all, ada_w, ada_b):
    rows = c_all.shape[0]
    return pl.pallas_call(
        _ada_kernel,
        grid=(DEPTH, 3),
        in_specs=[
            pl.BlockSpec((rows, D_MODEL), lambda l, j: (0, 0)),
            pl.BlockSpec((None, D_MODEL, D_MODEL), lambda l, j: (l, 0, j)),
            pl.BlockSpec((None, 1, D_MODEL), lambda l, j: (l, 0, j)),
        ],
        out_specs=pl.BlockSpec((None, rows, D_MODEL), lambda l, j: (l, 0, j)),
        out_shape=jax.ShapeDtypeStruct((DEPTH, rows, 3 * D_MODEL), F32),
        compiler_params=pltpu.CompilerParams(
            dimension_semantics=("arbitrary", "arbitrary"), vmem_limit_bytes=VMEM_LIMIT),
        name="adaln_mod",
    )(c_all, ada_w, ada_b.reshape(DEPTH, 1, 3 * D_MODEL))


COL_CI = ZA_W + ZB_W + 3 * C_WIDTH
COL_CO = COL_CI + 2 * C_HEADS
COL_MG = COL_CO + 2 * C_WIDTH
PREP_CHUNK = 256
PREP_SHIFT = 2 * C_HEADS
N_MAIN = COL_CI // PREP_CHUNK
N_CO = (2 * C_WIDTH) // PREP_CHUNK
N_MG = (3 * D_MODEL) // PREP_CHUNK
J_CIF = N_MAIN + N_CO
J_MG = J_CIF + 1


def _weight_prep_kernel(wa_ref, wb_ref, wcat_ref, wmg_ref):
    j = pl.program_id(1)

    def shifted_t():
        rows = jnp.concatenate([wa_ref[PREP_SHIFT:PREP_CHUNK, :], wb_ref[...]], axis=0)
        return rows.astype(BF16).T

    @pl.when(j < N_MAIN)
    def _():
        wcat_ref[...] = wa_ref[...].astype(BF16).T

    @pl.when((j >= N_MAIN) & (j < J_CIF))
    def _():
        wcat_ref[...] = shifted_t()

    @pl.when(j == J_CIF)
    def _():
        row = lax.broadcasted_iota(jnp.int32, (PREP_CHUNK, D_MODEL), 0)
        wcat_ref[...] = jnp.where(row < PREP_SHIFT, wa_ref[...], 0.0).astype(BF16).T

    @pl.when(j >= J_MG)
    def _():
        wmg_ref[...] = shifted_t()


def _weight_prep_call(w_in):
    in_width = w_in.shape[-1]
    assert in_width == COL_MG + 3 * D_MODEL
    assert COL_CI % PREP_CHUNK == 0 and COL_CO % PREP_CHUNK == PREP_SHIFT == COL_MG % PREP_CHUNK
    w_t = jnp.swapaxes(w_in, 1, 2)
    assert in_width % PREP_SHIFT == 0 and PREP_SHIFT == SUBLANES
    last_rows = in_width // PREP_SHIFT - 1
    groups_per_chunk = PREP_CHUNK // PREP_SHIFT

    def src_block(j):
        return jnp.where(j < J_CIF, j, jnp.where(j == J_CIF, N_MAIN, j - 1))

    return pl.pallas_call(
        _weight_prep_kernel,
        grid=(DEPTH, J_MG + N_MG),
        in_specs=[
            pl.BlockSpec((None, PREP_CHUNK, D_MODEL), lambda l, j: (l, src_block(j), 0)),
            pl.BlockSpec((None, PREP_SHIFT, D_MODEL),
                         lambda l, j: (l, jnp.minimum((src_block(j) + 1) * groups_per_chunk,
                                                      last_rows), 0)),
        ],
        out_specs=[
            pl.BlockSpec((None, D_MODEL, PREP_CHUNK), lambda l, j: (l, 0, jnp.minimum(j, J_CIF))),
            pl.BlockSpec((None, D_MODEL, PREP_CHUNK), lambda l, j: (l, 0, jnp.maximum(j - J_MG, 0))),
        ],
        out_shape=[
            jax.ShapeDtypeStruct((DEPTH, D_MODEL, ZCAT_W), BF16),
            jax.ShapeDtypeStruct((DEPTH, D_MODEL, 3 * D_MODEL), BF16),
        ],
        compiler_params=pltpu.CompilerParams(
            dimension_semantics=("arbitrary", "arbitrary"), vmem_limit_bytes=VMEM_LIMIT),
        name="weight_prep",
    )(w_t, w_t)


def _col_chunks(width, step):
    return [(o, min(step, width - o)) for o in range(0, width, step)]


def _inproj_pieces(get_h, w_ref, b_ref, za_ref, zb_ref, zc_ref, step):
    def piece(o_ref, off, woff, w):
        def run():
            o_ref[:, off:off + w] = _dot(get_h(), w_ref[:, woff:woff + w]) + b_ref[:, woff:woff + w]
        return run
    pieces = []
    base = 0
    for o_ref, width in ((za_ref, ZA_W), (zb_ref, ZB_W), (zc_ref, ZC_W)):
        pieces += [piece(o_ref, off, base + off, w) for off, w in _col_chunks(width, step)]
        base += width
    return pieces


def _inproj_kernel(x_ref, mod_ref, ng_ref, w_ref, b_ref, za_ref, zb_ref, zc_ref):
    h = _modulated_norm(x_ref[...], mod_ref, ng_ref)
    for piece in _inproj_pieces(lambda: h, w_ref, b_ref, za_ref, zb_ref, zc_ref, 512):
        piece()


def _mod_spec(tm, tokens_per_batch):
    if tokens_per_batch is None:
        return pl.BlockSpec((tm, 3 * D_MODEL), lambda i: (i, 0))
    tiles_per_batch = tokens_per_batch // tm
    return pl.BlockSpec((None, 1, 3 * D_MODEL), lambda i: (i // tiles_per_batch, 0, 0))


def _layer_weight_spec(layer, rows, cols):
    return pl.BlockSpec((None, rows, cols), lambda i: (layer, 0, 0), pipeline_mode=pl.Buffered(1))


def _inproj_call(layer, x2, mod, ng, wcat, bcat, tokens_per_batch):
    ntok = x2.shape[0]
    tm = PROJ_TILE
    const = lambda i: (0, 0)
    return pl.pallas_call(
        _inproj_kernel,
        grid=(ntok // tm,),
        in_specs=[
            pl.BlockSpec((tm, D_MODEL), lambda i: (i, 0)),
            _mod_spec(tm, tokens_per_batch),
            pl.BlockSpec((1, D_MODEL), const),
            _layer_weight_spec(layer, D_MODEL, ZCAT_W),
            pl.BlockSpec((1, ZCAT_W), const),
        ],
        out_specs=[
            pl.BlockSpec((tm, ZA_W), lambda i: (i, 0)),
            pl.BlockSpec((tm, ZB_W), lambda i: (i, 0)),
            pl.BlockSpec((tm, ZC_W), lambda i: (i, 0)),
        ],
        out_shape=[
            jax.ShapeDtypeStruct((ntok, ZA_W), F32),
            jax.ShapeDtypeStruct((ntok, ZB_W), F32),
            jax.ShapeDtypeStruct((ntok, ZC_W), F32),
        ],
        compiler_params=pltpu.CompilerParams(
            dimension_semantics=("arbitrary",), vmem_limit_bytes=VMEM_LIMIT),
        name="in_projection",
    )(x2, mod, ng, wcat, bcat)


def _outproj_kernel(x_ref, mod_ref, ng_ref, y_ref, wmg_ref, bmg_ref, wa_ref, wb_ref, wc_ref,
                    wo_ref, o_ref):
    x = x_ref[...]
    h = _modulated_norm(x, mod_ref, ng_ref)
    merged = None
    for i, wbr_ref in enumerate((wa_ref, wb_ref, wc_ref)):
        cols = slice(i * D_MODEL, (i + 1) * D_MODEL)
        gate = _sigmoid(_dot(h, wmg_ref[:, cols]) + bmg_ref[:, cols])
        term = gate * _dot(y_ref[:, i * A_WIDTH:(i + 1) * A_WIDTH], wbr_ref[...])
        merged = term if merged is None else merged + term
    ada_gate = mod_ref[:, 2 * D_MODEL:3 * D_MODEL]
    o_ref[...] = x + ada_gate * _dot(merged.astype(BF16), wo_ref[...])


def _outproj_call(layer, x2, mod, ng, y, wmg, bmg, wa, wb, wc, wo, tokens_per_batch):
    ntok = x2.shape[0]
    tm = PROJ_TILE
    const = lambda i: (0, 0)
    once = pl.Buffered(1)
    return pl.pallas_call(
        _outproj_kernel,
        grid=(ntok // tm,),
        in_specs=[
            pl.BlockSpec((tm, D_MODEL), lambda i: (i, 0)),
            _mod_spec(tm, tokens_per_batch),
            pl.BlockSpec((1, D_MODEL), const),
            pl.BlockSpec((tm, Y_W), lambda i: (i, 0)),
            _layer_weight_spec(layer, D_MODEL, 3 * D_MODEL),
            pl.BlockSpec((1, 3 * D_MODEL), const),
            pl.BlockSpec((A_WIDTH, D_MODEL), const, pipeline_mode=once),
            pl.BlockSpec((B_WIDTH, D_MODEL), const, pipeline_mode=once),
            pl.BlockSpec((C_WIDTH, D_MODEL), const, pipeline_mode=once),
            pl.BlockSpec((D_MODEL, D_MODEL), const, pipeline_mode=once),
        ],
        out_specs=pl.BlockSpec((tm, D_MODEL), lambda i: (i, 0)),
        out_shape=jax.ShapeDtypeStruct((ntok, D_MODEL), F32),
        compiler_params=pltpu.CompilerParams(
            dimension_semantics=("arbitrary",), vmem_limit_bytes=VMEM_LIMIT),
        name="out_projection",
    )(x2, mod, ng, y, wmg, bmg, wa, wb, wc, wo)


def _place_q_head(qn, h, rows):
    lane = lax.broadcasted_iota(jnp.int32, (rows, LANES), 1)
    slab = qn[:, (h // 2) * LANES:(h // 2 + 1) * LANES]
    src_hi = h % 2
    dst_hi = h // (B_HEADS // B_KV_HEADS)
    keep = (lane >= B_HEAD_DIM) if src_hi else (lane < B_HEAD_DIM)
    slab = jnp.where(keep, slab, 0.0)
    if src_hi != dst_hi:
        slab = pltpu.roll(slab, B_HEAD_DIM, 1)
    return slab


def _merge_head_pair(o_even, o_odd, h_even, rows):
    lane_lo = lax.broadcasted_iota(jnp.int32, (rows, LANES), 1) < B_HEAD_DIM
    kv_hi = h_even // (B_HEADS // B_KV_HEADS)
    if kv_hi:
        o_even = pltpu.roll(o_even, B_HEAD_DIM, 1)
    else:
        o_odd = pltpu.roll(o_odd, B_HEAD_DIM, 1)
    return jnp.where(lane_lo, o_even, o_odd)


def _conv_taps(xbuf, cw_ref, cb_ref, cols, ts):
    y = cb_ref[:, cols]
    for j in range(C_CONV):
        lo = SUBLANES - (C_CONV - 1) + j
        y = y + cw_ref[j:j + 1, cols] * xbuf[lo:lo + ts, cols]
    return y


def _prompt_mix_kernel(sink_ref, za_ref, zb_ref, zc_ref, vg_ref, gw_ref, gbs_ref, qg_ref, kg_ref,
                       cw_ref, cb_ref, fb_ref, hg_ref, tril_ref, band_ref, tri01_ref, tribias_ref,
                       y_ref, ko_ref, vo_ref, convo_ref, c_ref, n_ref, m_ref,
                       kprev, vprev, xbuf, first_tile, pump):
    ts = PROMPT_TILE

    wts = [(gw_ref[gi] * tril_ref[...]).astype(BF16) for gi in range(A_GROUPS)]
    for c in range(ts // WINDOW):
        rows = slice(c * WINDOW, (c + 1) * WINDOW)
        vnb = (_rms(za_ref[rows, A_WIDTH:2 * A_WIDTH]) * vg_ref[...]).astype(BF16)
        s = jnp.concatenate(
            [_dot(wts[gi], vnb[:, gi * GROUP_DIM:(gi + 1) * GROUP_DIM]) + gbs_ref[:, gi:gi + 1]
             for gi in range(A_GROUPS)], axis=1)
        sg = _silu(za_ref[rows, 2 * A_WIDTH:3 * A_WIDTH])
        y_ref[rows, 0:A_WIDTH] = (za_ref[rows, 0:A_WIDTH] * s * sg).astype(BF16)
        pump()

    kn = _qk_norm(zb_ref[:, B_WIDTH:B_WIDTH + B_KV_WIDTH], kg_ref[...])
    vv = zb_ref[:, B_WIDTH + B_KV_WIDTH:B_WIDTH + 2 * B_KV_WIDTH]
    pump()
    grp = B_HEADS // B_KV_HEADS
    nblk = ts // WINDOW
    lane_lo2 = lax.broadcasted_iota(jnp.int32, (2 * WINDOW, LANES), 1) < B_HEAD_DIM
    kblocks = [kprev[...]] + [kn[b * WINDOW:(b + 1) * WINDOW] for b in range(nblk)]
    vblocks = [vprev[...]] + [vv[b * WINDOW:(b + 1) * WINDOW] for b in range(nblk)]
    heads = [(kh, g) for kh in range(B_KV_HEADS) for g in range(grp)]
    snk = {k: sink_ref[k[0] * grp + k[1]] for k in heads}
    for blk in range(nblk):
        rows = slice(blk * WINDOW, (blk + 1) * WINDOW)
        if blk == 0 and first_tile is not False:
            bias = jnp.where(first_tile, band_ref[1], band_ref[0])
        else:
            bias = band_ref[0]
        kcat = jnp.concatenate([kblocks[blk], kblocks[blk + 1]], axis=0)
        vcat = jnp.concatenate([vblocks[blk], vblocks[blk + 1]], axis=0)
        krol = pltpu.roll(kcat, B_HEAD_DIM, 1)
        vrol = pltpu.roll(vcat, B_HEAD_DIM, 1)
        kdup, vdup = [], []
        for kh in range(B_KV_HEADS):
            own = lane_lo2 if kh == 0 else jnp.logical_not(lane_lo2)
            kdup.append(jnp.where(own, kcat, krol).astype(BF16))
            vdup.append(jnp.where(own, vcat, vrol).astype(BF16))
        qn = _qk_norm(zb_ref[rows, 0:B_WIDTH], qg_ref[...]) * (B_HEAD_DIM ** -0.5)
        pump()
        qs = [jnp.concatenate([_place_q_head(qn, kh * grp + g, WINDOW) for g in range(grp)],
                              axis=0).astype(BF16) for kh in range(B_KV_HEADS)]
        logits = [_dot_nt(qs[kh], kdup[kh]) for kh in range(B_KV_HEADS)]
        pump()
        lg = {(kh, g): logits[kh][g * WINDOW:(g + 1) * WINDOW] + bias for kh, g in heads}
        mx = {k: jnp.maximum(jnp.max(lg[k], axis=-1, keepdims=True), snk[k]) for k in heads}
        pump()
        p = {k: jnp.exp(lg[k] - mx[k]) for k in heads}
        pump()
        rden = {k: 1.0 / (jnp.sum(p[k], axis=-1, keepdims=True) + jnp.exp(snk[k] - mx[k]))
                for k in heads}
        pump()
        pv = [_dot(jnp.concatenate([p[kh, g].astype(BF16) for g in range(grp)], axis=0), vdup[kh])
              for kh in range(B_KV_HEADS)]
        pump()
        outs = {(kh, g): pv[kh][g * WINDOW:(g + 1) * WINDOW] * rden[kh, g] for kh, g in heads}
        yb = jnp.concatenate(
            [_merge_head_pair(outs[(2 * j) // grp, (2 * j) % grp],
                              outs[(2 * j + 1) // grp, (2 * j + 1) % grp], 2 * j, WINDOW)
             for j in range(B_HEADS // 2)], axis=1)
        sgb = _silu(zb_ref[rows, B_WIDTH + 2 * B_KV_WIDTH:ZB_W])
        y_ref[rows, A_WIDTH:A_WIDTH + B_WIDTH] = (yb * sgb).astype(BF16)
        pump()
    kprev[...] = kblocks[nblk]
    vprev[...] = vblocks[nblk]
    ko_ref[...] = kblocks[nblk]
    vo_ref[...] = vblocks[nblk]
    pump()

    xbuf[SUBLANES:SUBLANES + ts, :] = zc_ref[:, 0:2 * C_WIDTH]
    pump()
    ifp = zc_ref[:, 5 * C_WIDTH:5 * C_WIDTH + LANES]
    lf = _log_sigmoid(ifp + fb_ref[...])
    pump()
    cl = MLSTM_CHUNK
    lane_c = lax.broadcasted_iota(jnp.int32, (cl, LANES), 1)
    lane_1 = lax.broadcasted_iota(jnp.int32, (1, LANES), 1)
    m_row = m_ref[...]
    m_out = m_row
    cum_all = _dot_exact01(tri01_ref[...], lf)
    st_col = jnp.where(lane_c < C_HEADS, ifp, cum_all)
    st_row = st_col.T
    tribias = tribias_ref[...]
    pump()
    for hds in MLSTM_HEAD_GROUPS:
        hs = {hd: slice(hd * C_HEAD_DIM, (hd + 1) * C_HEAD_DIM) for hd in hds}
        i_c = {hd: st_col[:, hd:hd + 1] for hd in hds}
        cum_c = {hd: st_col[:, C_HEADS + hd:C_HEADS + hd + 1] for hd in hds}
        i_r = {hd: st_row[hd:hd + 1, :] for hd in hds}
        cum_r = {hd: st_row[C_HEADS + hd:C_HEADS + hd + 1, :] for hd in hds}
        m_prev = {hd: m_row[:, hd:hd + 1] for hd in hds}
        dmat = {hd: cum_c[hd] - cum_r[hd] + i_r[hd] + tribias for hd in hds}
        m_inter = {hd: cum_c[hd] + m_prev[hd] for hd in hds}
        m_t = {hd: jnp.maximum(m_inter[hd], jnp.max(dmat[hd], axis=-1, keepdims=True)) for hd in hds}
        pump()
        q_h = {hd: _silu(_conv_taps(xbuf, cw_ref, cb_ref, hs[hd], ts)).astype(BF16) for hd in hds}
        k_h = {hd: _silu(_conv_taps(xbuf, cw_ref, cb_ref,
                                    slice(C_WIDTH + hs[hd].start, C_WIDTH + hs[hd].stop), ts))
               * (C_HEAD_DIM ** -0.5) for hd in hds}
        pump()
        v_h = {hd: zc_ref[:, 2 * C_WIDTH + hd * C_HEAD_DIM:2 * C_WIDTH + (hd + 1) * C_HEAD_DIM].astype(BF16)
               for hd in hds}
        s_qk = {hd: _dot_nt(q_h[hd], k_h[hd].astype(BF16)) for hd in hds}
        a = {hd: jnp.exp(dmat[hd] - m_t[hd]) * s_qk[hd] for hd in hds}
        pump()
        w_inter = {hd: jnp.exp(m_inter[hd] - m_t[hd]) for hd in hds}
        c_prev = {hd: c_ref[hd] for hd in hds}
        n_prev = {hd: n_ref[hd:hd + 1, :] for hd in hds}
        inter = {hd: _dot(q_h[hd], c_prev[hd].astype(BF16)) for hd in hds}
        intra = {hd: _dot(a[hd].astype(BF16), v_h[hd]) for hd in hds}
        pump()
        den = {hd: jnp.sum(a[hd], axis=-1, keepdims=True)
               + w_inter[hd] * jnp.sum(q_h[hd].astype(F32) * n_prev[hd], axis=-1, keepdims=True)
               for hd in hds}
        rnorm = {hd: 1.0 / jnp.maximum(jnp.abs(den[hd]), jnp.exp(-m_t[hd])) for hd in hds}
        hh = {hd: (intra[hd] + w_inter[hd] * inter[hd]) * rnorm[hd] for hd in hds}
        pump()
        for hd in hds:
            o_cols = slice(3 * C_WIDTH + hd * C_HEAD_DIM, 3 * C_WIDTH + (hd + 1) * C_HEAD_DIM)
            g_cols = slice(4 * C_WIDTH + hd * C_HEAD_DIM, 4 * C_WIDTH + (hd + 1) * C_HEAD_DIM)
            gate_o = _sigmoid(zc_ref[:, o_cols]) * _silu(zc_ref[:, g_cols])
            y_cols = slice(A_WIDTH + B_WIDTH + hd * C_HEAD_DIM, A_WIDTH + B_WIDTH + (hd + 1) * C_HEAD_DIM)
            y_ref[:, y_cols] = (_rms(hh[hd]) * hg_ref[:, hs[hd]] * gate_o).astype(BF16)
        pump()
        total = {hd: cum_r[hd][:, cl - 1:cl] for hd in hds}
        g_r = {hd: total[hd] - cum_r[hd] + i_r[hd] for hd in hds}
        g_c = {hd: total[hd] - cum_c[hd] + i_c[hd] for hd in hds}
        m_new = {hd: jnp.maximum(total[hd] + m_prev[hd], jnp.max(g_r[hd], axis=-1, keepdims=True))
                 for hd in hds}
        kw = {hd: jnp.exp(g_c[hd] - m_new[hd]) * k_h[hd] for hd in hds}
        decay = {hd: jnp.exp(total[hd] + m_prev[hd] - m_new[hd]) for hd in hds}
        pump()
        upd = {hd: _dot(kw[hd].T.astype(BF16), v_h[hd]) for hd in hds}
        for hd in hds:
            c_ref[hd] = decay[hd] * c_prev[hd] + upd[hd]
            n_ref[hd:hd + 1, :] = decay[hd] * n_prev[hd] + jnp.sum(kw[hd], axis=0, keepdims=True)
            m_out = jnp.where(lane_1 == hd, m_new[hd], m_out)
        pump()
    m_ref[...] = m_out
    tail = xbuf[ts:ts + SUBLANES, :]
    xbuf[0:SUBLANES, :] = tail
    convo_ref[...] = tail


def _prompt_mask_constants():
    r = np.arange(WINDOW)[:, None]
    c = np.arange(2 * WINDOW)[None, :]
    band = (c > r) & (c <= r + WINDOW)
    band_first = band & (c >= WINDOW)
    band_bias = np.where(np.stack([band, band_first]), 0.0, NEG).astype(np.float32)
    tril = (np.arange(WINDOW)[:, None] >= np.arange(WINDOW)[None, :]).astype(np.float32)
    tri = np.arange(MLSTM_CHUNK)[:, None] >= np.arange(MLSTM_CHUNK)[None, :]
    return (jnp.asarray(tril), jnp.asarray(band_bias), jnp.asarray(tri, dtype=BF16),
            jnp.asarray(np.where(tri, 0.0, NEG).astype(np.float32)))


N_MIX_PARAMS = 13
MLSTM_HEAD_GROUPS = ((0, 1), (2, 3))
MIX_PUMP_CALLS = 37
TAIL_FILL_PIECES = 16
MXU_PIECE_COLS = 256


class _Interleaver:
    def __init__(self, pieces, calls, hold_back=0):
        self._pieces = list(pieces)
        self._hold_back = hold_back
        self._spread = len(self._pieces) - hold_back
        self._emitted = 0
        self._calls = calls
        self._call = 0

    def __call__(self):
        self._call += 1
        target = (self._call * self._spread) // self._calls
        while self._emitted < target:
            self._pieces.pop(0)()
            self._emitted += 1

    def finish(self):
        assert self._call == self._calls and len(self._pieces) == self._hold_back, self._call
        return self._pieces


def _gate_pieces(h_ref, wmg_ref, bmg_ref, g_ref):
    def piece(off):
        cols = slice(off, off + MXU_PIECE_COLS)
        def run():
            g_ref[:, cols] = _sigmoid(_dot(h_ref[...], wmg_ref[:, cols]) + bmg_ref[:, cols])
        return run
    return [piece(off) for off in range(0, 3 * D_MODEL, MXU_PIECE_COLS)]


def _merge_and_project(x, mod_ref, g_ref, y_ref, wa_ref, wb_ref, wc_ref, wo_ref, fillers=()):
    fillers = list(fillers)
    per_stage = -(-len(fillers) // 4)
    merged = None
    for i, wbr_ref in enumerate((wa_ref, wb_ref, wc_ref)):
        for piece in fillers[i * per_stage:(i + 1) * per_stage]:
            piece()
        term = (g_ref[:, i * D_MODEL:(i + 1) * D_MODEL]
                * _dot(y_ref[:, i * A_WIDTH:(i + 1) * A_WIDTH], wbr_ref[...]))
        merged = term if merged is None else merged + term
    for piece in fillers[3 * per_stage:]:
        piece()
    ada_gate = mod_ref[:, 2 * D_MODEL:3 * D_MODEL]
    return x + ada_gate * _dot(merged.astype(BF16), wo_ref[...])


def _prompt_layer_kernel(tiles_per_seq, sink_ref, x2_ref, xn_ref, mod_ref, modn_ref, ng_ref,
                         wcat_ref, bcat_ref, *rest):
    mix_params = rest[:N_MIX_PARAMS]
    wmg_ref, bmg_ref, wa_ref, wb_ref, wc_ref, wo_ref = rest[N_MIX_PARAMS:N_MIX_PARAMS + 6]
    o_ref, ko_ref, vo_ref, convo_ref, c_ref, n_ref, m_ref = rest[N_MIX_PARAMS + 6:N_MIX_PARAMS + 13]
    (za0, zb0, zc0, za1, zb1, zc1, h0, h1, y_scr, g_scr, kprev, vprev, xbuf) = rest[N_MIX_PARAMS + 13:]
    ts = PROMPT_TILE
    z = ((za0, zb0, zc0), (za1, zb1, zc1))
    h = (h0, h1)
    k = pl.program_id(0)
    seq_start = (k % (tiles_per_seq // 2)) == 0

    @pl.when(k == 0)
    def _():
        h0[...] = _modulated_norm(x2_ref[0:ts, :], mod_ref, ng_ref)
        for piece in _inproj_pieces(lambda: h0[...], wcat_ref, bcat_ref, *z[0], 512):
            piece()

    @pl.when(seq_start)
    def _():
        kprev[...] = jnp.zeros_like(kprev)
        vprev[...] = jnp.zeros_like(vprev)
        xbuf[0:SUBLANES, :] = jnp.zeros((SUBLANES, 2 * C_WIDTH), F32)
        c_ref[...] = jnp.zeros_like(c_ref)
        n_ref[...] = jnp.zeros_like(n_ref)
        m_ref[...] = jnp.zeros_like(m_ref)

    for half in range(2):
        cur, nxt = half, 1 - half
        rows = slice(half * ts, (half + 1) * ts)
        if half == 0:
            h[nxt][...] = _modulated_norm(x2_ref[ts:2 * ts, :], mod_ref, ng_ref)
        else:
            h[nxt][...] = _modulated_norm(xn_ref[...], modn_ref, ng_ref)
        get_h_next = functools.partial(lambda r: r[...], h[nxt])
        hold = TAIL_FILL_PIECES if half == 1 else 0
        proj = _inproj_pieces(get_h_next, wcat_ref, bcat_ref, *z[nxt], MXU_PIECE_COLS)
        pump = _Interleaver(
            proj[:len(proj) - hold] + _gate_pieces(h[cur], wmg_ref, bmg_ref, g_scr)
            + proj[len(proj) - hold:], MIX_PUMP_CALLS, hold_back=hold)
        _prompt_mix_kernel(sink_ref, *z[cur], *mix_params,
                           y_scr, ko_ref, vo_ref, convo_ref, c_ref, n_ref, m_ref, kprev, vprev, xbuf,
                           first_tile=seq_start if half == 0 else False, pump=pump)
        o_ref[rows, :] = _merge_and_project(x2_ref[rows, :], mod_ref, g_scr, y_scr,
                                            wa_ref, wb_ref, wc_ref, wo_ref, fillers=pump.finish())


def _prompt_layer_call(layer, x2, mod, lw, batch, seq):
    ts = PROMPT_TILE
    nt = seq // ts
    assert nt % 2 == 0
    last_tile = batch * nt - 1
    const2 = lambda k: (0, 0)
    const3 = lambda k: (0, 0, 0)
    per_b3 = lambda k: ((2 * k) // nt, 0, 0)
    next_tile = lambda k: jnp.minimum(2 * k + 2, last_tile)
    once = pl.Buffered(1)
    return pl.pallas_call(
        functools.partial(_prompt_layer_kernel, nt),
        grid=(batch * nt // 2,),
        in_specs=[
            pl.BlockSpec(memory_space=pltpu.SMEM),
            pl.BlockSpec((2 * ts, D_MODEL), lambda k: (k, 0)),
            pl.BlockSpec((ts, D_MODEL), lambda k: (next_tile(k), 0)),
            pl.BlockSpec((None, 1, 3 * D_MODEL), per_b3),
            pl.BlockSpec((None, 1, 3 * D_MODEL), lambda k: (next_tile(k) // nt, 0, 0)),
            pl.BlockSpec((1, D_MODEL), const2),
            _layer_weight_spec(layer, D_MODEL, ZCAT_W),
            pl.BlockSpec((1, ZCAT_W), const2),
            pl.BlockSpec((1, A_WIDTH), const2),
            pl.BlockSpec((A_GROUPS, WINDOW, WINDOW), const3),
            pl.BlockSpec((WINDOW, LANES), const2),
            pl.BlockSpec((1, B_WIDTH), const2),
            pl.BlockSpec((1, B_KV_WIDTH), const2),
            pl.BlockSpec((C_CONV, 2 * C_WIDTH), const2),
            pl.BlockSpec((1, 2 * C_WIDTH), const2),
            pl.BlockSpec((1, LANES), const2),
            pl.BlockSpec((1, C_WIDTH), const2),
            pl.BlockSpec((WINDOW, WINDOW), const2),
            pl.BlockSpec((2, WINDOW, 2 * WINDOW), const3),
            pl.BlockSpec((MLSTM_CHUNK, MLSTM_CHUNK), const2),
            pl.BlockSpec((MLSTM_CHUNK, MLSTM_CHUNK), const2),
            _layer_weight_spec(layer, D_MODEL, 3 * D_MODEL),
            pl.BlockSpec((1, 3 * D_MODEL), const2),
            pl.BlockSpec((A_WIDTH, D_MODEL), const2, pipeline_mode=once),
            pl.BlockSpec((B_WIDTH, D_MODEL), const2, pipeline_mode=once),
            pl.BlockSpec((C_WIDTH, D_MODEL), const2, pipeline_mode=once),
            pl.BlockSpec((D_MODEL, D_MODEL), const2, pipeline_mode=once),
        ],
        out_specs=[
            pl.BlockSpec((2 * ts, D_MODEL), lambda k: (k, 0)),
            pl.BlockSpec((None, WINDOW, B_KV_WIDTH), per_b3),
            pl.BlockSpec((None, WINDOW, B_KV_WIDTH), per_b3),
            pl.BlockSpec((None, SUBLANES, 2 * C_WIDTH), per_b3),
            pl.BlockSpec((None, C_HEADS, C_HEAD_DIM, C_HEAD_DIM), lambda k: ((2 * k) // nt, 0, 0, 0)),
            pl.BlockSpec((None, C_HEADS, C_HEAD_DIM), per_b3),
            pl.BlockSpec((None, 1, LANES), per_b3),
        ],
        out_shape=[
            jax.ShapeDtypeStruct((batch * seq, D_MODEL), F32),
            jax.ShapeDtypeStruct((batch, WINDOW, B_KV_WIDTH), F32),
            jax.ShapeDtypeStruct((batch, WINDOW, B_KV_WIDTH), F32),
            jax.ShapeDtypeStruct((batch, SUBLANES, 2 * C_WIDTH), F32),
            jax.ShapeDtypeStruct((batch, C_HEADS, C_HEAD_DIM, C_HEAD_DIM), F32),
            jax.ShapeDtypeStruct((batch, C_HEADS, C_HEAD_DIM), F32),
            jax.ShapeDtypeStruct((batch, 1, LANES), F32),
        ],
        scratch_shapes=(
            [pltpu.VMEM((ts, w), F32) for w in (ZA_W, ZB_W, ZC_W)] * 2
            + [pltpu.VMEM((ts, D_MODEL), BF16)] * 2
            + [pltpu.VMEM((ts, Y_W), BF16),
               pltpu.VMEM((ts, 3 * D_MODEL), F32),
               pltpu.VMEM((WINDOW, B_KV_WIDTH), F32),
               pltpu.VMEM((WINDOW, B_KV_WIDTH), F32),
               pltpu.VMEM((ts + SUBLANES, 2 * C_WIDTH), F32)]),
        compiler_params=pltpu.CompilerParams(
            dimension_semantics=("arbitrary",), vmem_limit_bytes=VMEM_LIMIT),
        name="prompt_layer",
    )(lw["sinks"], x2, x2, mod, mod, lw["ng"], lw["wcat"], lw["bcat"],
      lw["vg"], lw["gws"], lw["gbs_col"], lw["qg"], lw["kg"], lw["cw"], lw["cb"], lw["fb"], lw["hg"],
      *_prompt_mask_constants(),
      lw["wmg"], lw["bmg"], lw["wa"], lw["wb"], lw["wc"], lw["wo"])


def _sample_mix_kernel(sink_ref, za_ref, zb_ref, zc_ref, kc_ref, vc_ref, cs_ref, c0_ref, n0_ref,
                       m0_ref, vg_ref, gwb_ref, gbs_ref, qg_ref, kg_ref, cw_ref, cb_ref, fb_ref,
                       hg_ref,
                       y_ref, vrow_ref, ko_ref, vo_ref, convo_ref, c1_ref, n1_ref, m1_ref,
                       xbuf, first_layer=None):
    nb = SAMPLE_NB
    t = SUBLANES
    rows = nb * t
    if first_layer is not None:
        for other in range(DEPTH):
            if other != first_layer:
                c1_ref[other] = jnp.zeros(c1_ref.shape[1:], F32)
        c1_ref = c1_ref.at[first_layer]
    tok_r = lax.broadcasted_iota(jnp.int32, (rows, rows), 0)
    tok_c = lax.broadcasted_iota(jnp.int32, (rows, rows), 1)
    same_b = (tok_r // t) == (tok_c // t)
    causal_b = same_b & (tok_c <= tok_r)

    u = za_ref[:, 0:A_WIDTH]
    vn = _rms(za_ref[:, A_WIDTH:2 * A_WIDTH]) * vg_ref[...]
    sg = _silu(za_ref[:, 2 * A_WIDTH:3 * A_WIDTH])
    vrow_ref[...] = vn
    vnb = vn.astype(BF16)
    s_cols = []
    for gi in range(A_GROUPS):
        s_cols.append(_dot(gwb_ref[gi], vnb[:, gi * GROUP_DIM:(gi + 1) * GROUP_DIM])
                      + gbs_ref[:, gi:gi + 1])
    y_ref[:, 0:A_WIDTH] = (u * jnp.concatenate(s_cols, axis=1) * sg).astype(BF16)

    qn = _qk_norm(zb_ref[:, 0:B_WIDTH], qg_ref[...]) * (B_HEAD_DIM ** -0.5)
    kn = _qk_norm(zb_ref[:, B_WIDTH:B_WIDTH + B_KV_WIDTH], kg_ref[...])
    vv = zb_ref[:, B_WIDTH + B_KV_WIDTH:B_WIDTH + 2 * B_KV_WIDTH]
    sgb = _silu(zb_ref[:, B_WIDTH + 2 * B_KV_WIDTH:ZB_W])
    kn3 = kn.reshape(nb, t, B_KV_WIDTH)
    vv3 = vv.reshape(nb, t, B_KV_WIDTH)
    kcache = kc_ref[...]
    vcache = vc_ref[...]
    pad = jnp.zeros((nb, WINDOW - t, B_KV_WIDTH), F32)
    kall = jnp.concatenate([kcache, kn3, pad], axis=1).astype(BF16)
    vall = jnp.concatenate([vcache, vv3, pad], axis=1).astype(BF16)
    qp = jnp.concatenate([_place_q_head(qn, h, rows).reshape(nb, t, LANES) for h in range(B_HEADS)],
                         axis=1).astype(BF16)
    logits = lax.dot_general(qp, kall, (((2,), (2,)), ((0,), (0,))), preferred_element_type=F32)
    qrow = lax.broadcasted_iota(jnp.int32, (nb, B_HEADS * t, 2 * WINDOW), 1)
    kcol = lax.broadcasted_iota(jnp.int32, (nb, B_HEADS * t, 2 * WINDOW), 2)
    qt = qrow % t
    valid = ((kcol < WINDOW) & (kcol > qt)) | ((kcol >= WINDOW) & ((kcol - WINDOW) <= qt))
    hrow = lax.broadcasted_iota(jnp.int32, (B_HEADS * t, 1), 0) // t
    snk = jnp.zeros((B_HEADS * t, 1), F32)
    for h in range(B_HEADS):
        snk = jnp.where(hrow == h, sink_ref[h], snk)
    lg = jnp.where(valid, logits, NEG)
    mx = jnp.maximum(jnp.max(lg, axis=-1, keepdims=True), snk[None])
    p = jnp.exp(lg - mx)
    den = jnp.sum(p, axis=-1, keepdims=True) + jnp.exp(snk[None] - mx)
    pv = lax.dot_general(p.astype(BF16), vall, (((2,), (1,)), ((0,), (0,))),
                         preferred_element_type=F32) / den
    head_out = [pv[:, h * t:(h + 1) * t, :].reshape(rows, LANES) for h in range(B_HEADS)]
    yb = jnp.concatenate(
        [_merge_head_pair(head_out[2 * j], head_out[2 * j + 1], 2 * j, rows)
         for j in range(B_HEADS // 2)], axis=1)
    y_ref[:, A_WIDTH:A_WIDTH + B_WIDTH] = (yb * sgb).astype(BF16)
    ko_ref[...] = jnp.concatenate([kcache[:, t:, :], kn3], axis=1)
    vo_ref[...] = jnp.concatenate([vcache[:, t:, :], vv3], axis=1)

    xbuf[:, SUBLANES - (C_CONV - 1):SUBLANES, :] = cs_ref[...]
    xbuf[:, SUBLANES:2 * SUBLANES, :] = zc_ref[:, 0:2 * C_WIDTH].reshape(nb, t, 2 * C_WIDTH)
    y3 = cb_ref[...][None]
    for j in range(C_CONV):
        lo = SUBLANES - (C_CONV - 1) + j
        y3 = y3 + cw_ref[j:j + 1, :][None] * xbuf[:, lo:lo + t, :]
    convo_ref[...] = xbuf[:, 2 * SUBLANES - (C_CONV - 1):2 * SUBLANES, :]
    qk = _silu(y3.reshape(rows, 2 * C_WIDTH))
    qall = qk[:, 0:C_WIDTH].astype(BF16)
    kall_c = qk[:, C_WIDTH:2 * C_WIDTH] * (C_HEAD_DIM ** -0.5)
    vall_c = zc_ref[:, 2 * C_WIDTH:3 * C_WIDTH].astype(BF16)
    gate_o = _sigmoid(zc_ref[:, 3 * C_WIDTH:4 * C_WIDTH]) * _silu(zc_ref[:, 4 * C_WIDTH:5 * C_WIDTH])
    ifp = zc_ref[:, 5 * C_WIDTH:5 * C_WIDTH + LANES]
    lf = _log_sigmoid(ifp + fb_ref[...])
    lane_t = lax.broadcasted_iota(jnp.int32, (rows, LANES), 1)
    cum_all = _dot_exact01(jnp.where(causal_b, 1.0, 0.0).astype(BF16), lf)
    tot_all = _dot_exact01(jnp.where(same_b, 1.0, 0.0).astype(BF16), lf)
    st_col = jnp.where(lane_t < C_HEADS, ifp, cum_all)
    st_row = st_col.T
    tot_row = tot_all.T
    m0 = m0_ref[...]
    same_b_bf = jnp.where(same_b, 1.0, 0.0).astype(BF16)
    batch_of_lane = lax.broadcasted_iota(jnp.int32, (nb, 1, rows), 2) // t
    batch_id = lax.broadcasted_iota(jnp.int32, (nb, 1, rows), 0)
    own_tok = batch_of_lane == batch_id
    h_cols = []
    m_out = jnp.zeros((rows, LANES), F32)
    for hd in range(C_HEADS):
        hs = slice(hd * C_HEAD_DIM, (hd + 1) * C_HEAD_DIM)
        i_c = st_col[:, hd:hd + 1]
        cum_c = st_col[:, C_HEADS + hd:C_HEADS + hd + 1]
        tot_c = tot_all[:, C_HEADS + hd:C_HEADS + hd + 1]
        i_r = st_row[hd:hd + 1, :]
        cum_r = st_row[C_HEADS + hd:C_HEADS + hd + 1, :]
        tot_r = tot_row[C_HEADS + hd:C_HEADS + hd + 1, :]
        m_prev = m0[:, hd:hd + 1]
        dmat = jnp.where(causal_b, cum_c - cum_r + i_r, NEG)
        m_inter = cum_c + m_prev
        m_t = jnp.maximum(m_inter, jnp.max(dmat, axis=-1, keepdims=True))
        q_h = qall[:, hs]
        k_h = kall_c[:, hs]
        v_h = vall_c[:, hs]
        a = jnp.exp(dmat - m_t) * _dot_nt(q_h, k_h.astype(BF16))
        w_inter = jnp.exp(m_inter - m_t)
        c_prev = c0_ref[:, hd]
        n_tok = jnp.broadcast_to(n0_ref[hd][:, None, :], (nb, t, C_HEAD_DIM)).reshape(rows, C_HEAD_DIM)
        inter = lax.dot_general(q_h.reshape(nb, t, C_HEAD_DIM), c_prev.astype(BF16),
                                (((2,), (1,)), ((0,), (0,))), preferred_element_type=F32)
        num = _dot(a.astype(BF16), v_h) + w_inter * inter.reshape(rows, C_HEAD_DIM)
        den = (jnp.sum(a, axis=-1, keepdims=True)
               + w_inter * jnp.sum(q_h.astype(F32) * n_tok, axis=-1, keepdims=True))
        hh = num / jnp.maximum(jnp.abs(den), jnp.exp(-m_t))
        h_cols.append(_rms(hh))
        g_r = tot_r - cum_r + i_r
        g_c = tot_c - cum_c + i_c
        m_new = jnp.maximum(tot_c + m_prev,
                            jnp.max(jnp.where(same_b, g_r, NEG), axis=-1, keepdims=True))
        kw = jnp.exp(g_c - m_new) * k_h
        decay = jnp.exp(tot_c + m_prev - m_new)
        kwt = kw.T
        lhs = jnp.where(own_tok, kwt[None], 0.0).astype(BF16).reshape(nb * C_HEAD_DIM, rows)
        upd = _dot(lhs, v_h).reshape(nb, C_HEAD_DIM, C_HEAD_DIM)
        dec_b = jnp.broadcast_to(decay, (rows, C_HEAD_DIM)).reshape(nb, t, C_HEAD_DIM)[:, 0:1, :]
        c1_ref[:, hd] = dec_b * c_prev + upd
        n1_ref[hd] = decay * n_tok + _dot(same_b_bf, kw.astype(BF16))
        m_out = jnp.where(lane_t == hd, m_new, m_out)
    m1_ref[...] = m_out
    hn = jnp.concatenate(h_cols, axis=1) * hg_ref[...]
    y_ref[:, A_WIDTH + B_WIDTH:Y_W] = (hn * gate_o).astype(BF16)


def _sample_mix_call(l, za, zb, zc, kc, vc, cs, c0, n0t, m0tok, lw, nbatch, c1_all=None):
    nb = SAMPLE_NB
    t = SUBLANES
    rows = nb * t
    tok = lambda i: (i, 0)
    const2 = lambda i: (0, 0)
    const3 = lambda i: (0, 0, 0)
    b3 = lambda i: (i, 0, 0)
    lb4 = lambda i: (l, i, 0, 0)
    operands = [lw["sinks"], za, zb, zc, kc, vc, cs, c0, n0t, m0tok, lw["vg"], lw["gwb"],
                lw["gbs_tok"], lw["qg"], lw["kg"], lw["cw"], lw["cb"], lw["fb"], lw["hg"]]
    c_block = (nb, C_HEADS, C_HEAD_DIM, C_HEAD_DIM)
    if c1_all is None:
        kernel_fn, extra_specs, aliases = functools.partial(_sample_mix_kernel, first_layer=l), [], {}
        c1_spec = pl.BlockSpec((DEPTH,) + c_block, lambda i: (0, i, 0, 0, 0))
    else:
        n_in = len(operands)
        operands.append(c1_all)
        extra_specs = [pl.BlockSpec(memory_space=pl.ANY)]
        aliases = {n_in: 5}
        kernel_fn = lambda *refs: _sample_mix_kernel(*refs[:n_in], *refs[n_in + 1:])
        c1_spec = pl.BlockSpec((None,) + c_block, lambda i: (l, i, 0, 0, 0))
    return pl.pallas_call(
        kernel_fn,
        grid=(nbatch // nb,),
        input_output_aliases=aliases,
        in_specs=[
            pl.BlockSpec(memory_space=pltpu.SMEM),
            pl.BlockSpec((rows, ZA_W), tok),
            pl.BlockSpec((rows, ZB_W), tok),
            pl.BlockSpec((rows, ZC_W), tok),
            pl.BlockSpec((None, nb, WINDOW, B_KV_WIDTH), lb4),
            pl.BlockSpec((None, nb, WINDOW, B_KV_WIDTH), lb4),
            pl.BlockSpec((None, nb, C_CONV - 1, 2 * C_WIDTH), lb4),
            pl.BlockSpec((None, nb, C_HEADS, C_HEAD_DIM, C_HEAD_DIM), lambda i: (l, i, 0, 0, 0)),
            pl.BlockSpec((None, C_HEADS, nb, C_HEAD_DIM), lambda i: (l, 0, i, 0)),
            pl.BlockSpec((None, rows, LANES), lambda i: (l, i, 0)),
            pl.BlockSpec((1, A_WIDTH), const2),
            pl.BlockSpec((A_GROUPS, rows, rows), const3),
            pl.BlockSpec((rows, LANES), const2),
            pl.BlockSpec((1, B_WIDTH), const2),
            pl.BlockSpec((1, B_KV_WIDTH), const2),
            pl.BlockSpec((C_CONV, 2 * C_WIDTH), const2),
            pl.BlockSpec((1, 2 * C_WIDTH), const2),
            pl.BlockSpec((1, LANES), const2),
            pl.BlockSpec((1, C_WIDTH), const2),
        ] + extra_specs,
        out_specs=[
            pl.BlockSpec((rows, Y_W), tok),
            pl.BlockSpec((rows, A_WIDTH), tok),
            pl.BlockSpec((nb, WINDOW, B_KV_WIDTH), b3),
            pl.BlockSpec((nb, WINDOW, B_KV_WIDTH), b3),
            pl.BlockSpec((nb, C_CONV - 1, 2 * C_WIDTH), b3),
            c1_spec,
            pl.BlockSpec((C_HEADS, rows, C_HEAD_DIM), lambda i: (0, i, 0)),
            pl.BlockSpec((rows, LANES), tok),
        ],
        out_shape=[
            jax.ShapeDtypeStruct((nbatch * t, Y_W), BF16),
            jax.ShapeDtypeStruct((nbatch * t, A_WIDTH), F32),
            jax.ShapeDtypeStruct((nbatch, WINDOW, B_KV_WIDTH), F32),
            jax.ShapeDtypeStruct((nbatch, WINDOW, B_KV_WIDTH), F32),
            jax.ShapeDtypeStruct((nbatch, C_CONV - 1, 2 * C_WIDTH), F32),
            jax.ShapeDtypeStruct((DEPTH, nbatch, C_HEADS, C_HEAD_DIM, C_HEAD_DIM), F32),
            jax.ShapeDtypeStruct((C_HEADS, nbatch * t, C_HEAD_DIM), F32),
            jax.ShapeDtypeStruct((nbatch * t, LANES), F32),
        ],
        scratch_shapes=[pltpu.VMEM((nb, 2 * SUBLANES, 2 * C_WIDTH), F32)],
        compiler_params=pltpu.CompilerParams(
            dimension_semantics=("arbitrary",), vmem_limit_bytes=VMEM_LIMIT),
        name="sample_mixer",
    )(*operands)


def _layer_weights(l, wcat_all, wmg_all, b_in, gmlp_vnorm_g, gmlp_ws, gmlp_bs, swa_qnorm_g,
                   swa_knorm_g, swa_sinks, mlstm_conv_w, mlstm_conv_b, mlstm_f_bias, mlstm_hnorm_g,
                   w_branch_a, w_branch_b, w_branch_c, w_out, norm_g, dec_seq):
    bl = b_in[l]
    bcat = jnp.concatenate([bl[:COL_CI], bl[COL_CO:COL_MG], bl[COL_CI:COL_CO],
                            jnp.zeros((LANES - 2 * C_HEADS,), F32)])
    t = dec_seq
    nb = SAMPLE_NB
    ws_t = gmlp_ws[l][:, :t, :t] * jnp.tril(jnp.ones((t, t), F32))
    eye = jnp.eye(nb, dtype=F32)
    gwb = jnp.einsum("bc,gts->gbtcs", eye, ws_t).reshape(A_GROUPS, nb * t, nb * t).astype(BF16)
    gbs_col = jnp.pad(gmlp_bs[l].T, ((0, 0), (0, LANES - A_GROUPS)))
    gbs_tok = jnp.pad(jnp.tile(gmlp_bs[l][:, :t].T, (nb, 1)), ((0, 0), (0, LANES - A_GROUPS)))
    fb = jnp.pad(mlstm_f_bias[l], (C_HEADS, LANES - 2 * C_HEADS)).reshape(1, LANES)
    return dict(
        ng=norm_g[l].reshape(1, D_MODEL),
        wcat=wcat_all, bcat=bcat.reshape(1, ZCAT_W),
        wmg=wmg_all, bmg=bl[COL_MG:].reshape(1, 3 * D_MODEL),
        wa=w_branch_a[l].astype(BF16), wb=w_branch_b[l].astype(BF16),
        wc=w_branch_c[l].astype(BF16), wo=w_out[l].astype(BF16),
        vg=gmlp_vnorm_g[l].reshape(1, A_WIDTH), gws=gmlp_ws[l], gwb=gwb,
        gbs_col=gbs_col, gbs_tok=gbs_tok,
        qg=jnp.tile(swa_qnorm_g[l], B_HEADS).reshape(1, B_WIDTH),
        kg=jnp.tile(swa_knorm_g[l], B_KV_HEADS).reshape(1, B_KV_WIDTH),
        sinks=swa_sinks[l],
        cw=mlstm_conv_w[l], cb=mlstm_conv_b[l].reshape(1, 2 * C_WIDTH), fb=fb,
        hg=mlstm_hnorm_g[l].reshape(1, C_WIDTH),
    )


def kernel(x_prompt, x_sample, cache_swa_k, cache_swa_v, state_mlstm_conv, state_mlstm_C, state_mlstm_n, state_mlstm_m, c_prompt, c_sample, ada_w, ada_b, norm_g, w_in, b_in, gmlp_vnorm_g, gmlp_ws, gmlp_bs, swa_qnorm_g, swa_knorm_g, swa_sinks, mlstm_conv_w, mlstm_conv_b, mlstm_f_bias, mlstm_hnorm_g, w_branch_a, w_branch_b, w_branch_c, w_out):
    batch, seq, _ = x_prompt.shape
    nbatch, dec_seq, _ = x_sample.shape
    assert dec_seq == SUBLANES and seq % PROMPT_TILE == 0 and nbatch % SAMPLE_NB == 0
    assert seq % PROJ_TILE == 0 and (nbatch * dec_seq) % PROJ_TILE == 0
    wb_len = cache_swa_k.shape[2]
    assert wb_len == WINDOW

    nc = batch + nbatch
    nc_pad = -(-nc // SUBLANES) * SUBLANES
    c_all = jnp.concatenate([c_prompt, c_sample, jnp.zeros((nc_pad - nc, D_MODEL), F32)], axis=0)
    mod_all = _ada_call(c_all, ada_w, ada_b)

    xp = x_prompt.reshape(batch * seq, D_MODEL)
    xs = x_sample.reshape(nbatch * dec_seq, D_MODEL)
    kc_all = cache_swa_k.reshape(DEPTH, nbatch, WINDOW, B_KV_WIDTH)
    vc_all = cache_swa_v.reshape(DEPTH, nbatch, WINDOW, B_KV_WIDTH)
    n0t_all = jnp.transpose(state_mlstm_n, (0, 2, 1, 3))
    m0tok_all = jnp.pad(jnp.repeat(state_mlstm_m, dec_seq, axis=1),
                        ((0, 0), (0, 0), (0, LANES - C_HEADS)))
    wcat_all, wmg_all = _weight_prep_call(w_in)
    outs_p = [[] for _ in range(6)]
    outs_s = [[] for _ in range(6)]
    vrows = []
    c1_all = None
    for l in range(DEPTH):
        lw = _layer_weights(l, wcat_all, wmg_all, b_in, gmlp_vnorm_g, gmlp_ws, gmlp_bs, swa_qnorm_g,
                            swa_knorm_g, swa_sinks, mlstm_conv_w, mlstm_conv_b, mlstm_f_bias,
                            mlstm_hnorm_g, w_branch_a, w_branch_b, w_branch_c, w_out, norm_g,
                            dec_seq)
        mod_p = mod_all[l, :batch].reshape(batch, 1, 3 * D_MODEL)
        mod_s = jnp.repeat(mod_all[l, batch:nc], dec_seq, axis=0)

        xp, ko, vo, convo, c1, n1, m1 = _prompt_layer_call(l, xp, mod_p, lw, batch, seq)
        outs_p[0].append(ko.reshape(batch, WINDOW, B_KV_HEADS, B_HEAD_DIM))
        outs_p[1].append(vo.reshape(batch, WINDOW, B_KV_HEADS, B_HEAD_DIM))
        outs_p[2].append(convo[:, SUBLANES - (C_CONV - 1):, :])
        outs_p[3].append(c1)
        outs_p[4].append(n1)
        outs_p[5].append(m1[:, 0, :C_HEADS])

        za, zb, zc = _inproj_call(l, xs, mod_s, lw["ng"], lw["wcat"], lw["bcat"], None)
        y, vrow, ko, vo, convo, c1_all, n1tok, m1tok = _sample_mix_call(
            l, za, zb, zc, kc_all, vc_all, state_mlstm_conv, state_mlstm_C, n0t_all, m0tok_all,
            lw, nbatch, c1_all)
        xs = _outproj_call(l, xs, mod_s, lw["ng"], y, lw["wmg"], lw["bmg"], lw["wa"], lw["wb"],
                           lw["wc"], lw["wo"], None)
        outs_s[0].append(ko.reshape(nbatch, WINDOW, B_KV_HEADS, B_HEAD_DIM))
        outs_s[1].append(vo.reshape(nbatch, WINDOW, B_KV_HEADS, B_HEAD_DIM))
        outs_s[2].append(convo)
        outs_s[4].append(jnp.transpose(n1tok[:, ::dec_seq, :], (1, 0, 2)))
        outs_s[5].append(m1tok[::dec_seq, :C_HEADS])
        vrows.append(vrow.reshape(nbatch, dec_seq, A_WIDTH))

    sp = [jnp.stack(o) for o in outs_p]
    ss = [jnp.stack(o) if o else None for o in outs_s]
    return (xp.reshape(batch, seq, D_MODEL), xs.reshape(nbatch, dec_seq, D_MODEL),
            sp[0], sp[1], sp[2], sp[3], sp[4], sp[5],
            ss[0], ss[1], ss[2], c1_all, ss[4], ss[5], jnp.stack(vrows))
```
